```python
import math
import jax, jax.numpy as jnp
from jax import lax
import numpy as np

D_MODEL = 1024
BATCH = 8
SEQ = 4096
DEPTH = 1

SSD_EXPAND = 2
D_INNER = SSD_EXPAND * D_MODEL
SSD_HEAD_DIM = 64
SSD_HEADS = D_INNER // SSD_HEAD_DIM
SSD_GROUPS = 8
D_STATE = 128
CONV_K = 4
CONV_DIM = D_INNER + 2 * SSD_GROUPS * D_STATE
SSD_CHUNK = 128
SB_HEADS = 16
SB_HEAD_DIM = 64
SB_WIDTH = SB_HEADS * SB_HEAD_DIM
SB_BLOCK = 128
D_FF = 4 * D_MODEL
PLE_DIM = 256
N_BRANCHES = 2
RMS_EPS = 1e-6
IN_PROJ_DIM = D_INNER + CONV_DIM + SSD_HEADS + 3 * SB_WIDTH
SPLITS = [D_INNER, D_INNER + CONV_DIM, D_INNER + CONV_DIM + SSD_HEADS,
          D_INNER + CONV_DIM + SSD_HEADS + SB_WIDTH,
          D_INNER + CONV_DIM + SSD_HEADS + 2 * SB_WIDTH]

kernel_name = "hybrid_ssd_stickbreak_gated_block"


def rms_norm(x, w):
    xf = x.astype(jnp.float32)
    y = xf * lax.rsqrt(jnp.mean(xf * xf, axis=-1, keepdims=True) + RMS_EPS)
    return (y * w.astype(jnp.float32)).astype(x.dtype)


def causal_depthwise_conv(u, w, b):
    k = w.shape[0]
    c = u.shape[-1]
    y = lax.conv_general_dilated(u, w[:, None, :].astype(u.dtype), window_strides=(1,),
                                 padding=[(k - 1, 0)],
                                 dimension_numbers=("NWC", "WIO", "NWC"),
                                 feature_group_count=c)
    return y + b.astype(u.dtype)


def ssd_chunked_scan(xs, dt, a, bmat, cmat):
    bsz, seqlen, nh, hd = xs.shape
    ng, ns = bmat.shape[2], bmat.shape[3]
    r = nh // ng
    nc = seqlen // SSD_CHUNK
    shp = (bsz, nc, SSD_CHUNK, ng, r)
    xd = (xs * dt[..., None]).reshape(shp + (hd,))
    a_cs = jnp.cumsum((dt * a).reshape(shp), axis=2)
    bc = bmat.reshape(bsz, nc, SSD_CHUNK, ng, ns)
    cc = cmat.reshape(bsz, nc, SSD_CHUNK, ng, ns)
    seg = a_cs[:, :, :, None] - a_cs[:, :, None, :]
    causal = jnp.tril(jnp.ones((SSD_CHUNK, SSD_CHUNK), dtype=bool))[None, None, :, :, None, None]
    decay = jnp.exp(jnp.where(causal, seg, -jnp.inf))
    cb = jnp.einsum("bclgn,bcsgn->bclsg", cc, bc)
    y_diag = jnp.einsum("bclsgr,bcsgrp->bclgrp", cb[..., None] * decay, xd)
    decay_to_end = jnp.exp(a_cs[:, :, -1:] - a_cs)
    states = jnp.einsum("bcsgn,bcsgrp->bcgrpn", bc, xd * decay_to_end[..., None])
    chunk_decay = jnp.exp(a_cs[:, :, -1])

    def step(carry, inp):
        st, dec = inp
        return carry * dec[..., None, None] + st, carry

    init = jnp.zeros((bsz, ng, r, hd, ns), jnp.float32)
    _, prev = lax.scan(step, init, (jnp.moveaxis(states, 1, 0), jnp.moveaxis(chunk_decay, 1, 0)))
    prev = jnp.moveaxis(prev, 0, 1)
    y_off = jnp.einsum("bclgn,bcgrpn->bclgrp", cc, prev) * jnp.exp(a_cs)[..., None]
    return (y_diag + y_off).reshape(bsz, seqlen, nh, hd)


def ssd_mixer(z, xbc, dt_raw, conv_w, conv_b, dt_bias, a_log, d_skip, norm_w):
    bsz, seqlen, _ = z.shape
    xbc = jax.nn.silu(causal_depthwise_conv(xbc, conv_w, conv_b))
    xs, bmat, cmat = jnp.split(xbc, [D_INNER, D_INNER + SSD_GROUPS * D_STATE], axis=-1)
    xs = xs.astype(jnp.float32).reshape(bsz, seqlen, SSD_HEADS, SSD_HEAD_DIM)
    bmat = bmat.astype(jnp.float32).reshape(bsz, seqlen, SSD_GROUPS, D_STATE)
    cmat = cmat.astype(jnp.float32).reshape(bsz, seqlen, SSD_GROUPS, D_STATE)
    dt = jax.nn.softplus(dt_raw.astype(jnp.float32) + dt_bias.astype(jnp.float32))
    a = -jnp.exp(a_log.astype(jnp.float32))
    y = ssd_chunked_scan(xs, dt, a, bmat, cmat) + xs * d_skip.astype(jnp.float32)[:, None]
    y = y.reshape(bsz, seqlen, D_INNER) * jax.nn.silu(z.astype(jnp.float32))
    yg = y.reshape(bsz, seqlen, SSD_GROUPS, D_INNER // SSD_GROUPS)
    yg = yg * lax.rsqrt(jnp.mean(yg * yg, axis=-1, keepdims=True) + RMS_EPS)
    y = yg.reshape(bsz, seqlen, D_INNER) * norm_w.astype(jnp.float32)
    return y.astype(z.dtype)


def stick_breaking_attention(q, k, v):
    seqlen, hd = q.shape[2], q.shape[3]
    scale = hd ** -0.5
    outs = []
    for blk in range(seqlen // SB_BLOCK):
        t0 = blk * SB_BLOCK
        t1 = t0 + SB_BLOCK
        logits = jnp.einsum("bhtd,bhsd->bhts", q[:, :, t0:t1], k[:, :, :t1]).astype(jnp.float32) * scale
        mask = jnp.arange(t1)[None, :] < jnp.arange(t0, t1)[:, None]
        log_keep = jnp.where(mask, jax.nn.log_sigmoid(-logits), 0.0)
        later = jnp.flip(jnp.cumsum(jnp.flip(log_keep, -1), axis=-1), -1) - log_keep
        weights = jnp.where(mask, jnp.exp(jax.nn.log_sigmoid(logits) + later), 0.0)
        outs.append(jnp.einsum("bhts,bhsd->bhtd", weights, v[:, :, :t1].astype(jnp.float32)))
    return jnp.concatenate(outs, axis=2)


def _fwd_setup_inputs(seed: int = 0) -> dict:
    key = jax.random.key(seed)
    ks = jax.random.split(key, 26)
    f32 = jnp.float32

    def nrm(k, shape, fan_in):
        return jax.random.normal(k, shape, f32) * (fan_in ** -0.5)

    def gain(k, shape):
        return 1.0 + 0.05 * jax.random.normal(k, shape, f32)

    dt0 = jnp.exp(jax.random.uniform(ks[6], (DEPTH, SSD_HEADS), f32)
                  * (math.log(0.1) - math.log(0.001)) + math.log(0.001))
    dt_bias = dt0 + jnp.log(-jnp.expm1(-dt0))
    a_log = jnp.log(jax.random.uniform(ks[7], (DEPTH, SSD_HEADS), f32, 1.0, 16.0))
    return {
        "x": jax.random.normal(ks[0], (BATCH, SEQ, D_MODEL), f32),
        "p": jax.random.normal(ks[1], (DEPTH, BATCH, SEQ, PLE_DIM), f32),
        "norm_mix_pre": gain(ks[2], (DEPTH, D_MODEL)),
        "w_in": nrm(ks[3], (DEPTH, D_MODEL, IN_PROJ_DIM), D_MODEL),
        "conv_w": nrm(ks[4], (DEPTH, CONV_K, CONV_DIM), CONV_K),
        "conv_b": 0.01 * jax.random.normal(ks[5], (DEPTH, CONV_DIM), f32),
        "dt_bias": dt_bias,
        "a_log": a_log,
        "d_skip": 1.0 + 0.1 * jax.random.normal(ks[8], (DEPTH, SSD_HEADS), f32),
        "ssd_norm": gain(ks[9], (DEPTH, D_INNER)),
        "w_ssd_branch": nrm(ks[10], (DEPTH, D_INNER, D_MODEL), D_INNER),
        "w_sb_branch": nrm(ks[11], (DEPTH, SB_WIDTH, D_MODEL), SB_WIDTH),
        "w_gate": nrm(ks[12], (DEPTH, D_MODEL, N_BRANCHES * D_MODEL), D_MODEL),
        "b_gate": 0.01 * jax.random.normal(ks[13], (DEPTH, N_BRANCHES * D_MODEL), f32),
        "w_out": nrm(ks[14], (DEPTH, D_MODEL, D_MODEL), D_MODEL),
        "norm_mix_post": gain(ks[15], (DEPTH, D_MODEL)),
        "norm_ffn_pre": gain(ks[16], (DEPTH, D_MODEL)),
        "w_ff1": nrm(ks[17], (DEPTH, D_MODEL, D_FF), D_MODEL),
        "w_ff2": nrm(ks[18], (DEPTH, D_FF, D_MODEL), D_FF),
        "norm_ffn_post": gain(ks[19], (DEPTH, D_MODEL)),
        "w_ple": nrm(ks[20], (DEPTH, PLE_DIM, D_MODEL), PLE_DIM),
        "w_ple_gate": nrm(ks[21], (DEPTH, D_MODEL, D_MODEL), D_MODEL),
        "norm_ple_post": gain(ks[22], (DEPTH, D_MODEL)),
    }


def _fwd_reference(x, p, norm_mix_pre, w_in, conv_w, conv_b, dt_bias, a_log, d_skip, ssd_norm,
              w_ssd_branch, w_sb_branch, w_gate, b_gate, w_out, norm_mix_post,
              norm_ffn_pre, w_ff1, w_ff2, norm_ffn_post, w_ple, w_ple_gate, norm_ple_post):
    h = x
    bsz, seqlen, _ = x.shape
    for i in range(DEPTH):
        n1 = rms_norm(h, norm_mix_pre[i])
        proj = n1 @ w_in[i]
        z, xbc, dt_raw, q, k, v = jnp.split(proj, SPLITS, axis=-1)
        y_ssd = ssd_mixer(z, xbc, dt_raw, conv_w[i], conv_b[i], dt_bias[i], a_log[i],
                          d_skip[i], ssd_norm[i])
        heads = lambda t: t.reshape(bsz, seqlen, SB_HEADS, SB_HEAD_DIM).transpose(0, 2, 1, 3)
        y_sb = stick_breaking_attention(heads(q), heads(k), heads(v))
        y_sb = y_sb.transpose(0, 2, 1, 3).reshape(bsz, seqlen, SB_WIDTH).astype(h.dtype)
        gates = jax.nn.sigmoid((n1 @ w_gate[i] + b_gate[i]).astype(jnp.float32)).astype(h.dtype)
        g_ssd, g_sb = jnp.split(gates, N_BRANCHES, axis=-1)
        merged = g_ssd * (y_ssd @ w_ssd_branch[i]) + g_sb * (y_sb @ w_sb_branch[i])
        h = h + rms_norm(merged @ w_out[i], norm_mix_post[i])
        n2 = rms_norm(h, norm_ffn_pre[i])
        ff = jnp.square(jax.nn.relu(n2 @ w_ff1[i])) @ w_ff2[i]
        h = h + rms_norm(ff, norm_ffn_post[i])
        ple_gate = jax.nn.sigmoid((h @ w_ple_gate[i]).astype(jnp.float32)).astype(h.dtype)
        h = h + rms_norm(ple_gate * (p[i].astype(h.dtype) @ w_ple[i]), norm_ple_post[i])
    return h


import jax as _jax
import jax.numpy as _jnp

TWIN_FORMAT = 'train_step'
FWD_PARAMS = ['x', 'p', 'norm_mix_pre', 'w_in', 'conv_w', 'conv_b', 'dt_bias', 'a_log', 'd_skip', 'ssd_norm', 'w_ssd_branch', 'w_sb_branch', 'w_gate', 'b_gate', 'w_out', 'norm_mix_post', 'norm_ffn_pre', 'w_ff1', 'w_ff2', 'norm_ffn_post', 'w_ple', 'w_ple_gate', 'norm_ple_post']
TWIN_WEIGHTS = ['norm_mix_pre', 'w_in', 'conv_w', 'conv_b', 'dt_bias', 'a_log', 'd_skip', 'ssd_norm', 'w_ssd_branch', 'w_sb_branch', 'w_gate', 'b_gate', 'w_out', 'norm_mix_post', 'norm_ffn_pre', 'w_ff1', 'w_ff2', 'norm_ffn_post', 'w_ple', 'w_ple_gate', 'norm_ple_post']
TWIN_DIFF_INPUT = 'x'
TWIN_INPUTS = ['x', 'p', 'norm_mix_pre', 'w_in', 'conv_w', 'conv_b', 'dt_bias', 'a_log', 'd_skip', 'ssd_norm', 'w_ssd_branch', 'w_sb_branch', 'w_gate', 'b_gate', 'w_out', 'norm_mix_post', 'norm_ffn_pre', 'w_ff1', 'w_ff2', 'norm_ffn_post', 'w_ple', 'w_ple_gate', 'norm_ple_post', 'loss_target', 'm_norm_mix_pre', 'm_w_in', 'm_conv_w', 'm_conv_b', 'm_dt_bias', 'm_a_log', 'm_d_skip', 'm_ssd_norm', 'm_w_ssd_branch', 'm_w_sb_branch', 'm_w_gate', 'm_b_gate', 'm_w_out', 'm_norm_mix_post', 'm_norm_ffn_pre', 'm_w_ff1', 'm_w_ff2', 'm_norm_ffn_post', 'm_w_ple', 'm_w_ple_gate', 'm_norm_ple_post', 'v_norm_mix_pre', 'v_w_in', 'v_conv_w', 'v_conv_b', 'v_dt_bias', 'v_a_log', 'v_d_skip', 'v_ssd_norm', 'v_w_ssd_branch', 'v_w_sb_branch', 'v_w_gate', 'v_b_gate', 'v_w_out', 'v_norm_mix_post', 'v_norm_ffn_pre', 'v_w_ff1', 'v_w_ff2', 'v_norm_ffn_post', 'v_w_ple', 'v_w_ple_gate', 'v_norm_ple_post']
TWIN_OUTPUTS = ['loss', 'grad_x', 'grad_norm_mix_pre', 'grad_w_in', 'grad_conv_w', 'grad_conv_b', 'grad_dt_bias', 'grad_a_log', 'grad_d_skip', 'grad_ssd_norm', 'grad_w_ssd_branch', 'grad_w_sb_branch', 'grad_w_gate', 'grad_b_gate', 'grad_w_out', 'grad_norm_mix_post', 'grad_norm_ffn_pre', 'grad_w_ff1', 'grad_w_ff2', 'grad_norm_ffn_post', 'grad_w_ple', 'grad_w_ple_gate', 'grad_norm_ple_post', 'delta_norm_mix_pre', 'delta_w_in', 'delta_conv_w', 'delta_conv_b', 'delta_dt_bias', 'delta_a_log', 'delta_d_skip', 'delta_ssd_norm', 'delta_w_ssd_branch', 'delta_w_sb_branch', 'delta_w_gate', 'delta_b_gate', 'delta_w_out', 'delta_norm_mix_post', 'delta_norm_ffn_pre', 'delta_w_ff1', 'delta_w_ff2', 'delta_norm_ffn_post', 'delta_w_ple', 'delta_w_ple_gate', 'delta_norm_ple_post', 'new_m_norm_mix_pre', 'new_m_w_in', 'new_m_conv_w', 'new_m_conv_b', 'new_m_dt_bias', 'new_m_a_log', 'new_m_d_skip', 'new_m_ssd_norm', 'new_m_w_ssd_branch', 'new_m_w_sb_branch', 'new_m_w_gate', 'new_m_b_gate', 'new_m_w_out', 'new_m_norm_mix_post', 'new_m_norm_ffn_pre', 'new_m_w_ff1', 'new_m_w_ff2', 'new_m_norm_ffn_post', 'new_m_w_ple', 'new_m_w_ple_gate', 'new_m_norm_ple_post', 'new_v_norm_mix_pre', 'new_v_w_in', 'new_v_conv_w', 'new_v_conv_b', 'new_v_dt_bias', 'new_v_a_log', 'new_v_d_skip', 'new_v_ssd_norm', 'new_v_w_ssd_branch', 'new_v_w_sb_branch', 'new_v_w_gate', 'new_v_b_gate', 'new_v_w_out', 'new_v_norm_mix_post', 'new_v_norm_ffn_pre', 'new_v_w_ff1', 'new_v_w_ff2', 'new_v_norm_ffn_post', 'new_v_w_ple', 'new_v_w_ple_gate', 'new_v_norm_ple_post']
TWIN_LEAF_KINDS = {'loss': 'loss', 'grad_x': 'grad_x', 'grad_norm_mix_pre': 'grad_w', 'grad_w_in': 'grad_w', 'grad_conv_w': 'grad_w', 'grad_conv_b': 'grad_w', 'grad_dt_bias': 'grad_w', 'grad_a_log': 'grad_w', 'grad_d_skip': 'grad_w', 'grad_ssd_norm': 'grad_w', 'grad_w_ssd_branch': 'grad_w', 'grad_w_sb_branch': 'grad_w', 'grad_w_gate': 'grad_w', 'grad_b_gate': 'grad_w', 'grad_w_out': 'grad_w', 'grad_norm_mix_post': 'grad_w', 'grad_norm_ffn_pre': 'grad_w', 'grad_w_ff1': 'grad_w', 'grad_w_ff2': 'grad_w', 'grad_norm_ffn_post': 'grad_w', 'grad_w_ple': 'grad_w', 'grad_w_ple_gate': 'grad_w', 'grad_norm_ple_post': 'grad_w', 'delta_norm_mix_pre': 'delta_w', 'delta_w_in': 'delta_w', 'delta_conv_w': 'delta_w', 'delta_conv_b': 'delta_w', 'delta_dt_bias': 'delta_w', 'delta_a_log': 'delta_w', 'delta_d_skip': 'delta_w', 'delta_ssd_norm': 'delta_w', 'delta_w_ssd_branch': 'delta_w', 'delta_w_sb_branch': 'delta_w', 'delta_w_gate': 'delta_w', 'delta_b_gate': 'delta_w', 'delta_w_out': 'delta_w', 'delta_norm_mix_post': 'delta_w', 'delta_norm_ffn_pre': 'delta_w', 'delta_w_ff1': 'delta_w', 'delta_w_ff2': 'delta_w', 'delta_norm_ffn_post': 'delta_w', 'delta_w_ple': 'delta_w', 'delta_w_ple_gate': 'delta_w', 'delta_norm_ple_post': 'delta_w', 'new_m_norm_mix_pre': 'new_m', 'new_m_w_in': 'new_m', 'new_m_conv_w': 'new_m', 'new_m_conv_b': 'new_m', 'new_m_dt_bias': 'new_m', 'new_m_a_log': 'new_m', 'new_m_d_skip': 'new_m', 'new_m_ssd_norm': 'new_m', 'new_m_w_ssd_branch': 'new_m', 'new_m_w_sb_branch': 'new_m', 'new_m_w_gate': 'new_m', 'new_m_b_gate': 'new_m', 'new_m_w_out': 'new_m', 'new_m_norm_mix_post': 'new_m', 'new_m_norm_ffn_pre': 'new_m', 'new_m_w_ff1': 'new_m', 'new_m_w_ff2': 'new_m', 'new_m_norm_ffn_post': 'new_m', 'new_m_w_ple': 'new_m', 'new_m_w_ple_gate': 'new_m', 'new_m_norm_ple_post': 'new_m', 'new_v_norm_mix_pre': 'new_v', 'new_v_w_in': 'new_v', 'new_v_conv_w': 'new_v', 'new_v_conv_b': 'new_v', 'new_v_dt_bias': 'new_v', 'new_v_a_log': 'new_v', 'new_v_d_skip': 'new_v', 'new_v_ssd_norm': 'new_v', 'new_v_w_ssd_branch': 'new_v', 'new_v_w_sb_branch': 'new_v', 'new_v_w_gate': 'new_v', 'new_v_b_gate': 'new_v', 'new_v_w_out': 'new_v', 'new_v_norm_mix_post': 'new_v', 'new_v_norm_ffn_pre': 'new_v', 'new_v_w_ff1': 'new_v', 'new_v_w_ff2': 'new_v', 'new_v_norm_ffn_post': 'new_v', 'new_v_w_ple': 'new_v', 'new_v_w_ple_gate': 'new_v', 'new_v_norm_ple_post': 'new_v'}


def _forward(args):
    return _fwd_reference(*[args[k] for k in FWD_PARAMS])


def _output_shape():
    def fwd():
        inp = _fwd_setup_inputs(0)
        return _fwd_reference(*[inp[k] for k in FWD_PARAMS])
    out = _jax.eval_shape(fwd)
    return out.shape, out.dtype

N_MICROBATCH = 1
ADAM_LR = 0.001
ADAM_B1 = 0.9
ADAM_B2 = 0.999
ADAM_EPS = 1e-08
ADAM_WD = 0.01
ADAM_STEP = 10
PER_EXAMPLE_BATCH_AXIS = {'x': 0, 'p': 1, 'loss_target': 0}
SHARED_INPUTS = []
_WEIGHT_DTYPES = {'norm_mix_pre': _jnp.float32, 'w_in': _jnp.float32, 'conv_w': _jnp.float32, 'conv_b': _jnp.float32, 'dt_bias': _jnp.float32, 'a_log': _jnp.float32, 'd_skip': _jnp.float32, 'ssd_norm': _jnp.float32, 'w_ssd_branch': _jnp.float32, 'w_sb_branch': _jnp.float32, 'w_gate': _jnp.float32, 'b_gate': _jnp.float32, 'w_out': _jnp.float32, 'norm_mix_post': _jnp.float32, 'norm_ffn_pre': _jnp.float32, 'w_ff1': _jnp.float32, 'w_ff2': _jnp.float32, 'norm_ffn_post': _jnp.float32, 'w_ple': _jnp.float32, 'w_ple_gate': _jnp.float32, 'norm_ple_post': _jnp.float32}
MOMENT_SCALE = {'norm_mix_pre': 8.895748e-01, 'w_in': 2.566886e-01, 'conv_w': 6.931537e-01, 'conv_b': 2.441954e+00, 'dt_bias': 6.672395e-01, 'a_log': 5.118390e+00, 'd_skip': 2.607091e+00, 'ssd_norm': 1.610384e+00, 'w_ssd_branch': 2.117052e+00, 'w_sb_branch': 3.040038e-01, 'w_gate': 1.416118e-01, 'b_gate': 6.650294e-01, 'w_out': 2.086303e+00, 'norm_mix_post': 3.224975e+01, 'norm_ffn_pre': 1.024823e+00, 'w_ff1': 5.205212e-01, 'w_ff2': 2.287181e+00, 'norm_ffn_post': 3.322742e+01, 'w_ple': 3.820217e-01, 'w_ple_gate': 3.232389e-01, 'norm_ple_post': 3.293269e+01}


def _to_microbatches(a, axis):
    t = _jnp.moveaxis(a, axis, 0)
    t = t.reshape((N_MICROBATCH, t.shape[0] // N_MICROBATCH) + t.shape[1:])
    return _jnp.moveaxis(t, 1, axis + 1)


def setup_inputs(seed: int = 0) -> dict:
    inp = _fwd_setup_inputs(seed)
    key = _jax.random.fold_in(_jax.random.key(seed), 7919)
    shape, _ = _output_shape()
    out = dict(inp)
    out["loss_target"] = _jax.random.normal(_jax.random.fold_in(key, 0), shape, _jnp.float32)
    for i, name in enumerate(TWIN_WEIGHTS):
        w = inp[name].astype(_jnp.float32)
        if MOMENT_SCALE is None:
            s = _jnp.sqrt(_jnp.mean(_jnp.square(w)) + 1e-30)
        else:
            s = MOMENT_SCALE[name]
        km, kv = _jax.random.split(_jax.random.fold_in(key, i + 1))
        out[name] = w
        out["m_" + name] = s * _jax.random.normal(km, w.shape, _jnp.float32)
        out["v_" + name] = (s * s) * _jax.random.uniform(kv, w.shape, _jnp.float32, 0.5, 1.5)
    if N_MICROBATCH > 1:
        for name, axis in PER_EXAMPLE_BATCH_AXIS.items():
            out[name] = _to_microbatches(out[name], axis)
    return {'x': out['x'], 'p': out['p'], 'norm_mix_pre': out['norm_mix_pre'], 'w_in': out['w_in'], 'conv_w': out['conv_w'], 'conv_b': out['conv_b'], 'dt_bias': out['dt_bias'], 'a_log': out['a_log'], 'd_skip': out['d_skip'], 'ssd_norm': out['ssd_norm'], 'w_ssd_branch': out['w_ssd_branch'], 'w_sb_branch': out['w_sb_branch'], 'w_gate': out['w_gate'], 'b_gate': out['b_gate'], 'w_out': out['w_out'], 'norm_mix_post': out['norm_mix_post'], 'norm_ffn_pre': out['norm_ffn_pre'], 'w_ff1': out['w_ff1'], 'w_ff2': out['w_ff2'], 'norm_ffn_post': out['norm_ffn_post'], 'w_ple': out['w_ple'], 'w_ple_gate': out['w_ple_gate'], 'norm_ple_post': out['norm_ple_post'], 'loss_target': out['loss_target'], 'm_norm_mix_pre': out['m_norm_mix_pre'], 'm_w_in': out['m_w_in'], 'm_conv_w': out['m_conv_w'], 'm_conv_b': out['m_conv_b'], 'm_dt_bias': out['m_dt_bias'], 'm_a_log': out['m_a_log'], 'm_d_skip': out['m_d_skip'], 'm_ssd_norm': out['m_ssd_norm'], 'm_w_ssd_branch': out['m_w_ssd_branch'], 'm_w_sb_branch': out['m_w_sb_branch'], 'm_w_gate': out['m_w_gate'], 'm_b_gate': out['m_b_gate'], 'm_w_out': out['m_w_out'], 'm_norm_mix_post': out['m_norm_mix_post'], 'm_norm_ffn_pre': out['m_norm_ffn_pre'], 'm_w_ff1': out['m_w_ff1'], 'm_w_ff2': out['m_w_ff2'], 'm_norm_ffn_post': out['m_norm_ffn_post'], 'm_w_ple': out['m_w_ple'], 'm_w_ple_gate': out['m_w_ple_gate'], 'm_norm_ple_post': out['m_norm_ple_post'], 'v_norm_mix_pre': out['v_norm_mix_pre'], 'v_w_in': out['v_w_in'], 'v_conv_w': out['v_conv_w'], 'v_conv_b': out['v_conv_b'], 'v_dt_bias': out['v_dt_bias'], 'v_a_log': out['v_a_log'], 'v_d_skip': out['v_d_skip'], 'v_ssd_norm': out['v_ssd_norm'], 'v_w_ssd_branch': out['v_w_ssd_branch'], 'v_w_sb_branch': out['v_w_sb_branch'], 'v_w_gate': out['v_w_gate'], 'v_b_gate': out['v_b_gate'], 'v_w_out': out['v_w_out'], 'v_norm_mix_post': out['v_norm_mix_post'], 'v_norm_ffn_pre': out['v_norm_ffn_pre'], 'v_w_ff1': out['v_w_ff1'], 'v_w_ff2': out['v_w_ff2'], 'v_norm_ffn_post': out['v_norm_ffn_post'], 'v_w_ple': out['v_w_ple'], 'v_w_ple_gate': out['v_w_ple_gate'], 'v_norm_ple_post': out['v_norm_ple_post']}


def _loss(weights, diff, rest, loss_target):
    with _jax.named_scope("forward"):
        args = {**rest, TWIN_DIFF_INPUT: diff, **{k: w.astype(_WEIGHT_DTYPES[k]) for k, w in weights.items()}}
        y = _forward(args)
    with _jax.named_scope("loss_head"):
        err = _jnp.square(y.astype(_jnp.float32) - loss_target)
        return 0.5 * _jnp.sum(_jnp.mean(err, axis=-1)) if err.ndim else 0.5 * err


def _adamw(w, g, m, v):
    m = ADAM_B1 * m + (1.0 - ADAM_B1) * g
    v = ADAM_B2 * v + (1.0 - ADAM_B2) * _jnp.square(g)
    m_hat = m / (1.0 - ADAM_B1 ** ADAM_STEP)
    v_hat = v / (1.0 - ADAM_B2 ** ADAM_STEP)
    delta = -ADAM_LR * (m_hat / (_jnp.sqrt(v_hat) + ADAM_EPS) + ADAM_WD * w)
    return delta, m, v


def reference(x, p, norm_mix_pre, w_in, conv_w, conv_b, dt_bias, a_log, d_skip, ssd_norm, w_ssd_branch, w_sb_branch, w_gate, b_gate, w_out, norm_mix_post, norm_ffn_pre, w_ff1, w_ff2, norm_ffn_post, w_ple, w_ple_gate, norm_ple_post, loss_target, m_norm_mix_pre, m_w_in, m_conv_w, m_conv_b, m_dt_bias, m_a_log, m_d_skip, m_ssd_norm, m_w_ssd_branch, m_w_sb_branch, m_w_gate, m_b_gate, m_w_out, m_norm_mix_post, m_norm_ffn_pre, m_w_ff1, m_w_ff2, m_norm_ffn_post, m_w_ple, m_w_ple_gate, m_norm_ple_post, v_norm_mix_pre, v_w_in, v_conv_w, v_conv_b, v_dt_bias, v_a_log, v_d_skip, v_ssd_norm, v_w_ssd_branch, v_w_sb_branch, v_w_gate, v_b_gate, v_w_out, v_norm_mix_post, v_norm_ffn_pre, v_w_ff1, v_w_ff2, v_norm_ffn_post, v_w_ple, v_w_ple_gate, v_norm_ple_post):
    given = dict(x=x, p=p, norm_mix_pre=norm_mix_pre, w_in=w_in, conv_w=conv_w, conv_b=conv_b, dt_bias=dt_bias, a_log=a_log, d_skip=d_skip, ssd_norm=ssd_norm, w_ssd_branch=w_ssd_branch, w_sb_branch=w_sb_branch, w_gate=w_gate, b_gate=b_gate, w_out=w_out, norm_mix_post=norm_mix_post, norm_ffn_pre=norm_ffn_pre, w_ff1=w_ff1, w_ff2=w_ff2, norm_ffn_post=norm_ffn_post, w_ple=w_ple, w_ple_gate=w_ple_gate, norm_ple_post=norm_ple_post, loss_target=loss_target, m_norm_mix_pre=m_norm_mix_pre, m_w_in=m_w_in, m_conv_w=m_conv_w, m_conv_b=m_conv_b, m_dt_bias=m_dt_bias, m_a_log=m_a_log, m_d_skip=m_d_skip, m_ssd_norm=m_ssd_norm, m_w_ssd_branch=m_w_ssd_branch, m_w_sb_branch=m_w_sb_branch, m_w_gate=m_w_gate, m_b_gate=m_b_gate, m_w_out=m_w_out, m_norm_mix_post=m_norm_mix_post, m_norm_ffn_pre=m_norm_ffn_pre, m_w_ff1=m_w_ff1, m_w_ff2=m_w_ff2, m_norm_ffn_post=m_norm_ffn_post, m_w_ple=m_w_ple, m_w_ple_gate=m_w_ple_gate, m_norm_ple_post=m_norm_ple_post, v_norm_mix_pre=v_norm_mix_pre, v_w_in=v_w_in, v_conv_w=v_conv_w, v_conv_b=v_conv_b, v_dt_bias=v_dt_bias, v_a_log=v_a_log, v_d_skip=v_d_skip, v_ssd_norm=v_ssd_norm, v_w_ssd_branch=v_w_ssd_branch, v_w_sb_branch=v_w_sb_branch, v_w_gate=v_w_gate, v_b_gate=v_b_gate, v_w_out=v_w_out, v_norm_mix_post=v_norm_mix_post, v_norm_ffn_pre=v_norm_ffn_pre, v_w_ff1=v_w_ff1, v_w_ff2=v_w_ff2, v_norm_ffn_post=v_norm_ffn_post, v_w_ple=v_w_ple, v_w_ple_gate=v_w_ple_gate, v_norm_ple_post=v_norm_ple_post)
    weights = {n: given[n] for n in TWIN_WEIGHTS}
    shared = {n: given[n] for n in SHARED_INPUTS}
    per_example = {n: given[n] for n in ['x', 'p']}
    grad_fn = _jax.value_and_grad(_loss, argnums=(0, 1))

    def one_microbatch(ex, loss_target):
        ex = dict(ex)
        diff = ex.pop(TWIN_DIFF_INPUT)
        return grad_fn(weights, diff, {**shared, **ex}, loss_target)

    if N_MICROBATCH == 1:
        loss, (grad_w, grad_x) = one_microbatch(per_example, given["loss_target"])
    else:
        def body(carry, xs):
            loss_sum, grad_sum = carry
            l_k, (gw_k, gx_k) = one_microbatch(xs[0], xs[1])
            with _jax.named_scope("update"):
                return (loss_sum + l_k, _jax.tree.map(_jnp.add, grad_sum, gw_k)), gx_k

        init = (_jnp.zeros((), _jnp.float32), _jax.tree.map(_jnp.zeros_like, weights))
        (loss, grad_w), grad_x = _jax.lax.scan(body, init, (per_example, given["loss_target"]))
    with _jax.named_scope("update"):
        delta_w, new_m, new_v = {}, {}, {}
        for n in TWIN_WEIGHTS:
            delta_w[n], new_m[n], new_v[n] = _adamw(weights[n], grad_w[n], given["m_" + n], given["v_" + n])
    return (loss, grad_x, *[grad_w[n] for n in TWIN_WEIGHTS], *[delta_w[n] for n in TWIN_WEIGHTS],
            *[new_m[n] for n in TWIN_WEIGHTS], *[new_v[n] for n in TWIN_WEIGHTS])
```

```python
import functools

import jax
import jax.numpy as jnp
from jax import lax
from jax.experimental import pallas as pl
from jax.experimental.pallas import tpu as pltpu

F32 = jnp.float32
MXU = jnp.bfloat16
VMEM_LIMIT = 56 * 1024 * 1024

D_MODEL = 1024
D_INNER = 2048
SSD_HEADS = 32
HEAD_DIM = 64
SSD_GROUPS = 8
D_STATE = 128
CONV_K = 4
CONV_DIM = 4096
CHUNK = 128
SB_WIDTH = 1024
D_FF = 4096
PLE_DIM = 256
RMS_EPS = 1e-6
SB_SCALE = HEAD_DIM ** -0.5
N_DEV = 8
LANES = 128

OFF_Z, OFF_XBC, OFF_Q, OFF_K, OFF_V, OFF_G = 0, 2048, 6144, 7168, 8192, 9216
N_MAIN = 11264

ADAM_LR = 0.001
ADAM_B1 = 0.9
ADAM_B2 = 0.999
ADAM_EPS = 1e-08
ADAM_WD = 0.01
ADAM_STEP = 10


def _sig(x):
    return 1.0 / (1.0 + jnp.exp(-x))


def _softplus(x):
    return jnp.maximum(x, 0.0) + jnp.log(1.0 + jnp.exp(-jnp.abs(x)))


def _rms(x, w):
    return x * lax.rsqrt(jnp.mean(x * x, axis=-1, keepdims=True) + RMS_EPS) * w


def _dot(a, b):
    return jnp.dot(a, b, preferred_element_type=F32)


def _dot_nt(a, b):
    return lax.dot_general(a, b, (((1,), (1,)), ((), ())), preferred_element_type=F32)


def _dot_tn(a, b):
    return lax.dot_general(a, b, (((0,), (0,)), ((), ())), preferred_element_type=F32)


def _split3(x):
    x1 = x.astype(MXU)
    r = x - x1.astype(F32)
    x2 = r.astype(MXU)
    r = r - x2.astype(F32)
    return x1, x2, r.astype(MXU)


def _dot3_l(a, u):
    a1, a2, a3 = _split3(a)
    return (_dot(a3, u) + _dot(a2, u)) + _dot(a1, u)


def _dot3_r(u, a):
    a1, a2, a3 = _split3(a)
    return (_dot(u, a3) + _dot(u, a2)) + _dot(u, a1)


def _iota(shape, dim):
    return lax.broadcasted_iota(jnp.int32, shape, dim)


def _tri(n, cmp):
    r, c = _iota((n, n), 0), _iota((n, n), 1)
    return cmp(r, c).astype(F32).astype(MXU)


def _cparams(sem):
    return pltpu.CompilerParams(dimension_semantics=sem, vmem_limit_bytes=VMEM_LIMIT)


def _pick(n, cands):
    for c in cands:
        if n % c == 0:
            return c
    return n


def mm(name, a, b, mode, add=None, out_dtype=F32):
    if mode == "nn":
        (M, K), (K2, N) = a.shape, b.shape
    elif mode == "nt":
        (M, K), (N, K2) = a.shape, b.shape
    else:
        (K, M), (K2, N) = a.shape, b.shape
    assert K == K2, (name, a.shape, b.shape)
    tm = _pick(M, (1024, 512, 256, 128))
    tn = _pick(N, (1024, 512, 256, 128))
    tk = _pick(K, (1024, 512, 256, 128))
    nk = K // tk

    def body(*refs):
        if add is None:
            a_ref, b_ref, o_ref, acc = refs
        else:
            a_ref, b_ref, add_ref, o_ref, acc = refs
        k = pl.program_id(2)

        @pl.when(k == 0)
        def _():
            acc[...] = jnp.zeros_like(acc) if add is None else add_ref[...]

        av, bv = a_ref[...], b_ref[...]
        if mode == "nn":
            acc[...] += _dot(av, bv)
        elif mode == "nt":
            acc[...] += _dot_nt(av, bv)
        else:
            acc[...] += _dot_tn(av, bv)

        @pl.when(k == nk - 1)
        def _():
            o_ref[...] = acc[...].astype(o_ref.dtype)

    if mode == "nn":
        a_spec = pl.BlockSpec((tm, tk), lambda i, j, k: (i, k))
        b_spec = pl.BlockSpec((tk, tn), lambda i, j, k: (k, j))
    elif mode == "nt":
        a_spec = pl.BlockSpec((tm, tk), lambda i, j, k: (i, k))
        b_spec = pl.BlockSpec((tn, tk), lambda i, j, k: (j, k))
    else:
        a_spec = pl.BlockSpec((tk, tm), lambda i, j, k: (k, i))
        b_spec = pl.BlockSpec((tk, tn), lambda i, j, k: (k, j))
    o_spec = pl.BlockSpec((tm, tn), lambda i, j, k: (i, j))
    in_specs, args = [a_spec, b_spec], [a, b]
    if add is not None:
        in_specs.append(o_spec)
        args.append(add)
    return pl.pallas_call(
        body,
        name=name,
        out_shape=jax.ShapeDtypeStruct((M, N), out_dtype),
        grid=(M // tm, N // tn, nk),
        in_specs=in_specs,
        out_specs=o_spec,
        scratch_shapes=[pltpu.VMEM((tm, tn), F32)],
        compiler_params=_cparams(("parallel", "parallel", "arbitrary")),
    )(*args)


def rowwise(name, fn, rows, bcast, outs, accs=(), tr=256, ncb=1):
    S = rows[0][0].shape[0]
    tr = min(tr, S)
    nrb = S // tr
    in_specs, args = [], []
    for arr, off, w in rows:
        assert off % w == 0 and arr.shape[0] == S
        in_specs.append(pl.BlockSpec((tr, w), lambda j, i, ob=off // w: (i, ob + j)))
        args.append(arr)
    for arr, off, w in bcast:
        assert off % w == 0
        in_specs.append(pl.BlockSpec((arr.shape[0], w), lambda j, i, ob=off // w: (0, ob + j)))
        args.append(arr)
    out_shape, out_specs = [], []
    for tw, w, dt in outs:
        out_shape.append(jax.ShapeDtypeStruct((S, tw), dt))
        out_specs.append(pl.BlockSpec((tr, w), lambda j, i: (i, j)))
    for tw, w in accs:
        out_shape.append(jax.ShapeDtypeStruct((1, tw), F32))
        out_specs.append(pl.BlockSpec((1, w), lambda j, i: (0, j)))
    nin, nout = len(args), len(outs)

    def body(*refs):
        res = fn(*[r[...] for r in refs[:nin]])
        o_refs, a_refs = refs[nin:nin + nout], refs[nin + nout:]
        for r, v in zip(o_refs, res[:nout]):
            r[...] = v.astype(r.dtype)
        i = pl.program_id(1)
        for r, v in zip(a_refs, res[nout:]):
            @pl.when(i == 0)
            def _(r=r, v=v):
                r[...] = v

            @pl.when(i > 0)
            def _(r=r, v=v):
                r[...] += v

    res = pl.pallas_call(
        body,
        name=name,
        out_shape=out_shape,
        grid=(ncb, nrb),
        in_specs=in_specs,
        out_specs=out_specs,
        compiler_params=_cparams(("parallel", "arbitrary")),
    )(*args)
    return res


CONV_TC = 128


def _conv_pre(u, w, b, row):
    pre = b + w[3:4, :] * u
    shifted = []
    for s in (1, 2, 3):
        us = jnp.where(row >= s, pltpu.roll(u, s, 0), 0.0)
        shifted.append(us)
        pre = pre + w[3 - s:4 - s, :] * us
    return pre, shifted


def conv_fwd(projmain, conv_w, conv_b):
    S = projmain.shape[0]
    tc = CONV_TC

    def body(u_ref, w_ref, b_ref, o_ref):
        u = u_ref[...]
        row = _iota(u.shape, 0)
        pre, _ = _conv_pre(u, w_ref[...], b_ref[...], row)
        o_ref[...] = pre * _sig(pre)

    return pl.pallas_call(
        body,
        name="conv_fwd",
        out_shape=jax.ShapeDtypeStruct((S, CONV_DIM), F32),
        grid=(CONV_DIM // tc,),
        in_specs=[
            pl.BlockSpec((S, tc), lambda j: (0, OFF_XBC // tc + j)),
            pl.BlockSpec((CONV_K, tc), lambda j: (0, j)),
            pl.BlockSpec((1, tc), lambda j: (0, j)),
        ],
        out_specs=pl.BlockSpec((S, tc), lambda j: (0, j)),
        compiler_params=_cparams(("parallel",)),
    )(projmain, conv_w, conv_b)


def conv_bwd(dact, projmain, conv_w, conv_b):
    S = projmain.shape[0]
    tc = CONV_TC

    def body(d_ref, u_ref, w_ref, b_ref, du_ref, dw_ref, db_ref):
        u = u_ref[...]
        w = w_ref[...]
        row = _iota(u.shape, 0)
        pre, shifted = _conv_pre(u, w, b_ref[...], row)
        sg = _sig(pre)
        dpre = d_ref[...] * (sg * (1.0 + pre * (1.0 - sg)))
        du = w[3:4, :] * dpre
        dw_ref[3:4, :] = jnp.sum(dpre * u, axis=0, keepdims=True)
        for s in (1, 2, 3):
            ds = jnp.where(row < S - s, pltpu.roll(dpre, S - s, 0), 0.0)
            du = du + w[3 - s:4 - s, :] * ds
            dw_ref[3 - s:4 - s, :] = jnp.sum(dpre * shifted[s - 1], axis=0, keepdims=True)
        du_ref[...] = du.astype(du_ref.dtype)
        db_ref[...] = jnp.sum(dpre, axis=0, keepdims=True)

    return pl.pallas_call(
        body,
        name="conv_bwd",
        out_shape=[
            jax.ShapeDtypeStruct((S, CONV_DIM), MXU),
            jax.ShapeDtypeStruct((CONV_K, CONV_DIM), F32),
            jax.ShapeDtypeStruct((1, CONV_DIM), F32),
        ],
        grid=(CONV_DIM // tc,),
        in_specs=[
            pl.BlockSpec((S, tc), lambda j: (0, j)),
            pl.BlockSpec((S, tc), lambda j: (0, OFF_XBC // tc + j)),
            pl.BlockSpec((CONV_K, tc), lambda j: (0, j)),
            pl.BlockSpec((1, tc), lambda j: (0, j)),
        ],
        out_specs=[
            pl.BlockSpec((S, tc), lambda j: (0, j)),
            pl.BlockSpec((CONV_K, tc), lambda j: (0, j)),
            pl.BlockSpec((1, tc), lambda j: (0, j)),
        ],
        compiler_params=_cparams(("parallel",)),
    )(dact, projmain, conv_w, conv_b)


def _head_expand():
    r, j = _iota((LANES, D_INNER), 0), _iota((LANES, D_INNER), 1)
    return ((j >= r * HEAD_DIM) & (j < r * HEAD_DIM + HEAD_DIM)).astype(F32).astype(MXU)


def _head_reduce():
    j, r = _iota((D_INNER, LANES), 0), _iota((D_INNER, LANES), 1)
    return ((j >= r * HEAD_DIM) & (j < r * HEAD_DIM + HEAD_DIM)).astype(F32).astype(MXU)


def ssd_prep(dtraw, dt_bias_pad, a_exp):
    S = dtraw.shape[0]

    def body(dtr_ref, bias_ref, a_ref, dte_ref, cse_ref):
        dt = _softplus(dtr_ref[...] + bias_ref[...])
        dte = _dot3_l(dt, _head_expand())
        dte_ref[...] = dte
        incl = _tri(CHUNK, lambda r, c: r >= c)
        cse_ref[...] = _dot3_r(incl, dte * a_ref[...])

    return pl.pallas_call(
        body,
        name="ssd_prep",
        out_shape=[jax.ShapeDtypeStruct((S, D_INNER), F32)] * 2,
        grid=(S // CHUNK,),
        in_specs=[
            pl.BlockSpec((CHUNK, LANES), lambda c: (c, 0)),
            pl.BlockSpec((1, LANES), lambda c: (0, 0)),
            pl.BlockSpec((1, D_INNER), lambda c: (0, 0)),
        ],
        out_specs=[pl.BlockSpec((CHUNK, D_INNER), lambda c: (c, 0))] * 2,
        compiler_params=_cparams(("parallel",)),
    )(dtraw, dt_bias_pad, a_exp)


GW = 4 * HEAD_DIM


def ssd_fwd(xbc_act, dte, cse):
    S = xbc_act.shape[0]
    nc = S // CHUNK

    def body(xs_ref, b_ref, c_ref, dte_ref, cse_ref, y_ref, st_ref, s_scr):
        c = pl.program_id(1)

        @pl.when(c == 0)
        def _():
            s_scr[...] = jnp.zeros_like(s_scr)

        s_in = s_scr[...]
        st_ref[0] = s_in
        cs = cse_ref[...]
        xd = xs_ref[...] * dte_ref[...]
        bm, cm = b_ref[...], c_ref[...]
        bb, cb = bm.astype(MXU), cm.astype(MXU)
        cs_last = cs[CHUNK - 1:CHUNK, :]
        g = _dot_nt(cb, bb)
        cs_t = cs.T
        yoff = _dot(cb, s_in.astype(MXU)) * jnp.exp(cs)
        row, col = _iota((CHUNK, CHUNK), 0), _iota((CHUNK, CHUNK), 1)
        for p in range(2):
            sl = slice(LANES * p, LANES * (p + 1))
            xdp = xd[:, sl].astype(MXU)
            yd = []
            for hh in range(2):
                h = 2 * p + hh
                lo = HEAD_DIM * h
                lam = jnp.where(row >= col, jnp.exp(cs[:, lo:lo + 1] - cs_t[lo:lo + 1, :]), 0.0)
                yd.append(_dot((g * lam).astype(MXU), xdp))
            y_ref[:, sl] = yoff[:, sl] + jnp.where(col < HEAD_DIM, yd[0], yd[1])
        w = (xd * jnp.exp(cs_last - cs)).astype(MXU)
        s_scr[...] = jnp.exp(cs_last) * s_in + _dot(bm.T.astype(MXU), w)

    return pl.pallas_call(
        body,
        name="ssd_fwd",
        out_shape=[
            jax.ShapeDtypeStruct((S, D_INNER), F32),
            jax.ShapeDtypeStruct((nc, D_STATE, D_INNER), F32),
        ],
        grid=(SSD_GROUPS, nc),
        in_specs=[
            pl.BlockSpec((CHUNK, GW), lambda g, c: (c, g)),
            pl.BlockSpec((CHUNK, D_STATE), lambda g, c: (c, D_INNER // D_STATE + g)),
            pl.BlockSpec((CHUNK, D_STATE), lambda g, c: (c, D_INNER // D_STATE + SSD_GROUPS + g)),
            pl.BlockSpec((CHUNK, GW), lambda g, c: (c, g)),
            pl.BlockSpec((CHUNK, GW), lambda g, c: (c, g)),
        ],
        out_specs=[
            pl.BlockSpec((CHUNK, GW), lambda g, c: (c, g)),
            pl.BlockSpec((1, D_STATE, GW), lambda g, c: (c, 0, g)),
        ],
        scratch_shapes=[pltpu.VMEM((D_STATE, GW), F32)],
        compiler_params=_cparams(("parallel", "arbitrary")),
    )(xbc_act, xbc_act, xbc_act, dte, cse)


def ssd_bwd(dy, xbc_act, dte, cse, states, a_exp):
    S = xbc_act.shape[0]
    nc = S // CHUNK

    def body(dy_ref, xs_ref, b_ref, c_ref, dte_ref, cse_ref, sin_ref, sout_ref, a_ref,
             dxs_ref, db_ref, dc_ref, ddt_ref, dal_ref, ds_scr, dxd_scr, dcs_scr):
        j = pl.program_id(1)

        @pl.when(j == 0)
        def _():
            ds_scr[...] = jnp.zeros_like(ds_scr)
            dal_ref[...] = jnp.zeros_like(dal_ref)

        ds_out = ds_scr[...]
        dyv, xs = dy_ref[...], xs_ref[...]
        dt, cs = dte_ref[...], cse_ref[...]
        s_in = sin_ref[0]
        bm, cm = b_ref[...], c_ref[...]
        bb, cb = bm.astype(MXU), cm.astype(MXU)
        dsb = ds_out.astype(MXU)
        xd = xs * dt
        ecs = jnp.exp(cs)
        cs_last = cs[CHUNK - 1:CHUNK, :]
        eend = jnp.exp(cs_last - cs)
        g = _dot_nt(cb, bb)
        g_t = _dot_nt(bb, cb)
        cs_t = cs.T
        dxd_off = _dot(bb, dsb) * eend
        yoff = _dot(cb, s_in.astype(MXU)) * ecs
        row, col = _iota((CHUNK, CHUNK), 0), _iota((CHUNK, CHUNK), 1)
        dg = jnp.zeros((CHUNK, CHUNK), F32)
        for p in range(2):
            sl = slice(LANES * p, LANES * (p + 1))
            dyp = dyv[:, sl]
            xdpb = xd[:, sl].astype(MXU)
            acc = dxd_off[:, sl]
            dcs_p = jnp.zeros((CHUNK, LANES), F32)
            for hh in range(2):
                h = 2 * p + hh
                lo = HEAD_DIM * h
                hm = (col >= HEAD_DIM * hh) & (col < HEAD_DIM * (hh + 1))
                cs_col, cs_row = cs[:, lo:lo + 1], cs_t[lo:lo + 1, :]
                lam = jnp.where(row >= col, jnp.exp(cs_col - cs_row), 0.0)
                lam_t = jnp.where(col >= row, jnp.exp(cs_row - cs_col), 0.0)
                dym = jnp.where(hm, dyp, 0.0).astype(MXU)
                m_t = g_t * lam_t
                acc = acc + _dot(m_t.astype(MXU), dym)
                dm = _dot_nt(dym, xdpb)
                dm_t = _dot_nt(xdpb, dym)
                dg = dg + dm * lam
                wdiff = (jnp.sum(dm * (g * lam), axis=1, keepdims=True)
                         - jnp.sum(dm_t * m_t, axis=1, keepdims=True))
                dcs_p = dcs_p + jnp.where(col == HEAD_DIM * hh, wdiff, 0.0)
            dxd_scr[:, sl] = acc
            dcs_scr[:, sl] = dcs_p
        dxd = dxd_scr[...]
        dgb = dg.astype(MXU)
        dye = (dyv * ecs).astype(MXU)
        dc_ref[...] = _dot(dgb, bb) + _dot_nt(dye, s_in.astype(MXU))
        db_ref[...] = _dot(dg.T.astype(MXU), cb) + _dot_nt((xd * eend).astype(MXU), dsb)
        ds_scr[...] = jnp.exp(cs_last) * ds_out + _dot(cm.T.astype(MXU), dye)
        last = jnp.sum(ds_out * sout_ref[0], axis=0, keepdims=True)
        rows = _iota((CHUNK, GW), 0)
        dcs = dcs_scr[...] + dyv * yoff - xd * dxd_off + jnp.where(rows == CHUNK - 1, last, 0.0)
        dda = _dot3_r(_tri(CHUNK, lambda r, c: c >= r), dcs)
        ddt_ref[...] = a_ref[...] * dda + dxd * xs
        dal_ref[...] += jnp.sum(dt * dda, axis=0, keepdims=True)
        dxs_ref[...] = dxd * dt

    rc = lambda g, j: (nc - 1 - j, g)
    return pl.pallas_call(
        body,
        name="ssd_bwd",
        out_shape=[
            jax.ShapeDtypeStruct((S, D_INNER), F32),
            jax.ShapeDtypeStruct((S, SSD_GROUPS * D_STATE), F32),
            jax.ShapeDtypeStruct((S, SSD_GROUPS * D_STATE), F32),
            jax.ShapeDtypeStruct((S, D_INNER), F32),
            jax.ShapeDtypeStruct((1, D_INNER), F32),
        ],
        grid=(SSD_GROUPS, nc),
        in_specs=[
            pl.BlockSpec((CHUNK, GW), rc),
            pl.BlockSpec((CHUNK, GW), rc),
            pl.BlockSpec((CHUNK, D_STATE), lambda g, j: (nc - 1 - j, D_INNER // D_STATE + g)),
            pl.BlockSpec((CHUNK, D_STATE), lambda g, j: (nc - 1 - j, D_INNER // D_STATE + SSD_GROUPS + g)),
            pl.BlockSpec((CHUNK, GW), rc),
            pl.BlockSpec((CHUNK, GW), rc),
            pl.BlockSpec((1, D_STATE, GW), lambda g, j: (nc - 1 - j, 0, g)),
            pl.BlockSpec((1, D_STATE, GW), lambda g, j: (jnp.minimum(nc - j, nc - 1), 0, g)),
            pl.BlockSpec((1, GW), lambda g, j: (0, g)),
        ],
        out_specs=[
            pl.BlockSpec((CHUNK, GW), rc),
            pl.BlockSpec((CHUNK, D_STATE), rc),
            pl.BlockSpec((CHUNK, D_STATE), rc),
            pl.BlockSpec((CHUNK, GW), rc),
            pl.BlockSpec((1, GW), lambda g, j: (0, g)),
        ],
        scratch_shapes=[pltpu.VMEM((D_STATE, GW), F32), pltpu.VMEM((CHUNK, GW), F32),
                        pltpu.VMEM((CHUNK, GW), F32)],
        compiler_params=_cparams(("parallel", "arbitrary")),
    )(dy, xbc_act, xbc_act, xbc_act, dte, cse, states, states, a_exp)


SB_T = 128


def _sb_scores(qm, k_ref, ks, rowi, coli):
    kblk = k_ref[pl.ds(ks, SB_T), :].astype(MXU)
    z = _dot_nt(qm, kblk) * SB_SCALE
    mask = (ks + coli) < rowi
    sp = _softplus(z)
    return kblk, z, mask, sp, jnp.where(mask, sp, 0.0)


def sb_fwd(projmain):
    S = projmain.shape[0]
    nq = S // SB_T

    def body(q_ref, k_ref, v_ref, o_ref, t_ref):
        qb = pl.program_id(1)
        q = q_ref[...]
        lane = _iota((SB_T, LANES), 1)
        rowi = qb * SB_T + _iota((SB_T, SB_T), 0)
        coli = _iota((SB_T, SB_T), 1)
        u_after = _tri(SB_T, lambda r, c: r > c)
        res = []
        for hh in range(2):
            hm = (lane >= HEAD_DIM * hh) & (lane < HEAD_DIM * (hh + 1))
            qm = jnp.where(hm, q, 0.0).astype(MXU)

            def kstep(i, carry, qm=qm):
                r, acc = carry
                ks = pl.multiple_of((qb - i) * SB_T, SB_T)
                _, z, mask, sp, spm = _sb_scores(qm, k_ref, ks, rowi, coli)
                vblk = v_ref[pl.ds(ks, SB_T), :].astype(MXU)
                la = z - sp - _dot3_l(spm, u_after) - r
                a = jnp.where(mask, jnp.exp(la), 0.0)
                acc = acc + _dot(a.astype(MXU), vblk)
                return r + jnp.sum(spm, axis=1, keepdims=True), acc

            res.append(lax.fori_loop(0, qb + 1, kstep,
                                     (jnp.zeros((SB_T, 1), F32), jnp.zeros((SB_T, LANES), F32))))
        o_ref[...] = jnp.where(lane < HEAD_DIM, res[0][1], res[1][1]).astype(o_ref.dtype)
        t_ref[...] = jnp.where(lane < HEAD_DIM, res[0][0], res[1][0])

    return pl.pallas_call(
        body,
        name="sb_fwd",
        out_shape=[jax.ShapeDtypeStruct((S, SB_WIDTH), MXU), jax.ShapeDtypeStruct((S, SB_WIDTH), F32)],
        grid=(SB_WIDTH // LANES, nq),
        in_specs=[
            pl.BlockSpec((SB_T, LANES), lambda h, i: (i, OFF_Q // LANES + h)),
            pl.BlockSpec((S, LANES), lambda h, i: (0, OFF_K // LANES + h)),
            pl.BlockSpec((S, LANES), lambda h, i: (0, OFF_V // LANES + h)),
        ],
        out_specs=[pl.BlockSpec((SB_T, LANES), lambda h, i: (i, h))] * 2,
        compiler_params=_cparams(("parallel", "arbitrary")),
    )(projmain, projmain, projmain)


def sb_bwd(projmain, do, t_exp):
    S = projmain.shape[0]
    nq = S // SB_T

    def body(q_ref, k_ref, v_ref, do_ref, t_ref, dq_ref, dk_ref, dv_ref):
        qb = pl.program_id(1)

        @pl.when(qb == 0)
        def _():
            dk_ref[...] = jnp.zeros_like(dk_ref)
            dv_ref[...] = jnp.zeros_like(dv_ref)

        q, dov, tv = q_ref[...], do_ref[...], t_ref[...]
        lane = _iota((SB_T, LANES), 1)
        rowi = qb * SB_T + _iota((SB_T, SB_T), 0)
        coli = _iota((SB_T, SB_T), 1)
        u_upto = _tri(SB_T, lambda r, c: r <= c)
        u_before = _tri(SB_T, lambda r, c: r < c)
        dqs = []
        for hh in range(2):
            hm = (lane >= HEAD_DIM * hh) & (lane < HEAD_DIM * (hh + 1))
            qm = jnp.where(hm, q, 0.0).astype(MXU)
            dom = jnp.where(hm, dov, 0.0).astype(MXU)
            tot = jnp.sum(jnp.where(lane == HEAD_DIM * hh, tv, 0.0), axis=1, keepdims=True)

            def kstep(kb, carry, qm=qm, dom=dom, tot=tot):
                psp, pg, dq = carry
                ks = pl.multiple_of(kb * SB_T, SB_T)
                kblk, z, mask, sp, spm = _sb_scores(qm, k_ref, ks, rowi, coli)
                vblk = v_ref[pl.ds(ks, SB_T), :].astype(MXU)
                after = tot - (psp + _dot3_l(spm, u_upto))
                la = z - sp - after
                a = jnp.where(mask, jnp.exp(la), 0.0)
                gm = _dot_nt(dom, vblk) * a
                before = pg + _dot3_l(gm, u_before)
                sg = jnp.exp(z - sp)
                dz = jnp.where(mask, gm * (1.0 - sg) - sg * before, 0.0) * SB_SCALE
                dzb = dz.astype(MXU)
                dq = dq + _dot(dzb, kblk)
                dk_ref[pl.ds(ks, SB_T), :] += _dot_tn(dzb, qm)
                dv_ref[pl.ds(ks, SB_T), :] += _dot_tn(a.astype(MXU), dom)
                return (psp + jnp.sum(spm, axis=1, keepdims=True),
                        pg + jnp.sum(gm, axis=1, keepdims=True), dq)

            zero1 = jnp.zeros((SB_T, 1), F32)
            dqs.append(lax.fori_loop(0, qb + 1, kstep, (zero1, zero1, jnp.zeros((SB_T, LANES), F32)))[2])
        dq_ref[...] = jnp.where(lane < HEAD_DIM, dqs[0], dqs[1]).astype(dq_ref.dtype)

    return pl.pallas_call(
        body,
        name="sb_bwd",
        out_shape=[
            jax.ShapeDtypeStruct((S, SB_WIDTH), MXU),
            jax.ShapeDtypeStruct((S, SB_WIDTH), F32),
            jax.ShapeDtypeStruct((S, SB_WIDTH), F32),
        ],
        grid=(SB_WIDTH // LANES, nq),
        in_specs=[
            pl.BlockSpec((SB_T, LANES), lambda h, i: (i, OFF_Q // LANES + h)),
            pl.BlockSpec((S, LANES), lambda h, i: (0, OFF_K // LANES + h)),
            pl.BlockSpec((S, LANES), lambda h, i: (0, OFF_V // LANES + h)),
            pl.BlockSpec((SB_T, LANES), lambda h, i: (i, h)),
            pl.BlockSpec((SB_T, LANES), lambda h, i: (i, h)),
        ],
        out_specs=[
            pl.BlockSpec((SB_T, LANES), lambda h, i: (i, h)),
            pl.BlockSpec((S, LANES), lambda h, i: (0, h)),
            pl.BlockSpec((S, LANES), lambda h, i: (0, h)),
        ],
        compiler_params=_cparams(("parallel", "arbitrary")),
    )(projmain, projmain, projmain, do, t_exp)


def local_step(x, pb, target, W, P):
    D = D_MODEL
    full = lambda a, w=D: (a, 0, w)

    (n1b,) = rowwise("norm_pre", lambda xv, g: (_rms(xv, g),), [full(x)], [full(P["norm_mix_pre"])],
                     [(D, D, MXU)])
    projmain = mm("in_proj", n1b, W["main"], "nn")
    dtraw = mm("dt_proj", n1b, W["dt"], "nn")
    xbc_act = conv_fwd(projmain, P["conv_w"], P["conv_b"])
    dte, cse = ssd_prep(dtraw, P["dt_bias_pad"], P["a_exp"])
    y_scan, states = ssd_fwd(xbc_act, dte, cse)

    def f_gate(ysc, xs, z, dsk, nw):
        return _rms((ysc + xs * dsk) * (z * _sig(z)), nw)

    (y_ssd_b,) = rowwise("ssd_gate", lambda *a: (f_gate(*a),),
                         [(y_scan, 0, GW), (xbc_act, 0, GW), (projmain, OFF_Z, GW)],
                         [(P["dsk_exp"], 0, GW), (P["ssd_norm"], 0, GW)], [(D_INNER, GW, MXU)], ncb=SSD_GROUPS)
    y_sb_b, t_exp = sb_fwd(projmain)
    u1 = mm("ssd_branch", y_ssd_b, W["ssd"], "nn")
    u2 = mm("sb_branch", y_sb_b, W["sb"], "nn")

    def f_merge(a1, a2, g1, g2, b1, b2):
        return (_sig(g1 + b1) * a1 + _sig(g2 + b2) * a2,)

    gate_rows = [(projmain, OFF_G, D), (projmain, OFF_G + D, D)]
    gate_bias = [(P["b_gate"], 0, D), (P["b_gate"], D, D)]
    (merged_b,) = rowwise("merge", f_merge, [full(u1), full(u2)] + gate_rows, gate_bias, [(D, D, MXU)])
    mo = mm("out_proj", merged_b, W["out"], "nn")

    def f_mid(xv, m, gpost, gffn):
        h1 = xv + _rms(m, gpost)
        return h1, _rms(h1, gffn)

    h1, n2b = rowwise("mix_post", f_mid, [full(x), full(mo)],
                      [full(P["norm_mix_post"]), full(P["norm_ffn_pre"])], [(D, D, F32), (D, D, MXU)])
    a_ff = mm("ff1", n2b, W["ff1"], "nn")
    (rb,) = rowwise("relu2", lambda a: (jnp.square(jnp.maximum(a, 0.0)),), [(a_ff, 0, D)], [],
                    [(D_FF, D, MXU)], ncb=D_FF // D)
    ff = mm("ff2", rb, W["ff2"], "nn")

    def f_ffn_post(h, f, g):
        h2 = h + _rms(f, g)
        return h2, h2

    h2, h2b = rowwise("ffn_post", f_ffn_post, [full(h1), full(ff)], [full(P["norm_ffn_post"])],
                      [(D, D, F32), (D, D, MXU)])
    pgp = mm("ple_gate", h2b, W["pg"], "nn")
    pe = mm("ple_proj", pb, W["ple"], "nn")

    def f_ple(h2v, gp, pev, tgt, g):
        f = lambda h, a, b, gg: h + _rms(_sig(a) * b, gg)
        h3, vjp = jax.vjp(f, h2v, gp, pev, g)
        err = h3 - tgt
        dh2, dgp, dpe, dg = vjp(err * (1.0 / D))
        lossc = (0.5 / D) * jnp.sum(err * err, axis=0, keepdims=True)
        return dh2, dgp, dpe, dg, lossc

    dh2a, dpgp_b, dpe_b, g_ple, lossc = rowwise(
        "ple_loss", f_ple, [full(h2), full(pgp), full(pe), full(target)], [full(P["norm_ple_post"])],
        [(D, D, F32), (D, D, MXU), (D, D, MXU)], [(D, D), (D, D)])
    dh2 = mm("d_ple_gate_x", dpgp_b, W["pg"], "nt", add=dh2a)
    gW = {}
    gW["w_ple_gate"] = mm("d_ple_gate_w", h2b, dpgp_b, "tn")
    gW["w_ple"] = mm("d_ple_w", pb, dpe_b, "tn")

    def b_ffn_post(d, f, g):
        _, vjp = jax.vjp(_rms, f, g)
        return vjp(d)

    dff_b, g_ffn_post = rowwise("d_ffn_post", b_ffn_post, [full(dh2), full(ff)], [full(P["norm_ffn_post"])],
                                [(D, D, MXU)], [(D, D)])
    dr = mm("d_ff2_x", dff_b, W["ff2"], "nt")
    gW["w_ff2"] = mm("d_ff2_w", rb, dff_b, "tn")
    (da_b,) = rowwise("d_relu2", lambda d, a: (d * (2.0 * jnp.maximum(a, 0.0)),), [(dr, 0, D), (a_ff, 0, D)], [],
                      [(D_FF, D, MXU)], ncb=D_FF // D)
    dn2 = mm("d_ff1_x", da_b, W["ff1"], "nt")
    gW["w_ff1"] = mm("d_ff1_w", n2b, da_b, "tn")

    def b_mid(d2, dn, h, m, gpost, gffn):
        _, vjp = jax.vjp(_rms, h, gffn)
        dh, dgffn = vjp(dn)
        dh1 = d2 + dh
        _, vjp2 = jax.vjp(_rms, m, gpost)
        dm, dgpost = vjp2(dh1)
        return dh1, dm, dgpost, dgffn

    dh1, dmo_b, g_mix_post, g_ffn_pre = rowwise(
        "d_mix_post", b_mid, [full(dh2), full(dn2), full(h1), full(mo)],
        [full(P["norm_mix_post"]), full(P["norm_ffn_pre"])], [(D, D, F32), (D, D, MXU)], [(D, D), (D, D)])
    dmerged = mm("d_out_x", dmo_b, W["out"], "nt")
    gW["w_out"] = mm("d_out_w", merged_b, dmo_b, "tn")

    def b_merge(d, a1, a2, g1, g2, b1, b2):
        s1, s2 = _sig(g1 + b1), _sig(g2 + b2)
        dg1 = d * a1 * s1 * (1.0 - s1)
        dg2 = d * a2 * s2 * (1.0 - s2)
        return (d * s1, d * s2, dg1, dg2,
                jnp.sum(dg1, axis=0, keepdims=True), jnp.sum(dg2, axis=0, keepdims=True))

    du1_b, du2_b, dgp1_b, dgp2_b, g_bg1, g_bg2 = rowwise(
        "d_merge", b_merge, [full(dmerged), full(u1), full(u2)] + gate_rows, gate_bias,
        [(D, D, MXU)] * 4, [(D, D), (D, D)])
    dy_ssd = mm("d_ssd_branch_x", du1_b, W["ssd"], "nt")
    dy_sb = mm("d_sb_branch_x", du2_b, W["sb"], "nt")
    gW["w_ssd_branch"] = mm("d_ssd_branch_w", y_ssd_b, du1_b, "tn")
    gW["w_sb_branch"] = mm("d_sb_branch_w", y_sb_b, du2_b, "tn")

    def b_gate(d, ysc, xs, z, dsk, nw):
        _, vjp = jax.vjp(f_gate, ysc, xs, z, dsk, nw)
        return vjp(d)

    dy_scan, dxs_skip, dz_b, g_dsk_exp, g_ssd_norm = rowwise(
        "d_ssd_gate", b_gate, [(dy_ssd, 0, GW), (y_scan, 0, GW), (xbc_act, 0, GW), (projmain, OFF_Z, GW)],
        [(P["dsk_exp"], 0, GW), (P["ssd_norm"], 0, GW)],
        [(D_INNER, GW, F32), (D_INNER, GW, F32), (D_INNER, GW, MXU)], [(D_INNER, GW), (D_INNER, GW)],
        ncb=SSD_GROUPS)
    dxs, d_b, d_c, ddt_part, g_a_exp = ssd_bwd(dy_scan, xbc_act, dte, cse, states, P["a_exp"])

    def b_dt(dpart, dtr, bias):
        ddt = _dot3_l(dpart, _head_reduce())
        d = ddt * _sig(dtr + bias)
        return d, jnp.sum(d, axis=0, keepdims=True)

    ddt_b, g_dt_bias_pad = rowwise("d_dt", b_dt, [(ddt_part, 0, D_INNER), (dtraw, 0, LANES)],
                                   [(P["dt_bias_pad"], 0, LANES)], [(LANES, LANES, MXU)], [(LANES, LANES)])
    dxbc_act = jnp.concatenate([dxs + dxs_skip, d_b, d_c], axis=1)
    dxbc_b, g_conv_w, g_conv_b = conv_bwd(dxbc_act, projmain, P["conv_w"], P["conv_b"])
    dq_b, dk, dv = sb_bwd(projmain, dy_sb, t_exp)
    dmain_b = jnp.concatenate([dz_b, dxbc_b, dq_b, dk.astype(MXU), dv.astype(MXU), dgp1_b, dgp2_b], axis=1)
    dn1_dt = mm("d_dt_x", ddt_b, W["dt"], "nt")
    dn1 = mm("d_in_x", dmain_b, W["main"], "nt", add=dn1_dt)
    g_main = mm("d_in_w", n1b, dmain_b, "tn")
    g_dt = mm("d_dt_w", n1b, ddt_b, "tn")

    def b_pre(d1, dn, xv, g):
        _, vjp = jax.vjp(_rms, xv, g)
        dx, dg = vjp(dn)
        return d1 + dx, dg

    grad_x, g_mix_pre = rowwise("d_norm_pre", b_pre, [full(dh1), full(dn1), full(x)], [full(P["norm_mix_pre"])],
                                [(D, D, F32)], [(D, D)])

    gW["w_in"] = jnp.concatenate([g_main[:, :OFF_Q], g_dt[:, :SSD_HEADS], g_main[:, OFF_Q:OFF_G]], axis=1)
    gW["w_gate"] = g_main[:, OFF_G:]
    gW["conv_w"] = g_conv_w
    gS = {
        "norm_mix_pre": g_mix_pre, "conv_b": g_conv_b, "dt_bias_pad": g_dt_bias_pad, "a_exp": g_a_exp,
        "dsk_exp": g_dsk_exp, "ssd_norm": g_ssd_norm, "b_gate": jnp.concatenate([g_bg1, g_bg2], axis=1),
        "norm_mix_post": g_mix_post, "norm_ffn_pre": g_ffn_pre, "norm_ffn_post": g_ffn_post,
        "norm_ple_post": g_ple,
    }
    return lossc, grad_x, gW, gS


IN_SPLITS = (2048, 6144, 6176, 7200, 8224)
SHARDED = (
    ("w_in", (1024, 1156), 1), ("conv_w", (4, 512), 1), ("w_ssd_branch", (256, 1024), 0),
    ("w_sb_branch", (128, 1024), 0), ("w_gate", (1024, 256), 1), ("w_out", (128, 1024), 0),
    ("w_ff1", (1024, 512), 1), ("w_ff2", (512, 1024), 0), ("w_ple", (256, 128), 1),
    ("w_ple_gate", (128, 1024), 0),
)
SMALL = (
    ("norm_mix_pre", 1024), ("conv_b", 4096), ("dt_bias", 32), ("a_log", 32), ("d_skip", 32), ("ssd_norm", 2048),
    ("b_gate", 2048), ("norm_mix_post", 1024), ("norm_ffn_pre", 1024), ("norm_ffn_post", 1024),
    ("norm_ple_post", 1024),
)
ROW = 1024
FLAT_ROWS = 3200
SMALL_ROWS = 16


def _rows_of(n):
    return -(-n // ROW)


def build_matrices(wfull):
    w_in = wfull["w_in"]
    z, xbc, dtw, q, k, v = [w_in[:, a:b] for a, b in zip((0,) + IN_SPLITS, IN_SPLITS + (w_in.shape[1],))]
    c = lambda a: a.astype(MXU)
    return {
        "main": c(jnp.concatenate([z, xbc, q, k, v, wfull["w_gate"]], axis=1)),
        "dt": c(jnp.pad(dtw, ((0, 0), (0, LANES - SSD_HEADS)))),
        "ssd": c(wfull["w_ssd_branch"]), "sb": c(wfull["w_sb_branch"]), "out": c(wfull["w_out"]),
        "ff1": c(wfull["w_ff1"]), "ff2": c(wfull["w_ff2"]), "ple": c(wfull["w_ple"]), "pg": c(wfull["w_ple_gate"]),
    }


def build_small(small, conv_w_full):
    P = {k: small[k] for k in ("norm_mix_pre", "conv_b", "ssd_norm", "b_gate", "norm_mix_post", "norm_ffn_pre",
                               "norm_ffn_post", "norm_ple_post")}
    P["conv_w"] = conv_w_full
    P["dt_bias_pad"] = jnp.pad(small["dt_bias"], ((0, 0), (0, LANES - SSD_HEADS)))
    P["a_exp"] = jnp.repeat(-jnp.exp(small["a_log"]), HEAD_DIM, axis=1)
    P["dsk_exp"] = jnp.repeat(small["d_skip"], HEAD_DIM, axis=1)
    return P


def small_grads(gS, small):
    heads = lambda a: a.reshape(SSD_HEADS, HEAD_DIM).sum(axis=1)[None, :]
    out = {k: gS[k] for k in ("norm_mix_pre", "conv_b", "ssd_norm", "b_gate", "norm_mix_post", "norm_ffn_pre",
                              "norm_ffn_post", "norm_ple_post")}
    out["dt_bias"] = gS["dt_bias_pad"][:, :SSD_HEADS]
    out["a_log"] = heads(gS["a_exp"]) * (-jnp.exp(small["a_log"]))
    out["d_skip"] = heads(gS["dsk_exp"])
    return out


def _pad_to(v, n, axis=-1):
    pad = [(0, 0)] * v.ndim
    pad[axis] = (0, n - v.shape[axis])
    return jnp.pad(v, pad)


def pack_shards(get, dtype):
    cols = []
    for name, shape, _ in SHARDED:
        n = shape[0] * shape[1]
        cols.append(_pad_to(get(name).reshape(n), _rows_of(n) * ROW))
    return _pad_to(jnp.concatenate(cols), FLAT_ROWS * ROW).reshape(FLAT_ROWS, ROW).astype(dtype)


def unpack_shards(flat):
    out, r0 = {}, 0
    for name, shape, _ in SHARDED:
        n = shape[0] * shape[1]
        rows = _rows_of(n)
        out[name] = flat[r0:r0 + rows].reshape(rows * ROW)[:n].reshape((1,) + shape)
        r0 += rows
    return out


def full_from_gathered(gathered):
    out, r0 = {}, 0
    for name, shape, axis in SHARDED:
        n = shape[0] * shape[1]
        rows = _rows_of(n)
        blk = gathered[:, r0:r0 + rows].reshape(N_DEV, rows * ROW)[:, :n].reshape((N_DEV,) + shape)
        if axis == 0:
            out[name] = blk.reshape(N_DEV * shape[0], shape[1])
        else:
            out[name] = blk.transpose(1, 0, 2).reshape(shape[0], N_DEV * shape[1])
        r0 += rows
    return out


def slabs_from_full(grads):
    cols = []
    for name, shape, axis in SHARDED:
        g = grads[name]
        if axis == 0:
            blk = g.reshape(N_DEV, shape[0] * shape[1])
        else:
            blk = g.reshape(shape[0], N_DEV, shape[1]).transpose(1, 0, 2).reshape(N_DEV, shape[0] * shape[1])
        cols.append(_pad_to(blk, _rows_of(blk.shape[1]) * ROW))
    return _pad_to(jnp.concatenate(cols, axis=1), FLAT_ROWS * ROW).reshape(N_DEV, FLAT_ROWS, ROW)


def pack_small(get):
    cols = [_pad_to(get(name).reshape(n), _rows_of(n) * ROW) for name, n in SMALL]
    return jnp.concatenate(cols).reshape(SMALL_ROWS, ROW)


def unpack_small(flat):
    out, r0 = {}, 0
    for name, n in SMALL:
        rows = _rows_of(n)
        out[name] = flat[r0:r0 + rows].reshape(rows * ROW)[:n].reshape(1, n)
        r0 += rows
    return out


def exchange(name, srcs, scatter):
    n = len(srcs)
    out_shape = [jax.ShapeDtypeStruct(s.shape if sc else (N_DEV,) + s.shape, s.dtype)
                 for s, sc in zip(srcs, scatter)]

    def body(*refs):
        src, out = refs[:n], refs[n:2 * n]
        send_sems, recv_sems, local_sems = refs[2 * n:]
        x, y, c = lax.axis_index("x"), lax.axis_index("y"), lax.axis_index("c")
        index = lambda d: 4 * d[0] + 2 * d[1] + d[2]
        mine = index((x, y, c))
        peers = [(1 - x if k & 4 else x, 1 - y if k & 2 else y, 1 - c if k & 1 else c) for k in range(1, N_DEV)]

        def part(i, dev_index):
            return src[i].at[dev_index] if scatter[i] else src[i]

        def remote(i, k, src_ref, slot):
            s = k * n + i
            return pltpu.make_async_remote_copy(
                src_ref=src_ref, dst_ref=out[i].at[slot], send_sem=send_sems.at[s], recv_sem=recv_sems.at[s],
                device_id=peers[k], device_id_type=pl.DeviceIdType.MESH)

        local = [pltpu.make_async_copy(part(i, mine), out[i].at[mine], local_sems.at[i]) for i in range(n)]
        for cp in local:
            cp.start()
        sends = [remote(i, k, part(i, index(peers[k])), mine) for k in range(N_DEV - 1) for i in range(n)]
        for cp in sends:
            cp.start()
        for k in range(N_DEV - 1):
            for i in range(n):
                remote(i, k, part(i, mine), index(peers[k])).wait_recv()
        for cp in sends:
            cp.wait_send()
        for cp in local:
            cp.wait()

    any_spec = pl.BlockSpec(memory_space=pl.ANY)
    return pl.pallas_call(
        body,
        name=name,
        out_shape=out_shape,
        in_specs=[any_spec] * n,
        out_specs=[any_spec] * n,
        scratch_shapes=[
            pltpu.SemaphoreType.DMA(((N_DEV - 1) * n,)),
            pltpu.SemaphoreType.DMA(((N_DEV - 1) * n,)),
            pltpu.SemaphoreType.DMA((n,)),
        ],
    )(*srcs)


def adamw(name, parts, w, m, v, tr):
    rows = w.shape[0]

    def body(p_ref, w_ref, m_ref, v_ref, g_ref, d_ref, m2_ref, v2_ref):
        g = p_ref[0]
        for k in range(1, N_DEV):
            g = g + p_ref[k]
        m2 = ADAM_B1 * m_ref[...] + (1.0 - ADAM_B1) * g
        v2 = ADAM_B2 * v_ref[...] + (1.0 - ADAM_B2) * jnp.square(g)
        m_hat = m2 / (1.0 - ADAM_B1 ** ADAM_STEP)
        v_hat = v2 / (1.0 - ADAM_B2 ** ADAM_STEP)
        g_ref[...] = g
        d_ref[...] = -ADAM_LR * (m_hat / (jnp.sqrt(v_hat) + ADAM_EPS) + ADAM_WD * w_ref[...])
        m2_ref[...] = m2
        v2_ref[...] = v2

    spec = pl.BlockSpec((tr, ROW), lambda i: (i, 0))
    return pl.pallas_call(
        body,
        name=name,
        out_shape=[jax.ShapeDtypeStruct((rows, ROW), F32)] * 4,
        grid=(rows // tr,),
        in_specs=[pl.BlockSpec((N_DEV, tr, ROW), lambda i: (0, i, 0)), spec, spec, spec],
        out_specs=[spec] * 4,
        compiler_params=_cparams(("parallel",)),
    )(parts, w, m, v)


WEIGHT_ORDER = (
    "norm_mix_pre", "w_in", "conv_w", "conv_b", "dt_bias", "a_log", "d_skip", "ssd_norm", "w_ssd_branch",
    "w_sb_branch", "w_gate", "b_gate", "w_out", "norm_mix_post", "norm_ffn_pre", "w_ff1", "w_ff2", "norm_ffn_post",
    "w_ple", "w_ple_gate", "norm_ple_post",
)


def kernel(x, p, norm_mix_pre, w_in, conv_w, conv_b, dt_bias, a_log, d_skip, ssd_norm, w_ssd_branch, w_sb_branch, w_gate, b_gate, w_out, norm_mix_post, norm_ffn_pre, w_ff1, w_ff2, norm_ffn_post, w_ple, w_ple_gate, norm_ple_post, loss_target, m_norm_mix_pre, m_w_in, m_conv_w, m_conv_b, m_dt_bias, m_a_log, m_d_skip, m_ssd_norm, m_w_ssd_branch, m_w_sb_branch, m_w_gate, m_b_gate, m_w_out, m_norm_mix_post, m_norm_ffn_pre, m_w_ff1, m_w_ff2, m_norm_ffn_post, m_w_ple, m_w_ple_gate, m_norm_ple_post, v_norm_mix_pre, v_w_in, v_conv_w, v_conv_b, v_dt_bias, v_a_log, v_d_skip, v_ssd_norm, v_w_ssd_branch, v_w_sb_branch, v_w_gate, v_b_gate, v_w_out, v_norm_mix_post, v_norm_ffn_pre, v_w_ff1, v_w_ff2, v_norm_ffn_post, v_w_ple, v_w_ple_gate, v_norm_ple_post):
    a = dict(locals())
    seq = x.shape[1]
    x2 = x.reshape(seq, D_MODEL)
    target = loss_target.reshape(seq, D_MODEL)
    pb = p.reshape(seq, PLE_DIM).astype(MXU)

    conv_rows = _pad_to(a["conv_w"].reshape(2, ROW), 8, axis=0)
    gathered, conv_gathered = exchange("gather_weights", [pack_shards(lambda n: a[n], MXU), conv_rows],
                                       [False, False])
    wfull = full_from_gathered(gathered)
    conv_w_full = conv_gathered[:, :2].reshape(N_DEV, CONV_K, CONV_DIM // N_DEV).transpose(1, 0, 2)
    conv_w_full = conv_w_full.reshape(CONV_K, CONV_DIM)
    small = {n: a[n] for n, _ in SMALL}

    lossc, grad_x, g_full, g_acc = local_step(x2, pb, target, build_matrices(wfull),
                                              build_small(small, conv_w_full))
    loss = lax.psum(jnp.sum(lossc), ("x", "y", "c"))
    g_small = small_grads(g_acc, small)

    parts, small_parts = exchange("scatter_grads", [slabs_from_full(g_full), pack_small(lambda n: g_small[n])],
                                  [True, False])

    res = adamw("adamw", parts, pack_shards(lambda n: a[n], F32), pack_shards(lambda n: a["m_" + n], F32),
                pack_shards(lambda n: a["v_" + n], F32), tr=128)
    res_small = adamw("adamw_small", small_parts, pack_small(lambda n: a[n]), pack_small(lambda n: a["m_" + n]),
                      pack_small(lambda n: a["v_" + n]), tr=SMALL_ROWS)
    outs = [loss, grad_x.reshape(x.shape)]
    for big, little in zip(res, res_small):
        leaves = {**unpack_shards(big), **unpack_small(little)}
        outs += [leaves[n] for n in WEIGHT_ORDER]
    return tuple(outs)
```

```python
import functools

import jax
import jax.numpy as jnp
from jax import lax
from jax.experimental import pallas as pl
from jax.experimental.pallas import tpu as pltpu

F32 = jnp.float32
MXU = jnp.bfloat16
VMEM_LIMIT = 56 * 1024 * 1024

D_MODEL = 1024
D_INNER = 2048
SSD_HEADS = 32
HEAD_DIM = 64
SSD_GROUPS = 8
D_STATE = 128
CONV_K = 4
CONV_DIM = 4096
CHUNK = 128
SB_WIDTH = 1024
D_FF = 4096
PLE_DIM = 256
RMS_EPS = 1e-6
SB_SCALE = HEAD_DIM ** -0.5
N_DEV = 8
LANES = 128

OFF_Z, OFF_XBC, OFF_Q, OFF_K, OFF_V = 0, 2048, 6144, 7168, 8192

ADAM_LR = 0.001
ADAM_B1 = 0.9
ADAM_B2 = 0.999
ADAM_EPS = 1e-08
ADAM_WD = 0.01
ADAM_STEP = 10


def _sig(x):
    return 1.0 / (1.0 + jnp.exp(-x))


def _softplus(x):
    return jnp.maximum(x, 0.0) + jnp.log(1.0 + jnp.exp(-jnp.abs(x)))


def _rms(x, w):
    return x * lax.rsqrt(jnp.mean(x * x, axis=-1, keepdims=True) + RMS_EPS) * w


def _dot(a, b):
    return jnp.dot(a, b, preferred_element_type=F32)


def _dot_nt(a, b):
    return lax.dot_general(a, b, (((1,), (1,)), ((), ())), preferred_element_type=F32)


def _dot_tn(a, b):
    return lax.dot_general(a, b, (((0,), (0,)), ((), ())), preferred_element_type=F32)


def _split3(x):
    x1 = x.astype(MXU)
    r = x - x1.astype(F32)
    x2 = r.astype(MXU)
    r = r - x2.astype(F32)
    return x1, x2, r.astype(MXU)


def _dot3_l(a, u):
    a1, a2, a3 = _split3(a)
    return (_dot(a3, u) + _dot(a2, u)) + _dot(a1, u)


def _dot3_r(u, a):
    a1, a2, a3 = _split3(a)
    return (_dot(u, a3) + _dot(u, a2)) + _dot(u, a1)


def _iota(shape, dim):
    return lax.broadcasted_iota(jnp.int32, shape, dim)


def _tri(n, cmp):
    r, c = _iota((n, n), 0), _iota((n, n), 1)
    return cmp(r, c).astype(F32).astype(MXU)


def _cparams(sem):
    return pltpu.CompilerParams(dimension_semantics=sem, vmem_limit_bytes=VMEM_LIMIT)


def _pick(n, cands):
    for c in cands:
        if n % c == 0:
            return c
    return n


def mm(name, a, b, mode, add=None, out_dtype=F32, b_slabs=False, out_slabs=False):
    slab = None
    if b_slabs:
        slab = b.shape[2]
        bshape = (b.shape[1], N_DEV * slab)
    else:
        bshape = b.shape
    if mode == "nn":
        (M, K), (K2, N) = a.shape, bshape
    elif mode == "nt":
        (M, K), (N, K2) = a.shape, bshape
    else:
        (K, M), (K2, N) = a.shape, bshape
    assert K == K2, (name, a.shape, b.shape)
    tm = _pick(M, (1024, 512, 256, 128))
    tn = _pick(N, (1024, 512, 256, 128))
    tk = _pick(K, (1024, 512, 256, 128))
    if b_slabs and mode == "nn":
        tn = slab
    if b_slabs and mode == "nt":
        tk = slab
    if out_slabs:
        assert mode == "tn" and N % N_DEV == 0
        tn = N // N_DEV
    nk = K // tk

    def body(*refs):
        if add is None:
            a_ref, b_ref, o_ref, acc = refs
        else:
            a_ref, b_ref, add_ref, o_ref, acc = refs
        k = pl.program_id(2)

        @pl.when(k == 0)
        def _():
            acc[...] = jnp.zeros_like(acc) if add is None else add_ref[...]

        av, bv = a_ref[...], b_ref[...]
        if mode == "nn":
            acc[...] += _dot(av, bv)
        elif mode == "nt":
            acc[...] += _dot_nt(av, bv)
        else:
            acc[...] += _dot_tn(av, bv)

        @pl.when(k == nk - 1)
        def _():
            o_ref[...] = acc[...].astype(o_ref.dtype)

    if mode == "nn":
        a_spec = pl.BlockSpec((tm, tk), lambda i, j, k: (i, k))
        b_spec = pl.BlockSpec((tk, tn), lambda i, j, k: (k, j))
    elif mode == "nt":
        a_spec = pl.BlockSpec((tm, tk), lambda i, j, k: (i, k))
        b_spec = pl.BlockSpec((tn, tk), lambda i, j, k: (j, k))
    else:
        a_spec = pl.BlockSpec((tk, tm), lambda i, j, k: (k, i))
        b_spec = pl.BlockSpec((tk, tn), lambda i, j, k: (k, j))
    if b_slabs and mode == "nn":
        b_spec = pl.BlockSpec((None, tk, tn), lambda i, j, k: (j, k, 0))
    if b_slabs and mode == "nt":
        b_spec = pl.BlockSpec((None, tn, tk), lambda i, j, k: (k, j, 0))
    o_spec = pl.BlockSpec((tm, tn), lambda i, j, k: (i, j))
    in_specs, args = [a_spec, b_spec], [a, b]
    if add is not None:
        in_specs.append(o_spec)
        args.append(add)
    out_sds = jax.ShapeDtypeStruct((M, N), out_dtype)
    if out_slabs:
        o_spec = pl.BlockSpec((None, tm, tn), lambda i, j, k: (j, i, 0))
        out_sds = jax.ShapeDtypeStruct((N_DEV, M, tn), out_dtype)
    return pl.pallas_call(
        body,
        name=name,
        out_shape=out_sds,
        grid=(M // tm, N // tn, nk),
        in_specs=in_specs,
        out_specs=o_spec,
        scratch_shapes=[pltpu.VMEM((tm, tn), F32)],
        compiler_params=_cparams(("parallel", "parallel", "arbitrary")),
    )(*args)


def rowwise(name, fn, rows, bcast, outs, accs=(), tr=256, ncb=1):
    S = rows[0][0].shape[0]
    tr = min(tr, S)
    nrb = S // tr
    in_specs, args = [], []
    for arr, off, w in rows:
        assert off % w == 0 and arr.shape[0] == S
        in_specs.append(pl.BlockSpec((tr, w), lambda j, i, ob=off // w: (i, ob + j)))
        args.append(arr)
    for arr, off, w in bcast:
        assert off % w == 0
        in_specs.append(pl.BlockSpec((arr.shape[0], w), lambda j, i, ob=off // w: (0, ob + j)))
        args.append(arr)
    out_shape, out_specs = [], []
    for tw, w, dt in outs:
        out_shape.append(jax.ShapeDtypeStruct((S, tw), dt))
        out_specs.append(pl.BlockSpec((tr, w), lambda j, i: (i, j)))
    for tw, w in accs:
        out_shape.append(jax.ShapeDtypeStruct((1, tw), F32))
        out_specs.append(pl.BlockSpec((1, w), lambda j, i: (0, j)))
    nin, nout = len(args), len(outs)

    def body(*refs):
        res = fn(*[r[...] for r in refs[:nin]])
        o_refs, a_refs = refs[nin:nin + nout], refs[nin + nout:]
        for r, v in zip(o_refs, res[:nout]):
            r[...] = v.astype(r.dtype)
        i = pl.program_id(1)
        for r, v in zip(a_refs, res[nout:]):
            @pl.when(i == 0)
            def _(r=r, v=v):
                r[...] = v

            @pl.when(i > 0)
            def _(r=r, v=v):
                r[...] += v

    res = pl.pallas_call(
        body,
        name=name,
        out_shape=out_shape,
        grid=(ncb, nrb),
        in_specs=in_specs,
        out_specs=out_specs,
        compiler_params=_cparams(("parallel", "arbitrary")),
    )(*args)
    return res


CONV_TC = 128


def _conv_pre(u, w, b, row):
    pre = b + w[3:4, :] * u
    shifted = []
    for s in (1, 2, 3):
        us = jnp.where(row >= s, pltpu.roll(u, s, 0), 0.0)
        shifted.append(us)
        pre = pre + w[3 - s:4 - s, :] * us
    return pre, shifted


def conv_fwd(projmain, conv_w, conv_b):
    S = projmain.shape[0]
    tc = CONV_TC

    def body(u_ref, w_ref, b_ref, o_ref):
        u = u_ref[...]
        row = _iota(u.shape, 0)
        pre, _ = _conv_pre(u, w_ref[...], b_ref[...], row)
        o_ref[...] = pre * _sig(pre)

    return pl.pallas_call(
        body,
        name="conv_fwd",
        out_shape=jax.ShapeDtypeStruct((S, CONV_DIM), F32),
        grid=(CONV_DIM // tc,),
        in_specs=[
            pl.BlockSpec((S, tc), lambda j: (0, OFF_XBC // tc + j)),
            pl.BlockSpec((CONV_K, tc), lambda j: (0, j)),
            pl.BlockSpec((1, tc), lambda j: (0, j)),
        ],
        out_specs=pl.BlockSpec((S, tc), lambda j: (0, j)),
        compiler_params=_cparams(("parallel",)),
    )(projmain, conv_w, conv_b)


def conv_bwd(dact, projmain, conv_w, conv_b):
    S = projmain.shape[0]
    tc = CONV_TC

    def body(d_ref, u_ref, w_ref, b_ref, du_ref, dw_ref, db_ref):
        u = u_ref[...]
        w = w_ref[...]
        row = _iota(u.shape, 0)
        pre, shifted = _conv_pre(u, w, b_ref[...], row)
        sg = _sig(pre)
        dpre = d_ref[...] * (sg * (1.0 + pre * (1.0 - sg)))
        du = w[3:4, :] * dpre
        dw_ref[3:4, :] = jnp.sum(dpre * u, axis=0, keepdims=True)
        for s in (1, 2, 3):
            ds = jnp.where(row < S - s, pltpu.roll(dpre, S - s, 0), 0.0)
            du = du + w[3 - s:4 - s, :] * ds
            dw_ref[3 - s:4 - s, :] = jnp.sum(dpre * shifted[s - 1], axis=0, keepdims=True)
        du_ref[...] = du.astype(du_ref.dtype)
        db_ref[...] = jnp.sum(dpre, axis=0, keepdims=True)

    return pl.pallas_call(
        body,
        name="conv_bwd",
        out_shape=[
            jax.ShapeDtypeStruct((S, CONV_DIM), MXU),
            jax.ShapeDtypeStruct((CONV_K, CONV_DIM), F32),
            jax.ShapeDtypeStruct((1, CONV_DIM), F32),
        ],
        grid=(CONV_DIM // tc,),
        in_specs=[
            pl.BlockSpec((S, tc), lambda j: (0, j)),
            pl.BlockSpec((S, tc), lambda j: (0, OFF_XBC // tc + j)),
            pl.BlockSpec((CONV_K, tc), lambda j: (0, j)),
            pl.BlockSpec((1, tc), lambda j: (0, j)),
        ],
        out_specs=[
            pl.BlockSpec((S, tc), lambda j: (0, j)),
            pl.BlockSpec((CONV_K, tc), lambda j: (0, j)),
            pl.BlockSpec((1, tc), lambda j: (0, j)),
        ],
        compiler_params=_cparams(("parallel",)),
    )(dact, projmain, conv_w, conv_b)


def _head_expand():
    r, j = _iota((LANES, D_INNER), 0), _iota((LANES, D_INNER), 1)
    return ((j >= r * HEAD_DIM) & (j < r * HEAD_DIM + HEAD_DIM)).astype(F32).astype(MXU)


def _head_reduce():
    j, r = _iota((D_INNER, LANES), 0), _iota((D_INNER, LANES), 1)
    return ((j >= r * HEAD_DIM) & (j < r * HEAD_DIM + HEAD_DIM)).astype(F32).astype(MXU)


def ssd_prep(dtraw, dt_bias_pad, a_exp):
    S = dtraw.shape[0]

    def body(dtr_ref, bias_ref, a_ref, dte_ref, cse_ref):
        dt = _softplus(dtr_ref[...] + bias_ref[...])
        dte = _dot3_l(dt, _head_expand())
        dte_ref[...] = dte
        incl = _tri(CHUNK, lambda r, c: r >= c)
        cse_ref[...] = _dot3_r(incl, dte * a_ref[...])

    return pl.pallas_call(
        body,
        name="ssd_prep",
        out_shape=[jax.ShapeDtypeStruct((S, D_INNER), F32)] * 2,
        grid=(S // CHUNK,),
        in_specs=[
            pl.BlockSpec((CHUNK, LANES), lambda c: (c, 0)),
            pl.BlockSpec((1, LANES), lambda c: (0, 0)),
            pl.BlockSpec((1, D_INNER), lambda c: (0, 0)),
        ],
        out_specs=[pl.BlockSpec((CHUNK, D_INNER), lambda c: (c, 0))] * 2,
        compiler_params=_cparams(("parallel",)),
    )(dtraw, dt_bias_pad, a_exp)


GW = 4 * HEAD_DIM


def ssd_fwd(xbc_act, dte, cse):
    S = xbc_act.shape[0]
    nc = S // CHUNK

    def body(xs_ref, b_ref, c_ref, dte_ref, cse_ref, y_ref, st_ref, s_scr):
        c = pl.program_id(1)

        @pl.when(c == 0)
        def _():
            s_scr[...] = jnp.zeros_like(s_scr)

        s_in = s_scr[...]
        st_ref[0] = s_in
        cs = cse_ref[...]
        xd = xs_ref[...] * dte_ref[...]
        bm, cm = b_ref[...], c_ref[...]
        bb, cb = bm.astype(MXU), cm.astype(MXU)
        cs_last = cs[CHUNK - 1:CHUNK, :]
        g = _dot_nt(cb, bb)
        cs_t = cs.T
        yoff = _dot(cb, s_in.astype(MXU)) * jnp.exp(cs)
        row, col = _iota((CHUNK, CHUNK), 0), _iota((CHUNK, CHUNK), 1)
        for p in range(2):
            sl = slice(LANES * p, LANES * (p + 1))
            xdp = xd[:, sl].astype(MXU)
            yd = []
            for hh in range(2):
                h = 2 * p + hh
                lo = HEAD_DIM * h
                lam = jnp.where(row >= col, jnp.exp(cs[:, lo:lo + 1] - cs_t[lo:lo + 1, :]), 0.0)
                yd.append(_dot((g * lam).astype(MXU), xdp))
            y_ref[:, sl] = yoff[:, sl] + jnp.where(col < HEAD_DIM, yd[0], yd[1])
        w = (xd * jnp.exp(cs_last - cs)).astype(MXU)
        s_scr[...] = jnp.exp(cs_last) * s_in + _dot(bm.T.astype(MXU), w)

    return pl.pallas_call(
        body,
        name="ssd_fwd",
        out_shape=[
            jax.ShapeDtypeStruct((S, D_INNER), F32),
            jax.ShapeDtypeStruct((nc, D_STATE, D_INNER), F32),
        ],
        grid=(SSD_GROUPS, nc),
        in_specs=[
            pl.BlockSpec((CHUNK, GW), lambda g, c: (c, g)),
            pl.BlockSpec((CHUNK, D_STATE), lambda g, c: (c, D_INNER // D_STATE + g)),
            pl.BlockSpec((CHUNK, D_STATE), lambda g, c: (c, D_INNER // D_STATE + SSD_GROUPS + g)),
            pl.BlockSpec((CHUNK, GW), lambda g, c: (c, g)),
            pl.BlockSpec((CHUNK, GW), lambda g, c: (c, g)),
        ],
        out_specs=[
            pl.BlockSpec((CHUNK, GW), lambda g, c: (c, g)),
            pl.BlockSpec((1, D_STATE, GW), lambda g, c: (c, 0, g)),
        ],
        scratch_shapes=[pltpu.VMEM((D_STATE, GW), F32)],
        compiler_params=_cparams(("parallel", "arbitrary")),
    )(xbc_act, xbc_act, xbc_act, dte, cse)


def ssd_bwd(dy, xbc_act, dte, cse, states, a_exp):
    S = xbc_act.shape[0]
    nc = S // CHUNK

    def body(dy_ref, xs_ref, b_ref, c_ref, dte_ref, cse_ref, sin_ref, sout_ref, a_ref,
             dxs_ref, db_ref, dc_ref, ddt_ref, dal_ref, ds_scr, dxd_scr, dcs_scr):
        j = pl.program_id(1)

        @pl.when(j == 0)
        def _():
            ds_scr[...] = jnp.zeros_like(ds_scr)
            dal_ref[...] = jnp.zeros_like(dal_ref)

        ds_out = ds_scr[...]
        dyv, xs = dy_ref[...], xs_ref[...]
        dt, cs = dte_ref[...], cse_ref[...]
        s_in = sin_ref[0]
        bm, cm = b_ref[...], c_ref[...]
        bb, cb = bm.astype(MXU), cm.astype(MXU)
        dsb = ds_out.astype(MXU)
        xd = xs * dt
        ecs = jnp.exp(cs)
        cs_last = cs[CHUNK - 1:CHUNK, :]
        eend = jnp.exp(cs_last - cs)
        g = _dot_nt(cb, bb)
        g_t = _dot_nt(bb, cb)
        cs_t = cs.T
        dxd_off = _dot(bb, dsb) * eend
        yoff = _dot(cb, s_in.astype(MXU)) * ecs
        row, col = _iota((CHUNK, CHUNK), 0), _iota((CHUNK, CHUNK), 1)
        dg = jnp.zeros((CHUNK, CHUNK), F32)
        for p in range(2):
            sl = slice(LANES * p, LANES * (p + 1))
            dyp = dyv[:, sl]
            xdpb = xd[:, sl].astype(MXU)
            acc = dxd_off[:, sl]
            dcs_p = jnp.zeros((CHUNK, LANES), F32)
            for hh in range(2):
                h = 2 * p + hh
                lo = HEAD_DIM * h
                hm = (col >= HEAD_DIM * hh) & (col < HEAD_DIM * (hh + 1))
                cs_col, cs_row = cs[:, lo:lo + 1], cs_t[lo:lo + 1, :]
                lam = jnp.where(row >= col, jnp.exp(cs_col - cs_row), 0.0)
                lam_t = jnp.where(col >= row, jnp.exp(cs_row - cs_col), 0.0)
                dym = jnp.where(hm, dyp, 0.0).astype(MXU)
                m_t = g_t * lam_t
                acc = acc + _dot(m_t.astype(MXU), dym)
                dm = _dot_nt(dym, xdpb)
                dm_t = _dot_nt(xdpb, dym)
                dg = dg + dm * lam
                wdiff = (jnp.sum(dm * (g * lam), axis=1, keepdims=True)
                         - jnp.sum(dm_t * m_t, axis=1, keepdims=True))
                dcs_p = dcs_p + jnp.where(col == HEAD_DIM * hh, wdiff, 0.0)
            dxd_scr[:, sl] = acc
            dcs_scr[:, sl] = dcs_p
        dxd = dxd_scr[...]
        dgb = dg.astype(MXU)
        dye = (dyv * ecs).astype(MXU)
        dc_ref[...] = _dot(dgb, bb) + _dot_nt(dye, s_in.astype(MXU))
        db_ref[...] = _dot(dg.T.astype(MXU), cb) + _dot_nt((xd * eend).astype(MXU), dsb)
        ds_scr[...] = jnp.exp(cs_last) * ds_out + _dot(cm.T.astype(MXU), dye)
        last = jnp.sum(ds_out * sout_ref[0], axis=0, keepdims=True)
        rows = _iota((CHUNK, GW), 0)
        dcs = dcs_scr[...] + dyv * yoff - xd * dxd_off + jnp.where(rows == CHUNK - 1, last, 0.0)
        dda = _dot3_r(_tri(CHUNK, lambda r, c: c >= r), dcs)
        ddt_ref[...] = a_ref[...] * dda + dxd * xs
        dal_ref[...] += jnp.sum(dt * dda, axis=0, keepdims=True)
        dxs_ref[...] = dxd * dt

    rc = lambda g, j: (nc - 1 - j, g)
    return pl.pallas_call(
        body,
        name="ssd_bwd",
        out_shape=[
            jax.ShapeDtypeStruct((S, D_INNER), F32),
            jax.ShapeDtypeStruct((S, SSD_GROUPS * D_STATE), F32),
            jax.ShapeDtypeStruct((S, SSD_GROUPS * D_STATE), F32),
            jax.ShapeDtypeStruct((S, D_INNER), F32),
            jax.ShapeDtypeStruct((1, D_INNER), F32),
        ],
        grid=(SSD_GROUPS, nc),
        in_specs=[
            pl.BlockSpec((CHUNK, GW), rc),
            pl.BlockSpec((CHUNK, GW), rc),
            pl.BlockSpec((CHUNK, D_STATE), lambda g, j: (nc - 1 - j, D_INNER // D_STATE + g)),
            pl.BlockSpec((CHUNK, D_STATE), lambda g, j: (nc - 1 - j, D_INNER // D_STATE + SSD_GROUPS + g)),
            pl.BlockSpec((CHUNK, GW), rc),
            pl.BlockSpec((CHUNK, GW), rc),
            pl.BlockSpec((1, D_STATE, GW), lambda g, j: (nc - 1 - j, 0, g)),
            pl.BlockSpec((1, D_STATE, GW), lambda g, j: (jnp.minimum(nc - j, nc - 1), 0, g)),
            pl.BlockSpec((1, GW), lambda g, j: (0, g)),
        ],
        out_specs=[
            pl.BlockSpec((CHUNK, GW), rc),
            pl.BlockSpec((CHUNK, D_STATE), rc),
            pl.BlockSpec((CHUNK, D_STATE), rc),
            pl.BlockSpec((CHUNK, GW), rc),
            pl.BlockSpec((1, GW), lambda g, j: (0, g)),
        ],
        scratch_shapes=[pltpu.VMEM((D_STATE, GW), F32), pltpu.VMEM((CHUNK, GW), F32),
                        pltpu.VMEM((CHUNK, GW), F32)],
        compiler_params=_cparams(("parallel", "arbitrary")),
    )(dy, xbc_act, xbc_act, xbc_act, dte, cse, states, states, a_exp)


SB_TQ = 128
SB_T = 128
SB_DROP = 104.0


def _sb_scores(qm, k_ref, ks, rowi, coli):
    kblk = k_ref[pl.ds(ks, SB_T), :].astype(MXU)
    z = _dot_nt(qm, kblk) * SB_SCALE
    mask = (ks + coli) < rowi
    sp = _softplus(z)
    return kblk, z, mask, sp, jnp.where(mask, sp, 0.0)


def _sb_stack(v):
    lane = _iota(v.shape, 1)
    return jnp.concatenate([jnp.where(lane < HEAD_DIM, v, 0.0), jnp.where(lane >= HEAD_DIM, v, 0.0)], axis=0)


def _sb_unstack(v):
    lane = _iota((SB_TQ, LANES), 1)
    return jnp.where(lane < HEAD_DIM, v[:SB_TQ], v[SB_TQ:])


def _sb_rows(qb):
    r = _iota((2 * SB_TQ, SB_T), 0)
    return qb * SB_TQ + jnp.where(r >= SB_TQ, r - SB_TQ, r), _iota((2 * SB_TQ, SB_T), 1)


def sb_fwd(projmain):
    S = projmain.shape[0]
    nq = S // SB_TQ
    ratio = SB_TQ // SB_T

    def body(q_ref, k_ref, v_ref, o_ref, t_ref, n_ref):
        hp, qb = pl.program_id(0), pl.program_id(1)
        qst = _sb_stack(q_ref[...]).astype(MXU)
        rowi, coli = _sb_rows(qb)
        u_after = _tri(SB_T, lambda r, c: r > c)
        top = (qb + 1) * ratio

        def cond(carry):
            i, rmin, _, _ = carry
            return (i < top) & (rmin < SB_DROP)

        def kstep(carry):
            i, _, r, acc = carry
            ks = pl.multiple_of((top - 1 - i) * SB_T, SB_T)
            _, z, mask, sp, spm = _sb_scores(qst, k_ref, ks, rowi, coli)
            vblk = v_ref[pl.ds(ks, SB_T), :].astype(MXU)
            la = z - sp - _dot3_l(spm, u_after) - r
            a = jnp.where(mask, jnp.exp(la), 0.0)
            acc = acc + _dot(a.astype(MXU), vblk)
            r = r + jnp.sum(spm, axis=1, keepdims=True)
            return i + 1, jnp.min(r), r, acc

        n, _, r, acc = lax.while_loop(
            cond, kstep, (jnp.int32(0), jnp.float32(0.0), jnp.zeros((2 * SB_TQ, 1), F32),
                          jnp.zeros((2 * SB_TQ, LANES), F32)))
        o_ref[...] = _sb_unstack(acc).astype(o_ref.dtype)
        t_ref[...] = _sb_unstack(jnp.broadcast_to(r, (2 * SB_TQ, LANES)))
        n_ref[hp, qb] = n

    return pl.pallas_call(
        body,
        name="sb_fwd",
        out_shape=[jax.ShapeDtypeStruct((S, SB_WIDTH), MXU), jax.ShapeDtypeStruct((S, SB_WIDTH), F32),
                   jax.ShapeDtypeStruct((SB_WIDTH // LANES, nq), jnp.int32)],
        grid=(SB_WIDTH // LANES, nq),
        in_specs=[
            pl.BlockSpec((SB_TQ, LANES), lambda h, i: (i, OFF_Q // LANES + h)),
            pl.BlockSpec((S, LANES), lambda h, i: (0, OFF_K // LANES + h)),
            pl.BlockSpec((S, LANES), lambda h, i: (0, OFF_V // LANES + h)),
        ],
        out_specs=[pl.BlockSpec((SB_TQ, LANES), lambda h, i: (i, h))] * 2
        + [pl.BlockSpec(memory_space=pltpu.SMEM)],
        compiler_params=_cparams(("arbitrary", "arbitrary")),
    )(projmain, projmain, projmain)


def sb_bwd(projmain, do, t_exp, nblk):
    S = projmain.shape[0]
    nq = S // SB_TQ
    ratio = SB_TQ // SB_T

    def body(n_ref, q_ref, k_ref, v_ref, do_ref, t_ref, dq_ref, dk_ref, dv_ref):
        hp, qb = pl.program_id(0), pl.program_id(1)

        @pl.when(qb == 0)
        def _():
            dk_ref[...] = jnp.zeros_like(dk_ref)
            dv_ref[...] = jnp.zeros_like(dv_ref)

        qst = _sb_stack(q_ref[...]).astype(MXU)
        dost = _sb_stack(do_ref[...]).astype(MXU)
        tv = t_ref[...]
        lane = _iota((SB_TQ, LANES), 1)
        tot = jnp.concatenate(
            [jnp.sum(jnp.where(lane == HEAD_DIM * hh, tv, 0.0), axis=1, keepdims=True) for hh in range(2)], axis=0)
        rowi, coli = _sb_rows(qb)
        u_upto = _tri(SB_T, lambda r, c: r <= c)
        u_before = _tri(SB_T, lambda r, c: r < c)
        top = (qb + 1) * ratio
        kept = jnp.clip(n_ref[hp, qb], 0, top)

        def kstep(kb, carry):
            psp, pg, dq = carry
            ks = pl.multiple_of(kb * SB_T, SB_T)
            kblk, z, mask, sp, spm = _sb_scores(qst, k_ref, ks, rowi, coli)
            vblk = v_ref[pl.ds(ks, SB_T), :].astype(MXU)
            after = tot - (psp + _dot3_l(spm, u_upto))
            la = z - sp - after
            a = jnp.where(mask, jnp.exp(la), 0.0)
            gm = _dot_nt(dost, vblk) * a
            before = pg + _dot3_l(gm, u_before)
            sg = jnp.exp(z - sp)
            dz = jnp.where(mask, gm * (1.0 - sg) - sg * before, 0.0) * SB_SCALE
            dzb = dz.astype(MXU)
            dq = dq + _dot(dzb, kblk)
            dk_ref[pl.ds(ks, SB_T), :] += _dot_tn(dzb, qst)
            dv_ref[pl.ds(ks, SB_T), :] += _dot_tn(a.astype(MXU), dost)
            return (psp + jnp.sum(spm, axis=1, keepdims=True),
                    pg + jnp.sum(gm, axis=1, keepdims=True), dq)

        zero1 = jnp.zeros((2 * SB_TQ, 1), F32)
        dq = lax.fori_loop(top - kept, top, kstep, (zero1, zero1, jnp.zeros((2 * SB_TQ, LANES), F32)))[2]
        dq_ref[...] = _sb_unstack(dq).astype(dq_ref.dtype)

    return pl.pallas_call(
        body,
        name="sb_bwd",
        out_shape=[
            jax.ShapeDtypeStruct((S, SB_WIDTH), MXU),
            jax.ShapeDtypeStruct((S, SB_WIDTH), F32),
            jax.ShapeDtypeStruct((S, SB_WIDTH), F32),
        ],
        grid=(SB_WIDTH // LANES, nq),
        in_specs=[
            pl.BlockSpec(memory_space=pltpu.SMEM),
            pl.BlockSpec((SB_TQ, LANES), lambda h, i: (i, OFF_Q // LANES + h)),
            pl.BlockSpec((S, LANES), lambda h, i: (0, OFF_K // LANES + h)),
            pl.BlockSpec((S, LANES), lambda h, i: (0, OFF_V // LANES + h)),
            pl.BlockSpec((SB_TQ, LANES), lambda h, i: (i, h)),
            pl.BlockSpec((SB_TQ, LANES), lambda h, i: (i, h)),
        ],
        out_specs=[
            pl.BlockSpec((SB_TQ, LANES), lambda h, i: (i, h)),
            pl.BlockSpec((S, LANES), lambda h, i: (0, h)),
            pl.BlockSpec((S, LANES), lambda h, i: (0, h)),
        ],
        compiler_params=_cparams(("arbitrary", "arbitrary")),
    )(nblk, projmain, projmain, projmain, do, t_exp)


def local_step(x, pb, target, W, P):
    D = D_MODEL
    full = lambda a, w=D: (a, 0, w)

    (n1b,) = rowwise("norm_pre", lambda xv, g: (_rms(xv, g),), [full(x)], [full(P["norm_mix_pre"])],
                     [(D, D, MXU)])
    projmain = mm("in_proj", n1b, W["main"], "nn")
    gate_pre = mm("gate_proj", n1b, W["gate"], "nn", b_slabs=True)
    dtraw = mm("dt_proj", n1b, W["dt"], "nn")
    xbc_act = conv_fwd(projmain, P["conv_w"], P["conv_b"])
    dte, cse = ssd_prep(dtraw, P["dt_bias_pad"], P["a_exp"])
    y_scan, states = ssd_fwd(xbc_act, dte, cse)

    def f_gate(ysc, xs, z, dsk, nw):
        return _rms((ysc + xs * dsk) * (z * _sig(z)), nw)

    (y_ssd_b,) = rowwise("ssd_gate", lambda *a: (f_gate(*a),),
                         [(y_scan, 0, GW), (xbc_act, 0, GW), (projmain, OFF_Z, GW)],
                         [(P["dsk_exp"], 0, GW), (P["ssd_norm"], 0, GW)], [(D_INNER, GW, MXU)], ncb=SSD_GROUPS)
    y_sb_b, t_exp, sb_kept = sb_fwd(projmain)
    u1 = mm("ssd_branch", y_ssd_b, W["ssd"], "nn")
    u2 = mm("sb_branch", y_sb_b, W["sb"], "nn")

    def f_merge(a1, a2, g1, g2, b1, b2):
        return (_sig(g1 + b1) * a1 + _sig(g2 + b2) * a2,)

    gate_rows = [(gate_pre, 0, D), (gate_pre, D, D)]
    gate_bias = [(P["b_gate"], 0, D), (P["b_gate"], D, D)]
    (merged_b,) = rowwise("merge", f_merge, [full(u1), full(u2)] + gate_rows, gate_bias, [(D, D, MXU)])
    mo = mm("out_proj", merged_b, W["out"], "nn")

    def f_mid(xv, m, gpost, gffn):
        h1 = xv + _rms(m, gpost)
        return h1, _rms(h1, gffn)

    h1, n2b = rowwise("mix_post", f_mid, [full(x), full(mo)],
                      [full(P["norm_mix_post"]), full(P["norm_ffn_pre"])], [(D, D, F32), (D, D, MXU)])
    a_ff = mm("ff1", n2b, W["ff1"], "nn", b_slabs=True)
    (rb,) = rowwise("relu2", lambda a: (jnp.square(jnp.maximum(a, 0.0)),), [(a_ff, 0, D)], [],
                    [(D_FF, D, MXU)], ncb=D_FF // D)
    ff = mm("ff2", rb, W["ff2"], "nn")

    def f_ffn_post(h, f, g):
        h2 = h + _rms(f, g)
        return h2, h2

    h2, h2b = rowwise("ffn_post", f_ffn_post, [full(h1), full(ff)], [full(P["norm_ffn_post"])],
                      [(D, D, F32), (D, D, MXU)])
    pgp = mm("ple_gate", h2b, W["pg"], "nn")
    pe = mm("ple_proj", pb, W["ple"], "nn", b_slabs=True)

    def f_ple(h2v, gp, pev, tgt, g):
        f = lambda h, a, b, gg: h + _rms(_sig(a) * b, gg)
        h3, vjp = jax.vjp(f, h2v, gp, pev, g)
        err = h3 - tgt
        dh2, dgp, dpe, dg = vjp(err * (1.0 / D))
        lossc = (0.5 / D) * jnp.sum(err * err, axis=0, keepdims=True)
        return dh2, dgp, dpe, dg, lossc

    dh2a, dpgp_b, dpe_b, g_ple, lossc = rowwise(
        "ple_loss", f_ple, [full(h2), full(pgp), full(pe), full(target)], [full(P["norm_ple_post"])],
        [(D, D, F32), (D, D, MXU), (D, D, MXU)], [(D, D), (D, D)])
    dh2 = mm("d_ple_gate_x", dpgp_b, W["pg"], "nt", add=dh2a)
    gW = {}
    gW["w_ple_gate"] = mm("d_ple_gate_w", h2b, dpgp_b, "tn")
    gW["w_ple"] = mm("d_ple_w", pb, dpe_b, "tn", out_slabs=True)

    def b_ffn_post(d, f, g):
        _, vjp = jax.vjp(_rms, f, g)
        return vjp(d)

    dff_b, g_ffn_post = rowwise("d_ffn_post", b_ffn_post, [full(dh2), full(ff)], [full(P["norm_ffn_post"])],
                                [(D, D, MXU)], [(D, D)])
    dr = mm("d_ff2_x", dff_b, W["ff2"], "nt")
    gW["w_ff2"] = mm("d_ff2_w", rb, dff_b, "tn")
    (da_b,) = rowwise("d_relu2", lambda d, a: (d * (2.0 * jnp.maximum(a, 0.0)),), [(dr, 0, D), (a_ff, 0, D)], [],
                      [(D_FF, D, MXU)], ncb=D_FF // D)
    dn2 = mm("d_ff1_x", da_b, W["ff1"], "nt", b_slabs=True)
    gW["w_ff1"] = mm("d_ff1_w", n2b, da_b, "tn", out_slabs=True)

    def b_mid(d2, dn, h, m, gpost, gffn):
        _, vjp = jax.vjp(_rms, h, gffn)
        dh, dgffn = vjp(dn)
        dh1 = d2 + dh
        _, vjp2 = jax.vjp(_rms, m, gpost)
        dm, dgpost = vjp2(dh1)
        return dh1, dm, dgpost, dgffn

    dh1, dmo_b, g_mix_post, g_ffn_pre = rowwise(
        "d_mix_post", b_mid, [full(dh2), full(dn2), full(h1), full(mo)],
        [full(P["norm_mix_post"]), full(P["norm_ffn_pre"])], [(D, D, F32), (D, D, MXU)], [(D, D), (D, D)])
    dmerged = mm("d_out_x", dmo_b, W["out"], "nt")
    gW["w_out"] = mm("d_out_w", merged_b, dmo_b, "tn")

    def b_merge(d, a1, a2, g1, g2, b1, b2):
        s1, s2 = _sig(g1 + b1), _sig(g2 + b2)
        dg1 = d * a1 * s1 * (1.0 - s1)
        dg2 = d * a2 * s2 * (1.0 - s2)
        dg = jnp.concatenate([dg1, dg2], axis=1)
        return d * s1, d * s2, dg, jnp.sum(dg, axis=0, keepdims=True)

    du1_b, du2_b, dgp_b, g_b_gate = rowwise(
        "d_merge", b_merge, [full(dmerged), full(u1), full(u2)] + gate_rows, gate_bias,
        [(D, D, MXU), (D, D, MXU), (2 * D, 2 * D, MXU)], [(2 * D, 2 * D)])
    dy_ssd = mm("d_ssd_branch_x", du1_b, W["ssd"], "nt")
    dy_sb = mm("d_sb_branch_x", du2_b, W["sb"], "nt")
    gW["w_ssd_branch"] = mm("d_ssd_branch_w", y_ssd_b, du1_b, "tn")
    gW["w_sb_branch"] = mm("d_sb_branch_w", y_sb_b, du2_b, "tn")

    def b_gate(d, ysc, xs, z, dsk, nw):
        _, vjp = jax.vjp(f_gate, ysc, xs, z, dsk, nw)
        return vjp(d)

    dy_scan, dxs_skip, dz_b, g_dsk_exp, g_ssd_norm = rowwise(
        "d_ssd_gate", b_gate, [(dy_ssd, 0, GW), (y_scan, 0, GW), (xbc_act, 0, GW), (projmain, OFF_Z, GW)],
        [(P["dsk_exp"], 0, GW), (P["ssd_norm"], 0, GW)],
        [(D_INNER, GW, F32), (D_INNER, GW, F32), (D_INNER, GW, MXU)], [(D_INNER, GW), (D_INNER, GW)],
        ncb=SSD_GROUPS)
    dxs, d_b, d_c, ddt_part, g_a_exp = ssd_bwd(dy_scan, xbc_act, dte, cse, states, P["a_exp"])

    def b_dt(dpart, dtr, bias):
        ddt = _dot3_l(dpart, _head_reduce())
        d = ddt * _sig(dtr + bias)
        return d, jnp.sum(d, axis=0, keepdims=True)

    ddt_b, g_dt_bias_pad = rowwise("d_dt", b_dt, [(ddt_part, 0, D_INNER), (dtraw, 0, LANES)],
                                   [(P["dt_bias_pad"], 0, LANES)], [(LANES, LANES, MXU)], [(LANES, LANES)])
    dxbc_act = jnp.concatenate([dxs + dxs_skip, d_b, d_c], axis=1)
    dxbc_b, g_conv_w, g_conv_b = conv_bwd(dxbc_act, projmain, P["conv_w"], P["conv_b"])
    dq_b, dk, dv = sb_bwd(projmain, dy_sb, t_exp, sb_kept)
    dmain_b = jnp.concatenate([dz_b, dxbc_b, dq_b, dk.astype(MXU), dv.astype(MXU)], axis=1)
    dn1_dt = mm("d_dt_x", ddt_b, W["dt"], "nt")
    dn1_gate = mm("d_gate_x", dgp_b, W["gate"], "nt", add=dn1_dt, b_slabs=True)
    dn1 = mm("d_in_x", dmain_b, W["main"], "nt", add=dn1_gate)
    gW["main"] = mm("d_in_w", n1b, dmain_b, "tn")
    gW["dt"] = mm("d_dt_w", n1b, ddt_b, "tn")
    gW["w_gate"] = mm("d_gate_w", n1b, dgp_b, "tn", out_slabs=True)

    def b_pre(d1, dn, xv, g):
        _, vjp = jax.vjp(_rms, xv, g)
        dx, dg = vjp(dn)
        return d1 + dx, dg

    grad_x, g_mix_pre = rowwise("d_norm_pre", b_pre, [full(dh1), full(dn1), full(x)], [full(P["norm_mix_pre"])],
                                [(D, D, F32)], [(D, D)])

    gW["conv_w"] = g_conv_w
    gS = {
        "norm_mix_pre": g_mix_pre, "conv_b": g_conv_b, "dt_bias_pad": g_dt_bias_pad, "a_exp": g_a_exp,
        "dsk_exp": g_dsk_exp, "ssd_norm": g_ssd_norm, "b_gate": g_b_gate,
        "norm_mix_post": g_mix_post, "norm_ffn_pre": g_ffn_pre, "norm_ffn_post": g_ffn_post,
        "norm_ple_post": g_ple,
    }
    return lossc, grad_x, gW, gS


IN_SPLITS = (2048, 6144, 6176, 7200, 8224)
IN_SHARD = 1156
IN_PAD = 1280
COL_SHARDED = {"w_in": (1024, IN_SHARD), "conv_w": (4, 512), "w_gate": (1024, 256), "w_ff1": (1024, 512),
               "w_ple": (256, 128)}
ROW_SHARDED = {"w_ssd_branch": (256, 1024), "w_sb_branch": (128, 1024), "w_out": (128, 1024),
               "w_ff2": (512, 1024), "w_ple_gate": (128, 1024)}
SHARDED = tuple(COL_SHARDED) + tuple(ROW_SHARDED)
SMALL = (
    ("norm_mix_pre", 1024), ("conv_b", 4096), ("dt_bias", 32), ("a_log", 32), ("d_skip", 32), ("ssd_norm", 2048),
    ("b_gate", 2048), ("norm_mix_post", 1024), ("norm_ffn_pre", 1024), ("norm_ffn_post", 1024),
    ("norm_ple_post", 1024),
)
ROW = 1024
SMALL_ROWS = 16


def _rows_of(n):
    return -(-n // ROW)


def _pad_to(v, n, axis=-1):
    pad = [(0, 0)] * v.ndim
    pad[axis] = (0, n - v.shape[axis])
    return jnp.pad(v, pad)


def build_matrices(g):
    w_in = jnp.concatenate([g["w_in"][k, :, :IN_SHARD] for k in range(N_DEV)], axis=1)
    stack = lambda a: a.reshape(a.shape[0] * a.shape[1], a.shape[2])
    return {
        "main": jnp.concatenate([w_in[:, :IN_SPLITS[1]], w_in[:, IN_SPLITS[2]:]], axis=1),
        "dt": _pad_to(w_in[:, IN_SPLITS[1]:IN_SPLITS[2]], LANES),
        "gate": g["w_gate"], "ff1": g["w_ff1"], "ple": g["w_ple"],
        "ssd": stack(g["w_ssd_branch"]), "sb": stack(g["w_sb_branch"]), "out": stack(g["w_out"]),
        "ff2": stack(g["w_ff2"]), "pg": stack(g["w_ple_gate"]),
    }


def build_small(small, conv_w_slabs):
    P = {k: small[k] for k in ("norm_mix_pre", "conv_b", "ssd_norm", "b_gate", "norm_mix_post", "norm_ffn_pre",
                               "norm_ffn_post", "norm_ple_post")}
    P["conv_w"] = jnp.concatenate([conv_w_slabs[k] for k in range(N_DEV)], axis=1)
    P["dt_bias_pad"] = _pad_to(small["dt_bias"], LANES)
    P["a_exp"] = jnp.repeat(-jnp.exp(small["a_log"]), HEAD_DIM, axis=1)
    P["dsk_exp"] = jnp.repeat(small["d_skip"], HEAD_DIM, axis=1)
    return P


def small_grads(gS, small):
    heads = lambda a: a.reshape(SSD_HEADS, HEAD_DIM).sum(axis=1)[None, :]
    out = {k: gS[k] for k in ("norm_mix_pre", "conv_b", "ssd_norm", "b_gate", "norm_mix_post", "norm_ffn_pre",
                              "norm_ffn_post", "norm_ple_post")}
    out["dt_bias"] = gS["dt_bias_pad"][:, :SSD_HEADS]
    out["a_log"] = heads(gS["a_exp"]) * (-jnp.exp(small["a_log"]))
    out["d_skip"] = heads(gS["dsk_exp"])
    return out


def grad_slabs(gW):
    g_in = jnp.concatenate([gW["main"][:, :IN_SPLITS[1]], gW["dt"][:, :SSD_HEADS], gW["main"][:, IN_SPLITS[1]:]],
                           axis=1)
    out = {"w_in": jnp.stack([_pad_to(g_in[:, IN_SHARD * k:IN_SHARD * (k + 1)], IN_PAD) for k in range(N_DEV)])}
    width = COL_SHARDED["conv_w"][1]
    out["conv_w"] = jnp.stack([gW["conv_w"][:, width * k:width * (k + 1)] for k in range(N_DEV)])
    for name in ("w_gate", "w_ff1", "w_ple"):
        out[name] = gW[name]
    for name, shape in ROW_SHARDED.items():
        out[name] = gW[name].reshape((N_DEV,) + shape)
    return out


def pack_small(get):
    cols = [_pad_to(get(name).reshape(n), _rows_of(n) * ROW) for name, n in SMALL]
    return jnp.concatenate(cols).reshape(SMALL_ROWS, ROW)


def unpack_small(flat):
    out, r0 = {}, 0
    for name, n in SMALL:
        rows = _rows_of(n)
        out[name] = flat[r0:r0 + rows].reshape(rows * ROW)[:n].reshape(1, n)
        r0 += rows
    return out


def exchange(name, srcs, scatter):
    n = len(srcs)
    out_shape = [jax.ShapeDtypeStruct(s.shape if sc else (N_DEV,) + s.shape, s.dtype)
                 for s, sc in zip(srcs, scatter)]

    def body(*refs):
        src, out = refs[:n], refs[n:2 * n]
        send_sems, recv_sems, local_sems = refs[2 * n:]
        x, y, c = lax.axis_index("x"), lax.axis_index("y"), lax.axis_index("c")
        index = lambda d: 4 * d[0] + 2 * d[1] + d[2]
        mine = index((x, y, c))
        peers = [(1 - x if k & 4 else x, 1 - y if k & 2 else y, 1 - c if k & 1 else c) for k in range(1, N_DEV)]

        def part(i, dev_index):
            return src[i].at[dev_index] if scatter[i] else src[i]

        def remote(i, k, src_ref, slot):
            s = k * n + i
            return pltpu.make_async_remote_copy(
                src_ref=src_ref, dst_ref=out[i].at[slot], send_sem=send_sems.at[s], recv_sem=recv_sems.at[s],
                device_id=peers[k], device_id_type=pl.DeviceIdType.MESH)

        local = [pltpu.make_async_copy(part(i, mine), out[i].at[mine], local_sems.at[i]) for i in range(n)]
        for cp in local:
            cp.start()
        sends = [remote(i, k, part(i, index(peers[k])), mine) for k in range(N_DEV - 1) for i in range(n)]
        for cp in sends:
            cp.start()
        for k in range(N_DEV - 1):
            for i in range(n):
                remote(i, k, part(i, mine), index(peers[k])).wait_recv()
        for cp in sends:
            cp.wait_send()
        for cp in local:
            cp.wait()

    any_spec = pl.BlockSpec(memory_space=pl.ANY)
    return pl.pallas_call(
        body,
        name=name,
        out_shape=out_shape,
        in_specs=[any_spec] * n,
        out_specs=[any_spec] * n,
        scratch_shapes=[
            pltpu.SemaphoreType.DMA(((N_DEV - 1) * n,)),
            pltpu.SemaphoreType.DMA(((N_DEV - 1) * n,)),
            pltpu.SemaphoreType.DMA((n,)),
        ],
    )(*srcs)


def adamw(name, parts, w, m, v):
    rows, cols = w.shape
    tr = _pick(rows, (128, 64, 32, 16, 8))

    def body(p_ref, w_ref, m_ref, v_ref, g_ref, d_ref, m2_ref, v2_ref):
        g = p_ref[0]
        for k in range(1, N_DEV):
            g = g + p_ref[k]
        m2 = ADAM_B1 * m_ref[...] + (1.0 - ADAM_B1) * g
        v2 = ADAM_B2 * v_ref[...] + (1.0 - ADAM_B2) * jnp.square(g)
        m_hat = m2 / (1.0 - ADAM_B1 ** ADAM_STEP)
        v_hat = v2 / (1.0 - ADAM_B2 ** ADAM_STEP)
        g_ref[...] = g
        d_ref[...] = -ADAM_LR * (m_hat / (jnp.sqrt(v_hat) + ADAM_EPS) + ADAM_WD * w_ref[...])
        m2_ref[...] = m2
        v2_ref[...] = v2

    spec = pl.BlockSpec((tr, cols), lambda i: (i, 0))
    return pl.pallas_call(
        body,
        name=name,
        out_shape=[jax.ShapeDtypeStruct((rows, cols), F32)] * 4,
        grid=(rows // tr,),
        in_specs=[pl.BlockSpec((N_DEV, tr, cols), lambda i: (0, i, 0)), spec, spec, spec],
        out_specs=[spec] * 4,
        compiler_params=_cparams(("parallel",)),
    )(parts, w, m, v)


WEIGHT_ORDER = (
    "norm_mix_pre", "w_in", "conv_w", "conv_b", "dt_bias", "a_log", "d_skip", "ssd_norm", "w_ssd_branch",
    "w_sb_branch", "w_gate", "b_gate", "w_out", "norm_mix_post", "norm_ffn_pre", "w_ff1", "w_ff2", "norm_ffn_post",
    "w_ple", "w_ple_gate", "norm_ple_post",
)


def kernel(x, p, norm_mix_pre, w_in, conv_w, conv_b, dt_bias, a_log, d_skip, ssd_norm, w_ssd_branch, w_sb_branch, w_gate, b_gate, w_out, norm_mix_post, norm_ffn_pre, w_ff1, w_ff2, norm_ffn_post, w_ple, w_ple_gate, norm_ple_post, loss_target, m_norm_mix_pre, m_w_in, m_conv_w, m_conv_b, m_dt_bias, m_a_log, m_d_skip, m_ssd_norm, m_w_ssd_branch, m_w_sb_branch, m_w_gate, m_b_gate, m_w_out, m_norm_mix_post, m_norm_ffn_pre, m_w_ff1, m_w_ff2, m_norm_ffn_post, m_w_ple, m_w_ple_gate, m_norm_ple_post, v_norm_mix_pre, v_w_in, v_conv_w, v_conv_b, v_dt_bias, v_a_log, v_d_skip, v_ssd_norm, v_w_ssd_branch, v_w_sb_branch, v_w_gate, v_b_gate, v_w_out, v_norm_mix_post, v_norm_ffn_pre, v_w_ff1, v_w_ff2, v_norm_ffn_post, v_w_ple, v_w_ple_gate, v_norm_ple_post):
    a = dict(locals())
    seq = x.shape[1]
    x2 = x.reshape(seq, D_MODEL)
    target = loss_target.reshape(seq, D_MODEL)
    pb = p.reshape(seq, PLE_DIM).astype(MXU)

    def shard(prefix, name, dtype):
        v = a[prefix + name][0].astype(dtype)
        return _pad_to(v, IN_PAD) if name == "w_in" else v

    names = [n for n in SHARDED if n != "conv_w"]
    got = exchange("gather_weights", [shard("", n, MXU) for n in names] + [a["conv_w"][0]],
                   [False] * (len(names) + 1))
    small = {n: a[n] for n, _ in SMALL}

    lossc, grad_x, g_full, g_acc = local_step(x2, pb, target, build_matrices(dict(zip(names, got))),
                                              build_small(small, got[-1]))
    loss = lax.psum(jnp.sum(lossc), ("x", "y", "c"))
    g_small = small_grads(g_acc, small)

    slabs = grad_slabs(g_full)
    parts = exchange("scatter_grads", [slabs[n] for n in SHARDED] + [pack_small(lambda n: g_small[n])],
                     [True] * len(SHARDED) + [False])

    leaves = {}
    for n, part in zip(SHARDED, parts):
        res = adamw("adamw_" + n, part, shard("", n, F32), shard("m_", n, F32), shard("v_", n, F32))
        leaves[n] = [r[None, :, :IN_SHARD] if n == "w_in" else r[None] for r in res]
    res = adamw("adamw_small", parts[-1], pack_small(lambda n: a[n]), pack_small(lambda n: a["m_" + n]),
                pack_small(lambda n: a["v_" + n]))
    for j, r in enumerate(res):
        for n, leaf in unpack_small(r).items():
            leaves.setdefault(n, [None] * 4)[j] = leaf
    outs = [loss, grad_x.reshape(x.shape)]
    for j in range(4):
        outs += [leaves[n][j] for n in WEIGHT_ORDER]
    return tuple(outs)
```

```python
import functools

import jax
import jax.numpy as jnp
from jax import lax
from jax.experimental import pallas as pl
from jax.experimental.pallas import tpu as pltpu

F32 = jnp.float32
MXU = jnp.bfloat16
VMEM_LIMIT = 56 * 1024 * 1024

D_MODEL = 1024
D_INNER = 2048
SSD_HEADS = 32
HEAD_DIM = 64
SSD_GROUPS = 8
D_STATE = 128
CONV_K = 4
CONV_DIM = 4096
CHUNK = 128
SB_WIDTH = 1024
D_FF = 4096
PLE_DIM = 256
RMS_EPS = 1e-6
SB_SCALE = HEAD_DIM ** -0.5
N_DEV = 8
LANES = 128

OFF_Z, OFF_XBC, OFF_Q, OFF_K, OFF_V = 0, 2048, 6144, 7168, 8192

ADAM_LR = 0.001
ADAM_B1 = 0.9
ADAM_B2 = 0.999
ADAM_EPS = 1e-08
ADAM_WD = 0.01
ADAM_STEP = 10


def _sig(x):
    return 1.0 / (1.0 + jnp.exp(-x))


def _softplus(x):
    return jnp.maximum(x, 0.0) + jnp.log(1.0 + jnp.exp(-jnp.abs(x)))


def _rms(x, w):
    return x * lax.rsqrt(jnp.mean(x * x, axis=-1, keepdims=True) + RMS_EPS) * w


def _dot(a, b):
    return jnp.dot(a, b, preferred_element_type=F32)


def _dot_nt(a, b):
    return lax.dot_general(a, b, (((1,), (1,)), ((), ())), preferred_element_type=F32)


def _dot_tn(a, b):
    return lax.dot_general(a, b, (((0,), (0,)), ((), ())), preferred_element_type=F32)


def _split3(x):
    x1 = x.astype(MXU)
    r = x - x1.astype(F32)
    x2 = r.astype(MXU)
    r = r - x2.astype(F32)
    return x1, x2, r.astype(MXU)


def _dot3_l(a, u):
    a1, a2, a3 = _split3(a)
    return (_dot(a3, u) + _dot(a2, u)) + _dot(a1, u)


def _dot3_r(u, a):
    a1, a2, a3 = _split3(a)
    return (_dot(u, a3) + _dot(u, a2)) + _dot(u, a1)


def _iota(shape, dim):
    return lax.broadcasted_iota(jnp.int32, shape, dim)


def _tri(n, cmp):
    r, c = _iota((n, n), 0), _iota((n, n), 1)
    return cmp(r, c).astype(F32).astype(MXU)


def _cparams(sem):
    return pltpu.CompilerParams(dimension_semantics=sem, vmem_limit_bytes=VMEM_LIMIT)


def _pick(n, cands):
    for c in cands:
        if n % c == 0:
            return c
    return n


def mm(name, a, b, mode, add=None, out_dtype=F32, b_slabs=False, out_slabs=False):
    slab = None
    if b_slabs:
        slab = b.shape[2]
        bshape = (b.shape[1], N_DEV * slab)
    else:
        bshape = b.shape
    if mode == "nn":
        (M, K), (K2, N) = a.shape, bshape
    elif mode == "nt":
        (M, K), (N, K2) = a.shape, bshape
    else:
        (K, M), (K2, N) = a.shape, bshape
    assert K == K2, (name, a.shape, b.shape)
    tm = _pick(M, (1024, 512, 256, 128))
    tn = _pick(N, (1024, 512, 256, 128))
    tk = _pick(K, (1024, 512, 256, 128))
    if b_slabs and mode == "nn":
        tn = slab
    if b_slabs and mode == "nt":
        tk = slab
    if out_slabs:
        assert mode == "tn" and N % N_DEV == 0
        tn = N // N_DEV
    nk = K // tk

    def body(*refs):
        if add is None:
            a_ref, b_ref, o_ref, acc = refs
        else:
            a_ref, b_ref, add_ref, o_ref, acc = refs
        k = pl.program_id(2)

        @pl.when(k == 0)
        def _():
            acc[...] = jnp.zeros_like(acc) if add is None else add_ref[...]

        av, bv = a_ref[...], b_ref[...]
        if mode == "nn":
            acc[...] += _dot(av, bv)
        elif mode == "nt":
            acc[...] += _dot_nt(av, bv)
        else:
            acc[...] += _dot_tn(av, bv)

        @pl.when(k == nk - 1)
        def _():
            o_ref[...] = acc[...].astype(o_ref.dtype)

    if mode == "nn":
        a_spec = pl.BlockSpec((tm, tk), lambda i, j, k: (i, k))
        b_spec = pl.BlockSpec((tk, tn), lambda i, j, k: (k, j))
    elif mode == "nt":
        a_spec = pl.BlockSpec((tm, tk), lambda i, j, k: (i, k))
        b_spec = pl.BlockSpec((tn, tk), lambda i, j, k: (j, k))
    else:
        a_spec = pl.BlockSpec((tk, tm), lambda i, j, k: (k, i))
        b_spec = pl.BlockSpec((tk, tn), lambda i, j, k: (k, j))
    if b_slabs and mode == "nn":
        b_spec = pl.BlockSpec((None, tk, tn), lambda i, j, k: (j, k, 0))
    if b_slabs and mode == "nt":
        b_spec = pl.BlockSpec((None, tn, tk), lambda i, j, k: (k, j, 0))
    o_spec = pl.BlockSpec((tm, tn), lambda i, j, k: (i, j))
    in_specs, args = [a_spec, b_spec], [a, b]
    if add is not None:
        in_specs.append(o_spec)
        args.append(add)
    out_sds = jax.ShapeDtypeStruct((M, N), out_dtype)
    if out_slabs:
        o_spec = pl.BlockSpec((None, tm, tn), lambda i, j, k: (j, i, 0))
        out_sds = jax.ShapeDtypeStruct((N_DEV, M, tn), out_dtype)
    return pl.pallas_call(
        body,
        name=name,
        out_shape=out_sds,
        grid=(M // tm, N // tn, nk),
        in_specs=in_specs,
        out_specs=o_spec,
        scratch_shapes=[pltpu.VMEM((tm, tn), F32)],
        compiler_params=_cparams(("parallel", "parallel", "arbitrary")),
    )(*args)


def rowwise(name, fn, rows, bcast, outs, accs=(), tr=256, ncb=1):
    S = rows[0][0].shape[0]
    tr = min(tr, S)
    nrb = S // tr
    in_specs, args = [], []
    for arr, off, w in rows:
        assert off % w == 0 and arr.shape[0] == S
        in_specs.append(pl.BlockSpec((tr, w), lambda j, i, ob=off // w: (i, ob + j)))
        args.append(arr)
    for arr, off, w in bcast:
        assert off % w == 0
        in_specs.append(pl.BlockSpec((arr.shape[0], w), lambda j, i, ob=off // w: (0, ob + j)))
        args.append(arr)
    out_shape, out_specs = [], []
    for tw, w, dt in outs:
        out_shape.append(jax.ShapeDtypeStruct((S, tw), dt))
        out_specs.append(pl.BlockSpec((tr, w), lambda j, i: (i, j)))
    for tw, w in accs:
        out_shape.append(jax.ShapeDtypeStruct((1, tw), F32))
        out_specs.append(pl.BlockSpec((1, w), lambda j, i: (0, j)))
    nin, nout = len(args), len(outs)

    def body(*refs):
        res = fn(*[r[...] for r in refs[:nin]])
        o_refs, a_refs = refs[nin:nin + nout], refs[nin + nout:]
        for r, v in zip(o_refs, res[:nout]):
            r[...] = v.astype(r.dtype)
        i = pl.program_id(1)
        for r, v in zip(a_refs, res[nout:]):
            @pl.when(i == 0)
            def _(r=r, v=v):
                r[...] = v

            @pl.when(i > 0)
            def _(r=r, v=v):
                r[...] += v

    res = pl.pallas_call(
        body,
        name=name,
        out_shape=out_shape,
        grid=(ncb, nrb),
        in_specs=in_specs,
        out_specs=out_specs,
        compiler_params=_cparams(("parallel", "arbitrary")),
    )(*args)
    return res


CONV_TC = 128


def _conv_pre(u, w, b, row):
    pre = b + w[3:4, :] * u
    shifted = []
    for s in (1, 2, 3):
        us = jnp.where(row >= s, pltpu.roll(u, s, 0), 0.0)
        shifted.append(us)
        pre = pre + w[3 - s:4 - s, :] * us
    return pre, shifted


def conv_fwd(projmain, conv_w, conv_b):
    S = projmain.shape[0]
    tc = CONV_TC

    def body(u_ref, w_ref, b_ref, o_ref):
        u = u_ref[...]
        row = _iota(u.shape, 0)
        pre, _ = _conv_pre(u, w_ref[...], b_ref[...], row)
        o_ref[...] = pre * _sig(pre)

    return pl.pallas_call(
        body,
        name="conv_fwd",
        out_shape=jax.ShapeDtypeStruct((S, CONV_DIM), F32),
        grid=(CONV_DIM // tc,),
        in_specs=[
            pl.BlockSpec((S, tc), lambda j: (0, OFF_XBC // tc + j)),
            pl.BlockSpec((CONV_K, tc), lambda j: (0, j)),
            pl.BlockSpec((1, tc), lambda j: (0, j)),
        ],
        out_specs=pl.BlockSpec((S, tc), lambda j: (0, j)),
        compiler_params=_cparams(("parallel",)),
    )(projmain, conv_w, conv_b)


def conv_bwd(dact, projmain, conv_w, conv_b):
    S = projmain.shape[0]
    tc = CONV_TC

    def body(d_ref, u_ref, w_ref, b_ref, du_ref, dw_ref, db_ref):
        u = u_ref[...]
        w = w_ref[...]
        row = _iota(u.shape, 0)
        pre, shifted = _conv_pre(u, w, b_ref[...], row)
        sg = _sig(pre)
        dpre = d_ref[...] * (sg * (1.0 + pre * (1.0 - sg)))
        du = w[3:4, :] * dpre
        dw_ref[3:4, :] = jnp.sum(dpre * u, axis=0, keepdims=True)
        for s in (1, 2, 3):
            ds = jnp.where(row < S - s, pltpu.roll(dpre, S - s, 0), 0.0)
            du = du + w[3 - s:4 - s, :] * ds
            dw_ref[3 - s:4 - s, :] = jnp.sum(dpre * shifted[s - 1], axis=0, keepdims=True)
        du_ref[...] = du.astype(du_ref.dtype)
        db_ref[...] = jnp.sum(dpre, axis=0, keepdims=True)

    return pl.pallas_call(
        body,
        name="conv_bwd",
        out_shape=[
            jax.ShapeDtypeStruct((S, CONV_DIM), MXU),
            jax.ShapeDtypeStruct((CONV_K, CONV_DIM), F32),
            jax.ShapeDtypeStruct((1, CONV_DIM), F32),
        ],
        grid=(CONV_DIM // tc,),
        in_specs=[
            pl.BlockSpec((S, tc), lambda j: (0, j)),
            pl.BlockSpec((S, tc), lambda j: (0, OFF_XBC // tc + j)),
            pl.BlockSpec((CONV_K, tc), lambda j: (0, j)),
            pl.BlockSpec((1, tc), lambda j: (0, j)),
        ],
        out_specs=[
            pl.BlockSpec((S, tc), lambda j: (0, j)),
            pl.BlockSpec((CONV_K, tc), lambda j: (0, j)),
            pl.BlockSpec((1, tc), lambda j: (0, j)),
        ],
        compiler_params=_cparams(("parallel",)),
    )(dact, projmain, conv_w, conv_b)


def _head_expand():
    r, j = _iota((LANES, D_INNER), 0), _iota((LANES, D_INNER), 1)
    return ((j >= r * HEAD_DIM) & (j < r * HEAD_DIM + HEAD_DIM)).astype(F32).astype(MXU)


def _head_reduce():
    j, r = _iota((D_INNER, LANES), 0), _iota((D_INNER, LANES), 1)
    return ((j >= r * HEAD_DIM) & (j < r * HEAD_DIM + HEAD_DIM)).astype(F32).astype(MXU)


def ssd_prep(dtraw, dt_bias_pad, a_exp):
    S = dtraw.shape[0]

    def body(dtr_ref, bias_ref, a_ref, dte_ref, cse_ref):
        dt = _softplus(dtr_ref[...] + bias_ref[...])
        dte = _dot3_l(dt, _head_expand())
        dte_ref[...] = dte
        incl = _tri(CHUNK, lambda r, c: r >= c)
        cse_ref[...] = _dot3_r(incl, dte * a_ref[...])

    return pl.pallas_call(
        body,
        name="ssd_prep",
        out_shape=[jax.ShapeDtypeStruct((S, D_INNER), F32)] * 2,
        grid=(S // CHUNK,),
        in_specs=[
            pl.BlockSpec((CHUNK, LANES), lambda c: (c, 0)),
            pl.BlockSpec((1, LANES), lambda c: (0, 0)),
            pl.BlockSpec((1, D_INNER), lambda c: (0, 0)),
        ],
        out_specs=[pl.BlockSpec((CHUNK, D_INNER), lambda c: (c, 0))] * 2,
        compiler_params=_cparams(("parallel",)),
    )(dtraw, dt_bias_pad, a_exp)


GW = 4 * HEAD_DIM


def ssd_fwd(xbc_act, dte, cse):
    S = xbc_act.shape[0]
    nc = S // CHUNK

    def body(xs_ref, b_ref, c_ref, dte_ref, cse_ref, y_ref, st_ref, s_scr):
        c = pl.program_id(1)

        @pl.when(c == 0)
        def _():
            s_scr[...] = jnp.zeros_like(s_scr)

        s_in = s_scr[...]
        st_ref[0] = s_in
        cs = cse_ref[...]
        xd = xs_ref[...] * dte_ref[...]
        bm, cm = b_ref[...], c_ref[...]
        bb, cb = bm.astype(MXU), cm.astype(MXU)
        cs_last = cs[CHUNK - 1:CHUNK, :]
        g = _dot_nt(cb, bb)
        cs_t = cs.T
        yoff = _dot(cb, s_in.astype(MXU)) * jnp.exp(cs)
        row, col = _iota((CHUNK, CHUNK), 0), _iota((CHUNK, CHUNK), 1)
        for p in range(2):
            sl = slice(LANES * p, LANES * (p + 1))
            xdp = xd[:, sl].astype(MXU)
            yd = []
            for hh in range(2):
                h = 2 * p + hh
                lo = HEAD_DIM * h
                lam = jnp.where(row >= col, jnp.exp(cs[:, lo:lo + 1] - cs_t[lo:lo + 1, :]), 0.0)
                yd.append(_dot((g * lam).astype(MXU), xdp))
            y_ref[:, sl] = yoff[:, sl] + jnp.where(col < HEAD_DIM, yd[0], yd[1])
        w = (xd * jnp.exp(cs_last - cs)).astype(MXU)
        s_scr[...] = jnp.exp(cs_last) * s_in + _dot(bm.T.astype(MXU), w)

    return pl.pallas_call(
        body,
        name="ssd_fwd",
        out_shape=[
            jax.ShapeDtypeStruct((S, D_INNER), F32),
            jax.ShapeDtypeStruct((nc, D_STATE, D_INNER), F32),
        ],
        grid=(SSD_GROUPS, nc),
        in_specs=[
            pl.BlockSpec((CHUNK, GW), lambda g, c: (c, g)),
            pl.BlockSpec((CHUNK, D_STATE), lambda g, c: (c, D_INNER // D_STATE + g)),
            pl.BlockSpec((CHUNK, D_STATE), lambda g, c: (c, D_INNER // D_STATE + SSD_GROUPS + g)),
            pl.BlockSpec((CHUNK, GW), lambda g, c: (c, g)),
            pl.BlockSpec((CHUNK, GW), lambda g, c: (c, g)),
        ],
        out_specs=[
            pl.BlockSpec((CHUNK, GW), lambda g, c: (c, g)),
            pl.BlockSpec((1, D_STATE, GW), lambda g, c: (c, 0, g)),
        ],
        scratch_shapes=[pltpu.VMEM((D_STATE, GW), F32)],
        compiler_params=_cparams(("parallel", "arbitrary")),
    )(xbc_act, xbc_act, xbc_act, dte, cse)


def ssd_bwd(dy, xbc_act, dte, cse, states, a_exp):
    S = xbc_act.shape[0]
    nc = S // CHUNK

    def body(dy_ref, xs_ref, b_ref, c_ref, dte_ref, cse_ref, sin_ref, sout_ref, a_ref,
             dxs_ref, db_ref, dc_ref, ddt_ref, dal_ref, ds_scr, dxd_scr, dcs_scr):
        j = pl.program_id(1)

        @pl.when(j == 0)
        def _():
            ds_scr[...] = jnp.zeros_like(ds_scr)
            dal_ref[...] = jnp.zeros_like(dal_ref)

        ds_out = ds_scr[...]
        dyv, xs = dy_ref[...], xs_ref[...]
        dt, cs = dte_ref[...], cse_ref[...]
        s_in = sin_ref[0]
        bm, cm = b_ref[...], c_ref[...]
        bb, cb = bm.astype(MXU), cm.astype(MXU)
        dsb = ds_out.astype(MXU)
        xd = xs * dt
        ecs = jnp.exp(cs)
        cs_last = cs[CHUNK - 1:CHUNK, :]
        eend = jnp.exp(cs_last - cs)
        g = _dot_nt(cb, bb)
        g_t = _dot_nt(bb, cb)
        cs_t = cs.T
        dxd_off = _dot(bb, dsb) * eend
        yoff = _dot(cb, s_in.astype(MXU)) * ecs
        row, col = _iota((CHUNK, CHUNK), 0), _iota((CHUNK, CHUNK), 1)
        dg = jnp.zeros((CHUNK, CHUNK), F32)
        for p in range(2):
            sl = slice(LANES * p, LANES * (p + 1))
            dyp = dyv[:, sl]
            xdpb = xd[:, sl].astype(MXU)
            acc = dxd_off[:, sl]
            dcs_p = jnp.zeros((CHUNK, LANES), F32)
            for hh in range(2):
                h = 2 * p + hh
                lo = HEAD_DIM * h
                hm = (col >= HEAD_DIM * hh) & (col < HEAD_DIM * (hh + 1))
                cs_col, cs_row = cs[:, lo:lo + 1], cs_t[lo:lo + 1, :]
                lam = jnp.where(row >= col, jnp.exp(cs_col - cs_row), 0.0)
                lam_t = jnp.where(col >= row, jnp.exp(cs_row - cs_col), 0.0)
                dym = jnp.where(hm, dyp, 0.0).astype(MXU)
                m_t = g_t * lam_t
                acc = acc + _dot(m_t.astype(MXU), dym)
                dm = _dot_nt(dym, xdpb)
                dm_t = _dot_nt(xdpb, dym)
                dg = dg + dm * lam
                wdiff = (jnp.sum(dm * (g * lam), axis=1, keepdims=True)
                         - jnp.sum(dm_t * m_t, axis=1, keepdims=True))
                dcs_p = dcs_p + jnp.where(col == HEAD_DIM * hh, wdiff, 0.0)
            dxd_scr[:, sl] = acc
            dcs_scr[:, sl] = dcs_p
        dxd = dxd_scr[...]
        dgb = dg.astype(MXU)
        dye = (dyv * ecs).astype(MXU)
        dc_ref[...] = _dot(dgb, bb) + _dot_nt(dye, s_in.astype(MXU))
        db_ref[...] = _dot(dg.T.astype(MXU), cb) + _dot_nt((xd * eend).astype(MXU), dsb)
        ds_scr[...] = jnp.exp(cs_last) * ds_out + _dot(cm.T.astype(MXU), dye)
        last = jnp.sum(ds_out * sout_ref[0], axis=0, keepdims=True)
        rows = _iota((CHUNK, GW), 0)
        dcs = dcs_scr[...] + dyv * yoff - xd * dxd_off + jnp.where(rows == CHUNK - 1, last, 0.0)
        dda = _dot3_r(_tri(CHUNK, lambda r, c: c >= r), dcs)
        ddt_ref[...] = a_ref[...] * dda + dxd * xs
        dal_ref[...] += jnp.sum(dt * dda, axis=0, keepdims=True)
        dxs_ref[...] = dxd * dt

    rc = lambda g, j: (nc - 1 - j, g)
    return pl.pallas_call(
        body,
        name="ssd_bwd",
        out_shape=[
            jax.ShapeDtypeStruct((S, D_INNER), F32),
            jax.ShapeDtypeStruct((S, SSD_GROUPS * D_STATE), F32),
            jax.ShapeDtypeStruct((S, SSD_GROUPS * D_STATE), F32),
            jax.ShapeDtypeStruct((S, D_INNER), F32),
            jax.ShapeDtypeStruct((1, D_INNER), F32),
        ],
        grid=(SSD_GROUPS, nc),
        in_specs=[
            pl.BlockSpec((CHUNK, GW), rc),
            pl.BlockSpec((CHUNK, GW), rc),
            pl.BlockSpec((CHUNK, D_STATE), lambda g, j: (nc - 1 - j, D_INNER // D_STATE + g)),
            pl.BlockSpec((CHUNK, D_STATE), lambda g, j: (nc - 1 - j, D_INNER // D_STATE + SSD_GROUPS + g)),
            pl.BlockSpec((CHUNK, GW), rc),
            pl.BlockSpec((CHUNK, GW), rc),
            pl.BlockSpec((1, D_STATE, GW), lambda g, j: (nc - 1 - j, 0, g)),
            pl.BlockSpec((1, D_STATE, GW), lambda g, j: (jnp.minimum(nc - j, nc - 1), 0, g)),
            pl.BlockSpec((1, GW), lambda g, j: (0, g)),
        ],
        out_specs=[
            pl.BlockSpec((CHUNK, GW), rc),
            pl.BlockSpec((CHUNK, D_STATE), rc),
            pl.BlockSpec((CHUNK, D_STATE), rc),
            pl.BlockSpec((CHUNK, GW), rc),
            pl.BlockSpec((1, GW), lambda g, j: (0, g)),
        ],
        scratch_shapes=[pltpu.VMEM((D_STATE, GW), F32), pltpu.VMEM((CHUNK, GW), F32),
                        pltpu.VMEM((CHUNK, GW), F32)],
        compiler_params=_cparams(("parallel", "arbitrary")),
    )(dy, xbc_act, xbc_act, xbc_act, dte, cse, states, states, a_exp)


SB_TQ = 256
SB_T = 128
SB_DROP = 104.0


def _sb_scores(qm, k_ref, ks, rowi, coli):
    kblk = k_ref[pl.ds(ks, SB_T), :].astype(MXU)
    z = _dot_nt(qm, kblk) * SB_SCALE
    mask = (ks + coli) < rowi
    sp = _softplus(z)
    return kblk, z, mask, sp, jnp.where(mask, sp, 0.0)


def _sb_stack(v):
    lane = _iota(v.shape, 1)
    return jnp.concatenate([jnp.where(lane < HEAD_DIM, v, 0.0), jnp.where(lane >= HEAD_DIM, v, 0.0)], axis=0)


def _sb_unstack(v):
    lane = _iota((SB_TQ, LANES), 1)
    return jnp.where(lane < HEAD_DIM, v[:SB_TQ], v[SB_TQ:])


def _sb_rows(qb):
    r = _iota((2 * SB_TQ, SB_T), 0)
    return qb * SB_TQ + jnp.where(r >= SB_TQ, r - SB_TQ, r), _iota((2 * SB_TQ, SB_T), 1)


def sb_fwd(projmain):
    S = projmain.shape[0]
    nq = S // SB_TQ
    ratio = SB_TQ // SB_T

    def body(q_ref, k_ref, v_ref, o_ref, t_ref, n_ref):
        hp, qb = pl.program_id(0), pl.program_id(1)
        qst = _sb_stack(q_ref[...]).astype(MXU)
        rowi, coli = _sb_rows(qb)
        u_after = _tri(SB_T, lambda r, c: r > c)
        top = (qb + 1) * ratio

        def cond(carry):
            i, rmin, _, _ = carry
            return (i < top) & (rmin < SB_DROP)

        def kstep(carry):
            i, _, r, acc = carry
            ks = pl.multiple_of((top - 1 - i) * SB_T, SB_T)
            _, z, mask, sp, spm = _sb_scores(qst, k_ref, ks, rowi, coli)
            vblk = v_ref[pl.ds(ks, SB_T), :].astype(MXU)
            la = z - sp - _dot3_l(spm, u_after) - r
            a = jnp.where(mask, jnp.exp(la), 0.0)
            acc = acc + _dot(a.astype(MXU), vblk)
            r = r + jnp.sum(spm, axis=1, keepdims=True)
            return i + 1, jnp.min(r), r, acc

        n, _, r, acc = lax.while_loop(
            cond, kstep, (jnp.int32(0), jnp.float32(0.0), jnp.zeros((2 * SB_TQ, 1), F32),
                          jnp.zeros((2 * SB_TQ, LANES), F32)))
        o_ref[...] = _sb_unstack(acc).astype(o_ref.dtype)
        t_ref[...] = _sb_unstack(jnp.broadcast_to(r, (2 * SB_TQ, LANES)))
        n_ref[hp, qb] = n

    return pl.pallas_call(
        body,
        name="sb_fwd",
        out_shape=[jax.ShapeDtypeStruct((S, SB_WIDTH), MXU), jax.ShapeDtypeStruct((S, SB_WIDTH), F32),
                   jax.ShapeDtypeStruct((SB_WIDTH // LANES, nq), jnp.int32)],
        grid=(SB_WIDTH // LANES, nq),
        in_specs=[
            pl.BlockSpec((SB_TQ, LANES), lambda h, i: (i, OFF_Q // LANES + h)),
            pl.BlockSpec((S, LANES), lambda h, i: (0, OFF_K // LANES + h)),
            pl.BlockSpec((S, LANES), lambda h, i: (0, OFF_V // LANES + h)),
        ],
        out_specs=[pl.BlockSpec((SB_TQ, LANES), lambda h, i: (i, h))] * 2
        + [pl.BlockSpec(memory_space=pltpu.SMEM)],
        compiler_params=_cparams(("arbitrary", "arbitrary")),
    )(projmain, projmain, projmain)


def sb_bwd(projmain, do, t_exp, nblk):
    S = projmain.shape[0]
    nq = S // SB_TQ
    ratio = SB_TQ // SB_T

    def body(n_ref, q_ref, k_ref, v_ref, do_ref, t_ref, dq_ref, dk_ref, dv_ref):
        hp, qb = pl.program_id(0), pl.program_id(1)

        @pl.when(qb == 0)
        def _():
            dk_ref[...] = jnp.zeros_like(dk_ref)
            dv_ref[...] = jnp.zeros_like(dv_ref)

        qst = _sb_stack(q_ref[...]).astype(MXU)
        dost = _sb_stack(do_ref[...]).astype(MXU)
        tv = t_ref[...]
        lane = _iota((SB_TQ, LANES), 1)
        tot = jnp.concatenate(
            [jnp.sum(jnp.where(lane == HEAD_DIM * hh, tv, 0.0), axis=1, keepdims=True) for hh in range(2)], axis=0)
        rowi, coli = _sb_rows(qb)
        u_upto = _tri(SB_T, lambda r, c: r <= c)
        u_before = _tri(SB_T, lambda r, c: r < c)
        top = (qb + 1) * ratio
        kept = jnp.clip(n_ref[hp, qb], 0, top)

        def kstep(kb, carry):
            psp, pg, dq = carry
            ks = pl.multiple_of(kb * SB_T, SB_T)
            kblk, z, mask, sp, spm = _sb_scores(qst, k_ref, ks, rowi, coli)
            vblk = v_ref[pl.ds(ks, SB_T), :].astype(MXU)
            after = tot - (psp + _dot3_l(spm, u_upto))
            la = z - sp - after
            a = jnp.where(mask, jnp.exp(la), 0.0)
            gm = _dot_nt(dost, vblk) * a
            before = pg + _dot3_l(gm, u_before)
            sg = jnp.exp(z - sp)
            dz = jnp.where(mask, gm * (1.0 - sg) - sg * before, 0.0) * SB_SCALE
            dzb = dz.astype(MXU)
            dq = dq + _dot(dzb, kblk)
            dk_ref[pl.ds(ks, SB_T), :] += _dot_tn(dzb, qst)
            dv_ref[pl.ds(ks, SB_T), :] += _dot_tn(a.astype(MXU), dost)
            return (psp + jnp.sum(spm, axis=1, keepdims=True),
                    pg + jnp.sum(gm, axis=1, keepdims=True), dq)

        zero1 = jnp.zeros((2 * SB_TQ, 1), F32)
        dq = lax.fori_loop(top - kept, top, kstep, (zero1, zero1, jnp.zeros((2 * SB_TQ, LANES), F32)))[2]
        dq_ref[...] = _sb_unstack(dq).astype(dq_ref.dtype)

    return pl.pallas_call(
        body,
        name="sb_bwd",
        out_shape=[
            jax.ShapeDtypeStruct((S, SB_WIDTH), MXU),
            jax.ShapeDtypeStruct((S, SB_WIDTH), F32),
            jax.ShapeDtypeStruct((S, SB_WIDTH), F32),
        ],
        grid=(SB_WIDTH // LANES, nq),
        in_specs=[
            pl.BlockSpec(memory_space=pltpu.SMEM),
            pl.BlockSpec((SB_TQ, LANES), lambda h, i: (i, OFF_Q // LANES + h)),
            pl.BlockSpec((S, LANES), lambda h, i: (0, OFF_K // LANES + h)),
            pl.BlockSpec((S, LANES), lambda h, i: (0, OFF_V // LANES + h)),
            pl.BlockSpec((SB_TQ, LANES), lambda h, i: (i, h)),
            pl.BlockSpec((SB_TQ, LANES), lambda h, i: (i, h)),
        ],
        out_specs=[
            pl.BlockSpec((SB_TQ, LANES), lambda h, i: (i, h)),
            pl.BlockSpec((S, LANES), lambda h, i: (0, h)),
            pl.BlockSpec((S, LANES), lambda h, i: (0, h)),
        ],
        compiler_params=_cparams(("arbitrary", "arbitrary")),
    )(nblk, projmain, projmain, projmain, do, t_exp)


def local_step(x, pb, target, W, P):
    D = D_MODEL
    full = lambda a, w=D: (a, 0, w)

    (n1b,) = rowwise("norm_pre", lambda xv, g: (_rms(xv, g),), [full(x)], [full(P["norm_mix_pre"])],
                     [(D, D, MXU)])
    projmain = mm("in_proj", n1b, W["main"], "nn")
    gate_pre = mm("gate_proj", n1b, W["gate"], "nn", b_slabs=True)
    dtraw = mm("dt_proj", n1b, W["dt"], "nn")
    xbc_act = conv_fwd(projmain, P["conv_w"], P["conv_b"])
    dte, cse = ssd_prep(dtraw, P["dt_bias_pad"], P["a_exp"])
    y_scan, states = ssd_fwd(xbc_act, dte, cse)

    def f_gate(ysc, xs, z, dsk, nw):
        return _rms((ysc + xs * dsk) * (z * _sig(z)), nw)

    (y_ssd_b,) = rowwise("ssd_gate", lambda *a: (f_gate(*a),),
                         [(y_scan, 0, GW), (xbc_act, 0, GW), (projmain, OFF_Z, GW)],
                         [(P["dsk_exp"], 0, GW), (P["ssd_norm"], 0, GW)], [(D_INNER, GW, MXU)], ncb=SSD_GROUPS)
    y_sb_b, t_exp, sb_kept = sb_fwd(projmain)
    u1 = mm("ssd_branch", y_ssd_b, W["ssd"], "nn")
    u2 = mm("sb_branch", y_sb_b, W["sb"], "nn")

    def f_merge(a1, a2, g1, g2, b1, b2):
        return (_sig(g1 + b1) * a1 + _sig(g2 + b2) * a2,)

    gate_rows = [(gate_pre, 0, D), (gate_pre, D, D)]
    gate_bias = [(P["b_gate"], 0, D), (P["b_gate"], D, D)]
    (merged_b,) = rowwise("merge", f_merge, [full(u1), full(u2)] + gate_rows, gate_bias, [(D, D, MXU)])
    mo = mm("out_proj", merged_b, W["out"], "nn")

    def f_mid(xv, m, gpost, gffn):
        h1 = xv + _rms(m, gpost)
        return h1, _rms(h1, gffn)

    h1, n2b = rowwise("mix_post", f_mid, [full(x), full(mo)],
                      [full(P["norm_mix_post"]), full(P["norm_ffn_pre"])], [(D, D, F32), (D, D, MXU)])
    a_ff = mm("ff1", n2b, W["ff1"], "nn", b_slabs=True)
    (rb,) = rowwise("relu2", lambda a: (jnp.square(jnp.maximum(a, 0.0)),), [(a_ff, 0, D)], [],
                    [(D_FF, D, MXU)], ncb=D_FF // D)
    ff = mm("ff2", rb, W["ff2"], "nn")

    def f_ffn_post(h, f, g):
        h2 = h + _rms(f, g)
        return h2, h2

    h2, h2b = rowwise("ffn_post", f_ffn_post, [full(h1), full(ff)], [full(P["norm_ffn_post"])],
                      [(D, D, F32), (D, D, MXU)])
    pgp = mm("ple_gate", h2b, W["pg"], "nn")
    pe = mm("ple_proj", pb, W["ple"], "nn", b_slabs=True)

    def f_ple(h2v, gp, pev, tgt, g):
        f = lambda h, a, b, gg: h + _rms(_sig(a) * b, gg)
        h3, vjp = jax.vjp(f, h2v, gp, pev, g)
        err = h3 - tgt
        dh2, dgp, dpe, dg = vjp(err * (1.0 / D))
        lossc = (0.5 / D) * jnp.sum(err * err, axis=0, keepdims=True)
        return dh2, dgp, dpe, dg, lossc

    dh2a, dpgp_b, dpe_b, g_ple, lossc = rowwise(
        "ple_loss", f_ple, [full(h2), full(pgp), full(pe), full(target)], [full(P["norm_ple_post"])],
        [(D, D, F32), (D, D, MXU), (D, D, MXU)], [(D, D), (D, D)])
    dh2 = mm("d_ple_gate_x", dpgp_b, W["pg"], "nt", add=dh2a)
    gW = {}
    gW["w_ple_gate"] = mm("d_ple_gate_w", h2b, dpgp_b, "tn")
    gW["w_ple"] = mm("d_ple_w", pb, dpe_b, "tn", out_slabs=True)

    def b_ffn_post(d, f, g):
        _, vjp = jax.vjp(_rms, f, g)
        return vjp(d)

    dff_b, g_ffn_post = rowwise("d_ffn_post", b_ffn_post, [full(dh2), full(ff)], [full(P["norm_ffn_post"])],
                                [(D, D, MXU)], [(D, D)])
    dr = mm("d_ff2_x", dff_b, W["ff2"], "nt")
    gW["w_ff2"] = mm("d_ff2_w", rb, dff_b, "tn")
    (da_b,) = rowwise("d_relu2", lambda d, a: (d * (2.0 * jnp.maximum(a, 0.0)),), [(dr, 0, D), (a_ff, 0, D)], [],
                      [(D_FF, D, MXU)], ncb=D_FF // D)
    dn2 = mm("d_ff1_x", da_b, W["ff1"], "nt", b_slabs=True)
    gW["w_ff1"] = mm("d_ff1_w", n2b, da_b, "tn", out_slabs=True)

    def b_mid(d2, dn, h, m, gpost, gffn):
        _, vjp = jax.vjp(_rms, h, gffn)
        dh, dgffn = vjp(dn)
        dh1 = d2 + dh
        _, vjp2 = jax.vjp(_rms, m, gpost)
        dm, dgpost = vjp2(dh1)
        return dh1, dm, dgpost, dgffn

    dh1, dmo_b, g_mix_post, g_ffn_pre = rowwise(
        "d_mix_post", b_mid, [full(dh2), full(dn2), full(h1), full(mo)],
        [full(P["norm_mix_post"]), full(P["norm_ffn_pre"])], [(D, D, F32), (D, D, MXU)], [(D, D), (D, D)])
    dmerged = mm("d_out_x", dmo_b, W["out"], "nt")
    gW["w_out"] = mm("d_out_w", merged_b, dmo_b, "tn")

    def b_merge(d, a1, a2, g1, g2, b1, b2):
        s1, s2 = _sig(g1 + b1), _sig(g2 + b2)
        dg1 = d * a1 * s1 * (1.0 - s1)
        dg2 = d * a2 * s2 * (1.0 - s2)
        dg = jnp.concatenate([dg1, dg2], axis=1)
        return d * s1, d * s2, dg, jnp.sum(dg, axis=0, keepdims=True)

    du1_b, du2_b, dgp_b, g_b_gate = rowwise(
        "d_merge", b_merge, [full(dmerged), full(u1), full(u2)] + gate_rows, gate_bias,
        [(D, D, MXU), (D, D, MXU), (2 * D, 2 * D, MXU)], [(2 * D, 2 * D)])
    dy_ssd = mm("d_ssd_branch_x", du1_b, W["ssd"], "nt")
    dy_sb = mm("d_sb_branch_x", du2_b, W["sb"], "nt")
    gW["w_ssd_branch"] = mm("d_ssd_branch_w", y_ssd_b, du1_b, "tn")
    gW["w_sb_branch"] = mm("d_sb_branch_w", y_sb_b, du2_b, "tn")

    def b_gate(d, ysc, xs, z, dsk, nw):
        _, vjp = jax.vjp(f_gate, ysc, xs, z, dsk, nw)
        return vjp(d)

    dy_scan, dxs_skip, dz_b, g_dsk_exp, g_ssd_norm = rowwise(
        "d_ssd_gate", b_gate, [(dy_ssd, 0, GW), (y_scan, 0, GW), (xbc_act, 0, GW), (projmain, OFF_Z, GW)],
        [(P["dsk_exp"], 0, GW), (P["ssd_norm"], 0, GW)],
        [(D_INNER, GW, F32), (D_INNER, GW, F32), (D_INNER, GW, MXU)], [(D_INNER, GW), (D_INNER, GW)],
        ncb=SSD_GROUPS)
    dxs, d_b, d_c, ddt_part, g_a_exp = ssd_bwd(dy_scan, xbc_act, dte, cse, states, P["a_exp"])

    def b_dt(dpart, dtr, bias):
        ddt = _dot3_l(dpart, _head_reduce())
        d = ddt * _sig(dtr + bias)
        return d, jnp.sum(d, axis=0, keepdims=True)

    ddt_b, g_dt_bias_pad = rowwise("d_dt", b_dt, [(ddt_part, 0, D_INNER), (dtraw, 0, LANES)],
                                   [(P["dt_bias_pad"], 0, LANES)], [(LANES, LANES, MXU)], [(LANES, LANES)])
    dxbc_act = jnp.concatenate([dxs + dxs_skip, d_b, d_c], axis=1)
    dxbc_b, g_conv_w, g_conv_b = conv_bwd(dxbc_act, projmain, P["conv_w"], P["conv_b"])
    dq_b, dk, dv = sb_bwd(projmain, dy_sb, t_exp, sb_kept)
    dmain_b = jnp.concatenate([dz_b, dxbc_b, dq_b, dk.astype(MXU), dv.astype(MXU)], axis=1)
    dn1_dt = mm("d_dt_x", ddt_b, W["dt"], "nt")
    dn1_gate = mm("d_gate_x", dgp_b, W["gate"], "nt", add=dn1_dt, b_slabs=True)
    dn1 = mm("d_in_x", dmain_b, W["main"], "nt", add=dn1_gate)
    gW["main"] = mm("d_in_w", n1b, dmain_b, "tn")
    gW["dt"] = mm("d_dt_w", n1b, ddt_b, "tn")
    gW["w_gate"] = mm("d_gate_w", n1b, dgp_b, "tn", out_slabs=True)

    def b_pre(d1, dn, xv, g):
        _, vjp = jax.vjp(_rms, xv, g)
        dx, dg = vjp(dn)
        return d1 + dx, dg

    grad_x, g_mix_pre = rowwise("d_norm_pre", b_pre, [full(dh1), full(dn1), full(x)], [full(P["norm_mix_pre"])],
                                [(D, D, F32)], [(D, D)])

    gW["conv_w"] = g_conv_w
    gS = {
        "norm_mix_pre": g_mix_pre, "conv_b": g_conv_b, "dt_bias_pad": g_dt_bias_pad, "a_exp": g_a_exp,
        "dsk_exp": g_dsk_exp, "ssd_norm": g_ssd_norm, "b_gate": g_b_gate,
        "norm_mix_post": g_mix_post, "norm_ffn_pre": g_ffn_pre, "norm_ffn_post": g_ffn_post,
        "norm_ple_post": g_ple,
    }
    return lossc, grad_x, gW, gS


IN_SPLITS = (2048, 6144, 6176, 7200, 8224)
IN_SHARD = 1156
IN_PAD = 1280
COL_SHARDED = {"w_in": (1024, IN_SHARD), "conv_w": (4, 512), "w_gate": (1024, 256), "w_ff1": (1024, 512),
               "w_ple": (256, 128)}
ROW_SHARDED = {"w_ssd_branch": (256, 1024), "w_sb_branch": (128, 1024), "w_out": (128, 1024),
               "w_ff2": (512, 1024), "w_ple_gate": (128, 1024)}
SHARDED = tuple(COL_SHARDED) + tuple(ROW_SHARDED)
SMALL = (
    ("norm_mix_pre", 1024), ("conv_b", 4096), ("dt_bias", 32), ("a_log", 32), ("d_skip", 32), ("ssd_norm", 2048),
    ("b_gate", 2048), ("norm_mix_post", 1024), ("norm_ffn_pre", 1024), ("norm_ffn_post", 1024),
    ("norm_ple_post", 1024),
)
ROW = 1024
SMALL_ROWS = 16


def _rows_of(n):
    return -(-n // ROW)


def _pad_to(v, n, axis=-1):
    pad = [(0, 0)] * v.ndim
    pad[axis] = (0, n - v.shape[axis])
    return jnp.pad(v, pad)


def build_matrices(g):
    w_in = jnp.concatenate([g["w_in"][k, :, :IN_SHARD] for k in range(N_DEV)], axis=1)
    stack = lambda a: a.reshape(a.shape[0] * a.shape[1], a.shape[2])
    return {
        "main": jnp.concatenate([w_in[:, :IN_SPLITS[1]], w_in[:, IN_SPLITS[2]:]], axis=1),
        "dt": _pad_to(w_in[:, IN_SPLITS[1]:IN_SPLITS[2]], LANES),
        "gate": g["w_gate"], "ff1": g["w_ff1"], "ple": g["w_ple"],
        "ssd": stack(g["w_ssd_branch"]), "sb": stack(g["w_sb_branch"]), "out": stack(g["w_out"]),
        "ff2": stack(g["w_ff2"]), "pg": stack(g["w_ple_gate"]),
    }


def build_small(small, conv_w_slabs):
    P = {k: small[k] for k in ("norm_mix_pre", "conv_b", "ssd_norm", "b_gate", "norm_mix_post", "norm_ffn_pre",
                               "norm_ffn_post", "norm_ple_post")}
    P["conv_w"] = jnp.concatenate([conv_w_slabs[k] for k in range(N_DEV)], axis=1)
    P["dt_bias_pad"] = _pad_to(small["dt_bias"], LANES)
    P["a_exp"] = jnp.repeat(-jnp.exp(small["a_log"]), HEAD_DIM, axis=1)
    P["dsk_exp"] = jnp.repeat(small["d_skip"], HEAD_DIM, axis=1)
    return P


def small_grads(gS, small):
    heads = lambda a: a.reshape(SSD_HEADS, HEAD_DIM).sum(axis=1)[None, :]
    out = {k: gS[k] for k in ("norm_mix_pre", "conv_b", "ssd_norm", "b_gate", "norm_mix_post", "norm_ffn_pre",
                              "norm_ffn_post", "norm_ple_post")}
    out["dt_bias"] = gS["dt_bias_pad"][:, :SSD_HEADS]
    out["a_log"] = heads(gS["a_exp"]) * (-jnp.exp(small["a_log"]))
    out["d_skip"] = heads(gS["dsk_exp"])
    return out


def grad_slabs(gW):
    g_in = jnp.concatenate([gW["main"][:, :IN_SPLITS[1]], gW["dt"][:, :SSD_HEADS], gW["main"][:, IN_SPLITS[1]:]],
                           axis=1)
    out = {"w_in": jnp.stack([_pad_to(g_in[:, IN_SHARD * k:IN_SHARD * (k + 1)], IN_PAD) for k in range(N_DEV)])}
    width = COL_SHARDED["conv_w"][1]
    out["conv_w"] = jnp.stack([gW["conv_w"][:, width * k:width * (k + 1)] for k in range(N_DEV)])
    for name in ("w_gate", "w_ff1", "w_ple"):
        out[name] = gW[name]
    for name, shape in ROW_SHARDED.items():
        out[name] = gW[name].reshape((N_DEV,) + shape)
    return out


def pack_small(get):
    cols = [_pad_to(get(name).reshape(n), _rows_of(n) * ROW) for name, n in SMALL]
    return jnp.concatenate(cols).reshape(SMALL_ROWS, ROW)


def unpack_small(flat):
    out, r0 = {}, 0
    for name, n in SMALL:
        rows = _rows_of(n)
        out[name] = flat[r0:r0 + rows].reshape(rows * ROW)[:n].reshape(1, n)
        r0 += rows
    return out


N_CHIPS = 4
JOB_SEMS = {"all": 7, "scatter": 7, "gather2": 7, "pair": 4, "chips": 3}


def exchange(name, jobs):
    n = len(jobs)
    kinds = [k for k, _ in jobs]
    srcs = [s for _, s in jobs]
    shapes = {"all": lambda s: (N_DEV,) + s.shape, "gather2": lambda s: (N_DEV,) + s.shape,
              "scatter": lambda s: s.shape, "pair": lambda s: (N_CHIPS,) + s.shape[1:], "chips": lambda s: s.shape}
    out_shape = [jax.ShapeDtypeStruct(shapes[k](s), s.dtype) for k, s in jobs]
    offs = [sum(JOB_SEMS[k] for k in kinds[:i]) for i in range(n + 1)]

    def body(*refs):
        src, out = refs[:n], refs[n:2 * n]
        send_sems, recv_sems, local_sems = refs[2 * n:]
        x, y, c = lax.axis_index("x"), lax.axis_index("y"), lax.axis_index("c")
        dev = lambda d: 4 * d[0] + 2 * d[1] + d[2]
        chip_no = lambda ch: 2 * ch[0] + ch[1]
        me, sib, my_chip = (x, y, c), (x, y, 1 - c), (x, y)
        others = [(1 - x, y), (x, 1 - y), (1 - x, 1 - y)]
        all_chips = [(0, 0), (0, 1), (1, 0), (1, 1)]
        peers = [(1 - x if k & 4 else x, 1 - y if k & 2 else y, 1 - c if k & 1 else c) for k in range(1, N_DEV)]
        starts, recvs, local = [], [], []
        chained = [[] for _ in others]

        for i, kind in enumerate(kinds):
            s_ref, o_ref = src[i], out[i]

            def rc(k, src_ref, dst_ref, to, i=i):
                s = offs[i] + k
                return pltpu.make_async_remote_copy(src_ref=src_ref, dst_ref=dst_ref, send_sem=send_sems.at[s],
                                                    recv_sem=recv_sems.at[s], device_id=to,
                                                    device_id_type=pl.DeviceIdType.MESH)

            if kind == "all":
                local.append(pltpu.make_async_copy(s_ref, o_ref.at[dev(me)], local_sems.at[i]))
                for k, peer in enumerate(peers):
                    starts.append(rc(k, s_ref, o_ref.at[dev(me)], peer))
                    recvs.append(rc(k, s_ref, o_ref.at[dev(peer)], peer))
            elif kind == "scatter":
                local.append(pltpu.make_async_copy(s_ref.at[dev(me)], o_ref.at[dev(me)], local_sems.at[i]))
                for k, peer in enumerate(peers):
                    starts.append(rc(k, s_ref.at[dev(peer)], o_ref.at[dev(me)], peer))
                    recvs.append(rc(k, s_ref.at[dev(me)], o_ref.at[dev(peer)], peer))
            elif kind == "gather2":
                local.append(pltpu.make_async_copy(s_ref, o_ref.at[dev(me)], local_sems.at[i]))
                starts.append(rc(0, s_ref, o_ref.at[dev(me)], sib))
                recvs.append(rc(0, s_ref, o_ref.at[dev(sib)], sib))
                for j, ch in enumerate(others):
                    same, other = (*ch, c), (*ch, 1 - c)
                    starts.append(rc(1 + j, s_ref, o_ref.at[dev(me)], same))
                    chained[j].append((rc(1 + j, s_ref, o_ref.at[dev(same)], same),
                                       rc(4 + j, o_ref.at[dev(same)], o_ref.at[dev(same)], sib)))
                    recvs.append(rc(4 + j, s_ref, o_ref.at[dev(other)], sib))
            elif kind == "pair":
                for j, ch in enumerate(all_chips):
                    starts.append(rc(j, s_ref.at[dev((*ch, 1 - c))], o_ref.at[j], sib))
                    recvs.append(rc(j, s_ref.at[dev((*ch, c))], o_ref.at[j], sib))
            else:
                mine = chip_no(my_chip)
                local.append(pltpu.make_async_copy(s_ref.at[mine], o_ref.at[mine], local_sems.at[i]))
                for j, ch in enumerate(others):
                    starts.append(rc(j, s_ref.at[chip_no(ch)], o_ref.at[mine], (*ch, c)))
                    recvs.append(rc(j, s_ref.at[mine], o_ref.at[chip_no(ch)], (*ch, c)))

        for cp in local + starts:
            cp.start()
        passed = []
        for group in chained:
            for arrival, forward in group:
                arrival.wait_recv()
                forward.start()
                passed.append(forward)
        for cp in recvs:
            cp.wait_recv()
        for cp in starts + passed:
            cp.wait_send()
        for cp in local:
            cp.wait()

    any_spec = pl.BlockSpec(memory_space=pl.ANY)
    return pl.pallas_call(
        body,
        name=name,
        out_shape=out_shape,
        in_specs=[any_spec] * n,
        out_specs=[any_spec] * n,
        scratch_shapes=[
            pltpu.SemaphoreType.DMA((offs[n],)),
            pltpu.SemaphoreType.DMA((offs[n],)),
            pltpu.SemaphoreType.DMA((n,)),
        ],
    )(*srcs)


def pair_sum(name, g, sib, core):
    _, r, c = g.shape
    tr = _pick(r, (256, 128, 64, 32, 16, 8))

    def body(core_ref, g_ref, s_ref, o_ref):
        o_ref[...] = (g_ref[...] + s_ref[...]).astype(o_ref.dtype)

    return pl.pallas_call(
        body,
        name=name,
        out_shape=jax.ShapeDtypeStruct((N_CHIPS, r, c), MXU),
        grid_spec=pltpu.PrefetchScalarGridSpec(
            num_scalar_prefetch=1,
            grid=(N_CHIPS, r // tr),
            in_specs=[
                pl.BlockSpec((None, None, tr, c), lambda j, i, core_ref: (j, core_ref[0], i, 0)),
                pl.BlockSpec((None, tr, c), lambda j, i, core_ref: (j, i, 0)),
            ],
            out_specs=pl.BlockSpec((None, tr, c), lambda j, i, core_ref: (j, i, 0)),
        ),
        compiler_params=_cparams(("parallel", "parallel")),
    )(core, g.reshape(N_CHIPS, 2, r, c), sib)


def adamw(name, parts, w, m, v):
    rows, cols = w.shape
    tr = _pick(rows, (128, 64, 32, 16, 8))

    nparts = parts.shape[0]

    def body(p_ref, w_ref, m_ref, v_ref, g_ref, d_ref, m2_ref, v2_ref):
        g = p_ref[0].astype(F32)
        for k in range(1, nparts):
            g = g + p_ref[k].astype(F32)
        m2 = ADAM_B1 * m_ref[...] + (1.0 - ADAM_B1) * g
        v2 = ADAM_B2 * v_ref[...] + (1.0 - ADAM_B2) * jnp.square(g)
        m_hat = m2 / (1.0 - ADAM_B1 ** ADAM_STEP)
        v_hat = v2 / (1.0 - ADAM_B2 ** ADAM_STEP)
        g_ref[...] = g
        d_ref[...] = -ADAM_LR * (m_hat / (jnp.sqrt(v_hat) + ADAM_EPS) + ADAM_WD * w_ref[...])
        m2_ref[...] = m2
        v2_ref[...] = v2

    spec = pl.BlockSpec((tr, cols), lambda i: (i, 0))
    return pl.pallas_call(
        body,
        name=name,
        out_shape=[jax.ShapeDtypeStruct((rows, cols), F32)] * 4,
        grid=(rows // tr,),
        in_specs=[pl.BlockSpec((nparts, tr, cols), lambda i: (0, i, 0)), spec, spec, spec],
        out_specs=[spec] * 4,
        compiler_params=_cparams(("parallel",)),
    )(parts, w, m, v)


WEIGHT_ORDER = (
    "norm_mix_pre", "w_in", "conv_w", "conv_b", "dt_bias", "a_log", "d_skip", "ssd_norm", "w_ssd_branch",
    "w_sb_branch", "w_gate", "b_gate", "w_out", "norm_mix_post", "norm_ffn_pre", "w_ff1", "w_ff2", "norm_ffn_post",
    "w_ple", "w_ple_gate", "norm_ple_post",
)


def kernel(x, p, norm_mix_pre, w_in, conv_w, conv_b, dt_bias, a_log, d_skip, ssd_norm, w_ssd_branch, w_sb_branch, w_gate, b_gate, w_out, norm_mix_post, norm_ffn_pre, w_ff1, w_ff2, norm_ffn_post, w_ple, w_ple_gate, norm_ple_post, loss_target, m_norm_mix_pre, m_w_in, m_conv_w, m_conv_b, m_dt_bias, m_a_log, m_d_skip, m_ssd_norm, m_w_ssd_branch, m_w_sb_branch, m_w_gate, m_b_gate, m_w_out, m_norm_mix_post, m_norm_ffn_pre, m_w_ff1, m_w_ff2, m_norm_ffn_post, m_w_ple, m_w_ple_gate, m_norm_ple_post, v_norm_mix_pre, v_w_in, v_conv_w, v_conv_b, v_dt_bias, v_a_log, v_d_skip, v_ssd_norm, v_w_ssd_branch, v_w_sb_branch, v_w_gate, v_b_gate, v_w_out, v_norm_mix_post, v_norm_ffn_pre, v_w_ff1, v_w_ff2, v_norm_ffn_post, v_w_ple, v_w_ple_gate, v_norm_ple_post):
    a = dict(locals())
    seq = x.shape[1]
    x2 = x.reshape(seq, D_MODEL)
    target = loss_target.reshape(seq, D_MODEL)
    pb = p.reshape(seq, PLE_DIM).astype(MXU)

    def shard(prefix, name, dtype):
        v = a[prefix + name][0].astype(dtype)
        return _pad_to(v, IN_PAD) if name == "w_in" else v

    names = [n for n in SHARDED if n != "conv_w"]
    got = exchange("gather_weights", [("gather2", shard("", n, MXU)) for n in names] + [("all", a["conv_w"][0])])
    small = {n: a[n] for n, _ in SMALL}

    lossc, grad_x, g_full, g_acc = local_step(x2, pb, target, build_matrices(dict(zip(names, got))),
                                              build_small(small, got[-1]))
    loss = lax.psum(jnp.sum(lossc), ("x", "y", "c"))
    g_small = small_grads(g_acc, small)

    slabs = grad_slabs(g_full)
    core = lax.axis_index("c").astype(jnp.int32).reshape(1)
    from_sibling = exchange("pair_grads", [("pair", slabs[n]) for n in names])
    chip_sums = [pair_sum("pair_sum_" + n, slabs[n], s, core) for n, s in zip(names, from_sibling)]
    parts = exchange("chip_grads", [("chips", s) for s in chip_sums]
                     + [("scatter", slabs["conv_w"]), ("all", pack_small(lambda n: g_small[n]))])

    leaves = {}
    for n, part in zip(names + ["conv_w"], parts):
        res = adamw("adamw_" + n, part, shard("", n, F32), shard("m_", n, F32), shard("v_", n, F32))
        leaves[n] = [r[None, :, :IN_SHARD] if n == "w_in" else r[None] for r in res]
    res = adamw("adamw_small", parts[-1], pack_small(lambda n: a[n]), pack_small(lambda n: a["m_" + n]),
                pack_small(lambda n: a["v_" + n]))
    for j, r in enumerate(res):
        for n, leaf in unpack_small(r).items():
            leaves.setdefault(n, [None] * 4)[j] = leaf
    outs = [loss, grad_x.reshape(x.shape)]
    for j in range(4):
        outs += [leaves[n][j] for n in WEIGHT_ORDER]
    return tuple(outs)
```

```python
import functools

import jax
import jax.numpy as jnp
from jax import lax
from jax.experimental import pallas as pl
from jax.experimental.pallas import tpu as pltpu

F32 = jnp.float32
MXU = jnp.bfloat16
VMEM_LIMIT = 56 * 1024 * 1024

D_MODEL = 1024
D_INNER = 2048
SSD_HEADS = 32
HEAD_DIM = 64
SSD_GROUPS = 8
D_STATE = 128
CONV_K = 4
CONV_DIM = 4096
CHUNK = 128
SB_WIDTH = 1024
D_FF = 4096
PLE_DIM = 256
RMS_EPS = 1e-6
SB_SCALE = HEAD_DIM ** -0.5
N_DEV = 8
LANES = 128

OFF_Z, OFF_XBC, OFF_Q, OFF_K, OFF_V = 0, 2048, 6144, 7168, 8192

ADAM_LR = 0.001
ADAM_B1 = 0.9
ADAM_B2 = 0.999
ADAM_EPS = 1e-08
ADAM_WD = 0.01
ADAM_STEP = 10


def _sig(x):
    return 1.0 / (1.0 + jnp.exp(-x))


def _softplus(x):
    return jnp.maximum(x, 0.0) + jnp.log(1.0 + jnp.exp(-jnp.abs(x)))


def _rms(x, w):
    return x * lax.rsqrt(jnp.mean(x * x, axis=-1, keepdims=True) + RMS_EPS) * w


def _dot(a, b):
    return jnp.dot(a, b, preferred_element_type=F32)


def _dot_nt(a, b):
    return lax.dot_general(a, b, (((1,), (1,)), ((), ())), preferred_element_type=F32)


def _dot_tn(a, b):
    return lax.dot_general(a, b, (((0,), (0,)), ((), ())), preferred_element_type=F32)


def _split3(x):
    x1 = x.astype(MXU)
    r = x - x1.astype(F32)
    x2 = r.astype(MXU)
    r = r - x2.astype(F32)
    return x1, x2, r.astype(MXU)


def _dot3_l(a, u):
    m = a.shape[0]
    d = _dot(jnp.concatenate(_split3(a), axis=0), u)
    return (d[2 * m:] + d[m:2 * m]) + d[:m]


def _dot3_r(u, a):
    n = a.shape[1]
    d = _dot(u, jnp.concatenate(_split3(a), axis=1))
    return (d[:, 2 * n:] + d[:, n:2 * n]) + d[:, :n]


def _iota(shape, dim):
    return lax.broadcasted_iota(jnp.int32, shape, dim)


def _tri(n, cmp):
    r, c = _iota((n, n), 0), _iota((n, n), 1)
    return cmp(r, c).astype(F32).astype(MXU)


def _cparams(sem):
    return pltpu.CompilerParams(dimension_semantics=sem, vmem_limit_bytes=VMEM_LIMIT)


def _pick(n, cands):
    for c in cands:
        if n % c == 0:
            return c
    return n


def mm(name, a, b, mode, add=None, out_dtype=F32, b_slabs=False, out_slabs=False, epi=None, epi_args=(), extra=()):
    slab = None
    if b_slabs:
        slab = b.shape[2]
        bshape = (b.shape[1], N_DEV * slab)
    else:
        bshape = b.shape
    if mode == "nn":
        (M, K), (K2, N) = a.shape, bshape
    elif mode == "nt":
        (M, K), (N, K2) = a.shape, bshape
    else:
        (K, M), (K2, N) = a.shape, bshape
    assert K == K2, (name, a.shape, b.shape)
    tm = _pick(M, (1024, 512, 256, 128))
    tn = _pick(N, (1024, 512, 256, 128))
    tk = _pick(K, (1024, 512, 256, 128))
    if b_slabs and mode == "nn":
        tn = slab
    if b_slabs and mode == "nt":
        tk = slab
    if out_slabs:
        assert mode == "tn" and N % N_DEV == 0
        tn = N // N_DEV
    nk = K // tk

    def body(*refs):
        refs = list(refs)
        a_ref, b_ref = refs[:2]
        add_ref = refs[2] if add is not None else None
        n_in = 2 + (add is not None)
        epi_refs = refs[n_in:n_in + len(epi_args)]
        o_ref = refs[n_in + len(epi_args)]
        extra_refs = refs[n_in + len(epi_args) + 1:-1]
        acc = refs[-1]
        k = pl.program_id(2)

        @pl.when(k == 0)
        def _():
            acc[...] = jnp.zeros_like(acc) if add is None else add_ref[...]

        av, bv = a_ref[...], b_ref[...]
        if mode == "nn":
            acc[...] += _dot(av, bv)
        elif mode == "nt":
            acc[...] += _dot_nt(av, bv)
        else:
            acc[...] += _dot_tn(av, bv)

        @pl.when(k == nk - 1)
        def _():
            res = acc[...]
            main = res if epi is None else epi(res, *[r[...] for r in epi_refs])
            o_ref[...] = main.astype(o_ref.dtype)
            for r, (_, fn) in zip(extra_refs, extra):
                r[...] = fn(res).astype(r.dtype)

    if mode == "nn":
        a_spec = pl.BlockSpec((tm, tk), lambda i, j, k: (i, k))
        b_spec = pl.BlockSpec((tk, tn), lambda i, j, k: (k, j))
    elif mode == "nt":
        a_spec = pl.BlockSpec((tm, tk), lambda i, j, k: (i, k))
        b_spec = pl.BlockSpec((tn, tk), lambda i, j, k: (j, k))
    else:
        a_spec = pl.BlockSpec((tk, tm), lambda i, j, k: (k, i))
        b_spec = pl.BlockSpec((tk, tn), lambda i, j, k: (k, j))
    if b_slabs and mode == "nn":
        b_spec = pl.BlockSpec((None, tk, tn), lambda i, j, k: (j, k, 0))
    if b_slabs and mode == "nt":
        b_spec = pl.BlockSpec((None, tn, tk), lambda i, j, k: (k, j, 0))
    o_spec = pl.BlockSpec((tm, tn), lambda i, j, k: (i, j))
    in_specs, args = [a_spec, b_spec], [a, b]
    if add is not None:
        in_specs.append(o_spec)
        args.append(add)
    for e in epi_args:
        in_specs.append(o_spec)
        args.append(e)
    out_sds = jax.ShapeDtypeStruct((M, N), out_dtype)
    if out_slabs:
        o_spec = pl.BlockSpec((None, tm, tn), lambda i, j, k: (j, i, 0))
        out_sds = jax.ShapeDtypeStruct((N_DEV, M, tn), out_dtype)
    res = pl.pallas_call(
        body,
        name=name,
        out_shape=[out_sds] + [jax.ShapeDtypeStruct((M, N), dt) for dt, _ in extra],
        grid=(M // tm, N // tn, nk),
        in_specs=in_specs,
        out_specs=[o_spec] * (1 + len(extra)),
        scratch_shapes=[pltpu.VMEM((tm, tn), F32)],
        compiler_params=_cparams(("parallel", "parallel", "arbitrary")),
    )(*args)
    return res if extra else res[0]


def rowwise(name, fn, rows, bcast, outs, accs=(), tr=256, ncb=1):
    S = rows[0][0].shape[0]
    tr = min(tr, S)
    nrb = S // tr
    in_specs, args = [], []
    for arr, off, w in rows:
        assert off % w == 0 and arr.shape[0] == S
        in_specs.append(pl.BlockSpec((tr, w), lambda j, i, ob=off // w: (i, ob + j)))
        args.append(arr)
    for arr, off, w in bcast:
        assert off % w == 0
        in_specs.append(pl.BlockSpec((arr.shape[0], w), lambda j, i, ob=off // w: (0, ob + j)))
        args.append(arr)
    out_shape, out_specs = [], []
    for tw, w, dt in outs:
        out_shape.append(jax.ShapeDtypeStruct((S, tw), dt))
        out_specs.append(pl.BlockSpec((tr, w), lambda j, i: (i, j)))
    for tw, w in accs:
        out_shape.append(jax.ShapeDtypeStruct((1, tw), F32))
        out_specs.append(pl.BlockSpec((1, w), lambda j, i: (0, j)))
    nin, nout = len(args), len(outs)

    def body(*refs):
        res = fn(*[r[...] for r in refs[:nin]])
        o_refs, a_refs = refs[nin:nin + nout], refs[nin + nout:]
        for r, v in zip(o_refs, res[:nout]):
            r[...] = v.astype(r.dtype)
        i = pl.program_id(1)
        for r, v in zip(a_refs, res[nout:]):
            @pl.when(i == 0)
            def _(r=r, v=v):
                r[...] = v

            @pl.when(i > 0)
            def _(r=r, v=v):
                r[...] += v

    res = pl.pallas_call(
        body,
        name=name,
        out_shape=out_shape,
        grid=(ncb, nrb),
        in_specs=in_specs,
        out_specs=out_specs,
        compiler_params=_cparams(("parallel", "arbitrary")),
    )(*args)
    return res


CONV_TC = 128


def _conv_pre(u, w, b, row):
    pre = b + w[3:4, :] * u
    shifted = []
    for s in (1, 2, 3):
        us = jnp.where(row >= s, pltpu.roll(u, s, 0), 0.0)
        shifted.append(us)
        pre = pre + w[3 - s:4 - s, :] * us
    return pre, shifted


def conv_fwd(projmain, conv_w, conv_b):
    S = projmain.shape[0]
    tc = CONV_TC

    def body(u_ref, w_ref, b_ref, o_ref):
        u = u_ref[...]
        row = _iota(u.shape, 0)
        pre, _ = _conv_pre(u, w_ref[...], b_ref[...], row)
        o_ref[...] = pre * _sig(pre)

    return pl.pallas_call(
        body,
        name="conv_fwd",
        out_shape=jax.ShapeDtypeStruct((S, CONV_DIM), F32),
        grid=(CONV_DIM // tc,),
        in_specs=[
            pl.BlockSpec((S, tc), lambda j: (0, OFF_XBC // tc + j)),
            pl.BlockSpec((CONV_K, tc), lambda j: (0, j)),
            pl.BlockSpec((1, tc), lambda j: (0, j)),
        ],
        out_specs=pl.BlockSpec((S, tc), lambda j: (0, j)),
        compiler_params=_cparams(("parallel",)),
    )(projmain, conv_w, conv_b)


def conv_bwd(dact, projmain, conv_w, conv_b):
    S = projmain.shape[0]
    tc = CONV_TC

    def body(d_ref, u_ref, w_ref, b_ref, du_ref, dw_ref, db_ref):
        u = u_ref[...]
        w = w_ref[...]
        row = _iota(u.shape, 0)
        pre, shifted = _conv_pre(u, w, b_ref[...], row)
        sg = _sig(pre)
        dpre = d_ref[...] * (sg * (1.0 + pre * (1.0 - sg)))
        du = w[3:4, :] * dpre
        dw_ref[3:4, :] = jnp.sum(dpre * u, axis=0, keepdims=True)
        for s in (1, 2, 3):
            ds = jnp.where(row < S - s, pltpu.roll(dpre, S - s, 0), 0.0)
            du = du + w[3 - s:4 - s, :] * ds
            dw_ref[3 - s:4 - s, :] = jnp.sum(dpre * shifted[s - 1], axis=0, keepdims=True)
        du_ref[...] = du.astype(du_ref.dtype)
        db_ref[...] = jnp.sum(dpre, axis=0, keepdims=True)

    return pl.pallas_call(
        body,
        name="conv_bwd",
        out_shape=[
            jax.ShapeDtypeStruct((S, CONV_DIM), MXU),
            jax.ShapeDtypeStruct((CONV_K, CONV_DIM), F32),
            jax.ShapeDtypeStruct((1, CONV_DIM), F32),
        ],
        grid=(CONV_DIM // tc,),
        in_specs=[
            pl.BlockSpec((S, tc), lambda j: (0, j)),
            pl.BlockSpec((S, tc), lambda j: (0, OFF_XBC // tc + j)),
            pl.BlockSpec((CONV_K, tc), lambda j: (0, j)),
            pl.BlockSpec((1, tc), lambda j: (0, j)),
        ],
        out_specs=[
            pl.BlockSpec((S, tc), lambda j: (0, j)),
            pl.BlockSpec((CONV_K, tc), lambda j: (0, j)),
            pl.BlockSpec((1, tc), lambda j: (0, j)),
        ],
        compiler_params=_cparams(("parallel",)),
    )(dact, projmain, conv_w, conv_b)


def _head_expand():
    r, j = _iota((LANES, D_INNER), 0), _iota((LANES, D_INNER), 1)
    return ((j >= r * HEAD_DIM) & (j < r * HEAD_DIM + HEAD_DIM)).astype(F32).astype(MXU)


def _head_reduce():
    j, r = _iota((D_INNER, LANES), 0), _iota((D_INNER, LANES), 1)
    return ((j >= r * HEAD_DIM) & (j < r * HEAD_DIM + HEAD_DIM)).astype(F32).astype(MXU)


def ssd_prep(dtraw, dt_bias_pad, a_exp):
    S = dtraw.shape[0]

    def body(dtr_ref, bias_ref, a_ref, dte_ref, cse_ref):
        dt = _softplus(dtr_ref[...] + bias_ref[...])
        dte = _dot3_l(dt, _head_expand())
        dte_ref[...] = dte
        incl = _tri(CHUNK, lambda r, c: r >= c)
        cse_ref[...] = _dot3_r(incl, dte * a_ref[...])

    return pl.pallas_call(
        body,
        name="ssd_prep",
        out_shape=[jax.ShapeDtypeStruct((S, D_INNER), F32)] * 2,
        grid=(S // CHUNK,),
        in_specs=[
            pl.BlockSpec((CHUNK, LANES), lambda c: (c, 0)),
            pl.BlockSpec((1, LANES), lambda c: (0, 0)),
            pl.BlockSpec((1, D_INNER), lambda c: (0, 0)),
        ],
        out_specs=[pl.BlockSpec((CHUNK, D_INNER), lambda c: (c, 0))] * 2,
        compiler_params=_cparams(("parallel",)),
    )(dtraw, dt_bias_pad, a_exp)


GW = 4 * HEAD_DIM


def ssd_fwd(xbc_act, dte, cse):
    S = xbc_act.shape[0]
    nc = S // CHUNK

    def body(xs_ref, b_ref, c_ref, dte_ref, cse_ref, y_ref, st_ref, s_scr):
        c = pl.program_id(1)

        @pl.when(c == 0)
        def _():
            s_scr[...] = jnp.zeros_like(s_scr)

        s_in = s_scr[...]
        st_ref[0] = s_in
        cs = cse_ref[...]
        xd = xs_ref[...] * dte_ref[...]
        btb = b_ref[...].T.astype(MXU)
        cs_last = cs[CHUNK - 1:CHUNK, :]
        gy = _dot(c_ref[...].astype(MXU), jnp.concatenate([btb, s_in.astype(MXU)], axis=1))
        g, y = gy[:, :CHUNK], gy[:, CHUNK:] * jnp.exp(cs)
        cs_t = cs.T
        row, col = _iota((CHUNK, CHUNK), 0), _iota((CHUNK, CHUNK), 1)
        ms = []
        for h in range(4):
            lo = HEAD_DIM * h
            lam = jnp.where(row >= col, jnp.exp(cs[:, lo:lo + 1] - cs_t[lo:lo + 1, :]), 0.0)
            ms.append((g * lam).astype(MXU))
        yd = _dot(jnp.concatenate(ms, axis=0), xd.astype(MXU))
        lane = _iota((CHUNK, GW), 1)
        for h in range(4):
            y = y + jnp.where((lane >= HEAD_DIM * h) & (lane < HEAD_DIM * (h + 1)), yd[CHUNK * h:CHUNK * (h + 1)], 0.0)
        y_ref[...] = y
        w = (xd * jnp.exp(cs_last - cs)).astype(MXU)
        s_scr[...] = jnp.exp(cs_last) * s_in + _dot(btb, w)

    return pl.pallas_call(
        body,
        name="ssd_fwd",
        out_shape=[
            jax.ShapeDtypeStruct((S, D_INNER), F32),
            jax.ShapeDtypeStruct((nc, D_STATE, D_INNER), F32),
        ],
        grid=(SSD_GROUPS, nc),
        in_specs=[
            pl.BlockSpec((CHUNK, GW), lambda g, c: (c, g)),
            pl.BlockSpec((CHUNK, D_STATE), lambda g, c: (c, D_INNER // D_STATE + g)),
            pl.BlockSpec((CHUNK, D_STATE), lambda g, c: (c, D_INNER // D_STATE + SSD_GROUPS + g)),
            pl.BlockSpec((CHUNK, GW), lambda g, c: (c, g)),
            pl.BlockSpec((CHUNK, GW), lambda g, c: (c, g)),
        ],
        out_specs=[
            pl.BlockSpec((CHUNK, GW), lambda g, c: (c, g)),
            pl.BlockSpec((1, D_STATE, GW), lambda g, c: (c, 0, g)),
        ],
        scratch_shapes=[pltpu.VMEM((D_STATE, GW), F32)],
        compiler_params=_cparams(("parallel", "arbitrary")),
    )(xbc_act, xbc_act, xbc_act, dte, cse)


def ssd_bwd(dy, xbc_act, dte, cse, states, a_exp):
    S = xbc_act.shape[0]
    nc = S // CHUNK

    def body(dy_ref, xs_ref, b_ref, c_ref, dte_ref, cse_ref, sin_ref, sout_ref, a_ref,
             dxs_ref, db_ref, dc_ref, ddt_ref, dal_ref, ds_scr):
        j = pl.program_id(1)

        @pl.when(j == 0)
        def _():
            ds_scr[...] = jnp.zeros_like(ds_scr)
            dal_ref[...] = jnp.zeros_like(dal_ref)

        ds_out = ds_scr[...]
        dyv, xs = dy_ref[...], xs_ref[...]
        dt, cs = dte_ref[...], cse_ref[...]
        s_in = sin_ref[0]
        bm, cm = b_ref[...], c_ref[...]
        bb, cb = bm.astype(MXU), cm.astype(MXU)
        btb, ctb = bm.T.astype(MXU), cm.T.astype(MXU)
        dsb, sib = ds_out.astype(MXU), s_in.astype(MXU)
        xd = xs * dt
        ecs = jnp.exp(cs)
        cs_last = cs[CHUNK - 1:CHUNK, :]
        eend = jnp.exp(cs_last - cs)
        gy = _dot(cb, jnp.concatenate([btb, sib], axis=1))
        g, yoff = gy[:, :CHUNK], gy[:, CHUNK:] * ecs
        gd = _dot(bb, jnp.concatenate([ctb, dsb], axis=1))
        g_t, dxd_off = gd[:, :CHUNK], gd[:, CHUNK:] * eend
        cs_t = cs.T
        row, col = _iota((CHUNK, CHUNK), 0), _iota((CHUNK, CHUNK), 1)
        lane = _iota((CHUNK, GW), 1)
        heads = [(lane >= HEAD_DIM * h) & (lane < HEAD_DIM * (h + 1)) for h in range(4)]
        dyb, xdb = dyv.astype(MXU), xd.astype(MXU)
        dm_all = _dot_nt(jnp.concatenate([jnp.where(hm, dyv, 0.0) for hm in heads], axis=0).astype(MXU), xdb)
        dmt_all = _dot_nt(jnp.concatenate([jnp.where(hm, xd, 0.0) for hm in heads], axis=0).astype(MXU), dyb)
        lams, m_ts = [], []
        for h in range(4):
            lo = HEAD_DIM * h
            cs_col, cs_row = cs[:, lo:lo + 1], cs_t[lo:lo + 1, :]
            lams.append(jnp.where(row >= col, jnp.exp(cs_col - cs_row), 0.0))
            m_ts.append(g_t * jnp.where(col >= row, jnp.exp(cs_row - cs_col), 0.0))
        acc_all = _dot(jnp.concatenate(m_ts, axis=0).astype(MXU), dyb)
        dxd = dxd_off
        dg = jnp.zeros((CHUNK, CHUNK), F32)
        dcs = dyv * yoff - xd * dxd_off
        for h in range(4):
            blk = slice(CHUNK * h, CHUNK * (h + 1))
            dm, dm_t = dm_all[blk], dmt_all[blk]
            dxd = dxd + jnp.where(heads[h], acc_all[blk], 0.0)
            dg = dg + dm * lams[h]
            wdiff = (jnp.sum(dm * (g * lams[h]), axis=1, keepdims=True)
                     - jnp.sum(dm_t * m_ts[h], axis=1, keepdims=True))
            dcs = dcs + jnp.where(lane == HEAD_DIM * h, wdiff, 0.0)
        dye = (dyv * ecs).astype(MXU)
        dc_ref[...] = _dot(dg.astype(MXU), bb) + _dot_nt(dye, sib)
        db_ref[...] = _dot(dg.T.astype(MXU), cb) + _dot_nt((xd * eend).astype(MXU), dsb)
        ds_scr[...] = jnp.exp(cs_last) * ds_out + _dot(ctb, dye)
        last = jnp.sum(ds_out * sout_ref[0], axis=0, keepdims=True)
        rows = _iota((CHUNK, GW), 0)
        dcs = dcs + jnp.where(rows == CHUNK - 1, last, 0.0)
        dda = _dot3_r(_tri(CHUNK, lambda r, c: c >= r), dcs)
        ddt_ref[...] = a_ref[...] * dda + dxd * xs
        dal_ref[...] += jnp.sum(dt * dda, axis=0, keepdims=True)
        dxs_ref[...] = dxd * dt

    rc = lambda g, j: (nc - 1 - j, g)
    return pl.pallas_call(
        body,
        name="ssd_bwd",
        out_shape=[
            jax.ShapeDtypeStruct((S, D_INNER), F32),
            jax.ShapeDtypeStruct((S, SSD_GROUPS * D_STATE), F32),
            jax.ShapeDtypeStruct((S, SSD_GROUPS * D_STATE), F32),
            jax.ShapeDtypeStruct((S, D_INNER), F32),
            jax.ShapeDtypeStruct((1, D_INNER), F32),
        ],
        grid=(SSD_GROUPS, nc),
        in_specs=[
            pl.BlockSpec((CHUNK, GW), rc),
            pl.BlockSpec((CHUNK, GW), rc),
            pl.BlockSpec((CHUNK, D_STATE), lambda g, j: (nc - 1 - j, D_INNER // D_STATE + g)),
            pl.BlockSpec((CHUNK, D_STATE), lambda g, j: (nc - 1 - j, D_INNER // D_STATE + SSD_GROUPS + g)),
            pl.BlockSpec((CHUNK, GW), rc),
            pl.BlockSpec((CHUNK, GW), rc),
            pl.BlockSpec((1, D_STATE, GW), lambda g, j: (nc - 1 - j, 0, g)),
            pl.BlockSpec((1, D_STATE, GW), lambda g, j: (jnp.minimum(nc - j, nc - 1), 0, g)),
            pl.BlockSpec((1, GW), lambda g, j: (0, g)),
        ],
        out_specs=[
            pl.BlockSpec((CHUNK, GW), rc),
            pl.BlockSpec((CHUNK, D_STATE), rc),
            pl.BlockSpec((CHUNK, D_STATE), rc),
            pl.BlockSpec((CHUNK, GW), rc),
            pl.BlockSpec((1, GW), lambda g, j: (0, g)),
        ],
        scratch_shapes=[pltpu.VMEM((D_STATE, GW), F32)],
        compiler_params=_cparams(("parallel", "arbitrary")),
    )(dy, xbc_act, xbc_act, xbc_act, dte, cse, states, states, a_exp)


SB_T = 256
SB_DROP = 104.0


def _sb_scores(qm, k_ref, ks, rowi, coli):
    kblk = k_ref[pl.ds(ks, SB_T), :].astype(MXU)
    z = _dot_nt(qm, kblk) * SB_SCALE
    mask = (ks + coli) < rowi
    sp = _softplus(z)
    return kblk, z, mask, sp, jnp.where(mask, sp, 0.0)


def _sb_stack(v):
    lane = _iota(v.shape, 1)
    return jnp.concatenate([jnp.where(lane < HEAD_DIM, v, 0.0), jnp.where(lane >= HEAD_DIM, v, 0.0)], axis=0)


def _sb_unstack(v):
    lane = _iota((SB_T, LANES), 1)
    return jnp.where(lane < HEAD_DIM, v[:SB_T], v[SB_T:])


def _sb_rows(qb):
    r = _iota((2 * SB_T, SB_T), 0)
    return qb * SB_T + jnp.where(r >= SB_T, r - SB_T, r), _iota((2 * SB_T, SB_T), 1)


def sb_fwd(projmain):
    S = projmain.shape[0]
    nq = S // SB_T

    def body(q_ref, k_ref, v_ref, o_ref, t_ref, n_ref):
        hp, qb = pl.program_id(0), pl.program_id(1)
        qst = _sb_stack(q_ref[...]).astype(MXU)
        rowi, coli = _sb_rows(qb)
        u_after = _tri(SB_T, lambda r, c: r > c)

        def cond(carry):
            i, rmin, _, _ = carry
            return (i <= qb) & (rmin < SB_DROP)

        def kstep(carry):
            i, _, r, acc = carry
            ks = pl.multiple_of((qb - i) * SB_T, SB_T)
            _, z, mask, sp, spm = _sb_scores(qst, k_ref, ks, rowi, coli)
            vblk = v_ref[pl.ds(ks, SB_T), :].astype(MXU)
            a = jnp.where(mask, jnp.exp(z - sp - _dot3_l(spm, u_after) - r), 0.0)
            acc = acc + _dot(a.astype(MXU), vblk)
            r = r + jnp.sum(spm, axis=1, keepdims=True)
            return i + 1, jnp.min(r), r, acc

        n, _, r, acc = lax.while_loop(
            cond, kstep, (jnp.int32(0), jnp.float32(0.0), jnp.zeros((2 * SB_T, 1), F32),
                          jnp.zeros((2 * SB_T, LANES), F32)))
        o_ref[...] = _sb_unstack(acc).astype(o_ref.dtype)
        t_ref[...] = _sb_unstack(jnp.broadcast_to(r, (2 * SB_T, LANES)))
        n_ref[hp, qb] = n

    return pl.pallas_call(
        body,
        name="sb_fwd",
        out_shape=[jax.ShapeDtypeStruct((S, SB_WIDTH), MXU), jax.ShapeDtypeStruct((S, SB_WIDTH), F32),
                   jax.ShapeDtypeStruct((SB_WIDTH // LANES, nq), jnp.int32)],
        grid=(SB_WIDTH // LANES, nq),
        in_specs=[
            pl.BlockSpec((SB_T, LANES), lambda h, i: (i, OFF_Q // LANES + h)),
            pl.BlockSpec((S, LANES), lambda h, i: (0, OFF_K // LANES + h)),
            pl.BlockSpec((S, LANES), lambda h, i: (0, OFF_V // LANES + h)),
        ],
        out_specs=[pl.BlockSpec((SB_T, LANES), lambda h, i: (i, h))] * 2
        + [pl.BlockSpec(memory_space=pltpu.SMEM)],
        compiler_params=_cparams(("arbitrary", "arbitrary")),
    )(projmain, projmain, projmain)


def sb_bwd(projmain, do, t_exp, nblk):
    S = projmain.shape[0]
    nq = S // SB_T

    def body(n_ref, q_ref, k_ref, v_ref, do_ref, t_ref, dq_ref, dk_ref, dv_ref):
        hp, qb = pl.program_id(0), pl.program_id(1)

        @pl.when(qb == 0)
        def _():
            dk_ref[...] = jnp.zeros_like(dk_ref)
            dv_ref[...] = jnp.zeros_like(dv_ref)

        qst = _sb_stack(q_ref[...]).astype(MXU)
        dost = _sb_stack(do_ref[...]).astype(MXU)
        tv = t_ref[...]
        lane = _iota((SB_T, LANES), 1)
        tot = jnp.concatenate(
            [jnp.sum(jnp.where(lane == HEAD_DIM * hh, tv, 0.0), axis=1, keepdims=True) for hh in range(2)], axis=0)
        rowi, coli = _sb_rows(qb)
        u_upto = _tri(SB_T, lambda r, c: r <= c)
        u_before = _tri(SB_T, lambda r, c: r < c)
        kept = jnp.clip(n_ref[hp, qb], 0, qb + 1)

        def kstep(kb, carry):
            psp, pg, dq = carry
            ks = pl.multiple_of(kb * SB_T, SB_T)
            kblk, z, mask, sp, spm = _sb_scores(qst, k_ref, ks, rowi, coli)
            vblk = v_ref[pl.ds(ks, SB_T), :].astype(MXU)
            after = tot - (psp + _dot3_l(spm, u_upto))
            a = jnp.where(mask, jnp.exp(z - sp - after), 0.0)
            gm = _dot_nt(dost, vblk) * a
            before = pg + _dot3_l(gm, u_before)
            sg = jnp.exp(z - sp)
            dz = jnp.where(mask, gm * (1.0 - sg) - sg * before, 0.0) * SB_SCALE
            dzb = dz.astype(MXU)
            dq = dq + _dot(dzb, kblk)
            dk_ref[pl.ds(ks, SB_T), :] += _dot_tn(dzb, qst)
            dv_ref[pl.ds(ks, SB_T), :] += _dot_tn(a.astype(MXU), dost)
            return (psp + jnp.sum(spm, axis=1, keepdims=True),
                    pg + jnp.sum(gm, axis=1, keepdims=True), dq)

        zero1 = jnp.zeros((2 * SB_T, 1), F32)
        dq = lax.fori_loop(qb + 1 - kept, qb + 1, kstep, (zero1, zero1, jnp.zeros((2 * SB_T, LANES), F32)))[2]
        dq_ref[...] = _sb_unstack(dq).astype(dq_ref.dtype)

    return pl.pallas_call(
        body,
        name="sb_bwd",
        out_shape=[
            jax.ShapeDtypeStruct((S, SB_WIDTH), MXU),
            jax.ShapeDtypeStruct((S, SB_WIDTH), F32),
            jax.ShapeDtypeStruct((S, SB_WIDTH), F32),
        ],
        grid=(SB_WIDTH // LANES, nq),
        in_specs=[
            pl.BlockSpec(memory_space=pltpu.SMEM),
            pl.BlockSpec((SB_T, LANES), lambda h, i: (i, OFF_Q // LANES + h)),
            pl.BlockSpec((S, LANES), lambda h, i: (0, OFF_K // LANES + h)),
            pl.BlockSpec((S, LANES), lambda h, i: (0, OFF_V // LANES + h)),
            pl.BlockSpec((SB_T, LANES), lambda h, i: (i, h)),
            pl.BlockSpec((SB_T, LANES), lambda h, i: (i, h)),
        ],
        out_specs=[
            pl.BlockSpec((SB_T, LANES), lambda h, i: (i, h)),
            pl.BlockSpec((S, LANES), lambda h, i: (0, h)),
            pl.BlockSpec((S, LANES), lambda h, i: (0, h)),
        ],
        compiler_params=_cparams(("arbitrary", "arbitrary")),
    )(nblk, projmain, projmain, projmain, do, t_exp)


def local_step(x, pb, target, W, P):
    D = D_MODEL
    full = lambda a, w=D: (a, 0, w)

    (n1b,) = rowwise("norm_pre", lambda xv, g: (_rms(xv, g),), [full(x)], [full(P["norm_mix_pre"])],
                     [(D, D, MXU)])
    projmain = mm("in_proj", n1b, W["main"], "nn")
    gate_pre = mm("gate_proj", n1b, W["gate"], "nn", b_slabs=True)
    dtraw = mm("dt_proj", n1b, W["dt"], "nn")
    xbc_act = conv_fwd(projmain, P["conv_w"], P["conv_b"])
    dte, cse = ssd_prep(dtraw, P["dt_bias_pad"], P["a_exp"])
    y_scan, states = ssd_fwd(xbc_act, dte, cse)

    def f_gate(ysc, xs, z, dsk, nw):
        return _rms((ysc + xs * dsk) * (z * _sig(z)), nw)

    (y_ssd_b,) = rowwise("ssd_gate", lambda *a: (f_gate(*a),),
                         [(y_scan, 0, GW), (xbc_act, 0, GW), (projmain, OFF_Z, GW)],
                         [(P["dsk_exp"], 0, GW), (P["ssd_norm"], 0, GW)], [(D_INNER, GW, MXU)], ncb=SSD_GROUPS)
    y_sb_b, t_exp, sb_kept = sb_fwd(projmain)
    u1 = mm("ssd_branch", y_ssd_b, W["ssd"], "nn")
    u2 = mm("sb_branch", y_sb_b, W["sb"], "nn")

    def f_merge(a1, a2, g1, g2, b1, b2):
        return (_sig(g1 + b1) * a1 + _sig(g2 + b2) * a2,)

    gate_rows = [(gate_pre, 0, D), (gate_pre, D, D)]
    gate_bias = [(P["b_gate"], 0, D), (P["b_gate"], D, D)]
    (merged_b,) = rowwise("merge", f_merge, [full(u1), full(u2)] + gate_rows, gate_bias, [(D, D, MXU)])
    mo = mm("out_proj", merged_b, W["out"], "nn")

    def f_mid(xv, m, gpost, gffn):
        h1 = xv + _rms(m, gpost)
        return h1, _rms(h1, gffn)

    h1, n2b = rowwise("mix_post", f_mid, [full(x), full(mo)],
                      [full(P["norm_mix_post"]), full(P["norm_ffn_pre"])], [(D, D, F32), (D, D, MXU)])
    a_ff, rb = mm("ff1", n2b, W["ff1"], "nn", b_slabs=True,
                  extra=[(MXU, lambda acc: jnp.square(jnp.maximum(acc, 0.0)))])
    ff = mm("ff2", rb, W["ff2"], "nn")

    def f_ffn_post(h, f, g):
        h2 = h + _rms(f, g)
        return h2, h2

    h2, h2b = rowwise("ffn_post", f_ffn_post, [full(h1), full(ff)], [full(P["norm_ffn_post"])],
                      [(D, D, F32), (D, D, MXU)])
    pgp = mm("ple_gate", h2b, W["pg"], "nn")
    pe = mm("ple_proj", pb, W["ple"], "nn", b_slabs=True)

    def f_ple(h2v, gp, pev, tgt, g):
        f = lambda h, a, b, gg: h + _rms(_sig(a) * b, gg)
        h3, vjp = jax.vjp(f, h2v, gp, pev, g)
        err = h3 - tgt
        dh2, dgp, dpe, dg = vjp(err * (1.0 / D))
        lossc = (0.5 / D) * jnp.sum(err * err, axis=0, keepdims=True)
        return dh2, dgp, dpe, dg, lossc

    dh2a, dpgp_b, dpe_b, g_ple, lossc = rowwise(
        "ple_loss", f_ple, [full(h2), full(pgp), full(pe), full(target)], [full(P["norm_ple_post"])],
        [(D, D, F32), (D, D, MXU), (D, D, MXU)], [(D, D), (D, D)])
    dh2 = mm("d_ple_gate_x", dpgp_b, W["pg"], "nt", add=dh2a)
    gW = {}
    gW["w_ple_gate"] = mm("d_ple_gate_w", h2b, dpgp_b, "tn")
    gW["w_ple"] = mm("d_ple_w", pb, dpe_b, "tn", out_slabs=True)

    def b_ffn_post(d, f, g):
        _, vjp = jax.vjp(_rms, f, g)
        return vjp(d)

    dff_b, g_ffn_post = rowwise("d_ffn_post", b_ffn_post, [full(dh2), full(ff)], [full(P["norm_ffn_post"])],
                                [(D, D, MXU)], [(D, D)])
    da_b = mm("d_ff2_x", dff_b, W["ff2"], "nt", out_dtype=MXU,
              epi=lambda acc, act: acc * (2.0 * jnp.maximum(act, 0.0)), epi_args=[a_ff])
    gW["w_ff2"] = mm("d_ff2_w", rb, dff_b, "tn")
    dn2 = mm("d_ff1_x", da_b, W["ff1"], "nt", b_slabs=True)
    gW["w_ff1"] = mm("d_ff1_w", n2b, da_b, "tn", out_slabs=True)

    def b_mid(d2, dn, h, m, gpost, gffn):
        _, vjp = jax.vjp(_rms, h, gffn)
        dh, dgffn = vjp(dn)
        dh1 = d2 + dh
        _, vjp2 = jax.vjp(_rms, m, gpost)
        dm, dgpost = vjp2(dh1)
        return dh1, dm, dgpost, dgffn

    dh1, dmo_b, g_mix_post, g_ffn_pre = rowwise(
        "d_mix_post", b_mid, [full(dh2), full(dn2), full(h1), full(mo)],
        [full(P["norm_mix_post"]), full(P["norm_ffn_pre"])], [(D, D, F32), (D, D, MXU)], [(D, D), (D, D)])
    dmerged = mm("d_out_x", dmo_b, W["out"], "nt")
    gW["w_out"] = mm("d_out_w", merged_b, dmo_b, "tn")

    def b_merge(d, a1, a2, g1, g2, b1, b2):
        s1, s2 = _sig(g1 + b1), _sig(g2 + b2)
        dg1 = d * a1 * s1 * (1.0 - s1)
        dg2 = d * a2 * s2 * (1.0 - s2)
        dg = jnp.concatenate([dg1, dg2], axis=1)
        return d * s1, d * s2, dg, jnp.sum(dg, axis=0, keepdims=True)

    du1_b, du2_b, dgp_b, g_b_gate = rowwise(
        "d_merge", b_merge, [full(dmerged), full(u1), full(u2)] + gate_rows, gate_bias,
        [(D, D, MXU), (D, D, MXU), (2 * D, 2 * D, MXU)], [(2 * D, 2 * D)])
    dy_ssd = mm("d_ssd_branch_x", du1_b, W["ssd"], "nt")
    dy_sb = mm("d_sb_branch_x", du2_b, W["sb"], "nt")
    gW["w_ssd_branch"] = mm("d_ssd_branch_w", y_ssd_b, du1_b, "tn")
    gW["w_sb_branch"] = mm("d_sb_branch_w", y_sb_b, du2_b, "tn")

    def b_gate(d, ysc, xs, z, dsk, nw):
        _, vjp = jax.vjp(f_gate, ysc, xs, z, dsk, nw)
        return vjp(d)

    dy_scan, dxs_skip, dz_b, g_dsk_exp, g_ssd_norm = rowwise(
        "d_ssd_gate", b_gate, [(dy_ssd, 0, GW), (y_scan, 0, GW), (xbc_act, 0, GW), (projmain, OFF_Z, GW)],
        [(P["dsk_exp"], 0, GW), (P["ssd_norm"], 0, GW)],
        [(D_INNER, GW, F32), (D_INNER, GW, F32), (D_INNER, GW, MXU)], [(D_INNER, GW), (D_INNER, GW)],
        ncb=SSD_GROUPS)
    dxs, d_b, d_c, ddt_part, g_a_exp = ssd_bwd(dy_scan, xbc_act, dte, cse, states, P["a_exp"])

    def b_dt(dpart, dtr, bias):
        ddt = _dot3_l(dpart, _head_reduce())
        d = ddt * _sig(dtr + bias)
        return d, jnp.sum(d, axis=0, keepdims=True)

    ddt_b, g_dt_bias_pad = rowwise("d_dt", b_dt, [(ddt_part, 0, D_INNER), (dtraw, 0, LANES)],
                                   [(P["dt_bias_pad"], 0, LANES)], [(LANES, LANES, MXU)], [(LANES, LANES)])
    dxbc_act = jnp.concatenate([dxs + dxs_skip, d_b, d_c], axis=1)
    dxbc_b, g_conv_w, g_conv_b = conv_bwd(dxbc_act, projmain, P["conv_w"], P["conv_b"])
    dq_b, dk, dv = sb_bwd(projmain, dy_sb, t_exp, sb_kept)
    dmain_b = jnp.concatenate([dz_b, dxbc_b, dq_b, dk.astype(MXU), dv.astype(MXU)], axis=1)
    dn1_dt = mm("d_dt_x", ddt_b, W["dt"], "nt")
    dn1_gate = mm("d_gate_x", dgp_b, W["gate"], "nt", add=dn1_dt, b_slabs=True)
    dn1 = mm("d_in_x", dmain_b, W["main"], "nt", add=dn1_gate)
    gW["main"] = mm("d_in_w", n1b, dmain_b, "tn")
    gW["dt"] = mm("d_dt_w", n1b, ddt_b, "tn")
    gW["w_gate"] = mm("d_gate_w", n1b, dgp_b, "tn", out_slabs=True)

    def b_pre(d1, dn, xv, g):
        _, vjp = jax.vjp(_rms, xv, g)
        dx, dg = vjp(dn)
        return d1 + dx, dg

    grad_x, g_mix_pre = rowwise("d_norm_pre", b_pre, [full(dh1), full(dn1), full(x)], [full(P["norm_mix_pre"])],
                                [(D, D, F32)], [(D, D)])

    gW["conv_w"] = g_conv_w
    gS = {
        "norm_mix_pre": g_mix_pre, "conv_b": g_conv_b, "dt_bias_pad": g_dt_bias_pad, "a_exp": g_a_exp,
        "dsk_exp": g_dsk_exp, "ssd_norm": g_ssd_norm, "b_gate": g_b_gate,
        "norm_mix_post": g_mix_post, "norm_ffn_pre": g_ffn_pre, "norm_ffn_post": g_ffn_post,
        "norm_ple_post": g_ple,
    }
    return lossc, grad_x, gW, gS


IN_SPLITS = (2048, 6144, 6176, 7200, 8224)
IN_SHARD = 1156
IN_PAD = 1280
COL_SHARDED = {"w_in": (1024, IN_SHARD), "conv_w": (4, 512), "w_gate": (1024, 256), "w_ff1": (1024, 512),
               "w_ple": (256, 128)}
ROW_SHARDED = {"w_ssd_branch": (256, 1024), "w_sb_branch": (128, 1024), "w_out": (128, 1024),
               "w_ff2": (512, 1024), "w_ple_gate": (128, 1024)}
SHARDED = tuple(COL_SHARDED) + tuple(ROW_SHARDED)
SMALL = (
    ("norm_mix_pre", 1024), ("conv_b", 4096), ("dt_bias", 32), ("a_log", 32), ("d_skip", 32), ("ssd_norm", 2048),
    ("b_gate", 2048), ("norm_mix_post", 1024), ("norm_ffn_pre", 1024), ("norm_ffn_post", 1024),
    ("norm_ple_post", 1024),
)
ROW = 1024
SMALL_ROWS = 16


def _rows_of(n):
    return -(-n // ROW)


def _pad_to(v, n, axis=-1):
    pad = [(0, 0)] * v.ndim
    pad[axis] = (0, n - v.shape[axis])
    return jnp.pad(v, pad)


def build_matrices(g):
    w_in = jnp.concatenate([g["w_in"][k, :, :IN_SHARD] for k in range(N_DEV)], axis=1)
    stack = lambda a: a.reshape(a.shape[0] * a.shape[1], a.shape[2])
    return {
        "main": jnp.concatenate([w_in[:, :IN_SPLITS[1]], w_in[:, IN_SPLITS[2]:]], axis=1),
        "dt": _pad_to(w_in[:, IN_SPLITS[1]:IN_SPLITS[2]], LANES),
        "gate": g["w_gate"], "ff1": g["w_ff1"], "ple": g["w_ple"],
        "ssd": stack(g["w_ssd_branch"]), "sb": stack(g["w_sb_branch"]), "out": stack(g["w_out"]),
        "ff2": stack(g["w_ff2"]), "pg": stack(g["w_ple_gate"]),
    }


def build_small(small, conv_w_slabs):
    P = {k: small[k] for k in ("norm_mix_pre", "conv_b", "ssd_norm", "b_gate", "norm_mix_post", "norm_ffn_pre",
                               "norm_ffn_post", "norm_ple_post")}
    P["conv_w"] = jnp.concatenate([conv_w_slabs[k] for k in range(N_DEV)], axis=1)
    P["dt_bias_pad"] = _pad_to(small["dt_bias"], LANES)
    P["a_exp"] = jnp.repeat(-jnp.exp(small["a_log"]), HEAD_DIM, axis=1)
    P["dsk_exp"] = jnp.repeat(small["d_skip"], HEAD_DIM, axis=1)
    return P


def small_grads(gS, small):
    heads = lambda a: a.reshape(SSD_HEADS, HEAD_DIM).sum(axis=1)[None, :]
    out = {k: gS[k] for k in ("norm_mix_pre", "conv_b", "ssd_norm", "b_gate", "norm_mix_post", "norm_ffn_pre",
                              "norm_ffn_post", "norm_ple_post")}
    out["dt_bias"] = gS["dt_bias_pad"][:, :SSD_HEADS]
    out["a_log"] = heads(gS["a_exp"]) * (-jnp.exp(small["a_log"]))
    out["d_skip"] = heads(gS["dsk_exp"])
    return out


def grad_slabs(gW):
    g_in = jnp.concatenate([gW["main"][:, :IN_SPLITS[1]], gW["dt"][:, :SSD_HEADS], gW["main"][:, IN_SPLITS[1]:]],
                           axis=1)
    out = {"w_in": jnp.stack([_pad_to(g_in[:, IN_SHARD * k:IN_SHARD * (k + 1)], IN_PAD) for k in range(N_DEV)])}
    width = COL_SHARDED["conv_w"][1]
    out["conv_w"] = jnp.stack([gW["conv_w"][:, width * k:width * (k + 1)] for k in range(N_DEV)])
    for name in ("w_gate", "w_ff1", "w_ple"):
        out[name] = gW[name]
    for name, shape in ROW_SHARDED.items():
        out[name] = gW[name].reshape((N_DEV,) + shape)
    return out


def pack_small(get):
    cols = [_pad_to(get(name).reshape(n), _rows_of(n) * ROW) for name, n in SMALL]
    return jnp.concatenate(cols).reshape(SMALL_ROWS, ROW)


def unpack_small(flat):
    out, r0 = {}, 0
    for name, n in SMALL:
        rows = _rows_of(n)
        out[name] = flat[r0:r0 + rows].reshape(rows * ROW)[:n].reshape(1, n)
        r0 += rows
    return out


N_CHIPS = 4
JOB_SEMS = {"all": 7, "scatter": 7, "gather2": 7, "pair": 4, "chips": 3}


def exchange(name, jobs):
    n = len(jobs)
    kinds = [k for k, _ in jobs]
    srcs = [s for _, s in jobs]
    shapes = {"all": lambda s: (N_DEV,) + s.shape, "gather2": lambda s: (N_DEV,) + s.shape,
              "scatter": lambda s: s.shape, "pair": lambda s: (N_CHIPS,) + s.shape[1:], "chips": lambda s: s.shape}
    out_shape = [jax.ShapeDtypeStruct(shapes[k](s), s.dtype) for k, s in jobs]
    offs = [sum(JOB_SEMS[k] for k in kinds[:i]) for i in range(n + 1)]

    def body(*refs):
        src, out = refs[:n], refs[n:2 * n]
        send_sems, recv_sems, local_sems = refs[2 * n:]
        x, y, c = lax.axis_index("x"), lax.axis_index("y"), lax.axis_index("c")
        dev = lambda d: 4 * d[0] + 2 * d[1] + d[2]
        chip_no = lambda ch: 2 * ch[0] + ch[1]
        me, sib, my_chip = (x, y, c), (x, y, 1 - c), (x, y)
        others = [(1 - x, y), (x, 1 - y), (1 - x, 1 - y)]
        all_chips = [(0, 0), (0, 1), (1, 0), (1, 1)]
        peers = [(1 - x if k & 4 else x, 1 - y if k & 2 else y, 1 - c if k & 1 else c) for k in range(1, N_DEV)]
        starts, recvs, local = [], [], []
        chained = [[] for _ in others]

        for i, kind in enumerate(kinds):
            s_ref, o_ref = src[i], out[i]

            def rc(k, src_ref, dst_ref, to, i=i):
                s = offs[i] + k
                return pltpu.make_async_remote_copy(src_ref=src_ref, dst_ref=dst_ref, send_sem=send_sems.at[s],
                                                    recv_sem=recv_sems.at[s], device_id=to,
                                                    device_id_type=pl.DeviceIdType.MESH)

            if kind == "all":
                local.append(pltpu.make_async_copy(s_ref, o_ref.at[dev(me)], local_sems.at[i]))
                for k, peer in enumerate(peers):
                    starts.append(rc(k, s_ref, o_ref.at[dev(me)], peer))
                    recvs.append(rc(k, s_ref, o_ref.at[dev(peer)], peer))
            elif kind == "scatter":
                local.append(pltpu.make_async_copy(s_ref.at[dev(me)], o_ref.at[dev(me)], local_sems.at[i]))
                for k, peer in enumerate(peers):
                    starts.append(rc(k, s_ref.at[dev(peer)], o_ref.at[dev(me)], peer))
                    recvs.append(rc(k, s_ref.at[dev(me)], o_ref.at[dev(peer)], peer))
            elif kind == "gather2":
                local.append(pltpu.make_async_copy(s_ref, o_ref.at[dev(me)], local_sems.at[i]))
                starts.append(rc(0, s_ref, o_ref.at[dev(me)], sib))
                recvs.append(rc(0, s_ref, o_ref.at[dev(sib)], sib))
                for j, ch in enumerate(others):
                    same, other = (*ch, c), (*ch, 1 - c)
                    starts.append(rc(1 + j, s_ref, o_ref.at[dev(me)], same))
                    chained[j].append((rc(1 + j, s_ref, o_ref.at[dev(same)], same),
                                       rc(4 + j, o_ref.at[dev(same)], o_ref.at[dev(same)], sib)))
                    recvs.append(rc(4 + j, s_ref, o_ref.at[dev(other)], sib))
            elif kind == "pair":
                for j, ch in enumerate(all_chips):
                    starts.append(rc(j, s_ref.at[dev((*ch, 1 - c))], o_ref.at[j], sib))
                    recvs.append(rc(j, s_ref.at[dev((*ch, c))], o_ref.at[j], sib))
            else:
                mine = chip_no(my_chip)
                local.append(pltpu.make_async_copy(s_ref.at[mine], o_ref.at[mine], local_sems.at[i]))
                for j, ch in enumerate(others):
                    starts.append(rc(j, s_ref.at[chip_no(ch)], o_ref.at[mine], (*ch, c)))
                    recvs.append(rc(j, s_ref.at[mine], o_ref.at[chip_no(ch)], (*ch, c)))

        for cp in local + starts:
            cp.start()
        passed = []
        for group in chained:
            for arrival, forward in group:
                arrival.wait_recv()
                forward.start()
                passed.append(forward)
        for cp in recvs:
            cp.wait_recv()
        for cp in starts + passed:
            cp.wait_send()
        for cp in local:
            cp.wait()

    any_spec = pl.BlockSpec(memory_space=pl.ANY)
    return pl.pallas_call(
        body,
        name=name,
        out_shape=out_shape,
        in_specs=[any_spec] * n,
        out_specs=[any_spec] * n,
        scratch_shapes=[
            pltpu.SemaphoreType.DMA((offs[n],)),
            pltpu.SemaphoreType.DMA((offs[n],)),
            pltpu.SemaphoreType.DMA((n,)),
        ],
    )(*srcs)


def pair_sum(name, g, sib, core):
    _, r, c = g.shape
    tr = _pick(r, (256, 128, 64, 32, 16, 8))

    def body(core_ref, g_ref, s_ref, o_ref):
        o_ref[...] = (g_ref[...] + s_ref[...]).astype(o_ref.dtype)

    return pl.pallas_call(
        body,
        name=name,
        out_shape=jax.ShapeDtypeStruct((N_CHIPS, r, c), MXU),
        grid_spec=pltpu.PrefetchScalarGridSpec(
            num_scalar_prefetch=1,
            grid=(N_CHIPS, r // tr),
            in_specs=[
                pl.BlockSpec((None, None, tr, c), lambda j, i, core_ref: (j, core_ref[0], i, 0)),
                pl.BlockSpec((None, tr, c), lambda j, i, core_ref: (j, i, 0)),
            ],
            out_specs=pl.BlockSpec((None, tr, c), lambda j, i, core_ref: (j, i, 0)),
        ),
        compiler_params=_cparams(("parallel", "parallel")),
    )(core, g.reshape(N_CHIPS, 2, r, c), sib)


def adamw(name, parts, w, m, v):
    rows, cols = w.shape
    tr = _pick(rows, (128, 64, 32, 16, 8))

    nparts = parts.shape[0]

    def body(p_ref, w_ref, m_ref, v_ref, g_ref, d_ref, m2_ref, v2_ref):
        g = p_ref[0].astype(F32)
        for k in range(1, nparts):
            g = g + p_ref[k].astype(F32)
        m2 = ADAM_B1 * m_ref[...] + (1.0 - ADAM_B1) * g
        v2 = ADAM_B2 * v_ref[...] + (1.0 - ADAM_B2) * jnp.square(g)
        m_hat = m2 / (1.0 - ADAM_B1 ** ADAM_STEP)
        v_hat = v2 / (1.0 - ADAM_B2 ** ADAM_STEP)
        g_ref[...] = g
        d_ref[...] = -ADAM_LR * (m_hat / (jnp.sqrt(v_hat) + ADAM_EPS) + ADAM_WD * w_ref[...])
        m2_ref[...] = m2
        v2_ref[...] = v2

    spec = pl.BlockSpec((tr, cols), lambda i: (i, 0))
    return pl.pallas_call(
        body,
        name=name,
        out_shape=[jax.ShapeDtypeStruct((rows, cols), F32)] * 4,
        grid=(rows // tr,),
        in_specs=[pl.BlockSpec((nparts, tr, cols), lambda i: (0, i, 0)), spec, spec, spec],
        out_specs=[spec] * 4,
        compiler_params=_cparams(("parallel",)),
    )(parts, w, m, v)


WEIGHT_ORDER = (
    "norm_mix_pre", "w_in", "conv_w", "conv_b", "dt_bias", "a_log", "d_skip", "ssd_norm", "w_ssd_branch",
    "w_sb_branch", "w_gate", "b_gate", "w_out", "norm_mix_post", "norm_ffn_pre", "w_ff1", "w_ff2", "norm_ffn_post",
    "w_ple", "w_ple_gate", "norm_ple_post",
)


def kernel(x, p, norm_mix_pre, w_in, conv_w, conv_b, dt_bias, a_log, d_skip, ssd_norm, w_ssd_branch, w_sb_branch, w_gate, b_gate, w_out, norm_mix_post, norm_ffn_pre, w_ff1, w_ff2, norm_ffn_post, w_ple, w_ple_gate, norm_ple_post, loss_target, m_norm_mix_pre, m_w_in, m_conv_w, m_conv_b, m_dt_bias, m_a_log, m_d_skip, m_ssd_norm, m_w_ssd_branch, m_w_sb_branch, m_w_gate, m_b_gate, m_w_out, m_norm_mix_post, m_norm_ffn_pre, m_w_ff1, m_w_ff2, m_norm_ffn_post, m_w_ple, m_w_ple_gate, m_norm_ple_post, v_norm_mix_pre, v_w_in, v_conv_w, v_conv_b, v_dt_bias, v_a_log, v_d_skip, v_ssd_norm, v_w_ssd_branch, v_w_sb_branch, v_w_gate, v_b_gate, v_w_out, v_norm_mix_post, v_norm_ffn_pre, v_w_ff1, v_w_ff2, v_norm_ffn_post, v_w_ple, v_w_ple_gate, v_norm_ple_post):
    a = dict(locals())
    seq = x.shape[1]
    x2 = x.reshape(seq, D_MODEL)
    target = loss_target.reshape(seq, D_MODEL)
    pb = p.reshape(seq, PLE_DIM).astype(MXU)

    def shard(prefix, name, dtype):
        v = a[prefix + name][0].astype(dtype)
        return _pad_to(v, IN_PAD) if name == "w_in" else v

    names = [n for n in SHARDED if n != "conv_w"]
    got = exchange("gather_weights", [("gather2", shard("", n, MXU)) for n in names] + [("all", a["conv_w"][0])])
    small = {n: a[n] for n, _ in SMALL}

    lossc, grad_x, g_full, g_acc = local_step(x2, pb, target, build_matrices(dict(zip(names, got))),
                                              build_small(small, got[-1]))
    loss = lax.psum(jnp.sum(lossc), ("x", "y", "c"))
    g_small = small_grads(g_acc, small)

    slabs = grad_slabs(g_full)
    core = lax.axis_index("c").astype(jnp.int32).reshape(1)
    from_sibling = exchange("pair_grads", [("pair", slabs[n]) for n in names])
    chip_sums = [pair_sum("pair_sum_" + n, slabs[n], s, core) for n, s in zip(names, from_sibling)]
    parts = exchange("chip_grads", [("chips", s) for s in chip_sums]
                     + [("scatter", slabs["conv_w"]), ("all", pack_small(lambda n: g_small[n]))])

    leaves = {}
    for n, part in zip(names + ["conv_w"], parts):
        res = adamw("adamw_" + n, part, shard("", n, F32), shard("m_", n, F32), shard("v_", n, F32))
        leaves[n] = [r[None, :, :IN_SHARD] if n == "w_in" else r[None] for r in res]
    res = adamw("adamw_small", parts[-1], pack_small(lambda n: a[n]), pack_small(lambda n: a["m_" + n]),
                pack_small(lambda n: a["v_" + n]))
    for j, r in enumerate(res):
        for n, leaf in unpack_small(r).items():
            leaves.setdefault(n, [None] * 4)[j] = leaf
    outs = [loss, grad_x.reshape(x.shape)]
    for j in range(4):
        outs += [leaves[n][j] for n in WEIGHT_ORDER]
    return tuple(outs)
```

```python
import functools

import jax
import jax.numpy as jnp
from jax import lax
from jax.experimental import pallas as pl
from jax.experimental.pallas import tpu as pltpu

F32 = jnp.float32
MXU = jnp.bfloat16
VMEM_LIMIT = 56 * 1024 * 1024

D_MODEL = 1024
D_INNER = 2048
SSD_HEADS = 32
HEAD_DIM = 64
SSD_GROUPS = 8
D_STATE = 128
CONV_K = 4
CONV_DIM = 4096
CHUNK = 128
SB_WIDTH = 1024
D_FF = 4096
PLE_DIM = 256
RMS_EPS = 1e-6
SB_SCALE = HEAD_DIM ** -0.5
N_DEV = 8
LANES = 128

OFF_Z, OFF_XBC, OFF_Q, OFF_K, OFF_V = 0, 2048, 6144, 7168, 8192

ADAM_LR = 0.001
ADAM_B1 = 0.9
ADAM_B2 = 0.999
ADAM_EPS = 1e-08
ADAM_WD = 0.01
ADAM_STEP = 10


def _sig(x):
    return 1.0 / (1.0 + jnp.exp(-x))


def _softplus(x):
    return jnp.maximum(x, 0.0) + jnp.log(1.0 + jnp.exp(-jnp.abs(x)))


def _rms(x, w):
    return x * lax.rsqrt(jnp.mean(x * x, axis=-1, keepdims=True) + RMS_EPS) * w


def _dot(a, b):
    return jnp.dot(a, b, preferred_element_type=F32)


def _dot_nt(a, b):
    return lax.dot_general(a, b, (((1,), (1,)), ((), ())), preferred_element_type=F32)


def _dot_tn(a, b):
    return lax.dot_general(a, b, (((0,), (0,)), ((), ())), preferred_element_type=F32)


def _split3(x):
    x1 = x.astype(MXU)
    r = x - x1.astype(F32)
    x2 = r.astype(MXU)
    r = r - x2.astype(F32)
    return x1, x2, r.astype(MXU)


def _dot3_l(a, u):
    m = a.shape[0]
    d = _dot(jnp.concatenate(_split3(a), axis=0), u)
    return (d[2 * m:] + d[m:2 * m]) + d[:m]


def _dot3_r(u, a):
    n = a.shape[1]
    d = _dot(u, jnp.concatenate(_split3(a), axis=1))
    return (d[:, 2 * n:] + d[:, n:2 * n]) + d[:, :n]


def _iota(shape, dim):
    return lax.broadcasted_iota(jnp.int32, shape, dim)


def _tri(n, cmp):
    r, c = _iota((n, n), 0), _iota((n, n), 1)
    return cmp(r, c).astype(F32).astype(MXU)


def _cparams(sem):
    return pltpu.CompilerParams(dimension_semantics=sem, vmem_limit_bytes=VMEM_LIMIT)


def _pick(n, cands):
    for c in cands:
        if n % c == 0:
            return c
    return n


def mm(name, a, b, mode, add=None, out_dtype=F32, b_slabs=False, out_slabs=False, epi=None, epi_args=(), extra=()):
    slab = None
    if b_slabs:
        slab = b.shape[2]
        bshape = (b.shape[1], N_DEV * slab)
    else:
        bshape = b.shape
    if mode == "nn":
        (M, K), (K2, N) = a.shape, bshape
    elif mode == "nt":
        (M, K), (N, K2) = a.shape, bshape
    else:
        (K, M), (K2, N) = a.shape, bshape
    assert K == K2, (name, a.shape, b.shape)
    tm = _pick(M, (1024, 512, 256, 128))
    tn = _pick(N, (1024, 512, 256, 128))
    tk = _pick(K, (1024, 512, 256, 128))
    if b_slabs and mode == "nn":
        tn = slab
    if b_slabs and mode == "nt":
        tk = slab
    if out_slabs:
        assert mode == "tn" and N % N_DEV == 0
        tn = N // N_DEV
    nk = K // tk

    def body(*refs):
        refs = list(refs)
        a_ref, b_ref = refs[:2]
        add_ref = refs[2] if add is not None else None
        n_in = 2 + (add is not None)
        epi_refs = refs[n_in:n_in + len(epi_args)]
        o_ref = refs[n_in + len(epi_args)]
        extra_refs = refs[n_in + len(epi_args) + 1:-1]
        acc = refs[-1]
        k = pl.program_id(2)

        @pl.when(k == 0)
        def _():
            acc[...] = jnp.zeros_like(acc) if add is None else add_ref[...]

        av, bv = a_ref[...], b_ref[...]
        if mode == "nn":
            acc[...] += _dot(av, bv)
        elif mode == "nt":
            acc[...] += _dot_nt(av, bv)
        else:
            acc[...] += _dot_tn(av, bv)

        @pl.when(k == nk - 1)
        def _():
            res = acc[...]
            main = res if epi is None else epi(res, *[r[...] for r in epi_refs])
            o_ref[...] = main.astype(o_ref.dtype)
            for r, (_, fn) in zip(extra_refs, extra):
                r[...] = fn(res).astype(r.dtype)

    if mode == "nn":
        a_spec = pl.BlockSpec((tm, tk), lambda i, j, k: (i, k))
        b_spec = pl.BlockSpec((tk, tn), lambda i, j, k: (k, j))
    elif mode == "nt":
        a_spec = pl.BlockSpec((tm, tk), lambda i, j, k: (i, k))
        b_spec = pl.BlockSpec((tn, tk), lambda i, j, k: (j, k))
    else:
        a_spec = pl.BlockSpec((tk, tm), lambda i, j, k: (k, i))
        b_spec = pl.BlockSpec((tk, tn), lambda i, j, k: (k, j))
    if b_slabs and mode == "nn":
        b_spec = pl.BlockSpec((None, tk, tn), lambda i, j, k: (j, k, 0))
    if b_slabs and mode == "nt":
        b_spec = pl.BlockSpec((None, tn, tk), lambda i, j, k: (k, j, 0))
    o_spec = pl.BlockSpec((tm, tn), lambda i, j, k: (i, j))
    in_specs, args = [a_spec, b_spec], [a, b]
    if add is not None:
        in_specs.append(o_spec)
        args.append(add)
    for e in epi_args:
        in_specs.append(o_spec)
        args.append(e)
    out_sds = jax.ShapeDtypeStruct((M, N), out_dtype)
    if out_slabs:
        o_spec = pl.BlockSpec((None, tm, tn), lambda i, j, k: (j, i, 0))
        out_sds = jax.ShapeDtypeStruct((N_DEV, M, tn), out_dtype)
    res = pl.pallas_call(
        body,
        name=name,
        out_shape=[out_sds] + [jax.ShapeDtypeStruct((M, N), dt) for dt, _ in extra],
        grid=(M // tm, N // tn, nk),
        in_specs=in_specs,
        out_specs=[o_spec] * (1 + len(extra)),
        scratch_shapes=[pltpu.VMEM((tm, tn), F32)],
        compiler_params=_cparams(("parallel", "parallel", "arbitrary")),
    )(*args)
    return res if extra else res[0]


def rowwise(name, fn, rows, bcast, outs, accs=(), tr=256, ncb=1, side=None):
    S = rows[0][0].shape[0]
    tr = min(tr, S)
    nrb = S // tr
    in_specs, args = [], []
    for arr, off, w in rows:
        assert off % w == 0 and arr.shape[0] == S
        in_specs.append(pl.BlockSpec((tr, w), lambda j, i, ob=off // w: (i, ob + j)))
        args.append(arr)
    for arr, off, w in bcast:
        assert off % w == 0
        in_specs.append(pl.BlockSpec((arr.shape[0], w), lambda j, i, ob=off // w: (0, ob + j)))
        args.append(arr)
    out_shape, out_specs = [], []
    for tw, w, dt in outs:
        out_shape.append(jax.ShapeDtypeStruct((S, tw), dt))
        out_specs.append(pl.BlockSpec((tr, w), lambda j, i: (i, j)))
    for tw, w in accs:
        out_shape.append(jax.ShapeDtypeStruct((1, tw), F32))
        out_specs.append(pl.BlockSpec((1, w), lambda j, i: (0, j)))
    nin, nout = len(args), len(outs)

    def body(*refs):
        res = fn(*[r[...] for r in refs[:nin]])
        o_refs, a_refs = refs[nin:nin + nout], refs[nin + nout:]
        for r, v in zip(o_refs, res[:nout]):
            r[...] = v.astype(r.dtype)
        i = pl.program_id(1)
        for r, v in zip(a_refs, res[nout:]):
            @pl.when(i == 0)
            def _(r=r, v=v):
                r[...] = v

            @pl.when(i > 0)
            def _(r=r, v=v):
                r[...] += v

    body, s_in, s_args, s_shape, s_out, s_scr = host_exchange(body, nin, len(out_shape), (ncb, nrb), side)
    res = pl.pallas_call(
        body,
        name=name,
        out_shape=out_shape + s_shape,
        grid=(ncb, nrb),
        in_specs=in_specs + s_in,
        out_specs=out_specs + s_out,
        scratch_shapes=s_scr,
        compiler_params=_cparams(("parallel" if side is None else "arbitrary", "arbitrary")),
    )(*args, *s_args)
    return res


CONV_TC = 128


def _conv_pre(u, w, b, row):
    pre = b + w[3:4, :] * u
    shifted = []
    for s in (1, 2, 3):
        us = jnp.where(row >= s, pltpu.roll(u, s, 0), 0.0)
        shifted.append(us)
        pre = pre + w[3 - s:4 - s, :] * us
    return pre, shifted


def conv_fwd(projmain, conv_w, conv_b):
    S = projmain.shape[0]
    tc = CONV_TC

    def body(u_ref, w_ref, b_ref, o_ref):
        u = u_ref[...]
        row = _iota(u.shape, 0)
        pre, _ = _conv_pre(u, w_ref[...], b_ref[...], row)
        o_ref[...] = pre * _sig(pre)

    return pl.pallas_call(
        body,
        name="conv_fwd",
        out_shape=jax.ShapeDtypeStruct((S, CONV_DIM), F32),
        grid=(CONV_DIM // tc,),
        in_specs=[
            pl.BlockSpec((S, tc), lambda j: (0, OFF_XBC // tc + j)),
            pl.BlockSpec((CONV_K, tc), lambda j: (0, j)),
            pl.BlockSpec((1, tc), lambda j: (0, j)),
        ],
        out_specs=pl.BlockSpec((S, tc), lambda j: (0, j)),
        compiler_params=_cparams(("parallel",)),
    )(projmain, conv_w, conv_b)


def conv_bwd(dxs, dxs_skip, d_b, d_c, projmain, conv_w, conv_b):
    S = projmain.shape[0]
    tc = CONV_TC
    n_xs, n_b = D_INNER // tc, SSD_GROUPS * D_STATE // tc

    def body(dx_ref, dskip_ref, dbm_ref, dcm_ref, u_ref, w_ref, b_ref, du_ref, dw_ref, db_ref):
        j = pl.program_id(0)
        d = jnp.where(j < n_xs, dx_ref[...] + dskip_ref[...], jnp.where(j < n_xs + n_b, dbm_ref[...], dcm_ref[...]))
        u = u_ref[...]
        w = w_ref[...]
        row = _iota(u.shape, 0)
        pre, shifted = _conv_pre(u, w, b_ref[...], row)
        sg = _sig(pre)
        dpre = d * (sg * (1.0 + pre * (1.0 - sg)))
        du = w[3:4, :] * dpre
        dw_ref[3:4, :] = jnp.sum(dpre * u, axis=0, keepdims=True)
        for s in (1, 2, 3):
            ds = jnp.where(row < S - s, pltpu.roll(dpre, S - s, 0), 0.0)
            du = du + w[3 - s:4 - s, :] * ds
            dw_ref[3 - s:4 - s, :] = jnp.sum(dpre * shifted[s - 1], axis=0, keepdims=True)
        du_ref[...] = du.astype(du_ref.dtype)
        db_ref[...] = jnp.sum(dpre, axis=0, keepdims=True)

    return pl.pallas_call(
        body,
        name="conv_bwd",
        out_shape=[
            jax.ShapeDtypeStruct((S, CONV_DIM), MXU),
            jax.ShapeDtypeStruct((CONV_K, CONV_DIM), F32),
            jax.ShapeDtypeStruct((1, CONV_DIM), F32),
        ],
        grid=(CONV_DIM // tc,),
        in_specs=[
            pl.BlockSpec((S, tc), lambda j: (0, jnp.minimum(j, n_xs - 1))),
            pl.BlockSpec((S, tc), lambda j: (0, jnp.minimum(j, n_xs - 1))),
            pl.BlockSpec((S, tc), lambda j: (0, jnp.clip(j - n_xs, 0, n_b - 1))),
            pl.BlockSpec((S, tc), lambda j: (0, jnp.clip(j - n_xs - n_b, 0, n_b - 1))),
            pl.BlockSpec((S, tc), lambda j: (0, OFF_XBC // tc + j)),
            pl.BlockSpec((CONV_K, tc), lambda j: (0, j)),
            pl.BlockSpec((1, tc), lambda j: (0, j)),
        ],
        out_specs=[
            pl.BlockSpec((S, tc), lambda j: (0, j)),
            pl.BlockSpec((CONV_K, tc), lambda j: (0, j)),
            pl.BlockSpec((1, tc), lambda j: (0, j)),
        ],
        compiler_params=_cparams(("arbitrary",)),
    )(dxs, dxs_skip, d_b, d_c, projmain, conv_w, conv_b)


def _head_expand():
    r, j = _iota((LANES, D_INNER), 0), _iota((LANES, D_INNER), 1)
    return ((j >= r * HEAD_DIM) & (j < r * HEAD_DIM + HEAD_DIM)).astype(F32).astype(MXU)


def _head_reduce():
    j, r = _iota((D_INNER, LANES), 0), _iota((D_INNER, LANES), 1)
    return ((j >= r * HEAD_DIM) & (j < r * HEAD_DIM + HEAD_DIM)).astype(F32).astype(MXU)


def ssd_prep(dtraw, dt_bias_pad, a_exp):
    S = dtraw.shape[0]

    def body(dtr_ref, bias_ref, a_ref, dte_ref, cse_ref):
        dt = _softplus(dtr_ref[...] + bias_ref[...])
        dte = _dot3_l(dt, _head_expand())
        dte_ref[...] = dte
        incl = _tri(CHUNK, lambda r, c: r >= c)
        cse_ref[...] = _dot3_r(incl, dte * a_ref[...])

    return pl.pallas_call(
        body,
        name="ssd_prep",
        out_shape=[jax.ShapeDtypeStruct((S, D_INNER), F32)] * 2,
        grid=(S // CHUNK,),
        in_specs=[
            pl.BlockSpec((CHUNK, LANES), lambda c: (c, 0)),
            pl.BlockSpec((1, LANES), lambda c: (0, 0)),
            pl.BlockSpec((1, D_INNER), lambda c: (0, 0)),
        ],
        out_specs=[pl.BlockSpec((CHUNK, D_INNER), lambda c: (c, 0))] * 2,
        compiler_params=_cparams(("parallel",)),
    )(dtraw, dt_bias_pad, a_exp)


GW = 4 * HEAD_DIM


def ssd_fwd(xbc_act, dte, cse, side=None):
    S = xbc_act.shape[0]
    nc = S // CHUNK

    def body(xs_ref, b_ref, c_ref, dte_ref, cse_ref, y_ref, st_ref, s_scr):
        c = pl.program_id(1)

        @pl.when(c == 0)
        def _():
            s_scr[...] = jnp.zeros_like(s_scr)

        s_in = s_scr[...]
        st_ref[0] = s_in
        cs = cse_ref[...]
        xd = xs_ref[...] * dte_ref[...]
        btb = b_ref[...].T.astype(MXU)
        cs_last = cs[CHUNK - 1:CHUNK, :]
        gy = _dot(c_ref[...].astype(MXU), jnp.concatenate([btb, s_in.astype(MXU)], axis=1))
        g, y = gy[:, :CHUNK], gy[:, CHUNK:] * jnp.exp(cs)
        cs_t = cs.T
        row, col = _iota((CHUNK, CHUNK), 0), _iota((CHUNK, CHUNK), 1)
        ms = []
        for h in range(4):
            lo = HEAD_DIM * h
            lam = jnp.where(row >= col, jnp.exp(cs[:, lo:lo + 1] - cs_t[lo:lo + 1, :]), 0.0)
            ms.append((g * lam).astype(MXU))
        yd = _dot(jnp.concatenate(ms, axis=0), xd.astype(MXU))
        lane = _iota((CHUNK, GW), 1)
        for h in range(4):
            y = y + jnp.where((lane >= HEAD_DIM * h) & (lane < HEAD_DIM * (h + 1)), yd[CHUNK * h:CHUNK * (h + 1)], 0.0)
        y_ref[...] = y
        w = (xd * jnp.exp(cs_last - cs)).astype(MXU)
        s_scr[...] = jnp.exp(cs_last) * s_in + _dot(btb, w)

    grid = (SSD_GROUPS, nc)
    body, s_in, s_args, s_shape, s_out, s_scr = host_exchange(body, 5, 2, grid, side)
    return pl.pallas_call(
        body,
        name="ssd_fwd",
        out_shape=[
            jax.ShapeDtypeStruct((S, D_INNER), F32),
            jax.ShapeDtypeStruct((nc, D_STATE, D_INNER), F32),
        ] + s_shape,
        grid=grid,
        in_specs=[
            pl.BlockSpec((CHUNK, GW), lambda g, c: (c, g)),
            pl.BlockSpec((CHUNK, D_STATE), lambda g, c: (c, D_INNER // D_STATE + g)),
            pl.BlockSpec((CHUNK, D_STATE), lambda g, c: (c, D_INNER // D_STATE + SSD_GROUPS + g)),
            pl.BlockSpec((CHUNK, GW), lambda g, c: (c, g)),
            pl.BlockSpec((CHUNK, GW), lambda g, c: (c, g)),
        ] + s_in,
        out_specs=[
            pl.BlockSpec((CHUNK, GW), lambda g, c: (c, g)),
            pl.BlockSpec((1, D_STATE, GW), lambda g, c: (c, 0, g)),
        ] + s_out,
        scratch_shapes=[pltpu.VMEM((D_STATE, GW), F32)] + s_scr,
        compiler_params=_cparams(("parallel" if side is None else "arbitrary", "arbitrary")),
    )(xbc_act, xbc_act, xbc_act, dte, cse, *s_args)


def ssd_bwd(dy, xbc_act, dte, cse, states, a_exp, side=None):
    S = xbc_act.shape[0]
    nc = S // CHUNK

    def body(dy_ref, xs_ref, b_ref, c_ref, dte_ref, cse_ref, sin_ref, sout_ref, a_ref,
             dxs_ref, db_ref, dc_ref, ddt_ref, dal_ref, ds_scr):
        j = pl.program_id(1)

        @pl.when(j == 0)
        def _():
            ds_scr[...] = jnp.zeros_like(ds_scr)
            dal_ref[...] = jnp.zeros_like(dal_ref)

        ds_out = ds_scr[...]
        dyv, xs = dy_ref[...], xs_ref[...]
        dt, cs = dte_ref[...], cse_ref[...]
        s_in = sin_ref[0]
        bm, cm = b_ref[...], c_ref[...]
        bb, cb = bm.astype(MXU), cm.astype(MXU)
        btb, ctb = bm.T.astype(MXU), cm.T.astype(MXU)
        dsb, sib = ds_out.astype(MXU), s_in.astype(MXU)
        xd = xs * dt
        ecs = jnp.exp(cs)
        cs_last = cs[CHUNK - 1:CHUNK, :]
        eend = jnp.exp(cs_last - cs)
        gy = _dot(cb, jnp.concatenate([btb, sib], axis=1))
        g, yoff = gy[:, :CHUNK], gy[:, CHUNK:] * ecs
        gd = _dot(bb, jnp.concatenate([ctb, dsb], axis=1))
        g_t, dxd_off = gd[:, :CHUNK], gd[:, CHUNK:] * eend
        cs_t = cs.T
        row, col = _iota((CHUNK, CHUNK), 0), _iota((CHUNK, CHUNK), 1)
        lane = _iota((CHUNK, GW), 1)
        heads = [(lane >= HEAD_DIM * h) & (lane < HEAD_DIM * (h + 1)) for h in range(4)]
        dyb, xdb = dyv.astype(MXU), xd.astype(MXU)
        dm_all = _dot_nt(jnp.concatenate([jnp.where(hm, dyv, 0.0) for hm in heads], axis=0).astype(MXU), xdb)
        dmt_all = _dot_nt(jnp.concatenate([jnp.where(hm, xd, 0.0) for hm in heads], axis=0).astype(MXU), dyb)
        lams, m_ts = [], []
        for h in range(4):
            lo = HEAD_DIM * h
            cs_col, cs_row = cs[:, lo:lo + 1], cs_t[lo:lo + 1, :]
            lams.append(jnp.where(row >= col, jnp.exp(cs_col - cs_row), 0.0))
            m_ts.append(g_t * jnp.where(col >= row, jnp.exp(cs_row - cs_col), 0.0))
        acc_all = _dot(jnp.concatenate(m_ts, axis=0).astype(MXU), dyb)
        dxd = dxd_off
        dg = jnp.zeros((CHUNK, CHUNK), F32)
        dcs = dyv * yoff - xd * dxd_off
        for h in range(4):
            blk = slice(CHUNK * h, CHUNK * (h + 1))
            dm, dm_t = dm_all[blk], dmt_all[blk]
            dxd = dxd + jnp.where(heads[h], acc_all[blk], 0.0)
            dg = dg + dm * lams[h]
            wdiff = (jnp.sum(dm * (g * lams[h]), axis=1, keepdims=True)
                     - jnp.sum(dm_t * m_ts[h], axis=1, keepdims=True))
            dcs = dcs + jnp.where(lane == HEAD_DIM * h, wdiff, 0.0)
        dye = (dyv * ecs).astype(MXU)
        dc_ref[...] = _dot(dg.astype(MXU), bb) + _dot_nt(dye, sib)
        db_ref[...] = _dot(dg.T.astype(MXU), cb) + _dot_nt((xd * eend).astype(MXU), dsb)
        ds_scr[...] = jnp.exp(cs_last) * ds_out + _dot(ctb, dye)
        last = jnp.sum(ds_out * sout_ref[0], axis=0, keepdims=True)
        rows = _iota((CHUNK, GW), 0)
        dcs = dcs + jnp.where(rows == CHUNK - 1, last, 0.0)
        dda = _dot3_r(_tri(CHUNK, lambda r, c: c >= r), dcs)
        ddt_ref[...] = a_ref[...] * dda + dxd * xs
        dal_ref[...] += jnp.sum(dt * dda, axis=0, keepdims=True)
        dxs_ref[...] = dxd * dt

    rc = lambda g, j: (nc - 1 - j, g)
    grid = (SSD_GROUPS, nc)
    body, s_in, s_args, s_shape, s_out, s_scr = host_exchange(body, 9, 5, grid, side)
    return pl.pallas_call(
        body,
        name="ssd_bwd",
        out_shape=[
            jax.ShapeDtypeStruct((S, D_INNER), F32),
            jax.ShapeDtypeStruct((S, SSD_GROUPS * D_STATE), F32),
            jax.ShapeDtypeStruct((S, SSD_GROUPS * D_STATE), F32),
            jax.ShapeDtypeStruct((S, D_INNER), F32),
            jax.ShapeDtypeStruct((1, D_INNER), F32),
        ] + s_shape,
        grid=grid,
        in_specs=[
            pl.BlockSpec((CHUNK, GW), rc),
            pl.BlockSpec((CHUNK, GW), rc),
            pl.BlockSpec((CHUNK, D_STATE), lambda g, j: (nc - 1 - j, D_INNER // D_STATE + g)),
            pl.BlockSpec((CHUNK, D_STATE), lambda g, j: (nc - 1 - j, D_INNER // D_STATE + SSD_GROUPS + g)),
            pl.BlockSpec((CHUNK, GW), rc),
            pl.BlockSpec((CHUNK, GW), rc),
            pl.BlockSpec((1, D_STATE, GW), lambda g, j: (nc - 1 - j, 0, g)),
            pl.BlockSpec((1, D_STATE, GW), lambda g, j: (jnp.minimum(nc - j, nc - 1), 0, g)),
            pl.BlockSpec((1, GW), lambda g, j: (0, g)),
        ] + s_in,
        out_specs=[
            pl.BlockSpec((CHUNK, GW), rc),
            pl.BlockSpec((CHUNK, D_STATE), rc),
            pl.BlockSpec((CHUNK, D_STATE), rc),
            pl.BlockSpec((CHUNK, GW), rc),
            pl.BlockSpec((1, GW), lambda g, j: (0, g)),
        ] + s_out,
        scratch_shapes=[pltpu.VMEM((D_STATE, GW), F32)] + s_scr,
        compiler_params=_cparams(("parallel" if side is None else "arbitrary", "arbitrary")),
    )(dy, xbc_act, xbc_act, xbc_act, dte, cse, states, states, a_exp, *s_args)


SB_T = 256
SB_DROP = 104.0


def _sb_scores(qm, k_ref, ks, rowi, coli):
    kblk = k_ref[pl.ds(ks, SB_T), :].astype(MXU)
    z = _dot_nt(qm, kblk) * SB_SCALE
    mask = (ks + coli) < rowi
    sp = _softplus(z)
    return kblk, z, mask, sp, jnp.where(mask, sp, 0.0)


def _sb_stack(v):
    lane = _iota(v.shape, 1)
    return jnp.concatenate([jnp.where(lane < HEAD_DIM, v, 0.0), jnp.where(lane >= HEAD_DIM, v, 0.0)], axis=0)


def _sb_unstack(v):
    lane = _iota((SB_T, LANES), 1)
    return jnp.where(lane < HEAD_DIM, v[:SB_T], v[SB_T:])


def _sb_rows(qb):
    r = _iota((2 * SB_T, SB_T), 0)
    return qb * SB_T + jnp.where(r >= SB_T, r - SB_T, r), _iota((2 * SB_T, SB_T), 1)


def sb_fwd(projmain):
    S = projmain.shape[0]
    nq = S // SB_T

    def body(q_ref, k_ref, v_ref, o_ref, t_ref, n_ref):
        hp, qb = pl.program_id(0), pl.program_id(1)
        qst = _sb_stack(q_ref[...]).astype(MXU)
        rowi, coli = _sb_rows(qb)
        u_after = _tri(SB_T, lambda r, c: r > c)

        def cond(carry):
            i, rmin, _, _ = carry
            return (i <= qb) & (rmin < SB_DROP)

        def kstep(carry):
            i, _, r, acc = carry
            ks = pl.multiple_of((qb - i) * SB_T, SB_T)
            _, z, mask, sp, spm = _sb_scores(qst, k_ref, ks, rowi, coli)
            vblk = v_ref[pl.ds(ks, SB_T), :].astype(MXU)
            a = jnp.where(mask, jnp.exp(z - sp - _dot3_l(spm, u_after) - r), 0.0)
            acc = acc + _dot(a.astype(MXU), vblk)
            r = r + jnp.sum(spm, axis=1, keepdims=True)
            return i + 1, jnp.min(r), r, acc

        n, _, r, acc = lax.while_loop(
            cond, kstep, (jnp.int32(0), jnp.float32(0.0), jnp.zeros((2 * SB_T, 1), F32),
                          jnp.zeros((2 * SB_T, LANES), F32)))
        o_ref[...] = _sb_unstack(acc).astype(o_ref.dtype)
        t_ref[...] = _sb_unstack(jnp.broadcast_to(r, (2 * SB_T, LANES)))
        n_ref[hp, qb] = n

    return pl.pallas_call(
        body,
        name="sb_fwd",
        out_shape=[jax.ShapeDtypeStruct((S, SB_WIDTH), MXU), jax.ShapeDtypeStruct((S, SB_WIDTH), F32),
                   jax.ShapeDtypeStruct((SB_WIDTH // LANES, nq), jnp.int32)],
        grid=(SB_WIDTH // LANES, nq),
        in_specs=[
            pl.BlockSpec((SB_T, LANES), lambda h, i: (i, OFF_Q // LANES + h)),
            pl.BlockSpec((S, LANES), lambda h, i: (0, OFF_K // LANES + h)),
            pl.BlockSpec((S, LANES), lambda h, i: (0, OFF_V // LANES + h)),
        ],
        out_specs=[pl.BlockSpec((SB_T, LANES), lambda h, i: (i, h))] * 2
        + [pl.BlockSpec(memory_space=pltpu.SMEM)],
        compiler_params=_cparams(("arbitrary", "arbitrary")),
    )(projmain, projmain, projmain)


def sb_bwd(projmain, do, t_exp, nblk):
    S = projmain.shape[0]
    nq = S // SB_T

    def body(n_ref, q_ref, k_ref, v_ref, do_ref, t_ref, dq_ref, dk_ref, dv_ref):
        hp, qb = pl.program_id(0), pl.program_id(1)

        @pl.when(qb == 0)
        def _():
            dk_ref[...] = jnp.zeros_like(dk_ref)
            dv_ref[...] = jnp.zeros_like(dv_ref)

        qst = _sb_stack(q_ref[...]).astype(MXU)
        dost = _sb_stack(do_ref[...]).astype(MXU)
        tv = t_ref[...]
        lane = _iota((SB_T, LANES), 1)
        tot = jnp.concatenate(
            [jnp.sum(jnp.where(lane == HEAD_DIM * hh, tv, 0.0), axis=1, keepdims=True) for hh in range(2)], axis=0)
        rowi, coli = _sb_rows(qb)
        u_upto = _tri(SB_T, lambda r, c: r <= c)
        u_before = _tri(SB_T, lambda r, c: r < c)
        kept = jnp.clip(n_ref[hp, qb], 0, qb + 1)

        def kstep(kb, carry):
            psp, pg, dq = carry
            ks = pl.multiple_of(kb * SB_T, SB_T)
            kblk, z, mask, sp, spm = _sb_scores(qst, k_ref, ks, rowi, coli)
            vblk = v_ref[pl.ds(ks, SB_T), :].astype(MXU)
            after = tot - (psp + _dot3_l(spm, u_upto))
            a = jnp.where(mask, jnp.exp(z - sp - after), 0.0)
            gm = _dot_nt(dost, vblk) * a
            before = pg + _dot3_l(gm, u_before)
            sg = jnp.exp(z - sp)
            dz = jnp.where(mask, gm * (1.0 - sg) - sg * before, 0.0) * SB_SCALE
            dzb = dz.astype(MXU)
            dq = dq + _dot(dzb, kblk)
            dk_ref[pl.ds(ks, SB_T), :] += _dot_tn(dzb, qst)
            dv_ref[pl.ds(ks, SB_T), :] += _dot_tn(a.astype(MXU), dost)
            return (psp + jnp.sum(spm, axis=1, keepdims=True),
                    pg + jnp.sum(gm, axis=1, keepdims=True), dq)

        zero1 = jnp.zeros((2 * SB_T, 1), F32)
        dq = lax.fori_loop(qb + 1 - kept, qb + 1, kstep, (zero1, zero1, jnp.zeros((2 * SB_T, LANES), F32)))[2]
        dq_ref[...] = _sb_unstack(dq).astype(dq_ref.dtype)

    return pl.pallas_call(
        body,
        name="sb_bwd",
        out_shape=[
            jax.ShapeDtypeStruct((S, SB_WIDTH), MXU),
            jax.ShapeDtypeStruct((S, SB_WIDTH), F32),
            jax.ShapeDtypeStruct((S, SB_WIDTH), F32),
        ],
        grid=(SB_WIDTH // LANES, nq),
        in_specs=[
            pl.BlockSpec(memory_space=pltpu.SMEM),
            pl.BlockSpec((SB_T, LANES), lambda h, i: (i, OFF_Q // LANES + h)),
            pl.BlockSpec((S, LANES), lambda h, i: (0, OFF_K // LANES + h)),
            pl.BlockSpec((S, LANES), lambda h, i: (0, OFF_V // LANES + h)),
            pl.BlockSpec((SB_T, LANES), lambda h, i: (i, h)),
            pl.BlockSpec((SB_T, LANES), lambda h, i: (i, h)),
        ],
        out_specs=[
            pl.BlockSpec((SB_T, LANES), lambda h, i: (i, h)),
            pl.BlockSpec((S, LANES), lambda h, i: (0, h)),
            pl.BlockSpec((S, LANES), lambda h, i: (0, h)),
        ],
        compiler_params=_cparams(("arbitrary", "arbitrary")),
    )(nblk, projmain, projmain, projmain, do, t_exp)


def local_step(x, pb, target, W, P, late, core):
    D = D_MODEL
    full = lambda a, w=D: (a, 0, w)

    (n1b,) = rowwise("norm_pre", lambda xv, g: (_rms(xv, g),), [full(x)], [full(P["norm_mix_pre"])],
                     [(D, D, MXU)])
    projmain = mm("in_proj", n1b, W["main"], "nn")
    gate_pre = mm("gate_proj", n1b, W["gate"], "nn", b_slabs=True)
    dtraw = mm("dt_proj", n1b, W["dt"], "nn")
    xbc_act = conv_fwd(projmain, P["conv_w"], P["conv_b"])
    dte, cse = ssd_prep(dtraw, P["dt_bias_pad"], P["a_exp"])
    y_scan, states, *got = ssd_fwd(xbc_act, dte, cse, side=exchange_plan([("gather2", late[n]) for n in LATE]))
    W = {**W, **late_matrices(dict(zip(LATE, got)))}

    def f_gate(ysc, xs, z, dsk, nw):
        return _rms((ysc + xs * dsk) * (z * _sig(z)), nw)

    (y_ssd_b,) = rowwise("ssd_gate", lambda *a: (f_gate(*a),),
                         [(y_scan, 0, GW), (xbc_act, 0, GW), (projmain, OFF_Z, GW)],
                         [(P["dsk_exp"], 0, GW), (P["ssd_norm"], 0, GW)], [(D_INNER, GW, MXU)], ncb=SSD_GROUPS)
    y_sb_b, t_exp, sb_kept = sb_fwd(projmain)
    u1 = mm("ssd_branch", y_ssd_b, W["ssd"], "nn")
    u2 = mm("sb_branch", y_sb_b, W["sb"], "nn")

    def f_merge(a1, a2, g1, g2, b1, b2):
        return (_sig(g1 + b1) * a1 + _sig(g2 + b2) * a2,)

    gate_rows = [(gate_pre, 0, D), (gate_pre, D, D)]
    gate_bias = [(P["b_gate"], 0, D), (P["b_gate"], D, D)]
    (merged_b,) = rowwise("merge", f_merge, [full(u1), full(u2)] + gate_rows, gate_bias, [(D, D, MXU)])
    mo = mm("out_proj", merged_b, W["out"], "nn")

    def f_mid(xv, m, gpost, gffn):
        h1 = xv + _rms(m, gpost)
        return h1, _rms(h1, gffn)

    h1, n2b = rowwise("mix_post", f_mid, [full(x), full(mo)],
                      [full(P["norm_mix_post"]), full(P["norm_ffn_pre"])], [(D, D, F32), (D, D, MXU)])
    a_ff, rb = mm("ff1", n2b, W["ff1"], "nn", b_slabs=True,
                  extra=[(MXU, lambda acc: jnp.square(jnp.maximum(acc, 0.0)))])
    ff = mm("ff2", rb, W["ff2"], "nn")

    def f_ffn_post(h, f, g):
        h2 = h + _rms(f, g)
        return h2, h2

    h2, h2b = rowwise("ffn_post", f_ffn_post, [full(h1), full(ff)], [full(P["norm_ffn_post"])],
                      [(D, D, F32), (D, D, MXU)])
    pgp = mm("ple_gate", h2b, W["pg"], "nn")
    pe = mm("ple_proj", pb, W["ple"], "nn", b_slabs=True)

    def f_ple(h2v, gp, pev, tgt, g):
        f = lambda h, a, b, gg: h + _rms(_sig(a) * b, gg)
        h3, vjp = jax.vjp(f, h2v, gp, pev, g)
        err = h3 - tgt
        dh2, dgp, dpe, dg = vjp(err * (1.0 / D))
        lossc = (0.5 / D) * jnp.sum(err * err, axis=0, keepdims=True)
        return dh2, dgp, dpe, dg, lossc

    dh2a, dpgp_b, dpe_b, g_ple, lossc = rowwise(
        "ple_loss", f_ple, [full(h2), full(pgp), full(pe), full(target)], [full(P["norm_ple_post"])],
        [(D, D, F32), (D, D, MXU), (D, D, MXU)], [(D, D), (D, D)])
    dh2 = mm("d_ple_gate_x", dpgp_b, W["pg"], "nt", add=dh2a)
    gW = {}
    gW["w_ple_gate"] = mm("d_ple_gate_w", h2b, dpgp_b, "tn")
    gW["w_ple"] = mm("d_ple_w", pb, dpe_b, "tn", out_slabs=True)

    def b_ffn_post(d, f, g):
        _, vjp = jax.vjp(_rms, f, g)
        return vjp(d)

    dff_b, g_ffn_post = rowwise("d_ffn_post", b_ffn_post, [full(dh2), full(ff)], [full(P["norm_ffn_post"])],
                                [(D, D, MXU)], [(D, D)])
    da_b = mm("d_ff2_x", dff_b, W["ff2"], "nt", out_dtype=MXU,
              epi=lambda acc, act: acc * (2.0 * jnp.maximum(act, 0.0)), epi_args=[a_ff])
    gW["w_ff2"] = mm("d_ff2_w", rb, dff_b, "tn")
    dn2 = mm("d_ff1_x", da_b, W["ff1"], "nt", b_slabs=True)
    gW["w_ff1"] = mm("d_ff1_w", n2b, da_b, "tn", out_slabs=True)

    def b_mid(d2, dn, h, m, gpost, gffn):
        _, vjp = jax.vjp(_rms, h, gffn)
        dh, dgffn = vjp(dn)
        dh1 = d2 + dh
        _, vjp2 = jax.vjp(_rms, m, gpost)
        dm, dgpost = vjp2(dh1)
        return dh1, dm, dgpost, dgffn

    dh1, dmo_b, g_mix_post, g_ffn_pre = rowwise(
        "d_mix_post", b_mid, [full(dh2), full(dn2), full(h1), full(mo)],
        [full(P["norm_mix_post"]), full(P["norm_ffn_pre"])], [(D, D, F32), (D, D, MXU)], [(D, D), (D, D)])
    dmerged = mm("d_out_x", dmo_b, W["out"], "nt")
    gW["w_out"] = mm("d_out_w", merged_b, dmo_b, "tn")

    def b_merge(d, a1, a2, g1, g2, b1, b2):
        s1, s2 = _sig(g1 + b1), _sig(g2 + b2)
        dg1 = d * a1 * s1 * (1.0 - s1)
        dg2 = d * a2 * s2 * (1.0 - s2)
        dg = jnp.concatenate([dg1, dg2], axis=1)
        return d * s1, d * s2, dg, jnp.sum(dg, axis=0, keepdims=True)

    du1_b, du2_b, dgp_b, g_b_gate = rowwise(
        "d_merge", b_merge, [full(dmerged), full(u1), full(u2)] + gate_rows, gate_bias,
        [(D, D, MXU), (D, D, MXU), (2 * D, 2 * D, MXU)], [(2 * D, 2 * D)])
    dy_ssd = mm("d_ssd_branch_x", du1_b, W["ssd"], "nt")
    dy_sb = mm("d_sb_branch_x", du2_b, W["sb"], "nt")
    gW["w_ssd_branch"] = mm("d_ssd_branch_w", y_ssd_b, du1_b, "tn")
    gW["w_sb_branch"] = mm("d_sb_branch_w", y_sb_b, du2_b, "tn")

    def b_gate(d, ysc, xs, z, dsk, nw):
        _, vjp = jax.vjp(f_gate, ysc, xs, z, dsk, nw)
        return vjp(d)

    slabs = {n: gW[n] if n in COL_SHARDED else gW[n].reshape((N_DEV,) + ROW_SHARDED[n]) for n in LATE}
    dy_scan, dxs_skip, dz_b, g_dsk_exp, g_ssd_norm, *from_sibling = rowwise(
        "d_ssd_gate", b_gate, [(dy_ssd, 0, GW), (y_scan, 0, GW), (xbc_act, 0, GW), (projmain, OFF_Z, GW)],
        [(P["dsk_exp"], 0, GW), (P["ssd_norm"], 0, GW)],
        [(D_INNER, GW, F32), (D_INNER, GW, F32), (D_INNER, GW, MXU)], [(D_INNER, GW), (D_INNER, GW)],
        ncb=SSD_GROUPS, side=exchange_plan([("pair", slabs[n]) for n in LATE]))
    chip_sums = [pair_sum("pair_sum_" + n, slabs[n], s, core) for n, s in zip(LATE, from_sibling)]
    dxs, d_b, d_c, ddt_part, g_a_exp, *late_parts = ssd_bwd(
        dy_scan, xbc_act, dte, cse, states, P["a_exp"], side=exchange_plan([("chips", s) for s in chip_sums]))

    def b_dt(dpart, dtr, bias):
        ddt = _dot3_l(dpart, _head_reduce())
        d = ddt * _sig(dtr + bias)
        return d, jnp.sum(d, axis=0, keepdims=True)

    ddt_b, g_dt_bias_pad = rowwise("d_dt", b_dt, [(ddt_part, 0, D_INNER), (dtraw, 0, LANES)],
                                   [(P["dt_bias_pad"], 0, LANES)], [(LANES, LANES, MXU)], [(LANES, LANES)])
    dxbc_b, g_conv_w, g_conv_b = conv_bwd(dxs, dxs_skip, d_b, d_c, projmain, P["conv_w"], P["conv_b"])
    dq_b, dk, dv = sb_bwd(projmain, dy_sb, t_exp, sb_kept)
    dmain_b = jnp.concatenate([dz_b, dxbc_b, dq_b, dk.astype(MXU), dv.astype(MXU)], axis=1)
    dn1_dt = mm("d_dt_x", ddt_b, W["dt"], "nt")
    dn1_gate = mm("d_gate_x", dgp_b, W["gate"], "nt", add=dn1_dt, b_slabs=True)
    dn1 = mm("d_in_x", dmain_b, W["main"], "nt", add=dn1_gate)
    gW["main"] = mm("d_in_w", n1b, dmain_b, "tn")
    gW["dt"] = mm("d_dt_w", n1b, ddt_b, "tn")
    gW["w_gate"] = mm("d_gate_w", n1b, dgp_b, "tn", out_slabs=True)

    def b_pre(d1, dn, xv, g):
        _, vjp = jax.vjp(_rms, xv, g)
        dx, dg = vjp(dn)
        return d1 + dx, dg

    grad_x, g_mix_pre = rowwise("d_norm_pre", b_pre, [full(dh1), full(dn1), full(x)], [full(P["norm_mix_pre"])],
                                [(D, D, F32)], [(D, D)])

    g_in = {"main": gW["main"], "dt": gW["dt"], "w_gate": gW["w_gate"], "conv_w": g_conv_w}
    gS = {
        "norm_mix_pre": g_mix_pre, "conv_b": g_conv_b, "dt_bias_pad": g_dt_bias_pad, "a_exp": g_a_exp,
        "dsk_exp": g_dsk_exp, "ssd_norm": g_ssd_norm, "b_gate": g_b_gate,
        "norm_mix_post": g_mix_post, "norm_ffn_pre": g_ffn_pre, "norm_ffn_post": g_ffn_post,
        "norm_ple_post": g_ple,
    }
    return lossc, grad_x, g_in, dict(zip(LATE, late_parts)), gS


IN_SPLITS = (2048, 6144, 6176, 7200, 8224)
IN_SHARD = 1156
IN_PAD = 1280
COL_SHARDED = {"w_in": (1024, IN_SHARD), "conv_w": (4, 512), "w_gate": (1024, 256), "w_ff1": (1024, 512),
               "w_ple": (256, 128)}
ROW_SHARDED = {"w_ssd_branch": (256, 1024), "w_sb_branch": (128, 1024), "w_out": (128, 1024),
               "w_ff2": (512, 1024), "w_ple_gate": (128, 1024)}
SHARDED = tuple(COL_SHARDED) + tuple(ROW_SHARDED)
SMALL = (
    ("norm_mix_pre", 1024), ("conv_b", 4096), ("dt_bias", 32), ("a_log", 32), ("d_skip", 32), ("ssd_norm", 2048),
    ("b_gate", 2048), ("norm_mix_post", 1024), ("norm_ffn_pre", 1024), ("norm_ffn_post", 1024),
    ("norm_ple_post", 1024),
)
ROW = 1024
SMALL_ROWS = 16


def _rows_of(n):
    return -(-n // ROW)


def _pad_to(v, n, axis=-1):
    pad = [(0, 0)] * v.ndim
    pad[axis] = (0, n - v.shape[axis])
    return jnp.pad(v, pad)


LATE = ("w_ssd_branch", "w_sb_branch", "w_out", "w_ff1", "w_ff2", "w_ple", "w_ple_gate")


def in_matrices(w_in_slabs, w_gate_slabs):
    w_in = jnp.concatenate([w_in_slabs[k, :, :IN_SHARD] for k in range(N_DEV)], axis=1)
    return {
        "main": jnp.concatenate([w_in[:, :IN_SPLITS[1]], w_in[:, IN_SPLITS[2]:]], axis=1),
        "dt": _pad_to(w_in[:, IN_SPLITS[1]:IN_SPLITS[2]], LANES),
        "gate": w_gate_slabs,
    }


def late_matrices(g):
    stack = lambda a: a.reshape(a.shape[0] * a.shape[1], a.shape[2])
    return {
        "ff1": g["w_ff1"], "ple": g["w_ple"],
        "ssd": stack(g["w_ssd_branch"]), "sb": stack(g["w_sb_branch"]), "out": stack(g["w_out"]),
        "ff2": stack(g["w_ff2"]), "pg": stack(g["w_ple_gate"]),
    }


def build_small(small, conv_w_slabs):
    P = {k: small[k] for k in ("norm_mix_pre", "conv_b", "ssd_norm", "b_gate", "norm_mix_post", "norm_ffn_pre",
                               "norm_ffn_post", "norm_ple_post")}
    P["conv_w"] = jnp.concatenate([conv_w_slabs[k] for k in range(N_DEV)], axis=1)
    P["dt_bias_pad"] = _pad_to(small["dt_bias"], LANES)
    P["a_exp"] = jnp.repeat(-jnp.exp(small["a_log"]), HEAD_DIM, axis=1)
    P["dsk_exp"] = jnp.repeat(small["d_skip"], HEAD_DIM, axis=1)
    return P


def small_grads(gS, small):
    heads = lambda a: a.reshape(SSD_HEADS, HEAD_DIM).sum(axis=1)[None, :]
    out = {k: gS[k] for k in ("norm_mix_pre", "conv_b", "ssd_norm", "b_gate", "norm_mix_post", "norm_ffn_pre",
                              "norm_ffn_post", "norm_ple_post")}
    out["dt_bias"] = gS["dt_bias_pad"][:, :SSD_HEADS]
    out["a_log"] = heads(gS["a_exp"]) * (-jnp.exp(small["a_log"]))
    out["d_skip"] = heads(gS["dsk_exp"])
    return out


def in_grad_slabs(g):
    pieces = ((0, IN_SPLITS[1], g["main"], 0), (IN_SPLITS[1], IN_SPLITS[2], g["dt"], -IN_SPLITS[1]),
              (IN_SPLITS[2], N_DEV * IN_SHARD, g["main"], IN_SPLITS[1] - IN_SPLITS[2]))
    slabs = []
    for k in range(N_DEV):
        a, b = IN_SHARD * k, IN_SHARD * (k + 1)
        cut = [src[:, max(a, lo) + off:min(b, hi) + off] for lo, hi, src, off in pieces if max(a, lo) < min(b, hi)]
        slabs.append(_pad_to(jnp.concatenate(cut, axis=1), IN_PAD))
    width = COL_SHARDED["conv_w"][1]
    return jnp.stack(slabs), jnp.stack([g["conv_w"][:, width * k:width * (k + 1)] for k in range(N_DEV)])


def pack_small(get):
    cols = [_pad_to(get(name).reshape(n), _rows_of(n) * ROW) for name, n in SMALL]
    return jnp.concatenate(cols).reshape(SMALL_ROWS, ROW)


def unpack_small(flat):
    out, r0 = {}, 0
    for name, n in SMALL:
        rows = _rows_of(n)
        out[name] = flat[r0:r0 + rows].reshape(rows * ROW)[:n].reshape(1, n)
        r0 += rows
    return out


N_CHIPS = 4
JOB_SEMS = {"all": 7, "scatter": 7, "gather2": 7, "pair": 4, "chips": 3}


def exchange_plan(jobs):
    n = len(jobs)
    kinds = [k for k, _ in jobs]
    srcs = [s for _, s in jobs]
    shapes = {"all": lambda s: (N_DEV,) + s.shape, "gather2": lambda s: (N_DEV,) + s.shape,
              "scatter": lambda s: s.shape, "pair": lambda s: (N_CHIPS,) + s.shape[1:], "chips": lambda s: s.shape}
    out_shape = [jax.ShapeDtypeStruct(shapes[k](s), s.dtype) for k, s in jobs]
    offs = [sum(JOB_SEMS[k] for k in kinds[:i]) for i in range(n + 1)]

    def run(phases, src, out, send_sems, recv_sems, local_sems):
        x, y, c = lax.axis_index("x"), lax.axis_index("y"), lax.axis_index("c")
        dev = lambda d: 4 * d[0] + 2 * d[1] + d[2]
        chip_no = lambda ch: 2 * ch[0] + ch[1]
        me, sib, my_chip = (x, y, c), (x, y, 1 - c), (x, y)
        others = [(1 - x, y), (x, 1 - y), (1 - x, 1 - y)]
        all_chips = [(0, 0), (0, 1), (1, 0), (1, 1)]
        peers = [(1 - x if k & 4 else x, 1 - y if k & 2 else y, 1 - c if k & 1 else c) for k in range(1, N_DEV)]
        starts, recvs, local = [], [], []
        chained = [[] for _ in others]

        for i, kind in enumerate(kinds):
            s_ref, o_ref = src[i], out[i]

            def rc(k, src_ref, dst_ref, to, i=i):
                s = offs[i] + k
                return pltpu.make_async_remote_copy(src_ref=src_ref, dst_ref=dst_ref, send_sem=send_sems.at[s],
                                                    recv_sem=recv_sems.at[s], device_id=to,
                                                    device_id_type=pl.DeviceIdType.MESH)

            if kind == "all":
                local.append(pltpu.make_async_copy(s_ref, o_ref.at[dev(me)], local_sems.at[i]))
                for k, peer in enumerate(peers):
                    starts.append(rc(k, s_ref, o_ref.at[dev(me)], peer))
                    recvs.append(rc(k, s_ref, o_ref.at[dev(peer)], peer))
            elif kind == "scatter":
                local.append(pltpu.make_async_copy(s_ref.at[dev(me)], o_ref.at[dev(me)], local_sems.at[i]))
                for k, peer in enumerate(peers):
                    starts.append(rc(k, s_ref.at[dev(peer)], o_ref.at[dev(me)], peer))
                    recvs.append(rc(k, s_ref.at[dev(me)], o_ref.at[dev(peer)], peer))
            elif kind == "gather2":
                local.append(pltpu.make_async_copy(s_ref, o_ref.at[dev(me)], local_sems.at[i]))
                starts.append(rc(0, s_ref, o_ref.at[dev(me)], sib))
                recvs.append(rc(0, s_ref, o_ref.at[dev(sib)], sib))
                for j, ch in enumerate(others):
                    same, other = (*ch, c), (*ch, 1 - c)
                    starts.append(rc(1 + j, s_ref, o_ref.at[dev(me)], same))
                    chained[j].append((rc(1 + j, s_ref, o_ref.at[dev(same)], same),
                                       rc(4 + j, o_ref.at[dev(same)], o_ref.at[dev(same)], sib)))
                    recvs.append(rc(4 + j, s_ref, o_ref.at[dev(other)], sib))
            elif kind == "pair":
                for j, ch in enumerate(all_chips):
                    starts.append(rc(j, s_ref.at[dev((*ch, 1 - c))], o_ref.at[j], sib))
                    recvs.append(rc(j, s_ref.at[dev((*ch, c))], o_ref.at[j], sib))
            else:
                mine = chip_no(my_chip)
                local.append(pltpu.make_async_copy(s_ref.at[mine], o_ref.at[mine], local_sems.at[i]))
                for j, ch in enumerate(others):
                    starts.append(rc(j, s_ref.at[chip_no(ch)], o_ref.at[mine], (*ch, c)))
                    recvs.append(rc(j, s_ref.at[mine], o_ref.at[chip_no(ch)], (*ch, c)))

        if "start" in phases:
            for cp in local + starts:
                cp.start()
        if "pass" in phases:
            for group in chained:
                for arrival, forward in group:
                    arrival.wait_recv()
                    forward.start()
        if "finish" in phases:
            for cp in recvs:
                cp.wait_recv()
            for cp in starts + [forward for group in chained for _, forward in group]:
                cp.wait_send()
            for cp in local:
                cp.wait()

    scratch = [pltpu.SemaphoreType.DMA((offs[n],)), pltpu.SemaphoreType.DMA((offs[n],)),
               pltpu.SemaphoreType.DMA((n,))]
    return srcs, out_shape, scratch, run


EXCHANGE_PHASES = ("start", "pass", "finish")


def exchange(name, jobs):
    srcs, out_shape, scratch, run = exchange_plan(jobs)
    n = len(srcs)

    def body(*refs):
        run(EXCHANGE_PHASES, refs[:n], refs[n:2 * n], *refs[2 * n:])

    any_spec = pl.BlockSpec(memory_space=pl.ANY)
    return pl.pallas_call(body, name=name, out_shape=out_shape, in_specs=[any_spec] * n, out_specs=[any_spec] * n,
                          scratch_shapes=scratch)(*srcs)


def host_exchange(body, n_in, n_out, grid, plan):
    if plan is None:
        return body, [], [], [], [], []
    srcs, out_shape, scratch, run = plan
    n = len(srcs)
    steps = grid[0] * grid[1]

    def hosted(*refs):
        ins, side_in = refs[:n_in], refs[n_in:n_in + n]
        outs, side_out = refs[n_in + n:n_in + n + n_out], refs[n_in + n + n_out:n_in + 2 * n + n_out]
        rest = refs[n_in + 2 * n + n_out:]
        own, sems = rest[:len(rest) - 3], rest[len(rest) - 3:]
        step = pl.program_id(0) * grid[1] + pl.program_id(1)

        def at(when, phase):
            @pl.when(step == when)
            def _():
                run((phase,), side_in, side_out, *sems)

        at(0, "start")
        body(*ins, *outs, *own)
        at(steps // 2, "pass")
        at(steps - 1, "finish")

    any_spec = pl.BlockSpec(memory_space=pl.ANY)
    return hosted, [any_spec] * n, list(srcs), list(out_shape), [any_spec] * n, list(scratch)


def pair_sum(name, g, sib, core):
    _, r, c = g.shape
    tr = _pick(r, (256, 128, 64, 32, 16, 8))

    def body(core_ref, g_ref, s_ref, o_ref):
        o_ref[...] = (g_ref[...] + s_ref[...]).astype(o_ref.dtype)

    return pl.pallas_call(
        body,
        name=name,
        out_shape=jax.ShapeDtypeStruct((N_CHIPS, r, c), MXU),
        grid_spec=pltpu.PrefetchScalarGridSpec(
            num_scalar_prefetch=1,
            grid=(N_CHIPS, r // tr),
            in_specs=[
                pl.BlockSpec((None, None, tr, c), lambda j, i, core_ref: (j, core_ref[0], i, 0)),
                pl.BlockSpec((None, tr, c), lambda j, i, core_ref: (j, i, 0)),
            ],
            out_specs=pl.BlockSpec((None, tr, c), lambda j, i, core_ref: (j, i, 0)),
        ),
        compiler_params=_cparams(("parallel", "parallel")),
    )(core, g.reshape(N_CHIPS, 2, r, c), sib)


def adamw(name, parts, w, m, v):
    rows, cols = w.shape
    tr = _pick(rows, (128, 64, 32, 16, 8))

    nparts = parts.shape[0]

    def body(p_ref, w_ref, m_ref, v_ref, g_ref, d_ref, m2_ref, v2_ref):
        g = p_ref[0].astype(F32)
        for k in range(1, nparts):
            g = g + p_ref[k].astype(F32)
        m2 = ADAM_B1 * m_ref[...] + (1.0 - ADAM_B1) * g
        v2 = ADAM_B2 * v_ref[...] + (1.0 - ADAM_B2) * jnp.square(g)
        m_hat = m2 / (1.0 - ADAM_B1 ** ADAM_STEP)
        v_hat = v2 / (1.0 - ADAM_B2 ** ADAM_STEP)
        g_ref[...] = g
        d_ref[...] = -ADAM_LR * (m_hat / (jnp.sqrt(v_hat) + ADAM_EPS) + ADAM_WD * w_ref[...])
        m2_ref[...] = m2
        v2_ref[...] = v2

    spec = pl.BlockSpec((tr, cols), lambda i: (i, 0))
    return pl.pallas_call(
        body,
        name=name,
        out_shape=[jax.ShapeDtypeStruct((rows, cols), F32)] * 4,
        grid=(rows // tr,),
        in_specs=[pl.BlockSpec((nparts, tr, cols), lambda i: (0, i, 0)), spec, spec, spec],
        out_specs=[spec] * 4,
        compiler_params=_cparams(("parallel",)),
    )(parts, w, m, v)


WEIGHT_ORDER = (
    "norm_mix_pre", "w_in", "conv_w", "conv_b", "dt_bias", "a_log", "d_skip", "ssd_norm", "w_ssd_branch",
    "w_sb_branch", "w_gate", "b_gate", "w_out", "norm_mix_post", "norm_ffn_pre", "w_ff1", "w_ff2", "norm_ffn_post",
    "w_ple", "w_ple_gate", "norm_ple_post",
)


def kernel(x, p, norm_mix_pre, w_in, conv_w, conv_b, dt_bias, a_log, d_skip, ssd_norm, w_ssd_branch, w_sb_branch, w_gate, b_gate, w_out, norm_mix_post, norm_ffn_pre, w_ff1, w_ff2, norm_ffn_post, w_ple, w_ple_gate, norm_ple_post, loss_target, m_norm_mix_pre, m_w_in, m_conv_w, m_conv_b, m_dt_bias, m_a_log, m_d_skip, m_ssd_norm, m_w_ssd_branch, m_w_sb_branch, m_w_gate, m_b_gate, m_w_out, m_norm_mix_post, m_norm_ffn_pre, m_w_ff1, m_w_ff2, m_norm_ffn_post, m_w_ple, m_w_ple_gate, m_norm_ple_post, v_norm_mix_pre, v_w_in, v_conv_w, v_conv_b, v_dt_bias, v_a_log, v_d_skip, v_ssd_norm, v_w_ssd_branch, v_w_sb_branch, v_w_gate, v_b_gate, v_w_out, v_norm_mix_post, v_norm_ffn_pre, v_w_ff1, v_w_ff2, v_norm_ffn_post, v_w_ple, v_w_ple_gate, v_norm_ple_post):
    a = dict(locals())
    seq = x.shape[1]
    x2 = x.reshape(seq, D_MODEL)
    target = loss_target.reshape(seq, D_MODEL)
    pb = p.reshape(seq, PLE_DIM).astype(MXU)

    def shard(prefix, name, dtype):
        v = a[prefix + name][0].astype(dtype)
        return _pad_to(v, IN_PAD) if name == "w_in" else v

    w_in_slabs, w_gate_slabs, conv_w_slabs = exchange(
        "gather_weights", [("gather2", shard("", "w_in", MXU)), ("gather2", shard("", "w_gate", MXU)),
                           ("all", a["conv_w"][0])])
    small = {n: a[n] for n, _ in SMALL}
    core = lax.axis_index("c").astype(jnp.int32).reshape(1)

    lossc, grad_x, g_in, late_parts, g_acc = local_step(
        x2, pb, target, in_matrices(w_in_slabs, w_gate_slabs), build_small(small, conv_w_slabs),
        {n: shard("", n, MXU) for n in LATE}, core)
    loss = lax.psum(jnp.sum(lossc), ("x", "y", "c"))
    g_small = small_grads(g_acc, small)

    last = {"w_gate": g_in["w_gate"]}
    last["w_in"], conv_slabs = in_grad_slabs(g_in)
    from_sibling = exchange("pair_grads", [("pair", last[n]) for n in ("w_in", "w_gate")])
    chip_sums = [pair_sum("pair_sum_" + n, last[n], s, core) for n, s in zip(("w_in", "w_gate"), from_sibling)]
    w_in_parts, w_gate_parts, conv_parts, small_parts = exchange(
        "chip_grads", [("chips", s) for s in chip_sums]
        + [("scatter", conv_slabs), ("all", pack_small(lambda n: g_small[n]))])
    parts = {**late_parts, "w_in": w_in_parts, "w_gate": w_gate_parts, "conv_w": conv_parts}

    leaves = {}
    for n, part in parts.items():
        res = adamw("adamw_" + n, part, shard("", n, F32), shard("m_", n, F32), shard("v_", n, F32))
        leaves[n] = [r[None, :, :IN_SHARD] if n == "w_in" else r[None] for r in res]
    res = adamw("adamw_small", small_parts, pack_small(lambda n: a[n]), pack_small(lambda n: a["m_" + n]),
                pack_small(lambda n: a["v_" + n]))
    for j, r in enumerate(res):
        for n, leaf in unpack_small(r).items():
            leaves.setdefault(n, [None] * 4)[j] = leaf
    outs = [loss, grad_x.reshape(x.shape)]
    for j in range(4):
        outs += [leaves[n][j] for n in WEIGHT_ORDER]
    return tuple(outs)
```

```python
import functools

import jax
import jax.numpy as jnp
from jax import lax
from jax.experimental import pallas as pl
from jax.experimental.pallas import tpu as pltpu

F32 = jnp.float32
MXU = jnp.bfloat16
VMEM_LIMIT = 56 * 1024 * 1024

D_MODEL = 1024
D_INNER = 2048
SSD_HEADS = 32
HEAD_DIM = 64
SSD_GROUPS = 8
D_STATE = 128
CONV_K = 4
CONV_DIM = 4096
CHUNK = 128
SB_WIDTH = 1024
D_FF = 4096
PLE_DIM = 256
RMS_EPS = 1e-6
SB_SCALE = HEAD_DIM ** -0.5
N_DEV = 8
LANES = 128

OFF_Z, OFF_XBC, OFF_Q, OFF_K, OFF_V = 0, 2048, 6144, 7168, 8192

ADAM_LR = 0.001
ADAM_B1 = 0.9
ADAM_B2 = 0.999
ADAM_EPS = 1e-08
ADAM_WD = 0.01
ADAM_STEP = 10


def _sig(x):
    return 1.0 / (1.0 + jnp.exp(-x))


def _softplus(x):
    return jnp.maximum(x, 0.0) + jnp.log(1.0 + jnp.exp(-jnp.abs(x)))


def _rms(x, w):
    return x * lax.rsqrt(jnp.mean(x * x, axis=-1, keepdims=True) + RMS_EPS) * w


def _dot(a, b):
    return jnp.dot(a, b, preferred_element_type=F32)


def _dot_nt(a, b):
    return lax.dot_general(a, b, (((1,), (1,)), ((), ())), preferred_element_type=F32)


def _dot_tn(a, b):
    return lax.dot_general(a, b, (((0,), (0,)), ((), ())), preferred_element_type=F32)


def _split3(x):
    x1 = x.astype(MXU)
    r = x - x1.astype(F32)
    x2 = r.astype(MXU)
    r = r - x2.astype(F32)
    return x1, x2, r.astype(MXU)


def _dot3_l(a, u):
    m = a.shape[0]
    d = _dot(jnp.concatenate(_split3(a), axis=0), u)
    return (d[2 * m:] + d[m:2 * m]) + d[:m]


def _dot3_r(u, a):
    n = a.shape[1]
    d = _dot(u, jnp.concatenate(_split3(a), axis=1))
    return (d[:, 2 * n:] + d[:, n:2 * n]) + d[:, :n]


def _iota(shape, dim):
    return lax.broadcasted_iota(jnp.int32, shape, dim)


def _tri(n, cmp):
    r, c = _iota((n, n), 0), _iota((n, n), 1)
    return cmp(r, c).astype(F32).astype(MXU)


def _cparams(sem):
    return pltpu.CompilerParams(dimension_semantics=sem, vmem_limit_bytes=VMEM_LIMIT)


def _pick(n, cands):
    for c in cands:
        if n % c == 0:
            return c
    return n


def mm(name, a, b, mode, add=None, out_dtype=F32, b_slabs=False, out_slabs=False, epi=None, epi_args=(), extra=(),
       side=None):
    slab = None
    if b_slabs:
        slab = b.shape[2]
        bshape = (b.shape[1], N_DEV * slab)
    else:
        bshape = b.shape
    if mode == "nn":
        (M, K), (K2, N) = a.shape, bshape
    elif mode == "nt":
        (M, K), (N, K2) = a.shape, bshape
    else:
        (K, M), (K2, N) = a.shape, bshape
    assert K == K2, (name, a.shape, b.shape)
    tm = _pick(M, (1024, 512, 256, 128))
    tn = _pick(N, (1024, 512, 256, 128))
    tk = _pick(K, (1024, 512, 256, 128))
    if b_slabs and mode == "nn":
        tn = slab
    if b_slabs and mode == "nt":
        tk = slab
    if out_slabs:
        assert mode == "tn" and N % N_DEV == 0
        tn = N // N_DEV
    nk = K // tk

    def body(*refs):
        refs = list(refs)
        a_ref, b_ref = refs[:2]
        add_ref = refs[2] if add is not None else None
        n_in = 2 + (add is not None)
        epi_refs = refs[n_in:n_in + len(epi_args)]
        o_ref = refs[n_in + len(epi_args)]
        extra_refs = refs[n_in + len(epi_args) + 1:-1]
        acc = refs[-1]
        k = pl.program_id(2)

        @pl.when(k == 0)
        def _():
            acc[...] = jnp.zeros_like(acc) if add is None else add_ref[...]

        av, bv = a_ref[...], b_ref[...]
        if mode == "nn":
            acc[...] += _dot(av, bv)
        elif mode == "nt":
            acc[...] += _dot_nt(av, bv)
        else:
            acc[...] += _dot_tn(av, bv)

        @pl.when(k == nk - 1)
        def _():
            res = acc[...]
            main = res if epi is None else epi(res, *[r[...] for r in epi_refs])
            o_ref[...] = main.astype(o_ref.dtype)
            for r, (_, fn) in zip(extra_refs, extra):
                r[...] = fn(res).astype(r.dtype)

    if mode == "nn":
        a_spec = pl.BlockSpec((tm, tk), lambda i, j, k: (i, k))
        b_spec = pl.BlockSpec((tk, tn), lambda i, j, k: (k, j))
    elif mode == "nt":
        a_spec = pl.BlockSpec((tm, tk), lambda i, j, k: (i, k))
        b_spec = pl.BlockSpec((tn, tk), lambda i, j, k: (j, k))
    else:
        a_spec = pl.BlockSpec((tk, tm), lambda i, j, k: (k, i))
        b_spec = pl.BlockSpec((tk, tn), lambda i, j, k: (k, j))
    if b_slabs and mode == "nn":
        b_spec = pl.BlockSpec((None, tk, tn), lambda i, j, k: (j, k, 0))
    if b_slabs and mode == "nt":
        b_spec = pl.BlockSpec((None, tn, tk), lambda i, j, k: (k, j, 0))
    o_spec = pl.BlockSpec((tm, tn), lambda i, j, k: (i, j))
    in_specs, args = [a_spec, b_spec], [a, b]
    if add is not None:
        in_specs.append(o_spec)
        args.append(add)
    for e in epi_args:
        in_specs.append(o_spec)
        args.append(e)
    out_sds = jax.ShapeDtypeStruct((M, N), out_dtype)
    if out_slabs:
        o_spec = pl.BlockSpec((None, tm, tn), lambda i, j, k: (j, i, 0))
        out_sds = jax.ShapeDtypeStruct((N_DEV, M, tn), out_dtype)
    grid = (M // tm, N // tn, nk)
    body, s_in, s_args, s_shape, s_out, s_scr = host_exchange(body, len(args), 1 + len(extra), grid, side)
    res = pl.pallas_call(
        body,
        name=name,
        out_shape=[out_sds] + [jax.ShapeDtypeStruct((M, N), dt) for dt, _ in extra] + s_shape,
        grid=grid,
        in_specs=in_specs + s_in,
        out_specs=[o_spec] * (1 + len(extra)) + s_out,
        scratch_shapes=[pltpu.VMEM((tm, tn), F32)] + s_scr,
        compiler_params=_cparams(("parallel", "parallel", "arbitrary") if side is None else ("arbitrary",) * 3),
    )(*args, *s_args)
    return res if extra or side is not None else res[0]


def rowwise(name, fn, rows, bcast, outs, accs=(), tr=256, ncb=1, side=None):
    S = rows[0][0].shape[0]
    tr = min(tr, S)
    nrb = S // tr
    in_specs, args = [], []
    for arr, off, w in rows:
        assert off % w == 0 and arr.shape[0] == S
        in_specs.append(pl.BlockSpec((tr, w), lambda j, i, ob=off // w: (i, ob + j)))
        args.append(arr)
    for arr, off, w in bcast:
        assert off % w == 0
        in_specs.append(pl.BlockSpec((arr.shape[0], w), lambda j, i, ob=off // w: (0, ob + j)))
        args.append(arr)
    out_shape, out_specs = [], []
    for tw, w, dt in outs:
        out_shape.append(jax.ShapeDtypeStruct((S, tw), dt))
        out_specs.append(pl.BlockSpec((tr, w), lambda j, i: (i, j)))
    for tw, w in accs:
        out_shape.append(jax.ShapeDtypeStruct((1, tw), F32))
        out_specs.append(pl.BlockSpec((1, w), lambda j, i: (0, j)))
    nin, nout = len(args), len(outs)

    def body(*refs):
        res = fn(*[r[...] for r in refs[:nin]])
        o_refs, a_refs = refs[nin:nin + nout], refs[nin + nout:]
        for r, v in zip(o_refs, res[:nout]):
            r[...] = v.astype(r.dtype)
        i = pl.program_id(1)
        for r, v in zip(a_refs, res[nout:]):
            @pl.when(i == 0)
            def _(r=r, v=v):
                r[...] = v

            @pl.when(i > 0)
            def _(r=r, v=v):
                r[...] += v

    body, s_in, s_args, s_shape, s_out, s_scr = host_exchange(body, nin, len(out_shape), (ncb, nrb), side)
    res = pl.pallas_call(
        body,
        name=name,
        out_shape=out_shape + s_shape,
        grid=(ncb, nrb),
        in_specs=in_specs + s_in,
        out_specs=out_specs + s_out,
        scratch_shapes=s_scr,
        compiler_params=_cparams(("parallel" if side is None else "arbitrary", "arbitrary")),
    )(*args, *s_args)
    return res


CONV_TC = 128


def _conv_pre(u, w, b, row):
    pre = b + w[3:4, :] * u
    shifted = []
    for s in (1, 2, 3):
        us = jnp.where(row >= s, pltpu.roll(u, s, 0), 0.0)
        shifted.append(us)
        pre = pre + w[3 - s:4 - s, :] * us
    return pre, shifted


def conv_fwd(projmain, conv_w, conv_b):
    S = projmain.shape[0]
    tc = CONV_TC

    def body(u_ref, w_ref, b_ref, o_ref):
        u = u_ref[...]
        row = _iota(u.shape, 0)
        pre, _ = _conv_pre(u, w_ref[...], b_ref[...], row)
        o_ref[...] = pre * _sig(pre)

    return pl.pallas_call(
        body,
        name="conv_fwd",
        out_shape=jax.ShapeDtypeStruct((S, CONV_DIM), F32),
        grid=(CONV_DIM // tc,),
        in_specs=[
            pl.BlockSpec((S, tc), lambda j: (0, OFF_XBC // tc + j)),
            pl.BlockSpec((CONV_K, tc), lambda j: (0, j)),
            pl.BlockSpec((1, tc), lambda j: (0, j)),
        ],
        out_specs=pl.BlockSpec((S, tc), lambda j: (0, j)),
        compiler_params=_cparams(("parallel",)),
    )(projmain, conv_w, conv_b)


def conv_bwd(dxs, dxs_skip, d_b, d_c, projmain, conv_w, conv_b):
    S = projmain.shape[0]
    tc = CONV_TC
    n_xs, n_b = D_INNER // tc, SSD_GROUPS * D_STATE // tc

    def body(dx_ref, dskip_ref, dbm_ref, dcm_ref, u_ref, w_ref, b_ref, du_ref, dw_ref, db_ref):
        j = pl.program_id(0)
        d = jnp.where(j < n_xs, dx_ref[...] + dskip_ref[...], jnp.where(j < n_xs + n_b, dbm_ref[...], dcm_ref[...]))
        u = u_ref[...]
        w = w_ref[...]
        row = _iota(u.shape, 0)
        pre, shifted = _conv_pre(u, w, b_ref[...], row)
        sg = _sig(pre)
        dpre = d * (sg * (1.0 + pre * (1.0 - sg)))
        du = w[3:4, :] * dpre
        dw_ref[3:4, :] = jnp.sum(dpre * u, axis=0, keepdims=True)
        for s in (1, 2, 3):
            ds = jnp.where(row < S - s, pltpu.roll(dpre, S - s, 0), 0.0)
            du = du + w[3 - s:4 - s, :] * ds
            dw_ref[3 - s:4 - s, :] = jnp.sum(dpre * shifted[s - 1], axis=0, keepdims=True)
        du_ref[...] = du.astype(du_ref.dtype)
        db_ref[...] = jnp.sum(dpre, axis=0, keepdims=True)

    return pl.pallas_call(
        body,
        name="conv_bwd",
        out_shape=[
            jax.ShapeDtypeStruct((S, CONV_DIM), MXU),
            jax.ShapeDtypeStruct((CONV_K, CONV_DIM), F32),
            jax.ShapeDtypeStruct((1, CONV_DIM), F32),
        ],
        grid=(CONV_DIM // tc,),
        in_specs=[
            pl.BlockSpec((S, tc), lambda j: (0, jnp.minimum(j, n_xs - 1))),
            pl.BlockSpec((S, tc), lambda j: (0, jnp.minimum(j, n_xs - 1))),
            pl.BlockSpec((S, tc), lambda j: (0, jnp.clip(j - n_xs, 0, n_b - 1))),
            pl.BlockSpec((S, tc), lambda j: (0, jnp.clip(j - n_xs - n_b, 0, n_b - 1))),
            pl.BlockSpec((S, tc), lambda j: (0, OFF_XBC // tc + j)),
            pl.BlockSpec((CONV_K, tc), lambda j: (0, j)),
            pl.BlockSpec((1, tc), lambda j: (0, j)),
        ],
        out_specs=[
            pl.BlockSpec((S, tc), lambda j: (0, j)),
            pl.BlockSpec((CONV_K, tc), lambda j: (0, j)),
            pl.BlockSpec((1, tc), lambda j: (0, j)),
        ],
        compiler_params=_cparams(("arbitrary",)),
    )(dxs, dxs_skip, d_b, d_c, projmain, conv_w, conv_b)


def _head_expand():
    r, j = _iota((LANES, D_INNER), 0), _iota((LANES, D_INNER), 1)
    return ((j >= r * HEAD_DIM) & (j < r * HEAD_DIM + HEAD_DIM)).astype(F32).astype(MXU)


def _head_reduce():
    j, r = _iota((D_INNER, LANES), 0), _iota((D_INNER, LANES), 1)
    return ((j >= r * HEAD_DIM) & (j < r * HEAD_DIM + HEAD_DIM)).astype(F32).astype(MXU)


def ssd_prep(dtraw, dt_bias_pad, a_exp):
    S = dtraw.shape[0]

    def body(dtr_ref, bias_ref, a_ref, dte_ref, cse_ref):
        dt = _softplus(dtr_ref[...] + bias_ref[...])
        dte = _dot3_l(dt, _head_expand())
        dte_ref[...] = dte
        incl = _tri(CHUNK, lambda r, c: r >= c)
        cse_ref[...] = _dot3_r(incl, dte * a_ref[...])

    return pl.pallas_call(
        body,
        name="ssd_prep",
        out_shape=[jax.ShapeDtypeStruct((S, D_INNER), F32)] * 2,
        grid=(S // CHUNK,),
        in_specs=[
            pl.BlockSpec((CHUNK, LANES), lambda c: (c, 0)),
            pl.BlockSpec((1, LANES), lambda c: (0, 0)),
            pl.BlockSpec((1, D_INNER), lambda c: (0, 0)),
        ],
        out_specs=[pl.BlockSpec((CHUNK, D_INNER), lambda c: (c, 0))] * 2,
        compiler_params=_cparams(("parallel",)),
    )(dtraw, dt_bias_pad, a_exp)


GW = 4 * HEAD_DIM


def ssd_fwd(xbc_act, dte, cse, side=None):
    S = xbc_act.shape[0]
    nc = S // CHUNK

    def body(xs_ref, b_ref, c_ref, dte_ref, cse_ref, y_ref, st_ref, s_scr):
        c = pl.program_id(1)

        @pl.when(c == 0)
        def _():
            s_scr[...] = jnp.zeros_like(s_scr)

        s_in = s_scr[...]
        st_ref[0] = s_in
        cs = cse_ref[...]
        xd = xs_ref[...] * dte_ref[...]
        btb = b_ref[...].T.astype(MXU)
        cs_last = cs[CHUNK - 1:CHUNK, :]
        gy = _dot(c_ref[...].astype(MXU), jnp.concatenate([btb, s_in.astype(MXU)], axis=1))
        g, y = gy[:, :CHUNK], gy[:, CHUNK:] * jnp.exp(cs)
        cs_t = cs.T
        row, col = _iota((CHUNK, CHUNK), 0), _iota((CHUNK, CHUNK), 1)
        ms = []
        for h in range(4):
            lo = HEAD_DIM * h
            lam = jnp.where(row >= col, jnp.exp(cs[:, lo:lo + 1] - cs_t[lo:lo + 1, :]), 0.0)
            ms.append((g * lam).astype(MXU))
        yd = _dot(jnp.concatenate(ms, axis=0), xd.astype(MXU))
        lane = _iota((CHUNK, GW), 1)
        for h in range(4):
            y = y + jnp.where((lane >= HEAD_DIM * h) & (lane < HEAD_DIM * (h + 1)), yd[CHUNK * h:CHUNK * (h + 1)], 0.0)
        y_ref[...] = y
        w = (xd * jnp.exp(cs_last - cs)).astype(MXU)
        s_scr[...] = jnp.exp(cs_last) * s_in + _dot(btb, w)

    grid = (SSD_GROUPS, nc)
    body, s_in, s_args, s_shape, s_out, s_scr = host_exchange(body, 5, 2, grid, side)
    return pl.pallas_call(
        body,
        name="ssd_fwd",
        out_shape=[
            jax.ShapeDtypeStruct((S, D_INNER), F32),
            jax.ShapeDtypeStruct((nc, D_STATE, D_INNER), F32),
        ] + s_shape,
        grid=grid,
        in_specs=[
            pl.BlockSpec((CHUNK, GW), lambda g, c: (c, g)),
            pl.BlockSpec((CHUNK, D_STATE), lambda g, c: (c, D_INNER // D_STATE + g)),
            pl.BlockSpec((CHUNK, D_STATE), lambda g, c: (c, D_INNER // D_STATE + SSD_GROUPS + g)),
            pl.BlockSpec((CHUNK, GW), lambda g, c: (c, g)),
            pl.BlockSpec((CHUNK, GW), lambda g, c: (c, g)),
        ] + s_in,
        out_specs=[
            pl.BlockSpec((CHUNK, GW), lambda g, c: (c, g)),
            pl.BlockSpec((1, D_STATE, GW), lambda g, c: (c, 0, g)),
        ] + s_out,
        scratch_shapes=[pltpu.VMEM((D_STATE, GW), F32)] + s_scr,
        compiler_params=_cparams(("parallel" if side is None else "arbitrary", "arbitrary")),
    )(xbc_act, xbc_act, xbc_act, dte, cse, *s_args)


def ssd_bwd(dy, xbc_act, dte, cse, states, a_exp, side=None):
    S = xbc_act.shape[0]
    nc = S // CHUNK

    def body(dy_ref, xs_ref, b_ref, c_ref, dte_ref, cse_ref, sin_ref, sout_ref, a_ref,
             dxs_ref, db_ref, dc_ref, ddt_ref, dal_ref, ds_scr):
        j = pl.program_id(1)

        @pl.when(j == 0)
        def _():
            ds_scr[...] = jnp.zeros_like(ds_scr)
            dal_ref[...] = jnp.zeros_like(dal_ref)

        ds_out = ds_scr[...]
        dyv, xs = dy_ref[...], xs_ref[...]
        dt, cs = dte_ref[...], cse_ref[...]
        s_in = sin_ref[0]
        bm, cm = b_ref[...], c_ref[...]
        bb, cb = bm.astype(MXU), cm.astype(MXU)
        btb, ctb = bm.T.astype(MXU), cm.T.astype(MXU)
        dsb, sib = ds_out.astype(MXU), s_in.astype(MXU)
        xd = xs * dt
        ecs = jnp.exp(cs)
        cs_last = cs[CHUNK - 1:CHUNK, :]
        eend = jnp.exp(cs_last - cs)
        gy = _dot(cb, jnp.concatenate([btb, sib], axis=1))
        g, yoff = gy[:, :CHUNK], gy[:, CHUNK:] * ecs
        gd = _dot(bb, jnp.concatenate([ctb, dsb], axis=1))
        g_t, dxd_off = gd[:, :CHUNK], gd[:, CHUNK:] * eend
        cs_t = cs.T
        row, col = _iota((CHUNK, CHUNK), 0), _iota((CHUNK, CHUNK), 1)
        lane = _iota((CHUNK, GW), 1)
        heads = [(lane >= HEAD_DIM * h) & (lane < HEAD_DIM * (h + 1)) for h in range(4)]
        dyb, xdb = dyv.astype(MXU), xd.astype(MXU)
        dm_all = _dot_nt(jnp.concatenate([jnp.where(hm, dyv, 0.0) for hm in heads], axis=0).astype(MXU), xdb)
        dmt_all = _dot_nt(jnp.concatenate([jnp.where(hm, xd, 0.0) for hm in heads], axis=0).astype(MXU), dyb)
        lams, m_ts = [], []
        for h in range(4):
            lo = HEAD_DIM * h
            cs_col, cs_row = cs[:, lo:lo + 1], cs_t[lo:lo + 1, :]
            lams.append(jnp.where(row >= col, jnp.exp(cs_col - cs_row), 0.0))
            m_ts.append(g_t * jnp.where(col >= row, jnp.exp(cs_row - cs_col), 0.0))
        acc_all = _dot(jnp.concatenate(m_ts, axis=0).astype(MXU), dyb)
        dxd = dxd_off
        dg = jnp.zeros((CHUNK, CHUNK), F32)
        dcs = dyv * yoff - xd * dxd_off
        for h in range(4):
            blk = slice(CHUNK * h, CHUNK * (h + 1))
            dm, dm_t = dm_all[blk], dmt_all[blk]
            dxd = dxd + jnp.where(heads[h], acc_all[blk], 0.0)
            dg = dg + dm * lams[h]
            wdiff = (jnp.sum(dm * (g * lams[h]), axis=1, keepdims=True)
                     - jnp.sum(dm_t * m_ts[h], axis=1, keepdims=True))
            dcs = dcs + jnp.where(lane == HEAD_DIM * h, wdiff, 0.0)
        dye = (dyv * ecs).astype(MXU)
        dc_ref[...] = _dot(dg.astype(MXU), bb) + _dot_nt(dye, sib)
        db_ref[...] = _dot(dg.T.astype(MXU), cb) + _dot_nt((xd * eend).astype(MXU), dsb)
        ds_scr[...] = jnp.exp(cs_last) * ds_out + _dot(ctb, dye)
        last = jnp.sum(ds_out * sout_ref[0], axis=0, keepdims=True)
        rows = _iota((CHUNK, GW), 0)
        dcs = dcs + jnp.where(rows == CHUNK - 1, last, 0.0)
        dda = _dot3_r(_tri(CHUNK, lambda r, c: c >= r), dcs)
        ddt_ref[...] = a_ref[...] * dda + dxd * xs
        dal_ref[...] += jnp.sum(dt * dda, axis=0, keepdims=True)
        dxs_ref[...] = dxd * dt

    rc = lambda g, j: (nc - 1 - j, g)
    grid = (SSD_GROUPS, nc)
    body, s_in, s_args, s_shape, s_out, s_scr = host_exchange(body, 9, 5, grid, side)
    return pl.pallas_call(
        body,
        name="ssd_bwd",
        out_shape=[
            jax.ShapeDtypeStruct((S, D_INNER), F32),
            jax.ShapeDtypeStruct((S, SSD_GROUPS * D_STATE), F32),
            jax.ShapeDtypeStruct((S, SSD_GROUPS * D_STATE), F32),
            jax.ShapeDtypeStruct((S, D_INNER), F32),
            jax.ShapeDtypeStruct((1, D_INNER), F32),
        ] + s_shape,
        grid=grid,
        in_specs=[
            pl.BlockSpec((CHUNK, GW), rc),
            pl.BlockSpec((CHUNK, GW), rc),
            pl.BlockSpec((CHUNK, D_STATE), lambda g, j: (nc - 1 - j, D_INNER // D_STATE + g)),
            pl.BlockSpec((CHUNK, D_STATE), lambda g, j: (nc - 1 - j, D_INNER // D_STATE + SSD_GROUPS + g)),
            pl.BlockSpec((CHUNK, GW), rc),
            pl.BlockSpec((CHUNK, GW), rc),
            pl.BlockSpec((1, D_STATE, GW), lambda g, j: (nc - 1 - j, 0, g)),
            pl.BlockSpec((1, D_STATE, GW), lambda g, j: (jnp.minimum(nc - j, nc - 1), 0, g)),
            pl.BlockSpec((1, GW), lambda g, j: (0, g)),
        ] + s_in,
        out_specs=[
            pl.BlockSpec((CHUNK, GW), rc),
            pl.BlockSpec((CHUNK, D_STATE), rc),
            pl.BlockSpec((CHUNK, D_STATE), rc),
            pl.BlockSpec((CHUNK, GW), rc),
            pl.BlockSpec((1, GW), lambda g, j: (0, g)),
        ] + s_out,
        scratch_shapes=[pltpu.VMEM((D_STATE, GW), F32)] + s_scr,
        compiler_params=_cparams(("parallel" if side is None else "arbitrary", "arbitrary")),
    )(dy, xbc_act, xbc_act, xbc_act, dte, cse, states, states, a_exp, *s_args)


SB_T = 256
SB_DROP = 104.0


def _sb_scores(qm, k_ref, ks, rowi, coli, diag):
    kblk = k_ref[pl.ds(ks, SB_T), :].astype(MXU)
    z = _dot_nt(qm, kblk) * SB_SCALE
    sp = _softplus(z)
    if not diag:
        return kblk, z, None, sp, sp
    mask = (ks + coli) < rowi
    return kblk, z, mask, sp, jnp.where(mask, sp, 0.0)


def _sb_stack(v):
    lane = _iota(v.shape, 1)
    return jnp.concatenate([jnp.where(lane < HEAD_DIM, v, 0.0), jnp.where(lane >= HEAD_DIM, v, 0.0)], axis=0)


def _sb_unstack(v):
    lane = _iota((SB_T, LANES), 1)
    return jnp.where(lane < HEAD_DIM, v[:SB_T], v[SB_T:])


def _sb_rows(qb):
    r = _iota((2 * SB_T, SB_T), 0)
    return qb * SB_T + jnp.where(r >= SB_T, r - SB_T, r), _iota((2 * SB_T, SB_T), 1)


def sb_fwd(projmain):
    S = projmain.shape[0]
    nq = S // SB_T

    def body(q_ref, k_ref, v_ref, o_ref, t_ref, n_ref):
        hp, qb = pl.program_id(0), pl.program_id(1)
        qst = _sb_stack(q_ref[...]).astype(MXU)
        rowi, coli = _sb_rows(qb)
        u_after = _tri(SB_T, lambda r, c: r > c)

        def cond(carry):
            i, rmin, _, _ = carry
            return (i <= qb) & (rmin < SB_DROP)

        def kstep(carry, diag=False):
            i, _, r, acc = carry
            ks = pl.multiple_of((qb - i) * SB_T, SB_T)
            _, z, mask, sp, spm = _sb_scores(qst, k_ref, ks, rowi, coli, diag)
            vblk = v_ref[pl.ds(ks, SB_T), :].astype(MXU)
            a = jnp.exp(z - sp - _dot3_l(spm, u_after) - r)
            if diag:
                a = jnp.where(mask, a, 0.0)
            acc = acc + _dot(a.astype(MXU), vblk)
            r = r + jnp.sum(spm, axis=1, keepdims=True)
            return i + 1, jnp.min(r), r, acc

        first = kstep((jnp.int32(0), jnp.float32(0.0), jnp.zeros((2 * SB_T, 1), F32),
                       jnp.zeros((2 * SB_T, LANES), F32)), diag=True)
        n, _, r, acc = lax.while_loop(cond, kstep, first)
        o_ref[...] = _sb_unstack(acc).astype(o_ref.dtype)
        t_ref[...] = _sb_unstack(jnp.broadcast_to(r, (2 * SB_T, LANES)))
        n_ref[hp, qb] = n

    return pl.pallas_call(
        body,
        name="sb_fwd",
        out_shape=[jax.ShapeDtypeStruct((S, SB_WIDTH), MXU), jax.ShapeDtypeStruct((S, SB_WIDTH), F32),
                   jax.ShapeDtypeStruct((SB_WIDTH // LANES, nq), jnp.int32)],
        grid=(SB_WIDTH // LANES, nq),
        in_specs=[
            pl.BlockSpec((SB_T, LANES), lambda h, i: (i, OFF_Q // LANES + h)),
            pl.BlockSpec((S, LANES), lambda h, i: (0, OFF_K // LANES + h)),
            pl.BlockSpec((S, LANES), lambda h, i: (0, OFF_V // LANES + h)),
        ],
        out_specs=[pl.BlockSpec((SB_T, LANES), lambda h, i: (i, h))] * 2
        + [pl.BlockSpec(memory_space=pltpu.SMEM)],
        compiler_params=_cparams(("arbitrary", "arbitrary")),
    )(projmain, projmain, projmain)


def sb_bwd(projmain, do, t_exp, nblk):
    S = projmain.shape[0]
    nq = S // SB_T

    def body(n_ref, q_ref, k_ref, v_ref, do_ref, t_ref, dq_ref, dk_ref, dv_ref):
        hp, qb = pl.program_id(0), pl.program_id(1)

        @pl.when(qb == 0)
        def _():
            dk_ref[...] = jnp.zeros_like(dk_ref)
            dv_ref[...] = jnp.zeros_like(dv_ref)

        qst = _sb_stack(q_ref[...]).astype(MXU)
        dost = _sb_stack(do_ref[...]).astype(MXU)
        tv = t_ref[...]
        lane = _iota((SB_T, LANES), 1)
        tot = jnp.concatenate(
            [jnp.sum(jnp.where(lane == HEAD_DIM * hh, tv, 0.0), axis=1, keepdims=True) for hh in range(2)], axis=0)
        rowi, coli = _sb_rows(qb)
        u_upto = _tri(SB_T, lambda r, c: r <= c)
        u_before = _tri(SB_T, lambda r, c: r < c)
        kept = jnp.clip(n_ref[hp, qb], 1, qb + 1)

        def kstep(kb, carry, diag=False):
            psp, pg, dq = carry
            ks = pl.multiple_of(kb * SB_T, SB_T)
            kblk, z, mask, sp, spm = _sb_scores(qst, k_ref, ks, rowi, coli, diag)
            vblk = v_ref[pl.ds(ks, SB_T), :].astype(MXU)
            after = tot - (psp + _dot3_l(spm, u_upto))
            a = jnp.exp(z - sp - after)
            if diag:
                a = jnp.where(mask, a, 0.0)
            gm = _dot_nt(dost, vblk) * a
            before = pg + _dot3_l(gm, u_before)
            sg = jnp.exp(z - sp)
            dz = (gm * (1.0 - sg) - sg * before) * SB_SCALE
            if diag:
                dz = jnp.where(mask, dz, 0.0)
            dzb = dz.astype(MXU)
            dq = dq + _dot(dzb, kblk)
            dk_ref[pl.ds(ks, SB_T), :] += _dot_tn(dzb, qst)
            dv_ref[pl.ds(ks, SB_T), :] += _dot_tn(a.astype(MXU), dost)
            return (psp + jnp.sum(spm, axis=1, keepdims=True),
                    pg + jnp.sum(gm, axis=1, keepdims=True), dq)

        zero1 = jnp.zeros((2 * SB_T, 1), F32)
        state = lax.fori_loop(qb + 1 - kept, qb, kstep, (zero1, zero1, jnp.zeros((2 * SB_T, LANES), F32)))
        dq = kstep(qb, state, diag=True)[2]
        dq_ref[...] = _sb_unstack(dq).astype(dq_ref.dtype)

    return pl.pallas_call(
        body,
        name="sb_bwd",
        out_shape=[
            jax.ShapeDtypeStruct((S, SB_WIDTH), MXU),
            jax.ShapeDtypeStruct((S, SB_WIDTH), F32),
            jax.ShapeDtypeStruct((S, SB_WIDTH), F32),
        ],
        grid=(SB_WIDTH // LANES, nq),
        in_specs=[
            pl.BlockSpec(memory_space=pltpu.SMEM),
            pl.BlockSpec((SB_T, LANES), lambda h, i: (i, OFF_Q // LANES + h)),
            pl.BlockSpec((S, LANES), lambda h, i: (0, OFF_K // LANES + h)),
            pl.BlockSpec((S, LANES), lambda h, i: (0, OFF_V // LANES + h)),
            pl.BlockSpec((SB_T, LANES), lambda h, i: (i, h)),
            pl.BlockSpec((SB_T, LANES), lambda h, i: (i, h)),
        ],
        out_specs=[
            pl.BlockSpec((SB_T, LANES), lambda h, i: (i, h)),
            pl.BlockSpec((S, LANES), lambda h, i: (0, h)),
            pl.BlockSpec((S, LANES), lambda h, i: (0, h)),
        ],
        compiler_params=_cparams(("arbitrary", "arbitrary")),
    )(nblk, projmain, projmain, projmain, do, t_exp)


def local_step(x, pb, target, W, P, late, core):
    D = D_MODEL
    full = lambda a, w=D: (a, 0, w)

    (n1b,) = rowwise("norm_pre", lambda xv, g: (_rms(xv, g),), [full(x)], [full(P["norm_mix_pre"])],
                     [(D, D, MXU)])
    projmain = mm("in_proj", n1b, W["main"], "nn")
    gate_pre = mm("gate_proj", n1b, W["gate"], "nn", b_slabs=True)
    dtraw = mm("dt_proj", n1b, W["dt"], "nn")
    xbc_act = conv_fwd(projmain, P["conv_w"], P["conv_b"])
    dte, cse = ssd_prep(dtraw, P["dt_bias_pad"], P["a_exp"])
    y_scan, states, *got = ssd_fwd(xbc_act, dte, cse, side=exchange_plan([("gather2", late[n]) for n in LATE]))
    W = {**W, **late_matrices(dict(zip(LATE, got)))}

    def f_gate(ysc, xs, z, dsk, nw):
        return _rms((ysc + xs * dsk) * (z * _sig(z)), nw)

    (y_ssd_b,) = rowwise("ssd_gate", lambda *a: (f_gate(*a),),
                         [(y_scan, 0, GW), (xbc_act, 0, GW), (projmain, OFF_Z, GW)],
                         [(P["dsk_exp"], 0, GW), (P["ssd_norm"], 0, GW)], [(D_INNER, GW, MXU)], ncb=SSD_GROUPS)
    y_sb_b, t_exp, sb_kept = sb_fwd(projmain)
    u1 = mm("ssd_branch", y_ssd_b, W["ssd"], "nn")
    u2 = mm("sb_branch", y_sb_b, W["sb"], "nn")

    def f_merge(a1, a2, g1, g2, b1, b2):
        return (_sig(g1 + b1) * a1 + _sig(g2 + b2) * a2,)

    gate_rows = [(gate_pre, 0, D), (gate_pre, D, D)]
    gate_bias = [(P["b_gate"], 0, D), (P["b_gate"], D, D)]
    (merged_b,) = rowwise("merge", f_merge, [full(u1), full(u2)] + gate_rows, gate_bias, [(D, D, MXU)])
    mo = mm("out_proj", merged_b, W["out"], "nn")

    def f_mid(xv, m, gpost, gffn):
        h1 = xv + _rms(m, gpost)
        return h1, _rms(h1, gffn)

    h1, n2b = rowwise("mix_post", f_mid, [full(x), full(mo)],
                      [full(P["norm_mix_post"]), full(P["norm_ffn_pre"])], [(D, D, F32), (D, D, MXU)])
    a_ff, rb = mm("ff1", n2b, W["ff1"], "nn", b_slabs=True,
                  extra=[(MXU, lambda acc: jnp.square(jnp.maximum(acc, 0.0)))])
    ff = mm("ff2", rb, W["ff2"], "nn")

    def f_ffn_post(h, f, g):
        h2 = h + _rms(f, g)
        return h2, h2

    h2, h2b = rowwise("ffn_post", f_ffn_post, [full(h1), full(ff)], [full(P["norm_ffn_post"])],
                      [(D, D, F32), (D, D, MXU)])
    pgp = mm("ple_gate", h2b, W["pg"], "nn")
    pe = mm("ple_proj", pb, W["ple"], "nn", b_slabs=True)

    def f_ple(h2v, gp, pev, tgt, g):
        f = lambda h, a, b, gg: h + _rms(_sig(a) * b, gg)
        h3, vjp = jax.vjp(f, h2v, gp, pev, g)
        err = h3 - tgt
        dh2, dgp, dpe, dg = vjp(err * (1.0 / D))
        lossc = (0.5 / D) * jnp.sum(err * err, axis=0, keepdims=True)
        return dh2, dgp, dpe, dg, lossc

    dh2a, dpgp_b, dpe_b, g_ple, lossc = rowwise(
        "ple_loss", f_ple, [full(h2), full(pgp), full(pe), full(target)], [full(P["norm_ple_post"])],
        [(D, D, F32), (D, D, MXU), (D, D, MXU)], [(D, D), (D, D)])
    dh2 = mm("d_ple_gate_x", dpgp_b, W["pg"], "nt", add=dh2a)
    gW = {}
    gW["w_ple_gate"] = mm("d_ple_gate_w", h2b, dpgp_b, "tn")
    gW["w_ple"] = mm("d_ple_w", pb, dpe_b, "tn", out_slabs=True)

    def b_ffn_post(d, f, g):
        _, vjp = jax.vjp(_rms, f, g)
        return vjp(d)

    dff_b, g_ffn_post = rowwise("d_ffn_post", b_ffn_post, [full(dh2), full(ff)], [full(P["norm_ffn_post"])],
                                [(D, D, MXU)], [(D, D)])
    da_b = mm("d_ff2_x", dff_b, W["ff2"], "nt", out_dtype=MXU,
              epi=lambda acc, act: acc * (2.0 * jnp.maximum(act, 0.0)), epi_args=[a_ff])
    gW["w_ff2"] = mm("d_ff2_w", rb, dff_b, "tn")
    dn2 = mm("d_ff1_x", da_b, W["ff1"], "nt", b_slabs=True)
    gW["w_ff1"] = mm("d_ff1_w", n2b, da_b, "tn", out_slabs=True)

    def b_mid(d2, dn, h, m, gpost, gffn):
        _, vjp = jax.vjp(_rms, h, gffn)
        dh, dgffn = vjp(dn)
        dh1 = d2 + dh
        _, vjp2 = jax.vjp(_rms, m, gpost)
        dm, dgpost = vjp2(dh1)
        return dh1, dm, dgpost, dgffn

    dh1, dmo_b, g_mix_post, g_ffn_pre = rowwise(
        "d_mix_post", b_mid, [full(dh2), full(dn2), full(h1), full(mo)],
        [full(P["norm_mix_post"]), full(P["norm_ffn_pre"])], [(D, D, F32), (D, D, MXU)], [(D, D), (D, D)])
    dmerged = mm("d_out_x", dmo_b, W["out"], "nt")
    gW["w_out"] = mm("d_out_w", merged_b, dmo_b, "tn")

    def b_merge(d, a1, a2, g1, g2, b1, b2):
        s1, s2 = _sig(g1 + b1), _sig(g2 + b2)
        dg1 = d * a1 * s1 * (1.0 - s1)
        dg2 = d * a2 * s2 * (1.0 - s2)
        dg = jnp.concatenate([dg1, dg2], axis=1)
        return d * s1, d * s2, dg, jnp.sum(dg, axis=0, keepdims=True)

    du1_b, du2_b, dgp_b, g_b_gate = rowwise(
        "d_merge", b_merge, [full(dmerged), full(u1), full(u2)] + gate_rows, gate_bias,
        [(D, D, MXU), (D, D, MXU), (2 * D, 2 * D, MXU)], [(2 * D, 2 * D)])
    dy_ssd = mm("d_ssd_branch_x", du1_b, W["ssd"], "nt")
    dy_sb = mm("d_sb_branch_x", du2_b, W["sb"], "nt")
    gW["w_ssd_branch"] = mm("d_ssd_branch_w", y_ssd_b, du1_b, "tn")
    gW["w_sb_branch"] = mm("d_sb_branch_w", y_sb_b, du2_b, "tn")

    def b_gate(d, ysc, xs, z, dsk, nw):
        _, vjp = jax.vjp(f_gate, ysc, xs, z, dsk, nw)
        return vjp(d)

    slabs = {n: gW[n] if n in COL_SHARDED else gW[n].reshape((N_DEV,) + ROW_SHARDED[n]) for n in LATE}
    dy_scan, dxs_skip, dz_b, g_dsk_exp, g_ssd_norm, *from_sibling = rowwise(
        "d_ssd_gate", b_gate, [(dy_ssd, 0, GW), (y_scan, 0, GW), (xbc_act, 0, GW), (projmain, OFF_Z, GW)],
        [(P["dsk_exp"], 0, GW), (P["ssd_norm"], 0, GW)],
        [(D_INNER, GW, F32), (D_INNER, GW, F32), (D_INNER, GW, MXU)], [(D_INNER, GW), (D_INNER, GW)],
        ncb=SSD_GROUPS, side=exchange_plan([("pair", slabs[n]) for n in LATE]))
    chip_sums = [pair_sum("pair_sum_" + n, slabs[n], s, core) for n, s in zip(LATE, from_sibling)]
    dxs, d_b, d_c, ddt_part, g_a_exp, *late_parts = ssd_bwd(
        dy_scan, xbc_act, dte, cse, states, P["a_exp"], side=exchange_plan([("chips", s) for s in chip_sums]))

    def b_dt(dpart, dtr, bias):
        ddt = _dot3_l(dpart, _head_reduce())
        d = ddt * _sig(dtr + bias)
        return d, jnp.sum(d, axis=0, keepdims=True)

    ddt_b, g_dt_bias_pad = rowwise("d_dt", b_dt, [(ddt_part, 0, D_INNER), (dtraw, 0, LANES)],
                                   [(P["dt_bias_pad"], 0, LANES)], [(LANES, LANES, MXU)], [(LANES, LANES)])
    dxbc_b, g_conv_w, g_conv_b = conv_bwd(dxs, dxs_skip, d_b, d_c, projmain, P["conv_w"], P["conv_b"])
    dq_b, dk, dv = sb_bwd(projmain, dy_sb, t_exp, sb_kept)
    dmain_b = jnp.concatenate([dz_b, dxbc_b, dq_b, dk.astype(MXU), dv.astype(MXU)], axis=1)
    g_main = mm("d_in_w", n1b, dmain_b, "tn")
    g_dt = mm("d_dt_w", n1b, ddt_b, "tn")
    last = {"w_gate": mm("d_gate_w", n1b, dgp_b, "tn", out_slabs=True)}
    last["w_in"], conv_slabs = in_grad_slabs({"main": g_main, "dt": g_dt, "conv_w": g_conv_w})
    dn1_dt = mm("d_dt_x", ddt_b, W["dt"], "nt")
    dn1_gate, *from_sibling = mm("d_gate_x", dgp_b, W["gate"], "nt", add=dn1_dt, b_slabs=True,
                                 side=exchange_plan([("pair", last[n]) for n in FIRST]))
    chip_sums = [pair_sum("pair_sum_" + n, last[n], s, core) for n, s in zip(FIRST, from_sibling)]
    dn1, *first_parts = mm("d_in_x", dmain_b, W["main"], "nt", add=dn1_gate,
                           side=exchange_plan([("chips", s) for s in chip_sums]))

    def b_pre(d1, dn, xv, g):
        _, vjp = jax.vjp(_rms, xv, g)
        dx, dg = vjp(dn)
        return d1 + dx, dg

    grad_x, g_mix_pre = rowwise("d_norm_pre", b_pre, [full(dh1), full(dn1), full(x)], [full(P["norm_mix_pre"])],
                                [(D, D, F32)], [(D, D)])

    parts = {**dict(zip(LATE, late_parts)), **dict(zip(FIRST, first_parts))}
    gS = {
        "norm_mix_pre": g_mix_pre, "conv_b": g_conv_b, "dt_bias_pad": g_dt_bias_pad, "a_exp": g_a_exp,
        "dsk_exp": g_dsk_exp, "ssd_norm": g_ssd_norm, "b_gate": g_b_gate,
        "norm_mix_post": g_mix_post, "norm_ffn_pre": g_ffn_pre, "norm_ffn_post": g_ffn_post,
        "norm_ple_post": g_ple,
    }
    return lossc, grad_x, parts, conv_slabs, gS


IN_SPLITS = (2048, 6144, 6176, 7200, 8224)
IN_SHARD = 1156
IN_PAD = 1280
COL_SHARDED = {"w_in": (1024, IN_SHARD), "conv_w": (4, 512), "w_gate": (1024, 256), "w_ff1": (1024, 512),
               "w_ple": (256, 128)}
ROW_SHARDED = {"w_ssd_branch": (256, 1024), "w_sb_branch": (128, 1024), "w_out": (128, 1024),
               "w_ff2": (512, 1024), "w_ple_gate": (128, 1024)}
SHARDED = tuple(COL_SHARDED) + tuple(ROW_SHARDED)
SMALL = (
    ("norm_mix_pre", 1024), ("conv_b", 4096), ("dt_bias", 32), ("a_log", 32), ("d_skip", 32), ("ssd_norm", 2048),
    ("b_gate", 2048), ("norm_mix_post", 1024), ("norm_ffn_pre", 1024), ("norm_ffn_post", 1024),
    ("norm_ple_post", 1024),
)
ROW = 1024
SMALL_ROWS = 16


def _rows_of(n):
    return -(-n // ROW)


def _pad_to(v, n, axis=-1):
    pad = [(0, 0)] * v.ndim
    pad[axis] = (0, n - v.shape[axis])
    return jnp.pad(v, pad)


FIRST = ("w_in", "w_gate")
LATE = ("w_ssd_branch", "w_sb_branch", "w_out", "w_ff1", "w_ff2", "w_ple", "w_ple_gate")


def in_matrices(w_in_slabs, w_gate_slabs):
    w_in = jnp.concatenate([w_in_slabs[k, :, :IN_SHARD] for k in range(N_DEV)], axis=1)
    return {
        "main": jnp.concatenate([w_in[:, :IN_SPLITS[1]], w_in[:, IN_SPLITS[2]:]], axis=1),
        "dt": _pad_to(w_in[:, IN_SPLITS[1]:IN_SPLITS[2]], LANES),
        "gate": w_gate_slabs,
    }


def late_matrices(g):
    stack = lambda a: a.reshape(a.shape[0] * a.shape[1], a.shape[2])
    return {
        "ff1": g["w_ff1"], "ple": g["w_ple"],
        "ssd": stack(g["w_ssd_branch"]), "sb": stack(g["w_sb_branch"]), "out": stack(g["w_out"]),
        "ff2": stack(g["w_ff2"]), "pg": stack(g["w_ple_gate"]),
    }


def build_small(small, conv_w_slabs):
    P = {k: small[k] for k in ("norm_mix_pre", "conv_b", "ssd_norm", "b_gate", "norm_mix_post", "norm_ffn_pre",
                               "norm_ffn_post", "norm_ple_post")}
    P["conv_w"] = jnp.concatenate([conv_w_slabs[k] for k in range(N_DEV)], axis=1)
    P["dt_bias_pad"] = _pad_to(small["dt_bias"], LANES)
    P["a_exp"] = jnp.repeat(-jnp.exp(small["a_log"]), HEAD_DIM, axis=1)
    P["dsk_exp"] = jnp.repeat(small["d_skip"], HEAD_DIM, axis=1)
    return P


def small_grads(gS, small):
    heads = lambda a: a.reshape(SSD_HEADS, HEAD_DIM).sum(axis=1)[None, :]
    out = {k: gS[k] for k in ("norm_mix_pre", "conv_b", "ssd_norm", "b_gate", "norm_mix_post", "norm_ffn_pre",
                              "norm_ffn_post", "norm_ple_post")}
    out["dt_bias"] = gS["dt_bias_pad"][:, :SSD_HEADS]
    out["a_log"] = heads(gS["a_exp"]) * (-jnp.exp(small["a_log"]))
    out["d_skip"] = heads(gS["dsk_exp"])
    return out


def in_grad_slabs(g):
    pieces = ((0, IN_SPLITS[1], g["main"], 0), (IN_SPLITS[1], IN_SPLITS[2], g["dt"], -IN_SPLITS[1]),
              (IN_SPLITS[2], N_DEV * IN_SHARD, g["main"], IN_SPLITS[1] - IN_SPLITS[2]))
    slabs = []
    for k in range(N_DEV):
        a, b = IN_SHARD * k, IN_SHARD * (k + 1)
        cut = [src[:, max(a, lo) + off:min(b, hi) + off] for lo, hi, src, off in pieces if max(a, lo) < min(b, hi)]
        slabs.append(_pad_to(jnp.concatenate(cut, axis=1), IN_PAD))
    width = COL_SHARDED["conv_w"][1]
    return jnp.stack(slabs), jnp.stack([g["conv_w"][:, width * k:width * (k + 1)] for k in range(N_DEV)])


def pack_small(get):
    cols = [_pad_to(get(name).reshape(n), _rows_of(n) * ROW) for name, n in SMALL]
    return jnp.concatenate(cols).reshape(SMALL_ROWS, ROW)


def unpack_small(flat):
    out, r0 = {}, 0
    for name, n in SMALL:
        rows = _rows_of(n)
        out[name] = flat[r0:r0 + rows].reshape(rows * ROW)[:n].reshape(1, n)
        r0 += rows
    return out


N_CHIPS = 4
JOB_SEMS = {"all": 7, "scatter": 7, "gather2": 7, "pair": 4, "chips": 3}


def exchange_plan(jobs):
    n = len(jobs)
    kinds = [k for k, _ in jobs]
    srcs = [s for _, s in jobs]
    shapes = {"all": lambda s: (N_DEV,) + s.shape, "gather2": lambda s: (N_DEV,) + s.shape,
              "scatter": lambda s: s.shape, "pair": lambda s: (N_CHIPS,) + s.shape[1:], "chips": lambda s: s.shape}
    out_shape = [jax.ShapeDtypeStruct(shapes[k](s), s.dtype) for k, s in jobs]
    offs = [sum(JOB_SEMS[k] for k in kinds[:i]) for i in range(n + 1)]

    def run(phases, src, out, send_sems, recv_sems, local_sems):
        x, y, c = lax.axis_index("x"), lax.axis_index("y"), lax.axis_index("c")
        dev = lambda d: 4 * d[0] + 2 * d[1] + d[2]
        chip_no = lambda ch: 2 * ch[0] + ch[1]
        me, sib, my_chip = (x, y, c), (x, y, 1 - c), (x, y)
        others = [(1 - x, y), (x, 1 - y), (1 - x, 1 - y)]
        all_chips = [(0, 0), (0, 1), (1, 0), (1, 1)]
        peers = [(1 - x if k & 4 else x, 1 - y if k & 2 else y, 1 - c if k & 1 else c) for k in range(1, N_DEV)]
        starts, recvs, local = [], [], []
        chained = [[] for _ in others]

        for i, kind in enumerate(kinds):
            s_ref, o_ref = src[i], out[i]

            def rc(k, src_ref, dst_ref, to, i=i):
                s = offs[i] + k
                return pltpu.make_async_remote_copy(src_ref=src_ref, dst_ref=dst_ref, send_sem=send_sems.at[s],
                                                    recv_sem=recv_sems.at[s], device_id=to,
                                                    device_id_type=pl.DeviceIdType.MESH)

            if kind == "all":
                local.append(pltpu.make_async_copy(s_ref, o_ref.at[dev(me)], local_sems.at[i]))
                for k, peer in enumerate(peers):
                    starts.append(rc(k, s_ref, o_ref.at[dev(me)], peer))
                    recvs.append(rc(k, s_ref, o_ref.at[dev(peer)], peer))
            elif kind == "scatter":
                local.append(pltpu.make_async_copy(s_ref.at[dev(me)], o_ref.at[dev(me)], local_sems.at[i]))
                for k, peer in enumerate(peers):
                    starts.append(rc(k, s_ref.at[dev(peer)], o_ref.at[dev(me)], peer))
                    recvs.append(rc(k, s_ref.at[dev(me)], o_ref.at[dev(peer)], peer))
            elif kind == "gather2":
                local.append(pltpu.make_async_copy(s_ref, o_ref.at[dev(me)], local_sems.at[i]))
                starts.append(rc(0, s_ref, o_ref.at[dev(me)], sib))
                recvs.append(rc(0, s_ref, o_ref.at[dev(sib)], sib))
                for j, ch in enumerate(others):
                    same, other = (*ch, c), (*ch, 1 - c)
                    starts.append(rc(1 + j, s_ref, o_ref.at[dev(me)], same))
                    chained[j].append((rc(1 + j, s_ref, o_ref.at[dev(same)], same),
                                       rc(4 + j, o_ref.at[dev(same)], o_ref.at[dev(same)], sib)))
                    recvs.append(rc(4 + j, s_ref, o_ref.at[dev(other)], sib))
            elif kind == "pair":
                for j, ch in enumerate(all_chips):
                    starts.append(rc(j, s_ref.at[dev((*ch, 1 - c))], o_ref.at[j], sib))
                    recvs.append(rc(j, s_ref.at[dev((*ch, c))], o_ref.at[j], sib))
            else:
                mine = chip_no(my_chip)
                local.append(pltpu.make_async_copy(s_ref.at[mine], o_ref.at[mine], local_sems.at[i]))
                for j, ch in enumerate(others):
                    starts.append(rc(j, s_ref.at[chip_no(ch)], o_ref.at[mine], (*ch, c)))
                    recvs.append(rc(j, s_ref.at[mine], o_ref.at[chip_no(ch)], (*ch, c)))

        if "start" in phases:
            for cp in local + starts:
                cp.start()
        if "pass" in phases:
            for group in chained:
                for arrival, forward in group:
                    arrival.wait_recv()
                    forward.start()
        if "finish" in phases:
            for cp in recvs:
                cp.wait_recv()
            for cp in starts + [forward for group in chained for _, forward in group]:
                cp.wait_send()
            for cp in local:
                cp.wait()

    scratch = [pltpu.SemaphoreType.DMA((offs[n],)), pltpu.SemaphoreType.DMA((offs[n],)),
               pltpu.SemaphoreType.DMA((n,))]
    return srcs, out_shape, scratch, run


EXCHANGE_PHASES = ("start", "pass", "finish")


def exchange(name, jobs):
    srcs, out_shape, scratch, run = exchange_plan(jobs)
    n = len(srcs)

    def body(*refs):
        run(EXCHANGE_PHASES, refs[:n], refs[n:2 * n], *refs[2 * n:])

    any_spec = pl.BlockSpec(memory_space=pl.ANY)
    return pl.pallas_call(body, name=name, out_shape=out_shape, in_specs=[any_spec] * n, out_specs=[any_spec] * n,
                          scratch_shapes=scratch)(*srcs)


def host_exchange(body, n_in, n_out, grid, plan):
    if plan is None:
        return body, [], [], [], [], []
    srcs, out_shape, scratch, run = plan
    n = len(srcs)
    steps = 1
    for g in grid:
        steps *= g

    def hosted(*refs):
        ins, side_in = refs[:n_in], refs[n_in:n_in + n]
        outs, side_out = refs[n_in + n:n_in + n + n_out], refs[n_in + n + n_out:n_in + 2 * n + n_out]
        rest = refs[n_in + 2 * n + n_out:]
        own, sems = rest[:len(rest) - 3], rest[len(rest) - 3:]
        step = 0
        for d, g in enumerate(grid):
            step = step * g + pl.program_id(d)

        def at(when, phase):
            @pl.when(step == when)
            def _():
                run((phase,), side_in, side_out, *sems)

        at(0, "start")
        body(*ins, *outs, *own)
        at(steps * 13 // 16, "pass")
        at(steps - 1, "finish")

    any_spec = pl.BlockSpec(memory_space=pl.ANY)
    return hosted, [any_spec] * n, list(srcs), list(out_shape), [any_spec] * n, list(scratch)


def pair_sum(name, g, sib, core):
    _, r, c = g.shape
    tr = _pick(r, (256, 128, 64, 32, 16, 8))

    def body(core_ref, g_ref, s_ref, o_ref):
        o_ref[...] = (g_ref[...] + s_ref[...]).astype(o_ref.dtype)

    return pl.pallas_call(
        body,
        name=name,
        out_shape=jax.ShapeDtypeStruct((N_CHIPS, r, c), MXU),
        grid_spec=pltpu.PrefetchScalarGridSpec(
            num_scalar_prefetch=1,
            grid=(N_CHIPS, r // tr),
            in_specs=[
                pl.BlockSpec((None, None, tr, c), lambda j, i, core_ref: (j, core_ref[0], i, 0)),
                pl.BlockSpec((None, tr, c), lambda j, i, core_ref: (j, i, 0)),
            ],
            out_specs=pl.BlockSpec((None, tr, c), lambda j, i, core_ref: (j, i, 0)),
        ),
        compiler_params=_cparams(("parallel", "parallel")),
    )(core, g.reshape(N_CHIPS, 2, r, c), sib)


def adamw(name, parts, w, m, v):
    rows, cols = w.shape
    tr = _pick(rows, (128, 64, 32, 16, 8))

    nparts = parts.shape[0]

    def body(p_ref, w_ref, m_ref, v_ref, g_ref, d_ref, m2_ref, v2_ref):
        g = p_ref[0].astype(F32)
        for k in range(1, nparts):
            g = g + p_ref[k].astype(F32)
        m2 = ADAM_B1 * m_ref[...] + (1.0 - ADAM_B1) * g
        v2 = ADAM_B2 * v_ref[...] + (1.0 - ADAM_B2) * jnp.square(g)
        m_hat = m2 / (1.0 - ADAM_B1 ** ADAM_STEP)
        v_hat = v2 / (1.0 - ADAM_B2 ** ADAM_STEP)
        g_ref[...] = g
        d_ref[...] = -ADAM_LR * (m_hat / (jnp.sqrt(v_hat) + ADAM_EPS) + ADAM_WD * w_ref[...])
        m2_ref[...] = m2
        v2_ref[...] = v2

    spec = pl.BlockSpec((tr, cols), lambda i: (i, 0))
    return pl.pallas_call(
        body,
        name=name,
        out_shape=[jax.ShapeDtypeStruct((rows, cols), F32)] * 4,
        grid=(rows // tr,),
        in_specs=[pl.BlockSpec((nparts, tr, cols), lambda i: (0, i, 0)), spec, spec, spec],
        out_specs=[spec] * 4,
        compiler_params=_cparams(("parallel",)),
    )(parts, w, m, v)


WEIGHT_ORDER = (
    "norm_mix_pre", "w_in", "conv_w", "conv_b", "dt_bias", "a_log", "d_skip", "ssd_norm", "w_ssd_branch",
    "w_sb_branch", "w_gate", "b_gate", "w_out", "norm_mix_post", "norm_ffn_pre", "w_ff1", "w_ff2", "norm_ffn_post",
    "w_ple", "w_ple_gate", "norm_ple_post",
)


def kernel(x, p, norm_mix_pre, w_in, conv_w, conv_b, dt_bias, a_log, d_skip, ssd_norm, w_ssd_branch, w_sb_branch, w_gate, b_gate, w_out, norm_mix_post, norm_ffn_pre, w_ff1, w_ff2, norm_ffn_post, w_ple, w_ple_gate, norm_ple_post, loss_target, m_norm_mix_pre, m_w_in, m_conv_w, m_conv_b, m_dt_bias, m_a_log, m_d_skip, m_ssd_norm, m_w_ssd_branch, m_w_sb_branch, m_w_gate, m_b_gate, m_w_out, m_norm_mix_post, m_norm_ffn_pre, m_w_ff1, m_w_ff2, m_norm_ffn_post, m_w_ple, m_w_ple_gate, m_norm_ple_post, v_norm_mix_pre, v_w_in, v_conv_w, v_conv_b, v_dt_bias, v_a_log, v_d_skip, v_ssd_norm, v_w_ssd_branch, v_w_sb_branch, v_w_gate, v_b_gate, v_w_out, v_norm_mix_post, v_norm_ffn_pre, v_w_ff1, v_w_ff2, v_norm_ffn_post, v_w_ple, v_w_ple_gate, v_norm_ple_post):
    a = dict(locals())
    seq = x.shape[1]
    x2 = x.reshape(seq, D_MODEL)
    target = loss_target.reshape(seq, D_MODEL)
    pb = p.reshape(seq, PLE_DIM).astype(MXU)

    def shard(prefix, name, dtype):
        v = a[prefix + name][0].astype(dtype)
        return _pad_to(v, IN_PAD) if name == "w_in" else v

    w_in_slabs, w_gate_slabs, conv_w_slabs = exchange(
        "gather_weights", [("gather2", shard("", "w_in", MXU)), ("gather2", shard("", "w_gate", MXU)),
                           ("all", a["conv_w"][0])])
    small = {n: a[n] for n, _ in SMALL}
    core = lax.axis_index("c").astype(jnp.int32).reshape(1)

    lossc, grad_x, parts, conv_slabs, g_acc = local_step(
        x2, pb, target, in_matrices(w_in_slabs, w_gate_slabs), build_small(small, conv_w_slabs),
        {n: shard("", n, MXU) for n in LATE}, core)
    loss = lax.psum(jnp.sum(lossc), ("x", "y", "c"))
    g_small = small_grads(g_acc, small)

    parts["conv_w"], small_parts = exchange(
        "small_grads", [("scatter", conv_slabs), ("all", pack_small(lambda n: g_small[n]))])

    leaves = {}
    for n, part in parts.items():
        res = adamw("adamw_" + n, part, shard("", n, F32), shard("m_", n, F32), shard("v_", n, F32))
        leaves[n] = [r[None, :, :IN_SHARD] if n == "w_in" else r[None] for r in res]
    res = adamw("adamw_small", small_parts, pack_small(lambda n: a[n]), pack_small(lambda n: a["m_" + n]),
                pack_small(lambda n: a["v_" + n]))
    for j, r in enumerate(res):
        for n, leaf in unpack_small(r).items():
            leaves.setdefault(n, [None] * 4)[j] = leaf
    outs = [loss, grad_x.reshape(x.shape)]
    for j in range(4):
        outs += [leaves[n][j] for n in WEIGHT_ORDER]
    return tuple(outs)
```

```python
import functools

import jax
import jax.numpy as jnp
from jax import lax
from jax.experimental import pallas as pl
from jax.experimental.pallas import tpu as pltpu

F32 = jnp.float32
MXU = jnp.bfloat16
VMEM_LIMIT = 56 * 1024 * 1024

D_MODEL = 1024
D_INNER = 2048
SSD_HEADS = 32
HEAD_DIM = 64
SSD_GROUPS = 8
D_STATE = 128
CONV_K = 4
CONV_DIM = 4096
CHUNK = 128
SB_WIDTH = 1024
D_FF = 4096
PLE_DIM = 256
RMS_EPS = 1e-6
SB_SCALE = HEAD_DIM ** -0.5
N_DEV = 8
LANES = 128

OFF_Z, OFF_XBC, OFF_Q, OFF_K, OFF_V = 0, 2048, 6144, 7168, 8192

ADAM_LR = 0.001
ADAM_B1 = 0.9
ADAM_B2 = 0.999
ADAM_EPS = 1e-08
ADAM_WD = 0.01
ADAM_STEP = 10


def _sig(x):
    return 0.5 * jnp.tanh(0.5 * x) + 0.5


def _softplus(x):
    return jnp.maximum(x, 0.0) + jnp.log(1.0 + jnp.exp(-jnp.abs(x)))


def _rms(x, w):
    return x * lax.rsqrt(jnp.mean(x * x, axis=-1, keepdims=True) + RMS_EPS) * w


def _dot(a, b):
    return jnp.dot(a, b, preferred_element_type=F32)


def _dot_nt(a, b):
    return lax.dot_general(a, b, (((1,), (1,)), ((), ())), preferred_element_type=F32)


def _dot_tn(a, b):
    return lax.dot_general(a, b, (((0,), (0,)), ((), ())), preferred_element_type=F32)


def _split3(x):
    x1 = x.astype(MXU)
    r = x - x1.astype(F32)
    x2 = r.astype(MXU)
    r = r - x2.astype(F32)
    return x1, x2, r.astype(MXU)


def _dot3_l(a, u):
    m = a.shape[0]
    d = _dot(jnp.concatenate(_split3(a), axis=0), u)
    return (d[2 * m:] + d[m:2 * m]) + d[:m]


def _dot3_r(u, a):
    n = a.shape[1]
    d = _dot(u, jnp.concatenate(_split3(a), axis=1))
    return (d[:, 2 * n:] + d[:, n:2 * n]) + d[:, :n]


def _iota(shape, dim):
    return lax.broadcasted_iota(jnp.int32, shape, dim)


def _tri(n, cmp):
    r, c = _iota((n, n), 0), _iota((n, n), 1)
    return cmp(r, c).astype(F32).astype(MXU)


def _cparams(sem):
    return pltpu.CompilerParams(dimension_semantics=sem, vmem_limit_bytes=VMEM_LIMIT)


def _pick(n, cands):
    for c in cands:
        if n % c == 0:
            return c
    return n


def mm(name, a, b, mode, add=None, out_dtype=F32, b_slabs=False, out_slabs=False, epi=None, epi_args=(), extra=(),
       side=None):
    slab = None
    if b_slabs:
        slab = b.shape[2]
        bshape = (b.shape[1], N_DEV * slab)
    else:
        bshape = b.shape
    if mode == "nn":
        (M, K), (K2, N) = a.shape, bshape
    elif mode == "nt":
        (M, K), (N, K2) = a.shape, bshape
    else:
        (K, M), (K2, N) = a.shape, bshape
    assert K == K2, (name, a.shape, b.shape)
    tm = _pick(M, (1024, 512, 256, 128))
    tn = _pick(N, (1024, 512, 256, 128))
    tk = _pick(K, (1024, 512, 256, 128))
    if b_slabs and mode == "nn":
        tn = slab
    if b_slabs and mode == "nt":
        tk = slab
    if out_slabs:
        assert mode == "tn" and N % N_DEV == 0
        tn = N // N_DEV
    nk = K // tk

    def body(*refs):
        refs = list(refs)
        a_ref, b_ref = refs[:2]
        add_ref = refs[2] if add is not None else None
        n_in = 2 + (add is not None)
        epi_refs = refs[n_in:n_in + len(epi_args)]
        o_ref = refs[n_in + len(epi_args)]
        extra_refs = refs[n_in + len(epi_args) + 1:-1]
        acc = refs[-1]
        k = pl.program_id(2)

        @pl.when(k == 0)
        def _():
            acc[...] = jnp.zeros_like(acc) if add is None else add_ref[...]

        av, bv = a_ref[...], b_ref[...]
        if mode == "nn":
            acc[...] += _dot(av, bv)
        elif mode == "nt":
            acc[...] += _dot_nt(av, bv)
        else:
            acc[...] += _dot_tn(av, bv)

        @pl.when(k == nk - 1)
        def _():
            res = acc[...]
            main = res if epi is None else epi(res, *[r[...] for r in epi_refs])
            o_ref[...] = main.astype(o_ref.dtype)
            for r, (_, fn) in zip(extra_refs, extra):
                r[...] = fn(res).astype(r.dtype)

    if mode == "nn":
        a_spec = pl.BlockSpec((tm, tk), lambda i, j, k: (i, k))
        b_spec = pl.BlockSpec((tk, tn), lambda i, j, k: (k, j))
    elif mode == "nt":
        a_spec = pl.BlockSpec((tm, tk), lambda i, j, k: (i, k))
        b_spec = pl.BlockSpec((tn, tk), lambda i, j, k: (j, k))
    else:
        a_spec = pl.BlockSpec((tk, tm), lambda i, j, k: (k, i))
        b_spec = pl.BlockSpec((tk, tn), lambda i, j, k: (k, j))
    if b_slabs and mode == "nn":
        b_spec = pl.BlockSpec((None, tk, tn), lambda i, j, k: (j, k, 0))
    if b_slabs and mode == "nt":
        b_spec = pl.BlockSpec((None, tn, tk), lambda i, j, k: (k, j, 0))
    o_spec = pl.BlockSpec((tm, tn), lambda i, j, k: (i, j))
    in_specs, args = [a_spec, b_spec], [a, b]
    if add is not None:
        in_specs.append(o_spec)
        args.append(add)
    for e in epi_args:
        in_specs.append(o_spec)
        args.append(e)
    out_sds = jax.ShapeDtypeStruct((M, N), out_dtype)
    if out_slabs:
        o_spec = pl.BlockSpec((None, tm, tn), lambda i, j, k: (j, i, 0))
        out_sds = jax.ShapeDtypeStruct((N_DEV, M, tn), out_dtype)
    grid = (M // tm, N // tn, nk)
    body, s_in, s_args, s_shape, s_out, s_scr = host_exchange(body, len(args), 1 + len(extra), grid, side)
    res = pl.pallas_call(
        body,
        name=name,
        out_shape=[out_sds] + [jax.ShapeDtypeStruct((M, N), dt) for dt, _ in extra] + s_shape,
        grid=grid,
        in_specs=in_specs + s_in,
        out_specs=[o_spec] * (1 + len(extra)) + s_out,
        scratch_shapes=[pltpu.VMEM((tm, tn), F32)] + s_scr,
        compiler_params=_cparams(("parallel", "parallel", "arbitrary") if side is None else ("arbitrary",) * 3),
    )(*args, *s_args)
    return res if extra or side is not None else res[0]


def rowwise(name, fn, rows, bcast, outs, accs=(), tr=512, ncb=1, side=None, groups=1):
    S = rows[0][0].shape[0]
    tr = min(tr, S)
    nrb = S // tr
    G = groups
    in_specs, args, in_w = [], [], []
    for arr, off, w in rows:
        assert off % (w * G) == 0 and arr.shape[0] == S
        in_specs.append(pl.BlockSpec((tr, w * G), lambda j, i, ob=off // (w * G): (i, ob + j)))
        args.append(arr)
        in_w.append(w)
    for arr, off, w in bcast:
        assert off % (w * G) == 0
        in_specs.append(pl.BlockSpec((arr.shape[0], w * G), lambda j, i, ob=off // (w * G): (0, ob + j)))
        args.append(arr)
        in_w.append(w)
    out_shape, out_specs, out_w = [], [], []
    for tw, w, dt in outs:
        out_shape.append(jax.ShapeDtypeStruct((S, tw), dt))
        out_specs.append(pl.BlockSpec((tr, w * G), lambda j, i: (i, j)))
        out_w.append(w)
    for tw, w in accs:
        out_shape.append(jax.ShapeDtypeStruct((1, tw), F32))
        out_specs.append(pl.BlockSpec((1, w * G), lambda j, i: (0, j)))
        out_w.append(w)
    nin, nout = len(args), len(outs)

    def body(*refs):
        i = pl.program_id(1)
        for g in range(G):
            cut = lambda w: slice(g * w, (g + 1) * w)
            res = fn(*[r[:, cut(w)] for r, w in zip(refs[:nin], in_w)])
            for k, (r, v) in enumerate(zip(refs[nin:], res)):
                cols = cut(out_w[k])
                if k < nout:
                    r[:, cols] = v.astype(r.dtype)
                    continue

                @pl.when(i == 0)
                def _(r=r, v=v, cols=cols):
                    r[:, cols] = v

                @pl.when(i > 0)
                def _(r=r, v=v, cols=cols):
                    r[:, cols] += v

    body, s_in, s_args, s_shape, s_out, s_scr = host_exchange(body, nin, len(out_shape), (ncb, nrb), side)
    res = pl.pallas_call(
        body,
        name=name,
        out_shape=out_shape + s_shape,
        grid=(ncb, nrb),
        in_specs=in_specs + s_in,
        out_specs=out_specs + s_out,
        scratch_shapes=s_scr,
        compiler_params=_cparams(("parallel" if side is None else "arbitrary", "arbitrary")),
    )(*args, *s_args)
    return res


CONV_TC = 128


CONV_R = 128
HALO = 8


def _conv_pre(e, w, b):
    shifted = [pltpu.roll(e, s, 0) for s in (1, 2, 3)]
    pre = b + w[3:4, :] * e
    for s in (1, 2, 3):
        pre = pre + w[3 - s:4 - s, :] * shifted[s - 1]
    return pre, shifted


def conv_fwd(projmain, conv_w, conv_b):
    S = projmain.shape[0]
    tc = CONV_TC

    def body(u_ref, w_ref, b_ref, o_ref):
        u, w = u_ref[...], w_ref[...]
        row = _iota(u.shape, 0)
        pre = b_ref[...] + w[3:4, :] * u
        for s in (1, 2, 3):
            pre = pre + w[3 - s:4 - s, :] * jnp.where(row >= s, pltpu.roll(u, s, 0), 0.0)
        o_ref[...] = pre * _sig(pre)

    return pl.pallas_call(
        body,
        name="conv_fwd",
        out_shape=jax.ShapeDtypeStruct((S, CONV_DIM), F32),
        grid=(CONV_DIM // tc,),
        in_specs=[
            pl.BlockSpec((S, tc), lambda j: (0, OFF_XBC // tc + j)),
            pl.BlockSpec((CONV_K, tc), lambda j: (0, j)),
            pl.BlockSpec((1, tc), lambda j: (0, j)),
        ],
        out_specs=pl.BlockSpec((S, tc), lambda j: (0, j)),
        compiler_params=_cparams(("parallel",)),
    )(projmain, conv_w, conv_b)


def conv_bwd(dxs, dxs_skip, d_b, d_c, projmain, conv_w, conv_b):
    S = projmain.shape[0]
    tc = CONV_TC
    n_xs, n_b = D_INNER // tc, SSD_GROUPS * D_STATE // tc

    def body(dx_ref, dskip_ref, dbm_ref, dcm_ref, u_ref, w_ref, b_ref, du_ref, dw_ref, db_ref, u_pad, d_pad):
        j = pl.program_id(0)
        zeros = jnp.zeros((HALO, tc), F32)
        for pad in (u_pad, d_pad):
            pad[0:HALO, :] = zeros
            pad[HALO + S:2 * HALO + S, :] = zeros
        u_pad[HALO:HALO + S, :] = u_ref[...]
        d_pad[HALO:HALO + S, :] = jnp.where(j < n_xs, dx_ref[...] + dskip_ref[...],
                                            jnp.where(j < n_xs + n_b, dbm_ref[...], dcm_ref[...]))
        w, b = w_ref[...], b_ref[...]
        n = CONV_R + 2 * HALO
        keep = slice(HALO, HALO + CONV_R)

        def chunk(c, sums):
            r0 = pl.multiple_of(c * CONV_R, CONV_R)
            e = u_pad[pl.ds(r0, n), :]
            pre, shifted = _conv_pre(e, w, b)
            sg = _sig(pre)
            dpre = d_pad[pl.ds(r0, n), :] * (sg * (1.0 + pre * (1.0 - sg)))
            du = w[3:4, :] * dpre
            for s in (1, 2, 3):
                du = du + w[3 - s:4 - s, :] * pltpu.roll(dpre, n - s, 0)
            du_ref[pl.ds(r0, CONV_R), :] = du[keep].astype(du_ref.dtype)
            dk = dpre[keep]
            taps = [shifted[2], shifted[1], shifted[0], e]
            return tuple(acc + jnp.sum(dk * t[keep], axis=0, keepdims=True) for acc, t in zip(sums[:4], taps)) + (
                sums[4] + jnp.sum(dk, axis=0, keepdims=True),)

        zero = jnp.zeros((1, tc), F32)
        sums = lax.fori_loop(0, S // CONV_R, chunk, (zero,) * 5)
        for k in range(CONV_K):
            dw_ref[k:k + 1, :] = sums[k]
        db_ref[...] = sums[4]

    return pl.pallas_call(
        body,
        name="conv_bwd",
        out_shape=[
            jax.ShapeDtypeStruct((S, CONV_DIM), MXU),
            jax.ShapeDtypeStruct((CONV_K, CONV_DIM), F32),
            jax.ShapeDtypeStruct((1, CONV_DIM), F32),
        ],
        grid=(CONV_DIM // tc,),
        in_specs=[
            pl.BlockSpec((S, tc), lambda j: (0, jnp.minimum(j, n_xs - 1))),
            pl.BlockSpec((S, tc), lambda j: (0, jnp.minimum(j, n_xs - 1))),
            pl.BlockSpec((S, tc), lambda j: (0, jnp.clip(j - n_xs, 0, n_b - 1))),
            pl.BlockSpec((S, tc), lambda j: (0, jnp.clip(j - n_xs - n_b, 0, n_b - 1))),
            pl.BlockSpec((S, tc), lambda j: (0, OFF_XBC // tc + j)),
            pl.BlockSpec((CONV_K, tc), lambda j: (0, j)),
            pl.BlockSpec((1, tc), lambda j: (0, j)),
        ],
        out_specs=[
            pl.BlockSpec((S, tc), lambda j: (0, j)),
            pl.BlockSpec((CONV_K, tc), lambda j: (0, j)),
            pl.BlockSpec((1, tc), lambda j: (0, j)),
        ],
        scratch_shapes=[pltpu.VMEM((S + 2 * HALO, tc), F32)] * 2,
        compiler_params=_cparams(("arbitrary",)),
    )(dxs, dxs_skip, d_b, d_c, projmain, conv_w, conv_b)


def _head_expand():
    r, j = _iota((LANES, D_INNER), 0), _iota((LANES, D_INNER), 1)
    return ((j >= r * HEAD_DIM) & (j < r * HEAD_DIM + HEAD_DIM)).astype(F32).astype(MXU)


def _head_reduce():
    j, r = _iota((D_INNER, LANES), 0), _iota((D_INNER, LANES), 1)
    return ((j >= r * HEAD_DIM) & (j < r * HEAD_DIM + HEAD_DIM)).astype(F32).astype(MXU)


def ssd_prep(dtraw, dt_bias_pad, a_exp):
    S = dtraw.shape[0]

    def body(dtr_ref, bias_ref, a_ref, dte_ref, cse_ref):
        dt = _softplus(dtr_ref[...] + bias_ref[...])
        dte = _dot3_l(dt, _head_expand())
        dte_ref[...] = dte
        incl = _tri(CHUNK, lambda r, c: r >= c)
        cse_ref[...] = _dot3_r(incl, dte * a_ref[...])

    return pl.pallas_call(
        body,
        name="ssd_prep",
        out_shape=[jax.ShapeDtypeStruct((S, D_INNER), F32)] * 2,
        grid=(S // CHUNK,),
        in_specs=[
            pl.BlockSpec((CHUNK, LANES), lambda c: (c, 0)),
            pl.BlockSpec((1, LANES), lambda c: (0, 0)),
            pl.BlockSpec((1, D_INNER), lambda c: (0, 0)),
        ],
        out_specs=[pl.BlockSpec((CHUNK, D_INNER), lambda c: (c, 0))] * 2,
        compiler_params=_cparams(("parallel",)),
    )(dtraw, dt_bias_pad, a_exp)


GW = 4 * HEAD_DIM


def ssd_fwd(xbc_act, dte, cse, side=None):
    S = xbc_act.shape[0]
    nc = S // CHUNK

    def body(xs_ref, b_ref, c_ref, dte_ref, cse_ref, y_ref, st_ref, s_scr):
        c = pl.program_id(1)

        @pl.when(c == 0)
        def _():
            s_scr[...] = jnp.zeros_like(s_scr)

        s_in = s_scr[...]
        st_ref[0] = s_in
        cs = cse_ref[...]
        xd = xs_ref[...] * dte_ref[...]
        btb = b_ref[...].T.astype(MXU)
        cs_last = cs[CHUNK - 1:CHUNK, :]
        gy = _dot(c_ref[...].astype(MXU), jnp.concatenate([btb, s_in.astype(MXU)], axis=1))
        g, y = gy[:, :CHUNK], gy[:, CHUNK:] * jnp.exp(cs)
        cs_t = cs.T
        row, col = _iota((CHUNK, CHUNK), 0), _iota((CHUNK, CHUNK), 1)
        ms = []
        for h in range(4):
            lo = HEAD_DIM * h
            lam = jnp.where(row >= col, jnp.exp(cs[:, lo:lo + 1] - cs_t[lo:lo + 1, :]), 0.0)
            ms.append((g * lam).astype(MXU))
        yd = _dot(jnp.concatenate(ms, axis=0), xd.astype(MXU))
        lane = _iota((CHUNK, GW), 1)
        for h in range(4):
            y = y + jnp.where((lane >= HEAD_DIM * h) & (lane < HEAD_DIM * (h + 1)), yd[CHUNK * h:CHUNK * (h + 1)], 0.0)
        y_ref[...] = y
        w = (xd * jnp.exp(cs_last - cs)).astype(MXU)
        s_scr[...] = jnp.exp(cs_last) * s_in + _dot(btb, w)

    grid = (SSD_GROUPS, nc)
    body, s_in, s_args, s_shape, s_out, s_scr = host_exchange(body, 5, 2, grid, side)
    return pl.pallas_call(
        body,
        name="ssd_fwd",
        out_shape=[
            jax.ShapeDtypeStruct((S, D_INNER), F32),
            jax.ShapeDtypeStruct((nc, D_STATE, D_INNER), F32),
        ] + s_shape,
        grid=grid,
        in_specs=[
            pl.BlockSpec((CHUNK, GW), lambda g, c: (c, g)),
            pl.BlockSpec((CHUNK, D_STATE), lambda g, c: (c, D_INNER // D_STATE + g)),
            pl.BlockSpec((CHUNK, D_STATE), lambda g, c: (c, D_INNER // D_STATE + SSD_GROUPS + g)),
            pl.BlockSpec((CHUNK, GW), lambda g, c: (c, g)),
            pl.BlockSpec((CHUNK, GW), lambda g, c: (c, g)),
        ] + s_in,
        out_specs=[
            pl.BlockSpec((CHUNK, GW), lambda g, c: (c, g)),
            pl.BlockSpec((1, D_STATE, GW), lambda g, c: (c, 0, g)),
        ] + s_out,
        scratch_shapes=[pltpu.VMEM((D_STATE, GW), F32)] + s_scr,
        compiler_params=_cparams(("parallel" if side is None else "arbitrary", "arbitrary")),
    )(xbc_act, xbc_act, xbc_act, dte, cse, *s_args)


def ssd_bwd(dy, xbc_act, dte, cse, states, a_exp, side=None):
    S = xbc_act.shape[0]
    nc = S // CHUNK

    def body(dy_ref, xs_ref, b_ref, c_ref, dte_ref, cse_ref, sin_ref, sout_ref, a_ref,
             dxs_ref, db_ref, dc_ref, ddt_ref, dal_ref, ds_scr):
        j = pl.program_id(1)

        @pl.when(j == 0)
        def _():
            ds_scr[...] = jnp.zeros_like(ds_scr)
            dal_ref[...] = jnp.zeros_like(dal_ref)

        ds_out = ds_scr[...]
        dyv, xs = dy_ref[...], xs_ref[...]
        dt, cs = dte_ref[...], cse_ref[...]
        s_in = sin_ref[0]
        bm, cm = b_ref[...], c_ref[...]
        bb, cb = bm.astype(MXU), cm.astype(MXU)
        btb, ctb = bm.T.astype(MXU), cm.T.astype(MXU)
        dsb, sib = ds_out.astype(MXU), s_in.astype(MXU)
        xd = xs * dt
        ecs = jnp.exp(cs)
        cs_last = cs[CHUNK - 1:CHUNK, :]
        eend = jnp.exp(cs_last - cs)
        gy = _dot(cb, jnp.concatenate([btb, sib], axis=1))
        g, yoff = gy[:, :CHUNK], gy[:, CHUNK:] * ecs
        gd = _dot(bb, jnp.concatenate([ctb, dsb], axis=1))
        g_t, dxd_off = gd[:, :CHUNK], gd[:, CHUNK:] * eend
        cs_t = cs.T
        row, col = _iota((CHUNK, CHUNK), 0), _iota((CHUNK, CHUNK), 1)
        lane = _iota((CHUNK, GW), 1)
        heads = [(lane >= HEAD_DIM * h) & (lane < HEAD_DIM * (h + 1)) for h in range(4)]
        dyb, xdb = dyv.astype(MXU), xd.astype(MXU)
        dm_all = _dot_nt(jnp.concatenate([jnp.where(hm, dyv, 0.0) for hm in heads], axis=0).astype(MXU), xdb)
        dmt_all = _dot_nt(jnp.concatenate([jnp.where(hm, xd, 0.0) for hm in heads], axis=0).astype(MXU), dyb)
        lams, m_ts = [], []
        for h in range(4):
            lo = HEAD_DIM * h
            cs_col, cs_row = cs[:, lo:lo + 1], cs_t[lo:lo + 1, :]
            lams.append(jnp.where(row >= col, jnp.exp(cs_col - cs_row), 0.0))
            m_ts.append(g_t * jnp.where(col >= row, jnp.exp(cs_row - cs_col), 0.0))
        acc_all = _dot(jnp.concatenate(m_ts, axis=0).astype(MXU), dyb)
        dxd = dxd_off
        dg = jnp.zeros((CHUNK, CHUNK), F32)
        dcs = dyv * yoff - xd * dxd_off
        for h in range(4):
            blk = slice(CHUNK * h, CHUNK * (h + 1))
            dm, dm_t = dm_all[blk], dmt_all[blk]
            dxd = dxd + jnp.where(heads[h], acc_all[blk], 0.0)
            dg = dg + dm * lams[h]
            wdiff = (jnp.sum(dm * (g * lams[h]), axis=1, keepdims=True)
                     - jnp.sum(dm_t * m_ts[h], axis=1, keepdims=True))
            dcs = dcs + jnp.where(lane == HEAD_DIM * h, wdiff, 0.0)
        dye = (dyv * ecs).astype(MXU)
        dc_ref[...] = _dot(dg.astype(MXU), bb) + _dot_nt(dye, sib)
        db_ref[...] = _dot(dg.T.astype(MXU), cb) + _dot_nt((xd * eend).astype(MXU), dsb)
        ds_scr[...] = jnp.exp(cs_last) * ds_out + _dot(ctb, dye)
        last = jnp.sum(ds_out * sout_ref[0], axis=0, keepdims=True)
        rows = _iota((CHUNK, GW), 0)
        dcs = dcs + jnp.where(rows == CHUNK - 1, last, 0.0)
        dda = _dot3_r(_tri(CHUNK, lambda r, c: c >= r), dcs)
        ddt_ref[...] = a_ref[...] * dda + dxd * xs
        dal_ref[...] += jnp.sum(dt * dda, axis=0, keepdims=True)
        dxs_ref[...] = dxd * dt

    rc = lambda g, j: (nc - 1 - j, g)
    grid = (SSD_GROUPS, nc)
    body, s_in, s_args, s_shape, s_out, s_scr = host_exchange(body, 9, 5, grid, side)
    return pl.pallas_call(
        body,
        name="ssd_bwd",
        out_shape=[
            jax.ShapeDtypeStruct((S, D_INNER), F32),
            jax.ShapeDtypeStruct((S, SSD_GROUPS * D_STATE), F32),
            jax.ShapeDtypeStruct((S, SSD_GROUPS * D_STATE), F32),
            jax.ShapeDtypeStruct((S, D_INNER), F32),
            jax.ShapeDtypeStruct((1, D_INNER), F32),
        ] + s_shape,
        grid=grid,
        in_specs=[
            pl.BlockSpec((CHUNK, GW), rc),
            pl.BlockSpec((CHUNK, GW), rc),
            pl.BlockSpec((CHUNK, D_STATE), lambda g, j: (nc - 1 - j, D_INNER // D_STATE + g)),
            pl.BlockSpec((CHUNK, D_STATE), lambda g, j: (nc - 1 - j, D_INNER // D_STATE + SSD_GROUPS + g)),
            pl.BlockSpec((CHUNK, GW), rc),
            pl.BlockSpec((CHUNK, GW), rc),
            pl.BlockSpec((1, D_STATE, GW), lambda g, j: (nc - 1 - j, 0, g)),
            pl.BlockSpec((1, D_STATE, GW), lambda g, j: (jnp.minimum(nc - j, nc - 1), 0, g)),
            pl.BlockSpec((1, GW), lambda g, j: (0, g)),
        ] + s_in,
        out_specs=[
            pl.BlockSpec((CHUNK, GW), rc),
            pl.BlockSpec((CHUNK, D_STATE), rc),
            pl.BlockSpec((CHUNK, D_STATE), rc),
            pl.BlockSpec((CHUNK, GW), rc),
            pl.BlockSpec((1, GW), lambda g, j: (0, g)),
        ] + s_out,
        scratch_shapes=[pltpu.VMEM((D_STATE, GW), F32)] + s_scr,
        compiler_params=_cparams(("parallel" if side is None else "arbitrary", "arbitrary")),
    )(dy, xbc_act, xbc_act, xbc_act, dte, cse, states, states, a_exp, *s_args)


SB_T = 256
SB_DROP = 104.0


def _sb_scores(qm, k_ref, ks, rowi, coli, diag):
    kblk = k_ref[pl.ds(ks, SB_T), :].astype(MXU)
    z = _dot_nt(qm, kblk) * SB_SCALE
    sp = _softplus(z)
    if not diag:
        return kblk, z, None, sp, sp
    mask = (ks + coli) < rowi
    return kblk, z, mask, sp, jnp.where(mask, sp, 0.0)


def _sb_stack(v):
    lane = _iota(v.shape, 1)
    return jnp.concatenate([jnp.where(lane < HEAD_DIM, v, 0.0), jnp.where(lane >= HEAD_DIM, v, 0.0)], axis=0)


def _sb_unstack(v):
    lane = _iota((SB_T, LANES), 1)
    return jnp.where(lane < HEAD_DIM, v[:SB_T], v[SB_T:])


def _sb_rows(qb):
    r = _iota((2 * SB_T, SB_T), 0)
    return qb * SB_T + jnp.where(r >= SB_T, r - SB_T, r), _iota((2 * SB_T, SB_T), 1)


def sb_fwd(projmain):
    S = projmain.shape[0]
    nq = S // SB_T

    def body(q_ref, k_ref, v_ref, o_ref, t_ref, n_ref):
        hp, qb = pl.program_id(0), pl.program_id(1)
        qst = _sb_stack(q_ref[...]).astype(MXU)
        rowi, coli = _sb_rows(qb)
        u_after = _tri(SB_T, lambda r, c: r > c)

        def cond(carry):
            i, rmin, _, _ = carry
            return (i <= qb) & (rmin < SB_DROP)

        def kstep(carry, diag=False):
            i, _, r, acc = carry
            ks = pl.multiple_of((qb - i) * SB_T, SB_T)
            _, z, mask, sp, spm = _sb_scores(qst, k_ref, ks, rowi, coli, diag)
            vblk = v_ref[pl.ds(ks, SB_T), :].astype(MXU)
            a = jnp.exp(z - sp - _dot3_l(spm, u_after) - r)
            if diag:
                a = jnp.where(mask, a, 0.0)
            acc = acc + _dot(a.astype(MXU), vblk)
            r = r + jnp.sum(spm, axis=1, keepdims=True)
            return i + 1, jnp.min(r), r, acc

        first = kstep((jnp.int32(0), jnp.float32(0.0), jnp.zeros((2 * SB_T, 1), F32),
                       jnp.zeros((2 * SB_T, LANES), F32)), diag=True)
        n, _, r, acc = lax.while_loop(cond, kstep, first)
        o_ref[...] = _sb_unstack(acc).astype(o_ref.dtype)
        t_ref[...] = _sb_unstack(jnp.broadcast_to(r, (2 * SB_T, LANES)))
        n_ref[hp, qb] = n

    return pl.pallas_call(
        body,
        name="sb_fwd",
        out_shape=[jax.ShapeDtypeStruct((S, SB_WIDTH), MXU), jax.ShapeDtypeStruct((S, SB_WIDTH), F32),
                   jax.ShapeDtypeStruct((SB_WIDTH // LANES, nq), jnp.int32)],
        grid=(SB_WIDTH // LANES, nq),
        in_specs=[
            pl.BlockSpec((SB_T, LANES), lambda h, i: (i, OFF_Q // LANES + h)),
            pl.BlockSpec((S, LANES), lambda h, i: (0, OFF_K // LANES + h)),
            pl.BlockSpec((S, LANES), lambda h, i: (0, OFF_V // LANES + h)),
        ],
        out_specs=[pl.BlockSpec((SB_T, LANES), lambda h, i: (i, h))] * 2
        + [pl.BlockSpec(memory_space=pltpu.SMEM)],
        compiler_params=_cparams(("arbitrary", "arbitrary")),
    )(projmain, projmain, projmain)


def sb_bwd(projmain, do, t_exp, nblk):
    S = projmain.shape[0]
    nq = S // SB_T

    def body(n_ref, q_ref, k_ref, v_ref, do_ref, t_ref, dq_ref, dk_ref, dv_ref):
        hp, qb = pl.program_id(0), pl.program_id(1)

        @pl.when(qb == 0)
        def _():
            dk_ref[...] = jnp.zeros_like(dk_ref)
            dv_ref[...] = jnp.zeros_like(dv_ref)

        qst = _sb_stack(q_ref[...]).astype(MXU)
        dost = _sb_stack(do_ref[...]).astype(MXU)
        tv = t_ref[...]
        lane = _iota((SB_T, LANES), 1)
        tot = jnp.concatenate(
            [jnp.sum(jnp.where(lane == HEAD_DIM * hh, tv, 0.0), axis=1, keepdims=True) for hh in range(2)], axis=0)
        rowi, coli = _sb_rows(qb)
        u_upto = _tri(SB_T, lambda r, c: r <= c)
        u_before = _tri(SB_T, lambda r, c: r < c)
        kept = jnp.clip(n_ref[hp, qb], 1, qb + 1)

        def kstep(kb, carry, diag=False):
            psp, pg, dq = carry
            ks = pl.multiple_of(kb * SB_T, SB_T)
            kblk, z, mask, sp, spm = _sb_scores(qst, k_ref, ks, rowi, coli, diag)
            vblk = v_ref[pl.ds(ks, SB_T), :].astype(MXU)
            after = tot - (psp + _dot3_l(spm, u_upto))
            a = jnp.exp(z - sp - after)
            if diag:
                a = jnp.where(mask, a, 0.0)
            gm = _dot_nt(dost, vblk) * a
            before = pg + _dot3_l(gm, u_before)
            sg = jnp.exp(z - sp)
            dz = (gm * (1.0 - sg) - sg * before) * SB_SCALE
            if diag:
                dz = jnp.where(mask, dz, 0.0)
            dzb = dz.astype(MXU)
            dq = dq + _dot(dzb, kblk)
            dk_ref[pl.ds(ks, SB_T), :] += _dot_tn(dzb, qst)
            dv_ref[pl.ds(ks, SB_T), :] += _dot_tn(a.astype(MXU), dost)
            return (psp + jnp.sum(spm, axis=1, keepdims=True),
                    pg + jnp.sum(gm, axis=1, keepdims=True), dq)

        zero1 = jnp.zeros((2 * SB_T, 1), F32)
        state = lax.fori_loop(qb + 1 - kept, qb, kstep, (zero1, zero1, jnp.zeros((2 * SB_T, LANES), F32)))
        dq = kstep(qb, state, diag=True)[2]
        dq_ref[...] = _sb_unstack(dq).astype(dq_ref.dtype)

    return pl.pallas_call(
        body,
        name="sb_bwd",
        out_shape=[
            jax.ShapeDtypeStruct((S, SB_WIDTH), MXU),
            jax.ShapeDtypeStruct((S, SB_WIDTH), F32),
            jax.ShapeDtypeStruct((S, SB_WIDTH), F32),
        ],
        grid=(SB_WIDTH // LANES, nq),
        in_specs=[
            pl.BlockSpec(memory_space=pltpu.SMEM),
            pl.BlockSpec((SB_T, LANES), lambda h, i: (i, OFF_Q // LANES + h)),
            pl.BlockSpec((S, LANES), lambda h, i: (0, OFF_K // LANES + h)),
            pl.BlockSpec((S, LANES), lambda h, i: (0, OFF_V // LANES + h)),
            pl.BlockSpec((SB_T, LANES), lambda h, i: (i, h)),
            pl.BlockSpec((SB_T, LANES), lambda h, i: (i, h)),
        ],
        out_specs=[
            pl.BlockSpec((SB_T, LANES), lambda h, i: (i, h)),
            pl.BlockSpec((S, LANES), lambda h, i: (0, h)),
            pl.BlockSpec((S, LANES), lambda h, i: (0, h)),
        ],
        compiler_params=_cparams(("arbitrary", "arbitrary")),
    )(nblk, projmain, projmain, projmain, do, t_exp)


def local_step(x, pb, target, W, P, late, core):
    D = D_MODEL
    full = lambda a, w=D: (a, 0, w)

    (n1b,) = rowwise("norm_pre", lambda xv, g: (_rms(xv, g),), [full(x)], [full(P["norm_mix_pre"])],
                     [(D, D, MXU)])
    projmain = mm("in_proj", n1b, W["main"], "nn")
    gate_pre = mm("gate_proj", n1b, W["gate"], "nn", b_slabs=True)
    dtraw = mm("dt_proj", n1b, W["dt"], "nn")
    xbc_act = conv_fwd(projmain, P["conv_w"], P["conv_b"])
    dte, cse = ssd_prep(dtraw, P["dt_bias_pad"], P["a_exp"])
    y_scan, states, *got = ssd_fwd(xbc_act, dte, cse, side=exchange_plan([("gather2", late[n]) for n in LATE]))
    W = {**W, **late_matrices(dict(zip(LATE, got)))}

    def f_gate(ysc, xs, z, dsk, nw):
        return _rms((ysc + xs * dsk) * (z * _sig(z)), nw)

    (y_ssd_b,) = rowwise("ssd_gate", lambda *a: (f_gate(*a),),
                         [(y_scan, 0, GW), (xbc_act, 0, GW), (projmain, OFF_Z, GW)],
                         [(P["dsk_exp"], 0, GW), (P["ssd_norm"], 0, GW)], [(D_INNER, GW, MXU)], tr=256,
                         groups=SSD_GROUPS)
    y_sb_b, t_exp, sb_kept = sb_fwd(projmain)
    u1 = mm("ssd_branch", y_ssd_b, W["ssd"], "nn")
    u2 = mm("sb_branch", y_sb_b, W["sb"], "nn")

    def f_merge(a1, a2, g1, g2, b1, b2):
        return (_sig(g1 + b1) * a1 + _sig(g2 + b2) * a2,)

    gate_rows = [(gate_pre, 0, D), (gate_pre, D, D)]
    gate_bias = [(P["b_gate"], 0, D), (P["b_gate"], D, D)]
    (merged_b,) = rowwise("merge", f_merge, [full(u1), full(u2)] + gate_rows, gate_bias, [(D, D, MXU)])
    mo = mm("out_proj", merged_b, W["out"], "nn")

    def f_mid(xv, m, gpost, gffn):
        h1 = xv + _rms(m, gpost)
        return h1, _rms(h1, gffn)

    h1, n2b = rowwise("mix_post", f_mid, [full(x), full(mo)],
                      [full(P["norm_mix_post"]), full(P["norm_ffn_pre"])], [(D, D, F32), (D, D, MXU)])
    a_ff, rb = mm("ff1", n2b, W["ff1"], "nn", b_slabs=True,
                  extra=[(MXU, lambda acc: jnp.square(jnp.maximum(acc, 0.0)))])
    ff = mm("ff2", rb, W["ff2"], "nn")

    def f_ffn_post(h, f, g):
        h2 = h + _rms(f, g)
        return h2, h2

    h2, h2b = rowwise("ffn_post", f_ffn_post, [full(h1), full(ff)], [full(P["norm_ffn_post"])],
                      [(D, D, F32), (D, D, MXU)])
    pgp = mm("ple_gate", h2b, W["pg"], "nn")
    pe = mm("ple_proj", pb, W["ple"], "nn", b_slabs=True)

    def f_ple(h2v, gp, pev, tgt, g):
        f = lambda h, a, b, gg: h + _rms(_sig(a) * b, gg)
        h3, vjp = jax.vjp(f, h2v, gp, pev, g)
        err = h3 - tgt
        dh2, dgp, dpe, dg = vjp(err * (1.0 / D))
        lossc = (0.5 / D) * jnp.sum(err * err, axis=0, keepdims=True)
        return dh2, dgp, dpe, dg, lossc

    dh2a, dpgp_b, dpe_b, g_ple, lossc = rowwise(
        "ple_loss", f_ple, [full(h2), full(pgp), full(pe), full(target)], [full(P["norm_ple_post"])],
        [(D, D, F32), (D, D, MXU), (D, D, MXU)], [(D, D), (D, D)])
    dh2 = mm("d_ple_gate_x", dpgp_b, W["pg"], "nt", add=dh2a)
    gW = {}
    gW["w_ple_gate"] = mm("d_ple_gate_w", h2b, dpgp_b, "tn")
    gW["w_ple"] = mm("d_ple_w", pb, dpe_b, "tn", out_slabs=True)

    def b_ffn_post(d, f, g):
        _, vjp = jax.vjp(_rms, f, g)
        return vjp(d)

    dff_b, g_ffn_post = rowwise("d_ffn_post", b_ffn_post, [full(dh2), full(ff)], [full(P["norm_ffn_post"])],
                                [(D, D, MXU)], [(D, D)])
    da_b = mm("d_ff2_x", dff_b, W["ff2"], "nt", out_dtype=MXU,
              epi=lambda acc, act: acc * (2.0 * jnp.maximum(act, 0.0)), epi_args=[a_ff])
    gW["w_ff2"] = mm("d_ff2_w", rb, dff_b, "tn")
    dn2 = mm("d_ff1_x", da_b, W["ff1"], "nt", b_slabs=True)
    gW["w_ff1"] = mm("d_ff1_w", n2b, da_b, "tn", out_slabs=True)

    def b_mid(d2, dn, h, m, gpost, gffn):
        _, vjp = jax.vjp(_rms, h, gffn)
        dh, dgffn = vjp(dn)
        dh1 = d2 + dh
        _, vjp2 = jax.vjp(_rms, m, gpost)
        dm, dgpost = vjp2(dh1)
        return dh1, dm, dgpost, dgffn

    dh1, dmo_b, g_mix_post, g_ffn_pre = rowwise(
        "d_mix_post", b_mid, [full(dh2), full(dn2), full(h1), full(mo)],
        [full(P["norm_mix_post"]), full(P["norm_ffn_pre"])], [(D, D, F32), (D, D, MXU)], [(D, D), (D, D)])
    dmerged = mm("d_out_x", dmo_b, W["out"], "nt")
    gW["w_out"] = mm("d_out_w", merged_b, dmo_b, "tn")

    def b_merge(d, a1, a2, g1, g2, b1, b2):
        s1, s2 = _sig(g1 + b1), _sig(g2 + b2)
        dg1 = d * a1 * s1 * (1.0 - s1)
        dg2 = d * a2 * s2 * (1.0 - s2)
        dg = jnp.concatenate([dg1, dg2], axis=1)
        return d * s1, d * s2, dg, jnp.sum(dg, axis=0, keepdims=True)

    du1_b, du2_b, dgp_b, g_b_gate = rowwise(
        "d_merge", b_merge, [full(dmerged), full(u1), full(u2)] + gate_rows, gate_bias,
        [(D, D, MXU), (D, D, MXU), (2 * D, 2 * D, MXU)], [(2 * D, 2 * D)])
    dy_ssd = mm("d_ssd_branch_x", du1_b, W["ssd"], "nt")
    dy_sb = mm("d_sb_branch_x", du2_b, W["sb"], "nt")
    gW["w_ssd_branch"] = mm("d_ssd_branch_w", y_ssd_b, du1_b, "tn")
    gW["w_sb_branch"] = mm("d_sb_branch_w", y_sb_b, du2_b, "tn")

    def b_gate(d, ysc, xs, z, dsk, nw):
        _, vjp = jax.vjp(f_gate, ysc, xs, z, dsk, nw)
        return vjp(d)

    slabs = {n: gW[n] if n in COL_SHARDED else gW[n].reshape((N_DEV,) + ROW_SHARDED[n]) for n in LATE}
    dy_scan, dxs_skip, dz_b, g_dsk_exp, g_ssd_norm, *from_sibling = rowwise(
        "d_ssd_gate", b_gate, [(dy_ssd, 0, GW), (y_scan, 0, GW), (xbc_act, 0, GW), (projmain, OFF_Z, GW)],
        [(P["dsk_exp"], 0, GW), (P["ssd_norm"], 0, GW)],
        [(D_INNER, GW, F32), (D_INNER, GW, F32), (D_INNER, GW, MXU)], [(D_INNER, GW), (D_INNER, GW)],
        tr=256, groups=SSD_GROUPS, side=exchange_plan([("pair", slabs[n]) for n in LATE]))
    chip_sums = [pair_sum("pair_sum_" + n, slabs[n], s, core) for n, s in zip(LATE, from_sibling)]
    dxs, d_b, d_c, ddt_part, g_a_exp, *late_parts = ssd_bwd(
        dy_scan, xbc_act, dte, cse, states, P["a_exp"], side=exchange_plan([("chips", s) for s in chip_sums]))

    def b_dt(dpart, dtr, bias):
        ddt = _dot3_l(dpart, _head_reduce())
        d = ddt * _sig(dtr + bias)
        return d, jnp.sum(d, axis=0, keepdims=True)

    ddt_b, g_dt_bias_pad = rowwise("d_dt", b_dt, [(ddt_part, 0, D_INNER), (dtraw, 0, LANES)],
                                   [(P["dt_bias_pad"], 0, LANES)], [(LANES, LANES, MXU)], [(LANES, LANES)])
    dxbc_b, g_conv_w, g_conv_b = conv_bwd(dxs, dxs_skip, d_b, d_c, projmain, P["conv_w"], P["conv_b"])
    dq_b, dk, dv = sb_bwd(projmain, dy_sb, t_exp, sb_kept)
    dmain_b = jnp.concatenate([dz_b, dxbc_b, dq_b, dk.astype(MXU), dv.astype(MXU)], axis=1)
    g_main = mm("d_in_w", n1b, dmain_b, "tn")
    g_dt = mm("d_dt_w", n1b, ddt_b, "tn")
    last = {"w_gate": mm("d_gate_w", n1b, dgp_b, "tn", out_slabs=True)}
    last["w_in"], conv_slabs = in_grad_slabs({"main": g_main, "dt": g_dt, "conv_w": g_conv_w})
    dn1_dt = mm("d_dt_x", ddt_b, W["dt"], "nt")
    dn1_gate, *from_sibling = mm("d_gate_x", dgp_b, W["gate"], "nt", add=dn1_dt, b_slabs=True,
                                 side=exchange_plan([("pair", last[n]) for n in FIRST]))
    chip_sums = [pair_sum("pair_sum_" + n, last[n], s, core) for n, s in zip(FIRST, from_sibling)]
    dn1, *first_parts = mm("d_in_x", dmain_b, W["main"], "nt", add=dn1_gate,
                           side=exchange_plan([("chips", s) for s in chip_sums]))

    def b_pre(d1, dn, xv, g):
        _, vjp = jax.vjp(_rms, xv, g)
        dx, dg = vjp(dn)
        return d1 + dx, dg

    grad_x, g_mix_pre = rowwise("d_norm_pre", b_pre, [full(dh1), full(dn1), full(x)], [full(P["norm_mix_pre"])],
                                [(D, D, F32)], [(D, D)])

    parts = {**dict(zip(LATE, late_parts)), **dict(zip(FIRST, first_parts))}
    gS = {
        "norm_mix_pre": g_mix_pre, "conv_b": g_conv_b, "dt_bias_pad": g_dt_bias_pad, "a_exp": g_a_exp,
        "dsk_exp": g_dsk_exp, "ssd_norm": g_ssd_norm, "b_gate": g_b_gate,
        "norm_mix_post": g_mix_post, "norm_ffn_pre": g_ffn_pre, "norm_ffn_post": g_ffn_post,
        "norm_ple_post": g_ple,
    }
    return lossc, grad_x, parts, conv_slabs, gS


IN_SPLITS = (2048, 6144, 6176, 7200, 8224)
IN_SHARD = 1156
IN_PAD = 1280
COL_SHARDED = {"w_in": (1024, IN_SHARD), "conv_w": (4, 512), "w_gate": (1024, 256), "w_ff1": (1024, 512),
               "w_ple": (256, 128)}
ROW_SHARDED = {"w_ssd_branch": (256, 1024), "w_sb_branch": (128, 1024), "w_out": (128, 1024),
               "w_ff2": (512, 1024), "w_ple_gate": (128, 1024)}
SHARDED = tuple(COL_SHARDED) + tuple(ROW_SHARDED)
SMALL = (
    ("norm_mix_pre", 1024), ("conv_b", 4096), ("dt_bias", 32), ("a_log", 32), ("d_skip", 32), ("ssd_norm", 2048),
    ("b_gate", 2048), ("norm_mix_post", 1024), ("norm_ffn_pre", 1024), ("norm_ffn_post", 1024),
    ("norm_ple_post", 1024),
)
ROW = 1024
SMALL_ROWS = 16


def _rows_of(n):
    return -(-n // ROW)


def _pad_to(v, n, axis=-1):
    pad = [(0, 0)] * v.ndim
    pad[axis] = (0, n - v.shape[axis])
    return jnp.pad(v, pad)


FIRST = ("w_in", "w_gate")
LATE = ("w_ssd_branch", "w_sb_branch", "w_out", "w_ff1", "w_ff2", "w_ple", "w_ple_gate")


def in_matrices(w_in_slabs, w_gate_slabs):
    w_in = jnp.concatenate([w_in_slabs[k, :, :IN_SHARD] for k in range(N_DEV)], axis=1)
    return {
        "main": jnp.concatenate([w_in[:, :IN_SPLITS[1]], w_in[:, IN_SPLITS[2]:]], axis=1),
        "dt": _pad_to(w_in[:, IN_SPLITS[1]:IN_SPLITS[2]], LANES),
        "gate": w_gate_slabs,
    }


def late_matrices(g):
    stack = lambda a: a.reshape(a.shape[0] * a.shape[1], a.shape[2])
    return {
        "ff1": g["w_ff1"], "ple": g["w_ple"],
        "ssd": stack(g["w_ssd_branch"]), "sb": stack(g["w_sb_branch"]), "out": stack(g["w_out"]),
        "ff2": stack(g["w_ff2"]), "pg": stack(g["w_ple_gate"]),
    }


def build_small(small, conv_w_slabs):
    P = {k: small[k] for k in ("norm_mix_pre", "conv_b", "ssd_norm", "b_gate", "norm_mix_post", "norm_ffn_pre",
                               "norm_ffn_post", "norm_ple_post")}
    P["conv_w"] = jnp.concatenate([conv_w_slabs[k] for k in range(N_DEV)], axis=1)
    P["dt_bias_pad"] = _pad_to(small["dt_bias"], LANES)
    P["a_exp"] = jnp.repeat(-jnp.exp(small["a_log"]), HEAD_DIM, axis=1)
    P["dsk_exp"] = jnp.repeat(small["d_skip"], HEAD_DIM, axis=1)
    return P


def small_grads(gS, small):
    heads = lambda a: a.reshape(SSD_HEADS, HEAD_DIM).sum(axis=1)[None, :]
    out = {k: gS[k] for k in ("norm_mix_pre", "conv_b", "ssd_norm", "b_gate", "norm_mix_post", "norm_ffn_pre",
                              "norm_ffn_post", "norm_ple_post")}
    out["dt_bias"] = gS["dt_bias_pad"][:, :SSD_HEADS]
    out["a_log"] = heads(gS["a_exp"]) * (-jnp.exp(small["a_log"]))
    out["d_skip"] = heads(gS["dsk_exp"])
    return out


def in_grad_slabs(g):
    pieces = ((0, IN_SPLITS[1], g["main"], 0), (IN_SPLITS[1], IN_SPLITS[2], g["dt"], -IN_SPLITS[1]),
              (IN_SPLITS[2], N_DEV * IN_SHARD, g["main"], IN_SPLITS[1] - IN_SPLITS[2]))
    slabs = []
    for k in range(N_DEV):
        a, b = IN_SHARD * k, IN_SHARD * (k + 1)
        cut = [src[:, max(a, lo) + off:min(b, hi) + off] for lo, hi, src, off in pieces if max(a, lo) < min(b, hi)]
        slabs.append(_pad_to(jnp.concatenate(cut, axis=1), IN_PAD))
    width = COL_SHARDED["conv_w"][1]
    return jnp.stack(slabs), jnp.stack([g["conv_w"][:, width * k:width * (k + 1)] for k in range(N_DEV)])


def pack_small(get):
    cols = [_pad_to(get(name).reshape(n), _rows_of(n) * ROW) for name, n in SMALL]
    return jnp.concatenate(cols).reshape(SMALL_ROWS, ROW)


def unpack_small(flat):
    out, r0 = {}, 0
    for name, n in SMALL:
        rows = _rows_of(n)
        out[name] = flat[r0:r0 + rows].reshape(rows * ROW)[:n].reshape(1, n)
        r0 += rows
    return out


N_CHIPS = 4
JOB_SEMS = {"all": 7, "scatter": 7, "gather2": 7, "pair": 4, "chips": 3}


def exchange_plan(jobs):
    n = len(jobs)
    kinds = [k for k, _ in jobs]
    srcs = [s for _, s in jobs]
    shapes = {"all": lambda s: (N_DEV,) + s.shape, "gather2": lambda s: (N_DEV,) + s.shape,
              "scatter": lambda s: s.shape, "pair": lambda s: (N_CHIPS,) + s.shape[1:], "chips": lambda s: s.shape}
    out_shape = [jax.ShapeDtypeStruct(shapes[k](s), s.dtype) for k, s in jobs]
    offs = [sum(JOB_SEMS[k] for k in kinds[:i]) for i in range(n + 1)]

    def run(phases, src, out, send_sems, recv_sems, local_sems):
        x, y, c = lax.axis_index("x"), lax.axis_index("y"), lax.axis_index("c")
        dev = lambda d: 4 * d[0] + 2 * d[1] + d[2]
        chip_no = lambda ch: 2 * ch[0] + ch[1]
        me, sib, my_chip = (x, y, c), (x, y, 1 - c), (x, y)
        others = [(1 - x, y), (x, 1 - y), (1 - x, 1 - y)]
        all_chips = [(0, 0), (0, 1), (1, 0), (1, 1)]
        peers = [(1 - x if k & 4 else x, 1 - y if k & 2 else y, 1 - c if k & 1 else c) for k in range(1, N_DEV)]
        starts, recvs, local = [], [], []
        chained = [[] for _ in others]

        for i, kind in enumerate(kinds):
            s_ref, o_ref = src[i], out[i]

            def rc(k, src_ref, dst_ref, to, i=i):
                s = offs[i] + k
                return pltpu.make_async_remote_copy(src_ref=src_ref, dst_ref=dst_ref, send_sem=send_sems.at[s],
                                                    recv_sem=recv_sems.at[s], device_id=to,
                                                    device_id_type=pl.DeviceIdType.MESH)

            if kind == "all":
                local.append(pltpu.make_async_copy(s_ref, o_ref.at[dev(me)], local_sems.at[i]))
                for k, peer in enumerate(peers):
                    starts.append(rc(k, s_ref, o_ref.at[dev(me)], peer))
                    recvs.append(rc(k, s_ref, o_ref.at[dev(peer)], peer))
            elif kind == "scatter":
                local.append(pltpu.make_async_copy(s_ref.at[dev(me)], o_ref.at[dev(me)], local_sems.at[i]))
                for k, peer in enumerate(peers):
                    starts.append(rc(k, s_ref.at[dev(peer)], o_ref.at[dev(me)], peer))
                    recvs.append(rc(k, s_ref.at[dev(me)], o_ref.at[dev(peer)], peer))
            elif kind == "gather2":
                local.append(pltpu.make_async_copy(s_ref, o_ref.at[dev(me)], local_sems.at[i]))
                starts.append(rc(0, s_ref, o_ref.at[dev(me)], sib))
                recvs.append(rc(0, s_ref, o_ref.at[dev(sib)], sib))
                for j, ch in enumerate(others):
                    same, other = (*ch, c), (*ch, 1 - c)
                    starts.append(rc(1 + j, s_ref, o_ref.at[dev(me)], same))
                    chained[j].append((rc(1 + j, s_ref, o_ref.at[dev(same)], same),
                                       rc(4 + j, o_ref.at[dev(same)], o_ref.at[dev(same)], sib)))
                    recvs.append(rc(4 + j, s_ref, o_ref.at[dev(other)], sib))
            elif kind == "pair":
                for j, ch in enumerate(all_chips):
                    starts.append(rc(j, s_ref.at[dev((*ch, 1 - c))], o_ref.at[j], sib))
                    recvs.append(rc(j, s_ref.at[dev((*ch, c))], o_ref.at[j], sib))
            else:
                mine = chip_no(my_chip)
                local.append(pltpu.make_async_copy(s_ref.at[mine], o_ref.at[mine], local_sems.at[i]))
                for j, ch in enumerate(others):
                    starts.append(rc(j, s_ref.at[chip_no(ch)], o_ref.at[mine], (*ch, c)))
                    recvs.append(rc(j, s_ref.at[mine], o_ref.at[chip_no(ch)], (*ch, c)))

        if "start" in phases:
            for cp in local + starts:
                cp.start()
        if "pass" in phases:
            for group in chained:
                for arrival, forward in group:
                    arrival.wait_recv()
                    forward.start()
        if "finish" in phases:
            for cp in recvs:
                cp.wait_recv()
            for cp in starts + [forward for group in chained for _, forward in group]:
                cp.wait_send()
            for cp in local:
                cp.wait()

    scratch = [pltpu.SemaphoreType.DMA((offs[n],)), pltpu.SemaphoreType.DMA((offs[n],)),
               pltpu.SemaphoreType.DMA((n,))]
    return srcs, out_shape, scratch, run


EXCHANGE_PHASES = ("start", "pass", "finish")


def exchange(name, jobs):
    srcs, out_shape, scratch, run = exchange_plan(jobs)
    n = len(srcs)

    def body(*refs):
        run(EXCHANGE_PHASES, refs[:n], refs[n:2 * n], *refs[2 * n:])

    any_spec = pl.BlockSpec(memory_space=pl.ANY)
    return pl.pallas_call(body, name=name, out_shape=out_shape, in_specs=[any_spec] * n, out_specs=[any_spec] * n,
                          scratch_shapes=scratch)(*srcs)


def host_exchange(body, n_in, n_out, grid, plan):
    if plan is None:
        return body, [], [], [], [], []
    srcs, out_shape, scratch, run = plan
    n = len(srcs)
    steps = 1
    for g in grid:
        steps *= g

    def hosted(*refs):
        ins, side_in = refs[:n_in], refs[n_in:n_in + n]
        outs, side_out = refs[n_in + n:n_in + n + n_out], refs[n_in + n + n_out:n_in + 2 * n + n_out]
        rest = refs[n_in + 2 * n + n_out:]
        own, sems = rest[:len(rest) - 3], rest[len(rest) - 3:]
        step = 0
        for d, g in enumerate(grid):
            step = step * g + pl.program_id(d)

        def at(when, phase):
            @pl.when(step == when)
            def _():
                run((phase,), side_in, side_out, *sems)

        at(0, "start")
        body(*ins, *outs, *own)
        at(steps * 13 // 16, "pass")
        at(steps - 1, "finish")

    any_spec = pl.BlockSpec(memory_space=pl.ANY)
    return hosted, [any_spec] * n, list(srcs), list(out_shape), [any_spec] * n, list(scratch)


def pair_sum(name, g, sib, core):
    _, r, c = g.shape
    tr = _pick(r, (256, 128, 64, 32, 16, 8))

    def body(core_ref, g_ref, s_ref, o_ref):
        o_ref[...] = (g_ref[...] + s_ref[...]).astype(o_ref.dtype)

    return pl.pallas_call(
        body,
        name=name,
        out_shape=jax.ShapeDtypeStruct((N_CHIPS, r, c), MXU),
        grid_spec=pltpu.PrefetchScalarGridSpec(
            num_scalar_prefetch=1,
            grid=(N_CHIPS, r // tr),
            in_specs=[
                pl.BlockSpec((None, None, tr, c), lambda j, i, core_ref: (j, core_ref[0], i, 0)),
                pl.BlockSpec((None, tr, c), lambda j, i, core_ref: (j, i, 0)),
            ],
            out_specs=pl.BlockSpec((None, tr, c), lambda j, i, core_ref: (j, i, 0)),
        ),
        compiler_params=_cparams(("parallel", "parallel")),
    )(core, g.reshape(N_CHIPS, 2, r, c), sib)


def adamw(name, parts, w, m, v):
    rows, cols = w.shape
    tr = _pick(rows, (128, 64, 32, 16, 8))

    nparts = parts.shape[0]

    def body(p_ref, w_ref, m_ref, v_ref, g_ref, d_ref, m2_ref, v2_ref):
        g = p_ref[0].astype(F32)
        for k in range(1, nparts):
            g = g + p_ref[k].astype(F32)
        m2 = ADAM_B1 * m_ref[...] + (1.0 - ADAM_B1) * g
        v2 = ADAM_B2 * v_ref[...] + (1.0 - ADAM_B2) * jnp.square(g)
        m_hat = m2 / (1.0 - ADAM_B1 ** ADAM_STEP)
        v_hat = v2 / (1.0 - ADAM_B2 ** ADAM_STEP)
        g_ref[...] = g
        d_ref[...] = -ADAM_LR * (m_hat / (jnp.sqrt(v_hat) + ADAM_EPS) + ADAM_WD * w_ref[...])
        m2_ref[...] = m2
        v2_ref[...] = v2

    spec = pl.BlockSpec((tr, cols), lambda i: (i, 0))
    return pl.pallas_call(
        body,
        name=name,
        out_shape=[jax.ShapeDtypeStruct((rows, cols), F32)] * 4,
        grid=(rows // tr,),
        in_specs=[pl.BlockSpec((nparts, tr, cols), lambda i: (0, i, 0)), spec, spec, spec],
        out_specs=[spec] * 4,
        compiler_params=_cparams(("parallel",)),
    )(parts, w, m, v)


WEIGHT_ORDER = (
    "norm_mix_pre", "w_in", "conv_w", "conv_b", "dt_bias", "a_log", "d_skip", "ssd_norm", "w_ssd_branch",
    "w_sb_branch", "w_gate", "b_gate", "w_out", "norm_mix_post", "norm_ffn_pre", "w_ff1", "w_ff2", "norm_ffn_post",
    "w_ple", "w_ple_gate", "norm_ple_post",
)


def kernel(x, p, norm_mix_pre, w_in, conv_w, conv_b, dt_bias, a_log, d_skip, ssd_norm, w_ssd_branch, w_sb_branch, w_gate, b_gate, w_out, norm_mix_post, norm_ffn_pre, w_ff1, w_ff2, norm_ffn_post, w_ple, w_ple_gate, norm_ple_post, loss_target, m_norm_mix_pre, m_w_in, m_conv_w, m_conv_b, m_dt_bias, m_a_log, m_d_skip, m_ssd_norm, m_w_ssd_branch, m_w_sb_branch, m_w_gate, m_b_gate, m_w_out, m_norm_mix_post, m_norm_ffn_pre, m_w_ff1, m_w_ff2, m_norm_ffn_post, m_w_ple, m_w_ple_gate, m_norm_ple_post, v_norm_mix_pre, v_w_in, v_conv_w, v_conv_b, v_dt_bias, v_a_log, v_d_skip, v_ssd_norm, v_w_ssd_branch, v_w_sb_branch, v_w_gate, v_b_gate, v_w_out, v_norm_mix_post, v_norm_ffn_pre, v_w_ff1, v_w_ff2, v_norm_ffn_post, v_w_ple, v_w_ple_gate, v_norm_ple_post):
    a = dict(locals())
    seq = x.shape[1]
    x2 = x.reshape(seq, D_MODEL)
    target = loss_target.reshape(seq, D_MODEL)
    pb = p.reshape(seq, PLE_DIM).astype(MXU)

    def shard(prefix, name, dtype):
        v = a[prefix + name][0].astype(dtype)
        return _pad_to(v, IN_PAD) if name == "w_in" else v

    w_in_slabs, w_gate_slabs, conv_w_slabs = exchange(
        "gather_weights", [("gather2", shard("", "w_in", MXU)), ("gather2", shard("", "w_gate", MXU)),
                           ("all", a["conv_w"][0])])
    small = {n: a[n] for n, _ in SMALL}
    core = lax.axis_index("c").astype(jnp.int32).reshape(1)

    lossc, grad_x, parts, conv_slabs, g_acc = local_step(
        x2, pb, target, in_matrices(w_in_slabs, w_gate_slabs), build_small(small, conv_w_slabs),
        {n: shard("", n, MXU) for n in LATE}, core)
    loss = lax.psum(jnp.sum(lossc), ("x", "y", "c"))
    g_small = small_grads(g_acc, small)

    parts["conv_w"], small_parts = exchange(
        "small_grads", [("scatter", conv_slabs), ("all", pack_small(lambda n: g_small[n]))])

    leaves = {}
    for n, part in parts.items():
        res = adamw("adamw_" + n, part, shard("", n, F32), shard("m_", n, F32), shard("v_", n, F32))
        leaves[n] = [r[None, :, :IN_SHARD] if n == "w_in" else r[None] for r in res]
    res = adamw("adamw_small", small_parts, pack_small(lambda n: a[n]), pack_small(lambda n: a["m_" + n]),
                pack_small(lambda n: a["v_" + n]))
    for j, r in enumerate(res):
        for n, leaf in unpack_small(r).items():
            leaves.setdefault(n, [None] * 4)[j] = leaf
    outs = [loss, grad_x.reshape(x.shape)]
    for j in range(4):
        outs += [leaves[n][j] for n in WEIGHT_ORDER]
    return tuple(outs)
```

```python
import functools

import jax
import jax.numpy as jnp
from jax import lax
from jax.experimental import pallas as pl
from jax.experimental.pallas import tpu as pltpu

F32 = jnp.float32
MXU = jnp.bfloat16
VMEM_LIMIT = 56 * 1024 * 1024

D_MODEL = 1024
D_INNER = 2048
SSD_HEADS = 32
HEAD_DIM = 64
SSD_GROUPS = 8
D_STATE = 128
CONV_K = 4
CONV_DIM = 4096
CHUNK = 128
SB_WIDTH = 1024
D_FF = 4096
PLE_DIM = 256
RMS_EPS = 1e-6
SB_SCALE = HEAD_DIM ** -0.5
N_DEV = 8
LANES = 128

OFF_Z, OFF_XBC, OFF_Q, OFF_K, OFF_V = 0, 2048, 6144, 7168, 8192

ADAM_LR = 0.001
ADAM_B1 = 0.9
ADAM_B2 = 0.999
ADAM_EPS = 1e-08
ADAM_WD = 0.01
ADAM_STEP = 10


def _sig(x):
    return 0.5 * jnp.tanh(0.5 * x) + 0.5


def _softplus(x):
    return jnp.maximum(x, 0.0) + jnp.log(1.0 + jnp.exp(-jnp.abs(x)))


def _rms(x, w):
    return x * lax.rsqrt(jnp.mean(x * x, axis=-1, keepdims=True) + RMS_EPS) * w


def _dot(a, b):
    return jnp.dot(a, b, preferred_element_type=F32)


def _dot_nt(a, b):
    return lax.dot_general(a, b, (((1,), (1,)), ((), ())), preferred_element_type=F32)


def _dot_tn(a, b):
    return lax.dot_general(a, b, (((0,), (0,)), ((), ())), preferred_element_type=F32)


def _split3(x):
    x1 = x.astype(MXU)
    r = x - x1.astype(F32)
    x2 = r.astype(MXU)
    r = r - x2.astype(F32)
    return x1, x2, r.astype(MXU)


def _dot3_l(a, u):
    m = a.shape[0]
    d = _dot(jnp.concatenate(_split3(a), axis=0), u)
    return (d[2 * m:] + d[m:2 * m]) + d[:m]


def _dot3_r(u, a):
    n = a.shape[1]
    d = _dot(u, jnp.concatenate(_split3(a), axis=1))
    return (d[:, 2 * n:] + d[:, n:2 * n]) + d[:, :n]


def _iota(shape, dim):
    return lax.broadcasted_iota(jnp.int32, shape, dim)


def _tri(n, cmp):
    r, c = _iota((n, n), 0), _iota((n, n), 1)
    return cmp(r, c).astype(F32).astype(MXU)


def _cparams(sem):
    return pltpu.CompilerParams(dimension_semantics=sem, vmem_limit_bytes=VMEM_LIMIT)


def _pick(n, cands):
    for c in cands:
        if n % c == 0:
            return c
    return n


def mm(name, a, b, mode, add=None, out_dtype=F32, b_slabs=False, out_slabs=False, epi=None, epi_args=(), extra=(),
       side=None):
    slab = None
    if b_slabs:
        slab = b.shape[2]
        bshape = (b.shape[1], N_DEV * slab)
    else:
        bshape = b.shape
    if mode == "nn":
        (M, K), (K2, N) = a.shape, bshape
    elif mode == "nt":
        (M, K), (N, K2) = a.shape, bshape
    else:
        (K, M), (K2, N) = a.shape, bshape
    assert K == K2, (name, a.shape, b.shape)
    tm = _pick(M, (1024, 512, 256, 128))
    tn = _pick(N, (1024, 512, 256, 128))
    tk = _pick(K, (1024, 512, 256, 128))
    if b_slabs and mode == "nn":
        tn = slab
    if b_slabs and mode == "nt":
        tk = slab
    if out_slabs:
        assert mode == "tn" and N % N_DEV == 0
        tn = N // N_DEV
    nk = K // tk

    def body(*refs):
        refs = list(refs)
        a_ref, b_ref = refs[:2]
        add_ref = refs[2] if add is not None else None
        n_in = 2 + (add is not None)
        epi_refs = refs[n_in:n_in + len(epi_args)]
        o_ref = refs[n_in + len(epi_args)]
        extra_refs = refs[n_in + len(epi_args) + 1:-1]
        acc = refs[-1]
        k = pl.program_id(2)

        @pl.when(k == 0)
        def _():
            acc[...] = jnp.zeros_like(acc) if add is None else add_ref[...]

        av, bv = a_ref[...], b_ref[...]
        if mode == "nn":
            acc[...] += _dot(av, bv)
        elif mode == "nt":
            acc[...] += _dot_nt(av, bv)
        else:
            acc[...] += _dot_tn(av, bv)

        @pl.when(k == nk - 1)
        def _():
            res = acc[...]
            main = res if epi is None else epi(res, *[r[...] for r in epi_refs])
            o_ref[...] = main.astype(o_ref.dtype)
            for r, (_, fn) in zip(extra_refs, extra):
                r[...] = fn(res).astype(r.dtype)

    if mode == "nn":
        a_spec = pl.BlockSpec((tm, tk), lambda i, j, k: (i, k))
        b_spec = pl.BlockSpec((tk, tn), lambda i, j, k: (k, j))
    elif mode == "nt":
        a_spec = pl.BlockSpec((tm, tk), lambda i, j, k: (i, k))
        b_spec = pl.BlockSpec((tn, tk), lambda i, j, k: (j, k))
    else:
        a_spec = pl.BlockSpec((tk, tm), lambda i, j, k: (k, i))
        b_spec = pl.BlockSpec((tk, tn), lambda i, j, k: (k, j))
    if b_slabs and mode == "nn":
        b_spec = pl.BlockSpec((None, tk, tn), lambda i, j, k: (j, k, 0))
    if b_slabs and mode == "nt":
        b_spec = pl.BlockSpec((None, tn, tk), lambda i, j, k: (k, j, 0))
    o_spec = pl.BlockSpec((tm, tn), lambda i, j, k: (i, j))
    in_specs, args = [a_spec, b_spec], [a, b]
    if add is not None:
        in_specs.append(o_spec)
        args.append(add)
    for e in epi_args:
        in_specs.append(o_spec)
        args.append(e)
    out_sds = jax.ShapeDtypeStruct((M, N), out_dtype)
    if out_slabs:
        o_spec = pl.BlockSpec((None, tm, tn), lambda i, j, k: (j, i, 0))
        out_sds = jax.ShapeDtypeStruct((N_DEV, M, tn), out_dtype)
    grid = (M // tm, N // tn, nk)
    body, s_in, s_args, s_shape, s_out, s_scr = host_exchange(body, len(args), 1 + len(extra), grid, side)
    res = pl.pallas_call(
        body,
        name=name,
        out_shape=[out_sds] + [jax.ShapeDtypeStruct((M, N), dt) for dt, _ in extra] + s_shape,
        grid=grid,
        in_specs=in_specs + s_in,
        out_specs=[o_spec] * (1 + len(extra)) + s_out,
        scratch_shapes=[pltpu.VMEM((tm, tn), F32)] + s_scr,
        compiler_params=_cparams(("parallel", "parallel", "arbitrary") if side is None else ("arbitrary",) * 3),
    )(*args, *s_args)
    return res if extra or side is not None else res[0]


def rowwise(name, fn, rows, bcast, outs, accs=(), tr=512, ncb=1, side=None, groups=1):
    S = rows[0][0].shape[0]
    tr = min(tr, S)
    nrb = S // tr
    G = groups
    in_specs, args, in_w = [], [], []
    for arr, off, w in rows:
        assert off % (w * G) == 0 and arr.shape[0] == S
        in_specs.append(pl.BlockSpec((tr, w * G), lambda j, i, ob=off // (w * G): (i, ob + j)))
        args.append(arr)
        in_w.append(w)
    for arr, off, w in bcast:
        assert off % (w * G) == 0
        in_specs.append(pl.BlockSpec((arr.shape[0], w * G), lambda j, i, ob=off // (w * G): (0, ob + j)))
        args.append(arr)
        in_w.append(w)
    out_shape, out_specs, out_w = [], [], []
    for tw, w, dt in outs:
        out_shape.append(jax.ShapeDtypeStruct((S, tw), dt))
        out_specs.append(pl.BlockSpec((tr, w * G), lambda j, i: (i, j)))
        out_w.append(w)
    for tw, w in accs:
        out_shape.append(jax.ShapeDtypeStruct((1, tw), F32))
        out_specs.append(pl.BlockSpec((1, w * G), lambda j, i: (0, j)))
        out_w.append(w)
    nin, nout = len(args), len(outs)

    def body(*refs):
        i = pl.program_id(1)
        for g in range(G):
            cut = lambda w: slice(g * w, (g + 1) * w)
            res = fn(*[r[:, cut(w)] for r, w in zip(refs[:nin], in_w)])
            for k, (r, v) in enumerate(zip(refs[nin:], res)):
                cols = cut(out_w[k])
                if k < nout:
                    r[:, cols] = v.astype(r.dtype)
                    continue

                @pl.when(i == 0)
                def _(r=r, v=v, cols=cols):
                    r[:, cols] = v

                @pl.when(i > 0)
                def _(r=r, v=v, cols=cols):
                    r[:, cols] += v

    body, s_in, s_args, s_shape, s_out, s_scr = host_exchange(body, nin, len(out_shape), (ncb, nrb), side)
    res = pl.pallas_call(
        body,
        name=name,
        out_shape=out_shape + s_shape,
        grid=(ncb, nrb),
        in_specs=in_specs + s_in,
        out_specs=out_specs + s_out,
        scratch_shapes=s_scr,
        compiler_params=_cparams(("parallel" if side is None else "arbitrary", "arbitrary")),
    )(*args, *s_args)
    return res


CONV_TC = 128


CONV_R = 128
HALO = 8


def _conv_pre(e, w, b):
    shifted = [pltpu.roll(e, s, 0) for s in (1, 2, 3)]
    pre = b + w[3:4, :] * e
    for s in (1, 2, 3):
        pre = pre + w[3 - s:4 - s, :] * shifted[s - 1]
    return pre, shifted


def conv_fwd(projmain, conv_w, conv_b):
    S = projmain.shape[0]
    tc = CONV_TC

    def body(u_ref, w_ref, b_ref, o_ref):
        u, w = u_ref[...], w_ref[...]
        row = _iota(u.shape, 0)
        pre = b_ref[...] + w[3:4, :] * u
        for s in (1, 2, 3):
            pre = pre + w[3 - s:4 - s, :] * jnp.where(row >= s, pltpu.roll(u, s, 0), 0.0)
        o_ref[...] = pre * _sig(pre)

    return pl.pallas_call(
        body,
        name="conv_fwd",
        out_shape=jax.ShapeDtypeStruct((S, CONV_DIM), F32),
        grid=(CONV_DIM // tc,),
        in_specs=[
            pl.BlockSpec((S, tc), lambda j: (0, OFF_XBC // tc + j)),
            pl.BlockSpec((CONV_K, tc), lambda j: (0, j)),
            pl.BlockSpec((1, tc), lambda j: (0, j)),
        ],
        out_specs=pl.BlockSpec((S, tc), lambda j: (0, j)),
        compiler_params=_cparams(("parallel",)),
    )(projmain, conv_w, conv_b)


def conv_bwd(dxs, dxs_skip, d_b, d_c, projmain, conv_w, conv_b):
    S = projmain.shape[0]
    tc = CONV_TC
    n_xs, n_b = D_INNER // tc, SSD_GROUPS * D_STATE // tc

    def body(dx_ref, dskip_ref, dbm_ref, dcm_ref, u_ref, w_ref, b_ref, du_ref, dw_ref, db_ref, u_pad, d_pad):
        j = pl.program_id(0)
        zeros = jnp.zeros((HALO, tc), F32)
        for pad in (u_pad, d_pad):
            pad[0:HALO, :] = zeros
            pad[HALO + S:2 * HALO + S, :] = zeros
        u_pad[HALO:HALO + S, :] = u_ref[...]
        d_pad[HALO:HALO + S, :] = jnp.where(j < n_xs, dx_ref[...] + dskip_ref[...],
                                            jnp.where(j < n_xs + n_b, dbm_ref[...], dcm_ref[...]))
        w, b = w_ref[...], b_ref[...]
        n = CONV_R + 2 * HALO
        keep = slice(HALO, HALO + CONV_R)

        def chunk(c, sums):
            r0 = pl.multiple_of(c * CONV_R, CONV_R)
            e = u_pad[pl.ds(r0, n), :]
            pre, shifted = _conv_pre(e, w, b)
            sg = _sig(pre)
            dpre = d_pad[pl.ds(r0, n), :] * (sg * (1.0 + pre * (1.0 - sg)))
            du = w[3:4, :] * dpre
            for s in (1, 2, 3):
                du = du + w[3 - s:4 - s, :] * pltpu.roll(dpre, n - s, 0)
            du_ref[pl.ds(r0, CONV_R), :] = du[keep].astype(du_ref.dtype)
            dk = dpre[keep]
            taps = [shifted[2], shifted[1], shifted[0], e]
            return tuple(acc + jnp.sum(dk * t[keep], axis=0, keepdims=True) for acc, t in zip(sums[:4], taps)) + (
                sums[4] + jnp.sum(dk, axis=0, keepdims=True),)

        zero = jnp.zeros((1, tc), F32)
        sums = lax.fori_loop(0, S // CONV_R, chunk, (zero,) * 5)
        for k in range(CONV_K):
            dw_ref[k:k + 1, :] = sums[k]
        db_ref[...] = sums[4]

    return pl.pallas_call(
        body,
        name="conv_bwd",
        out_shape=[
            jax.ShapeDtypeStruct((S, CONV_DIM), MXU),
            jax.ShapeDtypeStruct((CONV_K, CONV_DIM), F32),
            jax.ShapeDtypeStruct((1, CONV_DIM), F32),
        ],
        grid=(CONV_DIM // tc,),
        in_specs=[
            pl.BlockSpec((S, tc), lambda j: (0, jnp.minimum(j, n_xs - 1))),
            pl.BlockSpec((S, tc), lambda j: (0, jnp.minimum(j, n_xs - 1))),
            pl.BlockSpec((S, tc), lambda j: (0, jnp.clip(j - n_xs, 0, n_b - 1))),
            pl.BlockSpec((S, tc), lambda j: (0, jnp.clip(j - n_xs - n_b, 0, n_b - 1))),
            pl.BlockSpec((S, tc), lambda j: (0, OFF_XBC // tc + j)),
            pl.BlockSpec((CONV_K, tc), lambda j: (0, j)),
            pl.BlockSpec((1, tc), lambda j: (0, j)),
        ],
        out_specs=[
            pl.BlockSpec((S, tc), lambda j: (0, j)),
            pl.BlockSpec((CONV_K, tc), lambda j: (0, j)),
            pl.BlockSpec((1, tc), lambda j: (0, j)),
        ],
        scratch_shapes=[pltpu.VMEM((S + 2 * HALO, tc), F32)] * 2,
        compiler_params=_cparams(("arbitrary",)),
    )(dxs, dxs_skip, d_b, d_c, projmain, conv_w, conv_b)


def _head_expand():
    r, j = _iota((LANES, D_INNER), 0), _iota((LANES, D_INNER), 1)
    return ((j >= r * HEAD_DIM) & (j < r * HEAD_DIM + HEAD_DIM)).astype(F32).astype(MXU)


def _head_reduce():
    j, r = _iota((D_INNER, LANES), 0), _iota((D_INNER, LANES), 1)
    return ((j >= r * HEAD_DIM) & (j < r * HEAD_DIM + HEAD_DIM)).astype(F32).astype(MXU)


def ssd_prep(dtraw, dt_bias_pad, a_exp):
    S = dtraw.shape[0]

    def body(dtr_ref, bias_ref, a_ref, dte_ref, cse_ref):
        dt = _softplus(dtr_ref[...] + bias_ref[...])
        dte = _dot3_l(dt, _head_expand())
        dte_ref[...] = dte
        incl = _tri(CHUNK, lambda r, c: r >= c)
        cse_ref[...] = _dot3_r(incl, dte * a_ref[...])

    return pl.pallas_call(
        body,
        name="ssd_prep",
        out_shape=[jax.ShapeDtypeStruct((S, D_INNER), F32)] * 2,
        grid=(S // CHUNK,),
        in_specs=[
            pl.BlockSpec((CHUNK, LANES), lambda c: (c, 0)),
            pl.BlockSpec((1, LANES), lambda c: (0, 0)),
            pl.BlockSpec((1, D_INNER), lambda c: (0, 0)),
        ],
        out_specs=[pl.BlockSpec((CHUNK, D_INNER), lambda c: (c, 0))] * 2,
        compiler_params=_cparams(("parallel",)),
    )(dtraw, dt_bias_pad, a_exp)


GW = 4 * HEAD_DIM
SSD_GPS = 8
SSD_GPS_BWD = 8


def ssd_fwd(xbc_act, dte, cse, side=None):
    S = xbc_act.shape[0]
    nc = S // CHUNK

    def body(xs_ref, b_ref, c_ref, dte_ref, cse_ref, y_ref, st_ref, s_scr):
        c = pl.program_id(1)

        @pl.when(c == 0)
        def _():
            s_scr[...] = jnp.zeros_like(s_scr)

        for gg in range(SSD_GPS):
            ch, st = slice(GW * gg, GW * (gg + 1)), slice(D_STATE * gg, D_STATE * (gg + 1))
            s_in = s_scr[:, ch]
            st_ref[0, :, ch] = s_in
            cs = cse_ref[:, ch]
            xd = xs_ref[:, ch] * dte_ref[:, ch]
            btb = b_ref[:, st].T.astype(MXU)
            cs_last = cs[CHUNK - 1:CHUNK, :]
            gy = _dot(c_ref[:, st].astype(MXU), jnp.concatenate([btb, s_in.astype(MXU)], axis=1))
            g, y = gy[:, :CHUNK], gy[:, CHUNK:] * jnp.exp(cs)
            cs_t = cs.T
            row, col = _iota((CHUNK, CHUNK), 0), _iota((CHUNK, CHUNK), 1)
            ms = []
            for h in range(4):
                lo = HEAD_DIM * h
                lam = jnp.where(row >= col, jnp.exp(cs[:, lo:lo + 1] - cs_t[lo:lo + 1, :]), 0.0)
                ms.append((g * lam).astype(MXU))
            yd = _dot(jnp.concatenate(ms, axis=0), xd.astype(MXU))
            lane = _iota((CHUNK, GW), 1)
            for h in range(4):
                y = y + jnp.where((lane >= HEAD_DIM * h) & (lane < HEAD_DIM * (h + 1)),
                                  yd[CHUNK * h:CHUNK * (h + 1)], 0.0)
            y_ref[:, ch] = y
            w = (xd * jnp.exp(cs_last - cs)).astype(MXU)
            s_scr[:, ch] = jnp.exp(cs_last) * s_in + _dot(btb, w)

    grid = (SSD_GROUPS // SSD_GPS, nc)
    gw, gs = GW * SSD_GPS, D_STATE * SSD_GPS
    body, s_in, s_args, s_shape, s_out, s_scr = host_exchange(body, 5, 2, grid, side)
    return pl.pallas_call(
        body,
        name="ssd_fwd",
        out_shape=[
            jax.ShapeDtypeStruct((S, D_INNER), F32),
            jax.ShapeDtypeStruct((nc, D_STATE, D_INNER), F32),
        ] + s_shape,
        grid=grid,
        in_specs=[
            pl.BlockSpec((CHUNK, gw), lambda g, c: (c, g)),
            pl.BlockSpec((CHUNK, gs), lambda g, c: (c, D_INNER // gs + g)),
            pl.BlockSpec((CHUNK, gs), lambda g, c: (c, (D_INNER + SSD_GROUPS * D_STATE) // gs + g)),
            pl.BlockSpec((CHUNK, gw), lambda g, c: (c, g)),
            pl.BlockSpec((CHUNK, gw), lambda g, c: (c, g)),
        ] + s_in,
        out_specs=[
            pl.BlockSpec((CHUNK, gw), lambda g, c: (c, g)),
            pl.BlockSpec((1, D_STATE, gw), lambda g, c: (c, 0, g)),
        ] + s_out,
        scratch_shapes=[pltpu.VMEM((D_STATE, gw), F32)] + s_scr,
        compiler_params=_cparams(("parallel" if side is None else "arbitrary", "arbitrary")),
    )(xbc_act, xbc_act, xbc_act, dte, cse, *s_args)


def ssd_bwd(dy, xbc_act, dte, cse, states, a_exp, side=None):
    S = xbc_act.shape[0]
    nc = S // CHUNK

    def body(*refs):
        for gg in range(SSD_GPS_BWD):
            ch, st = slice(GW * gg, GW * (gg + 1)), slice(D_STATE * gg, D_STATE * (gg + 1))
            cut = {GW * SSD_GPS_BWD: ch, D_STATE * SSD_GPS_BWD: st}
            one_group(*[r.at[(slice(None),) * (len(r.shape) - 1) + (cut[r.shape[-1]],)] for r in refs])

    def one_group(dy_ref, xs_ref, b_ref, c_ref, dte_ref, cse_ref, sin_ref, sout_ref, a_ref,
                  dxs_ref, db_ref, dc_ref, ddt_ref, dal_ref, ds_scr):
        j = pl.program_id(1)

        @pl.when(j == 0)
        def _():
            ds_scr[...] = jnp.zeros_like(ds_scr)
            dal_ref[...] = jnp.zeros_like(dal_ref)

        ds_out = ds_scr[...]
        dyv, xs = dy_ref[...], xs_ref[...]
        dt, cs = dte_ref[...], cse_ref[...]
        s_in = sin_ref[0]
        bm, cm = b_ref[...], c_ref[...]
        bb, cb = bm.astype(MXU), cm.astype(MXU)
        btb, ctb = bm.T.astype(MXU), cm.T.astype(MXU)
        dsb, sib = ds_out.astype(MXU), s_in.astype(MXU)
        xd = xs * dt
        ecs = jnp.exp(cs)
        cs_last = cs[CHUNK - 1:CHUNK, :]
        eend = jnp.exp(cs_last - cs)
        gy = _dot(cb, jnp.concatenate([btb, sib], axis=1))
        g, yoff = gy[:, :CHUNK], gy[:, CHUNK:] * ecs
        gd = _dot(bb, jnp.concatenate([ctb, dsb], axis=1))
        g_t, dxd_off = gd[:, :CHUNK], gd[:, CHUNK:] * eend
        cs_t = cs.T
        row, col = _iota((CHUNK, CHUNK), 0), _iota((CHUNK, CHUNK), 1)
        lane = _iota((CHUNK, GW), 1)
        heads = [(lane >= HEAD_DIM * h) & (lane < HEAD_DIM * (h + 1)) for h in range(4)]
        dyb, xdb = dyv.astype(MXU), xd.astype(MXU)
        dm_all = _dot_nt(jnp.concatenate([jnp.where(hm, dyv, 0.0) for hm in heads], axis=0).astype(MXU), xdb)
        dmt_all = _dot_nt(jnp.concatenate([jnp.where(hm, xd, 0.0) for hm in heads], axis=0).astype(MXU), dyb)
        lams, m_ts = [], []
        for h in range(4):
            lo = HEAD_DIM * h
            cs_col, cs_row = cs[:, lo:lo + 1], cs_t[lo:lo + 1, :]
            lams.append(jnp.where(row >= col, jnp.exp(cs_col - cs_row), 0.0))
            m_ts.append(g_t * jnp.where(col >= row, jnp.exp(cs_row - cs_col), 0.0))
        acc_all = _dot(jnp.concatenate(m_ts, axis=0).astype(MXU), dyb)
        dxd = dxd_off
        dg = jnp.zeros((CHUNK, CHUNK), F32)
        dcs = dyv * yoff - xd * dxd_off
        for h in range(4):
            blk = slice(CHUNK * h, CHUNK * (h + 1))
            dm, dm_t = dm_all[blk], dmt_all[blk]
            dxd = dxd + jnp.where(heads[h], acc_all[blk], 0.0)
            dg = dg + dm * lams[h]
            wdiff = (jnp.sum(dm * (g * lams[h]), axis=1, keepdims=True)
                     - jnp.sum(dm_t * m_ts[h], axis=1, keepdims=True))
            dcs = dcs + jnp.where(lane == HEAD_DIM * h, wdiff, 0.0)
        dye = (dyv * ecs).astype(MXU)
        dc_ref[...] = _dot(dg.astype(MXU), bb) + _dot_nt(dye, sib)
        db_ref[...] = _dot(dg.T.astype(MXU), cb) + _dot_nt((xd * eend).astype(MXU), dsb)
        ds_scr[...] = jnp.exp(cs_last) * ds_out + _dot(ctb, dye)
        last = jnp.sum(ds_out * sout_ref[0], axis=0, keepdims=True)
        rows = _iota((CHUNK, GW), 0)
        dcs = dcs + jnp.where(rows == CHUNK - 1, last, 0.0)
        dda = _dot3_r(_tri(CHUNK, lambda r, c: c >= r), dcs)
        ddt_ref[...] = a_ref[...] * dda + dxd * xs
        dal_ref[...] += jnp.sum(dt * dda, axis=0, keepdims=True)
        dxs_ref[...] = dxd * dt

    rc = lambda g, j: (nc - 1 - j, g)
    grid = (SSD_GROUPS // SSD_GPS_BWD, nc)
    gw, gs = GW * SSD_GPS_BWD, D_STATE * SSD_GPS_BWD
    body, s_in, s_args, s_shape, s_out, s_scr = host_exchange(body, 9, 5, grid, side)
    return pl.pallas_call(
        body,
        name="ssd_bwd",
        out_shape=[
            jax.ShapeDtypeStruct((S, D_INNER), F32),
            jax.ShapeDtypeStruct((S, SSD_GROUPS * D_STATE), F32),
            jax.ShapeDtypeStruct((S, SSD_GROUPS * D_STATE), F32),
            jax.ShapeDtypeStruct((S, D_INNER), F32),
            jax.ShapeDtypeStruct((1, D_INNER), F32),
        ] + s_shape,
        grid=grid,
        in_specs=[
            pl.BlockSpec((CHUNK, gw), rc),
            pl.BlockSpec((CHUNK, gw), rc),
            pl.BlockSpec((CHUNK, gs), lambda g, j: (nc - 1 - j, D_INNER // gs + g)),
            pl.BlockSpec((CHUNK, gs), lambda g, j: (nc - 1 - j, (D_INNER + SSD_GROUPS * D_STATE) // gs + g)),
            pl.BlockSpec((CHUNK, gw), rc),
            pl.BlockSpec((CHUNK, gw), rc),
            pl.BlockSpec((1, D_STATE, gw), lambda g, j: (nc - 1 - j, 0, g)),
            pl.BlockSpec((1, D_STATE, gw), lambda g, j: (jnp.minimum(nc - j, nc - 1), 0, g)),
            pl.BlockSpec((1, gw), lambda g, j: (0, g)),
        ] + s_in,
        out_specs=[
            pl.BlockSpec((CHUNK, gw), rc),
            pl.BlockSpec((CHUNK, gs), rc),
            pl.BlockSpec((CHUNK, gs), rc),
            pl.BlockSpec((CHUNK, gw), rc),
            pl.BlockSpec((1, gw), lambda g, j: (0, g)),
        ] + s_out,
        scratch_shapes=[pltpu.VMEM((D_STATE, gw), F32)] + s_scr,
        compiler_params=_cparams(("parallel" if side is None else "arbitrary", "arbitrary")),
    )(dy, xbc_act, xbc_act, xbc_act, dte, cse, states, states, a_exp, *s_args)


SB_T = 256
SB_DROP = 104.0


def _sb_scores(qm, k_ref, ks, rowi, coli, diag):
    kblk = k_ref[pl.ds(ks, SB_T), :].astype(MXU)
    z = _dot_nt(qm, kblk) * SB_SCALE
    sp = _softplus(z)
    if not diag:
        return kblk, z, None, sp, sp
    mask = (ks + coli) < rowi
    return kblk, z, mask, sp, jnp.where(mask, sp, 0.0)


def _sb_stack(v):
    lane = _iota(v.shape, 1)
    return jnp.concatenate([jnp.where(lane < HEAD_DIM, v, 0.0), jnp.where(lane >= HEAD_DIM, v, 0.0)], axis=0)


def _sb_unstack(v):
    lane = _iota((SB_T, LANES), 1)
    return jnp.where(lane < HEAD_DIM, v[:SB_T], v[SB_T:])


def _sb_rows(qb):
    r = _iota((2 * SB_T, SB_T), 0)
    return qb * SB_T + jnp.where(r >= SB_T, r - SB_T, r), _iota((2 * SB_T, SB_T), 1)


def sb_fwd(projmain, side=None):
    S = projmain.shape[0]
    nq = S // SB_T

    def body(q_ref, k_ref, v_ref, o_ref, t_ref, n_ref):
        hp, qb = pl.program_id(0), pl.program_id(1)
        qst = _sb_stack(q_ref[...]).astype(MXU)
        rowi, coli = _sb_rows(qb)
        u_after = _tri(SB_T, lambda r, c: r > c)

        def cond(carry):
            i, rmin, _, _ = carry
            return (i <= qb) & (rmin < SB_DROP)

        def kstep(carry, diag=False):
            i, _, r, acc = carry
            ks = pl.multiple_of((qb - i) * SB_T, SB_T)
            _, z, mask, sp, spm = _sb_scores(qst, k_ref, ks, rowi, coli, diag)
            vblk = v_ref[pl.ds(ks, SB_T), :].astype(MXU)
            a = jnp.exp(z - sp - _dot3_l(spm, u_after) - r)
            if diag:
                a = jnp.where(mask, a, 0.0)
            acc = acc + _dot(a.astype(MXU), vblk)
            r = r + jnp.sum(spm, axis=1, keepdims=True)
            return i + 1, jnp.min(r), r, acc

        first = kstep((jnp.int32(0), jnp.float32(0.0), jnp.zeros((2 * SB_T, 1), F32),
                       jnp.zeros((2 * SB_T, LANES), F32)), diag=True)
        n, _, r, acc = lax.while_loop(cond, kstep, first)
        o_ref[...] = _sb_unstack(acc).astype(o_ref.dtype)
        t_ref[...] = _sb_unstack(jnp.broadcast_to(r, (2 * SB_T, LANES)))
        n_ref[hp, qb] = n

    grid = (SB_WIDTH // LANES, nq)
    body, s_in, s_args, s_shape, s_out, s_scr = host_exchange(body, 3, 3, grid, side)
    return pl.pallas_call(
        body,
        name="sb_fwd",
        out_shape=[jax.ShapeDtypeStruct((S, SB_WIDTH), MXU), jax.ShapeDtypeStruct((S, SB_WIDTH), F32),
                   jax.ShapeDtypeStruct((SB_WIDTH // LANES, nq), jnp.int32)] + s_shape,
        grid=grid,
        in_specs=[
            pl.BlockSpec((SB_T, LANES), lambda h, i: (i, OFF_Q // LANES + h)),
            pl.BlockSpec((S, LANES), lambda h, i: (0, OFF_K // LANES + h)),
            pl.BlockSpec((S, LANES), lambda h, i: (0, OFF_V // LANES + h)),
        ] + s_in,
        out_specs=[pl.BlockSpec((SB_T, LANES), lambda h, i: (i, h))] * 2
        + [pl.BlockSpec(memory_space=pltpu.SMEM)] + s_out,
        scratch_shapes=s_scr,
        compiler_params=_cparams(("arbitrary", "arbitrary")),
    )(projmain, projmain, projmain, *s_args)


def sb_bwd(projmain, do, t_exp, nblk, side=None):
    S = projmain.shape[0]
    nq = S // SB_T

    def body(n_ref, q_ref, k_ref, v_ref, do_ref, t_ref, dq_ref, dk_ref, dv_ref):
        hp, qb = pl.program_id(0), pl.program_id(1)

        @pl.when(qb == 0)
        def _():
            dk_ref[...] = jnp.zeros_like(dk_ref)
            dv_ref[...] = jnp.zeros_like(dv_ref)

        qst = _sb_stack(q_ref[...]).astype(MXU)
        dost = _sb_stack(do_ref[...]).astype(MXU)
        tv = t_ref[...]
        lane = _iota((SB_T, LANES), 1)
        tot = jnp.concatenate(
            [jnp.sum(jnp.where(lane == HEAD_DIM * hh, tv, 0.0), axis=1, keepdims=True) for hh in range(2)], axis=0)
        rowi, coli = _sb_rows(qb)
        u_upto = _tri(SB_T, lambda r, c: r <= c)
        u_before = _tri(SB_T, lambda r, c: r < c)
        kept = jnp.clip(n_ref[hp, qb], 1, qb + 1)

        def kstep(kb, carry, diag=False):
            psp, pg, dq = carry
            ks = pl.multiple_of(kb * SB_T, SB_T)
            kblk, z, mask, sp, spm = _sb_scores(qst, k_ref, ks, rowi, coli, diag)
            vblk = v_ref[pl.ds(ks, SB_T), :].astype(MXU)
            after = tot - (psp + _dot3_l(spm, u_upto))
            a = jnp.exp(z - sp - after)
            if diag:
                a = jnp.where(mask, a, 0.0)
            gm = _dot_nt(dost, vblk) * a
            before = pg + _dot3_l(gm, u_before)
            sg = jnp.exp(z - sp)
            dz = (gm * (1.0 - sg) - sg * before) * SB_SCALE
            if diag:
                dz = jnp.where(mask, dz, 0.0)
            dzb = dz.astype(MXU)
            dq = dq + _dot(dzb, kblk)
            dk_ref[pl.ds(ks, SB_T), :] += _dot_tn(dzb, qst)
            dv_ref[pl.ds(ks, SB_T), :] += _dot_tn(a.astype(MXU), dost)
            return (psp + jnp.sum(spm, axis=1, keepdims=True),
                    pg + jnp.sum(gm, axis=1, keepdims=True), dq)

        zero1 = jnp.zeros((2 * SB_T, 1), F32)
        state = lax.fori_loop(qb + 1 - kept, qb, kstep, (zero1, zero1, jnp.zeros((2 * SB_T, LANES), F32)))
        dq = kstep(qb, state, diag=True)[2]
        dq_ref[...] = _sb_unstack(dq).astype(dq_ref.dtype)

    grid = (SB_WIDTH // LANES, nq)
    body, s_in, s_args, s_shape, s_out, s_scr = host_exchange(body, 6, 3, grid, side)
    return pl.pallas_call(
        body,
        name="sb_bwd",
        out_shape=[
            jax.ShapeDtypeStruct((S, SB_WIDTH), MXU),
            jax.ShapeDtypeStruct((S, SB_WIDTH), F32),
            jax.ShapeDtypeStruct((S, SB_WIDTH), F32),
        ] + s_shape,
        grid=grid,
        in_specs=[
            pl.BlockSpec(memory_space=pltpu.SMEM),
            pl.BlockSpec((SB_T, LANES), lambda h, i: (i, OFF_Q // LANES + h)),
            pl.BlockSpec((S, LANES), lambda h, i: (0, OFF_K // LANES + h)),
            pl.BlockSpec((S, LANES), lambda h, i: (0, OFF_V // LANES + h)),
            pl.BlockSpec((SB_T, LANES), lambda h, i: (i, h)),
            pl.BlockSpec((SB_T, LANES), lambda h, i: (i, h)),
        ] + s_in,
        out_specs=[
            pl.BlockSpec((SB_T, LANES), lambda h, i: (i, h)),
            pl.BlockSpec((S, LANES), lambda h, i: (0, h)),
            pl.BlockSpec((S, LANES), lambda h, i: (0, h)),
        ] + s_out,
        scratch_shapes=s_scr,
        compiler_params=_cparams(("arbitrary", "arbitrary")),
    )(nblk, projmain, projmain, projmain, do, t_exp, *s_args)


def local_step(x, pb, target, W, P, late, core):
    D = D_MODEL
    full = lambda a, w=D: (a, 0, w)

    (n1b,) = rowwise("norm_pre", lambda xv, g: (_rms(xv, g),), [full(x)], [full(P["norm_mix_pre"])],
                     [(D, D, MXU)])
    projmain = mm("in_proj", n1b, W["main"], "nn")
    gate_pre = mm("gate_proj", n1b, W["gate"], "nn", b_slabs=True)
    dtraw = mm("dt_proj", n1b, W["dt"], "nn")
    xbc_act = conv_fwd(projmain, P["conv_w"], P["conv_b"])
    dte, cse = ssd_prep(dtraw, P["dt_bias_pad"], P["a_exp"])
    y_scan, states, *got = ssd_fwd(xbc_act, dte, cse,
                                   side=exchange_plan([("gather2", late[n]) for n in LATE[:LATE_SPLIT]]))
    W = {**W, **late_matrices(dict(zip(LATE[:LATE_SPLIT], got)))}

    def f_gate(ysc, xs, z, dsk, nw):
        return _rms((ysc + xs * dsk) * (z * _sig(z)), nw)

    (y_ssd_b,) = rowwise("ssd_gate", lambda *a: (f_gate(*a),),
                         [(y_scan, 0, GW), (xbc_act, 0, GW), (projmain, OFF_Z, GW)],
                         [(P["dsk_exp"], 0, GW), (P["ssd_norm"], 0, GW)], [(D_INNER, GW, MXU)], tr=256,
                         groups=SSD_GROUPS)
    y_sb_b, t_exp, sb_kept, *got = sb_fwd(
        projmain, side=exchange_plan([("gather2", late[n]) for n in LATE[LATE_SPLIT:]]))
    W = {**W, **late_matrices(dict(zip(LATE[LATE_SPLIT:], got)))}
    u1 = mm("ssd_branch", y_ssd_b, W["ssd"], "nn")
    u2 = mm("sb_branch", y_sb_b, W["sb"], "nn")

    def f_merge(a1, a2, g1, g2, b1, b2):
        return (_sig(g1 + b1) * a1 + _sig(g2 + b2) * a2,)

    gate_rows = [(gate_pre, 0, D), (gate_pre, D, D)]
    gate_bias = [(P["b_gate"], 0, D), (P["b_gate"], D, D)]
    (merged_b,) = rowwise("merge", f_merge, [full(u1), full(u2)] + gate_rows, gate_bias, [(D, D, MXU)])
    mo = mm("out_proj", merged_b, W["out"], "nn")

    def f_mid(xv, m, gpost, gffn):
        h1 = xv + _rms(m, gpost)
        return h1, _rms(h1, gffn)

    h1, n2b = rowwise("mix_post", f_mid, [full(x), full(mo)],
                      [full(P["norm_mix_post"]), full(P["norm_ffn_pre"])], [(D, D, F32), (D, D, MXU)])
    a_ff, rb = mm("ff1", n2b, W["ff1"], "nn", b_slabs=True,
                  extra=[(MXU, lambda acc: jnp.square(jnp.maximum(acc, 0.0)))])
    ff = mm("ff2", rb, W["ff2"], "nn")

    def f_ffn_post(h, f, g):
        h2 = h + _rms(f, g)
        return h2, h2

    h2, h2b = rowwise("ffn_post", f_ffn_post, [full(h1), full(ff)], [full(P["norm_ffn_post"])],
                      [(D, D, F32), (D, D, MXU)])
    pgp = mm("ple_gate", h2b, W["pg"], "nn")
    pe = mm("ple_proj", pb, W["ple"], "nn", b_slabs=True)

    def f_ple(h2v, gp, pev, tgt, g):
        f = lambda h, a, b, gg: h + _rms(_sig(a) * b, gg)
        h3, vjp = jax.vjp(f, h2v, gp, pev, g)
        err = h3 - tgt
        dh2, dgp, dpe, dg = vjp(err * (1.0 / D))
        lossc = (0.5 / D) * jnp.sum(err * err, axis=0, keepdims=True)
        return dh2, dgp, dpe, dg, lossc

    dh2a, dpgp_b, dpe_b, g_ple, lossc = rowwise(
        "ple_loss", f_ple, [full(h2), full(pgp), full(pe), full(target)], [full(P["norm_ple_post"])],
        [(D, D, F32), (D, D, MXU), (D, D, MXU)], [(D, D), (D, D)])
    dh2 = mm("d_ple_gate_x", dpgp_b, W["pg"], "nt", add=dh2a)
    gW = {}
    gW["w_ple_gate"] = mm("d_ple_gate_w", h2b, dpgp_b, "tn")
    gW["w_ple"] = mm("d_ple_w", pb, dpe_b, "tn", out_slabs=True)

    def b_ffn_post(d, f, g):
        _, vjp = jax.vjp(_rms, f, g)
        return vjp(d)

    dff_b, g_ffn_post = rowwise("d_ffn_post", b_ffn_post, [full(dh2), full(ff)], [full(P["norm_ffn_post"])],
                                [(D, D, MXU)], [(D, D)])
    da_b = mm("d_ff2_x", dff_b, W["ff2"], "nt", out_dtype=MXU,
              epi=lambda acc, act: acc * (2.0 * jnp.maximum(act, 0.0)), epi_args=[a_ff])
    gW["w_ff2"] = mm("d_ff2_w", rb, dff_b, "tn")
    dn2 = mm("d_ff1_x", da_b, W["ff1"], "nt", b_slabs=True)
    gW["w_ff1"] = mm("d_ff1_w", n2b, da_b, "tn", out_slabs=True)

    def b_mid(d2, dn, h, m, gpost, gffn):
        _, vjp = jax.vjp(_rms, h, gffn)
        dh, dgffn = vjp(dn)
        dh1 = d2 + dh
        _, vjp2 = jax.vjp(_rms, m, gpost)
        dm, dgpost = vjp2(dh1)
        return dh1, dm, dgpost, dgffn

    dh1, dmo_b, g_mix_post, g_ffn_pre = rowwise(
        "d_mix_post", b_mid, [full(dh2), full(dn2), full(h1), full(mo)],
        [full(P["norm_mix_post"]), full(P["norm_ffn_pre"])], [(D, D, F32), (D, D, MXU)], [(D, D), (D, D)])
    dmerged = mm("d_out_x", dmo_b, W["out"], "nt")
    gW["w_out"] = mm("d_out_w", merged_b, dmo_b, "tn")

    def b_merge(d, a1, a2, g1, g2, b1, b2):
        s1, s2 = _sig(g1 + b1), _sig(g2 + b2)
        dg1 = d * a1 * s1 * (1.0 - s1)
        dg2 = d * a2 * s2 * (1.0 - s2)
        dg = jnp.concatenate([dg1, dg2], axis=1)
        return d * s1, d * s2, dg, jnp.sum(dg, axis=0, keepdims=True)

    du1_b, du2_b, dgp_b, g_b_gate = rowwise(
        "d_merge", b_merge, [full(dmerged), full(u1), full(u2)] + gate_rows, gate_bias,
        [(D, D, MXU), (D, D, MXU), (2 * D, 2 * D, MXU)], [(2 * D, 2 * D)])
    dy_ssd = mm("d_ssd_branch_x", du1_b, W["ssd"], "nt")
    dy_sb = mm("d_sb_branch_x", du2_b, W["sb"], "nt")
    gW["w_ssd_branch"] = mm("d_ssd_branch_w", y_ssd_b, du1_b, "tn")
    gW["w_sb_branch"] = mm("d_sb_branch_w", y_sb_b, du2_b, "tn")

    def b_gate(d, ysc, xs, z, dsk, nw):
        _, vjp = jax.vjp(f_gate, ysc, xs, z, dsk, nw)
        return vjp(d)

    slabs = {n: gW[n] if n in COL_SHARDED else gW[n].reshape((N_DEV,) + ROW_SHARDED[n]) for n in LATE}
    dy_scan, dxs_skip, dz_b, g_dsk_exp, g_ssd_norm, *from_sibling = rowwise(
        "d_ssd_gate", b_gate, [(dy_ssd, 0, GW), (y_scan, 0, GW), (xbc_act, 0, GW), (projmain, OFF_Z, GW)],
        [(P["dsk_exp"], 0, GW), (P["ssd_norm"], 0, GW)],
        [(D_INNER, GW, F32), (D_INNER, GW, F32), (D_INNER, GW, MXU)], [(D_INNER, GW), (D_INNER, GW)],
        tr=256, groups=SSD_GROUPS, side=exchange_plan([("pair", slabs[n]) for n in LATE]))
    chip_sums = [pair_sum("pair_sum_" + n, slabs[n], s, core) for n, s in zip(LATE, from_sibling)]
    dxs, d_b, d_c, ddt_part, g_a_exp, *late_parts = ssd_bwd(
        dy_scan, xbc_act, dte, cse, states, P["a_exp"], side=exchange_plan([("chips", s) for s in chip_sums]))

    def b_dt(dpart, dtr, bias):
        ddt = _dot3_l(dpart, _head_reduce())
        d = ddt * _sig(dtr + bias)
        return d, jnp.sum(d, axis=0, keepdims=True)

    ddt_b, g_dt_bias_pad = rowwise("d_dt", b_dt, [(ddt_part, 0, D_INNER), (dtraw, 0, LANES)],
                                   [(P["dt_bias_pad"], 0, LANES)], [(LANES, LANES, MXU)], [(LANES, LANES)])
    dxbc_b, g_conv_w, g_conv_b = conv_bwd(dxs, dxs_skip, d_b, d_c, projmain, P["conv_w"], P["conv_b"])
    dq_b, dk, dv = sb_bwd(projmain, dy_sb, t_exp, sb_kept)
    dmain_b = jnp.concatenate([dz_b, dxbc_b, dq_b, dk.astype(MXU), dv.astype(MXU)], axis=1)
    g_main = mm("d_in_w", n1b, dmain_b, "tn")
    g_dt = mm("d_dt_w", n1b, ddt_b, "tn")
    last = {"w_gate": mm("d_gate_w", n1b, dgp_b, "tn", out_slabs=True)}
    last["w_in"], conv_slabs = in_grad_slabs({"main": g_main, "dt": g_dt, "conv_w": g_conv_w})
    dn1_dt = mm("d_dt_x", ddt_b, W["dt"], "nt")
    dn1_gate, *from_sibling = mm("d_gate_x", dgp_b, W["gate"], "nt", add=dn1_dt, b_slabs=True,
                                 side=exchange_plan([("pair", last[n]) for n in FIRST]))
    chip_sums = [pair_sum("pair_sum_" + n, last[n], s, core) for n, s in zip(FIRST, from_sibling)]
    dn1, *first_parts = mm("d_in_x", dmain_b, W["main"], "nt", add=dn1_gate,
                           side=exchange_plan([("chips", s) for s in chip_sums]))

    def b_pre(d1, dn, xv, g):
        _, vjp = jax.vjp(_rms, xv, g)
        dx, dg = vjp(dn)
        return d1 + dx, dg

    grad_x, g_mix_pre = rowwise("d_norm_pre", b_pre, [full(dh1), full(dn1), full(x)], [full(P["norm_mix_pre"])],
                                [(D, D, F32)], [(D, D)])

    parts = {**dict(zip(LATE, late_parts)), **dict(zip(FIRST, first_parts))}
    gS = {
        "norm_mix_pre": g_mix_pre, "conv_b": g_conv_b, "dt_bias_pad": g_dt_bias_pad, "a_exp": g_a_exp,
        "dsk_exp": g_dsk_exp, "ssd_norm": g_ssd_norm, "b_gate": g_b_gate,
        "norm_mix_post": g_mix_post, "norm_ffn_pre": g_ffn_pre, "norm_ffn_post": g_ffn_post,
        "norm_ple_post": g_ple,
    }
    return lossc, grad_x, parts, conv_slabs, gS


IN_SPLITS = (2048, 6144, 6176, 7200, 8224)
IN_SHARD = 1156
IN_PAD = 1280
COL_SHARDED = {"w_in": (1024, IN_SHARD), "conv_w": (4, 512), "w_gate": (1024, 256), "w_ff1": (1024, 512),
               "w_ple": (256, 128)}
ROW_SHARDED = {"w_ssd_branch": (256, 1024), "w_sb_branch": (128, 1024), "w_out": (128, 1024),
               "w_ff2": (512, 1024), "w_ple_gate": (128, 1024)}
SHARDED = tuple(COL_SHARDED) + tuple(ROW_SHARDED)
SMALL = (
    ("norm_mix_pre", 1024), ("conv_b", 4096), ("dt_bias", 32), ("a_log", 32), ("d_skip", 32), ("ssd_norm", 2048),
    ("b_gate", 2048), ("norm_mix_post", 1024), ("norm_ffn_pre", 1024), ("norm_ffn_post", 1024),
    ("norm_ple_post", 1024),
)
ROW = 1024
SMALL_ROWS = 16


def _rows_of(n):
    return -(-n // ROW)


def _pad_to(v, n, axis=-1):
    pad = [(0, 0)] * v.ndim
    pad[axis] = (0, n - v.shape[axis])
    return jnp.pad(v, pad)


FIRST = ("w_in", "w_gate")
LATE = ("w_ssd_branch", "w_sb_branch", "w_out", "w_ff1", "w_ff2", "w_ple", "w_ple_gate")
LATE_SPLIT = 3


def in_matrices(w_in_slabs, w_gate_slabs):
    w_in = jnp.concatenate([w_in_slabs[k, :, :IN_SHARD] for k in range(N_DEV)], axis=1)
    return {
        "main": jnp.concatenate([w_in[:, :IN_SPLITS[1]], w_in[:, IN_SPLITS[2]:]], axis=1),
        "dt": _pad_to(w_in[:, IN_SPLITS[1]:IN_SPLITS[2]], LANES),
        "gate": w_gate_slabs,
    }


def late_matrices(g):
    short = {"w_ff1": "ff1", "w_ple": "ple", "w_ssd_branch": "ssd", "w_sb_branch": "sb", "w_out": "out",
             "w_ff2": "ff2", "w_ple_gate": "pg"}
    stack = lambda a: a.reshape(a.shape[0] * a.shape[1], a.shape[2])
    return {short[n]: a if n in COL_SHARDED else stack(a) for n, a in g.items()}


def build_small(small, conv_w_slabs):
    P = {k: small[k] for k in ("norm_mix_pre", "conv_b", "ssd_norm", "b_gate", "norm_mix_post", "norm_ffn_pre",
                               "norm_ffn_post", "norm_ple_post")}
    P["conv_w"] = jnp.concatenate([conv_w_slabs[k] for k in range(N_DEV)], axis=1)
    P["dt_bias_pad"] = _pad_to(small["dt_bias"], LANES)
    P["a_exp"] = jnp.repeat(-jnp.exp(small["a_log"]), HEAD_DIM, axis=1)
    P["dsk_exp"] = jnp.repeat(small["d_skip"], HEAD_DIM, axis=1)
    return P


def small_grads(gS, small):
    heads = lambda a: a.reshape(SSD_HEADS, HEAD_DIM).sum(axis=1)[None, :]
    out = {k: gS[k] for k in ("norm_mix_pre", "conv_b", "ssd_norm", "b_gate", "norm_mix_post", "norm_ffn_pre",
                              "norm_ffn_post", "norm_ple_post")}
    out["dt_bias"] = gS["dt_bias_pad"][:, :SSD_HEADS]
    out["a_log"] = heads(gS["a_exp"]) * (-jnp.exp(small["a_log"]))
    out["d_skip"] = heads(gS["dsk_exp"])
    return out


def in_grad_slabs(g):
    pieces = ((0, IN_SPLITS[1], g["main"], 0), (IN_SPLITS[1], IN_SPLITS[2], g["dt"], -IN_SPLITS[1]),
              (IN_SPLITS[2], N_DEV * IN_SHARD, g["main"], IN_SPLITS[1] - IN_SPLITS[2]))
    slabs = []
    for k in range(N_DEV):
        a, b = IN_SHARD * k, IN_SHARD * (k + 1)
        cut = [src[:, max(a, lo) + off:min(b, hi) + off] for lo, hi, src, off in pieces if max(a, lo) < min(b, hi)]
        slabs.append(_pad_to(jnp.concatenate(cut, axis=1), IN_PAD))
    width = COL_SHARDED["conv_w"][1]
    return jnp.stack(slabs), jnp.stack([g["conv_w"][:, width * k:width * (k + 1)] for k in range(N_DEV)])


def pack_small(get):
    cols = [_pad_to(get(name).reshape(n), _rows_of(n) * ROW) for name, n in SMALL]
    return jnp.concatenate(cols).reshape(SMALL_ROWS, ROW)


def unpack_small(flat):
    out, r0 = {}, 0
    for name, n in SMALL:
        rows = _rows_of(n)
        out[name] = flat[r0:r0 + rows].reshape(rows * ROW)[:n].reshape(1, n)
        r0 += rows
    return out


N_CHIPS = 4
JOB_SEMS = {"all": 7, "scatter": 7, "gather2": 7, "pair": 4, "chips": 3}


def exchange_plan(jobs):
    n = len(jobs)
    kinds = [k for k, _ in jobs]
    srcs = [s for _, s in jobs]
    shapes = {"all": lambda s: (N_DEV,) + s.shape, "gather2": lambda s: (N_DEV,) + s.shape,
              "scatter": lambda s: s.shape, "pair": lambda s: (N_CHIPS,) + s.shape[1:], "chips": lambda s: s.shape}
    out_shape = [jax.ShapeDtypeStruct(shapes[k](s), s.dtype) for k, s in jobs]
    offs = [sum(JOB_SEMS[k] for k in kinds[:i]) for i in range(n + 1)]

    def run(phases, src, out, send_sems, recv_sems, local_sems):
        x, y, c = lax.axis_index("x"), lax.axis_index("y"), lax.axis_index("c")
        dev = lambda d: 4 * d[0] + 2 * d[1] + d[2]
        chip_no = lambda ch: 2 * ch[0] + ch[1]
        me, sib, my_chip = (x, y, c), (x, y, 1 - c), (x, y)
        others = [(1 - x, y), (x, 1 - y), (1 - x, 1 - y)]
        all_chips = [(0, 0), (0, 1), (1, 0), (1, 1)]
        peers = [(1 - x if k & 4 else x, 1 - y if k & 2 else y, 1 - c if k & 1 else c) for k in range(1, N_DEV)]
        starts, recvs, local = [], [], []
        chained = [[] for _ in others]

        for i, kind in enumerate(kinds):
            s_ref, o_ref = src[i], out[i]

            def rc(k, src_ref, dst_ref, to, i=i):
                s = offs[i] + k
                return pltpu.make_async_remote_copy(src_ref=src_ref, dst_ref=dst_ref, send_sem=send_sems.at[s],
                                                    recv_sem=recv_sems.at[s], device_id=to,
                                                    device_id_type=pl.DeviceIdType.MESH)

            if kind == "all":
                local.append(pltpu.make_async_copy(s_ref, o_ref.at[dev(me)], local_sems.at[i]))
                for k, peer in enumerate(peers):
                    starts.append(rc(k, s_ref, o_ref.at[dev(me)], peer))
                    recvs.append(rc(k, s_ref, o_ref.at[dev(peer)], peer))
            elif kind == "scatter":
                local.append(pltpu.make_async_copy(s_ref.at[dev(me)], o_ref.at[dev(me)], local_sems.at[i]))
                for k, peer in enumerate(peers):
                    starts.append(rc(k, s_ref.at[dev(peer)], o_ref.at[dev(me)], peer))
                    recvs.append(rc(k, s_ref.at[dev(me)], o_ref.at[dev(peer)], peer))
            elif kind == "gather2":
                local.append(pltpu.make_async_copy(s_ref, o_ref.at[dev(me)], local_sems.at[i]))
                starts.append(rc(0, s_ref, o_ref.at[dev(me)], sib))
                recvs.append(rc(0, s_ref, o_ref.at[dev(sib)], sib))
                for j, ch in enumerate(others):
                    same, other = (*ch, c), (*ch, 1 - c)
                    starts.append(rc(1 + j, s_ref, o_ref.at[dev(me)], same))
                    chained[j].append((rc(1 + j, s_ref, o_ref.at[dev(same)], same),
                                       rc(4 + j, o_ref.at[dev(same)], o_ref.at[dev(same)], sib)))
                    recvs.append(rc(4 + j, s_ref, o_ref.at[dev(other)], sib))
            elif kind == "pair":
                for j, ch in enumerate(all_chips):
                    starts.append(rc(j, s_ref.at[dev((*ch, 1 - c))], o_ref.at[j], sib))
                    recvs.append(rc(j, s_ref.at[dev((*ch, c))], o_ref.at[j], sib))
            else:
                mine = chip_no(my_chip)
                local.append(pltpu.make_async_copy(s_ref.at[mine], o_ref.at[mine], local_sems.at[i]))
                for j, ch in enumerate(others):
                    starts.append(rc(j, s_ref.at[chip_no(ch)], o_ref.at[mine], (*ch, c)))
                    recvs.append(rc(j, s_ref.at[mine], o_ref.at[chip_no(ch)], (*ch, c)))

        if "start" in phases:
            for cp in local + starts:
                cp.start()
        if "pass" in phases:
            for group in chained:
                for arrival, forward in group:
                    arrival.wait_recv()
                    forward.start()
        if "finish" in phases:
            for cp in recvs:
                cp.wait_recv()
            for cp in starts + [forward for group in chained for _, forward in group]:
                cp.wait_send()
            for cp in local:
                cp.wait()

    scratch = [pltpu.SemaphoreType.DMA((offs[n],)), pltpu.SemaphoreType.DMA((offs[n],)),
               pltpu.SemaphoreType.DMA((n,))]
    return srcs, out_shape, scratch, run


EXCHANGE_PHASES = ("start", "pass", "finish")


def exchange(name, jobs):
    srcs, out_shape, scratch, run = exchange_plan(jobs)
    n = len(srcs)

    def body(*refs):
        run(EXCHANGE_PHASES, refs[:n], refs[n:2 * n], *refs[2 * n:])

    any_spec = pl.BlockSpec(memory_space=pl.ANY)
    return pl.pallas_call(body, name=name, out_shape=out_shape, in_specs=[any_spec] * n, out_specs=[any_spec] * n,
                          scratch_shapes=scratch)(*srcs)


def host_exchange(body, n_in, n_out, grid, plan):
    if plan is None:
        return body, [], [], [], [], []
    srcs, out_shape, scratch, run = plan
    n = len(srcs)
    steps = 1
    for g in grid:
        steps *= g

    def hosted(*refs):
        ins, side_in = refs[:n_in], refs[n_in:n_in + n]
        outs, side_out = refs[n_in + n:n_in + n + n_out], refs[n_in + n + n_out:n_in + 2 * n + n_out]
        rest = refs[n_in + 2 * n + n_out:]
        own, sems = rest[:len(rest) - 3], rest[len(rest) - 3:]
        step = 0
        for d, g in enumerate(grid):
            step = step * g + pl.program_id(d)

        def at(when, phase):
            @pl.when(step == when)
            def _():
                run((phase,), side_in, side_out, *sems)

        at(0, "start")
        body(*ins, *outs, *own)
        at(steps * 13 // 16, "pass")
        at(steps - 1, "finish")

    any_spec = pl.BlockSpec(memory_space=pl.ANY)
    return hosted, [any_spec] * n, list(srcs), list(out_shape), [any_spec] * n, list(scratch)


def pair_sum(name, g, sib, core):
    _, r, c = g.shape
    tr = _pick(r, (256, 128, 64, 32, 16, 8))

    def body(core_ref, g_ref, s_ref, o_ref):
        o_ref[...] = (g_ref[...] + s_ref[...]).astype(o_ref.dtype)

    return pl.pallas_call(
        body,
        name=name,
        out_shape=jax.ShapeDtypeStruct((N_CHIPS, r, c), MXU),
        grid_spec=pltpu.PrefetchScalarGridSpec(
            num_scalar_prefetch=1,
            grid=(N_CHIPS, r // tr),
            in_specs=[
                pl.BlockSpec((None, None, tr, c), lambda j, i, core_ref: (j, core_ref[0], i, 0)),
                pl.BlockSpec((None, tr, c), lambda j, i, core_ref: (j, i, 0)),
            ],
            out_specs=pl.BlockSpec((None, tr, c), lambda j, i, core_ref: (j, i, 0)),
        ),
        compiler_params=_cparams(("parallel", "parallel")),
    )(core, g.reshape(N_CHIPS, 2, r, c), sib)


def adamw(name, parts, w, m, v):
    rows, cols = w.shape
    tr = _pick(rows, (128, 64, 32, 16, 8))

    nparts = parts.shape[0]

    def body(p_ref, w_ref, m_ref, v_ref, g_ref, d_ref, m2_ref, v2_ref):
        g = p_ref[0].astype(F32)
        for k in range(1, nparts):
            g = g + p_ref[k].astype(F32)
        m2 = ADAM_B1 * m_ref[...] + (1.0 - ADAM_B1) * g
        v2 = ADAM_B2 * v_ref[...] + (1.0 - ADAM_B2) * jnp.square(g)
        m_hat = m2 / (1.0 - ADAM_B1 ** ADAM_STEP)
        v_hat = v2 / (1.0 - ADAM_B2 ** ADAM_STEP)
        g_ref[...] = g
        d_ref[...] = -ADAM_LR * (m_hat / (jnp.sqrt(v_hat) + ADAM_EPS) + ADAM_WD * w_ref[...])
        m2_ref[...] = m2
        v2_ref[...] = v2

    spec = pl.BlockSpec((tr, cols), lambda i: (i, 0))
    return pl.pallas_call(
        body,
        name=name,
        out_shape=[jax.ShapeDtypeStruct((rows, cols), F32)] * 4,
        grid=(rows // tr,),
        in_specs=[pl.BlockSpec((nparts, tr, cols), lambda i: (0, i, 0)), spec, spec, spec],
        out_specs=[spec] * 4,
        compiler_params=_cparams(("parallel",)),
    )(parts, w, m, v)


WEIGHT_ORDER = (
    "norm_mix_pre", "w_in", "conv_w", "conv_b", "dt_bias", "a_log", "d_skip", "ssd_norm", "w_ssd_branch",
    "w_sb_branch", "w_gate", "b_gate", "w_out", "norm_mix_post", "norm_ffn_pre", "w_ff1", "w_ff2", "norm_ffn_post",
    "w_ple", "w_ple_gate", "norm_ple_post",
)


def kernel(x, p, norm_mix_pre, w_in, conv_w, conv_b, dt_bias, a_log, d_skip, ssd_norm, w_ssd_branch, w_sb_branch, w_gate, b_gate, w_out, norm_mix_post, norm_ffn_pre, w_ff1, w_ff2, norm_ffn_post, w_ple, w_ple_gate, norm_ple_post, loss_target, m_norm_mix_pre, m_w_in, m_conv_w, m_conv_b, m_dt_bias, m_a_log, m_d_skip, m_ssd_norm, m_w_ssd_branch, m_w_sb_branch, m_w_gate, m_b_gate, m_w_out, m_norm_mix_post, m_norm_ffn_pre, m_w_ff1, m_w_ff2, m_norm_ffn_post, m_w_ple, m_w_ple_gate, m_norm_ple_post, v_norm_mix_pre, v_w_in, v_conv_w, v_conv_b, v_dt_bias, v_a_log, v_d_skip, v_ssd_norm, v_w_ssd_branch, v_w_sb_branch, v_w_gate, v_b_gate, v_w_out, v_norm_mix_post, v_norm_ffn_pre, v_w_ff1, v_w_ff2, v_norm_ffn_post, v_w_ple, v_w_ple_gate, v_norm_ple_post):
    a = dict(locals())
    seq = x.shape[1]
    x2 = x.reshape(seq, D_MODEL)
    target = loss_target.reshape(seq, D_MODEL)
    pb = p.reshape(seq, PLE_DIM).astype(MXU)

    def shard(prefix, name, dtype):
        v = a[prefix + name][0].astype(dtype)
        return _pad_to(v, IN_PAD) if name == "w_in" else v

    w_in_slabs, w_gate_slabs, conv_w_slabs = exchange(
        "gather_weights", [("gather2", shard("", "w_in", MXU)), ("gather2", shard("", "w_gate", MXU)),
                           ("all", a["conv_w"][0])])
    small = {n: a[n] for n, _ in SMALL}
    core = lax.axis_index("c").astype(jnp.int32).reshape(1)

    lossc, grad_x, parts, conv_slabs, g_acc = local_step(
        x2, pb, target, in_matrices(w_in_slabs, w_gate_slabs), build_small(small, conv_w_slabs),
        {n: shard("", n, MXU) for n in LATE}, core)
    loss = lax.psum(jnp.sum(lossc), ("x", "y", "c"))
    g_small = small_grads(g_acc, small)

    parts["conv_w"], small_parts = exchange(
        "small_grads", [("scatter", conv_slabs), ("all", pack_small(lambda n: g_small[n]))])

    leaves = {}
    for n, part in parts.items():
        res = adamw("adamw_" + n, part, shard("", n, F32), shard("m_", n, F32), shard("v_", n, F32))
        leaves[n] = [r[None, :, :IN_SHARD] if n == "w_in" else r[None] for r in res]
    res = adamw("adamw_small", small_parts, pack_small(lambda n: a[n]), pack_small(lambda n: a["m_" + n]),
                pack_small(lambda n: a["v_" + n]))
    for j, r in enumerate(res):
        for n, leaf in unpack_small(r).items():
            leaves.setdefault(n, [None] * 4)[j] = leaf
    outs = [loss, grad_x.reshape(x.shape)]
    for j in range(4):
        outs += [leaves[n][j] for n in WEIGHT_ORDER]
    return tuple(outs)
```

```python
import functools

import jax
import jax.numpy as jnp
from jax import lax
from jax.experimental import pallas as pl
from jax.experimental.pallas import tpu as pltpu

F32 = jnp.float32
MXU = jnp.bfloat16
VMEM_LIMIT = 56 * 1024 * 1024

D_MODEL = 1024
D_INNER = 2048
SSD_HEADS = 32
HEAD_DIM = 64
SSD_GROUPS = 8
D_STATE = 128
CONV_K = 4
CONV_DIM = 4096
CHUNK = 128
SB_WIDTH = 1024
D_FF = 4096
PLE_DIM = 256
RMS_EPS = 1e-6
SB_SCALE = HEAD_DIM ** -0.5
N_DEV = 8
LANES = 128

OFF_Z, OFF_XBC, OFF_Q, OFF_K, OFF_V = 0, 2048, 6144, 7168, 8192

ADAM_LR = 0.001
ADAM_B1 = 0.9
ADAM_B2 = 0.999
ADAM_EPS = 1e-08
ADAM_WD = 0.01
ADAM_STEP = 10


def _sig(x):
    return 0.5 * jnp.tanh(0.5 * x) + 0.5


def _softplus(x):
    return jnp.maximum(x, 0.0) + jnp.log(1.0 + jnp.exp(-jnp.abs(x)))


def _rms(x, w):
    return x * lax.rsqrt(jnp.mean(x * x, axis=-1, keepdims=True) + RMS_EPS) * w


def _dot(a, b):
    return jnp.dot(a, b, preferred_element_type=F32)


def _dot_nt(a, b):
    return lax.dot_general(a, b, (((1,), (1,)), ((), ())), preferred_element_type=F32)


def _dot_tn(a, b):
    return lax.dot_general(a, b, (((0,), (0,)), ((), ())), preferred_element_type=F32)


def _split3(x):
    x1 = x.astype(MXU)
    r = x - x1.astype(F32)
    x2 = r.astype(MXU)
    r = r - x2.astype(F32)
    return x1, x2, r.astype(MXU)


def _dot3_l(a, u, parts=3):
    m = a.shape[0]
    d = _dot(jnp.concatenate(_split3(a)[:parts], axis=0), u)
    if parts == 2:
        return d[m:] + d[:m]
    return (d[2 * m:] + d[m:2 * m]) + d[:m]


def _dot3_r(u, a):
    n = a.shape[1]
    d = _dot(u, jnp.concatenate(_split3(a), axis=1))
    return (d[:, 2 * n:] + d[:, n:2 * n]) + d[:, :n]


def _iota(shape, dim):
    return lax.broadcasted_iota(jnp.int32, shape, dim)


def _tri(n, cmp):
    r, c = _iota((n, n), 0), _iota((n, n), 1)
    return cmp(r, c).astype(F32).astype(MXU)


def _cparams(sem):
    return pltpu.CompilerParams(dimension_semantics=sem, vmem_limit_bytes=VMEM_LIMIT)


def _pick(n, cands):
    for c in cands:
        if n % c == 0:
            return c
    return n


def mm(name, a, b, mode, add=None, out_dtype=F32, b_slabs=False, out_slabs=False, epi=None, epi_args=(), extra=(),
       side=None):
    slab = None
    if b_slabs:
        slab = b.shape[2]
        bshape = (b.shape[1], N_DEV * slab)
    else:
        bshape = b.shape
    if mode == "nn":
        (M, K), (K2, N) = a.shape, bshape
    elif mode == "nt":
        (M, K), (N, K2) = a.shape, bshape
    else:
        (K, M), (K2, N) = a.shape, bshape
    assert K == K2, (name, a.shape, b.shape)
    tm = _pick(M, (1024, 512, 256, 128))
    tn = _pick(N, (1024, 512, 256, 128))
    tk = _pick(K, (1024, 512, 256, 128))
    if b_slabs and mode == "nn":
        tn = slab
    if b_slabs and mode == "nt":
        tk = slab
    if out_slabs:
        assert mode == "tn" and N % N_DEV == 0
        tn = N // N_DEV
    nk = K // tk

    def body(*refs):
        refs = list(refs)
        a_ref, b_ref = refs[:2]
        add_ref = refs[2] if add is not None else None
        n_in = 2 + (add is not None)
        epi_refs = refs[n_in:n_in + len(epi_args)]
        o_ref = refs[n_in + len(epi_args)]
        extra_refs = refs[n_in + len(epi_args) + 1:-1]
        acc = refs[-1]
        k = pl.program_id(2)

        @pl.when(k == 0)
        def _():
            acc[...] = jnp.zeros_like(acc) if add is None else add_ref[...]

        av, bv = a_ref[...], b_ref[...]
        if mode == "nn":
            acc[...] += _dot(av, bv)
        elif mode == "nt":
            acc[...] += _dot_nt(av, bv)
        else:
            acc[...] += _dot_tn(av, bv)

        @pl.when(k == nk - 1)
        def _():
            res = acc[...]
            main = res if epi is None else epi(res, *[r[...] for r in epi_refs])
            o_ref[...] = main.astype(o_ref.dtype)
            for r, (_, fn) in zip(extra_refs, extra):
                r[...] = fn(res).astype(r.dtype)

    if mode == "nn":
        a_spec = pl.BlockSpec((tm, tk), lambda i, j, k: (i, k))
        b_spec = pl.BlockSpec((tk, tn), lambda i, j, k: (k, j))
    elif mode == "nt":
        a_spec = pl.BlockSpec((tm, tk), lambda i, j, k: (i, k))
        b_spec = pl.BlockSpec((tn, tk), lambda i, j, k: (j, k))
    else:
        a_spec = pl.BlockSpec((tk, tm), lambda i, j, k: (k, i))
        b_spec = pl.BlockSpec((tk, tn), lambda i, j, k: (k, j))
    if b_slabs and mode == "nn":
        b_spec = pl.BlockSpec((None, tk, tn), lambda i, j, k: (j, k, 0))
    if b_slabs and mode == "nt":
        b_spec = pl.BlockSpec((None, tn, tk), lambda i, j, k: (k, j, 0))
    o_spec = pl.BlockSpec((tm, tn), lambda i, j, k: (i, j))
    in_specs, args = [a_spec, b_spec], [a, b]
    if add is not None:
        in_specs.append(o_spec)
        args.append(add)
    for e in epi_args:
        in_specs.append(o_spec)
        args.append(e)
    out_sds = jax.ShapeDtypeStruct((M, N), out_dtype)
    if out_slabs:
        o_spec = pl.BlockSpec((None, tm, tn), lambda i, j, k: (j, i, 0))
        out_sds = jax.ShapeDtypeStruct((N_DEV, M, tn), out_dtype)
    grid = (M // tm, N // tn, nk)
    body, s_in, s_args, s_shape, s_out, s_scr = host_exchange(body, len(args), 1 + len(extra), grid, side)
    res = pl.pallas_call(
        body,
        name=name,
        out_shape=[out_sds] + [jax.ShapeDtypeStruct((M, N), dt) for dt, _ in extra] + s_shape,
        grid=grid,
        in_specs=in_specs + s_in,
        out_specs=[o_spec] * (1 + len(extra)) + s_out,
        scratch_shapes=[pltpu.VMEM((tm, tn), F32)] + s_scr,
        compiler_params=_cparams(("parallel", "parallel", "arbitrary") if side is None else ("arbitrary",) * 3),
    )(*args, *s_args)
    return res if extra or side is not None else res[0]


def rowwise(name, fn, rows, bcast, outs, accs=(), tr=512, ncb=1, side=None, groups=1):
    S = rows[0][0].shape[0]
    tr = min(tr, S)
    nrb = S // tr
    G = groups
    in_specs, args, in_w = [], [], []
    for arr, off, w in rows:
        assert off % (w * G) == 0 and arr.shape[0] == S
        in_specs.append(pl.BlockSpec((tr, w * G), lambda j, i, ob=off // (w * G): (i, ob + j)))
        args.append(arr)
        in_w.append(w)
    for arr, off, w in bcast:
        assert off % (w * G) == 0
        in_specs.append(pl.BlockSpec((arr.shape[0], w * G), lambda j, i, ob=off // (w * G): (0, ob + j)))
        args.append(arr)
        in_w.append(w)
    out_shape, out_specs, out_w = [], [], []
    for tw, w, dt in outs:
        out_shape.append(jax.ShapeDtypeStruct((S, tw), dt))
        out_specs.append(pl.BlockSpec((tr, w * G), lambda j, i: (i, j)))
        out_w.append(w)
    for tw, w in accs:
        out_shape.append(jax.ShapeDtypeStruct((1, tw), F32))
        out_specs.append(pl.BlockSpec((1, w * G), lambda j, i: (0, j)))
        out_w.append(w)
    nin, nout = len(args), len(outs)

    def body(*refs):
        i = pl.program_id(1)
        for g in range(G):
            cut = lambda w: slice(g * w, (g + 1) * w)
            res = fn(*[r[:, cut(w)] for r, w in zip(refs[:nin], in_w)])
            for k, (r, v) in enumerate(zip(refs[nin:], res)):
                cols = cut(out_w[k])
                if k < nout:
                    r[:, cols] = v.astype(r.dtype)
                    continue

                @pl.when(i == 0)
                def _(r=r, v=v, cols=cols):
                    r[:, cols] = v

                @pl.when(i > 0)
                def _(r=r, v=v, cols=cols):
                    r[:, cols] += v

    body, s_in, s_args, s_shape, s_out, s_scr = host_exchange(body, nin, len(out_shape), (ncb, nrb), side)
    res = pl.pallas_call(
        body,
        name=name,
        out_shape=out_shape + s_shape,
        grid=(ncb, nrb),
        in_specs=in_specs + s_in,
        out_specs=out_specs + s_out,
        scratch_shapes=s_scr,
        compiler_params=_cparams(("parallel" if side is None else "arbitrary", "arbitrary")),
    )(*args, *s_args)
    return res


CONV_TC = 128


CONV_R = 128
HALO = 8


def _conv_pre(e, w, b):
    shifted = [pltpu.roll(e, s, 0) for s in (1, 2, 3)]
    pre = b + w[3:4, :] * e
    for s in (1, 2, 3):
        pre = pre + w[3 - s:4 - s, :] * shifted[s - 1]
    return pre, shifted


def conv_fwd(projmain, conv_w, conv_b):
    S = projmain.shape[0]
    tc = CONV_TC

    def body(u_ref, w_ref, b_ref, o_ref):
        u, w = u_ref[...], w_ref[...]
        row = _iota(u.shape, 0)
        pre = b_ref[...] + w[3:4, :] * u
        for s in (1, 2, 3):
            pre = pre + w[3 - s:4 - s, :] * jnp.where(row >= s, pltpu.roll(u, s, 0), 0.0)
        o_ref[...] = pre * _sig(pre)

    return pl.pallas_call(
        body,
        name="conv_fwd",
        out_shape=jax.ShapeDtypeStruct((S, CONV_DIM), F32),
        grid=(CONV_DIM // tc,),
        in_specs=[
            pl.BlockSpec((S, tc), lambda j: (0, OFF_XBC // tc + j)),
            pl.BlockSpec((CONV_K, tc), lambda j: (0, j)),
            pl.BlockSpec((1, tc), lambda j: (0, j)),
        ],
        out_specs=pl.BlockSpec((S, tc), lambda j: (0, j)),
        compiler_params=_cparams(("parallel",)),
    )(projmain, conv_w, conv_b)


def conv_bwd(dxs, dxs_skip, d_b, d_c, projmain, conv_w, conv_b):
    S = projmain.shape[0]
    tc = CONV_TC
    n_xs, n_b = D_INNER // tc, SSD_GROUPS * D_STATE // tc

    def body(dx_ref, dskip_ref, dbm_ref, dcm_ref, u_ref, w_ref, b_ref, du_ref, dw_ref, db_ref, u_pad, d_pad):
        j = pl.program_id(0)
        zeros = jnp.zeros((HALO, tc), F32)
        for pad in (u_pad, d_pad):
            pad[0:HALO, :] = zeros
            pad[HALO + S:2 * HALO + S, :] = zeros
        u_pad[HALO:HALO + S, :] = u_ref[...]
        d_pad[HALO:HALO + S, :] = jnp.where(j < n_xs, dx_ref[...] + dskip_ref[...],
                                            jnp.where(j < n_xs + n_b, dbm_ref[...], dcm_ref[...]))
        w, b = w_ref[...], b_ref[...]
        n = CONV_R + 2 * HALO
        keep = slice(HALO, HALO + CONV_R)

        def chunk(c, sums):
            r0 = pl.multiple_of(c * CONV_R, CONV_R)
            e = u_pad[pl.ds(r0, n), :]
            pre, shifted = _conv_pre(e, w, b)
            sg = _sig(pre)
            dpre = d_pad[pl.ds(r0, n), :] * (sg * (1.0 + pre * (1.0 - sg)))
            du = w[3:4, :] * dpre
            for s in (1, 2, 3):
                du = du + w[3 - s:4 - s, :] * pltpu.roll(dpre, n - s, 0)
            du_ref[pl.ds(r0, CONV_R), :] = du[keep].astype(du_ref.dtype)
            dk = dpre[keep]
            taps = [shifted[2], shifted[1], shifted[0], e]
            return tuple(acc + jnp.sum(dk * t[keep], axis=0, keepdims=True) for acc, t in zip(sums[:4], taps)) + (
                sums[4] + jnp.sum(dk, axis=0, keepdims=True),)

        zero = jnp.zeros((1, tc), F32)
        sums = lax.fori_loop(0, S // CONV_R, chunk, (zero,) * 5)
        for k in range(CONV_K):
            dw_ref[k:k + 1, :] = sums[k]
        db_ref[...] = sums[4]

    return pl.pallas_call(
        body,
        name="conv_bwd",
        out_shape=[
            jax.ShapeDtypeStruct((S, CONV_DIM), MXU),
            jax.ShapeDtypeStruct((CONV_K, CONV_DIM), F32),
            jax.ShapeDtypeStruct((1, CONV_DIM), F32),
        ],
        grid=(CONV_DIM // tc,),
        in_specs=[
            pl.BlockSpec((S, tc), lambda j: (0, jnp.minimum(j, n_xs - 1))),
            pl.BlockSpec((S, tc), lambda j: (0, jnp.minimum(j, n_xs - 1))),
            pl.BlockSpec((S, tc), lambda j: (0, jnp.clip(j - n_xs, 0, n_b - 1))),
            pl.BlockSpec((S, tc), lambda j: (0, jnp.clip(j - n_xs - n_b, 0, n_b - 1))),
            pl.BlockSpec((S, tc), lambda j: (0, OFF_XBC // tc + j)),
            pl.BlockSpec((CONV_K, tc), lambda j: (0, j)),
            pl.BlockSpec((1, tc), lambda j: (0, j)),
        ],
        out_specs=[
            pl.BlockSpec((S, tc), lambda j: (0, j)),
            pl.BlockSpec((CONV_K, tc), lambda j: (0, j)),
            pl.BlockSpec((1, tc), lambda j: (0, j)),
        ],
        scratch_shapes=[pltpu.VMEM((S + 2 * HALO, tc), F32)] * 2,
        compiler_params=_cparams(("arbitrary",)),
    )(dxs, dxs_skip, d_b, d_c, projmain, conv_w, conv_b)


def _head_expand():
    r, j = _iota((LANES, D_INNER), 0), _iota((LANES, D_INNER), 1)
    return ((j >= r * HEAD_DIM) & (j < r * HEAD_DIM + HEAD_DIM)).astype(F32).astype(MXU)


def _head_reduce():
    j, r = _iota((D_INNER, LANES), 0), _iota((D_INNER, LANES), 1)
    return ((j >= r * HEAD_DIM) & (j < r * HEAD_DIM + HEAD_DIM)).astype(F32).astype(MXU)


def ssd_prep(dtraw, dt_bias_pad, a_exp):
    S = dtraw.shape[0]

    def body(dtr_ref, bias_ref, a_ref, dte_ref, cse_ref):
        dt = _softplus(dtr_ref[...] + bias_ref[...])
        dte = _dot3_l(dt, _head_expand())
        dte_ref[...] = dte
        incl = _tri(CHUNK, lambda r, c: r >= c)
        cse_ref[...] = _dot3_r(incl, dte * a_ref[...])

    return pl.pallas_call(
        body,
        name="ssd_prep",
        out_shape=[jax.ShapeDtypeStruct((S, D_INNER), F32)] * 2,
        grid=(S // CHUNK,),
        in_specs=[
            pl.BlockSpec((CHUNK, LANES), lambda c: (c, 0)),
            pl.BlockSpec((1, LANES), lambda c: (0, 0)),
            pl.BlockSpec((1, D_INNER), lambda c: (0, 0)),
        ],
        out_specs=[pl.BlockSpec((CHUNK, D_INNER), lambda c: (c, 0))] * 2,
        compiler_params=_cparams(("parallel",)),
    )(dtraw, dt_bias_pad, a_exp)


GW = 4 * HEAD_DIM
SSD_GPS = 8
SSD_GPS_BWD = 8


def ssd_fwd(xbc_act, dte, cse, side=None):
    S = xbc_act.shape[0]
    nc = S // CHUNK

    def body(xs_ref, b_ref, c_ref, dte_ref, cse_ref, y_ref, st_ref, s_scr):
        c = pl.program_id(1)

        @pl.when(c == 0)
        def _():
            s_scr[...] = jnp.zeros_like(s_scr)

        for gg in range(SSD_GPS):
            ch, st = slice(GW * gg, GW * (gg + 1)), slice(D_STATE * gg, D_STATE * (gg + 1))
            s_in = s_scr[:, ch]
            st_ref[0, :, ch] = s_in
            cs = cse_ref[:, ch]
            xd = xs_ref[:, ch] * dte_ref[:, ch]
            btb = b_ref[:, st].T.astype(MXU)
            cs_last = cs[CHUNK - 1:CHUNK, :]
            gy = _dot(c_ref[:, st].astype(MXU), jnp.concatenate([btb, s_in.astype(MXU)], axis=1))
            g, y = gy[:, :CHUNK], gy[:, CHUNK:] * jnp.exp(cs)
            cs_t = cs.T
            row, col = _iota((CHUNK, CHUNK), 0), _iota((CHUNK, CHUNK), 1)
            ms = []
            for h in range(4):
                lo = HEAD_DIM * h
                lam = jnp.where(row >= col, jnp.exp(cs[:, lo:lo + 1] - cs_t[lo:lo + 1, :]), 0.0)
                ms.append((g * lam).astype(MXU))
            yd = _dot(jnp.concatenate(ms, axis=0), xd.astype(MXU))
            lane = _iota((CHUNK, GW), 1)
            for h in range(4):
                y = y + jnp.where((lane >= HEAD_DIM * h) & (lane < HEAD_DIM * (h + 1)),
                                  yd[CHUNK * h:CHUNK * (h + 1)], 0.0)
            y_ref[:, ch] = y
            w = (xd * jnp.exp(cs_last - cs)).astype(MXU)
            s_scr[:, ch] = jnp.exp(cs_last) * s_in + _dot(btb, w)

    grid = (SSD_GROUPS // SSD_GPS, nc)
    gw, gs = GW * SSD_GPS, D_STATE * SSD_GPS
    body, s_in, s_args, s_shape, s_out, s_scr = host_exchange(body, 5, 2, grid, side)
    return pl.pallas_call(
        body,
        name="ssd_fwd",
        out_shape=[
            jax.ShapeDtypeStruct((S, D_INNER), F32),
            jax.ShapeDtypeStruct((nc, D_STATE, D_INNER), F32),
        ] + s_shape,
        grid=grid,
        in_specs=[
            pl.BlockSpec((CHUNK, gw), lambda g, c: (c, g)),
            pl.BlockSpec((CHUNK, gs), lambda g, c: (c, D_INNER // gs + g)),
            pl.BlockSpec((CHUNK, gs), lambda g, c: (c, (D_INNER + SSD_GROUPS * D_STATE) // gs + g)),
            pl.BlockSpec((CHUNK, gw), lambda g, c: (c, g)),
            pl.BlockSpec((CHUNK, gw), lambda g, c: (c, g)),
        ] + s_in,
        out_specs=[
            pl.BlockSpec((CHUNK, gw), lambda g, c: (c, g)),
            pl.BlockSpec((1, D_STATE, gw), lambda g, c: (c, 0, g)),
        ] + s_out,
        scratch_shapes=[pltpu.VMEM((D_STATE, gw), F32)] + s_scr,
        compiler_params=_cparams(("parallel" if side is None else "arbitrary", "arbitrary")),
    )(xbc_act, xbc_act, xbc_act, dte, cse, *s_args)


def ssd_bwd(dy, xbc_act, dte, cse, states, a_exp, side=None):
    S = xbc_act.shape[0]
    nc = S // CHUNK

    def body(*refs):
        for gg in range(SSD_GPS_BWD):
            ch, st = slice(GW * gg, GW * (gg + 1)), slice(D_STATE * gg, D_STATE * (gg + 1))
            cut = {GW * SSD_GPS_BWD: ch, D_STATE * SSD_GPS_BWD: st}
            one_group(*[r.at[(slice(None),) * (len(r.shape) - 1) + (cut[r.shape[-1]],)] for r in refs])

    def one_group(dy_ref, xs_ref, b_ref, c_ref, dte_ref, cse_ref, sin_ref, sout_ref, a_ref,
                  dxs_ref, db_ref, dc_ref, ddt_ref, dal_ref, ds_scr):
        j = pl.program_id(1)

        @pl.when(j == 0)
        def _():
            ds_scr[...] = jnp.zeros_like(ds_scr)
            dal_ref[...] = jnp.zeros_like(dal_ref)

        ds_out = ds_scr[...]
        dyv, xs = dy_ref[...], xs_ref[...]
        dt, cs = dte_ref[...], cse_ref[...]
        s_in = sin_ref[0]
        bm, cm = b_ref[...], c_ref[...]
        bb, cb = bm.astype(MXU), cm.astype(MXU)
        btb, ctb = bm.T.astype(MXU), cm.T.astype(MXU)
        dsb, sib = ds_out.astype(MXU), s_in.astype(MXU)
        xd = xs * dt
        ecs = jnp.exp(cs)
        cs_last = cs[CHUNK - 1:CHUNK, :]
        eend = jnp.exp(cs_last - cs)
        gy = _dot(cb, jnp.concatenate([btb, sib], axis=1))
        g, yoff = gy[:, :CHUNK], gy[:, CHUNK:] * ecs
        gd = _dot(bb, jnp.concatenate([ctb, dsb], axis=1))
        g_t, dxd_off = gd[:, :CHUNK], gd[:, CHUNK:] * eend
        cs_t = cs.T
        row, col = _iota((CHUNK, CHUNK), 0), _iota((CHUNK, CHUNK), 1)
        lane = _iota((CHUNK, GW), 1)
        heads = [(lane >= HEAD_DIM * h) & (lane < HEAD_DIM * (h + 1)) for h in range(4)]
        dyb, xdb = dyv.astype(MXU), xd.astype(MXU)
        dm_all = _dot_nt(jnp.concatenate([jnp.where(hm, dyv, 0.0) for hm in heads], axis=0).astype(MXU), xdb)
        dmt_all = _dot_nt(jnp.concatenate([jnp.where(hm, xd, 0.0) for hm in heads], axis=0).astype(MXU), dyb)
        lams, m_ts = [], []
        for h in range(4):
            lo = HEAD_DIM * h
            cs_col, cs_row = cs[:, lo:lo + 1], cs_t[lo:lo + 1, :]
            lams.append(jnp.where(row >= col, jnp.exp(cs_col - cs_row), 0.0))
            m_ts.append(g_t * jnp.where(col >= row, jnp.exp(cs_row - cs_col), 0.0))
        acc_all = _dot(jnp.concatenate(m_ts, axis=0).astype(MXU), dyb)
        dxd = dxd_off
        dg = jnp.zeros((CHUNK, CHUNK), F32)
        dcs = dyv * yoff - xd * dxd_off
        for h in range(4):
            blk = slice(CHUNK * h, CHUNK * (h + 1))
            dm, dm_t = dm_all[blk], dmt_all[blk]
            dxd = dxd + jnp.where(heads[h], acc_all[blk], 0.0)
            dg = dg + dm * lams[h]
            wdiff = (jnp.sum(dm * (g * lams[h]), axis=1, keepdims=True)
                     - jnp.sum(dm_t * m_ts[h], axis=1, keepdims=True))
            dcs = dcs + jnp.where(lane == HEAD_DIM * h, wdiff, 0.0)
        dye = (dyv * ecs).astype(MXU)
        dc_ref[...] = _dot(dg.astype(MXU), bb) + _dot_nt(dye, sib)
        db_ref[...] = _dot(dg.T.astype(MXU), cb) + _dot_nt((xd * eend).astype(MXU), dsb)
        ds_scr[...] = jnp.exp(cs_last) * ds_out + _dot(ctb, dye)
        last = jnp.sum(ds_out * sout_ref[0], axis=0, keepdims=True)
        rows = _iota((CHUNK, GW), 0)
        dcs = dcs + jnp.where(rows == CHUNK - 1, last, 0.0)
        dda = _dot3_r(_tri(CHUNK, lambda r, c: c >= r), dcs)
        ddt_ref[...] = a_ref[...] * dda + dxd * xs
        dal_ref[...] += jnp.sum(dt * dda, axis=0, keepdims=True)
        dxs_ref[...] = dxd * dt

    rc = lambda g, j: (nc - 1 - j, g)
    grid = (SSD_GROUPS // SSD_GPS_BWD, nc)
    gw, gs = GW * SSD_GPS_BWD, D_STATE * SSD_GPS_BWD
    body, s_in, s_args, s_shape, s_out, s_scr = host_exchange(body, 9, 5, grid, side)
    return pl.pallas_call(
        body,
        name="ssd_bwd",
        out_shape=[
            jax.ShapeDtypeStruct((S, D_INNER), F32),
            jax.ShapeDtypeStruct((S, SSD_GROUPS * D_STATE), F32),
            jax.ShapeDtypeStruct((S, SSD_GROUPS * D_STATE), F32),
            jax.ShapeDtypeStruct((S, D_INNER), F32),
            jax.ShapeDtypeStruct((1, D_INNER), F32),
        ] + s_shape,
        grid=grid,
        in_specs=[
            pl.BlockSpec((CHUNK, gw), rc),
            pl.BlockSpec((CHUNK, gw), rc),
            pl.BlockSpec((CHUNK, gs), lambda g, j: (nc - 1 - j, D_INNER // gs + g)),
            pl.BlockSpec((CHUNK, gs), lambda g, j: (nc - 1 - j, (D_INNER + SSD_GROUPS * D_STATE) // gs + g)),
            pl.BlockSpec((CHUNK, gw), rc),
            pl.BlockSpec((CHUNK, gw), rc),
            pl.BlockSpec((1, D_STATE, gw), lambda g, j: (nc - 1 - j, 0, g)),
            pl.BlockSpec((1, D_STATE, gw), lambda g, j: (jnp.minimum(nc - j, nc - 1), 0, g)),
            pl.BlockSpec((1, gw), lambda g, j: (0, g)),
        ] + s_in,
        out_specs=[
            pl.BlockSpec((CHUNK, gw), rc),
            pl.BlockSpec((CHUNK, gs), rc),
            pl.BlockSpec((CHUNK, gs), rc),
            pl.BlockSpec((CHUNK, gw), rc),
            pl.BlockSpec((1, gw), lambda g, j: (0, g)),
        ] + s_out,
        scratch_shapes=[pltpu.VMEM((D_STATE, gw), F32)] + s_scr,
        compiler_params=_cparams(("parallel" if side is None else "arbitrary", "arbitrary")),
    )(dy, xbc_act, xbc_act, xbc_act, dte, cse, states, states, a_exp, *s_args)


SB_T = 256
SB_DROP = 104.0
SB_PAIRS = 1
SB_PARTS = 2
SB_W = SB_PAIRS * LANES
SB_LANES = [slice(LANES * p, LANES * (p + 1)) for p in range(SB_PAIRS)]


def _sb_scores(qm, k_ref, ks, rowi, coli, diag):
    kblk = k_ref[pl.ds(ks, SB_T), :].astype(MXU)
    z = _dot_nt(qm, kblk) * SB_SCALE
    sp = _softplus(z)
    if not diag:
        return kblk, z, None, sp, sp
    mask = (ks + coli) < rowi
    return kblk, z, mask, sp, jnp.where(mask, sp, 0.0)


def _sb_stack(v):
    lane = _iota(v.shape, 1)
    return jnp.concatenate([jnp.where(lane < HEAD_DIM, v, 0.0), jnp.where(lane >= HEAD_DIM, v, 0.0)], axis=0)


def _sb_unstack(v):
    lane = _iota((SB_T, LANES), 1)
    return jnp.where(lane < HEAD_DIM, v[:SB_T], v[SB_T:])


def _sb_rows(qb):
    r = _iota((2 * SB_T, SB_T), 0)
    return qb * SB_T + jnp.where(r >= SB_T, r - SB_T, r), _iota((2 * SB_T, SB_T), 1)


def sb_fwd(projmain, side=None):
    S = projmain.shape[0]
    nq = S // SB_T

    def body(q_ref, k_ref, v_ref, o_ref, t_ref, n_ref):
        hp, qb = pl.program_id(0), pl.program_id(1)
        qsts = [_sb_stack(q_ref[:, sl]).astype(MXU) for sl in SB_LANES]
        rowi, coli = _sb_rows(qb)
        u_after = _tri(SB_T, lambda r, c: r > c)

        def cond(carry):
            i, rmin, _, _ = carry
            return (i <= qb) & (rmin < SB_DROP)

        def kstep(carry, diag=False):
            i, _, rs, accs = carry
            ks = pl.multiple_of((qb - i) * SB_T, SB_T)
            rs, accs = list(rs), list(accs)
            for p, sl in enumerate(SB_LANES):
                _, z, mask, sp, spm = _sb_scores(qsts[p], k_ref.at[:, sl], ks, rowi, coli, diag)
                vblk = v_ref[pl.ds(ks, SB_T), sl].astype(MXU)
                a = jnp.exp(z - sp - _dot3_l(spm, u_after, SB_PARTS) - rs[p])
                if diag:
                    a = jnp.where(mask, a, 0.0)
                accs[p] = accs[p] + _dot(a.astype(MXU), vblk)
                rs[p] = rs[p] + jnp.sum(spm, axis=1, keepdims=True)
            return i + 1, jnp.min(functools.reduce(jnp.minimum, rs)), rs, accs

        first = kstep((jnp.int32(0), jnp.float32(0.0), [jnp.zeros((2 * SB_T, 1), F32)] * SB_PAIRS,
                       [jnp.zeros((2 * SB_T, LANES), F32)] * SB_PAIRS), diag=True)
        n, _, rs, accs = lax.while_loop(cond, kstep, first)
        for p, sl in enumerate(SB_LANES):
            o_ref[:, sl] = _sb_unstack(accs[p]).astype(o_ref.dtype)
            t_ref[:, sl] = _sb_unstack(jnp.broadcast_to(rs[p], (2 * SB_T, LANES)))
        n_ref[hp, qb] = n

    grid = (SB_WIDTH // SB_W, nq)
    body, s_in, s_args, s_shape, s_out, s_scr = host_exchange(body, 3, 3, grid, side)
    return pl.pallas_call(
        body,
        name="sb_fwd",
        out_shape=[jax.ShapeDtypeStruct((S, SB_WIDTH), MXU), jax.ShapeDtypeStruct((S, SB_WIDTH), F32),
                   jax.ShapeDtypeStruct((SB_WIDTH // SB_W, nq), jnp.int32)] + s_shape,
        grid=grid,
        in_specs=[
            pl.BlockSpec((SB_T, SB_W), lambda h, i: (i, OFF_Q // SB_W + h)),
            pl.BlockSpec((S, SB_W), lambda h, i: (0, OFF_K // SB_W + h)),
            pl.BlockSpec((S, SB_W), lambda h, i: (0, OFF_V // SB_W + h)),
        ] + s_in,
        out_specs=[pl.BlockSpec((SB_T, SB_W), lambda h, i: (i, h))] * 2
        + [pl.BlockSpec(memory_space=pltpu.SMEM)] + s_out,
        scratch_shapes=s_scr,
        compiler_params=_cparams(("arbitrary", "arbitrary")),
    )(projmain, projmain, projmain, *s_args)


def sb_bwd(projmain, do, t_exp, nblk, side=None):
    S = projmain.shape[0]
    nq = S // SB_T

    def body(n_ref, q_ref, k_ref, v_ref, do_ref, t_ref, dq_ref, dk_ref, dv_ref):
        hp, qb = pl.program_id(0), pl.program_id(1)

        @pl.when(qb == 0)
        def _():
            dk_ref[...] = jnp.zeros_like(dk_ref)
            dv_ref[...] = jnp.zeros_like(dv_ref)

        lane = _iota((SB_T, LANES), 1)
        qsts, dosts, tots = [], [], []
        for sl in SB_LANES:
            qsts.append(_sb_stack(q_ref[:, sl]).astype(MXU))
            dosts.append(_sb_stack(do_ref[:, sl]).astype(MXU))
            tv = t_ref[:, sl]
            tots.append(jnp.concatenate(
                [jnp.sum(jnp.where(lane == HEAD_DIM * hh, tv, 0.0), axis=1, keepdims=True) for hh in range(2)],
                axis=0))
        rowi, coli = _sb_rows(qb)
        u_upto = _tri(SB_T, lambda r, c: r <= c)
        u_before = _tri(SB_T, lambda r, c: r < c)
        kept = jnp.clip(n_ref[hp, qb], 1, qb + 1)

        def kstep(kb, carry, diag=False):
            ks = pl.multiple_of(kb * SB_T, SB_T)
            out = []
            for p, sl in enumerate(SB_LANES):
                psp, pg, dq = carry[p]
                kblk, z, mask, sp, spm = _sb_scores(qsts[p], k_ref.at[:, sl], ks, rowi, coli, diag)
                vblk = v_ref[pl.ds(ks, SB_T), sl].astype(MXU)
                after = tots[p] - (psp + _dot3_l(spm, u_upto, SB_PARTS))
                a = jnp.exp(z - sp - after)
                if diag:
                    a = jnp.where(mask, a, 0.0)
                gm = _dot_nt(dosts[p], vblk) * a
                before = pg + _dot3_l(gm, u_before, SB_PARTS)
                sg = jnp.exp(z - sp)
                dz = (gm * (1.0 - sg) - sg * before) * SB_SCALE
                if diag:
                    dz = jnp.where(mask, dz, 0.0)
                dzb = dz.astype(MXU)
                dk_ref[pl.ds(ks, SB_T), sl] += _dot_tn(dzb, qsts[p])
                dv_ref[pl.ds(ks, SB_T), sl] += _dot_tn(a.astype(MXU), dosts[p])
                out.append((psp + jnp.sum(spm, axis=1, keepdims=True),
                            pg + jnp.sum(gm, axis=1, keepdims=True), dq + _dot(dzb, kblk)))
            return tuple(out)

        zero1 = jnp.zeros((2 * SB_T, 1), F32)
        state = lax.fori_loop(qb + 1 - kept, qb, kstep,
                              ((zero1, zero1, jnp.zeros((2 * SB_T, LANES), F32)),) * SB_PAIRS)
        state = kstep(qb, state, diag=True)
        for p, sl in enumerate(SB_LANES):
            dq_ref[:, sl] = _sb_unstack(state[p][2]).astype(dq_ref.dtype)

    grid = (SB_WIDTH // SB_W, nq)
    body, s_in, s_args, s_shape, s_out, s_scr = host_exchange(body, 6, 3, grid, side)
    return pl.pallas_call(
        body,
        name="sb_bwd",
        out_shape=[
            jax.ShapeDtypeStruct((S, SB_WIDTH), MXU),
            jax.ShapeDtypeStruct((S, SB_WIDTH), F32),
            jax.ShapeDtypeStruct((S, SB_WIDTH), F32),
        ] + s_shape,
        grid=grid,
        in_specs=[
            pl.BlockSpec(memory_space=pltpu.SMEM),
            pl.BlockSpec((SB_T, SB_W), lambda h, i: (i, OFF_Q // SB_W + h)),
            pl.BlockSpec((S, SB_W), lambda h, i: (0, OFF_K // SB_W + h)),
            pl.BlockSpec((S, SB_W), lambda h, i: (0, OFF_V // SB_W + h)),
            pl.BlockSpec((SB_T, SB_W), lambda h, i: (i, h)),
            pl.BlockSpec((SB_T, SB_W), lambda h, i: (i, h)),
        ] + s_in,
        out_specs=[
            pl.BlockSpec((SB_T, SB_W), lambda h, i: (i, h)),
            pl.BlockSpec((S, SB_W), lambda h, i: (0, h)),
            pl.BlockSpec((S, SB_W), lambda h, i: (0, h)),
        ] + s_out,
        scratch_shapes=s_scr,
        compiler_params=_cparams(("arbitrary", "arbitrary")),
    )(nblk, projmain, projmain, projmain, do, t_exp, *s_args)


def local_step(x, pb, target, W, P, late, core):
    D = D_MODEL
    full = lambda a, w=D: (a, 0, w)

    (n1b,) = rowwise("norm_pre", lambda xv, g: (_rms(xv, g),), [full(x)], [full(P["norm_mix_pre"])],
                     [(D, D, MXU)])
    projmain = mm("in_proj", n1b, W["main"], "nn")
    gate_pre = mm("gate_proj", n1b, W["gate"], "nn", b_slabs=True)
    dtraw = mm("dt_proj", n1b, W["dt"], "nn")
    xbc_act = conv_fwd(projmain, P["conv_w"], P["conv_b"])
    dte, cse = ssd_prep(dtraw, P["dt_bias_pad"], P["a_exp"])
    y_scan, states, *got = ssd_fwd(xbc_act, dte, cse,
                                   side=exchange_plan([("gather2", late[n]) for n in LATE[:LATE_SPLIT]]))
    W = {**W, **late_matrices(dict(zip(LATE[:LATE_SPLIT], got)))}

    def f_gate(ysc, xs, z, dsk, nw):
        return _rms((ysc + xs * dsk) * (z * _sig(z)), nw)

    (y_ssd_b,) = rowwise("ssd_gate", lambda *a: (f_gate(*a),),
                         [(y_scan, 0, GW), (xbc_act, 0, GW), (projmain, OFF_Z, GW)],
                         [(P["dsk_exp"], 0, GW), (P["ssd_norm"], 0, GW)], [(D_INNER, GW, MXU)], tr=256,
                         groups=SSD_GROUPS)
    y_sb_b, t_exp, sb_kept, *got = sb_fwd(
        projmain, side=exchange_plan([("gather2", late[n]) for n in LATE[LATE_SPLIT:]]))
    W = {**W, **late_matrices(dict(zip(LATE[LATE_SPLIT:], got)))}
    u1 = mm("ssd_branch", y_ssd_b, W["ssd"], "nn")
    u2 = mm("sb_branch", y_sb_b, W["sb"], "nn")

    def f_merge(a1, a2, g1, g2, b1, b2):
        return (_sig(g1 + b1) * a1 + _sig(g2 + b2) * a2,)

    gate_rows = [(gate_pre, 0, D), (gate_pre, D, D)]
    gate_bias = [(P["b_gate"], 0, D), (P["b_gate"], D, D)]
    (merged_b,) = rowwise("merge", f_merge, [full(u1), full(u2)] + gate_rows, gate_bias, [(D, D, MXU)])
    mo = mm("out_proj", merged_b, W["out"], "nn")

    def f_mid(xv, m, gpost, gffn):
        h1 = xv + _rms(m, gpost)
        return h1, _rms(h1, gffn)

    h1, n2b = rowwise("mix_post", f_mid, [full(x), full(mo)],
                      [full(P["norm_mix_post"]), full(P["norm_ffn_pre"])], [(D, D, F32), (D, D, MXU)])
    a_ff, rb = mm("ff1", n2b, W["ff1"], "nn", b_slabs=True,
                  extra=[(MXU, lambda acc: jnp.square(jnp.maximum(acc, 0.0)))])
    ff = mm("ff2", rb, W["ff2"], "nn")

    def f_ffn_post(h, f, g):
        h2 = h + _rms(f, g)
        return h2, h2

    h2, h2b = rowwise("ffn_post", f_ffn_post, [full(h1), full(ff)], [full(P["norm_ffn_post"])],
                      [(D, D, F32), (D, D, MXU)])
    pgp = mm("ple_gate", h2b, W["pg"], "nn")
    pe = mm("ple_proj", pb, W["ple"], "nn", b_slabs=True)

    def f_ple(h2v, gp, pev, tgt, g):
        f = lambda h, a, b, gg: h + _rms(_sig(a) * b, gg)
        h3, vjp = jax.vjp(f, h2v, gp, pev, g)
        err = h3 - tgt
        dh2, dgp, dpe, dg = vjp(err * (1.0 / D))
        lossc = (0.5 / D) * jnp.sum(err * err, axis=0, keepdims=True)
        return dh2, dgp, dpe, dg, lossc

    dh2a, dpgp_b, dpe_b, g_ple, lossc = rowwise(
        "ple_loss", f_ple, [full(h2), full(pgp), full(pe), full(target)], [full(P["norm_ple_post"])],
        [(D, D, F32), (D, D, MXU), (D, D, MXU)], [(D, D), (D, D)])
    dh2 = mm("d_ple_gate_x", dpgp_b, W["pg"], "nt", add=dh2a)
    gW = {}
    gW["w_ple_gate"] = mm("d_ple_gate_w", h2b, dpgp_b, "tn")
    gW["w_ple"] = mm("d_ple_w", pb, dpe_b, "tn", out_slabs=True)

    def b_ffn_post(d, f, g):
        _, vjp = jax.vjp(_rms, f, g)
        return vjp(d)

    dff_b, g_ffn_post = rowwise("d_ffn_post", b_ffn_post, [full(dh2), full(ff)], [full(P["norm_ffn_post"])],
                                [(D, D, MXU)], [(D, D)])
    da_b = mm("d_ff2_x", dff_b, W["ff2"], "nt", out_dtype=MXU,
              epi=lambda acc, act: acc * (2.0 * jnp.maximum(act, 0.0)), epi_args=[a_ff])
    gW["w_ff2"] = mm("d_ff2_w", rb, dff_b, "tn")
    dn2 = mm("d_ff1_x", da_b, W["ff1"], "nt", b_slabs=True)
    gW["w_ff1"] = mm("d_ff1_w", n2b, da_b, "tn", out_slabs=True)

    def b_mid(d2, dn, h, m, gpost, gffn):
        _, vjp = jax.vjp(_rms, h, gffn)
        dh, dgffn = vjp(dn)
        dh1 = d2 + dh
        _, vjp2 = jax.vjp(_rms, m, gpost)
        dm, dgpost = vjp2(dh1)
        return dh1, dm, dgpost, dgffn

    dh1, dmo_b, g_mix_post, g_ffn_pre = rowwise(
        "d_mix_post", b_mid, [full(dh2), full(dn2), full(h1), full(mo)],
        [full(P["norm_mix_post"]), full(P["norm_ffn_pre"])], [(D, D, F32), (D, D, MXU)], [(D, D), (D, D)])
    dmerged = mm("d_out_x", dmo_b, W["out"], "nt")
    gW["w_out"] = mm("d_out_w", merged_b, dmo_b, "tn")

    def b_merge(d, a1, a2, g1, g2, b1, b2):
        s1, s2 = _sig(g1 + b1), _sig(g2 + b2)
        dg1 = d * a1 * s1 * (1.0 - s1)
        dg2 = d * a2 * s2 * (1.0 - s2)
        dg = jnp.concatenate([dg1, dg2], axis=1)
        return d * s1, d * s2, dg, jnp.sum(dg, axis=0, keepdims=True)

    du1_b, du2_b, dgp_b, g_b_gate = rowwise(
        "d_merge", b_merge, [full(dmerged), full(u1), full(u2)] + gate_rows, gate_bias,
        [(D, D, MXU), (D, D, MXU), (2 * D, 2 * D, MXU)], [(2 * D, 2 * D)])
    dy_ssd = mm("d_ssd_branch_x", du1_b, W["ssd"], "nt")
    dy_sb = mm("d_sb_branch_x", du2_b, W["sb"], "nt")
    gW["w_ssd_branch"] = mm("d_ssd_branch_w", y_ssd_b, du1_b, "tn")
    gW["w_sb_branch"] = mm("d_sb_branch_w", y_sb_b, du2_b, "tn")

    def b_gate(d, ysc, xs, z, dsk, nw):
        _, vjp = jax.vjp(f_gate, ysc, xs, z, dsk, nw)
        return vjp(d)

    slabs = {n: gW[n] if n in COL_SHARDED else gW[n].reshape((N_DEV,) + ROW_SHARDED[n]) for n in LATE}
    dy_scan, dxs_skip, dz_b, g_dsk_exp, g_ssd_norm, *from_sibling = rowwise(
        "d_ssd_gate", b_gate, [(dy_ssd, 0, GW), (y_scan, 0, GW), (xbc_act, 0, GW), (projmain, OFF_Z, GW)],
        [(P["dsk_exp"], 0, GW), (P["ssd_norm"], 0, GW)],
        [(D_INNER, GW, F32), (D_INNER, GW, F32), (D_INNER, GW, MXU)], [(D_INNER, GW), (D_INNER, GW)],
        tr=256, groups=SSD_GROUPS, side=exchange_plan([("pair", slabs[n]) for n in LATE]))
    chip_sums = [pair_sum("pair_sum_" + n, slabs[n], s, core) for n, s in zip(LATE, from_sibling)]
    dxs, d_b, d_c, ddt_part, g_a_exp, *late_parts = ssd_bwd(
        dy_scan, xbc_act, dte, cse, states, P["a_exp"], side=exchange_plan([("chips", s) for s in chip_sums]))

    def b_dt(dpart, dtr, bias):
        ddt = _dot3_l(dpart, _head_reduce())
        d = ddt * _sig(dtr + bias)
        return d, jnp.sum(d, axis=0, keepdims=True)

    ddt_b, g_dt_bias_pad = rowwise("d_dt", b_dt, [(ddt_part, 0, D_INNER), (dtraw, 0, LANES)],
                                   [(P["dt_bias_pad"], 0, LANES)], [(LANES, LANES, MXU)], [(LANES, LANES)])
    dxbc_b, g_conv_w, g_conv_b = conv_bwd(dxs, dxs_skip, d_b, d_c, projmain, P["conv_w"], P["conv_b"])
    dq_b, dk, dv = sb_bwd(projmain, dy_sb, t_exp, sb_kept)
    dmain_b = jnp.concatenate([dz_b, dxbc_b, dq_b, dk.astype(MXU), dv.astype(MXU)], axis=1)
    g_main = mm("d_in_w", n1b, dmain_b, "tn")
    g_dt = mm("d_dt_w", n1b, ddt_b, "tn")
    last = {"w_gate": mm("d_gate_w", n1b, dgp_b, "tn", out_slabs=True)}
    last["w_in"], conv_slabs = in_grad_slabs({"main": g_main, "dt": g_dt, "conv_w": g_conv_w})
    dn1_dt = mm("d_dt_x", ddt_b, W["dt"], "nt")
    dn1_gate, *from_sibling = mm("d_gate_x", dgp_b, W["gate"], "nt", add=dn1_dt, b_slabs=True,
                                 side=exchange_plan([("pair", last[n]) for n in FIRST]))
    chip_sums = [pair_sum("pair_sum_" + n, last[n], s, core) for n, s in zip(FIRST, from_sibling)]
    dn1, *first_parts = mm("d_in_x", dmain_b, W["main"], "nt", add=dn1_gate,
                           side=exchange_plan([("chips", s) for s in chip_sums]))

    def b_pre(d1, dn, xv, g):
        _, vjp = jax.vjp(_rms, xv, g)
        dx, dg = vjp(dn)
        return d1 + dx, dg

    grad_x, g_mix_pre = rowwise("d_norm_pre", b_pre, [full(dh1), full(dn1), full(x)], [full(P["norm_mix_pre"])],
                                [(D, D, F32)], [(D, D)])

    parts = {**dict(zip(LATE, late_parts)), **dict(zip(FIRST, first_parts))}
    gS = {
        "norm_mix_pre": g_mix_pre, "conv_b": g_conv_b, "dt_bias_pad": g_dt_bias_pad, "a_exp": g_a_exp,
        "dsk_exp": g_dsk_exp, "ssd_norm": g_ssd_norm, "b_gate": g_b_gate,
        "norm_mix_post": g_mix_post, "norm_ffn_pre": g_ffn_pre, "norm_ffn_post": g_ffn_post,
        "norm_ple_post": g_ple,
    }
    return lossc, grad_x, parts, conv_slabs, gS


IN_SPLITS = (2048, 6144, 6176, 7200, 8224)
IN_SHARD = 1156
IN_PAD = 1280
COL_SHARDED = {"w_in": (1024, IN_SHARD), "conv_w": (4, 512), "w_gate": (1024, 256), "w_ff1": (1024, 512),
               "w_ple": (256, 128)}
ROW_SHARDED = {"w_ssd_branch": (256, 1024), "w_sb_branch": (128, 1024), "w_out": (128, 1024),
               "w_ff2": (512, 1024), "w_ple_gate": (128, 1024)}
SHARDED = tuple(COL_SHARDED) + tuple(ROW_SHARDED)
SMALL = (
    ("norm_mix_pre", 1024), ("conv_b", 4096), ("dt_bias", 32), ("a_log", 32), ("d_skip", 32), ("ssd_norm", 2048),
    ("b_gate", 2048), ("norm_mix_post", 1024), ("norm_ffn_pre", 1024), ("norm_ffn_post", 1024),
    ("norm_ple_post", 1024),
)
ROW = 1024
SMALL_ROWS = 16


def _rows_of(n):
    return -(-n // ROW)


def _pad_to(v, n, axis=-1):
    pad = [(0, 0)] * v.ndim
    pad[axis] = (0, n - v.shape[axis])
    return jnp.pad(v, pad)


FIRST = ("w_in", "w_gate")
LATE = ("w_ssd_branch", "w_sb_branch", "w_out", "w_ff1", "w_ff2", "w_ple", "w_ple_gate")
LATE_SPLIT = 3


def in_matrices(w_in_slabs, w_gate_slabs):
    w_in = jnp.concatenate([w_in_slabs[k, :, :IN_SHARD] for k in range(N_DEV)], axis=1)
    return {
        "main": jnp.concatenate([w_in[:, :IN_SPLITS[1]], w_in[:, IN_SPLITS[2]:]], axis=1),
        "dt": _pad_to(w_in[:, IN_SPLITS[1]:IN_SPLITS[2]], LANES),
        "gate": w_gate_slabs,
    }


def late_matrices(g):
    short = {"w_ff1": "ff1", "w_ple": "ple", "w_ssd_branch": "ssd", "w_sb_branch": "sb", "w_out": "out",
             "w_ff2": "ff2", "w_ple_gate": "pg"}
    stack = lambda a: a.reshape(a.shape[0] * a.shape[1], a.shape[2])
    return {short[n]: a if n in COL_SHARDED else stack(a) for n, a in g.items()}


def build_small(small, conv_w_slabs):
    P = {k: small[k] for k in ("norm_mix_pre", "conv_b", "ssd_norm", "b_gate", "norm_mix_post", "norm_ffn_pre",
                               "norm_ffn_post", "norm_ple_post")}
    P["conv_w"] = jnp.concatenate([conv_w_slabs[k] for k in range(N_DEV)], axis=1)
    P["dt_bias_pad"] = _pad_to(small["dt_bias"], LANES)
    P["a_exp"] = jnp.repeat(-jnp.exp(small["a_log"]), HEAD_DIM, axis=1)
    P["dsk_exp"] = jnp.repeat(small["d_skip"], HEAD_DIM, axis=1)
    return P


def small_grads(gS, small):
    heads = lambda a: a.reshape(SSD_HEADS, HEAD_DIM).sum(axis=1)[None, :]
    out = {k: gS[k] for k in ("norm_mix_pre", "conv_b", "ssd_norm", "b_gate", "norm_mix_post", "norm_ffn_pre",
                              "norm_ffn_post", "norm_ple_post")}
    out["dt_bias"] = gS["dt_bias_pad"][:, :SSD_HEADS]
    out["a_log"] = heads(gS["a_exp"]) * (-jnp.exp(small["a_log"]))
    out["d_skip"] = heads(gS["dsk_exp"])
    return out


def in_grad_slabs(g):
    pieces = ((0, IN_SPLITS[1], g["main"], 0), (IN_SPLITS[1], IN_SPLITS[2], g["dt"], -IN_SPLITS[1]),
              (IN_SPLITS[2], N_DEV * IN_SHARD, g["main"], IN_SPLITS[1] - IN_SPLITS[2]))
    slabs = []
    for k in range(N_DEV):
        a, b = IN_SHARD * k, IN_SHARD * (k + 1)
        cut = [src[:, max(a, lo) + off:min(b, hi) + off] for lo, hi, src, off in pieces if max(a, lo) < min(b, hi)]
        slabs.append(_pad_to(jnp.concatenate(cut, axis=1), IN_PAD))
    width = COL_SHARDED["conv_w"][1]
    return jnp.stack(slabs), jnp.stack([g["conv_w"][:, width * k:width * (k + 1)] for k in range(N_DEV)])


def pack_small(get):
    cols = [_pad_to(get(name).reshape(n), _rows_of(n) * ROW) for name, n in SMALL]
    return jnp.concatenate(cols).reshape(SMALL_ROWS, ROW)


def unpack_small(flat):
    out, r0 = {}, 0
    for name, n in SMALL:
        rows = _rows_of(n)
        out[name] = flat[r0:r0 + rows].reshape(rows * ROW)[:n].reshape(1, n)
        r0 += rows
    return out


N_CHIPS = 4
JOB_SEMS = {"all": 7, "scatter": 7, "gather2": 7, "pair": 4, "chips": 3}


def exchange_plan(jobs):
    n = len(jobs)
    kinds = [k for k, _ in jobs]
    srcs = [s for _, s in jobs]
    shapes = {"all": lambda s: (N_DEV,) + s.shape, "gather2": lambda s: (N_DEV,) + s.shape,
              "scatter": lambda s: s.shape, "pair": lambda s: (N_CHIPS,) + s.shape[1:], "chips": lambda s: s.shape}
    out_shape = [jax.ShapeDtypeStruct(shapes[k](s), s.dtype) for k, s in jobs]
    offs = [sum(JOB_SEMS[k] for k in kinds[:i]) for i in range(n + 1)]

    def run(phases, src, out, send_sems, recv_sems, local_sems):
        x, y, c = lax.axis_index("x"), lax.axis_index("y"), lax.axis_index("c")
        dev = lambda d: 4 * d[0] + 2 * d[1] + d[2]
        chip_no = lambda ch: 2 * ch[0] + ch[1]
        me, sib, my_chip = (x, y, c), (x, y, 1 - c), (x, y)
        others = [(1 - x, y), (x, 1 - y), (1 - x, 1 - y)]
        all_chips = [(0, 0), (0, 1), (1, 0), (1, 1)]
        peers = [(1 - x if k & 4 else x, 1 - y if k & 2 else y, 1 - c if k & 1 else c) for k in range(1, N_DEV)]
        starts, recvs, local = [], [], []
        chained = [[] for _ in others]

        for i, kind in enumerate(kinds):
            s_ref, o_ref = src[i], out[i]

            def rc(k, src_ref, dst_ref, to, i=i):
                s = offs[i] + k
                return pltpu.make_async_remote_copy(src_ref=src_ref, dst_ref=dst_ref, send_sem=send_sems.at[s],
                                                    recv_sem=recv_sems.at[s], device_id=to,
                                                    device_id_type=pl.DeviceIdType.MESH)

            if kind == "all":
                local.append(pltpu.make_async_copy(s_ref, o_ref.at[dev(me)], local_sems.at[i]))
                for k, peer in enumerate(peers):
                    starts.append(rc(k, s_ref, o_ref.at[dev(me)], peer))
                    recvs.append(rc(k, s_ref, o_ref.at[dev(peer)], peer))
            elif kind == "scatter":
                local.append(pltpu.make_async_copy(s_ref.at[dev(me)], o_ref.at[dev(me)], local_sems.at[i]))
                for k, peer in enumerate(peers):
                    starts.append(rc(k, s_ref.at[dev(peer)], o_ref.at[dev(me)], peer))
                    recvs.append(rc(k, s_ref.at[dev(me)], o_ref.at[dev(peer)], peer))
            elif kind == "gather2":
                local.append(pltpu.make_async_copy(s_ref, o_ref.at[dev(me)], local_sems.at[i]))
                starts.append(rc(0, s_ref, o_ref.at[dev(me)], sib))
                recvs.append(rc(0, s_ref, o_ref.at[dev(sib)], sib))
                for j, ch in enumerate(others):
                    same, other = (*ch, c), (*ch, 1 - c)
                    starts.append(rc(1 + j, s_ref, o_ref.at[dev(me)], same))
                    chained[j].append((rc(1 + j, s_ref, o_ref.at[dev(same)], same),
                                       rc(4 + j, o_ref.at[dev(same)], o_ref.at[dev(same)], sib)))
                    recvs.append(rc(4 + j, s_ref, o_ref.at[dev(other)], sib))
            elif kind == "pair":
                for j, ch in enumerate(all_chips):
                    starts.append(rc(j, s_ref.at[dev((*ch, 1 - c))], o_ref.at[j], sib))
                    recvs.append(rc(j, s_ref.at[dev((*ch, c))], o_ref.at[j], sib))
            else:
                mine = chip_no(my_chip)
                local.append(pltpu.make_async_copy(s_ref.at[mine], o_ref.at[mine], local_sems.at[i]))
                for j, ch in enumerate(others):
                    starts.append(rc(j, s_ref.at[chip_no(ch)], o_ref.at[mine], (*ch, c)))
                    recvs.append(rc(j, s_ref.at[mine], o_ref.at[chip_no(ch)], (*ch, c)))

        if "start" in phases:
            for cp in local + starts:
                cp.start()
        if "pass" in phases:
            for group in chained:
                for arrival, forward in group:
                    arrival.wait_recv()
                    forward.start()
        if "finish" in phases:
            for cp in recvs:
                cp.wait_recv()
            for cp in starts + [forward for group in chained for _, forward in group]:
                cp.wait_send()
            for cp in local:
                cp.wait()

    scratch = [pltpu.SemaphoreType.DMA((offs[n],)), pltpu.SemaphoreType.DMA((offs[n],)),
               pltpu.SemaphoreType.DMA((n,))]
    return srcs, out_shape, scratch, run


EXCHANGE_PHASES = ("start", "pass", "finish")


def exchange(name, jobs):
    srcs, out_shape, scratch, run = exchange_plan(jobs)
    n = len(srcs)

    def body(*refs):
        run(EXCHANGE_PHASES, refs[:n], refs[n:2 * n], *refs[2 * n:])

    any_spec = pl.BlockSpec(memory_space=pl.ANY)
    return pl.pallas_call(body, name=name, out_shape=out_shape, in_specs=[any_spec] * n, out_specs=[any_spec] * n,
                          scratch_shapes=scratch)(*srcs)


def host_exchange(body, n_in, n_out, grid, plan):
    if plan is None:
        return body, [], [], [], [], []
    srcs, out_shape, scratch, run = plan
    n = len(srcs)
    steps = 1
    for g in grid:
        steps *= g

    def hosted(*refs):
        ins, side_in = refs[:n_in], refs[n_in:n_in + n]
        outs, side_out = refs[n_in + n:n_in + n + n_out], refs[n_in + n + n_out:n_in + 2 * n + n_out]
        rest = refs[n_in + 2 * n + n_out:]
        own, sems = rest[:len(rest) - 3], rest[len(rest) - 3:]
        step = 0
        for d, g in enumerate(grid):
            step = step * g + pl.program_id(d)

        def at(when, phase):
            @pl.when(step == when)
            def _():
                run((phase,), side_in, side_out, *sems)

        at(0, "start")
        body(*ins, *outs, *own)
        at(steps * 13 // 16, "pass")
        at(steps - 1, "finish")

    any_spec = pl.BlockSpec(memory_space=pl.ANY)
    return hosted, [any_spec] * n, list(srcs), list(out_shape), [any_spec] * n, list(scratch)


def pair_sum(name, g, sib, core):
    _, r, c = g.shape
    tr = _pick(r, (256, 128, 64, 32, 16, 8))

    def body(core_ref, g_ref, s_ref, o_ref):
        o_ref[...] = (g_ref[...] + s_ref[...]).astype(o_ref.dtype)

    return pl.pallas_call(
        body,
        name=name,
        out_shape=jax.ShapeDtypeStruct((N_CHIPS, r, c), MXU),
        grid_spec=pltpu.PrefetchScalarGridSpec(
            num_scalar_prefetch=1,
            grid=(N_CHIPS, r // tr),
            in_specs=[
                pl.BlockSpec((None, None, tr, c), lambda j, i, core_ref: (j, core_ref[0], i, 0)),
                pl.BlockSpec((None, tr, c), lambda j, i, core_ref: (j, i, 0)),
            ],
            out_specs=pl.BlockSpec((None, tr, c), lambda j, i, core_ref: (j, i, 0)),
        ),
        compiler_params=_cparams(("parallel", "parallel")),
    )(core, g.reshape(N_CHIPS, 2, r, c), sib)


def adamw(name, parts, w, m, v):
    rows, cols = w.shape
    tr = _pick(rows, (128, 64, 32, 16, 8))

    nparts = parts.shape[0]

    def body(p_ref, w_ref, m_ref, v_ref, g_ref, d_ref, m2_ref, v2_ref):
        g = p_ref[0].astype(F32)
        for k in range(1, nparts):
            g = g + p_ref[k].astype(F32)
        m2 = ADAM_B1 * m_ref[...] + (1.0 - ADAM_B1) * g
        v2 = ADAM_B2 * v_ref[...] + (1.0 - ADAM_B2) * jnp.square(g)
        m_hat = m2 / (1.0 - ADAM_B1 ** ADAM_STEP)
        v_hat = v2 / (1.0 - ADAM_B2 ** ADAM_STEP)
        g_ref[...] = g
        d_ref[...] = -ADAM_LR * (m_hat / (jnp.sqrt(v_hat) + ADAM_EPS) + ADAM_WD * w_ref[...])
        m2_ref[...] = m2
        v2_ref[...] = v2

    spec = pl.BlockSpec((tr, cols), lambda i: (i, 0))
    return pl.pallas_call(
        body,
        name=name,
        out_shape=[jax.ShapeDtypeStruct((rows, cols), F32)] * 4,
        grid=(rows // tr,),
        in_specs=[pl.BlockSpec((nparts, tr, cols), lambda i: (0, i, 0)), spec, spec, spec],
        out_specs=[spec] * 4,
        compiler_params=_cparams(("parallel",)),
    )(parts, w, m, v)


WEIGHT_ORDER = (
    "norm_mix_pre", "w_in", "conv_w", "conv_b", "dt_bias", "a_log", "d_skip", "ssd_norm", "w_ssd_branch",
    "w_sb_branch", "w_gate", "b_gate", "w_out", "norm_mix_post", "norm_ffn_pre", "w_ff1", "w_ff2", "norm_ffn_post",
    "w_ple", "w_ple_gate", "norm_ple_post",
)


def kernel(x, p, norm_mix_pre, w_in, conv_w, conv_b, dt_bias, a_log, d_skip, ssd_norm, w_ssd_branch, w_sb_branch, w_gate, b_gate, w_out, norm_mix_post, norm_ffn_pre, w_ff1, w_ff2, norm_ffn_post, w_ple, w_ple_gate, norm_ple_post, loss_target, m_norm_mix_pre, m_w_in, m_conv_w, m_conv_b, m_dt_bias, m_a_log, m_d_skip, m_ssd_norm, m_w_ssd_branch, m_w_sb_branch, m_w_gate, m_b_gate, m_w_out, m_norm_mix_post, m_norm_ffn_pre, m_w_ff1, m_w_ff2, m_norm_ffn_post, m_w_ple, m_w_ple_gate, m_norm_ple_post, v_norm_mix_pre, v_w_in, v_conv_w, v_conv_b, v_dt_bias, v_a_log, v_d_skip, v_ssd_norm, v_w_ssd_branch, v_w_sb_branch, v_w_gate, v_b_gate, v_w_out, v_norm_mix_post, v_norm_ffn_pre, v_w_ff1, v_w_ff2, v_norm_ffn_post, v_w_ple, v_w_ple_gate, v_norm_ple_post):
    a = dict(locals())
    seq = x.shape[1]
    x2 = x.reshape(seq, D_MODEL)
    target = loss_target.reshape(seq, D_MODEL)
    pb = p.reshape(seq, PLE_DIM).astype(MXU)

    def shard(prefix, name, dtype):
        v = a[prefix + name][0].astype(dtype)
        return _pad_to(v, IN_PAD) if name == "w_in" else v

    w_in_slabs, w_gate_slabs, conv_w_slabs = exchange(
        "gather_weights", [("gather2", shard("", "w_in", MXU)), ("gather2", shard("", "w_gate", MXU)),
                           ("all", a["conv_w"][0])])
    small = {n: a[n] for n, _ in SMALL}
    core = lax.axis_index("c").astype(jnp.int32).reshape(1)

    lossc, grad_x, parts, conv_slabs, g_acc = local_step(
        x2, pb, target, in_matrices(w_in_slabs, w_gate_slabs), build_small(small, conv_w_slabs),
        {n: shard("", n, MXU) for n in LATE}, core)
    loss = lax.psum(jnp.sum(lossc), ("x", "y", "c"))
    g_small = small_grads(g_acc, small)

    parts["conv_w"], small_parts = exchange(
        "small_grads", [("scatter", conv_slabs), ("all", pack_small(lambda n: g_small[n]))])

    leaves = {}
    for n, part in parts.items():
        res = adamw("adamw_" + n, part, shard("", n, F32), shard("m_", n, F32), shard("v_", n, F32))
        leaves[n] = [r[None, :, :IN_SHARD] if n == "w_in" else r[None] for r in res]
    res = adamw("adamw_small", small_parts, pack_small(lambda n: a[n]), pack_small(lambda n: a["m_" + n]),
                pack_small(lambda n: a["v_" + n]))
    for j, r in enumerate(res):
        for n, leaf in unpack_small(r).items():
            leaves.setdefault(n, [None] * 4)[j] = leaf
    outs = [loss, grad_x.reshape(x.shape)]
    for j in range(4):
        outs += [leaves[n][j] for n in WEIGHT_ORDER]
    return tuple(outs)
```

```python
import functools

import jax
import jax.numpy as jnp
from jax import lax
from jax.experimental import pallas as pl
from jax.experimental.pallas import tpu as pltpu

F32 = jnp.float32
MXU = jnp.bfloat16
VMEM_LIMIT = 56 * 1024 * 1024

D_MODEL = 1024
D_INNER = 2048
SSD_HEADS = 32
HEAD_DIM = 64
SSD_GROUPS = 8
D_STATE = 128
CONV_K = 4
CONV_DIM = 4096
CHUNK = 128
SB_WIDTH = 1024
D_FF = 4096
PLE_DIM = 256
RMS_EPS = 1e-6
SB_SCALE = HEAD_DIM ** -0.5
N_DEV = 8
LANES = 128

OFF_Z, OFF_XBC, OFF_Q, OFF_K, OFF_V = 0, 2048, 6144, 7168, 8192

ADAM_LR = 0.001
ADAM_B1 = 0.9
ADAM_B2 = 0.999
ADAM_EPS = 1e-08
ADAM_WD = 0.01
ADAM_STEP = 10


def _sig(x):
    return 0.5 * jnp.tanh(0.5 * x) + 0.5


def _softplus(x):
    return jnp.maximum(x, 0.0) + jnp.log(1.0 + jnp.exp(-jnp.abs(x)))


def _rms(x, w):
    return x * lax.rsqrt(jnp.mean(x * x, axis=-1, keepdims=True) + RMS_EPS) * w


def _dot(a, b):
    return jnp.dot(a, b, preferred_element_type=F32)


def _dot_nt(a, b):
    return lax.dot_general(a, b, (((1,), (1,)), ((), ())), preferred_element_type=F32)


def _dot_tn(a, b):
    return lax.dot_general(a, b, (((0,), (0,)), ((), ())), preferred_element_type=F32)


def _split3(x):
    x1 = x.astype(MXU)
    r = x - x1.astype(F32)
    x2 = r.astype(MXU)
    r = r - x2.astype(F32)
    return x1, x2, r.astype(MXU)


def _dot3_l(a, u, parts=3):
    m = a.shape[0]
    d = _dot(jnp.concatenate(_split3(a)[:parts], axis=0), u)
    if parts == 2:
        return d[m:] + d[:m]
    return (d[2 * m:] + d[m:2 * m]) + d[:m]


def _dot3_r(u, a):
    n = a.shape[1]
    d = _dot(u, jnp.concatenate(_split3(a), axis=1))
    return (d[:, 2 * n:] + d[:, n:2 * n]) + d[:, :n]


def _iota(shape, dim):
    return lax.broadcasted_iota(jnp.int32, shape, dim)


def _tri(n, cmp):
    r, c = _iota((n, n), 0), _iota((n, n), 1)
    return cmp(r, c).astype(F32).astype(MXU)


def _cparams(sem):
    return pltpu.CompilerParams(dimension_semantics=sem, vmem_limit_bytes=VMEM_LIMIT)


def _pick(n, cands):
    for c in cands:
        if n % c == 0:
            return c
    return n


def mm(name, a, b, mode, add=None, out_dtype=F32, b_slabs=False, out_slabs=False, epi=None, epi_args=(), extra=(),
       side=None):
    slab = None
    if b_slabs:
        slab = b.shape[2]
        bshape = (b.shape[1], N_DEV * slab)
    else:
        bshape = b.shape
    if mode == "nn":
        (M, K), (K2, N) = a.shape, bshape
    elif mode == "nt":
        (M, K), (N, K2) = a.shape, bshape
    else:
        (K, M), (K2, N) = a.shape, bshape
    assert K == K2, (name, a.shape, b.shape)
    tm = _pick(M, (1024, 512, 256, 128))
    tn = _pick(N, (1024, 512, 256, 128))
    tk = _pick(K, (1024, 512, 256, 128))
    if b_slabs and mode == "nn":
        tn = slab
    if b_slabs and mode == "nt":
        tk = slab
    if out_slabs:
        assert mode == "tn" and N % N_DEV == 0
        tn = N // N_DEV
    nk = K // tk

    def body(*refs):
        refs = list(refs)
        a_ref, b_ref = refs[:2]
        add_ref = refs[2] if add is not None else None
        n_in = 2 + (add is not None)
        epi_refs = refs[n_in:n_in + len(epi_args)]
        o_ref = refs[n_in + len(epi_args)]
        extra_refs = refs[n_in + len(epi_args) + 1:-1]
        acc = refs[-1]
        k = pl.program_id(2)

        @pl.when(k == 0)
        def _():
            acc[...] = jnp.zeros_like(acc) if add is None else add_ref[...]

        av, bv = a_ref[...], b_ref[...]
        if mode == "nn":
            acc[...] += _dot(av, bv)
        elif mode == "nt":
            acc[...] += _dot_nt(av, bv)
        else:
            acc[...] += _dot_tn(av, bv)

        @pl.when(k == nk - 1)
        def _():
            res = acc[...]
            main = res if epi is None else epi(res, *[r[...] for r in epi_refs])
            o_ref[...] = main.astype(o_ref.dtype)
            for r, (_, fn) in zip(extra_refs, extra):
                r[...] = fn(res).astype(r.dtype)

    if mode == "nn":
        a_spec = pl.BlockSpec((tm, tk), lambda i, j, k: (i, k))
        b_spec = pl.BlockSpec((tk, tn), lambda i, j, k: (k, j))
    elif mode == "nt":
        a_spec = pl.BlockSpec((tm, tk), lambda i, j, k: (i, k))
        b_spec = pl.BlockSpec((tn, tk), lambda i, j, k: (j, k))
    else:
        a_spec = pl.BlockSpec((tk, tm), lambda i, j, k: (k, i))
        b_spec = pl.BlockSpec((tk, tn), lambda i, j, k: (k, j))
    if b_slabs and mode == "nn":
        b_spec = pl.BlockSpec((None, tk, tn), lambda i, j, k: (j, k, 0))
    if b_slabs and mode == "nt":
        b_spec = pl.BlockSpec((None, tn, tk), lambda i, j, k: (k, j, 0))
    o_spec = pl.BlockSpec((tm, tn), lambda i, j, k: (i, j))
    in_specs, args = [a_spec, b_spec], [a, b]
    if add is not None:
        in_specs.append(o_spec)
        args.append(add)
    for e in epi_args:
        in_specs.append(o_spec)
        args.append(e)
    out_sds = jax.ShapeDtypeStruct((M, N), out_dtype)
    if out_slabs:
        o_spec = pl.BlockSpec((None, tm, tn), lambda i, j, k: (j, i, 0))
        out_sds = jax.ShapeDtypeStruct((N_DEV, M, tn), out_dtype)
    grid = (M // tm, N // tn, nk)
    body, s_in, s_args, s_shape, s_out, s_scr = host_exchange(body, len(args), 1 + len(extra), grid, side)
    res = pl.pallas_call(
        body,
        name=name,
        out_shape=[out_sds] + [jax.ShapeDtypeStruct((M, N), dt) for dt, _ in extra] + s_shape,
        grid=grid,
        in_specs=in_specs + s_in,
        out_specs=[o_spec] * (1 + len(extra)) + s_out,
        scratch_shapes=[pltpu.VMEM((tm, tn), F32)] + s_scr,
        compiler_params=_cparams(("parallel", "parallel", "arbitrary") if side is None else ("arbitrary",) * 3),
    )(*args, *s_args)
    return res if extra or side is not None else res[0]


def rowwise(name, fn, rows, bcast, outs, accs=(), tr=512, ncb=1, side=None, groups=1):
    S = rows[0][0].shape[0]
    tr = min(tr, S)
    nrb = S // tr
    G = groups
    in_specs, args, in_w = [], [], []
    for arr, off, w in rows:
        assert off % (w * G) == 0 and arr.shape[0] == S
        in_specs.append(pl.BlockSpec((tr, w * G), lambda j, i, ob=off // (w * G): (i, ob + j)))
        args.append(arr)
        in_w.append(w)
    for arr, off, w in bcast:
        assert off % (w * G) == 0
        in_specs.append(pl.BlockSpec((arr.shape[0], w * G), lambda j, i, ob=off // (w * G): (0, ob + j)))
        args.append(arr)
        in_w.append(w)
    out_shape, out_specs, out_w = [], [], []
    for tw, w, dt in outs:
        out_shape.append(jax.ShapeDtypeStruct((S, tw), dt))
        out_specs.append(pl.BlockSpec((tr, w * G), lambda j, i: (i, j)))
        out_w.append(w)
    for tw, w in accs:
        out_shape.append(jax.ShapeDtypeStruct((1, tw), F32))
        out_specs.append(pl.BlockSpec((1, w * G), lambda j, i: (0, j)))
        out_w.append(w)
    nin, nout = len(args), len(outs)

    def body(*refs):
        i = pl.program_id(1)
        for g in range(G):
            cut = lambda w: slice(g * w, (g + 1) * w)
            res = fn(*[r[:, cut(w)] for r, w in zip(refs[:nin], in_w)])
            for k, (r, v) in enumerate(zip(refs[nin:], res)):
                cols = cut(out_w[k])
                if k < nout:
                    r[:, cols] = v.astype(r.dtype)
                    continue

                @pl.when(i == 0)
                def _(r=r, v=v, cols=cols):
                    r[:, cols] = v

                @pl.when(i > 0)
                def _(r=r, v=v, cols=cols):
                    r[:, cols] += v

    body, s_in, s_args, s_shape, s_out, s_scr = host_exchange(body, nin, len(out_shape), (ncb, nrb), side)
    res = pl.pallas_call(
        body,
        name=name,
        out_shape=out_shape + s_shape,
        grid=(ncb, nrb),
        in_specs=in_specs + s_in,
        out_specs=out_specs + s_out,
        scratch_shapes=s_scr,
        compiler_params=_cparams(("parallel" if side is None else "arbitrary", "arbitrary")),
    )(*args, *s_args)
    return res


CONV_TC = 128


CONV_R = 128
HALO = 8


def _conv_pre(e, w, b):
    shifted = [pltpu.roll(e, s, 0) for s in (1, 2, 3)]
    pre = b + w[3:4, :] * e
    for s in (1, 2, 3):
        pre = pre + w[3 - s:4 - s, :] * shifted[s - 1]
    return pre, shifted


def conv_fwd(projmain, conv_w, conv_b):
    S = projmain.shape[0]
    tc = CONV_TC

    def body(u_ref, w_ref, b_ref, o_ref):
        u, w = u_ref[...], w_ref[...]
        row = _iota(u.shape, 0)
        pre = b_ref[...] + w[3:4, :] * u
        for s in (1, 2, 3):
            pre = pre + w[3 - s:4 - s, :] * jnp.where(row >= s, pltpu.roll(u, s, 0), 0.0)
        o_ref[...] = pre * _sig(pre)

    return pl.pallas_call(
        body,
        name="conv_fwd",
        out_shape=jax.ShapeDtypeStruct((S, CONV_DIM), F32),
        grid=(CONV_DIM // tc,),
        in_specs=[
            pl.BlockSpec((S, tc), lambda j: (0, OFF_XBC // tc + j)),
            pl.BlockSpec((CONV_K, tc), lambda j: (0, j)),
            pl.BlockSpec((1, tc), lambda j: (0, j)),
        ],
        out_specs=pl.BlockSpec((S, tc), lambda j: (0, j)),
        compiler_params=_cparams(("parallel",)),
    )(projmain, conv_w, conv_b)


def conv_bwd(dxs, dxs_skip, d_b, d_c, projmain, conv_w, conv_b):
    S = projmain.shape[0]
    tc = CONV_TC
    n_xs, n_b = D_INNER // tc, SSD_GROUPS * D_STATE // tc

    def body(dx_ref, dskip_ref, dbm_ref, dcm_ref, u_ref, w_ref, b_ref, du_ref, dw_ref, db_ref, u_pad, d_pad):
        j = pl.program_id(0)
        zeros = jnp.zeros((HALO, tc), F32)
        for pad in (u_pad, d_pad):
            pad[0:HALO, :] = zeros
            pad[HALO + S:2 * HALO + S, :] = zeros
        u_pad[HALO:HALO + S, :] = u_ref[...]
        d_pad[HALO:HALO + S, :] = jnp.where(j < n_xs, dx_ref[...] + dskip_ref[...],
                                            jnp.where(j < n_xs + n_b, dbm_ref[...], dcm_ref[...]))
        w, b = w_ref[...], b_ref[...]
        n = CONV_R + 2 * HALO
        keep = slice(HALO, HALO + CONV_R)

        def chunk(c, sums):
            r0 = pl.multiple_of(c * CONV_R, CONV_R)
            e = u_pad[pl.ds(r0, n), :]
            pre, shifted = _conv_pre(e, w, b)
            sg = _sig(pre)
            dpre = d_pad[pl.ds(r0, n), :] * (sg * (1.0 + pre * (1.0 - sg)))
            du = w[3:4, :] * dpre
            for s in (1, 2, 3):
                du = du + w[3 - s:4 - s, :] * pltpu.roll(dpre, n - s, 0)
            du_ref[pl.ds(r0, CONV_R), :] = du[keep].astype(du_ref.dtype)
            dk = dpre[keep]
            taps = [shifted[2], shifted[1], shifted[0], e]
            return tuple(acc + jnp.sum(dk * t[keep], axis=0, keepdims=True) for acc, t in zip(sums[:4], taps)) + (
                sums[4] + jnp.sum(dk, axis=0, keepdims=True),)

        zero = jnp.zeros((1, tc), F32)
        sums = lax.fori_loop(0, S // CONV_R, chunk, (zero,) * 5)
        for k in range(CONV_K):
            dw_ref[k:k + 1, :] = sums[k]
        db_ref[...] = sums[4]

    return pl.pallas_call(
        body,
        name="conv_bwd",
        out_shape=[
            jax.ShapeDtypeStruct((S, CONV_DIM), MXU),
            jax.ShapeDtypeStruct((CONV_K, CONV_DIM), F32),
            jax.ShapeDtypeStruct((1, CONV_DIM), F32),
        ],
        grid=(CONV_DIM // tc,),
        in_specs=[
            pl.BlockSpec((S, tc), lambda j: (0, jnp.minimum(j, n_xs - 1))),
            pl.BlockSpec((S, tc), lambda j: (0, jnp.minimum(j, n_xs - 1))),
            pl.BlockSpec((S, tc), lambda j: (0, jnp.clip(j - n_xs, 0, n_b - 1))),
            pl.BlockSpec((S, tc), lambda j: (0, jnp.clip(j - n_xs - n_b, 0, n_b - 1))),
            pl.BlockSpec((S, tc), lambda j: (0, OFF_XBC // tc + j)),
            pl.BlockSpec((CONV_K, tc), lambda j: (0, j)),
            pl.BlockSpec((1, tc), lambda j: (0, j)),
        ],
        out_specs=[
            pl.BlockSpec((S, tc), lambda j: (0, j)),
            pl.BlockSpec((CONV_K, tc), lambda j: (0, j)),
            pl.BlockSpec((1, tc), lambda j: (0, j)),
        ],
        scratch_shapes=[pltpu.VMEM((S + 2 * HALO, tc), F32)] * 2,
        compiler_params=_cparams(("arbitrary",)),
    )(dxs, dxs_skip, d_b, d_c, projmain, conv_w, conv_b)


def _head_expand():
    r, j = _iota((LANES, D_INNER), 0), _iota((LANES, D_INNER), 1)
    return ((j >= r * HEAD_DIM) & (j < r * HEAD_DIM + HEAD_DIM)).astype(F32).astype(MXU)


def _head_reduce():
    j, r = _iota((D_INNER, LANES), 0), _iota((D_INNER, LANES), 1)
    return ((j >= r * HEAD_DIM) & (j < r * HEAD_DIM + HEAD_DIM)).astype(F32).astype(MXU)


def ssd_prep(dtraw, dt_bias_pad, a_exp):
    S = dtraw.shape[0]

    def body(dtr_ref, bias_ref, a_ref, dte_ref, cse_ref):
        dt = _softplus(dtr_ref[...] + bias_ref[...])
        dte = _dot3_l(dt, _head_expand())
        dte_ref[...] = dte
        incl = _tri(CHUNK, lambda r, c: r >= c)
        cse_ref[...] = _dot3_r(incl, dte * a_ref[...])

    return pl.pallas_call(
        body,
        name="ssd_prep",
        out_shape=[jax.ShapeDtypeStruct((S, D_INNER), F32)] * 2,
        grid=(S // CHUNK,),
        in_specs=[
            pl.BlockSpec((CHUNK, LANES), lambda c: (c, 0)),
            pl.BlockSpec((1, LANES), lambda c: (0, 0)),
            pl.BlockSpec((1, D_INNER), lambda c: (0, 0)),
        ],
        out_specs=[pl.BlockSpec((CHUNK, D_INNER), lambda c: (c, 0))] * 2,
        compiler_params=_cparams(("parallel",)),
    )(dtraw, dt_bias_pad, a_exp)


GW = 4 * HEAD_DIM
SSD_GPS = 8
SSD_GPS_BWD = 8


def ssd_fwd(xbc_act, dte, cse, side=None):
    S = xbc_act.shape[0]
    nc = S // CHUNK

    def body(xs_ref, b_ref, c_ref, dte_ref, cse_ref, y_ref, st_ref, s_scr):
        c = pl.program_id(1)

        @pl.when(c == 0)
        def _():
            s_scr[...] = jnp.zeros_like(s_scr)

        for gg in range(SSD_GPS):
            ch, st = slice(GW * gg, GW * (gg + 1)), slice(D_STATE * gg, D_STATE * (gg + 1))
            s_in = s_scr[:, ch]
            st_ref[0, :, ch] = s_in
            cs = cse_ref[:, ch]
            xd = xs_ref[:, ch] * dte_ref[:, ch]
            btb = b_ref[:, st].T.astype(MXU)
            cs_last = cs[CHUNK - 1:CHUNK, :]
            gy = _dot(c_ref[:, st].astype(MXU), jnp.concatenate([btb, s_in.astype(MXU)], axis=1))
            g, y = gy[:, :CHUNK], gy[:, CHUNK:] * jnp.exp(cs)
            cs_t = cs.T
            row, col = _iota((CHUNK, CHUNK), 0), _iota((CHUNK, CHUNK), 1)
            ms = []
            for h in range(4):
                lo = HEAD_DIM * h
                lam = jnp.where(row >= col, jnp.exp(cs[:, lo:lo + 1] - cs_t[lo:lo + 1, :]), 0.0)
                ms.append((g * lam).astype(MXU))
            yd = _dot(jnp.concatenate(ms, axis=0), xd.astype(MXU))
            lane = _iota((CHUNK, GW), 1)
            for h in range(4):
                y = y + jnp.where((lane >= HEAD_DIM * h) & (lane < HEAD_DIM * (h + 1)),
                                  yd[CHUNK * h:CHUNK * (h + 1)], 0.0)
            y_ref[:, ch] = y
            w = (xd * jnp.exp(cs_last - cs)).astype(MXU)
            s_scr[:, ch] = jnp.exp(cs_last) * s_in + _dot(btb, w)

    grid = (SSD_GROUPS // SSD_GPS, nc)
    gw, gs = GW * SSD_GPS, D_STATE * SSD_GPS
    body, s_in, s_args, s_shape, s_out, s_scr = host_exchange(body, 5, 2, grid, side)
    return pl.pallas_call(
        body,
        name="ssd_fwd",
        out_shape=[
            jax.ShapeDtypeStruct((S, D_INNER), F32),
            jax.ShapeDtypeStruct((nc, D_STATE, D_INNER), F32),
        ] + s_shape,
        grid=grid,
        in_specs=[
            pl.BlockSpec((CHUNK, gw), lambda g, c: (c, g)),
            pl.BlockSpec((CHUNK, gs), lambda g, c: (c, D_INNER // gs + g)),
            pl.BlockSpec((CHUNK, gs), lambda g, c: (c, (D_INNER + SSD_GROUPS * D_STATE) // gs + g)),
            pl.BlockSpec((CHUNK, gw), lambda g, c: (c, g)),
            pl.BlockSpec((CHUNK, gw), lambda g, c: (c, g)),
        ] + s_in,
        out_specs=[
            pl.BlockSpec((CHUNK, gw), lambda g, c: (c, g)),
            pl.BlockSpec((1, D_STATE, gw), lambda g, c: (c, 0, g)),
        ] + s_out,
        scratch_shapes=[pltpu.VMEM((D_STATE, gw), F32)] + s_scr,
        compiler_params=_cparams(("parallel" if side is None else "arbitrary", "arbitrary")),
    )(xbc_act, xbc_act, xbc_act, dte, cse, *s_args)


def ssd_bwd(dy, xbc_act, dte, cse, states, a_exp, side=None):
    S = xbc_act.shape[0]
    nc = S // CHUNK

    def body(*refs):
        for gg in range(SSD_GPS_BWD):
            ch, st = slice(GW * gg, GW * (gg + 1)), slice(D_STATE * gg, D_STATE * (gg + 1))
            cut = {GW * SSD_GPS_BWD: ch, D_STATE * SSD_GPS_BWD: st}
            one_group(*[r.at[(slice(None),) * (len(r.shape) - 1) + (cut[r.shape[-1]],)] for r in refs])

    def one_group(dy_ref, xs_ref, b_ref, c_ref, dte_ref, cse_ref, sin_ref, sout_ref, a_ref,
                  dxs_ref, db_ref, dc_ref, ddt_ref, dal_ref, ds_scr):
        j = pl.program_id(1)

        @pl.when(j == 0)
        def _():
            ds_scr[...] = jnp.zeros_like(ds_scr)
            dal_ref[...] = jnp.zeros_like(dal_ref)

        ds_out = ds_scr[...]
        dyv, xs = dy_ref[...], xs_ref[...]
        dt, cs = dte_ref[...], cse_ref[...]
        s_in = sin_ref[0]
        bm, cm = b_ref[...], c_ref[...]
        bb, cb = bm.astype(MXU), cm.astype(MXU)
        btb, ctb = bm.T.astype(MXU), cm.T.astype(MXU)
        dsb, sib = ds_out.astype(MXU), s_in.astype(MXU)
        xd = xs * dt
        ecs = jnp.exp(cs)
        cs_last = cs[CHUNK - 1:CHUNK, :]
        eend = jnp.exp(cs_last - cs)
        gy = _dot(cb, jnp.concatenate([btb, sib], axis=1))
        g, yoff = gy[:, :CHUNK], gy[:, CHUNK:] * ecs
        gd = _dot(bb, jnp.concatenate([ctb, dsb], axis=1))
        g_t, dxd_off = gd[:, :CHUNK], gd[:, CHUNK:] * eend
        cs_t = cs.T
        row, col = _iota((CHUNK, CHUNK), 0), _iota((CHUNK, CHUNK), 1)
        lane = _iota((CHUNK, GW), 1)
        heads = [(lane >= HEAD_DIM * h) & (lane < HEAD_DIM * (h + 1)) for h in range(4)]
        dyb, xdb = dyv.astype(MXU), xd.astype(MXU)
        dm_all = _dot_nt(jnp.concatenate([jnp.where(hm, dyv, 0.0) for hm in heads], axis=0).astype(MXU), xdb)
        dmt_all = _dot_nt(jnp.concatenate([jnp.where(hm, xd, 0.0) for hm in heads], axis=0).astype(MXU), dyb)
        lams, m_ts = [], []
        for h in range(4):
            lo = HEAD_DIM * h
            cs_col, cs_row = cs[:, lo:lo + 1], cs_t[lo:lo + 1, :]
            lams.append(jnp.where(row >= col, jnp.exp(cs_col - cs_row), 0.0))
            m_ts.append(g_t * jnp.where(col >= row, jnp.exp(cs_row - cs_col), 0.0))
        acc_all = _dot(jnp.concatenate(m_ts, axis=0).astype(MXU), dyb)
        dxd = dxd_off
        dg = jnp.zeros((CHUNK, CHUNK), F32)
        dcs = dyv * yoff - xd * dxd_off
        for h in range(4):
            blk = slice(CHUNK * h, CHUNK * (h + 1))
            dm, dm_t = dm_all[blk], dmt_all[blk]
            dxd = dxd + jnp.where(heads[h], acc_all[blk], 0.0)
            dg = dg + dm * lams[h]
            wdiff = (jnp.sum(dm * (g * lams[h]), axis=1, keepdims=True)
                     - jnp.sum(dm_t * m_ts[h], axis=1, keepdims=True))
            dcs = dcs + jnp.where(lane == HEAD_DIM * h, wdiff, 0.0)
        dye = (dyv * ecs).astype(MXU)
        dc_ref[...] = _dot(dg.astype(MXU), bb) + _dot_nt(dye, sib)
        db_ref[...] = _dot(dg.T.astype(MXU), cb) + _dot_nt((xd * eend).astype(MXU), dsb)
        ds_scr[...] = jnp.exp(cs_last) * ds_out + _dot(ctb, dye)
        last = jnp.sum(ds_out * sout_ref[0], axis=0, keepdims=True)
        rows = _iota((CHUNK, GW), 0)
        dcs = dcs + jnp.where(rows == CHUNK - 1, last, 0.0)
        dda = _dot3_r(_tri(CHUNK, lambda r, c: c >= r), dcs)
        ddt_ref[...] = a_ref[...] * dda + dxd * xs
        dal_ref[...] += jnp.sum(dt * dda, axis=0, keepdims=True)
        dxs_ref[...] = dxd * dt

    rc = lambda g, j: (nc - 1 - j, g)
    grid = (SSD_GROUPS // SSD_GPS_BWD, nc)
    gw, gs = GW * SSD_GPS_BWD, D_STATE * SSD_GPS_BWD
    body, s_in, s_args, s_shape, s_out, s_scr = host_exchange(body, 9, 5, grid, side)
    return pl.pallas_call(
        body,
        name="ssd_bwd",
        out_shape=[
            jax.ShapeDtypeStruct((S, D_INNER), F32),
            jax.ShapeDtypeStruct((S, SSD_GROUPS * D_STATE), F32),
            jax.ShapeDtypeStruct((S, SSD_GROUPS * D_STATE), F32),
            jax.ShapeDtypeStruct((S, D_INNER), F32),
            jax.ShapeDtypeStruct((1, D_INNER), F32),
        ] + s_shape,
        grid=grid,
        in_specs=[
            pl.BlockSpec((CHUNK, gw), rc),
            pl.BlockSpec((CHUNK, gw), rc),
            pl.BlockSpec((CHUNK, gs), lambda g, j: (nc - 1 - j, D_INNER // gs + g)),
            pl.BlockSpec((CHUNK, gs), lambda g, j: (nc - 1 - j, (D_INNER + SSD_GROUPS * D_STATE) // gs + g)),
            pl.BlockSpec((CHUNK, gw), rc),
            pl.BlockSpec((CHUNK, gw), rc),
            pl.BlockSpec((1, D_STATE, gw), lambda g, j: (nc - 1 - j, 0, g)),
            pl.BlockSpec((1, D_STATE, gw), lambda g, j: (jnp.minimum(nc - j, nc - 1), 0, g)),
            pl.BlockSpec((1, gw), lambda g, j: (0, g)),
        ] + s_in,
        out_specs=[
            pl.BlockSpec((CHUNK, gw), rc),
            pl.BlockSpec((CHUNK, gs), rc),
            pl.BlockSpec((CHUNK, gs), rc),
            pl.BlockSpec((CHUNK, gw), rc),
            pl.BlockSpec((1, gw), lambda g, j: (0, g)),
        ] + s_out,
        scratch_shapes=[pltpu.VMEM((D_STATE, gw), F32)] + s_scr,
        compiler_params=_cparams(("parallel" if side is None else "arbitrary", "arbitrary")),
    )(dy, xbc_act, xbc_act, xbc_act, dte, cse, states, states, a_exp, *s_args)


SB_T = 256
SB_DROP = 104.0
SB_PAIRS = 1
SB_PARTS = 2
SB_W = SB_PAIRS * LANES
SB_LANES = [slice(LANES * p, LANES * (p + 1)) for p in range(SB_PAIRS)]


def _sb_scores(qm, k_ref, ks, rowi, coli, diag):
    kblk = k_ref[pl.ds(ks, SB_T), :].astype(MXU)
    z = _dot_nt(qm, kblk) * SB_SCALE
    sp = _softplus(z)
    if not diag:
        return kblk, z, None, sp, sp
    mask = (ks + coli) < rowi
    return kblk, z, mask, sp, jnp.where(mask, sp, 0.0)


def _sb_stack(v):
    lane = _iota(v.shape, 1)
    return jnp.concatenate([jnp.where(lane < HEAD_DIM, v, 0.0), jnp.where(lane >= HEAD_DIM, v, 0.0)], axis=0)


def _sb_unstack(v):
    lane = _iota((SB_T, LANES), 1)
    return jnp.where(lane < HEAD_DIM, v[:SB_T], v[SB_T:])


def _sb_rows(qb):
    r = _iota((2 * SB_T, SB_T), 0)
    return qb * SB_T + jnp.where(r >= SB_T, r - SB_T, r), _iota((2 * SB_T, SB_T), 1)


def sb_fwd(projmain, side=None):
    S = projmain.shape[0]
    nq = S // SB_T

    def body(q_ref, k_ref, v_ref, o_ref, t_ref, n_ref):
        hp, qb = pl.program_id(0), pl.program_id(1)
        qsts = [_sb_stack(q_ref[:, sl]).astype(MXU) for sl in SB_LANES]
        rowi, coli = _sb_rows(qb)
        u_after = _tri(SB_T, lambda r, c: r > c)

        def cond(carry):
            i, rmin, _, _ = carry
            return (i <= qb) & (rmin < SB_DROP)

        def kstep(carry, diag=False):
            i, _, rs, accs = carry
            ks = pl.multiple_of((qb - i) * SB_T, SB_T)
            rs, accs = list(rs), list(accs)
            for p, sl in enumerate(SB_LANES):
                _, z, mask, sp, spm = _sb_scores(qsts[p], k_ref.at[:, sl], ks, rowi, coli, diag)
                vblk = v_ref[pl.ds(ks, SB_T), sl].astype(MXU)
                a = jnp.exp(z - sp - _dot3_l(spm, u_after, SB_PARTS) - rs[p])
                if diag:
                    a = jnp.where(mask, a, 0.0)
                accs[p] = accs[p] + _dot(a.astype(MXU), vblk)
                rs[p] = rs[p] + jnp.sum(spm, axis=1, keepdims=True)
            return i + 1, jnp.min(functools.reduce(jnp.minimum, rs)), rs, accs

        first = kstep((jnp.int32(0), jnp.float32(0.0), [jnp.zeros((2 * SB_T, 1), F32)] * SB_PAIRS,
                       [jnp.zeros((2 * SB_T, LANES), F32)] * SB_PAIRS), diag=True)
        n, _, rs, accs = lax.while_loop(cond, kstep, first)
        for p, sl in enumerate(SB_LANES):
            o_ref[:, sl] = _sb_unstack(accs[p]).astype(o_ref.dtype)
            t_ref[:, sl] = _sb_unstack(jnp.broadcast_to(rs[p], (2 * SB_T, LANES)))
        n_ref[hp, qb] = n

    grid = (SB_WIDTH // SB_W, nq)
    body, s_in, s_args, s_shape, s_out, s_scr = host_exchange(body, 3, 3, grid, side)
    return pl.pallas_call(
        body,
        name="sb_fwd",
        out_shape=[jax.ShapeDtypeStruct((S, SB_WIDTH), MXU), jax.ShapeDtypeStruct((S, SB_WIDTH), F32),
                   jax.ShapeDtypeStruct((SB_WIDTH // SB_W, nq), jnp.int32)] + s_shape,
        grid=grid,
        in_specs=[
            pl.BlockSpec((SB_T, SB_W), lambda h, i: (i, OFF_Q // SB_W + h)),
            pl.BlockSpec((S, SB_W), lambda h, i: (0, OFF_K // SB_W + h)),
            pl.BlockSpec((S, SB_W), lambda h, i: (0, OFF_V // SB_W + h)),
        ] + s_in,
        out_specs=[pl.BlockSpec((SB_T, SB_W), lambda h, i: (i, h))] * 2
        + [pl.BlockSpec(memory_space=pltpu.SMEM)] + s_out,
        scratch_shapes=s_scr,
        compiler_params=_cparams(("arbitrary", "arbitrary")),
    )(projmain, projmain, projmain, *s_args)


def sb_bwd(projmain, do, t_exp, nblk, side=None):
    S = projmain.shape[0]
    nq = S // SB_T

    def body(n_ref, q_ref, k_ref, v_ref, do_ref, t_ref, dq_ref, dk_ref, dv_ref):
        hp, qb = pl.program_id(0), pl.program_id(1)

        @pl.when(qb == 0)
        def _():
            dk_ref[...] = jnp.zeros_like(dk_ref)
            dv_ref[...] = jnp.zeros_like(dv_ref)

        lane = _iota((SB_T, LANES), 1)
        qsts, dosts, tots = [], [], []
        for sl in SB_LANES:
            qsts.append(_sb_stack(q_ref[:, sl]).astype(MXU))
            dosts.append(_sb_stack(do_ref[:, sl]).astype(MXU))
            tv = t_ref[:, sl]
            tots.append(jnp.concatenate(
                [jnp.sum(jnp.where(lane == HEAD_DIM * hh, tv, 0.0), axis=1, keepdims=True) for hh in range(2)],
                axis=0))
        rowi, coli = _sb_rows(qb)
        u_upto = _tri(SB_T, lambda r, c: r <= c)
        u_before = _tri(SB_T, lambda r, c: r < c)
        kept = jnp.clip(n_ref[hp, qb], 1, qb + 1)

        def kstep(kb, carry, diag=False):
            ks = pl.multiple_of(kb * SB_T, SB_T)
            out = []
            for p, sl in enumerate(SB_LANES):
                psp, pg, dq = carry[p]
                kblk, z, mask, sp, spm = _sb_scores(qsts[p], k_ref.at[:, sl], ks, rowi, coli, diag)
                vblk = v_ref[pl.ds(ks, SB_T), sl].astype(MXU)
                after = tots[p] - (psp + _dot3_l(spm, u_upto, SB_PARTS))
                a = jnp.exp(z - sp - after)
                if diag:
                    a = jnp.where(mask, a, 0.0)
                gm = _dot_nt(dosts[p], vblk) * a
                before = pg + _dot3_l(gm, u_before, SB_PARTS)
                sg = jnp.exp(z - sp)
                dz = (gm * (1.0 - sg) - sg * before) * SB_SCALE
                if diag:
                    dz = jnp.where(mask, dz, 0.0)
                dzb = dz.astype(MXU)
                dk_ref[pl.ds(ks, SB_T), sl] += _dot_tn(dzb, qsts[p])
                dv_ref[pl.ds(ks, SB_T), sl] += _dot_tn(a.astype(MXU), dosts[p])
                out.append((psp + jnp.sum(spm, axis=1, keepdims=True),
                            pg + jnp.sum(gm, axis=1, keepdims=True), dq + _dot(dzb, kblk)))
            return tuple(out)

        zero1 = jnp.zeros((2 * SB_T, 1), F32)
        state = lax.fori_loop(qb + 1 - kept, qb, kstep,
                              ((zero1, zero1, jnp.zeros((2 * SB_T, LANES), F32)),) * SB_PAIRS)
        state = kstep(qb, state, diag=True)
        for p, sl in enumerate(SB_LANES):
            dq_ref[:, sl] = _sb_unstack(state[p][2]).astype(dq_ref.dtype)

    grid = (SB_WIDTH // SB_W, nq)
    body, s_in, s_args, s_shape, s_out, s_scr = host_exchange(body, 6, 3, grid, side)
    return pl.pallas_call(
        body,
        name="sb_bwd",
        out_shape=[
            jax.ShapeDtypeStruct((S, SB_WIDTH), MXU),
            jax.ShapeDtypeStruct((S, SB_WIDTH), F32),
            jax.ShapeDtypeStruct((S, SB_WIDTH), F32),
        ] + s_shape,
        grid=grid,
        in_specs=[
            pl.BlockSpec(memory_space=pltpu.SMEM),
            pl.BlockSpec((SB_T, SB_W), lambda h, i: (i, OFF_Q // SB_W + h)),
            pl.BlockSpec((S, SB_W), lambda h, i: (0, OFF_K // SB_W + h)),
            pl.BlockSpec((S, SB_W), lambda h, i: (0, OFF_V // SB_W + h)),
            pl.BlockSpec((SB_T, SB_W), lambda h, i: (i, h)),
            pl.BlockSpec((SB_T, SB_W), lambda h, i: (i, h)),
        ] + s_in,
        out_specs=[
            pl.BlockSpec((SB_T, SB_W), lambda h, i: (i, h)),
            pl.BlockSpec((S, SB_W), lambda h, i: (0, h)),
            pl.BlockSpec((S, SB_W), lambda h, i: (0, h)),
        ] + s_out,
        scratch_shapes=s_scr,
        compiler_params=_cparams(("arbitrary", "arbitrary")),
    )(nblk, projmain, projmain, projmain, do, t_exp, *s_args)


def local_step(x, pb, target, W, P, late, core):
    D = D_MODEL
    full = lambda a, w=D: (a, 0, w)

    (n1b,) = rowwise("norm_pre", lambda xv, g: (_rms(xv, g),), [full(x)], [full(P["norm_mix_pre"])],
                     [(D, D, MXU)])
    projmain, w_gate = mm("in_proj", n1b, W["main"], "nn", side=exchange_plan([("gather2", late["w_gate"])]))
    W = {**W, "gate": w_gate}
    gate_pre = mm("gate_proj", n1b, W["gate"], "nn", b_slabs=True)
    dtraw = mm("dt_proj", n1b, W["dt"], "nn")
    xbc_act = conv_fwd(projmain, P["conv_w"], P["conv_b"])
    dte, cse = ssd_prep(dtraw, P["dt_bias_pad"], P["a_exp"])
    y_scan, states, *got = ssd_fwd(xbc_act, dte, cse,
                                   side=exchange_plan([("gather2", late[n]) for n in LATE[:LATE_SPLIT]]))
    W = {**W, **late_matrices(dict(zip(LATE[:LATE_SPLIT], got)))}

    def f_gate(ysc, xs, z, dsk, nw):
        return _rms((ysc + xs * dsk) * (z * _sig(z)), nw)

    (y_ssd_b,) = rowwise("ssd_gate", lambda *a: (f_gate(*a),),
                         [(y_scan, 0, GW), (xbc_act, 0, GW), (projmain, OFF_Z, GW)],
                         [(P["dsk_exp"], 0, GW), (P["ssd_norm"], 0, GW)], [(D_INNER, GW, MXU)], tr=256,
                         groups=SSD_GROUPS)
    y_sb_b, t_exp, sb_kept, *got = sb_fwd(
        projmain, side=exchange_plan([("gather2", late[n]) for n in LATE[LATE_SPLIT:]]))
    W = {**W, **late_matrices(dict(zip(LATE[LATE_SPLIT:], got)))}
    u1 = mm("ssd_branch", y_ssd_b, W["ssd"], "nn")
    u2 = mm("sb_branch", y_sb_b, W["sb"], "nn")

    def f_merge(a1, a2, g1, g2, b1, b2):
        return (_sig(g1 + b1) * a1 + _sig(g2 + b2) * a2,)

    gate_rows = [(gate_pre, 0, D), (gate_pre, D, D)]
    gate_bias = [(P["b_gate"], 0, D), (P["b_gate"], D, D)]
    (merged_b,) = rowwise("merge", f_merge, [full(u1), full(u2)] + gate_rows, gate_bias, [(D, D, MXU)])
    mo = mm("out_proj", merged_b, W["out"], "nn")

    def f_mid(xv, m, gpost, gffn):
        h1 = xv + _rms(m, gpost)
        return h1, _rms(h1, gffn)

    h1, n2b = rowwise("mix_post", f_mid, [full(x), full(mo)],
                      [full(P["norm_mix_post"]), full(P["norm_ffn_pre"])], [(D, D, F32), (D, D, MXU)])
    a_ff, rb = mm("ff1", n2b, W["ff1"], "nn", b_slabs=True,
                  extra=[(MXU, lambda acc: jnp.square(jnp.maximum(acc, 0.0)))])
    ff = mm("ff2", rb, W["ff2"], "nn")

    def f_ffn_post(h, f, g):
        h2 = h + _rms(f, g)
        return h2, h2

    h2, h2b = rowwise("ffn_post", f_ffn_post, [full(h1), full(ff)], [full(P["norm_ffn_post"])],
                      [(D, D, F32), (D, D, MXU)])
    pgp = mm("ple_gate", h2b, W["pg"], "nn")
    pe = mm("ple_proj", pb, W["ple"], "nn", b_slabs=True)

    def f_ple(h2v, gp, pev, tgt, g):
        f = lambda h, a, b, gg: h + _rms(_sig(a) * b, gg)
        h3, vjp = jax.vjp(f, h2v, gp, pev, g)
        err = h3 - tgt
        dh2, dgp, dpe, dg = vjp(err * (1.0 / D))
        lossc = (0.5 / D) * jnp.sum(err * err, axis=0, keepdims=True)
        return dh2, dgp, dpe, dg, lossc

    dh2a, dpgp_b, dpe_b, g_ple, lossc = rowwise(
        "ple_loss", f_ple, [full(h2), full(pgp), full(pe), full(target)], [full(P["norm_ple_post"])],
        [(D, D, F32), (D, D, MXU), (D, D, MXU)], [(D, D), (D, D)])
    dh2 = mm("d_ple_gate_x", dpgp_b, W["pg"], "nt", add=dh2a)
    gW = {}
    gW["w_ple_gate"] = mm("d_ple_gate_w", h2b, dpgp_b, "tn")
    gW["w_ple"] = mm("d_ple_w", pb, dpe_b, "tn", out_slabs=True)

    def b_ffn_post(d, f, g):
        _, vjp = jax.vjp(_rms, f, g)
        return vjp(d)

    dff_b, g_ffn_post = rowwise("d_ffn_post", b_ffn_post, [full(dh2), full(ff)], [full(P["norm_ffn_post"])],
                                [(D, D, MXU)], [(D, D)])
    da_b = mm("d_ff2_x", dff_b, W["ff2"], "nt", out_dtype=MXU,
              epi=lambda acc, act: acc * (2.0 * jnp.maximum(act, 0.0)), epi_args=[a_ff])
    gW["w_ff2"] = mm("d_ff2_w", rb, dff_b, "tn")
    dn2 = mm("d_ff1_x", da_b, W["ff1"], "nt", b_slabs=True)
    gW["w_ff1"] = mm("d_ff1_w", n2b, da_b, "tn", out_slabs=True)

    def b_mid(d2, dn, h, m, gpost, gffn):
        _, vjp = jax.vjp(_rms, h, gffn)
        dh, dgffn = vjp(dn)
        dh1 = d2 + dh
        _, vjp2 = jax.vjp(_rms, m, gpost)
        dm, dgpost = vjp2(dh1)
        return dh1, dm, dgpost, dgffn

    dh1, dmo_b, g_mix_post, g_ffn_pre = rowwise(
        "d_mix_post", b_mid, [full(dh2), full(dn2), full(h1), full(mo)],
        [full(P["norm_mix_post"]), full(P["norm_ffn_pre"])], [(D, D, F32), (D, D, MXU)], [(D, D), (D, D)])
    dmerged = mm("d_out_x", dmo_b, W["out"], "nt")
    gW["w_out"] = mm("d_out_w", merged_b, dmo_b, "tn")

    def b_merge(d, a1, a2, g1, g2, b1, b2):
        s1, s2 = _sig(g1 + b1), _sig(g2 + b2)
        dg1 = d * a1 * s1 * (1.0 - s1)
        dg2 = d * a2 * s2 * (1.0 - s2)
        dg = jnp.concatenate([dg1, dg2], axis=1)
        return d * s1, d * s2, dg, jnp.sum(dg, axis=0, keepdims=True)

    du1_b, du2_b, dgp_b, g_b_gate = rowwise(
        "d_merge", b_merge, [full(dmerged), full(u1), full(u2)] + gate_rows, gate_bias,
        [(D, D, MXU), (D, D, MXU), (2 * D, 2 * D, MXU)], [(2 * D, 2 * D)])
    dy_ssd = mm("d_ssd_branch_x", du1_b, W["ssd"], "nt")
    dy_sb = mm("d_sb_branch_x", du2_b, W["sb"], "nt")
    gW["w_ssd_branch"] = mm("d_ssd_branch_w", y_ssd_b, du1_b, "tn")
    gW["w_sb_branch"] = mm("d_sb_branch_w", y_sb_b, du2_b, "tn")

    def b_gate(d, ysc, xs, z, dsk, nw):
        _, vjp = jax.vjp(f_gate, ysc, xs, z, dsk, nw)
        return vjp(d)

    slabs = {n: gW[n] if n in COL_SHARDED else gW[n].reshape((N_DEV,) + ROW_SHARDED[n]) for n in LATE}
    dy_scan, dxs_skip, dz_b, g_dsk_exp, g_ssd_norm, *from_sibling = rowwise(
        "d_ssd_gate", b_gate, [(dy_ssd, 0, GW), (y_scan, 0, GW), (xbc_act, 0, GW), (projmain, OFF_Z, GW)],
        [(P["dsk_exp"], 0, GW), (P["ssd_norm"], 0, GW)],
        [(D_INNER, GW, F32), (D_INNER, GW, F32), (D_INNER, GW, MXU)], [(D_INNER, GW), (D_INNER, GW)],
        tr=256, groups=SSD_GROUPS, side=exchange_plan([("pair", slabs[n]) for n in LATE]))
    chip_sums = [pair_sum("pair_sum_" + n, slabs[n], s, core) for n, s in zip(LATE, from_sibling)]
    dxs, d_b, d_c, ddt_part, g_a_exp, *late_parts = ssd_bwd(
        dy_scan, xbc_act, dte, cse, states, P["a_exp"], side=exchange_plan([("chips", s) for s in chip_sums]))

    def b_dt(dpart, dtr, bias):
        ddt = _dot3_l(dpart, _head_reduce())
        d = ddt * _sig(dtr + bias)
        return d, jnp.sum(d, axis=0, keepdims=True)

    ddt_b, g_dt_bias_pad = rowwise("d_dt", b_dt, [(ddt_part, 0, D_INNER), (dtraw, 0, LANES)],
                                   [(P["dt_bias_pad"], 0, LANES)], [(LANES, LANES, MXU)], [(LANES, LANES)])
    dxbc_b, g_conv_w, g_conv_b = conv_bwd(dxs, dxs_skip, d_b, d_c, projmain, P["conv_w"], P["conv_b"])
    dq_b, dk, dv = sb_bwd(projmain, dy_sb, t_exp, sb_kept)
    dmain_b = jnp.concatenate([dz_b, dxbc_b, dq_b, dk.astype(MXU), dv.astype(MXU)], axis=1)
    g_main = mm("d_in_w", n1b, dmain_b, "tn")
    g_dt = mm("d_dt_w", n1b, ddt_b, "tn")
    last = {"w_gate": mm("d_gate_w", n1b, dgp_b, "tn", out_slabs=True)}
    last["w_in"], conv_slabs = in_grad_slabs({"main": g_main, "dt": g_dt, "conv_w": g_conv_w})
    dn1_dt = mm("d_dt_x", ddt_b, W["dt"], "nt")
    dn1_gate, *from_sibling = mm("d_gate_x", dgp_b, W["gate"], "nt", add=dn1_dt, b_slabs=True,
                                 side=exchange_plan([("pair", last[n]) for n in FIRST]))
    chip_sums = [pair_sum("pair_sum_" + n, last[n], s, core) for n, s in zip(FIRST, from_sibling)]
    dn1, *first_parts = mm("d_in_x", dmain_b, W["main"], "nt", add=dn1_gate,
                           side=exchange_plan([("chips", s) for s in chip_sums]))

    def b_pre(d1, dn, xv, g):
        _, vjp = jax.vjp(_rms, xv, g)
        dx, dg = vjp(dn)
        return d1 + dx, dg

    grad_x, g_mix_pre = rowwise("d_norm_pre", b_pre, [full(dh1), full(dn1), full(x)], [full(P["norm_mix_pre"])],
                                [(D, D, F32)], [(D, D)])

    parts = {**dict(zip(LATE, late_parts)), **dict(zip(FIRST, first_parts))}
    gS = {
        "norm_mix_pre": g_mix_pre, "conv_b": g_conv_b, "dt_bias_pad": g_dt_bias_pad, "a_exp": g_a_exp,
        "dsk_exp": g_dsk_exp, "ssd_norm": g_ssd_norm, "b_gate": g_b_gate,
        "norm_mix_post": g_mix_post, "norm_ffn_pre": g_ffn_pre, "norm_ffn_post": g_ffn_post,
        "norm_ple_post": g_ple,
    }
    return lossc, grad_x, parts, conv_slabs, gS


IN_SPLITS = (2048, 6144, 6176, 7200, 8224)
IN_SHARD = 1156
IN_PAD = 1280
COL_SHARDED = {"w_in": (1024, IN_SHARD), "conv_w": (4, 512), "w_gate": (1024, 256), "w_ff1": (1024, 512),
               "w_ple": (256, 128)}
ROW_SHARDED = {"w_ssd_branch": (256, 1024), "w_sb_branch": (128, 1024), "w_out": (128, 1024),
               "w_ff2": (512, 1024), "w_ple_gate": (128, 1024)}
SHARDED = tuple(COL_SHARDED) + tuple(ROW_SHARDED)
SMALL = (
    ("norm_mix_pre", 1024), ("conv_b", 4096), ("dt_bias", 32), ("a_log", 32), ("d_skip", 32), ("ssd_norm", 2048),
    ("b_gate", 2048), ("norm_mix_post", 1024), ("norm_ffn_pre", 1024), ("norm_ffn_post", 1024),
    ("norm_ple_post", 1024),
)
ROW = 1024
SMALL_ROWS = 16


def _rows_of(n):
    return -(-n // ROW)


def _pad_to(v, n, axis=-1):
    pad = [(0, 0)] * v.ndim
    pad[axis] = (0, n - v.shape[axis])
    return jnp.pad(v, pad)


FIRST = ("w_in", "w_gate")
LATE = ("w_ssd_branch", "w_sb_branch", "w_out", "w_ff1", "w_ff2", "w_ple", "w_ple_gate")
LATE_SPLIT = 3


def in_matrices(w_in_slabs):
    w_in = jnp.concatenate([w_in_slabs[k, :, :IN_SHARD] for k in range(N_DEV)], axis=1)
    return {
        "main": jnp.concatenate([w_in[:, :IN_SPLITS[1]], w_in[:, IN_SPLITS[2]:]], axis=1),
        "dt": _pad_to(w_in[:, IN_SPLITS[1]:IN_SPLITS[2]], LANES),
    }


def late_matrices(g):
    short = {"w_ff1": "ff1", "w_ple": "ple", "w_ssd_branch": "ssd", "w_sb_branch": "sb", "w_out": "out",
             "w_ff2": "ff2", "w_ple_gate": "pg"}
    stack = lambda a: a.reshape(a.shape[0] * a.shape[1], a.shape[2])
    return {short[n]: a if n in COL_SHARDED else stack(a) for n, a in g.items()}


def build_small(small, conv_w_slabs):
    P = {k: small[k] for k in ("norm_mix_pre", "conv_b", "ssd_norm", "b_gate", "norm_mix_post", "norm_ffn_pre",
                               "norm_ffn_post", "norm_ple_post")}
    P["conv_w"] = jnp.concatenate([conv_w_slabs[k] for k in range(N_DEV)], axis=1)
    P["dt_bias_pad"] = _pad_to(small["dt_bias"], LANES)
    P["a_exp"] = jnp.repeat(-jnp.exp(small["a_log"]), HEAD_DIM, axis=1)
    P["dsk_exp"] = jnp.repeat(small["d_skip"], HEAD_DIM, axis=1)
    return P


def small_grads(gS, small):
    heads = lambda a: a.reshape(SSD_HEADS, HEAD_DIM).sum(axis=1)[None, :]
    out = {k: gS[k] for k in ("norm_mix_pre", "conv_b", "ssd_norm", "b_gate", "norm_mix_post", "norm_ffn_pre",
                              "norm_ffn_post", "norm_ple_post")}
    out["dt_bias"] = gS["dt_bias_pad"][:, :SSD_HEADS]
    out["a_log"] = heads(gS["a_exp"]) * (-jnp.exp(small["a_log"]))
    out["d_skip"] = heads(gS["dsk_exp"])
    return out


def in_grad_slabs(g):
    pieces = ((0, IN_SPLITS[1], g["main"], 0), (IN_SPLITS[1], IN_SPLITS[2], g["dt"], -IN_SPLITS[1]),
              (IN_SPLITS[2], N_DEV * IN_SHARD, g["main"], IN_SPLITS[1] - IN_SPLITS[2]))
    slabs = []
    for k in range(N_DEV):
        a, b = IN_SHARD * k, IN_SHARD * (k + 1)
        cut = [src[:, max(a, lo) + off:min(b, hi) + off] for lo, hi, src, off in pieces if max(a, lo) < min(b, hi)]
        slabs.append(_pad_to(jnp.concatenate(cut, axis=1), IN_PAD))
    width = COL_SHARDED["conv_w"][1]
    return jnp.stack(slabs), jnp.stack([g["conv_w"][:, width * k:width * (k + 1)] for k in range(N_DEV)])


def pack_small(get):
    cols = [_pad_to(get(name).reshape(n), _rows_of(n) * ROW) for name, n in SMALL]
    return jnp.concatenate(cols).reshape(SMALL_ROWS, ROW)


def unpack_small(flat):
    out, r0 = {}, 0
    for name, n in SMALL:
        rows = _rows_of(n)
        out[name] = flat[r0:r0 + rows].reshape(rows * ROW)[:n].reshape(1, n)
        r0 += rows
    return out


N_CHIPS = 4
JOB_SEMS = {"all": 7, "scatter": 7, "gather2": 7, "pair": 4, "chips": 3}


def exchange_plan(jobs):
    n = len(jobs)
    kinds = [k for k, _ in jobs]
    srcs = [s for _, s in jobs]
    shapes = {"all": lambda s: (N_DEV,) + s.shape, "gather2": lambda s: (N_DEV,) + s.shape,
              "scatter": lambda s: s.shape, "pair": lambda s: (N_CHIPS,) + s.shape[1:], "chips": lambda s: s.shape}
    out_shape = [jax.ShapeDtypeStruct(shapes[k](s), s.dtype) for k, s in jobs]
    offs = [sum(JOB_SEMS[k] for k in kinds[:i]) for i in range(n + 1)]

    def run(phases, src, out, send_sems, recv_sems, local_sems):
        x, y, c = lax.axis_index("x"), lax.axis_index("y"), lax.axis_index("c")
        dev = lambda d: 4 * d[0] + 2 * d[1] + d[2]
        chip_no = lambda ch: 2 * ch[0] + ch[1]
        me, sib, my_chip = (x, y, c), (x, y, 1 - c), (x, y)
        others = [(1 - x, y), (x, 1 - y), (1 - x, 1 - y)]
        all_chips = [(0, 0), (0, 1), (1, 0), (1, 1)]
        peers = [(1 - x if k & 4 else x, 1 - y if k & 2 else y, 1 - c if k & 1 else c) for k in range(1, N_DEV)]
        starts, recvs, local = [], [], []
        chained = [[] for _ in others]

        for i, kind in enumerate(kinds):
            s_ref, o_ref = src[i], out[i]

            def rc(k, src_ref, dst_ref, to, i=i):
                s = offs[i] + k
                return pltpu.make_async_remote_copy(src_ref=src_ref, dst_ref=dst_ref, send_sem=send_sems.at[s],
                                                    recv_sem=recv_sems.at[s], device_id=to,
                                                    device_id_type=pl.DeviceIdType.MESH)

            if kind == "all":
                local.append(pltpu.make_async_copy(s_ref, o_ref.at[dev(me)], local_sems.at[i]))
                for k, peer in enumerate(peers):
                    starts.append(rc(k, s_ref, o_ref.at[dev(me)], peer))
                    recvs.append(rc(k, s_ref, o_ref.at[dev(peer)], peer))
            elif kind == "scatter":
                local.append(pltpu.make_async_copy(s_ref.at[dev(me)], o_ref.at[dev(me)], local_sems.at[i]))
                for k, peer in enumerate(peers):
                    starts.append(rc(k, s_ref.at[dev(peer)], o_ref.at[dev(me)], peer))
                    recvs.append(rc(k, s_ref.at[dev(me)], o_ref.at[dev(peer)], peer))
            elif kind == "gather2":
                local.append(pltpu.make_async_copy(s_ref, o_ref.at[dev(me)], local_sems.at[i]))
                starts.append(rc(0, s_ref, o_ref.at[dev(me)], sib))
                recvs.append(rc(0, s_ref, o_ref.at[dev(sib)], sib))
                for j, ch in enumerate(others):
                    same, other = (*ch, c), (*ch, 1 - c)
                    starts.append(rc(1 + j, s_ref, o_ref.at[dev(me)], same))
                    chained[j].append((rc(1 + j, s_ref, o_ref.at[dev(same)], same),
                                       rc(4 + j, o_ref.at[dev(same)], o_ref.at[dev(same)], sib)))
                    recvs.append(rc(4 + j, s_ref, o_ref.at[dev(other)], sib))
            elif kind == "pair":
                for j, ch in enumerate(all_chips):
                    starts.append(rc(j, s_ref.at[dev((*ch, 1 - c))], o_ref.at[j], sib))
                    recvs.append(rc(j, s_ref.at[dev((*ch, c))], o_ref.at[j], sib))
            else:
                mine = chip_no(my_chip)
                local.append(pltpu.make_async_copy(s_ref.at[mine], o_ref.at[mine], local_sems.at[i]))
                for j, ch in enumerate(others):
                    starts.append(rc(j, s_ref.at[chip_no(ch)], o_ref.at[mine], (*ch, c)))
                    recvs.append(rc(j, s_ref.at[mine], o_ref.at[chip_no(ch)], (*ch, c)))

        if "start" in phases:
            for cp in local + starts:
                cp.start()
        if "pass" in phases:
            for group in chained:
                for arrival, forward in group:
                    arrival.wait_recv()
                    forward.start()
        if "finish" in phases:
            for cp in recvs:
                cp.wait_recv()
            for cp in starts + [forward for group in chained for _, forward in group]:
                cp.wait_send()
            for cp in local:
                cp.wait()

    scratch = [pltpu.SemaphoreType.DMA((offs[n],)), pltpu.SemaphoreType.DMA((offs[n],)),
               pltpu.SemaphoreType.DMA((n,))]
    return srcs, out_shape, scratch, run


EXCHANGE_PHASES = ("start", "pass", "finish")


def exchange(name, jobs):
    srcs, out_shape, scratch, run = exchange_plan(jobs)
    n = len(srcs)

    def body(*refs):
        run(EXCHANGE_PHASES, refs[:n], refs[n:2 * n], *refs[2 * n:])

    any_spec = pl.BlockSpec(memory_space=pl.ANY)
    return pl.pallas_call(body, name=name, out_shape=out_shape, in_specs=[any_spec] * n, out_specs=[any_spec] * n,
                          scratch_shapes=scratch)(*srcs)


def host_exchange(body, n_in, n_out, grid, plan):
    if plan is None:
        return body, [], [], [], [], []
    srcs, out_shape, scratch, run = plan
    n = len(srcs)
    steps = 1
    for g in grid:
        steps *= g

    def hosted(*refs):
        ins, side_in = refs[:n_in], refs[n_in:n_in + n]
        outs, side_out = refs[n_in + n:n_in + n + n_out], refs[n_in + n + n_out:n_in + 2 * n + n_out]
        rest = refs[n_in + 2 * n + n_out:]
        own, sems = rest[:len(rest) - 3], rest[len(rest) - 3:]
        step = 0
        for d, g in enumerate(grid):
            step = step * g + pl.program_id(d)

        def at(when, phase):
            @pl.when(step == when)
            def _():
                run((phase,), side_in, side_out, *sems)

        at(0, "start")
        body(*ins, *outs, *own)
        at(steps * 13 // 16, "pass")
        at(steps - 1, "finish")

    any_spec = pl.BlockSpec(memory_space=pl.ANY)
    return hosted, [any_spec] * n, list(srcs), list(out_shape), [any_spec] * n, list(scratch)


def pair_sum(name, g, sib, core):
    _, r, c = g.shape
    tr = _pick(r, (256, 128, 64, 32, 16, 8))

    def body(core_ref, g_ref, s_ref, o_ref):
        o_ref[...] = (g_ref[...] + s_ref[...]).astype(o_ref.dtype)

    return pl.pallas_call(
        body,
        name=name,
        out_shape=jax.ShapeDtypeStruct((N_CHIPS, r, c), MXU),
        grid_spec=pltpu.PrefetchScalarGridSpec(
            num_scalar_prefetch=1,
            grid=(N_CHIPS, r // tr),
            in_specs=[
                pl.BlockSpec((None, None, tr, c), lambda j, i, core_ref: (j, core_ref[0], i, 0)),
                pl.BlockSpec((None, tr, c), lambda j, i, core_ref: (j, i, 0)),
            ],
            out_specs=pl.BlockSpec((None, tr, c), lambda j, i, core_ref: (j, i, 0)),
        ),
        compiler_params=_cparams(("parallel", "parallel")),
    )(core, g.reshape(N_CHIPS, 2, r, c), sib)


def adamw(name, parts, w, m, v):
    rows, cols = w.shape
    tr = _pick(rows, (128, 64, 32, 16, 8))

    nparts = parts.shape[0]

    def body(p_ref, w_ref, m_ref, v_ref, g_ref, d_ref, m2_ref, v2_ref):
        g = p_ref[0].astype(F32)
        for k in range(1, nparts):
            g = g + p_ref[k].astype(F32)
        m2 = ADAM_B1 * m_ref[...] + (1.0 - ADAM_B1) * g
        v2 = ADAM_B2 * v_ref[...] + (1.0 - ADAM_B2) * jnp.square(g)
        m_hat = m2 / (1.0 - ADAM_B1 ** ADAM_STEP)
        v_hat = v2 / (1.0 - ADAM_B2 ** ADAM_STEP)
        g_ref[...] = g
        d_ref[...] = -ADAM_LR * (m_hat / (jnp.sqrt(v_hat) + ADAM_EPS) + ADAM_WD * w_ref[...])
        m2_ref[...] = m2
        v2_ref[...] = v2

    spec = pl.BlockSpec((tr, cols), lambda i: (i, 0))
    return pl.pallas_call(
        body,
        name=name,
        out_shape=[jax.ShapeDtypeStruct((rows, cols), F32)] * 4,
        grid=(rows // tr,),
        in_specs=[pl.BlockSpec((nparts, tr, cols), lambda i: (0, i, 0)), spec, spec, spec],
        out_specs=[spec] * 4,
        compiler_params=_cparams(("parallel",)),
    )(parts, w, m, v)


WEIGHT_ORDER = (
    "norm_mix_pre", "w_in", "conv_w", "conv_b", "dt_bias", "a_log", "d_skip", "ssd_norm", "w_ssd_branch",
    "w_sb_branch", "w_gate", "b_gate", "w_out", "norm_mix_post", "norm_ffn_pre", "w_ff1", "w_ff2", "norm_ffn_post",
    "w_ple", "w_ple_gate", "norm_ple_post",
)


def kernel(x, p, norm_mix_pre, w_in, conv_w, conv_b, dt_bias, a_log, d_skip, ssd_norm, w_ssd_branch, w_sb_branch, w_gate, b_gate, w_out, norm_mix_post, norm_ffn_pre, w_ff1, w_ff2, norm_ffn_post, w_ple, w_ple_gate, norm_ple_post, loss_target, m_norm_mix_pre, m_w_in, m_conv_w, m_conv_b, m_dt_bias, m_a_log, m_d_skip, m_ssd_norm, m_w_ssd_branch, m_w_sb_branch, m_w_gate, m_b_gate, m_w_out, m_norm_mix_post, m_norm_ffn_pre, m_w_ff1, m_w_ff2, m_norm_ffn_post, m_w_ple, m_w_ple_gate, m_norm_ple_post, v_norm_mix_pre, v_w_in, v_conv_w, v_conv_b, v_dt_bias, v_a_log, v_d_skip, v_ssd_norm, v_w_ssd_branch, v_w_sb_branch, v_w_gate, v_b_gate, v_w_out, v_norm_mix_post, v_norm_ffn_pre, v_w_ff1, v_w_ff2, v_norm_ffn_post, v_w_ple, v_w_ple_gate, v_norm_ple_post):
    a = dict(locals())
    seq = x.shape[1]
    x2 = x.reshape(seq, D_MODEL)
    target = loss_target.reshape(seq, D_MODEL)
    pb = p.reshape(seq, PLE_DIM).astype(MXU)

    def shard(prefix, name, dtype):
        v = a[prefix + name][0].astype(dtype)
        return _pad_to(v, IN_PAD) if name == "w_in" else v

    w_in_slabs, conv_w_slabs = exchange(
        "gather_weights", [("gather2", shard("", "w_in", MXU)), ("all", a["conv_w"][0])])
    small = {n: a[n] for n, _ in SMALL}
    core = lax.axis_index("c").astype(jnp.int32).reshape(1)

    lossc, grad_x, parts, conv_slabs, g_acc = local_step(
        x2, pb, target, in_matrices(w_in_slabs), build_small(small, conv_w_slabs),
        {n: shard("", n, MXU) for n in LATE + ("w_gate",)}, core)
    loss = lax.psum(jnp.sum(lossc), ("x", "y", "c"))
    g_small = small_grads(g_acc, small)

    parts["conv_w"], small_parts = exchange(
        "small_grads", [("scatter", conv_slabs), ("all", pack_small(lambda n: g_small[n]))])

    leaves = {}
    for n, part in parts.items():
        res = adamw("adamw_" + n, part, shard("", n, F32), shard("m_", n, F32), shard("v_", n, F32))
        leaves[n] = [r[None, :, :IN_SHARD] if n == "w_in" else r[None] for r in res]
    res = adamw("adamw_small", small_parts, pack_small(lambda n: a[n]), pack_small(lambda n: a["m_" + n]),
                pack_small(lambda n: a["v_" + n]))
    for j, r in enumerate(res):
        for n, leaf in unpack_small(r).items():
            leaves.setdefault(n, [None] * 4)[j] = leaf
    outs = [loss, grad_x.reshape(x.shape)]
    for j in range(4):
        outs += [leaves[n][j] for n in WEIGHT_ORDER]
    return tuple(outs)
```

```python
import functools

import jax
import jax.numpy as jnp
from jax import lax
from jax.experimental import pallas as pl
from jax.experimental.pallas import tpu as pltpu

F32 = jnp.float32
MXU = jnp.bfloat16
VMEM_LIMIT = 56 * 1024 * 1024

D_MODEL = 1024
D_INNER = 2048
SSD_HEADS = 32
HEAD_DIM = 64
SSD_GROUPS = 8
D_STATE = 128
CONV_K = 4
CONV_DIM = 4096
CHUNK = 128
SB_WIDTH = 1024
D_FF = 4096
PLE_DIM = 256
RMS_EPS = 1e-6
SB_SCALE = HEAD_DIM ** -0.5
N_DEV = 8
LANES = 128

OFF_Z, OFF_XBC, OFF_Q, OFF_K, OFF_V = 0, 2048, 6144, 7168, 8192

ADAM_LR = 0.001
ADAM_B1 = 0.9
ADAM_B2 = 0.999
ADAM_EPS = 1e-08
ADAM_WD = 0.01
ADAM_STEP = 10


def _sig(x):
    return 0.5 * jnp.tanh(0.5 * x) + 0.5


def _softplus(x):
    return jnp.maximum(x, 0.0) + jnp.log(1.0 + jnp.exp(-jnp.abs(x)))


def _rms(x, w):
    return x * lax.rsqrt(jnp.mean(x * x, axis=-1, keepdims=True) + RMS_EPS) * w


def _dot(a, b):
    return jnp.dot(a, b, preferred_element_type=F32)


def _dot_nt(a, b):
    return lax.dot_general(a, b, (((1,), (1,)), ((), ())), preferred_element_type=F32)


def _dot_tn(a, b):
    return lax.dot_general(a, b, (((0,), (0,)), ((), ())), preferred_element_type=F32)


def _split3(x):
    x1 = x.astype(MXU)
    r = x - x1.astype(F32)
    x2 = r.astype(MXU)
    r = r - x2.astype(F32)
    return x1, x2, r.astype(MXU)


def _dot3_l(a, u, parts=3):
    m = a.shape[0]
    d = _dot(jnp.concatenate(_split3(a)[:parts], axis=0), u)
    if parts == 2:
        return d[m:] + d[:m]
    return (d[2 * m:] + d[m:2 * m]) + d[:m]


def _dot3_r(u, a):
    n = a.shape[1]
    d = _dot(u, jnp.concatenate(_split3(a), axis=1))
    return (d[:, 2 * n:] + d[:, n:2 * n]) + d[:, :n]


def _iota(shape, dim):
    return lax.broadcasted_iota(jnp.int32, shape, dim)


def _tri(n, cmp):
    r, c = _iota((n, n), 0), _iota((n, n), 1)
    return cmp(r, c).astype(F32).astype(MXU)


def _cparams(sem):
    return pltpu.CompilerParams(dimension_semantics=sem, vmem_limit_bytes=VMEM_LIMIT)


def _pick(n, cands):
    for c in cands:
        if n % c == 0:
            return c
    return n


def mm(name, a, b, mode, add=None, out_dtype=F32, b_slabs=False, out_slabs=False, epi=None, epi_args=(), extra=(),
       side=None):
    slab = None
    if b_slabs:
        slab = b.shape[2]
        bshape = (b.shape[1], N_DEV * slab)
    else:
        bshape = b.shape
    if mode == "nn":
        (M, K), (K2, N) = a.shape, bshape
    elif mode == "nt":
        (M, K), (N, K2) = a.shape, bshape
    else:
        (K, M), (K2, N) = a.shape, bshape
    assert K == K2, (name, a.shape, b.shape)
    tm = _pick(M, (1024, 512, 256, 128))
    tn = _pick(N, (1024, 512, 256, 128))
    tk = _pick(K, (1024, 512, 256, 128))
    if b_slabs and mode == "nn":
        tn = slab
    if b_slabs and mode == "nt":
        tk = slab
    if out_slabs:
        assert mode == "tn" and N % N_DEV == 0
        tn = N // N_DEV
    nk = K // tk

    def body(*refs):
        refs = list(refs)
        a_ref, b_ref = refs[:2]
        add_ref = refs[2] if add is not None else None
        n_in = 2 + (add is not None)
        epi_refs = refs[n_in:n_in + len(epi_args)]
        o_ref = refs[n_in + len(epi_args)]
        extra_refs = refs[n_in + len(epi_args) + 1:-1]
        acc = refs[-1]
        k = pl.program_id(2)

        @pl.when(k == 0)
        def _():
            acc[...] = jnp.zeros_like(acc) if add is None else add_ref[...]

        av, bv = a_ref[...], b_ref[...]
        if mode == "nn":
            acc[...] += _dot(av, bv)
        elif mode == "nt":
            acc[...] += _dot_nt(av, bv)
        else:
            acc[...] += _dot_tn(av, bv)

        @pl.when(k == nk - 1)
        def _():
            res = acc[...]
            main = res if epi is None else epi(res, *[r[...] for r in epi_refs])
            o_ref[...] = main.astype(o_ref.dtype)
            for r, (_, fn) in zip(extra_refs, extra):
                r[...] = fn(res).astype(r.dtype)

    if mode == "nn":
        a_spec = pl.BlockSpec((tm, tk), lambda i, j, k: (i, k))
        b_spec = pl.BlockSpec((tk, tn), lambda i, j, k: (k, j))
    elif mode == "nt":
        a_spec = pl.BlockSpec((tm, tk), lambda i, j, k: (i, k))
        b_spec = pl.BlockSpec((tn, tk), lambda i, j, k: (j, k))
    else:
        a_spec = pl.BlockSpec((tk, tm), lambda i, j, k: (k, i))
        b_spec = pl.BlockSpec((tk, tn), lambda i, j, k: (k, j))
    if b_slabs and mode == "nn":
        b_spec = pl.BlockSpec((None, tk, tn), lambda i, j, k: (j, k, 0))
    if b_slabs and mode == "nt":
        b_spec = pl.BlockSpec((None, tn, tk), lambda i, j, k: (k, j, 0))
    o_spec = pl.BlockSpec((tm, tn), lambda i, j, k: (i, j))
    in_specs, args = [a_spec, b_spec], [a, b]
    if add is not None:
        in_specs.append(o_spec)
        args.append(add)
    for e in epi_args:
        in_specs.append(o_spec)
        args.append(e)
    out_sds = jax.ShapeDtypeStruct((M, N), out_dtype)
    if out_slabs:
        o_spec = pl.BlockSpec((None, tm, tn), lambda i, j, k: (j, i, 0))
        out_sds = jax.ShapeDtypeStruct((N_DEV, M, tn), out_dtype)
    grid = (M // tm, N // tn, nk)
    body, s_in, s_args, s_shape, s_out, s_scr = host_exchange(body, len(args), 1 + len(extra), grid, side)
    res = pl.pallas_call(
        body,
        name=name,
        out_shape=[out_sds] + [jax.ShapeDtypeStruct((M, N), dt) for dt, _ in extra] + s_shape,
        grid=grid,
        in_specs=in_specs + s_in,
        out_specs=[o_spec] * (1 + len(extra)) + s_out,
        scratch_shapes=[pltpu.VMEM((tm, tn), F32)] + s_scr,
        compiler_params=_cparams(("parallel", "parallel", "arbitrary") if side is None else ("arbitrary",) * 3),
    )(*args, *s_args)
    return res if extra or side is not None else res[0]


def rowwise(name, fn, rows, bcast, outs, accs=(), tr=512, ncb=1, side=None, groups=1):
    S = rows[0][0].shape[0]
    tr = min(tr, S)
    nrb = S // tr
    G = groups
    in_specs, args, in_w = [], [], []
    for arr, off, w in rows:
        assert off % (w * G) == 0 and arr.shape[0] == S
        in_specs.append(pl.BlockSpec((tr, w * G), lambda j, i, ob=off // (w * G): (i, ob + j)))
        args.append(arr)
        in_w.append(w)
    for arr, off, w in bcast:
        assert off % (w * G) == 0
        in_specs.append(pl.BlockSpec((arr.shape[0], w * G), lambda j, i, ob=off // (w * G): (0, ob + j)))
        args.append(arr)
        in_w.append(w)
    out_shape, out_specs, out_w = [], [], []
    for tw, w, dt in outs:
        out_shape.append(jax.ShapeDtypeStruct((S, tw), dt))
        out_specs.append(pl.BlockSpec((tr, w * G), lambda j, i: (i, j)))
        out_w.append(w)
    for tw, w in accs:
        out_shape.append(jax.ShapeDtypeStruct((1, tw), F32))
        out_specs.append(pl.BlockSpec((1, w * G), lambda j, i: (0, j)))
        out_w.append(w)
    nin, nout = len(args), len(outs)

    def body(*refs):
        i = pl.program_id(1)
        for g in range(G):
            cut = lambda w: slice(g * w, (g + 1) * w)
            res = fn(*[r[:, cut(w)] for r, w in zip(refs[:nin], in_w)])
            for k, (r, v) in enumerate(zip(refs[nin:], res)):
                cols = cut(out_w[k])
                if k < nout:
                    r[:, cols] = v.astype(r.dtype)
                    continue

                @pl.when(i == 0)
                def _(r=r, v=v, cols=cols):
                    r[:, cols] = v

                @pl.when(i > 0)
                def _(r=r, v=v, cols=cols):
                    r[:, cols] += v

    body, s_in, s_args, s_shape, s_out, s_scr = host_exchange(body, nin, len(out_shape), (ncb, nrb), side)
    res = pl.pallas_call(
        body,
        name=name,
        out_shape=out_shape + s_shape,
        grid=(ncb, nrb),
        in_specs=in_specs + s_in,
        out_specs=out_specs + s_out,
        scratch_shapes=s_scr,
        compiler_params=_cparams(("parallel" if side is None else "arbitrary", "arbitrary")),
    )(*args, *s_args)
    return res


CONV_TC = 128


CONV_R = 256
HALO = 8


def _conv_pre(e, w, b):
    shifted = [pltpu.roll(e, s, 0) for s in (1, 2, 3)]
    pre = b + w[3:4, :] * e
    for s in (1, 2, 3):
        pre = pre + w[3 - s:4 - s, :] * shifted[s - 1]
    return pre, shifted


def conv_fwd(projmain, conv_w, conv_b):
    S = projmain.shape[0]
    tc = CONV_TC

    def body(u_ref, w_ref, b_ref, o_ref):
        u, w = u_ref[...], w_ref[...]
        row = _iota(u.shape, 0)
        pre = b_ref[...] + w[3:4, :] * u
        for s in (1, 2, 3):
            pre = pre + w[3 - s:4 - s, :] * jnp.where(row >= s, pltpu.roll(u, s, 0), 0.0)
        o_ref[...] = pre * _sig(pre)

    return pl.pallas_call(
        body,
        name="conv_fwd",
        out_shape=jax.ShapeDtypeStruct((S, CONV_DIM), F32),
        grid=(CONV_DIM // tc,),
        in_specs=[
            pl.BlockSpec((S, tc), lambda j: (0, OFF_XBC // tc + j)),
            pl.BlockSpec((CONV_K, tc), lambda j: (0, j)),
            pl.BlockSpec((1, tc), lambda j: (0, j)),
        ],
        out_specs=pl.BlockSpec((S, tc), lambda j: (0, j)),
        compiler_params=_cparams(("parallel",)),
    )(projmain, conv_w, conv_b)


def conv_bwd(dxs, dxs_skip, d_b, d_c, projmain, conv_w, conv_b):
    S = projmain.shape[0]
    tc = CONV_TC
    n_xs, n_b = D_INNER // tc, SSD_GROUPS * D_STATE // tc

    def body(dx_ref, dskip_ref, dbm_ref, dcm_ref, u_ref, w_ref, b_ref, du_ref, dw_ref, db_ref, u_pad, d_pad):
        j = pl.program_id(0)
        zeros = jnp.zeros((HALO, tc), F32)
        for pad in (u_pad, d_pad):
            pad[0:HALO, :] = zeros
            pad[HALO + S:2 * HALO + S, :] = zeros
        u_pad[HALO:HALO + S, :] = u_ref[...]
        d_pad[HALO:HALO + S, :] = jnp.where(j < n_xs, dx_ref[...] + dskip_ref[...],
                                            jnp.where(j < n_xs + n_b, dbm_ref[...], dcm_ref[...]))
        w, b = w_ref[...], b_ref[...]
        n = CONV_R + 2 * HALO
        keep = slice(HALO, HALO + CONV_R)

        def chunk(c, sums):
            r0 = pl.multiple_of(c * CONV_R, CONV_R)
            e = u_pad[pl.ds(r0, n), :]
            pre, shifted = _conv_pre(e, w, b)
            sg = _sig(pre)
            dpre = d_pad[pl.ds(r0, n), :] * (sg * (1.0 + pre * (1.0 - sg)))
            du = w[3:4, :] * dpre
            for s in (1, 2, 3):
                du = du + w[3 - s:4 - s, :] * pltpu.roll(dpre, n - s, 0)
            du_ref[pl.ds(r0, CONV_R), :] = du[keep].astype(du_ref.dtype)
            dk = dpre[keep]
            taps = [shifted[2], shifted[1], shifted[0], e]
            return tuple(acc + jnp.sum(dk * t[keep], axis=0, keepdims=True) for acc, t in zip(sums[:4], taps)) + (
                sums[4] + jnp.sum(dk, axis=0, keepdims=True),)

        zero = jnp.zeros((1, tc), F32)
        sums = lax.fori_loop(0, S // CONV_R, chunk, (zero,) * 5)
        for k in range(CONV_K):
            dw_ref[k:k + 1, :] = sums[k]
        db_ref[...] = sums[4]

    return pl.pallas_call(
        body,
        name="conv_bwd",
        out_shape=[
            jax.ShapeDtypeStruct((S, CONV_DIM), MXU),
            jax.ShapeDtypeStruct((CONV_K, CONV_DIM), F32),
            jax.ShapeDtypeStruct((1, CONV_DIM), F32),
        ],
        grid=(CONV_DIM // tc,),
        in_specs=[
            pl.BlockSpec((S, tc), lambda j: (0, jnp.minimum(j, n_xs - 1))),
            pl.BlockSpec((S, tc), lambda j: (0, jnp.minimum(j, n_xs - 1))),
            pl.BlockSpec((S, tc), lambda j: (0, jnp.clip(j - n_xs, 0, n_b - 1))),
            pl.BlockSpec((S, tc), lambda j: (0, jnp.clip(j - n_xs - n_b, 0, n_b - 1))),
            pl.BlockSpec((S, tc), lambda j: (0, OFF_XBC // tc + j)),
            pl.BlockSpec((CONV_K, tc), lambda j: (0, j)),
            pl.BlockSpec((1, tc), lambda j: (0, j)),
        ],
        out_specs=[
            pl.BlockSpec((S, tc), lambda j: (0, j)),
            pl.BlockSpec((CONV_K, tc), lambda j: (0, j)),
            pl.BlockSpec((1, tc), lambda j: (0, j)),
        ],
        scratch_shapes=[pltpu.VMEM((S + 2 * HALO, tc), F32)] * 2,
        compiler_params=_cparams(("arbitrary",)),
    )(dxs, dxs_skip, d_b, d_c, projmain, conv_w, conv_b)


def _head_expand():
    r, j = _iota((LANES, D_INNER), 0), _iota((LANES, D_INNER), 1)
    return ((j >= r * HEAD_DIM) & (j < r * HEAD_DIM + HEAD_DIM)).astype(F32).astype(MXU)


def _head_reduce():
    j, r = _iota((D_INNER, LANES), 0), _iota((D_INNER, LANES), 1)
    return ((j >= r * HEAD_DIM) & (j < r * HEAD_DIM + HEAD_DIM)).astype(F32).astype(MXU)


def ssd_prep(dtraw, dt_bias_pad, a_exp):
    S = dtraw.shape[0]

    def body(dtr_ref, bias_ref, a_ref, dte_ref, cse_ref):
        dt = _softplus(dtr_ref[...] + bias_ref[...])
        dte = _dot3_l(dt, _head_expand())
        dte_ref[...] = dte
        incl = _tri(CHUNK, lambda r, c: r >= c)
        cse_ref[...] = _dot3_r(incl, dte * a_ref[...])

    return pl.pallas_call(
        body,
        name="ssd_prep",
        out_shape=[jax.ShapeDtypeStruct((S, D_INNER), F32)] * 2,
        grid=(S // CHUNK,),
        in_specs=[
            pl.BlockSpec((CHUNK, LANES), lambda c: (c, 0)),
            pl.BlockSpec((1, LANES), lambda c: (0, 0)),
            pl.BlockSpec((1, D_INNER), lambda c: (0, 0)),
        ],
        out_specs=[pl.BlockSpec((CHUNK, D_INNER), lambda c: (c, 0))] * 2,
        compiler_params=_cparams(("parallel",)),
    )(dtraw, dt_bias_pad, a_exp)


GW = 4 * HEAD_DIM
SSD_GPS = 8
SSD_GPS_BWD = 8


def ssd_fwd(xbc_act, dte, cse, side=None):
    S = xbc_act.shape[0]
    nc = S // CHUNK

    def body(xs_ref, b_ref, c_ref, dte_ref, cse_ref, y_ref, st_ref, s_scr):
        c = pl.program_id(1)

        @pl.when(c == 0)
        def _():
            s_scr[...] = jnp.zeros_like(s_scr)

        for gg in range(SSD_GPS):
            ch, st = slice(GW * gg, GW * (gg + 1)), slice(D_STATE * gg, D_STATE * (gg + 1))
            s_in = s_scr[:, ch]
            st_ref[0, :, ch] = s_in
            cs = cse_ref[:, ch]
            xd = xs_ref[:, ch] * dte_ref[:, ch]
            btb = b_ref[:, st].T.astype(MXU)
            cs_last = cs[CHUNK - 1:CHUNK, :]
            gy = _dot(c_ref[:, st].astype(MXU), jnp.concatenate([btb, s_in.astype(MXU)], axis=1))
            g, y = gy[:, :CHUNK], gy[:, CHUNK:] * jnp.exp(cs)
            cs_t = cs.T
            row, col = _iota((CHUNK, CHUNK), 0), _iota((CHUNK, CHUNK), 1)
            ms = []
            for h in range(4):
                lo = HEAD_DIM * h
                lam = jnp.where(row >= col, jnp.exp(cs[:, lo:lo + 1] - cs_t[lo:lo + 1, :]), 0.0)
                ms.append((g * lam).astype(MXU))
            yd = _dot(jnp.concatenate(ms, axis=0), xd.astype(MXU))
            lane = _iota((CHUNK, GW), 1)
            for h in range(4):
                y = y + jnp.where((lane >= HEAD_DIM * h) & (lane < HEAD_DIM * (h + 1)),
                                  yd[CHUNK * h:CHUNK * (h + 1)], 0.0)
            y_ref[:, ch] = y
            w = (xd * jnp.exp(cs_last - cs)).astype(MXU)
            s_scr[:, ch] = jnp.exp(cs_last) * s_in + _dot(btb, w)

    grid = (SSD_GROUPS // SSD_GPS, nc)
    gw, gs = GW * SSD_GPS, D_STATE * SSD_GPS
    body, s_in, s_args, s_shape, s_out, s_scr = host_exchange(body, 5, 2, grid, side)
    return pl.pallas_call(
        body,
        name="ssd_fwd",
        out_shape=[
            jax.ShapeDtypeStruct((S, D_INNER), F32),
            jax.ShapeDtypeStruct((nc, D_STATE, D_INNER), F32),
        ] + s_shape,
        grid=grid,
        in_specs=[
            pl.BlockSpec((CHUNK, gw), lambda g, c: (c, g)),
            pl.BlockSpec((CHUNK, gs), lambda g, c: (c, D_INNER // gs + g)),
            pl.BlockSpec((CHUNK, gs), lambda g, c: (c, (D_INNER + SSD_GROUPS * D_STATE) // gs + g)),
            pl.BlockSpec((CHUNK, gw), lambda g, c: (c, g)),
            pl.BlockSpec((CHUNK, gw), lambda g, c: (c, g)),
        ] + s_in,
        out_specs=[
            pl.BlockSpec((CHUNK, gw), lambda g, c: (c, g)),
            pl.BlockSpec((1, D_STATE, gw), lambda g, c: (c, 0, g)),
        ] + s_out,
        scratch_shapes=[pltpu.VMEM((D_STATE, gw), F32)] + s_scr,
        compiler_params=_cparams(("parallel" if side is None else "arbitrary", "arbitrary")),
    )(xbc_act, xbc_act, xbc_act, dte, cse, *s_args)


def ssd_bwd(dy, xbc_act, dte, cse, states, a_exp, side=None):
    S = xbc_act.shape[0]
    nc = S // CHUNK

    def body(*refs):
        for gg in range(SSD_GPS_BWD):
            ch, st = slice(GW * gg, GW * (gg + 1)), slice(D_STATE * gg, D_STATE * (gg + 1))
            cut = {GW * SSD_GPS_BWD: ch, D_STATE * SSD_GPS_BWD: st}
            one_group(*[r.at[(slice(None),) * (len(r.shape) - 1) + (cut[r.shape[-1]],)] for r in refs])

    def one_group(dy_ref, xs_ref, b_ref, c_ref, dte_ref, cse_ref, sin_ref, sout_ref, a_ref,
                  dxs_ref, db_ref, dc_ref, ddt_ref, dal_ref, ds_scr):
        j = pl.program_id(1)

        @pl.when(j == 0)
        def _():
            ds_scr[...] = jnp.zeros_like(ds_scr)
            dal_ref[...] = jnp.zeros_like(dal_ref)

        ds_out = ds_scr[...]
        dyv, xs = dy_ref[...], xs_ref[...]
        dt, cs = dte_ref[...], cse_ref[...]
        s_in = sin_ref[0]
        bm, cm = b_ref[...], c_ref[...]
        bb, cb = bm.astype(MXU), cm.astype(MXU)
        btb, ctb = bm.T.astype(MXU), cm.T.astype(MXU)
        dsb, sib = ds_out.astype(MXU), s_in.astype(MXU)
        xd = xs * dt
        ecs = jnp.exp(cs)
        cs_last = cs[CHUNK - 1:CHUNK, :]
        eend = jnp.exp(cs_last - cs)
        gy = _dot(cb, jnp.concatenate([btb, sib], axis=1))
        g, yoff = gy[:, :CHUNK], gy[:, CHUNK:] * ecs
        gd = _dot(bb, jnp.concatenate([ctb, dsb], axis=1))
        g_t, dxd_off = gd[:, :CHUNK], gd[:, CHUNK:] * eend
        cs_t = cs.T
        row, col = _iota((CHUNK, CHUNK), 0), _iota((CHUNK, CHUNK), 1)
        lane = _iota((CHUNK, GW), 1)
        heads = [(lane >= HEAD_DIM * h) & (lane < HEAD_DIM * (h + 1)) for h in range(4)]
        dyb, xdb = dyv.astype(MXU), xd.astype(MXU)
        dm_all = _dot_nt(jnp.concatenate([jnp.where(hm, dyv, 0.0) for hm in heads], axis=0).astype(MXU), xdb)
        dmt_all = _dot_nt(jnp.concatenate([jnp.where(hm, xd, 0.0) for hm in heads], axis=0).astype(MXU), dyb)
        lams, m_ts = [], []
        for h in range(4):
            lo = HEAD_DIM * h
            cs_col, cs_row = cs[:, lo:lo + 1], cs_t[lo:lo + 1, :]
            lams.append(jnp.where(row >= col, jnp.exp(cs_col - cs_row), 0.0))
            m_ts.append(g_t * jnp.where(col >= row, jnp.exp(cs_row - cs_col), 0.0))
        acc_all = _dot(jnp.concatenate(m_ts, axis=0).astype(MXU), dyb)
        dxd = dxd_off
        dg = jnp.zeros((CHUNK, CHUNK), F32)
        dcs = dyv * yoff - xd * dxd_off
        for h in range(4):
            blk = slice(CHUNK * h, CHUNK * (h + 1))
            dm, dm_t = dm_all[blk], dmt_all[blk]
            dxd = dxd + jnp.where(heads[h], acc_all[blk], 0.0)
            dg = dg + dm * lams[h]
            wdiff = (jnp.sum(dm * (g * lams[h]), axis=1, keepdims=True)
                     - jnp.sum(dm_t * m_ts[h], axis=1, keepdims=True))
            dcs = dcs + jnp.where(lane == HEAD_DIM * h, wdiff, 0.0)
        dye = (dyv * ecs).astype(MXU)
        dc_ref[...] = _dot(dg.astype(MXU), bb) + _dot_nt(dye, sib)
        db_ref[...] = _dot(dg.T.astype(MXU), cb) + _dot_nt((xd * eend).astype(MXU), dsb)
        ds_scr[...] = jnp.exp(cs_last) * ds_out + _dot(ctb, dye)
        last = jnp.sum(ds_out * sout_ref[0], axis=0, keepdims=True)
        rows = _iota((CHUNK, GW), 0)
        dcs = dcs + jnp.where(rows == CHUNK - 1, last, 0.0)
        dda = _dot3_r(_tri(CHUNK, lambda r, c: c >= r), dcs)
        ddt_ref[...] = a_ref[...] * dda + dxd * xs
        dal_ref[...] += jnp.sum(dt * dda, axis=0, keepdims=True)
        dxs_ref[...] = dxd * dt

    rc = lambda g, j: (nc - 1 - j, g)
    grid = (SSD_GROUPS // SSD_GPS_BWD, nc)
    gw, gs = GW * SSD_GPS_BWD, D_STATE * SSD_GPS_BWD
    body, s_in, s_args, s_shape, s_out, s_scr = host_exchange(body, 9, 5, grid, side)
    return pl.pallas_call(
        body,
        name="ssd_bwd",
        out_shape=[
            jax.ShapeDtypeStruct((S, D_INNER), F32),
            jax.ShapeDtypeStruct((S, SSD_GROUPS * D_STATE), F32),
            jax.ShapeDtypeStruct((S, SSD_GROUPS * D_STATE), F32),
            jax.ShapeDtypeStruct((S, D_INNER), F32),
            jax.ShapeDtypeStruct((1, D_INNER), F32),
        ] + s_shape,
        grid=grid,
        in_specs=[
            pl.BlockSpec((CHUNK, gw), rc),
            pl.BlockSpec((CHUNK, gw), rc),
            pl.BlockSpec((CHUNK, gs), lambda g, j: (nc - 1 - j, D_INNER // gs + g)),
            pl.BlockSpec((CHUNK, gs), lambda g, j: (nc - 1 - j, (D_INNER + SSD_GROUPS * D_STATE) // gs + g)),
            pl.BlockSpec((CHUNK, gw), rc),
            pl.BlockSpec((CHUNK, gw), rc),
            pl.BlockSpec((1, D_STATE, gw), lambda g, j: (nc - 1 - j, 0, g)),
            pl.BlockSpec((1, D_STATE, gw), lambda g, j: (jnp.minimum(nc - j, nc - 1), 0, g)),
            pl.BlockSpec((1, gw), lambda g, j: (0, g)),
        ] + s_in,
        out_specs=[
            pl.BlockSpec((CHUNK, gw), rc),
            pl.BlockSpec((CHUNK, gs), rc),
            pl.BlockSpec((CHUNK, gs), rc),
            pl.BlockSpec((CHUNK, gw), rc),
            pl.BlockSpec((1, gw), lambda g, j: (0, g)),
        ] + s_out,
        scratch_shapes=[pltpu.VMEM((D_STATE, gw), F32)] + s_scr,
        compiler_params=_cparams(("parallel" if side is None else "arbitrary", "arbitrary")),
    )(dy, xbc_act, xbc_act, xbc_act, dte, cse, states, states, a_exp, *s_args)


SB_T = 256
SB_DROP = 104.0
SB_PAIRS = 1
SB_PARTS = 2
SB_W = SB_PAIRS * LANES
SB_LANES = [slice(LANES * p, LANES * (p + 1)) for p in range(SB_PAIRS)]


def _sb_scores(qm, k_ref, ks, rowi, coli, diag):
    kblk = k_ref[pl.ds(ks, SB_T), :].astype(MXU)
    z = _dot_nt(qm, kblk) * SB_SCALE
    sp = _softplus(z)
    if not diag:
        return kblk, z, None, sp, sp
    mask = (ks + coli) < rowi
    return kblk, z, mask, sp, jnp.where(mask, sp, 0.0)


def _sb_stack(v):
    lane = _iota(v.shape, 1)
    return jnp.concatenate([jnp.where(lane < HEAD_DIM, v, 0.0), jnp.where(lane >= HEAD_DIM, v, 0.0)], axis=0)


def _sb_unstack(v):
    lane = _iota((SB_T, LANES), 1)
    return jnp.where(lane < HEAD_DIM, v[:SB_T], v[SB_T:])


def _sb_rows(qb):
    r = _iota((2 * SB_T, SB_T), 0)
    return qb * SB_T + jnp.where(r >= SB_T, r - SB_T, r), _iota((2 * SB_T, SB_T), 1)


def sb_fwd(projmain, side=None):
    S = projmain.shape[0]
    nq = S // SB_T

    def body(q_ref, k_ref, v_ref, o_ref, t_ref, n_ref):
        hp, qb = pl.program_id(0), pl.program_id(1)
        qsts = [_sb_stack(q_ref[:, sl]).astype(MXU) for sl in SB_LANES]
        rowi, coli = _sb_rows(qb)
        u_after = _tri(SB_T, lambda r, c: r > c)

        def cond(carry):
            i, rmin, _, _ = carry
            return (i <= qb) & (rmin < SB_DROP)

        def kstep(carry, diag=False):
            i, _, rs, accs = carry
            ks = pl.multiple_of((qb - i) * SB_T, SB_T)
            rs, accs = list(rs), list(accs)
            for p, sl in enumerate(SB_LANES):
                _, z, mask, sp, spm = _sb_scores(qsts[p], k_ref.at[:, sl], ks, rowi, coli, diag)
                vblk = v_ref[pl.ds(ks, SB_T), sl].astype(MXU)
                a = jnp.exp(z - sp - _dot3_l(spm, u_after, SB_PARTS) - rs[p])
                if diag:
                    a = jnp.where(mask, a, 0.0)
                accs[p] = accs[p] + _dot(a.astype(MXU), vblk)
                rs[p] = rs[p] + jnp.sum(spm, axis=1, keepdims=True)
            return i + 1, jnp.min(functools.reduce(jnp.minimum, rs)), rs, accs

        first = kstep((jnp.int32(0), jnp.float32(0.0), [jnp.zeros((2 * SB_T, 1), F32)] * SB_PAIRS,
                       [jnp.zeros((2 * SB_T, LANES), F32)] * SB_PAIRS), diag=True)
        n, _, rs, accs = lax.while_loop(cond, kstep, first)
        for p, sl in enumerate(SB_LANES):
            o_ref[:, sl] = _sb_unstack(accs[p]).astype(o_ref.dtype)
            t_ref[:, sl] = _sb_unstack(jnp.broadcast_to(rs[p], (2 * SB_T, LANES)))
        n_ref[hp, qb] = n

    grid = (SB_WIDTH // SB_W, nq)
    body, s_in, s_args, s_shape, s_out, s_scr = host_exchange(body, 3, 3, grid, side)
    return pl.pallas_call(
        body,
        name="sb_fwd",
        out_shape=[jax.ShapeDtypeStruct((S, SB_WIDTH), MXU), jax.ShapeDtypeStruct((S, SB_WIDTH), F32),
                   jax.ShapeDtypeStruct((SB_WIDTH // SB_W, nq), jnp.int32)] + s_shape,
        grid=grid,
        in_specs=[
            pl.BlockSpec((SB_T, SB_W), lambda h, i: (i, OFF_Q // SB_W + h)),
            pl.BlockSpec((S, SB_W), lambda h, i: (0, OFF_K // SB_W + h)),
            pl.BlockSpec((S, SB_W), lambda h, i: (0, OFF_V // SB_W + h)),
        ] + s_in,
        out_specs=[pl.BlockSpec((SB_T, SB_W), lambda h, i: (i, h))] * 2
        + [pl.BlockSpec(memory_space=pltpu.SMEM)] + s_out,
        scratch_shapes=s_scr,
        compiler_params=_cparams(("arbitrary", "arbitrary")),
    )(projmain, projmain, projmain, *s_args)


def sb_bwd(projmain, do, t_exp, nblk, side=None):
    S = projmain.shape[0]
    nq = S // SB_T

    def body(n_ref, q_ref, k_ref, v_ref, do_ref, t_ref, dq_ref, dk_ref, dv_ref):
        hp, qb = pl.program_id(0), pl.program_id(1)

        @pl.when(qb == 0)
        def _():
            dk_ref[...] = jnp.zeros_like(dk_ref)
            dv_ref[...] = jnp.zeros_like(dv_ref)

        lane = _iota((SB_T, LANES), 1)
        qsts, dosts, tots = [], [], []
        for sl in SB_LANES:
            qsts.append(_sb_stack(q_ref[:, sl]).astype(MXU))
            dosts.append(_sb_stack(do_ref[:, sl]).astype(MXU))
            tv = t_ref[:, sl]
            tots.append(jnp.concatenate(
                [jnp.sum(jnp.where(lane == HEAD_DIM * hh, tv, 0.0), axis=1, keepdims=True) for hh in range(2)],
                axis=0))
        rowi, coli = _sb_rows(qb)
        u_upto = _tri(SB_T, lambda r, c: r <= c)
        u_before = _tri(SB_T, lambda r, c: r < c)
        kept = jnp.clip(n_ref[hp, qb], 1, qb + 1)

        def kstep(kb, carry, diag=False):
            ks = pl.multiple_of(kb * SB_T, SB_T)
            out = []
            for p, sl in enumerate(SB_LANES):
                psp, pg, dq = carry[p]
                kblk, z, mask, sp, spm = _sb_scores(qsts[p], k_ref.at[:, sl], ks, rowi, coli, diag)
                vblk = v_ref[pl.ds(ks, SB_T), sl].astype(MXU)
                after = tots[p] - (psp + _dot3_l(spm, u_upto, SB_PARTS))
                a = jnp.exp(z - sp - after)
                if diag:
                    a = jnp.where(mask, a, 0.0)
                gm = _dot_nt(dosts[p], vblk) * a
                before = pg + _dot3_l(gm, u_before, SB_PARTS)
                sg = jnp.exp(z - sp)
                dz = (gm * (1.0 - sg) - sg * before) * SB_SCALE
                if diag:
                    dz = jnp.where(mask, dz, 0.0)
                dzb = dz.astype(MXU)
                dk_ref[pl.ds(ks, SB_T), sl] += _dot_tn(dzb, qsts[p])
                dv_ref[pl.ds(ks, SB_T), sl] += _dot_tn(a.astype(MXU), dosts[p])
                out.append((psp + jnp.sum(spm, axis=1, keepdims=True),
                            pg + jnp.sum(gm, axis=1, keepdims=True), dq + _dot(dzb, kblk)))
            return tuple(out)

        zero1 = jnp.zeros((2 * SB_T, 1), F32)
        state = lax.fori_loop(qb + 1 - kept, qb, kstep,
                              ((zero1, zero1, jnp.zeros((2 * SB_T, LANES), F32)),) * SB_PAIRS)
        state = kstep(qb, state, diag=True)
        for p, sl in enumerate(SB_LANES):
            dq_ref[:, sl] = _sb_unstack(state[p][2]).astype(dq_ref.dtype)

    grid = (SB_WIDTH // SB_W, nq)
    body, s_in, s_args, s_shape, s_out, s_scr = host_exchange(body, 6, 3, grid, side)
    return pl.pallas_call(
        body,
        name="sb_bwd",
        out_shape=[
            jax.ShapeDtypeStruct((S, SB_WIDTH), MXU),
            jax.ShapeDtypeStruct((S, SB_WIDTH), F32),
            jax.ShapeDtypeStruct((S, SB_WIDTH), F32),
        ] + s_shape,
        grid=grid,
        in_specs=[
            pl.BlockSpec(memory_space=pltpu.SMEM),
            pl.BlockSpec((SB_T, SB_W), lambda h, i: (i, OFF_Q // SB_W + h)),
            pl.BlockSpec((S, SB_W), lambda h, i: (0, OFF_K // SB_W + h)),
            pl.BlockSpec((S, SB_W), lambda h, i: (0, OFF_V // SB_W + h)),
            pl.BlockSpec((SB_T, SB_W), lambda h, i: (i, h)),
            pl.BlockSpec((SB_T, SB_W), lambda h, i: (i, h)),
        ] + s_in,
        out_specs=[
            pl.BlockSpec((SB_T, SB_W), lambda h, i: (i, h)),
            pl.BlockSpec((S, SB_W), lambda h, i: (0, h)),
            pl.BlockSpec((S, SB_W), lambda h, i: (0, h)),
        ] + s_out,
        scratch_shapes=s_scr,
        compiler_params=_cparams(("arbitrary", "arbitrary")),
    )(nblk, projmain, projmain, projmain, do, t_exp, *s_args)


def local_step(x, pb, target, W, P, late, core):
    D = D_MODEL
    full = lambda a, w=D: (a, 0, w)

    (n1b,) = rowwise("norm_pre", lambda xv, g: (_rms(xv, g),), [full(x)], [full(P["norm_mix_pre"])],
                     [(D, D, MXU)])
    projmain, w_gate = mm("in_proj", n1b, W["main"], "nn", side=exchange_plan([("gather2", late["w_gate"])]))
    W = {**W, "gate": w_gate}
    gate_pre = mm("gate_proj", n1b, W["gate"], "nn", b_slabs=True)
    dtraw = mm("dt_proj", n1b, W["dt"], "nn")
    xbc_act = conv_fwd(projmain, P["conv_w"], P["conv_b"])
    dte, cse = ssd_prep(dtraw, P["dt_bias_pad"], P["a_exp"])
    y_scan, states, *got = ssd_fwd(xbc_act, dte, cse,
                                   side=exchange_plan([("gather2", late[n]) for n in LATE[:LATE_SPLIT]]))
    W = {**W, **late_matrices(dict(zip(LATE[:LATE_SPLIT], got)))}

    def f_gate(ysc, xs, z, dsk, nw):
        return _rms((ysc + xs * dsk) * (z * _sig(z)), nw)

    (y_ssd_b,) = rowwise("ssd_gate", lambda *a: (f_gate(*a),),
                         [(y_scan, 0, GW), (xbc_act, 0, GW), (projmain, OFF_Z, GW)],
                         [(P["dsk_exp"], 0, GW), (P["ssd_norm"], 0, GW)], [(D_INNER, GW, MXU)], tr=256,
                         groups=SSD_GROUPS)
    y_sb_b, t_exp, sb_kept, *got = sb_fwd(
        projmain, side=exchange_plan([("gather2", late[n]) for n in LATE[LATE_SPLIT:]]))
    W = {**W, **late_matrices(dict(zip(LATE[LATE_SPLIT:], got)))}
    u1 = mm("ssd_branch", y_ssd_b, W["ssd"], "nn")
    u2 = mm("sb_branch", y_sb_b, W["sb"], "nn")

    def f_merge(a1, a2, g1, g2, b1, b2):
        return (_sig(g1 + b1) * a1 + _sig(g2 + b2) * a2,)

    gate_rows = [(gate_pre, 0, D), (gate_pre, D, D)]
    gate_bias = [(P["b_gate"], 0, D), (P["b_gate"], D, D)]
    (merged_b,) = rowwise("merge", f_merge, [full(u1), full(u2)] + gate_rows, gate_bias, [(D, D, MXU)])
    mo = mm("out_proj", merged_b, W["out"], "nn")

    def f_mid(xv, m, gpost, gffn):
        h1 = xv + _rms(m, gpost)
        return h1, _rms(h1, gffn)

    h1, n2b = rowwise("mix_post", f_mid, [full(x), full(mo)],
                      [full(P["norm_mix_post"]), full(P["norm_ffn_pre"])], [(D, D, F32), (D, D, MXU)])
    a_ff, rb = mm("ff1", n2b, W["ff1"], "nn", b_slabs=True,
                  extra=[(MXU, lambda acc: jnp.square(jnp.maximum(acc, 0.0)))])
    ff = mm("ff2", rb, W["ff2"], "nn")

    def f_ffn_post(h, f, g):
        h2 = h + _rms(f, g)
        return h2, h2

    h2, h2b = rowwise("ffn_post", f_ffn_post, [full(h1), full(ff)], [full(P["norm_ffn_post"])],
                      [(D, D, F32), (D, D, MXU)])
    pgp = mm("ple_gate", h2b, W["pg"], "nn")
    pe = mm("ple_proj", pb, W["ple"], "nn", b_slabs=True)

    def f_ple(h2v, gp, pev, tgt, g):
        f = lambda h, a, b, gg: h + _rms(_sig(a) * b, gg)
        h3, vjp = jax.vjp(f, h2v, gp, pev, g)
        err = h3 - tgt
        dh2, dgp, dpe, dg = vjp(err * (1.0 / D))
        lossc = (0.5 / D) * jnp.sum(err * err, axis=0, keepdims=True)
        return dh2, dgp, dpe, dg, lossc

    dh2a, dpgp_b, dpe_b, g_ple, lossc = rowwise(
        "ple_loss", f_ple, [full(h2), full(pgp), full(pe), full(target)], [full(P["norm_ple_post"])],
        [(D, D, F32), (D, D, MXU), (D, D, MXU)], [(D, D), (D, D)])
    dh2 = mm("d_ple_gate_x", dpgp_b, W["pg"], "nt", add=dh2a)
    gW = {}
    gW["w_ple_gate"] = mm("d_ple_gate_w", h2b, dpgp_b, "tn")
    gW["w_ple"] = mm("d_ple_w", pb, dpe_b, "tn", out_slabs=True)

    def b_ffn_post(d, f, g):
        _, vjp = jax.vjp(_rms, f, g)
        return vjp(d)

    dff_b, g_ffn_post = rowwise("d_ffn_post", b_ffn_post, [full(dh2), full(ff)], [full(P["norm_ffn_post"])],
                                [(D, D, MXU)], [(D, D)])
    da_b = mm("d_ff2_x", dff_b, W["ff2"], "nt", out_dtype=MXU,
              epi=lambda acc, act: acc * (2.0 * jnp.maximum(act, 0.0)), epi_args=[a_ff])
    gW["w_ff2"] = mm("d_ff2_w", rb, dff_b, "tn")
    dn2 = mm("d_ff1_x", da_b, W["ff1"], "nt", b_slabs=True)
    gW["w_ff1"] = mm("d_ff1_w", n2b, da_b, "tn", out_slabs=True)

    def b_mid(d2, dn, h, m, gpost, gffn):
        _, vjp = jax.vjp(_rms, h, gffn)
        dh, dgffn = vjp(dn)
        dh1 = d2 + dh
        _, vjp2 = jax.vjp(_rms, m, gpost)
        dm, dgpost = vjp2(dh1)
        return dh1, dm, dgpost, dgffn

    dh1, dmo_b, g_mix_post, g_ffn_pre = rowwise(
        "d_mix_post", b_mid, [full(dh2), full(dn2), full(h1), full(mo)],
        [full(P["norm_mix_post"]), full(P["norm_ffn_pre"])], [(D, D, F32), (D, D, MXU)], [(D, D), (D, D)])
    dmerged = mm("d_out_x", dmo_b, W["out"], "nt")
    gW["w_out"] = mm("d_out_w", merged_b, dmo_b, "tn")

    def b_merge(d, a1, a2, g1, g2, b1, b2):
        s1, s2 = _sig(g1 + b1), _sig(g2 + b2)
        dg1 = d * a1 * s1 * (1.0 - s1)
        dg2 = d * a2 * s2 * (1.0 - s2)
        dg = jnp.concatenate([dg1, dg2], axis=1)
        return d * s1, d * s2, dg, jnp.sum(dg, axis=0, keepdims=True)

    du1_b, du2_b, dgp_b, g_b_gate = rowwise(
        "d_merge", b_merge, [full(dmerged), full(u1), full(u2)] + gate_rows, gate_bias,
        [(D, D, MXU), (D, D, MXU), (2 * D, 2 * D, MXU)], [(2 * D, 2 * D)])
    dy_ssd = mm("d_ssd_branch_x", du1_b, W["ssd"], "nt")
    dy_sb = mm("d_sb_branch_x", du2_b, W["sb"], "nt")
    gW["w_ssd_branch"] = mm("d_ssd_branch_w", y_ssd_b, du1_b, "tn")
    gW["w_sb_branch"] = mm("d_sb_branch_w", y_sb_b, du2_b, "tn")

    def b_gate(d, ysc, xs, z, dsk, nw):
        _, vjp = jax.vjp(f_gate, ysc, xs, z, dsk, nw)
        return vjp(d)

    slabs = {n: gW[n] if n in COL_SHARDED else gW[n].reshape((N_DEV,) + ROW_SHARDED[n]) for n in LATE}
    dy_scan, dxs_skip, dz_b, g_dsk_exp, g_ssd_norm, *from_sibling = rowwise(
        "d_ssd_gate", b_gate, [(dy_ssd, 0, GW), (y_scan, 0, GW), (xbc_act, 0, GW), (projmain, OFF_Z, GW)],
        [(P["dsk_exp"], 0, GW), (P["ssd_norm"], 0, GW)],
        [(D_INNER, GW, F32), (D_INNER, GW, F32), (D_INNER, GW, MXU)], [(D_INNER, GW), (D_INNER, GW)],
        tr=256, groups=SSD_GROUPS, side=exchange_plan([("pair", slabs[n]) for n in LATE]))
    chip_sums = [pair_sum("pair_sum_" + n, slabs[n], s, core) for n, s in zip(LATE, from_sibling)]
    dxs, d_b, d_c, ddt_part, g_a_exp, *late_parts = ssd_bwd(
        dy_scan, xbc_act, dte, cse, states, P["a_exp"], side=exchange_plan([("chips", s) for s in chip_sums]))

    def b_dt(dpart, dtr, bias):
        ddt = _dot3_l(dpart, _head_reduce())
        d = ddt * _sig(dtr + bias)
        return d, jnp.sum(d, axis=0, keepdims=True)

    ddt_b, g_dt_bias_pad = rowwise("d_dt", b_dt, [(ddt_part, 0, D_INNER), (dtraw, 0, LANES)],
                                   [(P["dt_bias_pad"], 0, LANES)], [(LANES, LANES, MXU)], [(LANES, LANES)])
    dxbc_b, g_conv_w, g_conv_b = conv_bwd(dxs, dxs_skip, d_b, d_c, projmain, P["conv_w"], P["conv_b"])
    dq_b, dk, dv = sb_bwd(projmain, dy_sb, t_exp, sb_kept)
    dmain_b = jnp.concatenate([dz_b, dxbc_b, dq_b, dk.astype(MXU), dv.astype(MXU)], axis=1)
    g_main = mm("d_in_w", n1b, dmain_b, "tn")
    g_dt = mm("d_dt_w", n1b, ddt_b, "tn")
    last = {"w_gate": mm("d_gate_w", n1b, dgp_b, "tn", out_slabs=True)}
    last["w_in"], conv_slabs = in_grad_slabs({"main": g_main, "dt": g_dt, "conv_w": g_conv_w})
    dn1_dt = mm("d_dt_x", ddt_b, W["dt"], "nt")
    dn1_gate, *from_sibling = mm("d_gate_x", dgp_b, W["gate"], "nt", add=dn1_dt, b_slabs=True,
                                 side=exchange_plan([("pair", last[n]) for n in FIRST]))
    chip_sums = [pair_sum("pair_sum_" + n, last[n], s, core) for n, s in zip(FIRST, from_sibling)]
    dn1, *first_parts = mm("d_in_x", dmain_b, W["main"], "nt", add=dn1_gate,
                           side=exchange_plan([("chips", s) for s in chip_sums]))

    def b_pre(d1, dn, xv, g):
        _, vjp = jax.vjp(_rms, xv, g)
        dx, dg = vjp(dn)
        return d1 + dx, dg

    grad_x, g_mix_pre = rowwise("d_norm_pre", b_pre, [full(dh1), full(dn1), full(x)], [full(P["norm_mix_pre"])],
                                [(D, D, F32)], [(D, D)])

    parts = {**dict(zip(LATE, late_parts)), **dict(zip(FIRST, first_parts))}
    gS = {
        "norm_mix_pre": g_mix_pre, "conv_b": g_conv_b, "dt_bias_pad": g_dt_bias_pad, "a_exp": g_a_exp,
        "dsk_exp": g_dsk_exp, "ssd_norm": g_ssd_norm, "b_gate": g_b_gate,
        "norm_mix_post": g_mix_post, "norm_ffn_pre": g_ffn_pre, "norm_ffn_post": g_ffn_post,
        "norm_ple_post": g_ple,
    }
    return lossc, grad_x, parts, conv_slabs, gS


IN_SPLITS = (2048, 6144, 6176, 7200, 8224)
IN_SHARD = 1156
IN_PAD = 1280
COL_SHARDED = {"w_in": (1024, IN_SHARD), "conv_w": (4, 512), "w_gate": (1024, 256), "w_ff1": (1024, 512),
               "w_ple": (256, 128)}
ROW_SHARDED = {"w_ssd_branch": (256, 1024), "w_sb_branch": (128, 1024), "w_out": (128, 1024),
               "w_ff2": (512, 1024), "w_ple_gate": (128, 1024)}
SHARDED = tuple(COL_SHARDED) + tuple(ROW_SHARDED)
SMALL = (
    ("norm_mix_pre", 1024), ("conv_b", 4096), ("dt_bias", 32), ("a_log", 32), ("d_skip", 32), ("ssd_norm", 2048),
    ("b_gate", 2048), ("norm_mix_post", 1024), ("norm_ffn_pre", 1024), ("norm_ffn_post", 1024),
    ("norm_ple_post", 1024),
)
ROW = 1024
SMALL_ROWS = 16


def _rows_of(n):
    return -(-n // ROW)


def _pad_to(v, n, axis=-1):
    pad = [(0, 0)] * v.ndim
    pad[axis] = (0, n - v.shape[axis])
    return jnp.pad(v, pad)


FIRST = ("w_in", "w_gate")
LATE = ("w_ssd_branch", "w_sb_branch", "w_out", "w_ff1", "w_ff2", "w_ple", "w_ple_gate")
LATE_SPLIT = 3


def in_matrices(w_in_slabs):
    w_in = jnp.concatenate([w_in_slabs[k, :, :IN_SHARD] for k in range(N_DEV)], axis=1)
    return {
        "main": jnp.concatenate([w_in[:, :IN_SPLITS[1]], w_in[:, IN_SPLITS[2]:]], axis=1),
        "dt": _pad_to(w_in[:, IN_SPLITS[1]:IN_SPLITS[2]], LANES),
    }


def late_matrices(g):
    short = {"w_ff1": "ff1", "w_ple": "ple", "w_ssd_branch": "ssd", "w_sb_branch": "sb", "w_out": "out",
             "w_ff2": "ff2", "w_ple_gate": "pg"}
    stack = lambda a: a.reshape(a.shape[0] * a.shape[1], a.shape[2])
    return {short[n]: a if n in COL_SHARDED else stack(a) for n, a in g.items()}


def build_small(small, conv_w_slabs):
    P = {k: small[k] for k in ("norm_mix_pre", "conv_b", "ssd_norm", "b_gate", "norm_mix_post", "norm_ffn_pre",
                               "norm_ffn_post", "norm_ple_post")}
    P["conv_w"] = jnp.concatenate([conv_w_slabs[k] for k in range(N_DEV)], axis=1)
    P["dt_bias_pad"] = _pad_to(small["dt_bias"], LANES)
    P["a_exp"] = jnp.repeat(-jnp.exp(small["a_log"]), HEAD_DIM, axis=1)
    P["dsk_exp"] = jnp.repeat(small["d_skip"], HEAD_DIM, axis=1)
    return P


def small_grads(gS, small):
    heads = lambda a: a.reshape(SSD_HEADS, HEAD_DIM).sum(axis=1)[None, :]
    out = {k: gS[k] for k in ("norm_mix_pre", "conv_b", "ssd_norm", "b_gate", "norm_mix_post", "norm_ffn_pre",
                              "norm_ffn_post", "norm_ple_post")}
    out["dt_bias"] = gS["dt_bias_pad"][:, :SSD_HEADS]
    out["a_log"] = heads(gS["a_exp"]) * (-jnp.exp(small["a_log"]))
    out["d_skip"] = heads(gS["dsk_exp"])
    return out


def in_grad_slabs(g):
    pieces = ((0, IN_SPLITS[1], g["main"], 0), (IN_SPLITS[1], IN_SPLITS[2], g["dt"], -IN_SPLITS[1]),
              (IN_SPLITS[2], N_DEV * IN_SHARD, g["main"], IN_SPLITS[1] - IN_SPLITS[2]))
    slabs = []
    for k in range(N_DEV):
        a, b = IN_SHARD * k, IN_SHARD * (k + 1)
        cut = [src[:, max(a, lo) + off:min(b, hi) + off] for lo, hi, src, off in pieces if max(a, lo) < min(b, hi)]
        slabs.append(_pad_to(jnp.concatenate(cut, axis=1), IN_PAD))
    width = COL_SHARDED["conv_w"][1]
    return jnp.stack(slabs), jnp.stack([g["conv_w"][:, width * k:width * (k + 1)] for k in range(N_DEV)])


def pack_small(get):
    cols = [_pad_to(get(name).reshape(n), _rows_of(n) * ROW) for name, n in SMALL]
    return jnp.concatenate(cols).reshape(SMALL_ROWS, ROW)


def unpack_small(flat):
    out, r0 = {}, 0
    for name, n in SMALL:
        rows = _rows_of(n)
        out[name] = flat[r0:r0 + rows].reshape(rows * ROW)[:n].reshape(1, n)
        r0 += rows
    return out


N_CHIPS = 4
JOB_SEMS = {"all": 7, "scatter": 7, "gather2": 7, "pair": 4, "chips": 3}


def exchange_plan(jobs):
    n = len(jobs)
    kinds = [k for k, _ in jobs]
    srcs = [s for _, s in jobs]
    shapes = {"all": lambda s: (N_DEV,) + s.shape, "gather2": lambda s: (N_DEV,) + s.shape,
              "scatter": lambda s: s.shape, "pair": lambda s: (N_CHIPS,) + s.shape[1:], "chips": lambda s: s.shape}
    out_shape = [jax.ShapeDtypeStruct(shapes[k](s), s.dtype) for k, s in jobs]
    offs = [sum(JOB_SEMS[k] for k in kinds[:i]) for i in range(n + 1)]

    def run(phases, src, out, send_sems, recv_sems, local_sems):
        x, y, c = lax.axis_index("x"), lax.axis_index("y"), lax.axis_index("c")
        dev = lambda d: 4 * d[0] + 2 * d[1] + d[2]
        chip_no = lambda ch: 2 * ch[0] + ch[1]
        me, sib, my_chip = (x, y, c), (x, y, 1 - c), (x, y)
        others = [(1 - x, y), (x, 1 - y), (1 - x, 1 - y)]
        all_chips = [(0, 0), (0, 1), (1, 0), (1, 1)]
        peers = [(1 - x if k & 4 else x, 1 - y if k & 2 else y, 1 - c if k & 1 else c) for k in range(1, N_DEV)]
        starts, recvs, local = [], [], []
        chained = [[] for _ in others]

        for i, kind in enumerate(kinds):
            s_ref, o_ref = src[i], out[i]

            def rc(k, src_ref, dst_ref, to, i=i):
                s = offs[i] + k
                return pltpu.make_async_remote_copy(src_ref=src_ref, dst_ref=dst_ref, send_sem=send_sems.at[s],
                                                    recv_sem=recv_sems.at[s], device_id=to,
                                                    device_id_type=pl.DeviceIdType.MESH)

            if kind == "all":
                local.append(pltpu.make_async_copy(s_ref, o_ref.at[dev(me)], local_sems.at[i]))
                for k, peer in enumerate(peers):
                    starts.append(rc(k, s_ref, o_ref.at[dev(me)], peer))
                    recvs.append(rc(k, s_ref, o_ref.at[dev(peer)], peer))
            elif kind == "scatter":
                local.append(pltpu.make_async_copy(s_ref.at[dev(me)], o_ref.at[dev(me)], local_sems.at[i]))
                for k, peer in enumerate(peers):
                    starts.append(rc(k, s_ref.at[dev(peer)], o_ref.at[dev(me)], peer))
                    recvs.append(rc(k, s_ref.at[dev(me)], o_ref.at[dev(peer)], peer))
            elif kind == "gather2":
                local.append(pltpu.make_async_copy(s_ref, o_ref.at[dev(me)], local_sems.at[i]))
                starts.append(rc(0, s_ref, o_ref.at[dev(me)], sib))
                recvs.append(rc(0, s_ref, o_ref.at[dev(sib)], sib))
                for j, ch in enumerate(others):
                    same, other = (*ch, c), (*ch, 1 - c)
                    starts.append(rc(1 + j, s_ref, o_ref.at[dev(me)], same))
                    chained[j].append((rc(1 + j, s_ref, o_ref.at[dev(same)], same),
                                       rc(4 + j, o_ref.at[dev(same)], o_ref.at[dev(same)], sib)))
                    recvs.append(rc(4 + j, s_ref, o_ref.at[dev(other)], sib))
            elif kind == "pair":
                for j, ch in enumerate(all_chips):
                    starts.append(rc(j, s_ref.at[dev((*ch, 1 - c))], o_ref.at[j], sib))
                    recvs.append(rc(j, s_ref.at[dev((*ch, c))], o_ref.at[j], sib))
            else:
                mine = chip_no(my_chip)
                local.append(pltpu.make_async_copy(s_ref.at[mine], o_ref.at[mine], local_sems.at[i]))
                for j, ch in enumerate(others):
                    starts.append(rc(j, s_ref.at[chip_no(ch)], o_ref.at[mine], (*ch, c)))
                    recvs.append(rc(j, s_ref.at[mine], o_ref.at[chip_no(ch)], (*ch, c)))

        if "start" in phases:
            for cp in local + starts:
                cp.start()
        if "pass" in phases:
            for group in chained:
                for arrival, forward in group:
                    arrival.wait_recv()
                    forward.start()
        if "finish" in phases:
            for cp in recvs:
                cp.wait_recv()
            for cp in starts + [forward for group in chained for _, forward in group]:
                cp.wait_send()
            for cp in local:
                cp.wait()

    scratch = [pltpu.SemaphoreType.DMA((offs[n],)), pltpu.SemaphoreType.DMA((offs[n],)),
               pltpu.SemaphoreType.DMA((n,))]
    return srcs, out_shape, scratch, run


EXCHANGE_PHASES = ("start", "pass", "finish")


def exchange(name, jobs):
    srcs, out_shape, scratch, run = exchange_plan(jobs)
    n = len(srcs)

    def body(*refs):
        run(EXCHANGE_PHASES, refs[:n], refs[n:2 * n], *refs[2 * n:])

    any_spec = pl.BlockSpec(memory_space=pl.ANY)
    return pl.pallas_call(body, name=name, out_shape=out_shape, in_specs=[any_spec] * n, out_specs=[any_spec] * n,
                          scratch_shapes=scratch)(*srcs)


def host_exchange(body, n_in, n_out, grid, plan):
    if plan is None:
        return body, [], [], [], [], []
    srcs, out_shape, scratch, run = plan
    n = len(srcs)
    steps = 1
    for g in grid:
        steps *= g

    def hosted(*refs):
        ins, side_in = refs[:n_in], refs[n_in:n_in + n]
        outs, side_out = refs[n_in + n:n_in + n + n_out], refs[n_in + n + n_out:n_in + 2 * n + n_out]
        rest = refs[n_in + 2 * n + n_out:]
        own, sems = rest[:len(rest) - 3], rest[len(rest) - 3:]
        step = 0
        for d, g in enumerate(grid):
            step = step * g + pl.program_id(d)

        def at(when, phase):
            @pl.when(step == when)
            def _():
                run((phase,), side_in, side_out, *sems)

        at(0, "start")
        body(*ins, *outs, *own)
        at(steps * 13 // 16, "pass")
        at(steps - 1, "finish")

    any_spec = pl.BlockSpec(memory_space=pl.ANY)
    return hosted, [any_spec] * n, list(srcs), list(out_shape), [any_spec] * n, list(scratch)


def pair_sum(name, g, sib, core):
    _, r, c = g.shape
    tr = _pick(r, (256, 128, 64, 32, 16, 8))

    def body(core_ref, g_ref, s_ref, o_ref):
        o_ref[...] = (g_ref[...] + s_ref[...]).astype(o_ref.dtype)

    return pl.pallas_call(
        body,
        name=name,
        out_shape=jax.ShapeDtypeStruct((N_CHIPS, r, c), MXU),
        grid_spec=pltpu.PrefetchScalarGridSpec(
            num_scalar_prefetch=1,
            grid=(N_CHIPS, r // tr),
            in_specs=[
                pl.BlockSpec((None, None, tr, c), lambda j, i, core_ref: (j, core_ref[0], i, 0)),
                pl.BlockSpec((None, tr, c), lambda j, i, core_ref: (j, i, 0)),
            ],
            out_specs=pl.BlockSpec((None, tr, c), lambda j, i, core_ref: (j, i, 0)),
        ),
        compiler_params=_cparams(("parallel", "parallel")),
    )(core, g.reshape(N_CHIPS, 2, r, c), sib)


def adamw(name, parts, w, m, v):
    rows, cols = w.shape
    tr = _pick(rows, (128, 64, 32, 16, 8))

    nparts = parts.shape[0]

    def body(p_ref, w_ref, m_ref, v_ref, g_ref, d_ref, m2_ref, v2_ref):
        g = p_ref[0].astype(F32)
        for k in range(1, nparts):
            g = g + p_ref[k].astype(F32)
        m2 = ADAM_B1 * m_ref[...] + (1.0 - ADAM_B1) * g
        v2 = ADAM_B2 * v_ref[...] + (1.0 - ADAM_B2) * jnp.square(g)
        m_hat = m2 / (1.0 - ADAM_B1 ** ADAM_STEP)
        v_hat = v2 / (1.0 - ADAM_B2 ** ADAM_STEP)
        g_ref[...] = g
        d_ref[...] = -ADAM_LR * (m_hat / (jnp.sqrt(v_hat) + ADAM_EPS) + ADAM_WD * w_ref[...])
        m2_ref[...] = m2
        v2_ref[...] = v2

    spec = pl.BlockSpec((tr, cols), lambda i: (i, 0))
    return pl.pallas_call(
        body,
        name=name,
        out_shape=[jax.ShapeDtypeStruct((rows, cols), F32)] * 4,
        grid=(rows // tr,),
        in_specs=[pl.BlockSpec((nparts, tr, cols), lambda i: (0, i, 0)), spec, spec, spec],
        out_specs=[spec] * 4,
        compiler_params=_cparams(("parallel",)),
    )(parts, w, m, v)


WEIGHT_ORDER = (
    "norm_mix_pre", "w_in", "conv_w", "conv_b", "dt_bias", "a_log", "d_skip", "ssd_norm", "w_ssd_branch",
    "w_sb_branch", "w_gate", "b_gate", "w_out", "norm_mix_post", "norm_ffn_pre", "w_ff1", "w_ff2", "norm_ffn_post",
    "w_ple", "w_ple_gate", "norm_ple_post",
)


def kernel(x, p, norm_mix_pre, w_in, conv_w, conv_b, dt_bias, a_log, d_skip, ssd_norm, w_ssd_branch, w_sb_branch, w_gate, b_gate, w_out, norm_mix_post, norm_ffn_pre, w_ff1, w_ff2, norm_ffn_post, w_ple, w_ple_gate, norm_ple_post, loss_target, m_norm_mix_pre, m_w_in, m_conv_w, m_conv_b, m_dt_bias, m_a_log, m_d_skip, m_ssd_norm, m_w_ssd_branch, m_w_sb_branch, m_w_gate, m_b_gate, m_w_out, m_norm_mix_post, m_norm_ffn_pre, m_w_ff1, m_w_ff2, m_norm_ffn_post, m_w_ple, m_w_ple_gate, m_norm_ple_post, v_norm_mix_pre, v_w_in, v_conv_w, v_conv_b, v_dt_bias, v_a_log, v_d_skip, v_ssd_norm, v_w_ssd_branch, v_w_sb_branch, v_w_gate, v_b_gate, v_w_out, v_norm_mix_post, v_norm_ffn_pre, v_w_ff1, v_w_ff2, v_norm_ffn_post, v_w_ple, v_w_ple_gate, v_norm_ple_post):
    a = dict(locals())
    seq = x.shape[1]
    x2 = x.reshape(seq, D_MODEL)
    target = loss_target.reshape(seq, D_MODEL)
    pb = p.reshape(seq, PLE_DIM).astype(MXU)

    def shard(prefix, name, dtype):
        v = a[prefix + name][0].astype(dtype)
        return _pad_to(v, IN_PAD) if name == "w_in" else v

    w_in_slabs, conv_w_slabs = exchange(
        "gather_weights", [("gather2", shard("", "w_in", MXU)), ("all", a["conv_w"][0])])
    small = {n: a[n] for n, _ in SMALL}
    core = lax.axis_index("c").astype(jnp.int32).reshape(1)

    lossc, grad_x, parts, conv_slabs, g_acc = local_step(
        x2, pb, target, in_matrices(w_in_slabs), build_small(small, conv_w_slabs),
        {n: shard("", n, MXU) for n in LATE + ("w_gate",)}, core)
    loss = lax.psum(jnp.sum(lossc), ("x", "y", "c"))
    g_small = small_grads(g_acc, small)

    parts["conv_w"], small_parts = exchange(
        "small_grads", [("scatter", conv_slabs), ("all", pack_small(lambda n: g_small[n]))])

    leaves = {}
    for n, part in parts.items():
        res = adamw("adamw_" + n, part, shard("", n, F32), shard("m_", n, F32), shard("v_", n, F32))
        leaves[n] = [r[None, :, :IN_SHARD] if n == "w_in" else r[None] for r in res]
    res = adamw("adamw_small", small_parts, pack_small(lambda n: a[n]), pack_small(lambda n: a["m_" + n]),
                pack_small(lambda n: a["v_" + n]))
    for j, r in enumerate(res):
        for n, leaf in unpack_small(r).items():
            leaves.setdefault(n, [None] * 4)[j] = leaf
    outs = [loss, grad_x.reshape(x.shape)]
    for j in range(4):
        outs += [leaves[n][j] for n in WEIGHT_ORDER]
    return tuple(outs)
```

```python
import functools

import jax
import jax.numpy as jnp
from jax import lax
from jax.experimental import pallas as pl
from jax.experimental.pallas import tpu as pltpu

F32 = jnp.float32
MXU = jnp.bfloat16
VMEM_LIMIT = 56 * 1024 * 1024

D_MODEL = 1024
D_INNER = 2048
SSD_HEADS = 32
HEAD_DIM = 64
SSD_GROUPS = 8
D_STATE = 128
CONV_K = 4
CONV_DIM = 4096
CHUNK = 128
SB_WIDTH = 1024
D_FF = 4096
PLE_DIM = 256
RMS_EPS = 1e-6
SB_SCALE = HEAD_DIM ** -0.5
N_DEV = 8
LANES = 128

OFF_Z, OFF_XBC, OFF_Q, OFF_K, OFF_V = 0, 2048, 6144, 7168, 8192

ADAM_LR = 0.001
ADAM_B1 = 0.9
ADAM_B2 = 0.999
ADAM_EPS = 1e-08
ADAM_WD = 0.01
ADAM_STEP = 10


def _sig(x):
    return 0.5 * jnp.tanh(0.5 * x) + 0.5


def _softplus(x):
    return jnp.maximum(x, 0.0) + jnp.log(1.0 + jnp.exp(-jnp.abs(x)))


def _rms(x, w):
    return x * lax.rsqrt(jnp.mean(x * x, axis=-1, keepdims=True) + RMS_EPS) * w


def _dot(a, b):
    return jnp.dot(a, b, preferred_element_type=F32)


def _dot_nt(a, b):
    return lax.dot_general(a, b, (((1,), (1,)), ((), ())), preferred_element_type=F32)


def _dot_tn(a, b):
    return lax.dot_general(a, b, (((0,), (0,)), ((), ())), preferred_element_type=F32)


def _split3(x):
    x1 = x.astype(MXU)
    r = x - x1.astype(F32)
    x2 = r.astype(MXU)
    r = r - x2.astype(F32)
    return x1, x2, r.astype(MXU)


def _dot3_l(a, u, parts=3):
    m = a.shape[0]
    d = _dot(jnp.concatenate(_split3(a)[:parts], axis=0), u)
    if parts == 2:
        return d[m:] + d[:m]
    return (d[2 * m:] + d[m:2 * m]) + d[:m]


def _dot3_r(u, a):
    n = a.shape[1]
    d = _dot(u, jnp.concatenate(_split3(a), axis=1))
    return (d[:, 2 * n:] + d[:, n:2 * n]) + d[:, :n]


def _iota(shape, dim):
    return lax.broadcasted_iota(jnp.int32, shape, dim)


def _tri(n, cmp):
    r, c = _iota((n, n), 0), _iota((n, n), 1)
    return cmp(r, c).astype(F32).astype(MXU)


def _cparams(sem):
    return pltpu.CompilerParams(dimension_semantics=sem, vmem_limit_bytes=VMEM_LIMIT)


def _pick(n, cands):
    for c in cands:
        if n % c == 0:
            return c
    return n


def mm(name, a, b, mode, add=None, out_dtype=F32, b_slabs=False, out_slabs=False, epi=None, epi_args=(), extra=(),
       side=None):
    slab = None
    if b_slabs:
        slab = b.shape[2]
        bshape = (b.shape[1], N_DEV * slab)
    else:
        bshape = b.shape
    if mode == "nn":
        (M, K), (K2, N) = a.shape, bshape
    elif mode == "nt":
        (M, K), (N, K2) = a.shape, bshape
    else:
        (K, M), (K2, N) = a.shape, bshape
    assert K == K2, (name, a.shape, b.shape)
    tm = _pick(M, (1024, 512, 256, 128))
    tn = _pick(N, (1024, 512, 256, 128))
    tk = _pick(K, (1024, 512, 256, 128))
    if b_slabs and mode == "nn":
        tn = slab
    if b_slabs and mode == "nt":
        tk = slab
    if out_slabs:
        assert mode == "tn" and N % N_DEV == 0
        tn = N // N_DEV
    nk = K // tk

    def body(*refs):
        refs = list(refs)
        a_ref, b_ref = refs[:2]
        add_ref = refs[2] if add is not None else None
        n_in = 2 + (add is not None)
        epi_refs = refs[n_in:n_in + len(epi_args)]
        o_ref = refs[n_in + len(epi_args)]
        extra_refs = refs[n_in + len(epi_args) + 1:-1]
        acc = refs[-1]
        k = pl.program_id(2)

        @pl.when(k == 0)
        def _():
            acc[...] = jnp.zeros_like(acc) if add is None else add_ref[...]

        av, bv = a_ref[...], b_ref[...]
        if mode == "nn":
            acc[...] += _dot(av, bv)
        elif mode == "nt":
            acc[...] += _dot_nt(av, bv)
        else:
            acc[...] += _dot_tn(av, bv)

        @pl.when(k == nk - 1)
        def _():
            res = acc[...]
            main = res if epi is None else epi(res, *[r[...] for r in epi_refs])
            o_ref[...] = main.astype(o_ref.dtype)
            for r, (_, fn) in zip(extra_refs, extra):
                r[...] = fn(res).astype(r.dtype)

    if mode == "nn":
        a_spec = pl.BlockSpec((tm, tk), lambda i, j, k: (i, k))
        b_spec = pl.BlockSpec((tk, tn), lambda i, j, k: (k, j))
    elif mode == "nt":
        a_spec = pl.BlockSpec((tm, tk), lambda i, j, k: (i, k))
        b_spec = pl.BlockSpec((tn, tk), lambda i, j, k: (j, k))
    else:
        a_spec = pl.BlockSpec((tk, tm), lambda i, j, k: (k, i))
        b_spec = pl.BlockSpec((tk, tn), lambda i, j, k: (k, j))
    if b_slabs and mode == "nn":
        b_spec = pl.BlockSpec((None, tk, tn), lambda i, j, k: (j, k, 0))
    if b_slabs and mode == "nt":
        b_spec = pl.BlockSpec((None, tn, tk), lambda i, j, k: (k, j, 0))
    o_spec = pl.BlockSpec((tm, tn), lambda i, j, k: (i, j))
    in_specs, args = [a_spec, b_spec], [a, b]
    if add is not None:
        in_specs.append(o_spec)
        args.append(add)
    for e in epi_args:
        in_specs.append(o_spec)
        args.append(e)
    out_sds = jax.ShapeDtypeStruct((M, N), out_dtype)
    if out_slabs:
        o_spec = pl.BlockSpec((None, tm, tn), lambda i, j, k: (j, i, 0))
        out_sds = jax.ShapeDtypeStruct((N_DEV, M, tn), out_dtype)
    grid = (M // tm, N // tn, nk)
    body, s_in, s_args, s_shape, s_out, s_scr = host_exchange(body, len(args), 1 + len(extra), grid, side)
    res = pl.pallas_call(
        body,
        name=name,
        out_shape=[out_sds] + [jax.ShapeDtypeStruct((M, N), dt) for dt, _ in extra] + s_shape,
        grid=grid,
        in_specs=in_specs + s_in,
        out_specs=[o_spec] * (1 + len(extra)) + s_out,
        scratch_shapes=[pltpu.VMEM((tm, tn), F32)] + s_scr,
        compiler_params=_cparams(("parallel", "parallel", "arbitrary") if side is None else ("arbitrary",) * 3),
    )(*args, *s_args)
    return res if extra or side is not None else res[0]


def rowwise(name, fn, rows, bcast, outs, accs=(), tr=512, ncb=1, side=None, groups=1):
    S = rows[0][0].shape[0]
    tr = min(tr, S)
    nrb = S // tr
    G = groups
    in_specs, args, in_w = [], [], []
    for arr, off, w in rows:
        assert off % (w * G) == 0 and arr.shape[0] == S
        in_specs.append(pl.BlockSpec((tr, w * G), lambda j, i, ob=off // (w * G): (i, ob + j)))
        args.append(arr)
        in_w.append(w)
    for arr, off, w in bcast:
        assert off % (w * G) == 0
        in_specs.append(pl.BlockSpec((arr.shape[0], w * G), lambda j, i, ob=off // (w * G): (0, ob + j)))
        args.append(arr)
        in_w.append(w)
    out_shape, out_specs, out_w = [], [], []
    for tw, w, dt in outs:
        out_shape.append(jax.ShapeDtypeStruct((S, tw), dt))
        out_specs.append(pl.BlockSpec((tr, w * G), lambda j, i: (i, j)))
        out_w.append(w)
    for tw, w in accs:
        out_shape.append(jax.ShapeDtypeStruct((1, tw), F32))
        out_specs.append(pl.BlockSpec((1, w * G), lambda j, i: (0, j)))
        out_w.append(w)
    nin, nout = len(args), len(outs)

    def body(*refs):
        i = pl.program_id(1)
        for g in range(G):
            cut = lambda w: slice(g * w, (g + 1) * w)
            res = fn(*[r[:, cut(w)] for r, w in zip(refs[:nin], in_w)])
            for k, (r, v) in enumerate(zip(refs[nin:], res)):
                cols = cut(out_w[k])
                if k < nout:
                    r[:, cols] = v.astype(r.dtype)
                    continue

                @pl.when(i == 0)
                def _(r=r, v=v, cols=cols):
                    r[:, cols] = v

                @pl.when(i > 0)
                def _(r=r, v=v, cols=cols):
                    r[:, cols] += v

    body, s_in, s_args, s_shape, s_out, s_scr = host_exchange(body, nin, len(out_shape), (ncb, nrb), side)
    res = pl.pallas_call(
        body,
        name=name,
        out_shape=out_shape + s_shape,
        grid=(ncb, nrb),
        in_specs=in_specs + s_in,
        out_specs=out_specs + s_out,
        scratch_shapes=s_scr,
        compiler_params=_cparams(("parallel" if side is None else "arbitrary", "arbitrary")),
    )(*args, *s_args)
    return res


CONV_TC = 128


CONV_R = 256
HALO = 8


def _conv_pre(e, w, b):
    shifted = [pltpu.roll(e, s, 0) for s in (1, 2, 3)]
    pre = b + w[3:4, :] * e
    for s in (1, 2, 3):
        pre = pre + w[3 - s:4 - s, :] * shifted[s - 1]
    return pre, shifted


def conv_fwd(projmain, conv_w, conv_b):
    S = projmain.shape[0]
    tc = CONV_TC

    def body(u_ref, w_ref, b_ref, o_ref):
        u, w = u_ref[...], w_ref[...]
        row = _iota(u.shape, 0)
        pre = b_ref[...] + w[3:4, :] * u
        for s in (1, 2, 3):
            pre = pre + w[3 - s:4 - s, :] * jnp.where(row >= s, pltpu.roll(u, s, 0), 0.0)
        o_ref[...] = pre * _sig(pre)

    return pl.pallas_call(
        body,
        name="conv_fwd",
        out_shape=jax.ShapeDtypeStruct((S, CONV_DIM), F32),
        grid=(CONV_DIM // tc,),
        in_specs=[
            pl.BlockSpec((S, tc), lambda j: (0, OFF_XBC // tc + j)),
            pl.BlockSpec((CONV_K, tc), lambda j: (0, j)),
            pl.BlockSpec((1, tc), lambda j: (0, j)),
        ],
        out_specs=pl.BlockSpec((S, tc), lambda j: (0, j)),
        compiler_params=_cparams(("parallel",)),
    )(projmain, conv_w, conv_b)


def conv_bwd(dxs, dxs_skip, d_b, d_c, projmain, conv_w, conv_b):
    S = projmain.shape[0]
    tc = CONV_TC
    n_xs, n_b = D_INNER // tc, SSD_GROUPS * D_STATE // tc

    def body(dx_ref, dskip_ref, dbm_ref, dcm_ref, u_ref, w_ref, b_ref, du_ref, dw_ref, db_ref, u_pad, d_pad):
        j = pl.program_id(0)
        zeros = jnp.zeros((HALO, tc), F32)
        for pad in (u_pad, d_pad):
            pad[0:HALO, :] = zeros
            pad[HALO + S:2 * HALO + S, :] = zeros
        u_pad[HALO:HALO + S, :] = u_ref[...]
        d_pad[HALO:HALO + S, :] = jnp.where(j < n_xs, dx_ref[...] + dskip_ref[...],
                                            jnp.where(j < n_xs + n_b, dbm_ref[...], dcm_ref[...]))
        w, b = w_ref[...], b_ref[...]
        n = CONV_R + 2 * HALO
        keep = slice(HALO, HALO + CONV_R)

        def chunk(c, sums):
            r0 = pl.multiple_of(c * CONV_R, CONV_R)
            e = u_pad[pl.ds(r0, n), :]
            pre, shifted = _conv_pre(e, w, b)
            sg = _sig(pre)
            dpre = d_pad[pl.ds(r0, n), :] * (sg * (1.0 + pre * (1.0 - sg)))
            du = w[3:4, :] * dpre
            for s in (1, 2, 3):
                du = du + w[3 - s:4 - s, :] * pltpu.roll(dpre, n - s, 0)
            du_ref[pl.ds(r0, CONV_R), :] = du[keep].astype(du_ref.dtype)
            dk = dpre[keep]
            taps = [shifted[2], shifted[1], shifted[0], e]
            return tuple(acc + jnp.sum(dk * t[keep], axis=0, keepdims=True) for acc, t in zip(sums[:4], taps)) + (
                sums[4] + jnp.sum(dk, axis=0, keepdims=True),)

        zero = jnp.zeros((1, tc), F32)
        sums = lax.fori_loop(0, S // CONV_R, chunk, (zero,) * 5)
        for k in range(CONV_K):
            dw_ref[k:k + 1, :] = sums[k]
        db_ref[...] = sums[4]

    return pl.pallas_call(
        body,
        name="conv_bwd",
        out_shape=[
            jax.ShapeDtypeStruct((S, CONV_DIM), MXU),
            jax.ShapeDtypeStruct((CONV_K, CONV_DIM), F32),
            jax.ShapeDtypeStruct((1, CONV_DIM), F32),
        ],
        grid=(CONV_DIM // tc,),
        in_specs=[
            pl.BlockSpec((S, tc), lambda j: (0, jnp.minimum(j, n_xs - 1))),
            pl.BlockSpec((S, tc), lambda j: (0, jnp.minimum(j, n_xs - 1))),
            pl.BlockSpec((S, tc), lambda j: (0, jnp.clip(j - n_xs, 0, n_b - 1))),
            pl.BlockSpec((S, tc), lambda j: (0, jnp.clip(j - n_xs - n_b, 0, n_b - 1))),
            pl.BlockSpec((S, tc), lambda j: (0, OFF_XBC // tc + j)),
            pl.BlockSpec((CONV_K, tc), lambda j: (0, j)),
            pl.BlockSpec((1, tc), lambda j: (0, j)),
        ],
        out_specs=[
            pl.BlockSpec((S, tc), lambda j: (0, j)),
            pl.BlockSpec((CONV_K, tc), lambda j: (0, j)),
            pl.BlockSpec((1, tc), lambda j: (0, j)),
        ],
        scratch_shapes=[pltpu.VMEM((S + 2 * HALO, tc), F32)] * 2,
        compiler_params=_cparams(("arbitrary",)),
    )(dxs, dxs_skip, d_b, d_c, projmain, conv_w, conv_b)


def _head_expand():
    r, j = _iota((LANES, D_INNER), 0), _iota((LANES, D_INNER), 1)
    return ((j >= r * HEAD_DIM) & (j < r * HEAD_DIM + HEAD_DIM)).astype(F32).astype(MXU)


def _head_reduce():
    j, r = _iota((D_INNER, LANES), 0), _iota((D_INNER, LANES), 1)
    return ((j >= r * HEAD_DIM) & (j < r * HEAD_DIM + HEAD_DIM)).astype(F32).astype(MXU)


def ssd_prep(dtraw, dt_bias_pad, a_exp):
    S = dtraw.shape[0]

    def body(dtr_ref, bias_ref, a_ref, dte_ref, cse_ref):
        dt = _softplus(dtr_ref[...] + bias_ref[...])
        dte = _dot3_l(dt, _head_expand())
        dte_ref[...] = dte
        incl = _tri(CHUNK, lambda r, c: r >= c)
        cse_ref[...] = _dot3_r(incl, dte * a_ref[...])

    return pl.pallas_call(
        body,
        name="ssd_prep",
        out_shape=[jax.ShapeDtypeStruct((S, D_INNER), F32)] * 2,
        grid=(S // CHUNK,),
        in_specs=[
            pl.BlockSpec((CHUNK, LANES), lambda c: (c, 0)),
            pl.BlockSpec((1, LANES), lambda c: (0, 0)),
            pl.BlockSpec((1, D_INNER), lambda c: (0, 0)),
        ],
        out_specs=[pl.BlockSpec((CHUNK, D_INNER), lambda c: (c, 0))] * 2,
        compiler_params=_cparams(("parallel",)),
    )(dtraw, dt_bias_pad, a_exp)


GW = 4 * HEAD_DIM
SSD_GPS = 8
SSD_GPS_BWD = 8


def ssd_fwd(xbc_act, dte, cse, side=None):
    S = xbc_act.shape[0]
    nc = S // CHUNK

    def body(xs_ref, b_ref, c_ref, dte_ref, cse_ref, y_ref, st_ref, s_scr):
        c = pl.program_id(1)

        @pl.when(c == 0)
        def _():
            s_scr[...] = jnp.zeros_like(s_scr)

        for gg in range(SSD_GPS):
            ch, st = slice(GW * gg, GW * (gg + 1)), slice(D_STATE * gg, D_STATE * (gg + 1))
            s_in = s_scr[:, ch]
            st_ref[0, :, ch] = s_in
            cs = cse_ref[:, ch]
            xd = xs_ref[:, ch] * dte_ref[:, ch]
            btb = b_ref[:, st].T.astype(MXU)
            cs_last = cs[CHUNK - 1:CHUNK, :]
            gy = _dot(c_ref[:, st].astype(MXU), jnp.concatenate([btb, s_in.astype(MXU)], axis=1))
            g, y = gy[:, :CHUNK], gy[:, CHUNK:] * jnp.exp(cs)
            cs_t = cs.T
            row, col = _iota((CHUNK, CHUNK), 0), _iota((CHUNK, CHUNK), 1)
            ms = []
            for h in range(4):
                lo = HEAD_DIM * h
                lam = jnp.where(row >= col, jnp.exp(cs[:, lo:lo + 1] - cs_t[lo:lo + 1, :]), 0.0)
                ms.append((g * lam).astype(MXU))
            yd = _dot(jnp.concatenate(ms, axis=0), xd.astype(MXU))
            lane = _iota((CHUNK, GW), 1)
            for h in range(4):
                y = y + jnp.where((lane >= HEAD_DIM * h) & (lane < HEAD_DIM * (h + 1)),
                                  yd[CHUNK * h:CHUNK * (h + 1)], 0.0)
            y_ref[:, ch] = y
            w = (xd * jnp.exp(cs_last - cs)).astype(MXU)
            s_scr[:, ch] = jnp.exp(cs_last) * s_in + _dot(btb, w)

    grid = (SSD_GROUPS // SSD_GPS, nc)
    gw, gs = GW * SSD_GPS, D_STATE * SSD_GPS
    body, s_in, s_args, s_shape, s_out, s_scr = host_exchange(body, 5, 2, grid, side)
    return pl.pallas_call(
        body,
        name="ssd_fwd",
        out_shape=[
            jax.ShapeDtypeStruct((S, D_INNER), F32),
            jax.ShapeDtypeStruct((nc, D_STATE, D_INNER), F32),
        ] + s_shape,
        grid=grid,
        in_specs=[
            pl.BlockSpec((CHUNK, gw), lambda g, c: (c, g)),
            pl.BlockSpec((CHUNK, gs), lambda g, c: (c, D_INNER // gs + g)),
            pl.BlockSpec((CHUNK, gs), lambda g, c: (c, (D_INNER + SSD_GROUPS * D_STATE) // gs + g)),
            pl.BlockSpec((CHUNK, gw), lambda g, c: (c, g)),
            pl.BlockSpec((CHUNK, gw), lambda g, c: (c, g)),
        ] + s_in,
        out_specs=[
            pl.BlockSpec((CHUNK, gw), lambda g, c: (c, g)),
            pl.BlockSpec((1, D_STATE, gw), lambda g, c: (c, 0, g)),
        ] + s_out,
        scratch_shapes=[pltpu.VMEM((D_STATE, gw), F32)] + s_scr,
        compiler_params=_cparams(("parallel" if side is None else "arbitrary", "arbitrary")),
    )(xbc_act, xbc_act, xbc_act, dte, cse, *s_args)


def ssd_bwd(dy, xbc_act, dte, cse, states, a_exp, side=None):
    S = xbc_act.shape[0]
    nc = S // CHUNK

    def body(*refs):
        for gg in range(SSD_GPS_BWD):
            ch, st = slice(GW * gg, GW * (gg + 1)), slice(D_STATE * gg, D_STATE * (gg + 1))
            cut = {GW * SSD_GPS_BWD: ch, D_STATE * SSD_GPS_BWD: st}
            one_group(*[r.at[(slice(None),) * (len(r.shape) - 1) + (cut[r.shape[-1]],)] for r in refs])

    def one_group(dy_ref, xs_ref, b_ref, c_ref, dte_ref, cse_ref, sin_ref, sout_ref, a_ref,
                  dxs_ref, db_ref, dc_ref, ddt_ref, dal_ref, ds_scr):
        j = pl.program_id(1)

        @pl.when(j == 0)
        def _():
            ds_scr[...] = jnp.zeros_like(ds_scr)
            dal_ref[...] = jnp.zeros_like(dal_ref)

        ds_out = ds_scr[...]
        dyv, xs = dy_ref[...], xs_ref[...]
        dt, cs = dte_ref[...], cse_ref[...]
        s_in = sin_ref[0]
        bm, cm = b_ref[...], c_ref[...]
        bb, cb = bm.astype(MXU), cm.astype(MXU)
        btb, ctb = bm.T.astype(MXU), cm.T.astype(MXU)
        dsb, sib = ds_out.astype(MXU), s_in.astype(MXU)
        xd = xs * dt
        ecs = jnp.exp(cs)
        cs_last = cs[CHUNK - 1:CHUNK, :]
        eend = jnp.exp(cs_last - cs)
        gy = _dot(cb, jnp.concatenate([btb, sib], axis=1))
        g, yoff = gy[:, :CHUNK], gy[:, CHUNK:] * ecs
        gd = _dot(bb, jnp.concatenate([ctb, dsb], axis=1))
        g_t, dxd_off = gd[:, :CHUNK], gd[:, CHUNK:] * eend
        cs_t = cs.T
        row, col = _iota((CHUNK, CHUNK), 0), _iota((CHUNK, CHUNK), 1)
        lane = _iota((CHUNK, GW), 1)
        heads = [(lane >= HEAD_DIM * h) & (lane < HEAD_DIM * (h + 1)) for h in range(4)]
        dyb, xdb = dyv.astype(MXU), xd.astype(MXU)
        dm_all = _dot_nt(jnp.concatenate([jnp.where(hm, dyv, 0.0) for hm in heads], axis=0).astype(MXU), xdb)
        dmt_all = _dot_nt(jnp.concatenate([jnp.where(hm, xd, 0.0) for hm in heads], axis=0).astype(MXU), dyb)
        lams, m_ts = [], []
        for h in range(4):
            lo = HEAD_DIM * h
            cs_col, cs_row = cs[:, lo:lo + 1], cs_t[lo:lo + 1, :]
            lams.append(jnp.where(row >= col, jnp.exp(cs_col - cs_row), 0.0))
            m_ts.append(g_t * jnp.where(col >= row, jnp.exp(cs_row - cs_col), 0.0))
        acc_all = _dot(jnp.concatenate(m_ts, axis=0).astype(MXU), dyb)
        dxd = dxd_off
        dg = jnp.zeros((CHUNK, CHUNK), F32)
        dcs = dyv * yoff - xd * dxd_off
        for h in range(4):
            blk = slice(CHUNK * h, CHUNK * (h + 1))
            dm, dm_t = dm_all[blk], dmt_all[blk]
            dxd = dxd + jnp.where(heads[h], acc_all[blk], 0.0)
            dg = dg + dm * lams[h]
            wdiff = (jnp.sum(dm * (g * lams[h]), axis=1, keepdims=True)
                     - jnp.sum(dm_t * m_ts[h], axis=1, keepdims=True))
            dcs = dcs + jnp.where(lane == HEAD_DIM * h, wdiff, 0.0)
        dye = (dyv * ecs).astype(MXU)
        dc_ref[...] = _dot(dg.astype(MXU), bb) + _dot_nt(dye, sib)
        db_ref[...] = _dot(dg.T.astype(MXU), cb) + _dot_nt((xd * eend).astype(MXU), dsb)
        ds_scr[...] = jnp.exp(cs_last) * ds_out + _dot(ctb, dye)
        last = jnp.sum(ds_out * sout_ref[0], axis=0, keepdims=True)
        rows = _iota((CHUNK, GW), 0)
        dcs = dcs + jnp.where(rows == CHUNK - 1, last, 0.0)
        dda = _dot3_r(_tri(CHUNK, lambda r, c: c >= r), dcs)
        ddt_ref[...] = a_ref[...] * dda + dxd * xs
        dal_ref[...] += jnp.sum(dt * dda, axis=0, keepdims=True)
        dxs_ref[...] = dxd * dt

    rc = lambda g, j: (nc - 1 - j, g)
    grid = (SSD_GROUPS // SSD_GPS_BWD, nc)
    gw, gs = GW * SSD_GPS_BWD, D_STATE * SSD_GPS_BWD
    body, s_in, s_args, s_shape, s_out, s_scr = host_exchange(body, 9, 5, grid, side)
    return pl.pallas_call(
        body,
        name="ssd_bwd",
        out_shape=[
            jax.ShapeDtypeStruct((S, D_INNER), F32),
            jax.ShapeDtypeStruct((S, SSD_GROUPS * D_STATE), F32),
            jax.ShapeDtypeStruct((S, SSD_GROUPS * D_STATE), F32),
            jax.ShapeDtypeStruct((S, D_INNER), F32),
            jax.ShapeDtypeStruct((1, D_INNER), F32),
        ] + s_shape,
        grid=grid,
        in_specs=[
            pl.BlockSpec((CHUNK, gw), rc),
            pl.BlockSpec((CHUNK, gw), rc),
            pl.BlockSpec((CHUNK, gs), lambda g, j: (nc - 1 - j, D_INNER // gs + g)),
            pl.BlockSpec((CHUNK, gs), lambda g, j: (nc - 1 - j, (D_INNER + SSD_GROUPS * D_STATE) // gs + g)),
            pl.BlockSpec((CHUNK, gw), rc),
            pl.BlockSpec((CHUNK, gw), rc),
            pl.BlockSpec((1, D_STATE, gw), lambda g, j: (nc - 1 - j, 0, g)),
            pl.BlockSpec((1, D_STATE, gw), lambda g, j: (jnp.minimum(nc - j, nc - 1), 0, g)),
            pl.BlockSpec((1, gw), lambda g, j: (0, g)),
        ] + s_in,
        out_specs=[
            pl.BlockSpec((CHUNK, gw), rc),
            pl.BlockSpec((CHUNK, gs), rc),
            pl.BlockSpec((CHUNK, gs), rc),
            pl.BlockSpec((CHUNK, gw), rc),
            pl.BlockSpec((1, gw), lambda g, j: (0, g)),
        ] + s_out,
        scratch_shapes=[pltpu.VMEM((D_STATE, gw), F32)] + s_scr,
        compiler_params=_cparams(("parallel" if side is None else "arbitrary", "arbitrary")),
    )(dy, xbc_act, xbc_act, xbc_act, dte, cse, states, states, a_exp, *s_args)


SB_T = 256
SB_DROP = 104.0
SB_PAIRS = 1
SB_PARTS = 2
SB_W = SB_PAIRS * LANES
SB_LANES = [slice(LANES * p, LANES * (p + 1)) for p in range(SB_PAIRS)]


def _sb_scores(qm, k_ref, ks, rowi, coli, diag):
    kblk = k_ref[pl.ds(ks, SB_T), :].astype(MXU)
    z = _dot_nt(qm, kblk) * SB_SCALE
    sp = _softplus(z)
    if not diag:
        return kblk, z, None, sp, sp
    mask = (ks + coli) < rowi
    return kblk, z, mask, sp, jnp.where(mask, sp, 0.0)


def _sb_stack(v):
    lane = _iota(v.shape, 1)
    return jnp.concatenate([jnp.where(lane < HEAD_DIM, v, 0.0), jnp.where(lane >= HEAD_DIM, v, 0.0)], axis=0)


def _sb_unstack(v):
    lane = _iota((SB_T, LANES), 1)
    return jnp.where(lane < HEAD_DIM, v[:SB_T], v[SB_T:])


def _sb_rows(qb):
    r = _iota((2 * SB_T, SB_T), 0)
    return qb * SB_T + jnp.where(r >= SB_T, r - SB_T, r), _iota((2 * SB_T, SB_T), 1)


def sb_fwd(projmain, side=None):
    S = projmain.shape[0]
    nq = S // SB_T

    def body(q_ref, k_ref, v_ref, o_ref, t_ref, n_ref):
        hp, qb = pl.program_id(0), pl.program_id(1)
        qsts = [_sb_stack(q_ref[:, sl]).astype(MXU) for sl in SB_LANES]
        rowi, coli = _sb_rows(qb)
        u_after = _tri(SB_T, lambda r, c: r > c)

        def cond(carry):
            i, rmin, _, _ = carry
            return (i <= qb) & (rmin < SB_DROP)

        def kstep(carry, diag=False):
            i, _, rs, accs = carry
            ks = pl.multiple_of((qb - i) * SB_T, SB_T)
            rs, accs = list(rs), list(accs)
            for p, sl in enumerate(SB_LANES):
                _, z, mask, sp, spm = _sb_scores(qsts[p], k_ref.at[:, sl], ks, rowi, coli, diag)
                vblk = v_ref[pl.ds(ks, SB_T), sl].astype(MXU)
                a = jnp.exp(z - sp - _dot3_l(spm, u_after, SB_PARTS) - rs[p])
                if diag:
                    a = jnp.where(mask, a, 0.0)
                accs[p] = accs[p] + _dot(a.astype(MXU), vblk)
                rs[p] = rs[p] + jnp.sum(spm, axis=1, keepdims=True)
            return i + 1, jnp.min(functools.reduce(jnp.minimum, rs)), rs, accs

        first = kstep((jnp.int32(0), jnp.float32(0.0), [jnp.zeros((2 * SB_T, 1), F32)] * SB_PAIRS,
                       [jnp.zeros((2 * SB_T, LANES), F32)] * SB_PAIRS), diag=True)
        n, _, rs, accs = lax.while_loop(cond, kstep, first)
        for p, sl in enumerate(SB_LANES):
            o_ref[:, sl] = _sb_unstack(accs[p]).astype(o_ref.dtype)
            t_ref[:, sl] = _sb_unstack(jnp.broadcast_to(rs[p], (2 * SB_T, LANES)))
        n_ref[hp, qb] = n

    grid = (SB_WIDTH // SB_W, nq)
    body, s_in, s_args, s_shape, s_out, s_scr = host_exchange(body, 3, 3, grid, side)
    return pl.pallas_call(
        body,
        name="sb_fwd",
        out_shape=[jax.ShapeDtypeStruct((S, SB_WIDTH), MXU), jax.ShapeDtypeStruct((S, SB_WIDTH), F32),
                   jax.ShapeDtypeStruct((SB_WIDTH // SB_W, nq), jnp.int32)] + s_shape,
        grid=grid,
        in_specs=[
            pl.BlockSpec((SB_T, SB_W), lambda h, i: (i, OFF_Q // SB_W + h)),
            pl.BlockSpec((S, SB_W), lambda h, i: (0, OFF_K // SB_W + h)),
            pl.BlockSpec((S, SB_W), lambda h, i: (0, OFF_V // SB_W + h)),
        ] + s_in,
        out_specs=[pl.BlockSpec((SB_T, SB_W), lambda h, i: (i, h))] * 2
        + [pl.BlockSpec(memory_space=pltpu.SMEM)] + s_out,
        scratch_shapes=s_scr,
        compiler_params=_cparams(("arbitrary", "arbitrary")),
    )(projmain, projmain, projmain, *s_args)


def sb_bwd(projmain, do, t_exp, nblk, side=None):
    S = projmain.shape[0]
    nq = S // SB_T

    def body(n_ref, q_ref, k_ref, v_ref, do_ref, t_ref, dq_ref, dk_ref, dv_ref):
        hp, qb = pl.program_id(0), pl.program_id(1)

        @pl.when(qb == 0)
        def _():
            dk_ref[...] = jnp.zeros_like(dk_ref)
            dv_ref[...] = jnp.zeros_like(dv_ref)

        lane = _iota((SB_T, LANES), 1)
        qsts, dosts, tots = [], [], []
        for sl in SB_LANES:
            qsts.append(_sb_stack(q_ref[:, sl]).astype(MXU))
            dosts.append(_sb_stack(do_ref[:, sl]).astype(MXU))
            tv = t_ref[:, sl]
            tots.append(jnp.concatenate(
                [jnp.sum(jnp.where(lane == HEAD_DIM * hh, tv, 0.0), axis=1, keepdims=True) for hh in range(2)],
                axis=0))
        rowi, coli = _sb_rows(qb)
        u_upto = _tri(SB_T, lambda r, c: r <= c)
        u_before = _tri(SB_T, lambda r, c: r < c)
        kept = jnp.clip(n_ref[hp, qb], 1, qb + 1)

        def kstep(kb, carry, diag=False):
            ks = pl.multiple_of(kb * SB_T, SB_T)
            out = []
            for p, sl in enumerate(SB_LANES):
                psp, pg, dq = carry[p]
                kblk, z, mask, sp, spm = _sb_scores(qsts[p], k_ref.at[:, sl], ks, rowi, coli, diag)
                vblk = v_ref[pl.ds(ks, SB_T), sl].astype(MXU)
                after = tots[p] - (psp + _dot3_l(spm, u_upto, SB_PARTS))
                a = jnp.exp(z - sp - after)
                if diag:
                    a = jnp.where(mask, a, 0.0)
                gm = _dot_nt(dosts[p], vblk) * a
                before = pg + _dot3_l(gm, u_before, SB_PARTS)
                sg = jnp.exp(z - sp)
                dz = (gm * (1.0 - sg) - sg * before) * SB_SCALE
                if diag:
                    dz = jnp.where(mask, dz, 0.0)
                dzb = dz.astype(MXU)
                dk_ref[pl.ds(ks, SB_T), sl] += _dot_tn(dzb, qsts[p])
                dv_ref[pl.ds(ks, SB_T), sl] += _dot_tn(a.astype(MXU), dosts[p])
                out.append((psp + jnp.sum(spm, axis=1, keepdims=True),
                            pg + jnp.sum(gm, axis=1, keepdims=True), dq + _dot(dzb, kblk)))
            return tuple(out)

        zero1 = jnp.zeros((2 * SB_T, 1), F32)
        state = lax.fori_loop(qb + 1 - kept, qb, kstep,
                              ((zero1, zero1, jnp.zeros((2 * SB_T, LANES), F32)),) * SB_PAIRS)
        state = kstep(qb, state, diag=True)
        for p, sl in enumerate(SB_LANES):
            dq_ref[:, sl] = _sb_unstack(state[p][2]).astype(dq_ref.dtype)

    grid = (SB_WIDTH // SB_W, nq)
    body, s_in, s_args, s_shape, s_out, s_scr = host_exchange(body, 6, 3, grid, side)
    return pl.pallas_call(
        body,
        name="sb_bwd",
        out_shape=[
            jax.ShapeDtypeStruct((S, SB_WIDTH), MXU),
            jax.ShapeDtypeStruct((S, SB_WIDTH), F32),
            jax.ShapeDtypeStruct((S, SB_WIDTH), F32),
        ] + s_shape,
        grid=grid,
        in_specs=[
            pl.BlockSpec(memory_space=pltpu.SMEM),
            pl.BlockSpec((SB_T, SB_W), lambda h, i: (i, OFF_Q // SB_W + h)),
            pl.BlockSpec((S, SB_W), lambda h, i: (0, OFF_K // SB_W + h)),
            pl.BlockSpec((S, SB_W), lambda h, i: (0, OFF_V // SB_W + h)),
            pl.BlockSpec((SB_T, SB_W), lambda h, i: (i, h)),
            pl.BlockSpec((SB_T, SB_W), lambda h, i: (i, h)),
        ] + s_in,
        out_specs=[
            pl.BlockSpec((SB_T, SB_W), lambda h, i: (i, h)),
            pl.BlockSpec((S, SB_W), lambda h, i: (0, h)),
            pl.BlockSpec((S, SB_W), lambda h, i: (0, h)),
        ] + s_out,
        scratch_shapes=s_scr,
        compiler_params=_cparams(("arbitrary", "arbitrary")),
    )(nblk, projmain, projmain, projmain, do, t_exp, *s_args)


def local_step(x, pb, target, W, P, late, core):
    D = D_MODEL
    full = lambda a, w=D: (a, 0, w)

    (n1b,) = rowwise("norm_pre", lambda xv, g: (_rms(xv, g),), [full(x)], [full(P["norm_mix_pre"])],
                     [(D, D, MXU)])
    projmain, w_gate = mm("in_proj", n1b, W["main"], "nn", side=exchange_plan([("gather2", late["w_gate"])]))
    W = {**W, "gate": w_gate}
    gate_pre = mm("gate_proj", n1b, W["gate"], "nn", b_slabs=True)
    dtraw = mm("dt_proj", n1b, W["dt"], "nn")
    xbc_act = conv_fwd(projmain, P["conv_w"], P["conv_b"])
    dte, cse = ssd_prep(dtraw, P["dt_bias_pad"], P["a_exp"])
    y_scan, states, *got = ssd_fwd(xbc_act, dte, cse,
                                   side=exchange_plan([("gather2", late[n]) for n in LATE[:LATE_SPLIT]]))
    W = {**W, **late_matrices(dict(zip(LATE[:LATE_SPLIT], got)))}

    def f_gate(ysc, xs, z, dsk, nw):
        return _rms((ysc + xs * dsk) * (z * _sig(z)), nw)

    (y_ssd_b,) = rowwise("ssd_gate", lambda *a: (f_gate(*a),),
                         [(y_scan, 0, GW), (xbc_act, 0, GW), (projmain, OFF_Z, GW)],
                         [(P["dsk_exp"], 0, GW), (P["ssd_norm"], 0, GW)], [(D_INNER, GW, MXU)], tr=256,
                         groups=SSD_GROUPS)
    y_sb_b, t_exp, sb_kept, *got = sb_fwd(
        projmain, side=exchange_plan([("gather2", late[n]) for n in LATE[LATE_SPLIT:]]))
    W = {**W, **late_matrices(dict(zip(LATE[LATE_SPLIT:], got)))}
    u1 = mm("ssd_branch", y_ssd_b, W["ssd"], "nn")
    u2 = mm("sb_branch", y_sb_b, W["sb"], "nn")

    def f_merge(a1, a2, g1, g2, b1, b2):
        return (_sig(g1 + b1) * a1 + _sig(g2 + b2) * a2,)

    gate_rows = [(gate_pre, 0, D), (gate_pre, D, D)]
    gate_bias = [(P["b_gate"], 0, D), (P["b_gate"], D, D)]
    (merged_b,) = rowwise("merge", f_merge, [full(u1), full(u2)] + gate_rows, gate_bias, [(D, D, MXU)])
    mo = mm("out_proj", merged_b, W["out"], "nn")

    def f_mid(xv, m, gpost, gffn):
        h1 = xv + _rms(m, gpost)
        return h1, _rms(h1, gffn)

    h1, n2b = rowwise("mix_post", f_mid, [full(x), full(mo)],
                      [full(P["norm_mix_post"]), full(P["norm_ffn_pre"])], [(D, D, F32), (D, D, MXU)])
    a_ff, rb = mm("ff1", n2b, W["ff1"], "nn", b_slabs=True,
                  extra=[(MXU, lambda acc: jnp.square(jnp.maximum(acc, 0.0)))])
    ff = mm("ff2", rb, W["ff2"], "nn")

    def f_ffn_post(h, f, g):
        h2 = h + _rms(f, g)
        return h2, h2

    h2, h2b = rowwise("ffn_post", f_ffn_post, [full(h1), full(ff)], [full(P["norm_ffn_post"])],
                      [(D, D, F32), (D, D, MXU)])
    pgp = mm("ple_gate", h2b, W["pg"], "nn")
    pe = mm("ple_proj", pb, W["ple"], "nn", b_slabs=True)

    def f_ple(h2v, gp, pev, tgt, g):
        f = lambda h, a, b, gg: h + _rms(_sig(a) * b, gg)
        h3, vjp = jax.vjp(f, h2v, gp, pev, g)
        err = h3 - tgt
        dh2, dgp, dpe, dg = vjp(err * (1.0 / D))
        lossc = (0.5 / D) * jnp.sum(err * err, axis=0, keepdims=True)
        return dh2, dgp, dpe, dg, lossc

    dh2a, dpgp_b, dpe_b, g_ple, lossc = rowwise(
        "ple_loss", f_ple, [full(h2), full(pgp), full(pe), full(target)], [full(P["norm_ple_post"])],
        [(D, D, F32), (D, D, MXU), (D, D, MXU)], [(D, D), (D, D)])
    dh2 = mm("d_ple_gate_x", dpgp_b, W["pg"], "nt", add=dh2a)
    gW = {}
    gW["w_ple_gate"] = mm("d_ple_gate_w", h2b, dpgp_b, "tn")
    gW["w_ple"] = mm("d_ple_w", pb, dpe_b, "tn", out_slabs=True)

    def b_ffn_post(d, f, g):
        _, vjp = jax.vjp(_rms, f, g)
        return vjp(d)

    dff_b, g_ffn_post = rowwise("d_ffn_post", b_ffn_post, [full(dh2), full(ff)], [full(P["norm_ffn_post"])],
                                [(D, D, MXU)], [(D, D)])
    da_b = mm("d_ff2_x", dff_b, W["ff2"], "nt", out_dtype=MXU,
              epi=lambda acc, act: acc * (2.0 * jnp.maximum(act, 0.0)), epi_args=[a_ff])
    gW["w_ff2"] = mm("d_ff2_w", rb, dff_b, "tn")
    dn2 = mm("d_ff1_x", da_b, W["ff1"], "nt", b_slabs=True)
    gW["w_ff1"] = mm("d_ff1_w", n2b, da_b, "tn", out_slabs=True)

    def b_mid(d2, dn, h, m, gpost, gffn):
        _, vjp = jax.vjp(_rms, h, gffn)
        dh, dgffn = vjp(dn)
        dh1 = d2 + dh
        _, vjp2 = jax.vjp(_rms, m, gpost)
        dm, dgpost = vjp2(dh1)
        return dh1, dm, dgpost, dgffn

    dh1, dmo_b, g_mix_post, g_ffn_pre = rowwise(
        "d_mix_post", b_mid, [full(dh2), full(dn2), full(h1), full(mo)],
        [full(P["norm_mix_post"]), full(P["norm_ffn_pre"])], [(D, D, F32), (D, D, MXU)], [(D, D), (D, D)])
    dmerged = mm("d_out_x", dmo_b, W["out"], "nt")
    gW["w_out"] = mm("d_out_w", merged_b, dmo_b, "tn")

    def b_merge(d, a1, a2, g1, g2, b1, b2):
        s1, s2 = _sig(g1 + b1), _sig(g2 + b2)
        dg1 = d * a1 * s1 * (1.0 - s1)
        dg2 = d * a2 * s2 * (1.0 - s2)
        dg = jnp.concatenate([dg1, dg2], axis=1)
        return d * s1, d * s2, dg, jnp.sum(dg, axis=0, keepdims=True)

    du1_b, du2_b, dgp_b, g_b_gate = rowwise(
        "d_merge", b_merge, [full(dmerged), full(u1), full(u2)] + gate_rows, gate_bias,
        [(D, D, MXU), (D, D, MXU), (2 * D, 2 * D, MXU)], [(2 * D, 2 * D)])
    dy_ssd = mm("d_ssd_branch_x", du1_b, W["ssd"], "nt")
    dy_sb = mm("d_sb_branch_x", du2_b, W["sb"], "nt")
    gW["w_ssd_branch"] = mm("d_ssd_branch_w", y_ssd_b, du1_b, "tn")
    gW["w_sb_branch"] = mm("d_sb_branch_w", y_sb_b, du2_b, "tn")

    def b_gate(d, ysc, xs, z, dsk, nw):
        _, vjp = jax.vjp(f_gate, ysc, xs, z, dsk, nw)
        return vjp(d)

    slabs = {n: gW[n] if n in COL_SHARDED else gW[n].reshape((N_DEV,) + ROW_SHARDED[n]) for n in LATE}
    dy_scan, dxs_skip, dz_b, g_dsk_exp, g_ssd_norm, *from_sibling = rowwise(
        "d_ssd_gate", b_gate, [(dy_ssd, 0, GW), (y_scan, 0, GW), (xbc_act, 0, GW), (projmain, OFF_Z, GW)],
        [(P["dsk_exp"], 0, GW), (P["ssd_norm"], 0, GW)],
        [(D_INNER, GW, F32), (D_INNER, GW, F32), (D_INNER, GW, MXU)], [(D_INNER, GW), (D_INNER, GW)],
        tr=256, groups=SSD_GROUPS, side=exchange_plan([("pair", slabs[n]) for n in LATE]))
    chip_sums = [pair_sum("pair_sum_" + n, slabs[n], s, core) for n, s in zip(LATE, from_sibling)]
    dxs, d_b, d_c, ddt_part, g_a_exp, *late_parts = ssd_bwd(
        dy_scan, xbc_act, dte, cse, states, P["a_exp"], side=exchange_plan([("chips", s) for s in chip_sums]))

    def b_dt(dpart, dtr, bias):
        ddt = _dot3_l(dpart, _head_reduce())
        d = ddt * _sig(dtr + bias)
        return d, jnp.sum(d, axis=0, keepdims=True)

    ddt_b, g_dt_bias_pad = rowwise("d_dt", b_dt, [(ddt_part, 0, D_INNER), (dtraw, 0, LANES)],
                                   [(P["dt_bias_pad"], 0, LANES)], [(LANES, LANES, MXU)], [(LANES, LANES)])
    dxbc_b, g_conv_w, g_conv_b = conv_bwd(dxs, dxs_skip, d_b, d_c, projmain, P["conv_w"], P["conv_b"])
    dq_b, dk, dv = sb_bwd(projmain, dy_sb, t_exp, sb_kept)
    dmain_b = jnp.concatenate([dz_b, dxbc_b, dq_b, dk.astype(MXU), dv.astype(MXU)], axis=1)
    g_main = mm("d_in_w", n1b, dmain_b, "tn")
    g_dt = mm("d_dt_w", n1b, ddt_b, "tn")
    last = {"w_gate": mm("d_gate_w", n1b, dgp_b, "tn", out_slabs=True)}
    last["w_in"], conv_slabs = in_grad_slabs({"main": g_main, "dt": g_dt, "conv_w": g_conv_w})
    dn1_dt = mm("d_dt_x", ddt_b, W["dt"], "nt")
    dn1_gate, *from_sibling = mm("d_gate_x", dgp_b, W["gate"], "nt", add=dn1_dt, b_slabs=True,
                                 side=exchange_plan([("pair", last[n]) for n in FIRST]))
    chip_sums = [pair_sum("pair_sum_" + n, last[n], s, core) for n, s in zip(FIRST, from_sibling)]
    dn1, w_in_parts = mm("d_in_x", dmain_b, W["main"], "nt", add=dn1_gate,
                         side=exchange_plan([("chips", chip_sums[0])]))

    def b_pre(d1, dn, xv, g):
        _, vjp = jax.vjp(_rms, xv, g)
        dx, dg = vjp(dn)
        return d1 + dx, dg

    grad_x, g_mix_pre, w_gate_parts = rowwise(
        "d_norm_pre", b_pre, [full(dh1), full(dn1), full(x)], [full(P["norm_mix_pre"])],
        [(D, D, F32)], [(D, D)], side=exchange_plan([("chips", chip_sums[1])]))

    parts = {**dict(zip(LATE, late_parts)), "w_in": w_in_parts, "w_gate": w_gate_parts}
    gS = {
        "norm_mix_pre": g_mix_pre, "conv_b": g_conv_b, "dt_bias_pad": g_dt_bias_pad, "a_exp": g_a_exp,
        "dsk_exp": g_dsk_exp, "ssd_norm": g_ssd_norm, "b_gate": g_b_gate,
        "norm_mix_post": g_mix_post, "norm_ffn_pre": g_ffn_pre, "norm_ffn_post": g_ffn_post,
        "norm_ple_post": g_ple,
    }
    return lossc, grad_x, parts, conv_slabs, gS


IN_SPLITS = (2048, 6144, 6176, 7200, 8224)
IN_SHARD = 1156
IN_PAD = 1280
COL_SHARDED = {"w_in": (1024, IN_SHARD), "conv_w": (4, 512), "w_gate": (1024, 256), "w_ff1": (1024, 512),
               "w_ple": (256, 128)}
ROW_SHARDED = {"w_ssd_branch": (256, 1024), "w_sb_branch": (128, 1024), "w_out": (128, 1024),
               "w_ff2": (512, 1024), "w_ple_gate": (128, 1024)}
SHARDED = tuple(COL_SHARDED) + tuple(ROW_SHARDED)
SMALL = (
    ("norm_mix_pre", 1024), ("conv_b", 4096), ("dt_bias", 32), ("a_log", 32), ("d_skip", 32), ("ssd_norm", 2048),
    ("b_gate", 2048), ("norm_mix_post", 1024), ("norm_ffn_pre", 1024), ("norm_ffn_post", 1024),
    ("norm_ple_post", 1024),
)
ROW = 1024
SMALL_ROWS = 16


def _rows_of(n):
    return -(-n // ROW)


def _pad_to(v, n, axis=-1):
    pad = [(0, 0)] * v.ndim
    pad[axis] = (0, n - v.shape[axis])
    return jnp.pad(v, pad)


FIRST = ("w_in", "w_gate")
LATE = ("w_ssd_branch", "w_sb_branch", "w_out", "w_ff1", "w_ff2", "w_ple", "w_ple_gate")
LATE_SPLIT = 3


def in_matrices(w_in_slabs):
    w_in = jnp.concatenate([w_in_slabs[k, :, :IN_SHARD] for k in range(N_DEV)], axis=1)
    return {
        "main": jnp.concatenate([w_in[:, :IN_SPLITS[1]], w_in[:, IN_SPLITS[2]:]], axis=1),
        "dt": _pad_to(w_in[:, IN_SPLITS[1]:IN_SPLITS[2]], LANES),
    }


def late_matrices(g):
    short = {"w_ff1": "ff1", "w_ple": "ple", "w_ssd_branch": "ssd", "w_sb_branch": "sb", "w_out": "out",
             "w_ff2": "ff2", "w_ple_gate": "pg"}
    stack = lambda a: a.reshape(a.shape[0] * a.shape[1], a.shape[2])
    return {short[n]: a if n in COL_SHARDED else stack(a) for n, a in g.items()}


def build_small(small, conv_w_slabs):
    P = {k: small[k] for k in ("norm_mix_pre", "conv_b", "ssd_norm", "b_gate", "norm_mix_post", "norm_ffn_pre",
                               "norm_ffn_post", "norm_ple_post")}
    P["conv_w"] = jnp.concatenate([conv_w_slabs[k] for k in range(N_DEV)], axis=1)
    P["dt_bias_pad"] = _pad_to(small["dt_bias"], LANES)
    P["a_exp"] = jnp.repeat(-jnp.exp(small["a_log"]), HEAD_DIM, axis=1)
    P["dsk_exp"] = jnp.repeat(small["d_skip"], HEAD_DIM, axis=1)
    return P


def small_grads(gS, small):
    heads = lambda a: a.reshape(SSD_HEADS, HEAD_DIM).sum(axis=1)[None, :]
    out = {k: gS[k] for k in ("norm_mix_pre", "conv_b", "ssd_norm", "b_gate", "norm_mix_post", "norm_ffn_pre",
                              "norm_ffn_post", "norm_ple_post")}
    out["dt_bias"] = gS["dt_bias_pad"][:, :SSD_HEADS]
    out["a_log"] = heads(gS["a_exp"]) * (-jnp.exp(small["a_log"]))
    out["d_skip"] = heads(gS["dsk_exp"])
    return out


def in_grad_slabs(g):
    pieces = ((0, IN_SPLITS[1], g["main"], 0), (IN_SPLITS[1], IN_SPLITS[2], g["dt"], -IN_SPLITS[1]),
              (IN_SPLITS[2], N_DEV * IN_SHARD, g["main"], IN_SPLITS[1] - IN_SPLITS[2]))
    slabs = []
    for k in range(N_DEV):
        a, b = IN_SHARD * k, IN_SHARD * (k + 1)
        cut = [src[:, max(a, lo) + off:min(b, hi) + off] for lo, hi, src, off in pieces if max(a, lo) < min(b, hi)]
        slabs.append(_pad_to(jnp.concatenate(cut, axis=1), IN_PAD))
    width = COL_SHARDED["conv_w"][1]
    return jnp.stack(slabs), jnp.stack([g["conv_w"][:, width * k:width * (k + 1)] for k in range(N_DEV)])


def pack_small(get):
    cols = [_pad_to(get(name).reshape(n), _rows_of(n) * ROW) for name, n in SMALL]
    return jnp.concatenate(cols).reshape(SMALL_ROWS, ROW)


def unpack_small(flat):
    out, r0 = {}, 0
    for name, n in SMALL:
        rows = _rows_of(n)
        out[name] = flat[r0:r0 + rows].reshape(rows * ROW)[:n].reshape(1, n)
        r0 += rows
    return out


N_CHIPS = 4
JOB_SEMS = {"all": 7, "scatter": 7, "gather2": 7, "pair": 4, "chips": 3}


def exchange_plan(jobs):
    n = len(jobs)
    kinds = [k for k, _ in jobs]
    srcs = [s for _, s in jobs]
    shapes = {"all": lambda s: (N_DEV,) + s.shape, "gather2": lambda s: (N_DEV,) + s.shape,
              "scatter": lambda s: s.shape, "pair": lambda s: (N_CHIPS,) + s.shape[1:], "chips": lambda s: s.shape}
    out_shape = [jax.ShapeDtypeStruct(shapes[k](s), s.dtype) for k, s in jobs]
    offs = [sum(JOB_SEMS[k] for k in kinds[:i]) for i in range(n + 1)]

    def run(phases, src, out, send_sems, recv_sems, local_sems):
        x, y, c = lax.axis_index("x"), lax.axis_index("y"), lax.axis_index("c")
        dev = lambda d: 4 * d[0] + 2 * d[1] + d[2]
        chip_no = lambda ch: 2 * ch[0] + ch[1]
        me, sib, my_chip = (x, y, c), (x, y, 1 - c), (x, y)
        others = [(1 - x, y), (x, 1 - y), (1 - x, 1 - y)]
        all_chips = [(0, 0), (0, 1), (1, 0), (1, 1)]
        peers = [(1 - x if k & 4 else x, 1 - y if k & 2 else y, 1 - c if k & 1 else c) for k in range(1, N_DEV)]
        starts, recvs, local = [], [], []
        chained = [[] for _ in others]

        for i, kind in enumerate(kinds):
            s_ref, o_ref = src[i], out[i]

            def rc(k, src_ref, dst_ref, to, i=i):
                s = offs[i] + k
                return pltpu.make_async_remote_copy(src_ref=src_ref, dst_ref=dst_ref, send_sem=send_sems.at[s],
                                                    recv_sem=recv_sems.at[s], device_id=to,
                                                    device_id_type=pl.DeviceIdType.MESH)

            if kind == "all":
                local.append(pltpu.make_async_copy(s_ref, o_ref.at[dev(me)], local_sems.at[i]))
                for k, peer in enumerate(peers):
                    starts.append(rc(k, s_ref, o_ref.at[dev(me)], peer))
                    recvs.append(rc(k, s_ref, o_ref.at[dev(peer)], peer))
            elif kind == "scatter":
                local.append(pltpu.make_async_copy(s_ref.at[dev(me)], o_ref.at[dev(me)], local_sems.at[i]))
                for k, peer in enumerate(peers):
                    starts.append(rc(k, s_ref.at[dev(peer)], o_ref.at[dev(me)], peer))
                    recvs.append(rc(k, s_ref.at[dev(me)], o_ref.at[dev(peer)], peer))
            elif kind == "gather2":
                local.append(pltpu.make_async_copy(s_ref, o_ref.at[dev(me)], local_sems.at[i]))
                starts.append(rc(0, s_ref, o_ref.at[dev(me)], sib))
                recvs.append(rc(0, s_ref, o_ref.at[dev(sib)], sib))
                for j, ch in enumerate(others):
                    same, other = (*ch, c), (*ch, 1 - c)
                    starts.append(rc(1 + j, s_ref, o_ref.at[dev(me)], same))
                    chained[j].append((rc(1 + j, s_ref, o_ref.at[dev(same)], same),
                                       rc(4 + j, o_ref.at[dev(same)], o_ref.at[dev(same)], sib)))
                    recvs.append(rc(4 + j, s_ref, o_ref.at[dev(other)], sib))
            elif kind == "pair":
                for j, ch in enumerate(all_chips):
                    starts.append(rc(j, s_ref.at[dev((*ch, 1 - c))], o_ref.at[j], sib))
                    recvs.append(rc(j, s_ref.at[dev((*ch, c))], o_ref.at[j], sib))
            else:
                mine = chip_no(my_chip)
                local.append(pltpu.make_async_copy(s_ref.at[mine], o_ref.at[mine], local_sems.at[i]))
                for j, ch in enumerate(others):
                    starts.append(rc(j, s_ref.at[chip_no(ch)], o_ref.at[mine], (*ch, c)))
                    recvs.append(rc(j, s_ref.at[mine], o_ref.at[chip_no(ch)], (*ch, c)))

        if "start" in phases:
            for cp in local + starts:
                cp.start()
        if "pass" in phases:
            for group in chained:
                for arrival, forward in group:
                    arrival.wait_recv()
                    forward.start()
        if "finish" in phases:
            for cp in recvs:
                cp.wait_recv()
            for cp in starts + [forward for group in chained for _, forward in group]:
                cp.wait_send()
            for cp in local:
                cp.wait()

    scratch = [pltpu.SemaphoreType.DMA((offs[n],)), pltpu.SemaphoreType.DMA((offs[n],)),
               pltpu.SemaphoreType.DMA((n,))]
    return srcs, out_shape, scratch, run


EXCHANGE_PHASES = ("start", "pass", "finish")


def exchange(name, jobs):
    srcs, out_shape, scratch, run = exchange_plan(jobs)
    n = len(srcs)

    def body(*refs):
        run(EXCHANGE_PHASES, refs[:n], refs[n:2 * n], *refs[2 * n:])

    any_spec = pl.BlockSpec(memory_space=pl.ANY)
    return pl.pallas_call(body, name=name, out_shape=out_shape, in_specs=[any_spec] * n, out_specs=[any_spec] * n,
                          scratch_shapes=scratch)(*srcs)


def host_exchange(body, n_in, n_out, grid, plan):
    if plan is None:
        return body, [], [], [], [], []
    srcs, out_shape, scratch, run = plan
    n = len(srcs)
    steps = 1
    for g in grid:
        steps *= g

    def hosted(*refs):
        ins, side_in = refs[:n_in], refs[n_in:n_in + n]
        outs, side_out = refs[n_in + n:n_in + n + n_out], refs[n_in + n + n_out:n_in + 2 * n + n_out]
        rest = refs[n_in + 2 * n + n_out:]
        own, sems = rest[:len(rest) - 3], rest[len(rest) - 3:]
        step = 0
        for d, g in enumerate(grid):
            step = step * g + pl.program_id(d)

        def at(when, phase):
            @pl.when(step == when)
            def _():
                run((phase,), side_in, side_out, *sems)

        at(0, "start")
        body(*ins, *outs, *own)
        at(steps * 13 // 16, "pass")
        at(steps - 1, "finish")

    any_spec = pl.BlockSpec(memory_space=pl.ANY)
    return hosted, [any_spec] * n, list(srcs), list(out_shape), [any_spec] * n, list(scratch)


def pair_sum(name, g, sib, core):
    _, r, c = g.shape
    tr = _pick(r, (256, 128, 64, 32, 16, 8))

    def body(core_ref, g_ref, s_ref, o_ref):
        o_ref[...] = (g_ref[...] + s_ref[...]).astype(o_ref.dtype)

    return pl.pallas_call(
        body,
        name=name,
        out_shape=jax.ShapeDtypeStruct((N_CHIPS, r, c), MXU),
        grid_spec=pltpu.PrefetchScalarGridSpec(
            num_scalar_prefetch=1,
            grid=(N_CHIPS, r // tr),
            in_specs=[
                pl.BlockSpec((None, None, tr, c), lambda j, i, core_ref: (j, core_ref[0], i, 0)),
                pl.BlockSpec((None, tr, c), lambda j, i, core_ref: (j, i, 0)),
            ],
            out_specs=pl.BlockSpec((None, tr, c), lambda j, i, core_ref: (j, i, 0)),
        ),
        compiler_params=_cparams(("parallel", "parallel")),
    )(core, g.reshape(N_CHIPS, 2, r, c), sib)


def adamw(name, parts, w, m, v):
    rows, cols = w.shape
    tr = _pick(rows, (128, 64, 32, 16, 8))

    nparts = parts.shape[0]

    def body(p_ref, w_ref, m_ref, v_ref, g_ref, d_ref, m2_ref, v2_ref):
        g = p_ref[0].astype(F32)
        for k in range(1, nparts):
            g = g + p_ref[k].astype(F32)
        m2 = ADAM_B1 * m_ref[...] + (1.0 - ADAM_B1) * g
        v2 = ADAM_B2 * v_ref[...] + (1.0 - ADAM_B2) * jnp.square(g)
        m_hat = m2 / (1.0 - ADAM_B1 ** ADAM_STEP)
        v_hat = v2 / (1.0 - ADAM_B2 ** ADAM_STEP)
        g_ref[...] = g
        d_ref[...] = -ADAM_LR * (m_hat / (jnp.sqrt(v_hat) + ADAM_EPS) + ADAM_WD * w_ref[...])
        m2_ref[...] = m2
        v2_ref[...] = v2

    spec = pl.BlockSpec((tr, cols), lambda i: (i, 0))
    return pl.pallas_call(
        body,
        name=name,
        out_shape=[jax.ShapeDtypeStruct((rows, cols), F32)] * 4,
        grid=(rows // tr,),
        in_specs=[pl.BlockSpec((nparts, tr, cols), lambda i: (0, i, 0)), spec, spec, spec],
        out_specs=[spec] * 4,
        compiler_params=_cparams(("parallel",)),
    )(parts, w, m, v)


WEIGHT_ORDER = (
    "norm_mix_pre", "w_in", "conv_w", "conv_b", "dt_bias", "a_log", "d_skip", "ssd_norm", "w_ssd_branch",
    "w_sb_branch", "w_gate", "b_gate", "w_out", "norm_mix_post", "norm_ffn_pre", "w_ff1", "w_ff2", "norm_ffn_post",
    "w_ple", "w_ple_gate", "norm_ple_post",
)


def kernel(x, p, norm_mix_pre, w_in, conv_w, conv_b, dt_bias, a_log, d_skip, ssd_norm, w_ssd_branch, w_sb_branch, w_gate, b_gate, w_out, norm_mix_post, norm_ffn_pre, w_ff1, w_ff2, norm_ffn_post, w_ple, w_ple_gate, norm_ple_post, loss_target, m_norm_mix_pre, m_w_in, m_conv_w, m_conv_b, m_dt_bias, m_a_log, m_d_skip, m_ssd_norm, m_w_ssd_branch, m_w_sb_branch, m_w_gate, m_b_gate, m_w_out, m_norm_mix_post, m_norm_ffn_pre, m_w_ff1, m_w_ff2, m_norm_ffn_post, m_w_ple, m_w_ple_gate, m_norm_ple_post, v_norm_mix_pre, v_w_in, v_conv_w, v_conv_b, v_dt_bias, v_a_log, v_d_skip, v_ssd_norm, v_w_ssd_branch, v_w_sb_branch, v_w_gate, v_b_gate, v_w_out, v_norm_mix_post, v_norm_ffn_pre, v_w_ff1, v_w_ff2, v_norm_ffn_post, v_w_ple, v_w_ple_gate, v_norm_ple_post):
    a = dict(locals())
    seq = x.shape[1]
    x2 = x.reshape(seq, D_MODEL)
    target = loss_target.reshape(seq, D_MODEL)
    pb = p.reshape(seq, PLE_DIM).astype(MXU)

    def shard(prefix, name, dtype):
        v = a[prefix + name][0].astype(dtype)
        return _pad_to(v, IN_PAD) if name == "w_in" else v

    w_in_slabs, conv_w_slabs = exchange(
        "gather_weights", [("gather2", shard("", "w_in", MXU)), ("all", a["conv_w"][0])])
    small = {n: a[n] for n, _ in SMALL}
    core = lax.axis_index("c").astype(jnp.int32).reshape(1)

    lossc, grad_x, parts, conv_slabs, g_acc = local_step(
        x2, pb, target, in_matrices(w_in_slabs), build_small(small, conv_w_slabs),
        {n: shard("", n, MXU) for n in LATE + ("w_gate",)}, core)
    loss = lax.psum(jnp.sum(lossc), ("x", "y", "c"))
    g_small = small_grads(g_acc, small)

    parts["conv_w"], small_parts = exchange(
        "small_grads", [("scatter", conv_slabs), ("all", pack_small(lambda n: g_small[n]))])

    leaves = {}
    for n, part in parts.items():
        res = adamw("adamw_" + n, part, shard("", n, F32), shard("m_", n, F32), shard("v_", n, F32))
        leaves[n] = [r[None, :, :IN_SHARD] if n == "w_in" else r[None] for r in res]
    res = adamw("adamw_small", small_parts, pack_small(lambda n: a[n]), pack_small(lambda n: a["m_" + n]),
                pack_small(lambda n: a["v_" + n]))
    for j, r in enumerate(res):
        for n, leaf in unpack_small(r).items():
            leaves.setdefault(n, [None] * 4)[j] = leaf
    outs = [loss, grad_x.reshape(x.shape)]
    for j in range(4):
        outs += [leaves[n][j] for n in WEIGHT_ORDER]
    return tuple(outs)
```

```python
import functools

import jax
import jax.numpy as jnp
from jax import lax
from jax.experimental import pallas as pl
from jax.experimental.pallas import tpu as pltpu

F32 = jnp.float32
MXU = jnp.bfloat16
VMEM_LIMIT = 56 * 1024 * 1024

D_MODEL = 1024
D_INNER = 2048
SSD_HEADS = 32
HEAD_DIM = 64
SSD_GROUPS = 8
D_STATE = 128
CONV_K = 4
CONV_DIM = 4096
CHUNK = 128
SB_WIDTH = 1024
D_FF = 4096
PLE_DIM = 256
RMS_EPS = 1e-6
SB_SCALE = HEAD_DIM ** -0.5
N_DEV = 8
LANES = 128

OFF_Z, OFF_XBC, OFF_QKV = 0, 2048, 6144
OFF_Q, OFF_K, OFF_V = 0, 1024, 2048

ADAM_LR = 0.001
ADAM_B1 = 0.9
ADAM_B2 = 0.999
ADAM_EPS = 1e-08
ADAM_WD = 0.01
ADAM_STEP = 10


def _sig(x):
    return 0.5 * jnp.tanh(0.5 * x) + 0.5


def _softplus(x):
    return jnp.maximum(x, 0.0) + jnp.log(1.0 + jnp.exp(-jnp.abs(x)))


def _rms(x, w):
    return x * lax.rsqrt(jnp.mean(x * x, axis=-1, keepdims=True) + RMS_EPS) * w


def _dot(a, b):
    return jnp.dot(a, b, preferred_element_type=F32)


def _dot_nt(a, b):
    return lax.dot_general(a, b, (((1,), (1,)), ((), ())), preferred_element_type=F32)


def _dot_tn(a, b):
    return lax.dot_general(a, b, (((0,), (0,)), ((), ())), preferred_element_type=F32)


def _split3(x):
    x1 = x.astype(MXU)
    r = x - x1.astype(F32)
    x2 = r.astype(MXU)
    r = r - x2.astype(F32)
    return x1, x2, r.astype(MXU)


def _dot3_l(a, u, parts=3):
    m = a.shape[0]
    d = _dot(jnp.concatenate(_split3(a)[:parts], axis=0), u)
    if parts == 2:
        return d[m:] + d[:m]
    return (d[2 * m:] + d[m:2 * m]) + d[:m]


def _dot3_r(u, a):
    n = a.shape[1]
    d = _dot(u, jnp.concatenate(_split3(a), axis=1))
    return (d[:, 2 * n:] + d[:, n:2 * n]) + d[:, :n]


def _iota(shape, dim):
    return lax.broadcasted_iota(jnp.int32, shape, dim)


def _tri(n, cmp):
    r, c = _iota((n, n), 0), _iota((n, n), 1)
    return cmp(r, c).astype(F32).astype(MXU)


def _cparams(sem):
    return pltpu.CompilerParams(dimension_semantics=sem, vmem_limit_bytes=VMEM_LIMIT)


def _pick(n, cands):
    for c in cands:
        if n % c == 0:
            return c
    return n


def mm(name, a, b, mode, add=None, out_dtype=F32, b_slabs=False, out_slabs=False, epi=None, epi_args=(), extra=(),
       side=None):
    slab = None
    if b_slabs:
        slab = b.shape[2]
        bshape = (b.shape[1], N_DEV * slab)
    else:
        bshape = b.shape
    if mode == "nn":
        (M, K), (K2, N) = a.shape, bshape
    elif mode == "nt":
        (M, K), (N, K2) = a.shape, bshape
    else:
        (K, M), (K2, N) = a.shape, bshape
    assert K == K2, (name, a.shape, b.shape)
    tm = _pick(M, (1024, 512, 256, 128))
    tn = _pick(N, (1024, 512, 256, 128))
    tk = _pick(K, (1024, 512, 256, 128))
    if b_slabs and mode == "nn":
        tn = slab
    if b_slabs and mode == "nt":
        tk = slab
    if out_slabs:
        assert mode == "tn" and N % N_DEV == 0
        tn = N // N_DEV
    nk = K // tk

    def body(*refs):
        refs = list(refs)
        a_ref, b_ref = refs[:2]
        add_ref = refs[2] if add is not None else None
        n_in = 2 + (add is not None)
        epi_refs = refs[n_in:n_in + len(epi_args)]
        o_ref = refs[n_in + len(epi_args)]
        extra_refs = refs[n_in + len(epi_args) + 1:-1]
        acc = refs[-1]
        k = pl.program_id(2)

        @pl.when(k == 0)
        def _():
            acc[...] = jnp.zeros_like(acc) if add is None else add_ref[...]

        av, bv = a_ref[...], b_ref[...]
        if mode == "nn":
            acc[...] += _dot(av, bv)
        elif mode == "nt":
            acc[...] += _dot_nt(av, bv)
        else:
            acc[...] += _dot_tn(av, bv)

        @pl.when(k == nk - 1)
        def _():
            res = acc[...]
            main = res if epi is None else epi(res, *[r[...] for r in epi_refs])
            o_ref[...] = main.astype(o_ref.dtype)
            for r, (_, fn) in zip(extra_refs, extra):
                r[...] = fn(res).astype(r.dtype)

    if mode == "nn":
        a_spec = pl.BlockSpec((tm, tk), lambda i, j, k: (i, k))
        b_spec = pl.BlockSpec((tk, tn), lambda i, j, k: (k, j))
    elif mode == "nt":
        a_spec = pl.BlockSpec((tm, tk), lambda i, j, k: (i, k))
        b_spec = pl.BlockSpec((tn, tk), lambda i, j, k: (j, k))
    else:
        a_spec = pl.BlockSpec((tk, tm), lambda i, j, k: (k, i))
        b_spec = pl.BlockSpec((tk, tn), lambda i, j, k: (k, j))
    if b_slabs and mode == "nn":
        b_spec = pl.BlockSpec((None, tk, tn), lambda i, j, k: (j, k, 0))
    if b_slabs and mode == "nt":
        b_spec = pl.BlockSpec((None, tn, tk), lambda i, j, k: (k, j, 0))
    o_spec = pl.BlockSpec((tm, tn), lambda i, j, k: (i, j))
    in_specs, args = [a_spec, b_spec], [a, b]
    if add is not None:
        in_specs.append(o_spec)
        args.append(add)
    for e in epi_args:
        in_specs.append(o_spec)
        args.append(e)
    out_sds = jax.ShapeDtypeStruct((M, N), out_dtype)
    if out_slabs:
        o_spec = pl.BlockSpec((None, tm, tn), lambda i, j, k: (j, i, 0))
        out_sds = jax.ShapeDtypeStruct((N_DEV, M, tn), out_dtype)
    grid = (M // tm, N // tn, nk)
    body, s_in, s_args, s_shape, s_out, s_scr = host_exchange(body, len(args), 1 + len(extra), grid, side)
    res = pl.pallas_call(
        body,
        name=name,
        out_shape=[out_sds] + [jax.ShapeDtypeStruct((M, N), dt) for dt, _ in extra] + s_shape,
        grid=grid,
        in_specs=in_specs + s_in,
        out_specs=[o_spec] * (1 + len(extra)) + s_out,
        scratch_shapes=[pltpu.VMEM((tm, tn), F32)] + s_scr,
        compiler_params=_cparams(("parallel", "parallel", "arbitrary") if side is None else ("arbitrary",) * 3),
    )(*args, *s_args)
    return res if extra or side is not None else res[0]


def rowwise(name, fn, rows, bcast, outs, accs=(), tr=512, ncb=1, side=None, groups=1):
    S = rows[0][0].shape[0]
    tr = min(tr, S)
    nrb = S // tr
    G = groups
    in_specs, args, in_w = [], [], []
    for arr, off, w in rows:
        assert off % (w * G) == 0 and arr.shape[0] == S
        in_specs.append(pl.BlockSpec((tr, w * G), lambda j, i, ob=off // (w * G): (i, ob + j)))
        args.append(arr)
        in_w.append(w)
    for arr, off, w in bcast:
        assert off % (w * G) == 0
        in_specs.append(pl.BlockSpec((arr.shape[0], w * G), lambda j, i, ob=off // (w * G): (0, ob + j)))
        args.append(arr)
        in_w.append(w)
    out_shape, out_specs, out_w = [], [], []
    for tw, w, dt in outs:
        out_shape.append(jax.ShapeDtypeStruct((S, tw), dt))
        out_specs.append(pl.BlockSpec((tr, w * G), lambda j, i: (i, j)))
        out_w.append(w)
    for tw, w in accs:
        out_shape.append(jax.ShapeDtypeStruct((1, tw), F32))
        out_specs.append(pl.BlockSpec((1, w * G), lambda j, i: (0, j)))
        out_w.append(w)
    nin, nout = len(args), len(outs)

    def body(*refs):
        i = pl.program_id(1)
        for g in range(G):
            cut = lambda w: slice(g * w, (g + 1) * w)
            res = fn(*[r[:, cut(w)] for r, w in zip(refs[:nin], in_w)])
            for k, (r, v) in enumerate(zip(refs[nin:], res)):
                cols = cut(out_w[k])
                if k < nout:
                    r[:, cols] = v.astype(r.dtype)
                    continue

                @pl.when(i == 0)
                def _(r=r, v=v, cols=cols):
                    r[:, cols] = v

                @pl.when(i > 0)
                def _(r=r, v=v, cols=cols):
                    r[:, cols] += v

    body, s_in, s_args, s_shape, s_out, s_scr = host_exchange(body, nin, len(out_shape), (ncb, nrb), side)
    res = pl.pallas_call(
        body,
        name=name,
        out_shape=out_shape + s_shape,
        grid=(ncb, nrb),
        in_specs=in_specs + s_in,
        out_specs=out_specs + s_out,
        scratch_shapes=s_scr,
        compiler_params=_cparams(("parallel" if side is None else "arbitrary", "arbitrary")),
    )(*args, *s_args)
    return res


CONV_TC = 128


CONV_R = 256
HALO = 8


def _conv_pre(e, w, b):
    shifted = [pltpu.roll(e, s, 0) for s in (1, 2, 3)]
    pre = b + w[3:4, :] * e
    for s in (1, 2, 3):
        pre = pre + w[3 - s:4 - s, :] * shifted[s - 1]
    return pre, shifted


def conv_fwd(projmain, conv_w, conv_b):
    S = projmain.shape[0]
    tc = CONV_TC

    def body(u_ref, w_ref, b_ref, o_ref):
        u, w = u_ref[...], w_ref[...]
        row = _iota(u.shape, 0)
        pre = b_ref[...] + w[3:4, :] * u
        for s in (1, 2, 3):
            pre = pre + w[3 - s:4 - s, :] * jnp.where(row >= s, pltpu.roll(u, s, 0), 0.0)
        o_ref[...] = pre * _sig(pre)

    return pl.pallas_call(
        body,
        name="conv_fwd",
        out_shape=jax.ShapeDtypeStruct((S, CONV_DIM), F32),
        grid=(CONV_DIM // tc,),
        in_specs=[
            pl.BlockSpec((S, tc), lambda j: (0, OFF_XBC // tc + j)),
            pl.BlockSpec((CONV_K, tc), lambda j: (0, j)),
            pl.BlockSpec((1, tc), lambda j: (0, j)),
        ],
        out_specs=pl.BlockSpec((S, tc), lambda j: (0, j)),
        compiler_params=_cparams(("parallel",)),
    )(projmain, conv_w, conv_b)


def conv_bwd(dxs, dxs_skip, d_b, d_c, projmain, conv_w, conv_b):
    S = projmain.shape[0]
    tc = CONV_TC
    n_xs, n_b = D_INNER // tc, SSD_GROUPS * D_STATE // tc

    def body(dx_ref, dskip_ref, dbm_ref, dcm_ref, u_ref, w_ref, b_ref, du_ref, dw_ref, db_ref, u_pad, d_pad):
        j = pl.program_id(0)
        zeros = jnp.zeros((HALO, tc), F32)
        for pad in (u_pad, d_pad):
            pad[0:HALO, :] = zeros
            pad[HALO + S:2 * HALO + S, :] = zeros
        u_pad[HALO:HALO + S, :] = u_ref[...]
        d_pad[HALO:HALO + S, :] = jnp.where(j < n_xs, dx_ref[...] + dskip_ref[...],
                                            jnp.where(j < n_xs + n_b, dbm_ref[...], dcm_ref[...]))
        w, b = w_ref[...], b_ref[...]
        n = CONV_R + 2 * HALO
        keep = slice(HALO, HALO + CONV_R)

        def chunk(c, sums):
            r0 = pl.multiple_of(c * CONV_R, CONV_R)
            e = u_pad[pl.ds(r0, n), :]
            pre, shifted = _conv_pre(e, w, b)
            sg = _sig(pre)
            dpre = d_pad[pl.ds(r0, n), :] * (sg * (1.0 + pre * (1.0 - sg)))
            du = w[3:4, :] * dpre
            for s in (1, 2, 3):
                du = du + w[3 - s:4 - s, :] * pltpu.roll(dpre, n - s, 0)
            du_ref[pl.ds(r0, CONV_R), :] = du[keep].astype(du_ref.dtype)
            dk = dpre[keep]
            taps = [shifted[2], shifted[1], shifted[0], e]
            return tuple(acc + jnp.sum(dk * t[keep], axis=0, keepdims=True) for acc, t in zip(sums[:4], taps)) + (
                sums[4] + jnp.sum(dk, axis=0, keepdims=True),)

        zero = jnp.zeros((1, tc), F32)
        sums = lax.fori_loop(0, S // CONV_R, chunk, (zero,) * 5)
        for k in range(CONV_K):
            dw_ref[k:k + 1, :] = sums[k]
        db_ref[...] = sums[4]

    return pl.pallas_call(
        body,
        name="conv_bwd",
        out_shape=[
            jax.ShapeDtypeStruct((S, CONV_DIM), MXU),
            jax.ShapeDtypeStruct((CONV_K, CONV_DIM), F32),
            jax.ShapeDtypeStruct((1, CONV_DIM), F32),
        ],
        grid=(CONV_DIM // tc,),
        in_specs=[
            pl.BlockSpec((S, tc), lambda j: (0, jnp.minimum(j, n_xs - 1))),
            pl.BlockSpec((S, tc), lambda j: (0, jnp.minimum(j, n_xs - 1))),
            pl.BlockSpec((S, tc), lambda j: (0, jnp.clip(j - n_xs, 0, n_b - 1))),
            pl.BlockSpec((S, tc), lambda j: (0, jnp.clip(j - n_xs - n_b, 0, n_b - 1))),
            pl.BlockSpec((S, tc), lambda j: (0, OFF_XBC // tc + j)),
            pl.BlockSpec((CONV_K, tc), lambda j: (0, j)),
            pl.BlockSpec((1, tc), lambda j: (0, j)),
        ],
        out_specs=[
            pl.BlockSpec((S, tc), lambda j: (0, j)),
            pl.BlockSpec((CONV_K, tc), lambda j: (0, j)),
            pl.BlockSpec((1, tc), lambda j: (0, j)),
        ],
        scratch_shapes=[pltpu.VMEM((S + 2 * HALO, tc), F32)] * 2,
        compiler_params=_cparams(("arbitrary",)),
    )(dxs, dxs_skip, d_b, d_c, projmain, conv_w, conv_b)


def _head_expand():
    r, j = _iota((LANES, D_INNER), 0), _iota((LANES, D_INNER), 1)
    return ((j >= r * HEAD_DIM) & (j < r * HEAD_DIM + HEAD_DIM)).astype(F32).astype(MXU)


def _head_reduce():
    j, r = _iota((D_INNER, LANES), 0), _iota((D_INNER, LANES), 1)
    return ((j >= r * HEAD_DIM) & (j < r * HEAD_DIM + HEAD_DIM)).astype(F32).astype(MXU)


def ssd_prep(dtraw, dt_bias_pad, a_exp):
    S = dtraw.shape[0]

    def body(dtr_ref, bias_ref, a_ref, dte_ref, cse_ref):
        dt = _softplus(dtr_ref[...] + bias_ref[...])
        dte = _dot3_l(dt, _head_expand())
        dte_ref[...] = dte
        incl = _tri(CHUNK, lambda r, c: r >= c)
        cse_ref[...] = _dot3_r(incl, dte * a_ref[...])

    return pl.pallas_call(
        body,
        name="ssd_prep",
        out_shape=[jax.ShapeDtypeStruct((S, D_INNER), F32)] * 2,
        grid=(S // CHUNK,),
        in_specs=[
            pl.BlockSpec((CHUNK, LANES), lambda c: (c, 0)),
            pl.BlockSpec((1, LANES), lambda c: (0, 0)),
            pl.BlockSpec((1, D_INNER), lambda c: (0, 0)),
        ],
        out_specs=[pl.BlockSpec((CHUNK, D_INNER), lambda c: (c, 0))] * 2,
        compiler_params=_cparams(("parallel",)),
    )(dtraw, dt_bias_pad, a_exp)


GW = 4 * HEAD_DIM
SSD_GPS = 8
SSD_GPS_BWD = 8


def ssd_fwd(xbc_act, dte, cse, side=None):
    S = xbc_act.shape[0]
    nc = S // CHUNK

    def body(xs_ref, b_ref, c_ref, dte_ref, cse_ref, y_ref, st_ref, s_scr):
        c = pl.program_id(1)

        @pl.when(c == 0)
        def _():
            s_scr[...] = jnp.zeros_like(s_scr)

        for gg in range(SSD_GPS):
            ch, st = slice(GW * gg, GW * (gg + 1)), slice(D_STATE * gg, D_STATE * (gg + 1))
            s_in = s_scr[:, ch]
            st_ref[0, :, ch] = s_in
            cs = cse_ref[:, ch]
            xd = xs_ref[:, ch] * dte_ref[:, ch]
            btb = b_ref[:, st].T.astype(MXU)
            cs_last = cs[CHUNK - 1:CHUNK, :]
            gy = _dot(c_ref[:, st].astype(MXU), jnp.concatenate([btb, s_in.astype(MXU)], axis=1))
            g, y = gy[:, :CHUNK], gy[:, CHUNK:] * jnp.exp(cs)
            cs_t = cs.T
            row, col = _iota((CHUNK, CHUNK), 0), _iota((CHUNK, CHUNK), 1)
            ms = []
            for h in range(4):
                lo = HEAD_DIM * h
                lam = jnp.where(row >= col, jnp.exp(cs[:, lo:lo + 1] - cs_t[lo:lo + 1, :]), 0.0)
                ms.append((g * lam).astype(MXU))
            yd = _dot(jnp.concatenate(ms, axis=0), xd.astype(MXU))
            lane = _iota((CHUNK, GW), 1)
            for h in range(4):
                y = y + jnp.where((lane >= HEAD_DIM * h) & (lane < HEAD_DIM * (h + 1)),
                                  yd[CHUNK * h:CHUNK * (h + 1)], 0.0)
            y_ref[:, ch] = y
            w = (xd * jnp.exp(cs_last - cs)).astype(MXU)
            s_scr[:, ch] = jnp.exp(cs_last) * s_in + _dot(btb, w)

    grid = (SSD_GROUPS // SSD_GPS, nc)
    gw, gs = GW * SSD_GPS, D_STATE * SSD_GPS
    body, s_in, s_args, s_shape, s_out, s_scr = host_exchange(body, 5, 2, grid, side)
    return pl.pallas_call(
        body,
        name="ssd_fwd",
        out_shape=[
            jax.ShapeDtypeStruct((S, D_INNER), F32),
            jax.ShapeDtypeStruct((nc, D_STATE, D_INNER), F32),
        ] + s_shape,
        grid=grid,
        in_specs=[
            pl.BlockSpec((CHUNK, gw), lambda g, c: (c, g)),
            pl.BlockSpec((CHUNK, gs), lambda g, c: (c, D_INNER // gs + g)),
            pl.BlockSpec((CHUNK, gs), lambda g, c: (c, (D_INNER + SSD_GROUPS * D_STATE) // gs + g)),
            pl.BlockSpec((CHUNK, gw), lambda g, c: (c, g)),
            pl.BlockSpec((CHUNK, gw), lambda g, c: (c, g)),
        ] + s_in,
        out_specs=[
            pl.BlockSpec((CHUNK, gw), lambda g, c: (c, g)),
            pl.BlockSpec((1, D_STATE, gw), lambda g, c: (c, 0, g)),
        ] + s_out,
        scratch_shapes=[pltpu.VMEM((D_STATE, gw), F32)] + s_scr,
        compiler_params=_cparams(("parallel" if side is None else "arbitrary", "arbitrary")),
    )(xbc_act, xbc_act, xbc_act, dte, cse, *s_args)


def ssd_bwd(dy, xbc_act, dte, cse, states, a_exp, side=None):
    S = xbc_act.shape[0]
    nc = S // CHUNK

    def body(*refs):
        for gg in range(SSD_GPS_BWD):
            ch, st = slice(GW * gg, GW * (gg + 1)), slice(D_STATE * gg, D_STATE * (gg + 1))
            cut = {GW * SSD_GPS_BWD: ch, D_STATE * SSD_GPS_BWD: st}
            one_group(*[r.at[(slice(None),) * (len(r.shape) - 1) + (cut[r.shape[-1]],)] for r in refs])

    def one_group(dy_ref, xs_ref, b_ref, c_ref, dte_ref, cse_ref, sin_ref, sout_ref, a_ref,
                  dxs_ref, db_ref, dc_ref, ddt_ref, dal_ref, ds_scr):
        j = pl.program_id(1)

        @pl.when(j == 0)
        def _():
            ds_scr[...] = jnp.zeros_like(ds_scr)
            dal_ref[...] = jnp.zeros_like(dal_ref)

        ds_out = ds_scr[...]
        dyv, xs = dy_ref[...], xs_ref[...]
        dt, cs = dte_ref[...], cse_ref[...]
        s_in = sin_ref[0]
        bm, cm = b_ref[...], c_ref[...]
        bb, cb = bm.astype(MXU), cm.astype(MXU)
        btb, ctb = bm.T.astype(MXU), cm.T.astype(MXU)
        dsb, sib = ds_out.astype(MXU), s_in.astype(MXU)
        xd = xs * dt
        ecs = jnp.exp(cs)
        cs_last = cs[CHUNK - 1:CHUNK, :]
        eend = jnp.exp(cs_last - cs)
        gy = _dot(cb, jnp.concatenate([btb, sib], axis=1))
        g, yoff = gy[:, :CHUNK], gy[:, CHUNK:] * ecs
        gd = _dot(bb, jnp.concatenate([ctb, dsb], axis=1))
        g_t, dxd_off = gd[:, :CHUNK], gd[:, CHUNK:] * eend
        cs_t = cs.T
        row, col = _iota((CHUNK, CHUNK), 0), _iota((CHUNK, CHUNK), 1)
        lane = _iota((CHUNK, GW), 1)
        heads = [(lane >= HEAD_DIM * h) & (lane < HEAD_DIM * (h + 1)) for h in range(4)]
        dyb, xdb = dyv.astype(MXU), xd.astype(MXU)
        dm_all = _dot_nt(jnp.concatenate([jnp.where(hm, dyv, 0.0) for hm in heads], axis=0).astype(MXU), xdb)
        dmt_all = _dot_nt(jnp.concatenate([jnp.where(hm, xd, 0.0) for hm in heads], axis=0).astype(MXU), dyb)
        lams, m_ts = [], []
        for h in range(4):
            lo = HEAD_DIM * h
            cs_col, cs_row = cs[:, lo:lo + 1], cs_t[lo:lo + 1, :]
            lams.append(jnp.where(row >= col, jnp.exp(cs_col - cs_row), 0.0))
            m_ts.append(g_t * jnp.where(col >= row, jnp.exp(cs_row - cs_col), 0.0))
        acc_all = _dot(jnp.concatenate(m_ts, axis=0).astype(MXU), dyb)
        dxd = dxd_off
        dg = jnp.zeros((CHUNK, CHUNK), F32)
        dcs = dyv * yoff - xd * dxd_off
        for h in range(4):
            blk = slice(CHUNK * h, CHUNK * (h + 1))
            dm, dm_t = dm_all[blk], dmt_all[blk]
            dxd = dxd + jnp.where(heads[h], acc_all[blk], 0.0)
            dg = dg + dm * lams[h]
            wdiff = (jnp.sum(dm * (g * lams[h]), axis=1, keepdims=True)
                     - jnp.sum(dm_t * m_ts[h], axis=1, keepdims=True))
            dcs = dcs + jnp.where(lane == HEAD_DIM * h, wdiff, 0.0)
        dye = (dyv * ecs).astype(MXU)
        dc_ref[...] = _dot(dg.astype(MXU), bb) + _dot_nt(dye, sib)
        db_ref[...] = _dot(dg.T.astype(MXU), cb) + _dot_nt((xd * eend).astype(MXU), dsb)
        ds_scr[...] = jnp.exp(cs_last) * ds_out + _dot(ctb, dye)
        last = jnp.sum(ds_out * sout_ref[0], axis=0, keepdims=True)
        rows = _iota((CHUNK, GW), 0)
        dcs = dcs + jnp.where(rows == CHUNK - 1, last, 0.0)
        dda = _dot3_r(_tri(CHUNK, lambda r, c: c >= r), dcs)
        ddt_ref[...] = a_ref[...] * dda + dxd * xs
        dal_ref[...] += jnp.sum(dt * dda, axis=0, keepdims=True)
        dxs_ref[...] = dxd * dt

    rc = lambda g, j: (nc - 1 - j, g)
    grid = (SSD_GROUPS // SSD_GPS_BWD, nc)
    gw, gs = GW * SSD_GPS_BWD, D_STATE * SSD_GPS_BWD
    body, s_in, s_args, s_shape, s_out, s_scr = host_exchange(body, 9, 5, grid, side)
    return pl.pallas_call(
        body,
        name="ssd_bwd",
        out_shape=[
            jax.ShapeDtypeStruct((S, D_INNER), F32),
            jax.ShapeDtypeStruct((S, SSD_GROUPS * D_STATE), F32),
            jax.ShapeDtypeStruct((S, SSD_GROUPS * D_STATE), F32),
            jax.ShapeDtypeStruct((S, D_INNER), F32),
            jax.ShapeDtypeStruct((1, D_INNER), F32),
        ] + s_shape,
        grid=grid,
        in_specs=[
            pl.BlockSpec((CHUNK, gw), rc),
            pl.BlockSpec((CHUNK, gw), rc),
            pl.BlockSpec((CHUNK, gs), lambda g, j: (nc - 1 - j, D_INNER // gs + g)),
            pl.BlockSpec((CHUNK, gs), lambda g, j: (nc - 1 - j, (D_INNER + SSD_GROUPS * D_STATE) // gs + g)),
            pl.BlockSpec((CHUNK, gw), rc),
            pl.BlockSpec((CHUNK, gw), rc),
            pl.BlockSpec((1, D_STATE, gw), lambda g, j: (nc - 1 - j, 0, g)),
            pl.BlockSpec((1, D_STATE, gw), lambda g, j: (jnp.minimum(nc - j, nc - 1), 0, g)),
            pl.BlockSpec((1, gw), lambda g, j: (0, g)),
        ] + s_in,
        out_specs=[
            pl.BlockSpec((CHUNK, gw), rc),
            pl.BlockSpec((CHUNK, gs), rc),
            pl.BlockSpec((CHUNK, gs), rc),
            pl.BlockSpec((CHUNK, gw), rc),
            pl.BlockSpec((1, gw), lambda g, j: (0, g)),
        ] + s_out,
        scratch_shapes=[pltpu.VMEM((D_STATE, gw), F32)] + s_scr,
        compiler_params=_cparams(("parallel" if side is None else "arbitrary", "arbitrary")),
    )(dy, xbc_act, xbc_act, xbc_act, dte, cse, states, states, a_exp, *s_args)


SB_T = 256
SB_DROP = 104.0
SB_PAIRS = 1
SB_PARTS = 2
SB_W = SB_PAIRS * LANES
SB_LANES = [slice(LANES * p, LANES * (p + 1)) for p in range(SB_PAIRS)]


def _sb_scores(qm, k_ref, ks, rowi, coli, diag):
    kblk = k_ref[pl.ds(ks, SB_T), :].astype(MXU)
    z = _dot_nt(qm, kblk) * SB_SCALE
    sp = _softplus(z)
    if not diag:
        return kblk, z, None, sp, sp
    mask = (ks + coli) < rowi
    return kblk, z, mask, sp, jnp.where(mask, sp, 0.0)


def _sb_stack(v):
    lane = _iota(v.shape, 1)
    return jnp.concatenate([jnp.where(lane < HEAD_DIM, v, 0.0), jnp.where(lane >= HEAD_DIM, v, 0.0)], axis=0)


def _sb_unstack(v):
    lane = _iota((SB_T, LANES), 1)
    return jnp.where(lane < HEAD_DIM, v[:SB_T], v[SB_T:])


def _sb_rows(qb):
    r = _iota((2 * SB_T, SB_T), 0)
    return qb * SB_T + jnp.where(r >= SB_T, r - SB_T, r), _iota((2 * SB_T, SB_T), 1)


def sb_fwd(projmain, side=None):
    S = projmain.shape[0]
    nq = S // SB_T

    def body(q_ref, k_ref, v_ref, o_ref, t_ref, n_ref):
        hp, qb = pl.program_id(0), pl.program_id(1)
        qsts = [_sb_stack(q_ref[:, sl]).astype(MXU) for sl in SB_LANES]
        rowi, coli = _sb_rows(qb)
        u_after = _tri(SB_T, lambda r, c: r > c)

        def cond(carry):
            i, rmin, _, _ = carry
            return (i <= qb) & (rmin < SB_DROP)

        def kstep(carry, diag=False):
            i, _, rs, accs = carry
            ks = pl.multiple_of((qb - i) * SB_T, SB_T)
            rs, accs = list(rs), list(accs)
            for p, sl in enumerate(SB_LANES):
                _, z, mask, sp, spm = _sb_scores(qsts[p], k_ref.at[:, sl], ks, rowi, coli, diag)
                vblk = v_ref[pl.ds(ks, SB_T), sl].astype(MXU)
                a = jnp.exp(z - sp - _dot3_l(spm, u_after, SB_PARTS) - rs[p])
                if diag:
                    a = jnp.where(mask, a, 0.0)
                accs[p] = accs[p] + _dot(a.astype(MXU), vblk)
                rs[p] = rs[p] + jnp.sum(spm, axis=1, keepdims=True)
            return i + 1, jnp.min(functools.reduce(jnp.minimum, rs)), rs, accs

        first = kstep((jnp.int32(0), jnp.float32(0.0), [jnp.zeros((2 * SB_T, 1), F32)] * SB_PAIRS,
                       [jnp.zeros((2 * SB_T, LANES), F32)] * SB_PAIRS), diag=True)
        n, _, rs, accs = lax.while_loop(cond, kstep, first)
        for p, sl in enumerate(SB_LANES):
            o_ref[:, sl] = _sb_unstack(accs[p]).astype(o_ref.dtype)
            t_ref[:, sl] = _sb_unstack(jnp.broadcast_to(rs[p], (2 * SB_T, LANES)))
        n_ref[hp, qb] = n

    grid = (SB_WIDTH // SB_W, nq)
    body, s_in, s_args, s_shape, s_out, s_scr = host_exchange(body, 3, 3, grid, side)
    return pl.pallas_call(
        body,
        name="sb_fwd",
        out_shape=[jax.ShapeDtypeStruct((S, SB_WIDTH), MXU), jax.ShapeDtypeStruct((S, SB_WIDTH), F32),
                   jax.ShapeDtypeStruct((SB_WIDTH // SB_W, nq), jnp.int32)] + s_shape,
        grid=grid,
        in_specs=[
            pl.BlockSpec((SB_T, SB_W), lambda h, i: (i, OFF_Q // SB_W + h)),
            pl.BlockSpec((S, SB_W), lambda h, i: (0, OFF_K // SB_W + h)),
            pl.BlockSpec((S, SB_W), lambda h, i: (0, OFF_V // SB_W + h)),
        ] + s_in,
        out_specs=[pl.BlockSpec((SB_T, SB_W), lambda h, i: (i, h))] * 2
        + [pl.BlockSpec(memory_space=pltpu.SMEM)] + s_out,
        scratch_shapes=s_scr,
        compiler_params=_cparams(("arbitrary", "arbitrary")),
    )(projmain, projmain, projmain, *s_args)


def sb_bwd(projmain, do, t_exp, nblk, side=None):
    S = projmain.shape[0]
    nq = S // SB_T

    def body(n_ref, q_ref, k_ref, v_ref, do_ref, t_ref, dq_ref, dk_ref, dv_ref):
        hp, qb = pl.program_id(0), pl.program_id(1)

        @pl.when(qb == 0)
        def _():
            dk_ref[...] = jnp.zeros_like(dk_ref)
            dv_ref[...] = jnp.zeros_like(dv_ref)

        lane = _iota((SB_T, LANES), 1)
        qsts, dosts, tots = [], [], []
        for sl in SB_LANES:
            qsts.append(_sb_stack(q_ref[:, sl]).astype(MXU))
            dosts.append(_sb_stack(do_ref[:, sl]).astype(MXU))
            tv = t_ref[:, sl]
            tots.append(jnp.concatenate(
                [jnp.sum(jnp.where(lane == HEAD_DIM * hh, tv, 0.0), axis=1, keepdims=True) for hh in range(2)],
                axis=0))
        rowi, coli = _sb_rows(qb)
        u_upto = _tri(SB_T, lambda r, c: r <= c)
        u_before = _tri(SB_T, lambda r, c: r < c)
        kept = jnp.clip(n_ref[hp, qb], 1, qb + 1)

        def kstep(kb, carry, diag=False):
            ks = pl.multiple_of(kb * SB_T, SB_T)
            out = []
            for p, sl in enumerate(SB_LANES):
                psp, pg, dq = carry[p]
                kblk, z, mask, sp, spm = _sb_scores(qsts[p], k_ref.at[:, sl], ks, rowi, coli, diag)
                vblk = v_ref[pl.ds(ks, SB_T), sl].astype(MXU)
                after = tots[p] - (psp + _dot3_l(spm, u_upto, SB_PARTS))
                a = jnp.exp(z - sp - after)
                if diag:
                    a = jnp.where(mask, a, 0.0)
                gm = _dot_nt(dosts[p], vblk) * a
                before = pg + _dot3_l(gm, u_before, SB_PARTS)
                sg = jnp.exp(z - sp)
                dz = (gm * (1.0 - sg) - sg * before) * SB_SCALE
                if diag:
                    dz = jnp.where(mask, dz, 0.0)
                dzb = dz.astype(MXU)
                dk_ref[pl.ds(ks, SB_T), sl] += _dot_tn(dzb, qsts[p])
                dv_ref[pl.ds(ks, SB_T), sl] += _dot_tn(a.astype(MXU), dosts[p])
                out.append((psp + jnp.sum(spm, axis=1, keepdims=True),
                            pg + jnp.sum(gm, axis=1, keepdims=True), dq + _dot(dzb, kblk)))
            return tuple(out)

        zero1 = jnp.zeros((2 * SB_T, 1), F32)
        state = lax.fori_loop(qb + 1 - kept, qb, kstep,
                              ((zero1, zero1, jnp.zeros((2 * SB_T, LANES), F32)),) * SB_PAIRS)
        state = kstep(qb, state, diag=True)
        for p, sl in enumerate(SB_LANES):
            dq_ref[:, sl] = _sb_unstack(state[p][2]).astype(dq_ref.dtype)

    grid = (SB_WIDTH // SB_W, nq)
    body, s_in, s_args, s_shape, s_out, s_scr = host_exchange(body, 6, 3, grid, side)
    return pl.pallas_call(
        body,
        name="sb_bwd",
        out_shape=[
            jax.ShapeDtypeStruct((S, SB_WIDTH), MXU),
            jax.ShapeDtypeStruct((S, SB_WIDTH), F32),
            jax.ShapeDtypeStruct((S, SB_WIDTH), F32),
        ] + s_shape,
        grid=grid,
        in_specs=[
            pl.BlockSpec(memory_space=pltpu.SMEM),
            pl.BlockSpec((SB_T, SB_W), lambda h, i: (i, OFF_Q // SB_W + h)),
            pl.BlockSpec((S, SB_W), lambda h, i: (0, OFF_K // SB_W + h)),
            pl.BlockSpec((S, SB_W), lambda h, i: (0, OFF_V // SB_W + h)),
            pl.BlockSpec((SB_T, SB_W), lambda h, i: (i, h)),
            pl.BlockSpec((SB_T, SB_W), lambda h, i: (i, h)),
        ] + s_in,
        out_specs=[
            pl.BlockSpec((SB_T, SB_W), lambda h, i: (i, h)),
            pl.BlockSpec((S, SB_W), lambda h, i: (0, h)),
            pl.BlockSpec((S, SB_W), lambda h, i: (0, h)),
        ] + s_out,
        scratch_shapes=s_scr,
        compiler_params=_cparams(("arbitrary", "arbitrary")),
    )(nblk, projmain, projmain, projmain, do, t_exp, *s_args)


def local_step(x, pb, target, W, P, late, core):
    D = D_MODEL
    full = lambda a, w=D: (a, 0, w)

    (n1b,) = rowwise("norm_pre", lambda xv, g: (_rms(xv, g),), [full(x)], [full(P["norm_mix_pre"])],
                     [(D, D, MXU)])
    projmain, w_gate = mm("in_proj", n1b, W["main_zx"], "nn",
                          side=exchange_plan([("gather2", late["w_gate"])]))
    qkv = mm("qkv_proj", n1b, W["main_qkv"], "nn", out_dtype=MXU)
    W = {**W, "gate": w_gate}
    gate_pre = mm("gate_proj", n1b, W["gate"], "nn", b_slabs=True)
    dtraw = mm("dt_proj", n1b, W["dt"], "nn")
    xbc_act = conv_fwd(projmain, P["conv_w"], P["conv_b"])
    dte, cse = ssd_prep(dtraw, P["dt_bias_pad"], P["a_exp"])
    y_scan, states, *got = ssd_fwd(xbc_act, dte, cse,
                                   side=exchange_plan([("gather2", late[n]) for n in LATE[:LATE_SPLIT]]))
    W = {**W, **late_matrices(dict(zip(LATE[:LATE_SPLIT], got)))}

    def f_gate(ysc, xs, z, dsk, nw):
        return _rms((ysc + xs * dsk) * (z * _sig(z)), nw)

    (y_ssd_b,) = rowwise("ssd_gate", lambda *a: (f_gate(*a),),
                         [(y_scan, 0, GW), (xbc_act, 0, GW), (projmain, OFF_Z, GW)],
                         [(P["dsk_exp"], 0, GW), (P["ssd_norm"], 0, GW)], [(D_INNER, GW, MXU)], tr=256,
                         groups=SSD_GROUPS)
    y_sb_b, t_exp, sb_kept, *got = sb_fwd(
        qkv, side=exchange_plan([("gather2", late[n]) for n in LATE[LATE_SPLIT:]]))
    W = {**W, **late_matrices(dict(zip(LATE[LATE_SPLIT:], got)))}
    u1 = mm("ssd_branch", y_ssd_b, W["ssd"], "nn")
    u2 = mm("sb_branch", y_sb_b, W["sb"], "nn")

    def f_merge(a1, a2, g1, g2, b1, b2):
        return (_sig(g1 + b1) * a1 + _sig(g2 + b2) * a2,)

    gate_rows = [(gate_pre, 0, D), (gate_pre, D, D)]
    gate_bias = [(P["b_gate"], 0, D), (P["b_gate"], D, D)]
    (merged_b,) = rowwise("merge", f_merge, [full(u1), full(u2)] + gate_rows, gate_bias, [(D, D, MXU)])
    mo = mm("out_proj", merged_b, W["out"], "nn")

    def f_mid(xv, m, gpost, gffn):
        h1 = xv + _rms(m, gpost)
        return h1, _rms(h1, gffn)

    h1, n2b = rowwise("mix_post", f_mid, [full(x), full(mo)],
                      [full(P["norm_mix_post"]), full(P["norm_ffn_pre"])], [(D, D, F32), (D, D, MXU)])
    a_ff, rb = mm("ff1", n2b, W["ff1"], "nn", b_slabs=True,
                  extra=[(MXU, lambda acc: jnp.square(jnp.maximum(acc, 0.0)))])
    ff = mm("ff2", rb, W["ff2"], "nn")

    def f_ffn_post(h, f, g):
        h2 = h + _rms(f, g)
        return h2, h2

    h2, h2b = rowwise("ffn_post", f_ffn_post, [full(h1), full(ff)], [full(P["norm_ffn_post"])],
                      [(D, D, F32), (D, D, MXU)])
    pgp = mm("ple_gate", h2b, W["pg"], "nn")
    pe = mm("ple_proj", pb, W["ple"], "nn", b_slabs=True)

    def f_ple(h2v, gp, pev, tgt, g):
        f = lambda h, a, b, gg: h + _rms(_sig(a) * b, gg)
        h3, vjp = jax.vjp(f, h2v, gp, pev, g)
        err = h3 - tgt
        dh2, dgp, dpe, dg = vjp(err * (1.0 / D))
        lossc = (0.5 / D) * jnp.sum(err * err, axis=0, keepdims=True)
        return dh2, dgp, dpe, dg, lossc

    dh2a, dpgp_b, dpe_b, g_ple, lossc = rowwise(
        "ple_loss", f_ple, [full(h2), full(pgp), full(pe), full(target)], [full(P["norm_ple_post"])],
        [(D, D, F32), (D, D, MXU), (D, D, MXU)], [(D, D), (D, D)])
    dh2 = mm("d_ple_gate_x", dpgp_b, W["pg"], "nt", add=dh2a)
    gW = {}
    gW["w_ple_gate"] = mm("d_ple_gate_w", h2b, dpgp_b, "tn")
    gW["w_ple"] = mm("d_ple_w", pb, dpe_b, "tn", out_slabs=True)

    def b_ffn_post(d, f, g):
        _, vjp = jax.vjp(_rms, f, g)
        return vjp(d)

    dff_b, g_ffn_post = rowwise("d_ffn_post", b_ffn_post, [full(dh2), full(ff)], [full(P["norm_ffn_post"])],
                                [(D, D, MXU)], [(D, D)])
    da_b = mm("d_ff2_x", dff_b, W["ff2"], "nt", out_dtype=MXU,
              epi=lambda acc, act: acc * (2.0 * jnp.maximum(act, 0.0)), epi_args=[a_ff])
    gW["w_ff2"] = mm("d_ff2_w", rb, dff_b, "tn")
    dn2 = mm("d_ff1_x", da_b, W["ff1"], "nt", b_slabs=True)
    gW["w_ff1"] = mm("d_ff1_w", n2b, da_b, "tn", out_slabs=True)

    def b_mid(d2, dn, h, m, gpost, gffn):
        _, vjp = jax.vjp(_rms, h, gffn)
        dh, dgffn = vjp(dn)
        dh1 = d2 + dh
        _, vjp2 = jax.vjp(_rms, m, gpost)
        dm, dgpost = vjp2(dh1)
        return dh1, dm, dgpost, dgffn

    dh1, dmo_b, g_mix_post, g_ffn_pre = rowwise(
        "d_mix_post", b_mid, [full(dh2), full(dn2), full(h1), full(mo)],
        [full(P["norm_mix_post"]), full(P["norm_ffn_pre"])], [(D, D, F32), (D, D, MXU)], [(D, D), (D, D)])
    dmerged = mm("d_out_x", dmo_b, W["out"], "nt")
    gW["w_out"] = mm("d_out_w", merged_b, dmo_b, "tn")

    def b_merge(d, a1, a2, g1, g2, b1, b2):
        s1, s2 = _sig(g1 + b1), _sig(g2 + b2)
        dg1 = d * a1 * s1 * (1.0 - s1)
        dg2 = d * a2 * s2 * (1.0 - s2)
        dg = jnp.concatenate([dg1, dg2], axis=1)
        return d * s1, d * s2, dg, jnp.sum(dg, axis=0, keepdims=True)

    du1_b, du2_b, dgp_b, g_b_gate = rowwise(
        "d_merge", b_merge, [full(dmerged), full(u1), full(u2)] + gate_rows, gate_bias,
        [(D, D, MXU), (D, D, MXU), (2 * D, 2 * D, MXU)], [(2 * D, 2 * D)])
    dy_ssd = mm("d_ssd_branch_x", du1_b, W["ssd"], "nt")
    dy_sb = mm("d_sb_branch_x", du2_b, W["sb"], "nt")
    gW["w_ssd_branch"] = mm("d_ssd_branch_w", y_ssd_b, du1_b, "tn")
    gW["w_sb_branch"] = mm("d_sb_branch_w", y_sb_b, du2_b, "tn")

    def b_gate(d, ysc, xs, z, dsk, nw):
        _, vjp = jax.vjp(f_gate, ysc, xs, z, dsk, nw)
        return vjp(d)

    slabs = {n: gW[n] if n in COL_SHARDED else gW[n].reshape((N_DEV,) + ROW_SHARDED[n]) for n in LATE}
    dy_scan, dxs_skip, dz_b, g_dsk_exp, g_ssd_norm, *from_sibling = rowwise(
        "d_ssd_gate", b_gate, [(dy_ssd, 0, GW), (y_scan, 0, GW), (xbc_act, 0, GW), (projmain, OFF_Z, GW)],
        [(P["dsk_exp"], 0, GW), (P["ssd_norm"], 0, GW)],
        [(D_INNER, GW, F32), (D_INNER, GW, F32), (D_INNER, GW, MXU)], [(D_INNER, GW), (D_INNER, GW)],
        tr=256, groups=SSD_GROUPS, side=exchange_plan([("pair", slabs[n]) for n in LATE]))
    chip_sums = [pair_sum("pair_sum_" + n, slabs[n], s, core) for n, s in zip(LATE, from_sibling)]
    dxs, d_b, d_c, ddt_part, g_a_exp, *late_parts = ssd_bwd(
        dy_scan, xbc_act, dte, cse, states, P["a_exp"], side=exchange_plan([("chips", s) for s in chip_sums]))

    def b_dt(dpart, dtr, bias):
        ddt = _dot3_l(dpart, _head_reduce())
        d = ddt * _sig(dtr + bias)
        return d, jnp.sum(d, axis=0, keepdims=True)

    ddt_b, g_dt_bias_pad = rowwise("d_dt", b_dt, [(ddt_part, 0, D_INNER), (dtraw, 0, LANES)],
                                   [(P["dt_bias_pad"], 0, LANES)], [(LANES, LANES, MXU)], [(LANES, LANES)])
    dxbc_b, g_conv_w, g_conv_b = conv_bwd(dxs, dxs_skip, d_b, d_c, projmain, P["conv_w"], P["conv_b"])
    dq_b, dk, dv = sb_bwd(qkv, dy_sb, t_exp, sb_kept)
    dmain_b = jnp.concatenate([dz_b, dxbc_b, dq_b, dk.astype(MXU), dv.astype(MXU)], axis=1)
    g_main = mm("d_in_w", n1b, dmain_b, "tn")
    g_dt = mm("d_dt_w", n1b, ddt_b, "tn")
    last = {"w_gate": mm("d_gate_w", n1b, dgp_b, "tn", out_slabs=True)}
    last["w_in"], conv_slabs = in_grad_slabs({"main": g_main, "dt": g_dt, "conv_w": g_conv_w})
    dn1_dt = mm("d_dt_x", ddt_b, W["dt"], "nt")
    dn1_gate, *from_sibling = mm("d_gate_x", dgp_b, W["gate"], "nt", add=dn1_dt, b_slabs=True,
                                 side=exchange_plan([("pair", last[n]) for n in FIRST]))
    chip_sums = [pair_sum("pair_sum_" + n, last[n], s, core) for n, s in zip(FIRST, from_sibling)]
    dn1, *first_parts = mm("d_in_x", dmain_b, W["main"], "nt", add=dn1_gate,
                           side=exchange_plan([("chips", s) for s in chip_sums]))

    def b_pre(d1, dn, xv, g):
        _, vjp = jax.vjp(_rms, xv, g)
        dx, dg = vjp(dn)
        return d1 + dx, dg

    grad_x, g_mix_pre = rowwise("d_norm_pre", b_pre, [full(dh1), full(dn1), full(x)], [full(P["norm_mix_pre"])],
                                [(D, D, F32)], [(D, D)])

    parts = {**dict(zip(LATE, late_parts)), **dict(zip(FIRST, first_parts))}
    gS = {
        "norm_mix_pre": g_mix_pre, "conv_b": g_conv_b, "dt_bias_pad": g_dt_bias_pad, "a_exp": g_a_exp,
        "dsk_exp": g_dsk_exp, "ssd_norm": g_ssd_norm, "b_gate": g_b_gate,
        "norm_mix_post": g_mix_post, "norm_ffn_pre": g_ffn_pre, "norm_ffn_post": g_ffn_post,
        "norm_ple_post": g_ple,
    }
    return lossc, grad_x, parts, conv_slabs, gS


IN_SPLITS = (2048, 6144, 6176, 7200, 8224)
IN_SHARD = 1156
IN_PAD = 1280
COL_SHARDED = {"w_in": (1024, IN_SHARD), "conv_w": (4, 512), "w_gate": (1024, 256), "w_ff1": (1024, 512),
               "w_ple": (256, 128)}
ROW_SHARDED = {"w_ssd_branch": (256, 1024), "w_sb_branch": (128, 1024), "w_out": (128, 1024),
               "w_ff2": (512, 1024), "w_ple_gate": (128, 1024)}
SHARDED = tuple(COL_SHARDED) + tuple(ROW_SHARDED)
SMALL = (
    ("norm_mix_pre", 1024), ("conv_b", 4096), ("dt_bias", 32), ("a_log", 32), ("d_skip", 32), ("ssd_norm", 2048),
    ("b_gate", 2048), ("norm_mix_post", 1024), ("norm_ffn_pre", 1024), ("norm_ffn_post", 1024),
    ("norm_ple_post", 1024),
)
ROW = 1024
SMALL_ROWS = 16


def _rows_of(n):
    return -(-n // ROW)


def _pad_to(v, n, axis=-1):
    pad = [(0, 0)] * v.ndim
    pad[axis] = (0, n - v.shape[axis])
    return jnp.pad(v, pad)


FIRST = ("w_in", "w_gate")
LATE = ("w_ssd_branch", "w_sb_branch", "w_out", "w_ff1", "w_ff2", "w_ple", "w_ple_gate")
LATE_SPLIT = 3


def in_matrices(w_in_slabs):
    w_in = jnp.concatenate([w_in_slabs[k, :, :IN_SHARD] for k in range(N_DEV)], axis=1)
    return {
        "main": jnp.concatenate([w_in[:, :IN_SPLITS[1]], w_in[:, IN_SPLITS[2]:]], axis=1),
        "main_zx": w_in[:, :IN_SPLITS[1]], "main_qkv": w_in[:, IN_SPLITS[2]:],
        "dt": _pad_to(w_in[:, IN_SPLITS[1]:IN_SPLITS[2]], LANES),
    }


def late_matrices(g):
    short = {"w_ff1": "ff1", "w_ple": "ple", "w_ssd_branch": "ssd", "w_sb_branch": "sb", "w_out": "out",
             "w_ff2": "ff2", "w_ple_gate": "pg"}
    stack = lambda a: a.reshape(a.shape[0] * a.shape[1], a.shape[2])
    return {short[n]: a if n in COL_SHARDED else stack(a) for n, a in g.items()}


def build_small(small, conv_w_slabs):
    P = {k: small[k] for k in ("norm_mix_pre", "conv_b", "ssd_norm", "b_gate", "norm_mix_post", "norm_ffn_pre",
                               "norm_ffn_post", "norm_ple_post")}
    P["conv_w"] = jnp.concatenate([conv_w_slabs[k] for k in range(N_DEV)], axis=1)
    P["dt_bias_pad"] = _pad_to(small["dt_bias"], LANES)
    P["a_exp"] = jnp.repeat(-jnp.exp(small["a_log"]), HEAD_DIM, axis=1)
    P["dsk_exp"] = jnp.repeat(small["d_skip"], HEAD_DIM, axis=1)
    return P


def small_grads(gS, small):
    heads = lambda a: a.reshape(SSD_HEADS, HEAD_DIM).sum(axis=1)[None, :]
    out = {k: gS[k] for k in ("norm_mix_pre", "conv_b", "ssd_norm", "b_gate", "norm_mix_post", "norm_ffn_pre",
                              "norm_ffn_post", "norm_ple_post")}
    out["dt_bias"] = gS["dt_bias_pad"][:, :SSD_HEADS]
    out["a_log"] = heads(gS["a_exp"]) * (-jnp.exp(small["a_log"]))
    out["d_skip"] = heads(gS["dsk_exp"])
    return out


def in_grad_slabs(g):
    pieces = ((0, IN_SPLITS[1], g["main"], 0), (IN_SPLITS[1], IN_SPLITS[2], g["dt"], -IN_SPLITS[1]),
              (IN_SPLITS[2], N_DEV * IN_SHARD, g["main"], IN_SPLITS[1] - IN_SPLITS[2]))
    slabs = []
    for k in range(N_DEV):
        a, b = IN_SHARD * k, IN_SHARD * (k + 1)
        cut = [src[:, max(a, lo) + off:min(b, hi) + off] for lo, hi, src, off in pieces if max(a, lo) < min(b, hi)]
        slabs.append(_pad_to(jnp.concatenate(cut, axis=1), IN_PAD))
    width = COL_SHARDED["conv_w"][1]
    return jnp.stack(slabs), jnp.stack([g["conv_w"][:, width * k:width * (k + 1)] for k in range(N_DEV)])


def pack_small(get):
    cols = [_pad_to(get(name).reshape(n), _rows_of(n) * ROW) for name, n in SMALL]
    return jnp.concatenate(cols).reshape(SMALL_ROWS, ROW)


def unpack_small(flat):
    out, r0 = {}, 0
    for name, n in SMALL:
        rows = _rows_of(n)
        out[name] = flat[r0:r0 + rows].reshape(rows * ROW)[:n].reshape(1, n)
        r0 += rows
    return out


N_CHIPS = 4
JOB_SEMS = {"all": 7, "scatter": 7, "gather2": 7, "pair": 4, "chips": 3}


def exchange_plan(jobs):
    n = len(jobs)
    kinds = [k for k, _ in jobs]
    srcs = [s for _, s in jobs]
    shapes = {"all": lambda s: (N_DEV,) + s.shape, "gather2": lambda s: (N_DEV,) + s.shape,
              "scatter": lambda s: s.shape, "pair": lambda s: (N_CHIPS,) + s.shape[1:], "chips": lambda s: s.shape}
    out_shape = [jax.ShapeDtypeStruct(shapes[k](s), s.dtype) for k, s in jobs]
    offs = [sum(JOB_SEMS[k] for k in kinds[:i]) for i in range(n + 1)]

    def run(phases, src, out, send_sems, recv_sems, local_sems):
        x, y, c = lax.axis_index("x"), lax.axis_index("y"), lax.axis_index("c")
        dev = lambda d: 4 * d[0] + 2 * d[1] + d[2]
        chip_no = lambda ch: 2 * ch[0] + ch[1]
        me, sib, my_chip = (x, y, c), (x, y, 1 - c), (x, y)
        others = [(1 - x, y), (x, 1 - y), (1 - x, 1 - y)]
        all_chips = [(0, 0), (0, 1), (1, 0), (1, 1)]
        peers = [(1 - x if k & 4 else x, 1 - y if k & 2 else y, 1 - c if k & 1 else c) for k in range(1, N_DEV)]
        starts, recvs, local = [], [], []
        chained = [[] for _ in others]

        for i, kind in enumerate(kinds):
            s_ref, o_ref = src[i], out[i]

            def rc(k, src_ref, dst_ref, to, i=i):
                s = offs[i] + k
                return pltpu.make_async_remote_copy(src_ref=src_ref, dst_ref=dst_ref, send_sem=send_sems.at[s],
                                                    recv_sem=recv_sems.at[s], device_id=to,
                                                    device_id_type=pl.DeviceIdType.MESH)

            if kind == "all":
                local.append(pltpu.make_async_copy(s_ref, o_ref.at[dev(me)], local_sems.at[i]))
                for k, peer in enumerate(peers):
                    starts.append(rc(k, s_ref, o_ref.at[dev(me)], peer))
                    recvs.append(rc(k, s_ref, o_ref.at[dev(peer)], peer))
            elif kind == "scatter":
                local.append(pltpu.make_async_copy(s_ref.at[dev(me)], o_ref.at[dev(me)], local_sems.at[i]))
                for k, peer in enumerate(peers):
                    starts.append(rc(k, s_ref.at[dev(peer)], o_ref.at[dev(me)], peer))
                    recvs.append(rc(k, s_ref.at[dev(me)], o_ref.at[dev(peer)], peer))
            elif kind == "gather2":
                local.append(pltpu.make_async_copy(s_ref, o_ref.at[dev(me)], local_sems.at[i]))
                starts.append(rc(0, s_ref, o_ref.at[dev(me)], sib))
                recvs.append(rc(0, s_ref, o_ref.at[dev(sib)], sib))
                for j, ch in enumerate(others):
                    same, other = (*ch, c), (*ch, 1 - c)
                    starts.append(rc(1 + j, s_ref, o_ref.at[dev(me)], same))
                    chained[j].append((rc(1 + j, s_ref, o_ref.at[dev(same)], same),
                                       rc(4 + j, o_ref.at[dev(same)], o_ref.at[dev(same)], sib)))
                    recvs.append(rc(4 + j, s_ref, o_ref.at[dev(other)], sib))
            elif kind == "pair":
                for j, ch in enumerate(all_chips):
                    starts.append(rc(j, s_ref.at[dev((*ch, 1 - c))], o_ref.at[j], sib))
                    recvs.append(rc(j, s_ref.at[dev((*ch, c))], o_ref.at[j], sib))
            else:
                mine = chip_no(my_chip)
                local.append(pltpu.make_async_copy(s_ref.at[mine], o_ref.at[mine], local_sems.at[i]))
                for j, ch in enumerate(others):
                    starts.append(rc(j, s_ref.at[chip_no(ch)], o_ref.at[mine], (*ch, c)))
                    recvs.append(rc(j, s_ref.at[mine], o_ref.at[chip_no(ch)], (*ch, c)))

        if "start" in phases:
            for cp in local + starts:
                cp.start()
        if "pass" in phases:
            for group in chained:
                for arrival, forward in group:
                    arrival.wait_recv()
                    forward.start()
        if "finish" in phases:
            for cp in recvs:
                cp.wait_recv()
            for cp in starts + [forward for group in chained for _, forward in group]:
                cp.wait_send()
            for cp in local:
                cp.wait()

    scratch = [pltpu.SemaphoreType.DMA((offs[n],)), pltpu.SemaphoreType.DMA((offs[n],)),
               pltpu.SemaphoreType.DMA((n,))]
    return srcs, out_shape, scratch, run


EXCHANGE_PHASES = ("start", "pass", "finish")


def exchange(name, jobs):
    srcs, out_shape, scratch, run = exchange_plan(jobs)
    n = len(srcs)

    def body(*refs):
        run(EXCHANGE_PHASES, refs[:n], refs[n:2 * n], *refs[2 * n:])

    any_spec = pl.BlockSpec(memory_space=pl.ANY)
    return pl.pallas_call(body, name=name, out_shape=out_shape, in_specs=[any_spec] * n, out_specs=[any_spec] * n,
                          scratch_shapes=scratch)(*srcs)


def host_exchange(body, n_in, n_out, grid, plan):
    if plan is None:
        return body, [], [], [], [], []
    srcs, out_shape, scratch, run = plan
    n = len(srcs)
    steps = 1
    for g in grid:
        steps *= g

    def hosted(*refs):
        ins, side_in = refs[:n_in], refs[n_in:n_in + n]
        outs, side_out = refs[n_in + n:n_in + n + n_out], refs[n_in + n + n_out:n_in + 2 * n + n_out]
        rest = refs[n_in + 2 * n + n_out:]
        own, sems = rest[:len(rest) - 3], rest[len(rest) - 3:]
        step = 0
        for d, g in enumerate(grid):
            step = step * g + pl.program_id(d)

        def at(when, phase):
            @pl.when(step == when)
            def _():
                run((phase,), side_in, side_out, *sems)

        at(0, "start")
        body(*ins, *outs, *own)
        at(steps * 13 // 16, "pass")
        at(steps - 1, "finish")

    any_spec = pl.BlockSpec(memory_space=pl.ANY)
    return hosted, [any_spec] * n, list(srcs), list(out_shape), [any_spec] * n, list(scratch)


def pair_sum(name, g, sib, core):
    _, r, c = g.shape
    tr = _pick(r, (256, 128, 64, 32, 16, 8))

    def body(core_ref, g_ref, s_ref, o_ref):
        o_ref[...] = (g_ref[...] + s_ref[...]).astype(o_ref.dtype)

    return pl.pallas_call(
        body,
        name=name,
        out_shape=jax.ShapeDtypeStruct((N_CHIPS, r, c), MXU),
        grid_spec=pltpu.PrefetchScalarGridSpec(
            num_scalar_prefetch=1,
            grid=(N_CHIPS, r // tr),
            in_specs=[
                pl.BlockSpec((None, None, tr, c), lambda j, i, core_ref: (j, core_ref[0], i, 0)),
                pl.BlockSpec((None, tr, c), lambda j, i, core_ref: (j, i, 0)),
            ],
            out_specs=pl.BlockSpec((None, tr, c), lambda j, i, core_ref: (j, i, 0)),
        ),
        compiler_params=_cparams(("parallel", "parallel")),
    )(core, g.reshape(N_CHIPS, 2, r, c), sib)


def adamw(name, parts, w, m, v):
    rows, cols = w.shape
    tr = _pick(rows, (128, 64, 32, 16, 8))

    nparts = parts.shape[0]

    def body(p_ref, w_ref, m_ref, v_ref, g_ref, d_ref, m2_ref, v2_ref):
        g = p_ref[0].astype(F32)
        for k in range(1, nparts):
            g = g + p_ref[k].astype(F32)
        m2 = ADAM_B1 * m_ref[...] + (1.0 - ADAM_B1) * g
        v2 = ADAM_B2 * v_ref[...] + (1.0 - ADAM_B2) * jnp.square(g)
        m_hat = m2 / (1.0 - ADAM_B1 ** ADAM_STEP)
        v_hat = v2 / (1.0 - ADAM_B2 ** ADAM_STEP)
        g_ref[...] = g
        d_ref[...] = -ADAM_LR * (m_hat / (jnp.sqrt(v_hat) + ADAM_EPS) + ADAM_WD * w_ref[...])
        m2_ref[...] = m2
        v2_ref[...] = v2

    spec = pl.BlockSpec((tr, cols), lambda i: (i, 0))
    return pl.pallas_call(
        body,
        name=name,
        out_shape=[jax.ShapeDtypeStruct((rows, cols), F32)] * 4,
        grid=(rows // tr,),
        in_specs=[pl.BlockSpec((nparts, tr, cols), lambda i: (0, i, 0)), spec, spec, spec],
        out_specs=[spec] * 4,
        compiler_params=_cparams(("parallel",)),
    )(parts, w, m, v)


WEIGHT_ORDER = (
    "norm_mix_pre", "w_in", "conv_w", "conv_b", "dt_bias", "a_log", "d_skip", "ssd_norm", "w_ssd_branch",
    "w_sb_branch", "w_gate", "b_gate", "w_out", "norm_mix_post", "norm_ffn_pre", "w_ff1", "w_ff2", "norm_ffn_post",
    "w_ple", "w_ple_gate", "norm_ple_post",
)


def kernel(x, p, norm_mix_pre, w_in, conv_w, conv_b, dt_bias, a_log, d_skip, ssd_norm, w_ssd_branch, w_sb_branch, w_gate, b_gate, w_out, norm_mix_post, norm_ffn_pre, w_ff1, w_ff2, norm_ffn_post, w_ple, w_ple_gate, norm_ple_post, loss_target, m_norm_mix_pre, m_w_in, m_conv_w, m_conv_b, m_dt_bias, m_a_log, m_d_skip, m_ssd_norm, m_w_ssd_branch, m_w_sb_branch, m_w_gate, m_b_gate, m_w_out, m_norm_mix_post, m_norm_ffn_pre, m_w_ff1, m_w_ff2, m_norm_ffn_post, m_w_ple, m_w_ple_gate, m_norm_ple_post, v_norm_mix_pre, v_w_in, v_conv_w, v_conv_b, v_dt_bias, v_a_log, v_d_skip, v_ssd_norm, v_w_ssd_branch, v_w_sb_branch, v_w_gate, v_b_gate, v_w_out, v_norm_mix_post, v_norm_ffn_pre, v_w_ff1, v_w_ff2, v_norm_ffn_post, v_w_ple, v_w_ple_gate, v_norm_ple_post):
    a = dict(locals())
    seq = x.shape[1]
    x2 = x.reshape(seq, D_MODEL)
    target = loss_target.reshape(seq, D_MODEL)
    pb = p.reshape(seq, PLE_DIM).astype(MXU)

    def shard(prefix, name, dtype):
        v = a[prefix + name][0].astype(dtype)
        return _pad_to(v, IN_PAD) if name == "w_in" else v

    w_in_slabs, conv_w_slabs = exchange(
        "gather_weights", [("gather2", shard("", "w_in", MXU)), ("all", a["conv_w"][0])])
    small = {n: a[n] for n, _ in SMALL}
    core = lax.axis_index("c").astype(jnp.int32).reshape(1)

    lossc, grad_x, parts, conv_slabs, g_acc = local_step(
        x2, pb, target, in_matrices(w_in_slabs), build_small(small, conv_w_slabs),
        {n: shard("", n, MXU) for n in LATE + ("w_gate",)}, core)
    loss = lax.psum(jnp.sum(lossc), ("x", "y", "c"))
    g_small = small_grads(g_acc, small)

    parts["conv_w"], small_parts = exchange(
        "small_grads", [("scatter", conv_slabs), ("all", pack_small(lambda n: g_small[n]))])

    leaves = {}
    for n, part in parts.items():
        res = adamw("adamw_" + n, part, shard("", n, F32), shard("m_", n, F32), shard("v_", n, F32))
        leaves[n] = [r[None, :, :IN_SHARD] if n == "w_in" else r[None] for r in res]
    res = adamw("adamw_small", small_parts, pack_small(lambda n: a[n]), pack_small(lambda n: a["m_" + n]),
                pack_small(lambda n: a["v_" + n]))
    for j, r in enumerate(res):
        for n, leaf in unpack_small(r).items():
            leaves.setdefault(n, [None] * 4)[j] = leaf
    outs = [loss, grad_x.reshape(x.shape)]
    for j in range(4):
        outs += [leaves[n][j] for n in WEIGHT_ORDER]
    return tuple(outs)
```

```python
import functools

import jax
import jax.numpy as jnp
from jax import lax
from jax.experimental import pallas as pl
from jax.experimental.pallas import tpu as pltpu

F32 = jnp.float32
MXU = jnp.bfloat16
VMEM_LIMIT = 56 * 1024 * 1024

D_MODEL = 1024
D_INNER = 2048
SSD_HEADS = 32
HEAD_DIM = 64
SSD_GROUPS = 8
D_STATE = 128
CONV_K = 4
CONV_DIM = 4096
CHUNK = 128
SB_WIDTH = 1024
D_FF = 4096
PLE_DIM = 256
RMS_EPS = 1e-6
SB_SCALE = HEAD_DIM ** -0.5
N_DEV = 8
LANES = 128

OFF_Z, OFF_XBC, OFF_Q, OFF_K, OFF_V = 0, 2048, 6144, 7168, 8192

ADAM_LR = 0.001
ADAM_B1 = 0.9
ADAM_B2 = 0.999
ADAM_EPS = 1e-08
ADAM_WD = 0.01
ADAM_STEP = 10


def _sig(x):
    return 0.5 * jnp.tanh(0.5 * x) + 0.5


def _softplus(x):
    return jnp.maximum(x, 0.0) + jnp.log(1.0 + jnp.exp(-jnp.abs(x)))


def _rms(x, w):
    return x * lax.rsqrt(jnp.mean(x * x, axis=-1, keepdims=True) + RMS_EPS) * w


def _dot(a, b):
    return jnp.dot(a, b, preferred_element_type=F32)


def _dot_nt(a, b):
    return lax.dot_general(a, b, (((1,), (1,)), ((), ())), preferred_element_type=F32)


def _dot_tn(a, b):
    return lax.dot_general(a, b, (((0,), (0,)), ((), ())), preferred_element_type=F32)


def _split3(x):
    x1 = x.astype(MXU)
    r = x - x1.astype(F32)
    x2 = r.astype(MXU)
    r = r - x2.astype(F32)
    return x1, x2, r.astype(MXU)


def _dot3_l(a, u, parts=3):
    m = a.shape[0]
    d = _dot(jnp.concatenate(_split3(a)[:parts], axis=0), u)
    if parts == 2:
        return d[m:] + d[:m]
    return (d[2 * m:] + d[m:2 * m]) + d[:m]


def _dot3_r(u, a):
    n = a.shape[1]
    d = _dot(u, jnp.concatenate(_split3(a), axis=1))
    return (d[:, 2 * n:] + d[:, n:2 * n]) + d[:, :n]


def _iota(shape, dim):
    return lax.broadcasted_iota(jnp.int32, shape, dim)


def _tri(n, cmp):
    r, c = _iota((n, n), 0), _iota((n, n), 1)
    return cmp(r, c).astype(F32).astype(MXU)


def _cparams(sem):
    return pltpu.CompilerParams(dimension_semantics=sem, vmem_limit_bytes=VMEM_LIMIT)


def _pick(n, cands):
    for c in cands:
        if n % c == 0:
            return c
    return n


def mm(name, a, b, mode, add=None, out_dtype=F32, b_slabs=False, out_slabs=False, epi=None, epi_args=(), extra=(),
       side=None):
    slab = None
    if b_slabs:
        slab = b.shape[2]
        bshape = (b.shape[1], N_DEV * slab)
    else:
        bshape = b.shape
    if mode == "nn":
        (M, K), (K2, N) = a.shape, bshape
    elif mode == "nt":
        (M, K), (N, K2) = a.shape, bshape
    else:
        (K, M), (K2, N) = a.shape, bshape
    assert K == K2, (name, a.shape, b.shape)
    tm = _pick(M, (1024, 512, 256, 128))
    tn = _pick(N, (1024, 512, 256, 128))
    tk = _pick(K, (1024, 512, 256, 128))
    if b_slabs and mode == "nn":
        tn = slab
    if b_slabs and mode == "nt":
        tk = slab
    if out_slabs:
        assert mode == "tn" and N % N_DEV == 0
        tn = N // N_DEV
    nk = K // tk

    def body(*refs):
        refs = list(refs)
        a_ref, b_ref = refs[:2]
        add_ref = refs[2] if add is not None else None
        n_in = 2 + (add is not None)
        epi_refs = refs[n_in:n_in + len(epi_args)]
        o_ref = refs[n_in + len(epi_args)]
        extra_refs = refs[n_in + len(epi_args) + 1:-1]
        acc = refs[-1]
        k = pl.program_id(2)

        @pl.when(k == 0)
        def _():
            acc[...] = jnp.zeros_like(acc) if add is None else add_ref[...]

        av, bv = a_ref[...], b_ref[...]
        if mode == "nn":
            acc[...] += _dot(av, bv)
        elif mode == "nt":
            acc[...] += _dot_nt(av, bv)
        else:
            acc[...] += _dot_tn(av, bv)

        @pl.when(k == nk - 1)
        def _():
            res = acc[...]
            main = res if epi is None else epi(res, *[r[...] for r in epi_refs])
            o_ref[...] = main.astype(o_ref.dtype)
            for r, (_, fn) in zip(extra_refs, extra):
                r[...] = fn(res).astype(r.dtype)

    if mode == "nn":
        a_spec = pl.BlockSpec((tm, tk), lambda i, j, k: (i, k))
        b_spec = pl.BlockSpec((tk, tn), lambda i, j, k: (k, j))
    elif mode == "nt":
        a_spec = pl.BlockSpec((tm, tk), lambda i, j, k: (i, k))
        b_spec = pl.BlockSpec((tn, tk), lambda i, j, k: (j, k))
    else:
        a_spec = pl.BlockSpec((tk, tm), lambda i, j, k: (k, i))
        b_spec = pl.BlockSpec((tk, tn), lambda i, j, k: (k, j))
    if b_slabs and mode == "nn":
        b_spec = pl.BlockSpec((None, tk, tn), lambda i, j, k: (j, k, 0))
    if b_slabs and mode == "nt":
        b_spec = pl.BlockSpec((None, tn, tk), lambda i, j, k: (k, j, 0))
    o_spec = pl.BlockSpec((tm, tn), lambda i, j, k: (i, j))
    in_specs, args = [a_spec, b_spec], [a, b]
    if add is not None:
        in_specs.append(o_spec)
        args.append(add)
    for e in epi_args:
        in_specs.append(o_spec)
        args.append(e)
    out_sds = jax.ShapeDtypeStruct((M, N), out_dtype)
    if out_slabs:
        o_spec = pl.BlockSpec((None, tm, tn), lambda i, j, k: (j, i, 0))
        out_sds = jax.ShapeDtypeStruct((N_DEV, M, tn), out_dtype)
    grid = (M // tm, N // tn, nk)
    body, s_in, s_args, s_shape, s_out, s_scr = host_exchange(body, len(args), 1 + len(extra), grid, side)
    res = pl.pallas_call(
        body,
        name=name,
        out_shape=[out_sds] + [jax.ShapeDtypeStruct((M, N), dt) for dt, _ in extra] + s_shape,
        grid=grid,
        in_specs=in_specs + s_in,
        out_specs=[o_spec] * (1 + len(extra)) + s_out,
        scratch_shapes=[pltpu.VMEM((tm, tn), F32)] + s_scr,
        compiler_params=_cparams(("parallel", "parallel", "arbitrary") if side is None else ("arbitrary",) * 3),
    )(*args, *s_args)
    return res if extra or side is not None else res[0]


def rowwise(name, fn, rows, bcast, outs, accs=(), tr=512, ncb=1, side=None, groups=1):
    S = rows[0][0].shape[0]
    tr = min(tr, S)
    nrb = S // tr
    G = groups
    in_specs, args, in_w = [], [], []
    for arr, off, w in rows:
        assert off % (w * G) == 0 and arr.shape[0] == S
        in_specs.append(pl.BlockSpec((tr, w * G), lambda j, i, ob=off // (w * G): (i, ob + j)))
        args.append(arr)
        in_w.append(w)
    for arr, off, w in bcast:
        assert off % (w * G) == 0
        in_specs.append(pl.BlockSpec((arr.shape[0], w * G), lambda j, i, ob=off // (w * G): (0, ob + j)))
        args.append(arr)
        in_w.append(w)
    out_shape, out_specs, out_w = [], [], []
    for tw, w, dt in outs:
        out_shape.append(jax.ShapeDtypeStruct((S, tw), dt))
        out_specs.append(pl.BlockSpec((tr, w * G), lambda j, i: (i, j)))
        out_w.append(w)
    for tw, w in accs:
        out_shape.append(jax.ShapeDtypeStruct((1, tw), F32))
        out_specs.append(pl.BlockSpec((1, w * G), lambda j, i: (0, j)))
        out_w.append(w)
    nin, nout = len(args), len(outs)

    def body(*refs):
        i = pl.program_id(1)
        for g in range(G):
            cut = lambda w: slice(g * w, (g + 1) * w)
            res = fn(*[r[:, cut(w)] for r, w in zip(refs[:nin], in_w)])
            for k, (r, v) in enumerate(zip(refs[nin:], res)):
                cols = cut(out_w[k])
                if k < nout:
                    r[:, cols] = v.astype(r.dtype)
                    continue

                @pl.when(i == 0)
                def _(r=r, v=v, cols=cols):
                    r[:, cols] = v

                @pl.when(i > 0)
                def _(r=r, v=v, cols=cols):
                    r[:, cols] += v

    body, s_in, s_args, s_shape, s_out, s_scr = host_exchange(body, nin, len(out_shape), (ncb, nrb), side)
    res = pl.pallas_call(
        body,
        name=name,
        out_shape=out_shape + s_shape,
        grid=(ncb, nrb),
        in_specs=in_specs + s_in,
        out_specs=out_specs + s_out,
        scratch_shapes=s_scr,
        compiler_params=_cparams(("parallel" if side is None else "arbitrary", "arbitrary")),
    )(*args, *s_args)
    return res


CONV_TC = 128


CONV_R = 256
HALO = 8


def _conv_pre(e, w, b):
    shifted = [pltpu.roll(e, s, 0) for s in (1, 2, 3)]
    pre = b + w[3:4, :] * e
    for s in (1, 2, 3):
        pre = pre + w[3 - s:4 - s, :] * shifted[s - 1]
    return pre, shifted


def conv_fwd(projmain, conv_w, conv_b):
    S = projmain.shape[0]
    tc = CONV_TC

    def body(u_ref, w_ref, b_ref, o_ref):
        u, w = u_ref[...], w_ref[...]
        row = _iota(u.shape, 0)
        pre = b_ref[...] + w[3:4, :] * u
        for s in (1, 2, 3):
            pre = pre + w[3 - s:4 - s, :] * jnp.where(row >= s, pltpu.roll(u, s, 0), 0.0)
        o_ref[...] = pre * _sig(pre)

    return pl.pallas_call(
        body,
        name="conv_fwd",
        out_shape=jax.ShapeDtypeStruct((S, CONV_DIM), F32),
        grid=(CONV_DIM // tc,),
        in_specs=[
            pl.BlockSpec((S, tc), lambda j: (0, OFF_XBC // tc + j)),
            pl.BlockSpec((CONV_K, tc), lambda j: (0, j)),
            pl.BlockSpec((1, tc), lambda j: (0, j)),
        ],
        out_specs=pl.BlockSpec((S, tc), lambda j: (0, j)),
        compiler_params=_cparams(("parallel",)),
    )(projmain, conv_w, conv_b)


def conv_bwd(dxs, dxs_skip, d_b, d_c, projmain, conv_w, conv_b):
    S = projmain.shape[0]
    tc = CONV_TC
    n_xs, n_b = D_INNER // tc, SSD_GROUPS * D_STATE // tc

    def body(dx_ref, dskip_ref, dbm_ref, dcm_ref, u_ref, w_ref, b_ref, du_ref, dw_ref, db_ref, u_pad, d_pad):
        j = pl.program_id(0)
        zeros = jnp.zeros((HALO, tc), F32)
        for pad in (u_pad, d_pad):
            pad[0:HALO, :] = zeros
            pad[HALO + S:2 * HALO + S, :] = zeros
        u_pad[HALO:HALO + S, :] = u_ref[...]
        d_pad[HALO:HALO + S, :] = jnp.where(j < n_xs, dx_ref[...] + dskip_ref[...],
                                            jnp.where(j < n_xs + n_b, dbm_ref[...], dcm_ref[...]))
        w, b = w_ref[...], b_ref[...]
        n = CONV_R + 2 * HALO
        keep = slice(HALO, HALO + CONV_R)

        def chunk(c, sums):
            r0 = pl.multiple_of(c * CONV_R, CONV_R)
            e = u_pad[pl.ds(r0, n), :]
            pre, shifted = _conv_pre(e, w, b)
            sg = _sig(pre)
            dpre = d_pad[pl.ds(r0, n), :] * (sg * (1.0 + pre * (1.0 - sg)))
            du = w[3:4, :] * dpre
            for s in (1, 2, 3):
                du = du + w[3 - s:4 - s, :] * pltpu.roll(dpre, n - s, 0)
            du_ref[pl.ds(r0, CONV_R), :] = du[keep].astype(du_ref.dtype)
            dk = dpre[keep]
            taps = [shifted[2], shifted[1], shifted[0], e]
            return tuple(acc + jnp.sum(dk * t[keep], axis=0, keepdims=True) for acc, t in zip(sums[:4], taps)) + (
                sums[4] + jnp.sum(dk, axis=0, keepdims=True),)

        zero = jnp.zeros((1, tc), F32)
        sums = lax.fori_loop(0, S // CONV_R, chunk, (zero,) * 5)
        for k in range(CONV_K):
            dw_ref[k:k + 1, :] = sums[k]
        db_ref[...] = sums[4]

    return pl.pallas_call(
        body,
        name="conv_bwd",
        out_shape=[
            jax.ShapeDtypeStruct((S, CONV_DIM), MXU),
            jax.ShapeDtypeStruct((CONV_K, CONV_DIM), F32),
            jax.ShapeDtypeStruct((1, CONV_DIM), F32),
        ],
        grid=(CONV_DIM // tc,),
        in_specs=[
            pl.BlockSpec((S, tc), lambda j: (0, jnp.minimum(j, n_xs - 1))),
            pl.BlockSpec((S, tc), lambda j: (0, jnp.minimum(j, n_xs - 1))),
            pl.BlockSpec((S, tc), lambda j: (0, jnp.clip(j - n_xs, 0, n_b - 1))),
            pl.BlockSpec((S, tc), lambda j: (0, jnp.clip(j - n_xs - n_b, 0, n_b - 1))),
            pl.BlockSpec((S, tc), lambda j: (0, OFF_XBC // tc + j)),
            pl.BlockSpec((CONV_K, tc), lambda j: (0, j)),
            pl.BlockSpec((1, tc), lambda j: (0, j)),
        ],
        out_specs=[
            pl.BlockSpec((S, tc), lambda j: (0, j)),
            pl.BlockSpec((CONV_K, tc), lambda j: (0, j)),
            pl.BlockSpec((1, tc), lambda j: (0, j)),
        ],
        scratch_shapes=[pltpu.VMEM((S + 2 * HALO, tc), F32)] * 2,
        compiler_params=_cparams(("arbitrary",)),
    )(dxs, dxs_skip, d_b, d_c, projmain, conv_w, conv_b)


def _head_expand():
    r, j = _iota((LANES, D_INNER), 0), _iota((LANES, D_INNER), 1)
    return ((j >= r * HEAD_DIM) & (j < r * HEAD_DIM + HEAD_DIM)).astype(F32).astype(MXU)


def _head_reduce():
    j, r = _iota((D_INNER, LANES), 0), _iota((D_INNER, LANES), 1)
    return ((j >= r * HEAD_DIM) & (j < r * HEAD_DIM + HEAD_DIM)).astype(F32).astype(MXU)


def ssd_prep(dtraw, dt_bias_pad, a_exp):
    S = dtraw.shape[0]

    def body(dtr_ref, bias_ref, a_ref, dte_ref, cse_ref):
        dt = _softplus(dtr_ref[...] + bias_ref[...])
        dte = _dot3_l(dt, _head_expand())
        dte_ref[...] = dte
        incl = _tri(CHUNK, lambda r, c: r >= c)
        cse_ref[...] = _dot3_r(incl, dte * a_ref[...])

    return pl.pallas_call(
        body,
        name="ssd_prep",
        out_shape=[jax.ShapeDtypeStruct((S, D_INNER), F32)] * 2,
        grid=(S // CHUNK,),
        in_specs=[
            pl.BlockSpec((CHUNK, LANES), lambda c: (c, 0)),
            pl.BlockSpec((1, LANES), lambda c: (0, 0)),
            pl.BlockSpec((1, D_INNER), lambda c: (0, 0)),
        ],
        out_specs=[pl.BlockSpec((CHUNK, D_INNER), lambda c: (c, 0))] * 2,
        compiler_params=_cparams(("parallel",)),
    )(dtraw, dt_bias_pad, a_exp)


GW = 4 * HEAD_DIM
SSD_GPS = 8
SSD_GPS_BWD = 8


def ssd_fwd(xbc_act, dte, cse, side=None):
    S = xbc_act.shape[0]
    nc = S // CHUNK

    def body(xs_ref, b_ref, c_ref, dte_ref, cse_ref, y_ref, st_ref, s_scr):
        c = pl.program_id(1)

        @pl.when(c == 0)
        def _():
            s_scr[...] = jnp.zeros_like(s_scr)

        for gg in range(SSD_GPS):
            ch, st = slice(GW * gg, GW * (gg + 1)), slice(D_STATE * gg, D_STATE * (gg + 1))
            s_in = s_scr[:, ch]
            st_ref[0, :, ch] = s_in
            cs = cse_ref[:, ch]
            xd = xs_ref[:, ch] * dte_ref[:, ch]
            btb = b_ref[:, st].T.astype(MXU)
            cs_last = cs[CHUNK - 1:CHUNK, :]
            gy = _dot(c_ref[:, st].astype(MXU), jnp.concatenate([btb, s_in.astype(MXU)], axis=1))
            g, y = gy[:, :CHUNK], gy[:, CHUNK:] * jnp.exp(cs)
            cs_t = cs.T
            row, col = _iota((CHUNK, CHUNK), 0), _iota((CHUNK, CHUNK), 1)
            ms = []
            for h in range(4):
                lo = HEAD_DIM * h
                lam = jnp.where(row >= col, jnp.exp(cs[:, lo:lo + 1] - cs_t[lo:lo + 1, :]), 0.0)
                ms.append((g * lam).astype(MXU))
            yd = _dot(jnp.concatenate(ms, axis=0), xd.astype(MXU))
            lane = _iota((CHUNK, GW), 1)
            for h in range(4):
                y = y + jnp.where((lane >= HEAD_DIM * h) & (lane < HEAD_DIM * (h + 1)),
                                  yd[CHUNK * h:CHUNK * (h + 1)], 0.0)
            y_ref[:, ch] = y
            w = (xd * jnp.exp(cs_last - cs)).astype(MXU)
            s_scr[:, ch] = jnp.exp(cs_last) * s_in + _dot(btb, w)

    grid = (SSD_GROUPS // SSD_GPS, nc)
    gw, gs = GW * SSD_GPS, D_STATE * SSD_GPS
    body, s_in, s_args, s_shape, s_out, s_scr = host_exchange(body, 5, 2, grid, side)
    return pl.pallas_call(
        body,
        name="ssd_fwd",
        out_shape=[
            jax.ShapeDtypeStruct((S, D_INNER), F32),
            jax.ShapeDtypeStruct((nc, D_STATE, D_INNER), F32),
        ] + s_shape,
        grid=grid,
        in_specs=[
            pl.BlockSpec((CHUNK, gw), lambda g, c: (c, g)),
            pl.BlockSpec((CHUNK, gs), lambda g, c: (c, D_INNER // gs + g)),
            pl.BlockSpec((CHUNK, gs), lambda g, c: (c, (D_INNER + SSD_GROUPS * D_STATE) // gs + g)),
            pl.BlockSpec((CHUNK, gw), lambda g, c: (c, g)),
            pl.BlockSpec((CHUNK, gw), lambda g, c: (c, g)),
        ] + s_in,
        out_specs=[
            pl.BlockSpec((CHUNK, gw), lambda g, c: (c, g)),
            pl.BlockSpec((1, D_STATE, gw), lambda g, c: (c, 0, g)),
        ] + s_out,
        scratch_shapes=[pltpu.VMEM((D_STATE, gw), F32)] + s_scr,
        compiler_params=_cparams(("parallel" if side is None else "arbitrary", "arbitrary")),
    )(xbc_act, xbc_act, xbc_act, dte, cse, *s_args)


def ssd_bwd(dy, xbc_act, dte, cse, states, a_exp, side=None):
    S = xbc_act.shape[0]
    nc = S // CHUNK

    def body(*refs):
        for gg in range(SSD_GPS_BWD):
            ch, st = slice(GW * gg, GW * (gg + 1)), slice(D_STATE * gg, D_STATE * (gg + 1))
            cut = {GW * SSD_GPS_BWD: ch, D_STATE * SSD_GPS_BWD: st}
            one_group(*[r.at[(slice(None),) * (len(r.shape) - 1) + (cut[r.shape[-1]],)] for r in refs])

    def one_group(dy_ref, xs_ref, b_ref, c_ref, dte_ref, cse_ref, sin_ref, sout_ref, a_ref,
                  dxs_ref, db_ref, dc_ref, ddt_ref, dal_ref, ds_scr):
        j = pl.program_id(1)

        @pl.when(j == 0)
        def _():
            ds_scr[...] = jnp.zeros_like(ds_scr)
            dal_ref[...] = jnp.zeros_like(dal_ref)

        ds_out = ds_scr[...]
        dyv, xs = dy_ref[...], xs_ref[...]
        dt, cs = dte_ref[...], cse_ref[...]
        s_in = sin_ref[0]
        bm, cm = b_ref[...], c_ref[...]
        bb, cb = bm.astype(MXU), cm.astype(MXU)
        btb, ctb = bm.T.astype(MXU), cm.T.astype(MXU)
        dsb, sib = ds_out.astype(MXU), s_in.astype(MXU)
        xd = xs * dt
        ecs = jnp.exp(cs)
        cs_last = cs[CHUNK - 1:CHUNK, :]
        eend = jnp.exp(cs_last - cs)
        gy = _dot(cb, jnp.concatenate([btb, sib], axis=1))
        g, yoff = gy[:, :CHUNK], gy[:, CHUNK:] * ecs
        gd = _dot(bb, jnp.concatenate([ctb, dsb], axis=1))
        g_t, dxd_off = gd[:, :CHUNK], gd[:, CHUNK:] * eend
        cs_t = cs.T
        row, col = _iota((CHUNK, CHUNK), 0), _iota((CHUNK, CHUNK), 1)
        lane = _iota((CHUNK, GW), 1)
        heads = [(lane >= HEAD_DIM * h) & (lane < HEAD_DIM * (h + 1)) for h in range(4)]
        dyb, xdb = dyv.astype(MXU), xd.astype(MXU)
        dm_all = _dot_nt(jnp.concatenate([jnp.where(hm, dyv, 0.0) for hm in heads], axis=0).astype(MXU), xdb)
        dmt_all = _dot_nt(jnp.concatenate([jnp.where(hm, xd, 0.0) for hm in heads], axis=0).astype(MXU), dyb)
        lams, m_ts = [], []
        for h in range(4):
            lo = HEAD_DIM * h
            cs_col, cs_row = cs[:, lo:lo + 1], cs_t[lo:lo + 1, :]
            lams.append(jnp.where(row >= col, jnp.exp(cs_col - cs_row), 0.0))
            m_ts.append(g_t * jnp.where(col >= row, jnp.exp(cs_row - cs_col), 0.0))
        acc_all = _dot(jnp.concatenate(m_ts, axis=0).astype(MXU), dyb)
        dxd = dxd_off
        dg = jnp.zeros((CHUNK, CHUNK), F32)
        dcs = dyv * yoff - xd * dxd_off
        for h in range(4):
            blk = slice(CHUNK * h, CHUNK * (h + 1))
            dm, dm_t = dm_all[blk], dmt_all[blk]
            dxd = dxd + jnp.where(heads[h], acc_all[blk], 0.0)
            dg = dg + dm * lams[h]
            wdiff = (jnp.sum(dm * (g * lams[h]), axis=1, keepdims=True)
                     - jnp.sum(dm_t * m_ts[h], axis=1, keepdims=True))
            dcs = dcs + jnp.where(lane == HEAD_DIM * h, wdiff, 0.0)
        dye = (dyv * ecs).astype(MXU)
        dc_ref[...] = _dot(dg.astype(MXU), bb) + _dot_nt(dye, sib)
        db_ref[...] = _dot(dg.T.astype(MXU), cb) + _dot_nt((xd * eend).astype(MXU), dsb)
        ds_scr[...] = jnp.exp(cs_last) * ds_out + _dot(ctb, dye)
        last = jnp.sum(ds_out * sout_ref[0], axis=0, keepdims=True)
        rows = _iota((CHUNK, GW), 0)
        dcs = dcs + jnp.where(rows == CHUNK - 1, last, 0.0)
        dda = _dot3_r(_tri(CHUNK, lambda r, c: c >= r), dcs)
        ddt_ref[...] = a_ref[...] * dda + dxd * xs
        dal_ref[...] += jnp.sum(dt * dda, axis=0, keepdims=True)
        dxs_ref[...] = dxd * dt

    rc = lambda g, j: (nc - 1 - j, g)
    grid = (SSD_GROUPS // SSD_GPS_BWD, nc)
    gw, gs = GW * SSD_GPS_BWD, D_STATE * SSD_GPS_BWD
    body, s_in, s_args, s_shape, s_out, s_scr = host_exchange(body, 9, 5, grid, side)
    return pl.pallas_call(
        body,
        name="ssd_bwd",
        out_shape=[
            jax.ShapeDtypeStruct((S, D_INNER), F32),
            jax.ShapeDtypeStruct((S, SSD_GROUPS * D_STATE), F32),
            jax.ShapeDtypeStruct((S, SSD_GROUPS * D_STATE), F32),
            jax.ShapeDtypeStruct((S, D_INNER), F32),
            jax.ShapeDtypeStruct((1, D_INNER), F32),
        ] + s_shape,
        grid=grid,
        in_specs=[
            pl.BlockSpec((CHUNK, gw), rc),
            pl.BlockSpec((CHUNK, gw), rc),
            pl.BlockSpec((CHUNK, gs), lambda g, j: (nc - 1 - j, D_INNER // gs + g)),
            pl.BlockSpec((CHUNK, gs), lambda g, j: (nc - 1 - j, (D_INNER + SSD_GROUPS * D_STATE) // gs + g)),
            pl.BlockSpec((CHUNK, gw), rc),
            pl.BlockSpec((CHUNK, gw), rc),
            pl.BlockSpec((1, D_STATE, gw), lambda g, j: (nc - 1 - j, 0, g)),
            pl.BlockSpec((1, D_STATE, gw), lambda g, j: (jnp.minimum(nc - j, nc - 1), 0, g)),
            pl.BlockSpec((1, gw), lambda g, j: (0, g)),
        ] + s_in,
        out_specs=[
            pl.BlockSpec((CHUNK, gw), rc),
            pl.BlockSpec((CHUNK, gs), rc),
            pl.BlockSpec((CHUNK, gs), rc),
            pl.BlockSpec((CHUNK, gw), rc),
            pl.BlockSpec((1, gw), lambda g, j: (0, g)),
        ] + s_out,
        scratch_shapes=[pltpu.VMEM((D_STATE, gw), F32)] + s_scr,
        compiler_params=_cparams(("parallel" if side is None else "arbitrary", "arbitrary")),
    )(dy, xbc_act, xbc_act, xbc_act, dte, cse, states, states, a_exp, *s_args)


SB_T = 256
SB_DROP = 104.0
SB_PAIRS = 1
SB_PARTS = 2
SB_W = SB_PAIRS * LANES
SB_LANES = [slice(LANES * p, LANES * (p + 1)) for p in range(SB_PAIRS)]


def _sb_scores(qm, k_ref, ks, rowi, coli, diag):
    kblk = k_ref[pl.ds(ks, SB_T), :].astype(MXU)
    z = _dot_nt(qm, kblk) * SB_SCALE
    sp = _softplus(z)
    if not diag:
        return kblk, z, None, sp, sp
    mask = (ks + coli) < rowi
    return kblk, z, mask, sp, jnp.where(mask, sp, 0.0)


def _sb_stack(v):
    lane = _iota(v.shape, 1)
    return jnp.concatenate([jnp.where(lane < HEAD_DIM, v, 0.0), jnp.where(lane >= HEAD_DIM, v, 0.0)], axis=0)


def _sb_unstack(v):
    lane = _iota((SB_T, LANES), 1)
    return jnp.where(lane < HEAD_DIM, v[:SB_T], v[SB_T:])


def _sb_rows(qb):
    r = _iota((2 * SB_T, SB_T), 0)
    return qb * SB_T + jnp.where(r >= SB_T, r - SB_T, r), _iota((2 * SB_T, SB_T), 1)


def sb_fwd(projmain, side=None):
    S = projmain.shape[0]
    nq = S // SB_T

    def body(q_ref, k_ref, v_ref, o_ref, t_ref, n_ref):
        hp, qb = pl.program_id(0), pl.program_id(1)
        qsts = [_sb_stack(q_ref[:, sl]).astype(MXU) for sl in SB_LANES]
        rowi, coli = _sb_rows(qb)
        u_after = _tri(SB_T, lambda r, c: r > c)

        def cond(carry):
            i, rmin, _, _ = carry
            return (i <= qb) & (rmin < SB_DROP)

        def kstep(carry, diag=False):
            i, _, rs, accs = carry
            ks = pl.multiple_of((qb - i) * SB_T, SB_T)
            rs, accs = list(rs), list(accs)
            for p, sl in enumerate(SB_LANES):
                _, z, mask, sp, spm = _sb_scores(qsts[p], k_ref.at[:, sl], ks, rowi, coli, diag)
                vblk = v_ref[pl.ds(ks, SB_T), sl].astype(MXU)
                a = jnp.exp(z - sp - _dot3_l(spm, u_after, SB_PARTS) - rs[p])
                if diag:
                    a = jnp.where(mask, a, 0.0)
                accs[p] = accs[p] + _dot(a.astype(MXU), vblk)
                rs[p] = rs[p] + jnp.sum(spm, axis=1, keepdims=True)
            return i + 1, jnp.min(functools.reduce(jnp.minimum, rs)), rs, accs

        first = kstep((jnp.int32(0), jnp.float32(0.0), [jnp.zeros((2 * SB_T, 1), F32)] * SB_PAIRS,
                       [jnp.zeros((2 * SB_T, LANES), F32)] * SB_PAIRS), diag=True)
        n, _, rs, accs = lax.while_loop(cond, kstep, first)
        for p, sl in enumerate(SB_LANES):
            o_ref[:, sl] = _sb_unstack(accs[p]).astype(o_ref.dtype)
            t_ref[:, sl] = _sb_unstack(jnp.broadcast_to(rs[p], (2 * SB_T, LANES)))
        n_ref[hp, qb] = n

    grid = (SB_WIDTH // SB_W, nq)
    body, s_in, s_args, s_shape, s_out, s_scr = host_exchange(body, 3, 3, grid, side)
    return pl.pallas_call(
        body,
        name="sb_fwd",
        out_shape=[jax.ShapeDtypeStruct((S, SB_WIDTH), MXU), jax.ShapeDtypeStruct((S, SB_WIDTH), F32),
                   jax.ShapeDtypeStruct((SB_WIDTH // SB_W, nq), jnp.int32)] + s_shape,
        grid=grid,
        in_specs=[
            pl.BlockSpec((SB_T, SB_W), lambda h, i: (i, OFF_Q // SB_W + h)),
            pl.BlockSpec((S, SB_W), lambda h, i: (0, OFF_K // SB_W + h)),
            pl.BlockSpec((S, SB_W), lambda h, i: (0, OFF_V // SB_W + h)),
        ] + s_in,
        out_specs=[pl.BlockSpec((SB_T, SB_W), lambda h, i: (i, h))] * 2
        + [pl.BlockSpec(memory_space=pltpu.SMEM)] + s_out,
        scratch_shapes=s_scr,
        compiler_params=_cparams(("arbitrary", "arbitrary")),
    )(projmain, projmain, projmain, *s_args)


def sb_bwd(projmain, do, t_exp, nblk, side=None):
    S = projmain.shape[0]
    nq = S // SB_T

    def body(n_ref, q_ref, k_ref, v_ref, do_ref, t_ref, dq_ref, dk_ref, dv_ref):
        hp, qb = pl.program_id(0), pl.program_id(1)

        @pl.when(qb == 0)
        def _():
            dk_ref[...] = jnp.zeros_like(dk_ref)
            dv_ref[...] = jnp.zeros_like(dv_ref)

        lane = _iota((SB_T, LANES), 1)
        qsts, dosts, tots = [], [], []
        for sl in SB_LANES:
            qsts.append(_sb_stack(q_ref[:, sl]).astype(MXU))
            dosts.append(_sb_stack(do_ref[:, sl]).astype(MXU))
            tv = t_ref[:, sl]
            tots.append(jnp.concatenate(
                [jnp.sum(jnp.where(lane == HEAD_DIM * hh, tv, 0.0), axis=1, keepdims=True) for hh in range(2)],
                axis=0))
        rowi, coli = _sb_rows(qb)
        u_upto = _tri(SB_T, lambda r, c: r <= c)
        u_before = _tri(SB_T, lambda r, c: r < c)
        kept = jnp.clip(n_ref[hp, qb], 1, qb + 1)

        def kstep(kb, carry, diag=False):
            ks = pl.multiple_of(kb * SB_T, SB_T)
            out = []
            for p, sl in enumerate(SB_LANES):
                psp, pg, dq = carry[p]
                kblk, z, mask, sp, spm = _sb_scores(qsts[p], k_ref.at[:, sl], ks, rowi, coli, diag)
                vblk = v_ref[pl.ds(ks, SB_T), sl].astype(MXU)
                after = tots[p] - (psp + _dot3_l(spm, u_upto, SB_PARTS))
                a = jnp.exp(z - sp - after)
                if diag:
                    a = jnp.where(mask, a, 0.0)
                gm = _dot_nt(dosts[p], vblk) * a
                before = pg + _dot3_l(gm, u_before, SB_PARTS)
                sg = jnp.exp(z - sp)
                dz = (gm * (1.0 - sg) - sg * before) * SB_SCALE
                if diag:
                    dz = jnp.where(mask, dz, 0.0)
                dzb = dz.astype(MXU)
                dk_ref[pl.ds(ks, SB_T), sl] += _dot_tn(dzb, qsts[p])
                dv_ref[pl.ds(ks, SB_T), sl] += _dot_tn(a.astype(MXU), dosts[p])
                out.append((psp + jnp.sum(spm, axis=1, keepdims=True),
                            pg + jnp.sum(gm, axis=1, keepdims=True), dq + _dot(dzb, kblk)))
            return tuple(out)

        zero1 = jnp.zeros((2 * SB_T, 1), F32)
        state = lax.fori_loop(qb + 1 - kept, qb, kstep,
                              ((zero1, zero1, jnp.zeros((2 * SB_T, LANES), F32)),) * SB_PAIRS)
        state = kstep(qb, state, diag=True)
        for p, sl in enumerate(SB_LANES):
            dq_ref[:, sl] = _sb_unstack(state[p][2]).astype(dq_ref.dtype)

    grid = (SB_WIDTH // SB_W, nq)
    body, s_in, s_args, s_shape, s_out, s_scr = host_exchange(body, 6, 3, grid, side)
    return pl.pallas_call(
        body,
        name="sb_bwd",
        out_shape=[
            jax.ShapeDtypeStruct((S, SB_WIDTH), MXU),
            jax.ShapeDtypeStruct((S, SB_WIDTH), F32),
            jax.ShapeDtypeStruct((S, SB_WIDTH), F32),
        ] + s_shape,
        grid=grid,
        in_specs=[
            pl.BlockSpec(memory_space=pltpu.SMEM),
            pl.BlockSpec((SB_T, SB_W), lambda h, i: (i, OFF_Q // SB_W + h)),
            pl.BlockSpec((S, SB_W), lambda h, i: (0, OFF_K // SB_W + h)),
            pl.BlockSpec((S, SB_W), lambda h, i: (0, OFF_V // SB_W + h)),
            pl.BlockSpec((SB_T, SB_W), lambda h, i: (i, h)),
            pl.BlockSpec((SB_T, SB_W), lambda h, i: (i, h)),
        ] + s_in,
        out_specs=[
            pl.BlockSpec((SB_T, SB_W), lambda h, i: (i, h)),
            pl.BlockSpec((S, SB_W), lambda h, i: (0, h)),
            pl.BlockSpec((S, SB_W), lambda h, i: (0, h)),
        ] + s_out,
        scratch_shapes=s_scr,
        compiler_params=_cparams(("arbitrary", "arbitrary")),
    )(nblk, projmain, projmain, projmain, do, t_exp, *s_args)


def local_step(x, pb, target, W, P, late, core):
    D = D_MODEL
    full = lambda a, w=D: (a, 0, w)

    (n1b,) = rowwise("norm_pre", lambda xv, g: (_rms(xv, g),), [full(x)], [full(P["norm_mix_pre"])],
                     [(D, D, MXU)])
    projmain, w_gate = mm("in_proj", n1b, W["main"], "nn", side=exchange_plan([("gather2", late["w_gate"])]))
    W = {**W, "gate": w_gate}
    gate_pre = mm("gate_proj", n1b, W["gate"], "nn", b_slabs=True)
    dtraw = mm("dt_proj", n1b, W["dt"], "nn")
    xbc_act = conv_fwd(projmain, P["conv_w"], P["conv_b"])
    dte, cse = ssd_prep(dtraw, P["dt_bias_pad"], P["a_exp"])
    y_scan, states, *got = ssd_fwd(xbc_act, dte, cse,
                                   side=exchange_plan([("gather2", late[n]) for n in LATE[:LATE_SPLIT]]))
    W = {**W, **late_matrices(dict(zip(LATE[:LATE_SPLIT], got)))}

    def f_gate(ysc, xs, z, dsk, nw):
        return _rms((ysc + xs * dsk) * (z * _sig(z)), nw)

    (y_ssd_b,) = rowwise("ssd_gate", lambda *a: (f_gate(*a),),
                         [(y_scan, 0, GW), (xbc_act, 0, GW), (projmain, OFF_Z, GW)],
                         [(P["dsk_exp"], 0, GW), (P["ssd_norm"], 0, GW)], [(D_INNER, GW, MXU)], tr=256,
                         groups=SSD_GROUPS)
    y_sb_b, t_exp, sb_kept, *got = sb_fwd(
        projmain, side=exchange_plan([("gather2", late[n]) for n in LATE[LATE_SPLIT:]]))
    W = {**W, **late_matrices(dict(zip(LATE[LATE_SPLIT:], got)))}
    u1 = mm("ssd_branch", y_ssd_b, W["ssd"], "nn")
    u2 = mm("sb_branch", y_sb_b, W["sb"], "nn")

    def f_merge(a1, a2, g1, g2, b1, b2):
        return (_sig(g1 + b1) * a1 + _sig(g2 + b2) * a2,)

    gate_rows = [(gate_pre, 0, D), (gate_pre, D, D)]
    gate_bias = [(P["b_gate"], 0, D), (P["b_gate"], D, D)]
    (merged_b,) = rowwise("merge", f_merge, [full(u1), full(u2)] + gate_rows, gate_bias, [(D, D, MXU)])
    mo = mm("out_proj", merged_b, W["out"], "nn")

    def f_mid(xv, m, gpost, gffn):
        h1 = xv + _rms(m, gpost)
        return h1, _rms(h1, gffn)

    h1, n2b = rowwise("mix_post", f_mid, [full(x), full(mo)],
                      [full(P["norm_mix_post"]), full(P["norm_ffn_pre"])], [(D, D, F32), (D, D, MXU)])
    a_ff, rb = mm("ff1", n2b, W["ff1"], "nn", b_slabs=True,
                  extra=[(MXU, lambda acc: jnp.square(jnp.maximum(acc, 0.0)))])
    ff = mm("ff2", rb, W["ff2"], "nn")

    def f_ffn_post(h, f, g):
        h2 = h + _rms(f, g)
        return h2, h2

    h2, h2b = rowwise("ffn_post", f_ffn_post, [full(h1), full(ff)], [full(P["norm_ffn_post"])],
                      [(D, D, F32), (D, D, MXU)])
    pgp = mm("ple_gate", h2b, W["pg"], "nn")
    pe = mm("ple_proj", pb, W["ple"], "nn", b_slabs=True)

    def f_ple(h2v, gp, pev, tgt, g):
        f = lambda h, a, b, gg: h + _rms(_sig(a) * b, gg)
        h3, vjp = jax.vjp(f, h2v, gp, pev, g)
        err = h3 - tgt
        dh2, dgp, dpe, dg = vjp(err * (1.0 / D))
        lossc = (0.5 / D) * jnp.sum(err * err, axis=0, keepdims=True)
        return dh2, dgp, dpe, dg, lossc

    dh2a, dpgp_b, dpe_b, g_ple, lossc = rowwise(
        "ple_loss", f_ple, [full(h2), full(pgp), full(pe), full(target)], [full(P["norm_ple_post"])],
        [(D, D, F32), (D, D, MXU), (D, D, MXU)], [(D, D), (D, D)], tr=128)
    dh2 = mm("d_ple_gate_x", dpgp_b, W["pg"], "nt", add=dh2a)
    gW = {}
    gW["w_ple_gate"] = mm("d_ple_gate_w", h2b, dpgp_b, "tn")
    gW["w_ple"] = mm("d_ple_w", pb, dpe_b, "tn", out_slabs=True)

    def b_ffn_post(d, f, g):
        _, vjp = jax.vjp(_rms, f, g)
        return vjp(d)

    dff_b, g_ffn_post = rowwise("d_ffn_post", b_ffn_post, [full(dh2), full(ff)], [full(P["norm_ffn_post"])],
                                [(D, D, MXU)], [(D, D)])
    da_b = mm("d_ff2_x", dff_b, W["ff2"], "nt", out_dtype=MXU,
              epi=lambda acc, act: acc * (2.0 * jnp.maximum(act, 0.0)), epi_args=[a_ff])
    gW["w_ff2"] = mm("d_ff2_w", rb, dff_b, "tn")
    dn2 = mm("d_ff1_x", da_b, W["ff1"], "nt", b_slabs=True)
    gW["w_ff1"] = mm("d_ff1_w", n2b, da_b, "tn", out_slabs=True)

    def b_mid(d2, dn, h, m, gpost, gffn):
        _, vjp = jax.vjp(_rms, h, gffn)
        dh, dgffn = vjp(dn)
        dh1 = d2 + dh
        _, vjp2 = jax.vjp(_rms, m, gpost)
        dm, dgpost = vjp2(dh1)
        return dh1, dm, dgpost, dgffn

    dh1, dmo_b, g_mix_post, g_ffn_pre = rowwise(
        "d_mix_post", b_mid, [full(dh2), full(dn2), full(h1), full(mo)],
        [full(P["norm_mix_post"]), full(P["norm_ffn_pre"])], [(D, D, F32), (D, D, MXU)], [(D, D), (D, D)], tr=128)
    dmerged = mm("d_out_x", dmo_b, W["out"], "nt")
    gW["w_out"] = mm("d_out_w", merged_b, dmo_b, "tn")

    def b_merge(d, a1, a2, g1, g2, b1, b2):
        s1, s2 = _sig(g1 + b1), _sig(g2 + b2)
        dg1 = d * a1 * s1 * (1.0 - s1)
        dg2 = d * a2 * s2 * (1.0 - s2)
        dg = jnp.concatenate([dg1, dg2], axis=1)
        return d * s1, d * s2, dg, jnp.sum(dg, axis=0, keepdims=True)

    du1_b, du2_b, dgp_b, g_b_gate = rowwise(
        "d_merge", b_merge, [full(dmerged), full(u1), full(u2)] + gate_rows, gate_bias,
        [(D, D, MXU), (D, D, MXU), (2 * D, 2 * D, MXU)], [(2 * D, 2 * D)], tr=128)
    dy_ssd = mm("d_ssd_branch_x", du1_b, W["ssd"], "nt")
    dy_sb = mm("d_sb_branch_x", du2_b, W["sb"], "nt")
    gW["w_ssd_branch"] = mm("d_ssd_branch_w", y_ssd_b, du1_b, "tn")
    gW["w_sb_branch"] = mm("d_sb_branch_w", y_sb_b, du2_b, "tn")

    def b_gate(d, ysc, xs, z, dsk, nw):
        _, vjp = jax.vjp(f_gate, ysc, xs, z, dsk, nw)
        return vjp(d)

    slabs = {n: gW[n] if n in COL_SHARDED else gW[n].reshape((N_DEV,) + ROW_SHARDED[n]) for n in LATE}
    dy_scan, dxs_skip, dz_b, g_dsk_exp, g_ssd_norm, *from_sibling = rowwise(
        "d_ssd_gate", b_gate, [(dy_ssd, 0, GW), (y_scan, 0, GW), (xbc_act, 0, GW), (projmain, OFF_Z, GW)],
        [(P["dsk_exp"], 0, GW), (P["ssd_norm"], 0, GW)],
        [(D_INNER, GW, F32), (D_INNER, GW, F32), (D_INNER, GW, MXU)], [(D_INNER, GW), (D_INNER, GW)],
        tr=256, groups=SSD_GROUPS, side=exchange_plan([("pair", slabs[n]) for n in LATE]))
    chip_sums = [pair_sum("pair_sum_" + n, slabs[n], s, core) for n, s in zip(LATE, from_sibling)]
    dxs, d_b, d_c, ddt_part, g_a_exp, *late_parts = ssd_bwd(
        dy_scan, xbc_act, dte, cse, states, P["a_exp"], side=exchange_plan([("chips", s) for s in chip_sums]))

    def b_dt(dpart, dtr, bias):
        ddt = _dot3_l(dpart, _head_reduce())
        d = ddt * _sig(dtr + bias)
        return d, jnp.sum(d, axis=0, keepdims=True)

    ddt_b, g_dt_bias_pad = rowwise("d_dt", b_dt, [(ddt_part, 0, D_INNER), (dtraw, 0, LANES)],
                                   [(P["dt_bias_pad"], 0, LANES)], [(LANES, LANES, MXU)], [(LANES, LANES)])
    dxbc_b, g_conv_w, g_conv_b = conv_bwd(dxs, dxs_skip, d_b, d_c, projmain, P["conv_w"], P["conv_b"])
    dq_b, dk, dv = sb_bwd(projmain, dy_sb, t_exp, sb_kept)
    dmain_b = jnp.concatenate([dz_b, dxbc_b, dq_b, dk.astype(MXU), dv.astype(MXU)], axis=1)
    g_main = mm("d_in_w", n1b, dmain_b, "tn")
    g_dt = mm("d_dt_w", n1b, ddt_b, "tn")
    last = {"w_gate": mm("d_gate_w", n1b, dgp_b, "tn", out_slabs=True)}
    last["w_in"], conv_slabs = in_grad_slabs({"main": g_main, "dt": g_dt, "conv_w": g_conv_w})
    dn1_dt = mm("d_dt_x", ddt_b, W["dt"], "nt")
    dn1_gate, *from_sibling = mm("d_gate_x", dgp_b, W["gate"], "nt", add=dn1_dt, b_slabs=True,
                                 side=exchange_plan([("pair", last[n]) for n in FIRST]))
    chip_sums = [pair_sum("pair_sum_" + n, last[n], s, core) for n, s in zip(FIRST, from_sibling)]
    dn1, *first_parts = mm("d_in_x", dmain_b, W["main"], "nt", add=dn1_gate,
                           side=exchange_plan([("chips", s) for s in chip_sums]))

    def b_pre(d1, dn, xv, g):
        _, vjp = jax.vjp(_rms, xv, g)
        dx, dg = vjp(dn)
        return d1 + dx, dg

    grad_x, g_mix_pre = rowwise("d_norm_pre", b_pre, [full(dh1), full(dn1), full(x)], [full(P["norm_mix_pre"])],
                                [(D, D, F32)], [(D, D)])

    parts = {**dict(zip(LATE, late_parts)), **dict(zip(FIRST, first_parts))}
    gS = {
        "norm_mix_pre": g_mix_pre, "conv_b": g_conv_b, "dt_bias_pad": g_dt_bias_pad, "a_exp": g_a_exp,
        "dsk_exp": g_dsk_exp, "ssd_norm": g_ssd_norm, "b_gate": g_b_gate,
        "norm_mix_post": g_mix_post, "norm_ffn_pre": g_ffn_pre, "norm_ffn_post": g_ffn_post,
        "norm_ple_post": g_ple,
    }
    return lossc, grad_x, parts, conv_slabs, gS


IN_SPLITS = (2048, 6144, 6176, 7200, 8224)
IN_SHARD = 1156
IN_PAD = 1280
COL_SHARDED = {"w_in": (1024, IN_SHARD), "conv_w": (4, 512), "w_gate": (1024, 256), "w_ff1": (1024, 512),
               "w_ple": (256, 128)}
ROW_SHARDED = {"w_ssd_branch": (256, 1024), "w_sb_branch": (128, 1024), "w_out": (128, 1024),
               "w_ff2": (512, 1024), "w_ple_gate": (128, 1024)}
SHARDED = tuple(COL_SHARDED) + tuple(ROW_SHARDED)
SMALL = (
    ("norm_mix_pre", 1024), ("conv_b", 4096), ("dt_bias", 32), ("a_log", 32), ("d_skip", 32), ("ssd_norm", 2048),
    ("b_gate", 2048), ("norm_mix_post", 1024), ("norm_ffn_pre", 1024), ("norm_ffn_post", 1024),
    ("norm_ple_post", 1024),
)
ROW = 1024
SMALL_ROWS = 16


def _rows_of(n):
    return -(-n // ROW)


def _pad_to(v, n, axis=-1):
    pad = [(0, 0)] * v.ndim
    pad[axis] = (0, n - v.shape[axis])
    return jnp.pad(v, pad)


FIRST = ("w_in", "w_gate")
LATE = ("w_ssd_branch", "w_sb_branch", "w_out", "w_ff1", "w_ff2", "w_ple", "w_ple_gate")
LATE_SPLIT = 3


def in_matrices(w_in_slabs):
    w_in = jnp.concatenate([w_in_slabs[k, :, :IN_SHARD] for k in range(N_DEV)], axis=1)
    return {
        "main": jnp.concatenate([w_in[:, :IN_SPLITS[1]], w_in[:, IN_SPLITS[2]:]], axis=1),
        "dt": _pad_to(w_in[:, IN_SPLITS[1]:IN_SPLITS[2]], LANES),
    }


def late_matrices(g):
    short = {"w_ff1": "ff1", "w_ple": "ple", "w_ssd_branch": "ssd", "w_sb_branch": "sb", "w_out": "out",
             "w_ff2": "ff2", "w_ple_gate": "pg"}
    stack = lambda a: a.reshape(a.shape[0] * a.shape[1], a.shape[2])
    return {short[n]: a if n in COL_SHARDED else stack(a) for n, a in g.items()}


def build_small(small, conv_w_slabs):
    P = {k: small[k] for k in ("norm_mix_pre", "conv_b", "ssd_norm", "b_gate", "norm_mix_post", "norm_ffn_pre",
                               "norm_ffn_post", "norm_ple_post")}
    P["conv_w"] = jnp.concatenate([conv_w_slabs[k] for k in range(N_DEV)], axis=1)
    P["dt_bias_pad"] = _pad_to(small["dt_bias"], LANES)
    P["a_exp"] = jnp.repeat(-jnp.exp(small["a_log"]), HEAD_DIM, axis=1)
    P["dsk_exp"] = jnp.repeat(small["d_skip"], HEAD_DIM, axis=1)
    return P


def small_grads(gS, small):
    heads = lambda a: a.reshape(SSD_HEADS, HEAD_DIM).sum(axis=1)[None, :]
    out = {k: gS[k] for k in ("norm_mix_pre", "conv_b", "ssd_norm", "b_gate", "norm_mix_post", "norm_ffn_pre",
                              "norm_ffn_post", "norm_ple_post")}
    out["dt_bias"] = gS["dt_bias_pad"][:, :SSD_HEADS]
    out["a_log"] = heads(gS["a_exp"]) * (-jnp.exp(small["a_log"]))
    out["d_skip"] = heads(gS["dsk_exp"])
    return out


def in_grad_slabs(g):
    pieces = ((0, IN_SPLITS[1], g["main"], 0), (IN_SPLITS[1], IN_SPLITS[2], g["dt"], -IN_SPLITS[1]),
              (IN_SPLITS[2], N_DEV * IN_SHARD, g["main"], IN_SPLITS[1] - IN_SPLITS[2]))
    slabs = []
    for k in range(N_DEV):
        a, b = IN_SHARD * k, IN_SHARD * (k + 1)
        cut = [src[:, max(a, lo) + off:min(b, hi) + off] for lo, hi, src, off in pieces if max(a, lo) < min(b, hi)]
        slabs.append(_pad_to(jnp.concatenate(cut, axis=1), IN_PAD))
    width = COL_SHARDED["conv_w"][1]
    return jnp.stack(slabs), jnp.stack([g["conv_w"][:, width * k:width * (k + 1)] for k in range(N_DEV)])


def pack_small(get):
    cols = [_pad_to(get(name).reshape(n), _rows_of(n) * ROW) for name, n in SMALL]
    return jnp.concatenate(cols).reshape(SMALL_ROWS, ROW)


def unpack_small(flat):
    out, r0 = {}, 0
    for name, n in SMALL:
        rows = _rows_of(n)
        out[name] = flat[r0:r0 + rows].reshape(rows * ROW)[:n].reshape(1, n)
        r0 += rows
    return out


N_CHIPS = 4
JOB_SEMS = {"all": 7, "scatter": 7, "gather2": 7, "pair": 4, "chips": 3}


def exchange_plan(jobs):
    n = len(jobs)
    kinds = [k for k, _ in jobs]
    srcs = [s for _, s in jobs]
    shapes = {"all": lambda s: (N_DEV,) + s.shape, "gather2": lambda s: (N_DEV,) + s.shape,
              "scatter": lambda s: s.shape, "pair": lambda s: (N_CHIPS,) + s.shape[1:], "chips": lambda s: s.shape}
    out_shape = [jax.ShapeDtypeStruct(shapes[k](s), s.dtype) for k, s in jobs]
    offs = [sum(JOB_SEMS[k] for k in kinds[:i]) for i in range(n + 1)]

    def run(phases, src, out, send_sems, recv_sems, local_sems):
        x, y, c = lax.axis_index("x"), lax.axis_index("y"), lax.axis_index("c")
        dev = lambda d: 4 * d[0] + 2 * d[1] + d[2]
        chip_no = lambda ch: 2 * ch[0] + ch[1]
        me, sib, my_chip = (x, y, c), (x, y, 1 - c), (x, y)
        others = [(1 - x, y), (x, 1 - y), (1 - x, 1 - y)]
        all_chips = [(0, 0), (0, 1), (1, 0), (1, 1)]
        peers = [(1 - x if k & 4 else x, 1 - y if k & 2 else y, 1 - c if k & 1 else c) for k in range(1, N_DEV)]
        starts, recvs, local = [], [], []
        chained = [[] for _ in others]

        for i, kind in enumerate(kinds):
            s_ref, o_ref = src[i], out[i]

            def rc(k, src_ref, dst_ref, to, i=i):
                s = offs[i] + k
                return pltpu.make_async_remote_copy(src_ref=src_ref, dst_ref=dst_ref, send_sem=send_sems.at[s],
                                                    recv_sem=recv_sems.at[s], device_id=to,
                                                    device_id_type=pl.DeviceIdType.MESH)

            if kind == "all":
                local.append(pltpu.make_async_copy(s_ref, o_ref.at[dev(me)], local_sems.at[i]))
                for k, peer in enumerate(peers):
                    starts.append(rc(k, s_ref, o_ref.at[dev(me)], peer))
                    recvs.append(rc(k, s_ref, o_ref.at[dev(peer)], peer))
            elif kind == "scatter":
                local.append(pltpu.make_async_copy(s_ref.at[dev(me)], o_ref.at[dev(me)], local_sems.at[i]))
                for k, peer in enumerate(peers):
                    starts.append(rc(k, s_ref.at[dev(peer)], o_ref.at[dev(me)], peer))
                    recvs.append(rc(k, s_ref.at[dev(me)], o_ref.at[dev(peer)], peer))
            elif kind == "gather2":
                local.append(pltpu.make_async_copy(s_ref, o_ref.at[dev(me)], local_sems.at[i]))
                starts.append(rc(0, s_ref, o_ref.at[dev(me)], sib))
                recvs.append(rc(0, s_ref, o_ref.at[dev(sib)], sib))
                for j, ch in enumerate(others):
                    same, other = (*ch, c), (*ch, 1 - c)
                    starts.append(rc(1 + j, s_ref, o_ref.at[dev(me)], same))
                    chained[j].append((rc(1 + j, s_ref, o_ref.at[dev(same)], same),
                                       rc(4 + j, o_ref.at[dev(same)], o_ref.at[dev(same)], sib)))
                    recvs.append(rc(4 + j, s_ref, o_ref.at[dev(other)], sib))
            elif kind == "pair":
                for j, ch in enumerate(all_chips):
                    starts.append(rc(j, s_ref.at[dev((*ch, 1 - c))], o_ref.at[j], sib))
                    recvs.append(rc(j, s_ref.at[dev((*ch, c))], o_ref.at[j], sib))
            else:
                mine = chip_no(my_chip)
                local.append(pltpu.make_async_copy(s_ref.at[mine], o_ref.at[mine], local_sems.at[i]))
                for j, ch in enumerate(others):
                    starts.append(rc(j, s_ref.at[chip_no(ch)], o_ref.at[mine], (*ch, c)))
                    recvs.append(rc(j, s_ref.at[mine], o_ref.at[chip_no(ch)], (*ch, c)))

        if "start" in phases:
            for cp in local + starts:
                cp.start()
        if "pass" in phases:
            for group in chained:
                for arrival, forward in group:
                    arrival.wait_recv()
                    forward.start()
        if "finish" in phases:
            for cp in recvs:
                cp.wait_recv()
            for cp in starts + [forward for group in chained for _, forward in group]:
                cp.wait_send()
            for cp in local:
                cp.wait()

    scratch = [pltpu.SemaphoreType.DMA((offs[n],)), pltpu.SemaphoreType.DMA((offs[n],)),
               pltpu.SemaphoreType.DMA((n,))]
    return srcs, out_shape, scratch, run


EXCHANGE_PHASES = ("start", "pass", "finish")


def exchange(name, jobs):
    srcs, out_shape, scratch, run = exchange_plan(jobs)
    n = len(srcs)

    def body(*refs):
        run(EXCHANGE_PHASES, refs[:n], refs[n:2 * n], *refs[2 * n:])

    any_spec = pl.BlockSpec(memory_space=pl.ANY)
    return pl.pallas_call(body, name=name, out_shape=out_shape, in_specs=[any_spec] * n, out_specs=[any_spec] * n,
                          scratch_shapes=scratch)(*srcs)


def host_exchange(body, n_in, n_out, grid, plan):
    if plan is None:
        return body, [], [], [], [], []
    srcs, out_shape, scratch, run = plan
    n = len(srcs)
    steps = 1
    for g in grid:
        steps *= g

    def hosted(*refs):
        ins, side_in = refs[:n_in], refs[n_in:n_in + n]
        outs, side_out = refs[n_in + n:n_in + n + n_out], refs[n_in + n + n_out:n_in + 2 * n + n_out]
        rest = refs[n_in + 2 * n + n_out:]
        own, sems = rest[:len(rest) - 3], rest[len(rest) - 3:]
        step = 0
        for d, g in enumerate(grid):
            step = step * g + pl.program_id(d)

        def at(when, phase):
            @pl.when(step == when)
            def _():
                run((phase,), side_in, side_out, *sems)

        at(0, "start")
        body(*ins, *outs, *own)
        at(steps * 13 // 16, "pass")
        at(steps - 1, "finish")

    any_spec = pl.BlockSpec(memory_space=pl.ANY)
    return hosted, [any_spec] * n, list(srcs), list(out_shape), [any_spec] * n, list(scratch)


def pair_sum(name, g, sib, core):
    _, r, c = g.shape
    tr = _pick(r, (256, 128, 64, 32, 16, 8))

    def body(core_ref, g_ref, s_ref, o_ref):
        o_ref[...] = (g_ref[...] + s_ref[...]).astype(o_ref.dtype)

    return pl.pallas_call(
        body,
        name=name,
        out_shape=jax.ShapeDtypeStruct((N_CHIPS, r, c), MXU),
        grid_spec=pltpu.PrefetchScalarGridSpec(
            num_scalar_prefetch=1,
            grid=(N_CHIPS, r // tr),
            in_specs=[
                pl.BlockSpec((None, None, tr, c), lambda j, i, core_ref: (j, core_ref[0], i, 0)),
                pl.BlockSpec((None, tr, c), lambda j, i, core_ref: (j, i, 0)),
            ],
            out_specs=pl.BlockSpec((None, tr, c), lambda j, i, core_ref: (j, i, 0)),
        ),
        compiler_params=_cparams(("parallel", "parallel")),
    )(core, g.reshape(N_CHIPS, 2, r, c), sib)


def adamw(name, parts, w, m, v):
    rows, cols = w.shape
    tr = _pick(rows, (128, 64, 32, 16, 8))

    nparts = parts.shape[0]

    def body(p_ref, w_ref, m_ref, v_ref, g_ref, d_ref, m2_ref, v2_ref):
        g = p_ref[0].astype(F32)
        for k in range(1, nparts):
            g = g + p_ref[k].astype(F32)
        m2 = ADAM_B1 * m_ref[...] + (1.0 - ADAM_B1) * g
        v2 = ADAM_B2 * v_ref[...] + (1.0 - ADAM_B2) * jnp.square(g)
        m_hat = m2 / (1.0 - ADAM_B1 ** ADAM_STEP)
        v_hat = v2 / (1.0 - ADAM_B2 ** ADAM_STEP)
        g_ref[...] = g
        d_ref[...] = -ADAM_LR * (m_hat / (jnp.sqrt(v_hat) + ADAM_EPS) + ADAM_WD * w_ref[...])
        m2_ref[...] = m2
        v2_ref[...] = v2

    spec = pl.BlockSpec((tr, cols), lambda i: (i, 0))
    return pl.pallas_call(
        body,
        name=name,
        out_shape=[jax.ShapeDtypeStruct((rows, cols), F32)] * 4,
        grid=(rows // tr,),
        in_specs=[pl.BlockSpec((nparts, tr, cols), lambda i: (0, i, 0)), spec, spec, spec],
        out_specs=[spec] * 4,
        compiler_params=_cparams(("parallel",)),
    )(parts, w, m, v)


WEIGHT_ORDER = (
    "norm_mix_pre", "w_in", "conv_w", "conv_b", "dt_bias", "a_log", "d_skip", "ssd_norm", "w_ssd_branch",
    "w_sb_branch", "w_gate", "b_gate", "w_out", "norm_mix_post", "norm_ffn_pre", "w_ff1", "w_ff2", "norm_ffn_post",
    "w_ple", "w_ple_gate", "norm_ple_post",
)


def kernel(x, p, norm_mix_pre, w_in, conv_w, conv_b, dt_bias, a_log, d_skip, ssd_norm, w_ssd_branch, w_sb_branch, w_gate, b_gate, w_out, norm_mix_post, norm_ffn_pre, w_ff1, w_ff2, norm_ffn_post, w_ple, w_ple_gate, norm_ple_post, loss_target, m_norm_mix_pre, m_w_in, m_conv_w, m_conv_b, m_dt_bias, m_a_log, m_d_skip, m_ssd_norm, m_w_ssd_branch, m_w_sb_branch, m_w_gate, m_b_gate, m_w_out, m_norm_mix_post, m_norm_ffn_pre, m_w_ff1, m_w_ff2, m_norm_ffn_post, m_w_ple, m_w_ple_gate, m_norm_ple_post, v_norm_mix_pre, v_w_in, v_conv_w, v_conv_b, v_dt_bias, v_a_log, v_d_skip, v_ssd_norm, v_w_ssd_branch, v_w_sb_branch, v_w_gate, v_b_gate, v_w_out, v_norm_mix_post, v_norm_ffn_pre, v_w_ff1, v_w_ff2, v_norm_ffn_post, v_w_ple, v_w_ple_gate, v_norm_ple_post):
    a = dict(locals())
    seq = x.shape[1]
    x2 = x.reshape(seq, D_MODEL)
    target = loss_target.reshape(seq, D_MODEL)
    pb = p.reshape(seq, PLE_DIM).astype(MXU)

    def shard(prefix, name, dtype):
        v = a[prefix + name][0].astype(dtype)
        return _pad_to(v, IN_PAD) if name == "w_in" else v

    w_in_slabs, conv_w_slabs = exchange(
        "gather_weights", [("gather2", shard("", "w_in", MXU)), ("all", a["conv_w"][0])])
    small = {n: a[n] for n, _ in SMALL}
    core = lax.axis_index("c").astype(jnp.int32).reshape(1)

    lossc, grad_x, parts, conv_slabs, g_acc = local_step(
        x2, pb, target, in_matrices(w_in_slabs), build_small(small, conv_w_slabs),
        {n: shard("", n, MXU) for n in LATE + ("w_gate",)}, core)
    loss = lax.psum(jnp.sum(lossc), ("x", "y", "c"))
    g_small = small_grads(g_acc, small)

    parts["conv_w"], small_parts = exchange(
        "small_grads", [("scatter", conv_slabs), ("all", pack_small(lambda n: g_small[n]))])

    leaves = {}
    for n, part in parts.items():
        res = adamw("adamw_" + n, part, shard("", n, F32), shard("m_", n, F32), shard("v_", n, F32))
        leaves[n] = [r[None, :, :IN_SHARD] if n == "w_in" else r[None] for r in res]
    res = adamw("adamw_small", small_parts, pack_small(lambda n: a[n]), pack_small(lambda n: a["m_" + n]),
                pack_small(lambda n: a["v_" + n]))
    for j, r in enumerate(res):
        for n, leaf in unpack_small(r).items():
            leaves.setdefault(n, [None] * 4)[j] = leaf
    outs = [loss, grad_x.reshape(x.shape)]
    for j in range(4):
        outs += [leaves[n][j] for n in WEIGHT_ORDER]
    return tuple(outs)
```
